```python
import jax, jax.numpy as jnp
from jax import lax
import numpy as np

D_MODEL = 1024
BATCH = 8
SEQ = 16384
DEPTH = 1

D_MIX = D_MODEL
D_POOL = D_MIX // 2
POOL_WINDOWS = (2, 4, 8, 16)
N_POOL_GROUPS = len(POOL_WINDOWS)
POOL_GROUP_DIM = D_POOL // N_POOL_GROUPS
D_ATTN = D_MIX - D_POOL
HEAD_DIM = 64
N_HEADS = D_ATTN // HEAD_DIM
Q_BLOCK = 128
D_FF = ((8 * D_MODEL // 3 + 127) // 128) * 128
CONV_WIDTH = 3
EPS = 1e-6
D_IN_PROJ = D_POOL + 3 * D_ATTN

kernel_name = "hymba_pool_stickbreaking_convffn"


def rms_normalize(x):
    xf = x.astype(jnp.float32)
    return xf * lax.rsqrt(jnp.mean(xf * xf, axis=-1, keepdims=True) + EPS)


def rmsnorm(x, gain):
    return (rms_normalize(x) * gain.astype(jnp.float32)).astype(x.dtype)


def multiscale_pool(u, w_pool):
    B, S, _ = u.shape
    uf = u.astype(jnp.float32)
    cs = jnp.pad(jnp.cumsum(uf, axis=1), ((0, 0), (1, 0), (0, 0)))
    t1 = jnp.arange(1, S + 1)
    groups = []
    for g, w in enumerate(POOL_WINDOWS):
        sl = slice(g * POOL_GROUP_DIM, (g + 1) * POOL_GROUP_DIM)
        c = cs[:, :, sl]
        lagged = jnp.pad(c[:, :S + 1 - w], ((0, 0), (w - 1, 0), (0, 0)))
        cnt = jnp.minimum(t1, w).astype(jnp.float32)[None, :, None]
        groups.append((c[:, 1:] - lagged) / cnt - uf[:, :, sl])
    p = jnp.stack(groups, axis=2)
    y = jnp.einsum('bsgc,gcd->bsgd', p, w_pool.astype(jnp.float32))
    return y.reshape(B, S, D_POOL)


def stick_breaking_attention(q, k, v):
    B, H, S, Dh = q.shape
    nb = S // Q_BLOCK
    qb = q.astype(jnp.float32).reshape(B, H, nb, Q_BLOCK, Dh).transpose(2, 0, 1, 3, 4)
    kf = k.astype(jnp.float32)
    vf = v.astype(jnp.float32)
    key_pos = jnp.arange(S)
    scale = Dh ** -0.5

    def block(args):
        i, qi = args
        z = jnp.einsum('bhqd,bhkd->bhqk', qi, kf) * scale
        q_pos = i * Q_BLOCK + jnp.arange(Q_BLOCK)
        causal = key_pos[None, :] < q_pos[:, None]
        log_1m_beta = jnp.where(causal, jax.nn.log_sigmoid(-z), 0.0)
        stick = lax.cumsum(log_1m_beta, axis=3, reverse=True) - log_1m_beta
        a = jnp.where(causal, jnp.exp(jax.nn.log_sigmoid(z) + stick), 0.0)
        return jnp.einsum('bhqk,bhkd->bhqd', a, vf)

    out = lax.map(block, (jnp.arange(nb), qb))
    return out.transpose(1, 2, 0, 3, 4).reshape(B, H, S, Dh)


def causal_depthwise_conv(h, conv_w, conv_b):
    S = h.shape[1]
    hp = jnp.pad(h, ((0, 0), (CONV_WIDTH - 1, 0), (0, 0)))
    y = conv_b
    for kk in range(CONV_WIDTH):
        y = y + conv_w[kk] * hp[:, kk:kk + S]
    return y


def _fwd_setup_inputs(seed: int = 0) -> dict:
    key = jax.random.key(seed)
    ks = jax.random.split(key, 20)
    f32 = jnp.float32

    def nrm(k, shape, scale):
        return jax.random.normal(k, shape, f32) * scale

    def gain(k, n):
        return 1.0 + 0.05 * jax.random.normal(k, (DEPTH, n), f32)

    return {
        "x": jax.random.normal(ks[0], (BATCH, SEQ, D_MODEL), f32),
        "norm_mix_pre": gain(ks[1], D_MODEL),
        "w_in": nrm(ks[2], (DEPTH, D_MODEL, D_IN_PROJ), D_MODEL ** -0.5),
        "w_pool": nrm(ks[3], (DEPTH, N_POOL_GROUPS, POOL_GROUP_DIM, POOL_GROUP_DIM), POOL_GROUP_DIM ** -0.5),
        "pool_scale": gain(ks[4], D_POOL),
        "attn_scale": gain(ks[5], D_ATTN),
        "w_out": nrm(ks[6], (DEPTH, D_MIX, D_MODEL), D_MIX ** -0.5),
        "norm_mix_post": gain(ks[7], D_MODEL),
        "norm_ffn_pre": gain(ks[8], D_MODEL),
        "w_up": nrm(ks[9], (DEPTH, D_MODEL, 2 * D_FF), D_MODEL ** -0.5),
        "conv_w": nrm(ks[10], (DEPTH, CONV_WIDTH, 2 * D_FF), CONV_WIDTH ** -0.5),
        "conv_b": nrm(ks[11], (DEPTH, 2 * D_FF), 0.01),
        "w_down": nrm(ks[12], (DEPTH, D_FF, D_MODEL), D_FF ** -0.5),
        "norm_ffn_post": gain(ks[13], D_MODEL),
    }


def _fwd_reference(x, norm_mix_pre, w_in, w_pool, pool_scale, attn_scale, w_out, norm_mix_post,
              norm_ffn_pre, w_up, conv_w, conv_b, w_down, norm_ffn_post):
    B, S, _ = x.shape
    for l in range(DEPTH):
        h = rmsnorm(x, norm_mix_pre[l])
        proj = h @ w_in[l]
        u_pool = proj[..., :D_POOL]
        q = proj[..., D_POOL:D_POOL + D_ATTN]
        k = proj[..., D_POOL + D_ATTN:D_POOL + 2 * D_ATTN]
        v = proj[..., D_POOL + 2 * D_ATTN:]

        pool_out = multiscale_pool(u_pool, w_pool[l])

        def heads(t):
            return t.reshape(B, S, N_HEADS, HEAD_DIM).transpose(0, 2, 1, 3)
        attn_out = stick_breaking_attention(heads(q), heads(k), heads(v))
        attn_out = attn_out.transpose(0, 2, 1, 3).reshape(B, S, D_ATTN)

        merged = jnp.concatenate([
            rms_normalize(pool_out) * pool_scale[l].astype(jnp.float32),
            rms_normalize(attn_out) * attn_scale[l].astype(jnp.float32),
        ], axis=-1).astype(x.dtype)
        mix = merged @ w_out[l]
        x = x + rmsnorm(mix, norm_mix_post[l])

        h = rmsnorm(x, norm_ffn_pre[l])
        up = causal_depthwise_conv(h @ w_up[l], conv_w[l], conv_b[l])
        gate, val = up[..., :D_FF], up[..., D_FF:]
        f = (jax.nn.silu(gate) * val) @ w_down[l]
        x = x + rmsnorm(f, norm_ffn_post[l])
    return x


import jax as _jax
import jax.numpy as _jnp

TWIN_FORMAT = 'train_step'
FWD_PARAMS = ['x', 'norm_mix_pre', 'w_in', 'w_pool', 'pool_scale', 'attn_scale', 'w_out', 'norm_mix_post', 'norm_ffn_pre', 'w_up', 'conv_w', 'conv_b', 'w_down', 'norm_ffn_post']
TWIN_WEIGHTS = ['norm_mix_pre', 'w_in', 'w_pool', 'pool_scale', 'attn_scale', 'w_out', 'norm_mix_post', 'norm_ffn_pre', 'w_up', 'conv_w', 'conv_b', 'w_down', 'norm_ffn_post']
TWIN_DIFF_INPUT = 'x'
TWIN_INPUTS = ['x', 'norm_mix_pre', 'w_in', 'w_pool', 'pool_scale', 'attn_scale', 'w_out', 'norm_mix_post', 'norm_ffn_pre', 'w_up', 'conv_w', 'conv_b', 'w_down', 'norm_ffn_post', 'loss_target', 'm_norm_mix_pre', 'm_w_in', 'm_w_pool', 'm_pool_scale', 'm_attn_scale', 'm_w_out', 'm_norm_mix_post', 'm_norm_ffn_pre', 'm_w_up', 'm_conv_w', 'm_conv_b', 'm_w_down', 'm_norm_ffn_post', 'v_norm_mix_pre', 'v_w_in', 'v_w_pool', 'v_pool_scale', 'v_attn_scale', 'v_w_out', 'v_norm_mix_post', 'v_norm_ffn_pre', 'v_w_up', 'v_conv_w', 'v_conv_b', 'v_w_down', 'v_norm_ffn_post']
TWIN_OUTPUTS = ['loss', 'grad_x', 'grad_norm_mix_pre', 'grad_w_in', 'grad_w_pool', 'grad_pool_scale', 'grad_attn_scale', 'grad_w_out', 'grad_norm_mix_post', 'grad_norm_ffn_pre', 'grad_w_up', 'grad_conv_w', 'grad_conv_b', 'grad_w_down', 'grad_norm_ffn_post', 'delta_norm_mix_pre', 'delta_w_in', 'delta_w_pool', 'delta_pool_scale', 'delta_attn_scale', 'delta_w_out', 'delta_norm_mix_post', 'delta_norm_ffn_pre', 'delta_w_up', 'delta_conv_w', 'delta_conv_b', 'delta_w_down', 'delta_norm_ffn_post', 'new_m_norm_mix_pre', 'new_m_w_in', 'new_m_w_pool', 'new_m_pool_scale', 'new_m_attn_scale', 'new_m_w_out', 'new_m_norm_mix_post', 'new_m_norm_ffn_pre', 'new_m_w_up', 'new_m_conv_w', 'new_m_conv_b', 'new_m_w_down', 'new_m_norm_ffn_post', 'new_v_norm_mix_pre', 'new_v_w_in', 'new_v_w_pool', 'new_v_pool_scale', 'new_v_attn_scale', 'new_v_w_out', 'new_v_norm_mix_post', 'new_v_norm_ffn_pre', 'new_v_w_up', 'new_v_conv_w', 'new_v_conv_b', 'new_v_w_down', 'new_v_norm_ffn_post']
TWIN_LEAF_KINDS = {'loss': 'loss', 'grad_x': 'grad_x', 'grad_norm_mix_pre': 'grad_w', 'grad_w_in': 'grad_w', 'grad_w_pool': 'grad_w', 'grad_pool_scale': 'grad_w', 'grad_attn_scale': 'grad_w', 'grad_w_out': 'grad_w', 'grad_norm_mix_post': 'grad_w', 'grad_norm_ffn_pre': 'grad_w', 'grad_w_up': 'grad_w', 'grad_conv_w': 'grad_w', 'grad_conv_b': 'grad_w', 'grad_w_down': 'grad_w', 'grad_norm_ffn_post': 'grad_w', 'delta_norm_mix_pre': 'delta_w', 'delta_w_in': 'delta_w', 'delta_w_pool': 'delta_w', 'delta_pool_scale': 'delta_w', 'delta_attn_scale': 'delta_w', 'delta_w_out': 'delta_w', 'delta_norm_mix_post': 'delta_w', 'delta_norm_ffn_pre': 'delta_w', 'delta_w_up': 'delta_w', 'delta_conv_w': 'delta_w', 'delta_conv_b': 'delta_w', 'delta_w_down': 'delta_w', 'delta_norm_ffn_post': 'delta_w', 'new_m_norm_mix_pre': 'new_m', 'new_m_w_in': 'new_m', 'new_m_w_pool': 'new_m', 'new_m_pool_scale': 'new_m', 'new_m_attn_scale': 'new_m', 'new_m_w_out': 'new_m', 'new_m_norm_mix_post': 'new_m', 'new_m_norm_ffn_pre': 'new_m', 'new_m_w_up': 'new_m', 'new_m_conv_w': 'new_m', 'new_m_conv_b': 'new_m', 'new_m_w_down': 'new_m', 'new_m_norm_ffn_post': 'new_m', 'new_v_norm_mix_pre': 'new_v', 'new_v_w_in': 'new_v', 'new_v_w_pool': 'new_v', 'new_v_pool_scale': 'new_v', 'new_v_attn_scale': 'new_v', 'new_v_w_out': 'new_v', 'new_v_norm_mix_post': 'new_v', 'new_v_norm_ffn_pre': 'new_v', 'new_v_w_up': 'new_v', 'new_v_conv_w': 'new_v', 'new_v_conv_b': 'new_v', 'new_v_w_down': 'new_v', 'new_v_norm_ffn_post': 'new_v'}


def _forward(args):
    return _fwd_reference(*[args[k] for k in FWD_PARAMS])


def _output_shape():
    def fwd():
        inp = _fwd_setup_inputs(0)
        return _fwd_reference(*[inp[k] for k in FWD_PARAMS])
    out = _jax.eval_shape(fwd)
    return out.shape, out.dtype

N_MICROBATCH = 1
ADAM_LR = 0.001
ADAM_B1 = 0.9
ADAM_B2 = 0.999
ADAM_EPS = 1e-08
ADAM_WD = 0.01
ADAM_STEP = 10
PER_EXAMPLE_BATCH_AXIS = {'x': 0, 'loss_target': 0}
SHARED_INPUTS = []
_WEIGHT_DTYPES = {'norm_mix_pre': _jnp.float32, 'w_in': _jnp.float32, 'w_pool': _jnp.float32, 'pool_scale': _jnp.float32, 'attn_scale': _jnp.float32, 'w_out': _jnp.float32, 'norm_mix_post': _jnp.float32, 'norm_ffn_pre': _jnp.float32, 'w_up': _jnp.float32, 'conv_w': _jnp.float32, 'conv_b': _jnp.float32, 'w_down': _jnp.float32, 'norm_ffn_post': _jnp.float32}
MOMENT_SCALE = {'norm_mix_pre': 1.708449e+00, 'w_in': 1.125914e+00, 'w_pool': 2.197213e+00, 'pool_scale': 2.627762e+00, 'attn_scale': 1.253713e+00, 'w_out': 1.707637e+00, 'norm_mix_post': 1.277821e+02, 'norm_ffn_pre': 1.183090e+00, 'w_up': 4.934885e-01, 'conv_w': 5.669053e-01, 'conv_b': 1.273178e+00, 'w_down': 1.028302e+00, 'norm_ffn_post': 1.284281e+02}


def _to_microbatches(a, axis):
    t = _jnp.moveaxis(a, axis, 0)
    t = t.reshape((N_MICROBATCH, t.shape[0] // N_MICROBATCH) + t.shape[1:])
    return _jnp.moveaxis(t, 1, axis + 1)


def setup_inputs(seed: int = 0) -> dict:
    inp = _fwd_setup_inputs(seed)
    key = _jax.random.fold_in(_jax.random.key(seed), 7919)
    shape, _ = _output_shape()
    out = dict(inp)
    out["loss_target"] = _jax.random.normal(_jax.random.fold_in(key, 0), shape, _jnp.float32)
    for i, name in enumerate(TWIN_WEIGHTS):
        w = inp[name].astype(_jnp.float32)
        if MOMENT_SCALE is None:
            s = _jnp.sqrt(_jnp.mean(_jnp.square(w)) + 1e-30)
        else:
            s = MOMENT_SCALE[name]
        km, kv = _jax.random.split(_jax.random.fold_in(key, i + 1))
        out[name] = w
        out["m_" + name] = s * _jax.random.normal(km, w.shape, _jnp.float32)
        out["v_" + name] = (s * s) * _jax.random.uniform(kv, w.shape, _jnp.float32, 0.5, 1.5)
    if N_MICROBATCH > 1:
        for name, axis in PER_EXAMPLE_BATCH_AXIS.items():
            out[name] = _to_microbatches(out[name], axis)
    return {'x': out['x'], 'norm_mix_pre': out['norm_mix_pre'], 'w_in': out['w_in'], 'w_pool': out['w_pool'], 'pool_scale': out['pool_scale'], 'attn_scale': out['attn_scale'], 'w_out': out['w_out'], 'norm_mix_post': out['norm_mix_post'], 'norm_ffn_pre': out['norm_ffn_pre'], 'w_up': out['w_up'], 'conv_w': out['conv_w'], 'conv_b': out['conv_b'], 'w_down': out['w_down'], 'norm_ffn_post': out['norm_ffn_post'], 'loss_target': out['loss_target'], 'm_norm_mix_pre': out['m_norm_mix_pre'], 'm_w_in': out['m_w_in'], 'm_w_pool': out['m_w_pool'], 'm_pool_scale': out['m_pool_scale'], 'm_attn_scale': out['m_attn_scale'], 'm_w_out': out['m_w_out'], 'm_norm_mix_post': out['m_norm_mix_post'], 'm_norm_ffn_pre': out['m_norm_ffn_pre'], 'm_w_up': out['m_w_up'], 'm_conv_w': out['m_conv_w'], 'm_conv_b': out['m_conv_b'], 'm_w_down': out['m_w_down'], 'm_norm_ffn_post': out['m_norm_ffn_post'], 'v_norm_mix_pre': out['v_norm_mix_pre'], 'v_w_in': out['v_w_in'], 'v_w_pool': out['v_w_pool'], 'v_pool_scale': out['v_pool_scale'], 'v_attn_scale': out['v_attn_scale'], 'v_w_out': out['v_w_out'], 'v_norm_mix_post': out['v_norm_mix_post'], 'v_norm_ffn_pre': out['v_norm_ffn_pre'], 'v_w_up': out['v_w_up'], 'v_conv_w': out['v_conv_w'], 'v_conv_b': out['v_conv_b'], 'v_w_down': out['v_w_down'], 'v_norm_ffn_post': out['v_norm_ffn_post']}


def _loss(weights, diff, rest, loss_target):
    with _jax.named_scope("forward"):
        args = {**rest, TWIN_DIFF_INPUT: diff, **{k: w.astype(_WEIGHT_DTYPES[k]) for k, w in weights.items()}}
        y = _forward(args)
    with _jax.named_scope("loss_head"):
        err = _jnp.square(y.astype(_jnp.float32) - loss_target)
        return 0.5 * _jnp.sum(_jnp.mean(err, axis=-1)) if err.ndim else 0.5 * err


def _adamw(w, g, m, v):
    m = ADAM_B1 * m + (1.0 - ADAM_B1) * g
    v = ADAM_B2 * v + (1.0 - ADAM_B2) * _jnp.square(g)
    m_hat = m / (1.0 - ADAM_B1 ** ADAM_STEP)
    v_hat = v / (1.0 - ADAM_B2 ** ADAM_STEP)
    delta = -ADAM_LR * (m_hat / (_jnp.sqrt(v_hat) + ADAM_EPS) + ADAM_WD * w)
    return delta, m, v


def reference(x, norm_mix_pre, w_in, w_pool, pool_scale, attn_scale, w_out, norm_mix_post, norm_ffn_pre, w_up, conv_w, conv_b, w_down, norm_ffn_post, loss_target, m_norm_mix_pre, m_w_in, m_w_pool, m_pool_scale, m_attn_scale, m_w_out, m_norm_mix_post, m_norm_ffn_pre, m_w_up, m_conv_w, m_conv_b, m_w_down, m_norm_ffn_post, v_norm_mix_pre, v_w_in, v_w_pool, v_pool_scale, v_attn_scale, v_w_out, v_norm_mix_post, v_norm_ffn_pre, v_w_up, v_conv_w, v_conv_b, v_w_down, v_norm_ffn_post):
    given = dict(x=x, norm_mix_pre=norm_mix_pre, w_in=w_in, w_pool=w_pool, pool_scale=pool_scale, attn_scale=attn_scale, w_out=w_out, norm_mix_post=norm_mix_post, norm_ffn_pre=norm_ffn_pre, w_up=w_up, conv_w=conv_w, conv_b=conv_b, w_down=w_down, norm_ffn_post=norm_ffn_post, loss_target=loss_target, m_norm_mix_pre=m_norm_mix_pre, m_w_in=m_w_in, m_w_pool=m_w_pool, m_pool_scale=m_pool_scale, m_attn_scale=m_attn_scale, m_w_out=m_w_out, m_norm_mix_post=m_norm_mix_post, m_norm_ffn_pre=m_norm_ffn_pre, m_w_up=m_w_up, m_conv_w=m_conv_w, m_conv_b=m_conv_b, m_w_down=m_w_down, m_norm_ffn_post=m_norm_ffn_post, v_norm_mix_pre=v_norm_mix_pre, v_w_in=v_w_in, v_w_pool=v_w_pool, v_pool_scale=v_pool_scale, v_attn_scale=v_attn_scale, v_w_out=v_w_out, v_norm_mix_post=v_norm_mix_post, v_norm_ffn_pre=v_norm_ffn_pre, v_w_up=v_w_up, v_conv_w=v_conv_w, v_conv_b=v_conv_b, v_w_down=v_w_down, v_norm_ffn_post=v_norm_ffn_post)
    weights = {n: given[n] for n in TWIN_WEIGHTS}
    shared = {n: given[n] for n in SHARED_INPUTS}
    per_example = {n: given[n] for n in ['x']}
    grad_fn = _jax.value_and_grad(_loss, argnums=(0, 1))

    def one_microbatch(ex, loss_target):
        ex = dict(ex)
        diff = ex.pop(TWIN_DIFF_INPUT)
        return grad_fn(weights, diff, {**shared, **ex}, loss_target)

    if N_MICROBATCH == 1:
        loss, (grad_w, grad_x) = one_microbatch(per_example, given["loss_target"])
    else:
        def body(carry, xs):
            loss_sum, grad_sum = carry
            l_k, (gw_k, gx_k) = one_microbatch(xs[0], xs[1])
            with _jax.named_scope("update"):
                return (loss_sum + l_k, _jax.tree.map(_jnp.add, grad_sum, gw_k)), gx_k

        init = (_jnp.zeros((), _jnp.float32), _jax.tree.map(_jnp.zeros_like, weights))
        (loss, grad_w), grad_x = _jax.lax.scan(body, init, (per_example, given["loss_target"]))
    with _jax.named_scope("update"):
        delta_w, new_m, new_v = {}, {}, {}
        for n in TWIN_WEIGHTS:
            delta_w[n], new_m[n], new_v[n] = _adamw(weights[n], grad_w[n], given["m_" + n], given["v_" + n])
    return (loss, grad_x, *[grad_w[n] for n in TWIN_WEIGHTS], *[delta_w[n] for n in TWIN_WEIGHTS],
            *[new_m[n] for n in TWIN_WEIGHTS], *[new_v[n] for n in TWIN_WEIGHTS])
```

```python
import functools

import jax
import jax.numpy as jnp
from jax import lax
from jax.experimental import pallas as pl
from jax.experimental.pallas import tpu as pltpu

F32 = jnp.float32
BF16 = jnp.bfloat16
HIGHEST = lax.Precision.HIGHEST

N_DEV = 8
EPS = 1e-6
POOL_WINDOWS = (2, 4, 8, 16)
POOL_GROUP = 128
HALO = 16
HEAD_DIM = 64
QB = 128
ATTN_SCALE = HEAD_DIM ** -0.5
EXP_UNDERFLOW = -100.0
D_FF_SHARDS = 4

ADAM_LR = 0.001
ADAM_B1 = 0.9
ADAM_B2 = 0.999
ADAM_EPS = 1e-08
ADAM_WD = 0.01
ADAM_STEP = 10

VMEM_LIMIT_V7X = 56 * 1024 * 1024
MESH = pl.DeviceIdType.MESH


def _params(*semantics):
    return pltpu.CompilerParams(dimension_semantics=semantics, vmem_limit_bytes=VMEM_LIMIT_V7X)


def _const(shape):
    zeros = (0,) * len(shape)
    return pl.BlockSpec(shape, lambda *_: zeros, pipeline_mode=pl.Buffered(1))


def _dot(a, b):
    return jnp.dot(a, b, preferred_element_type=F32)


def _dot_nt(a, b):
    return lax.dot_general(a, b, (((1,), (1,)), ((), ())), preferred_element_type=F32)


def _dot_tn(a, b):
    return lax.dot_general(a, b, (((0,), (0,)), ((), ())), preferred_element_type=F32)


def _rms(v):
    return lax.rsqrt(jnp.mean(v * v, axis=-1, keepdims=True) + EPS)


def _norm_bwd(dn_times_gain, n, r):
    return r * (dn_times_gain - n * jnp.mean(dn_times_gain * n, axis=-1, keepdims=True))


def _accumulate(ref, value, first):
    @pl.when(first)
    def _():
        ref[...] = value

    @pl.when(jnp.logical_not(first))
    def _():
        ref[...] += value


def _colsum(v):
    return jnp.sum(v, axis=0, keepdims=True)


def _fwd_inproj(x, g1, w_in_g, tile):
    S, D = x.shape
    nb, _, cs = w_in_g.shape
    d_pool = 2 * cs

    def body(x_ref, g_ref, w_ref, h_ref, u_ref, qkv_ref):
        xf = x_ref[...]
        h = (xf * _rms(xf) * g_ref[...]).astype(BF16)
        h_ref[...] = h
        for d in range(nb):
            o = _dot(h, w_ref[d])
            if d < 2:
                u_ref[:, d * cs:(d + 1) * cs] = o
            else:
                qkv_ref[:, (d - 2) * cs:(d - 1) * cs] = o.astype(BF16)

    return pl.pallas_call(
        body, name="fwd_inproj", grid=(S // tile,),
        in_specs=[pl.BlockSpec((tile, D), lambda i: (i, 0)), _const((1, D)), _const(w_in_g.shape)],
        out_specs=[pl.BlockSpec((tile, D), lambda i: (i, 0)), pl.BlockSpec((tile, d_pool), lambda i: (i, 0)),
                   pl.BlockSpec((tile, 3 * d_pool), lambda i: (i, 0))],
        out_shape=[jax.ShapeDtypeStruct((S, D), BF16), jax.ShapeDtypeStruct((S, d_pool), F32),
                   jax.ShapeDtypeStruct((S, 3 * d_pool), BF16)],
        compiler_params=_params("parallel"),
    )(x, g1, w_in_g)


def _window_sums(ext, forward):
    n = ext.shape[0]
    sums, s, sh = {}, ext, 1
    while sh < POOL_WINDOWS[-1]:
        s = s + pltpu.roll(s, (n - sh) if forward else sh, axis=0)
        sh *= 2
        sums[sh] = s
    return sums


def _pool_counts(t0, rows):
    t1 = (lax.broadcasted_iota(jnp.int32, (rows, 1), 0) + t0 + 1).astype(F32)
    return [jnp.minimum(t1, float(w)) for w in POOL_WINDOWS]


def _pool_deviation(u, halo, t0):
    T = u.shape[0]
    sums = _window_sums(jnp.concatenate([halo, u], axis=0), forward=False)
    counts = _pool_counts(t0, T)
    parts = []
    for g, w in enumerate(POOL_WINDOWS):
        lanes = slice(g * POOL_GROUP, (g + 1) * POOL_GROUP)
        parts.append(sums[w][HALO:, lanes] / counts[g] - u[:, lanes])
    return parts


def _prev_halo_spec(tile, width):
    return pl.BlockSpec((HALO, width), lambda i: (jnp.maximum(i * (tile // HALO) - 1, 0), 0))


def _next_halo_spec(tile, width, n_tiles):
    last = n_tiles * (tile // HALO) - 1
    return pl.BlockSpec((HALO, width), lambda i: (jnp.minimum((i + 1) * (tile // HALO), last), 0))


def _fwd_pool(u, w_pool, tile):
    S, C = u.shape

    def body(u_ref, halo_ref, wp_ref, o_ref):
        i = pl.program_id(0)
        halo = jnp.where(i > 0, halo_ref[...], 0.0)
        parts = _pool_deviation(u_ref[...], halo, i * tile)
        for g, p in enumerate(parts):
            o_ref[:, g * POOL_GROUP:(g + 1) * POOL_GROUP] = _dot(p.astype(BF16), wp_ref[g].astype(BF16))

    return pl.pallas_call(
        body, name="fwd_pool", grid=(S // tile,),
        in_specs=[pl.BlockSpec((tile, C), lambda i: (i, 0)), _prev_halo_spec(tile, C), _const(w_pool.shape)],
        out_specs=pl.BlockSpec((tile, C), lambda i: (i, 0)),
        out_shape=jax.ShapeDtypeStruct((S, C), F32),
        compiler_params=_params("parallel"),
    )(u, u, w_pool)


def _attn_masks():
    lane = lax.broadcasted_iota(jnp.int32, (QB, 2 * HEAD_DIM), 1)
    row = lax.broadcasted_iota(jnp.int32, (QB, QB), 0)
    col = lax.broadcasted_iota(jnp.int32, (QB, QB), 1)
    return lane < HEAD_DIM, row, col


def _attn_block(qh, kb, mask, carry, after_s):
    z = _dot_nt(qh, kb) * ATTN_SCALE
    e = jnp.exp(-jnp.abs(z))
    softplus = jnp.maximum(z, 0.0) + jnp.log(1.0 + e)
    log_1m_beta = jnp.where(mask, -softplus, 0.0)
    stick = jnp.dot(log_1m_beta, after_s, precision=HIGHEST, preferred_element_type=F32) + carry
    a = jnp.where(mask, jnp.exp(z - softplus + stick), 0.0)
    return z, e, a, log_1m_beta


def _fwd_attn(qkv, n_pairs):
    S = qkv.shape[0]
    nqb = S // QB

    def body(q_ref, k_ref, v_ref, o_ref):
        i = pl.program_id(1)
        low_lanes, row, col = _attn_masks()
        after_s = (row > col).astype(F32)
        q = q_ref[...].astype(F32)
        qh = (jnp.where(low_lanes, q, 0.0).astype(BF16), jnp.where(low_lanes, 0.0, q).astype(BF16))

        def cond(c):
            return jnp.logical_and(c[0] <= i, c[1] == 0)

        def step(c):
            n, _, acc, c0, c1 = c
            start = pl.multiple_of((i - n) * QB, QB)
            kb = k_ref[pl.ds(start, QB), :]
            vb = v_ref[pl.ds(start, QB), :]
            mask = jnp.logical_or(col < row, n > 0)
            carries, outs = [c0, c1], []
            for h in range(2):
                _, _, a, lmb = _attn_block(qh[h], kb, mask, carries[h], after_s)
                outs.append(_dot(a.astype(BF16), vb))
                carries[h] = carries[h] + jnp.sum(lmb, axis=1, keepdims=True)
            acc = acc + jnp.where(low_lanes, outs[0], outs[1])
            done = (jnp.maximum(jnp.max(carries[0]), jnp.max(carries[1])) < EXP_UNDERFLOW).astype(jnp.int32)
            return n + 1, done, acc, carries[0], carries[1]

        init = (jnp.int32(0), jnp.int32(0), jnp.zeros((QB, QB), F32), jnp.zeros((QB, 1), F32), jnp.zeros((QB, 1), F32))
        o_ref[...] = lax.while_loop(cond, step, init)[2]

    return pl.pallas_call(
        body, name="fwd_attn", grid=(n_pairs, nqb),
        in_specs=[pl.BlockSpec((QB, QB), lambda p, i: (i, p)),
                  pl.BlockSpec((S, QB), lambda p, i: (0, n_pairs + p), pipeline_mode=pl.Buffered(1)),
                  pl.BlockSpec((S, QB), lambda p, i: (0, 2 * n_pairs + p), pipeline_mode=pl.Buffered(1))],
        out_specs=pl.BlockSpec((QB, QB), lambda p, i: (i, p)),
        out_shape=jax.ShapeDtypeStruct((S, n_pairs * QB), F32),
        compiler_params=_params("parallel", "parallel"),
    )(qkv, qkv, qkv)


def _normalized_heads(pool_out, attn_out):
    rp, ra = _rms(pool_out), _rms(attn_out)
    return pool_out * rp, rp, attn_out * ra, ra


def _fwd_outproj(pool_out, attn_out, pool_scale, attn_scale, w_out, x, g2, g3, tile):
    S, D = x.shape
    C = pool_out.shape[1]

    def body(p_ref, a_ref, ps_ref, as_ref, w_ref, x_ref, g2_ref, g3_ref, mix_ref, x2_ref, h2_ref):
        n_p, _, n_a, _ = _normalized_heads(p_ref[...], a_ref[...])
        mix = _dot((n_p * ps_ref[...]).astype(BF16), w_ref[:C, :]) + _dot((n_a * as_ref[...]).astype(BF16), w_ref[C:, :])
        mix_ref[...] = mix
        x2 = x_ref[...] + mix * _rms(mix) * g2_ref[...]
        x2_ref[...] = x2
        h2_ref[...] = (x2 * _rms(x2) * g3_ref[...]).astype(BF16)

    row = lambda w: pl.BlockSpec((tile, w), lambda i: (i, 0))
    return pl.pallas_call(
        body, name="fwd_outproj", grid=(S // tile,),
        in_specs=[row(C), row(C), _const((1, C)), _const((1, C)), _const(w_out.shape), row(D), _const((1, D)), _const((1, D))],
        out_specs=[row(D), row(D), row(D)],
        out_shape=[jax.ShapeDtypeStruct((S, D), F32), jax.ShapeDtypeStruct((S, D), F32), jax.ShapeDtypeStruct((S, D), BF16)],
        compiler_params=_params("parallel"),
    )(pool_out, attn_out, pool_scale, attn_scale, w_out, x, g2, g3)


def _fwd_up(h2, w_up_g, tile):
    S, D = h2.shape
    nb, _, cs = w_up_g.shape

    def body(h_ref, w_ref, o_ref):
        h = h_ref[...]
        for d in range(nb):
            o_ref[d] = _dot(h, w_ref[d]).astype(BF16)

    return pl.pallas_call(
        body, name="fwd_up", grid=(S // tile,),
        in_specs=[pl.BlockSpec((tile, D), lambda i: (i, 0)), _const(w_up_g.shape)],
        out_specs=pl.BlockSpec((nb, tile, cs), lambda i: (0, i, 0)),
        out_shape=jax.ShapeDtypeStruct((nb, S, cs), BF16),
        compiler_params=_params("parallel"),
    )(h2, w_up_g)


def _conv_taps(tile_rows, halo_rows):
    T = tile_rows.shape[0]
    ext = jnp.concatenate([halo_rows.astype(F32), tile_rows.astype(F32)], axis=0)
    return pltpu.roll(ext, 2, axis=0)[HALO:], pltpu.roll(ext, 1, axis=0)[HALO:], ext[HALO:]


def _tap_rows(cw_ref, d):
    return [cw_ref[d, k:k + 1, :] for k in range(3)]


def _gated_unit(taps_gate, taps_val, cw_gate, cw_val, cb_gate, cb_val):
    gate = cw_gate[0] * taps_gate[0] + cw_gate[1] * taps_gate[1] + cw_gate[2] * taps_gate[2] + cb_gate
    val = cw_val[0] * taps_val[0] + cw_val[1] * taps_val[1] + cw_val[2] * taps_val[2] + cb_val
    sig = 1.0 / (1.0 + jnp.exp(-gate))
    return gate, val, sig


def _fwd_ffn_loss(upre, conv_w_g, conv_b_g, w_down4, x2, target, g4, tile):
    nb, S, cs = upre.shape
    D = x2.shape[1]

    def body(u_ref, halo_ref, cw_ref, cb_ref, wd_ref, x2_ref, t_ref, g4_ref, dy_ref, df_ref, loss_ref, dg4_ref):
        i = pl.program_id(0)
        first = i == 0
        f = jnp.zeros((tile, D), F32)
        for s in range(D_FF_SHARDS):
            halo_g = jnp.where(first, jnp.zeros_like(halo_ref[s]), halo_ref[s])
            halo_v = jnp.where(first, jnp.zeros_like(halo_ref[s]), halo_ref[s + D_FF_SHARDS])
            gate, val, sig = _gated_unit(_conv_taps(u_ref[s], halo_g), _conv_taps(u_ref[s + D_FF_SHARDS], halo_v),
                                         _tap_rows(cw_ref, s), _tap_rows(cw_ref, s + D_FF_SHARDS), cb_ref[s], cb_ref[s + D_FF_SHARDS])
            f = f + _dot((gate * sig * val).astype(BF16), wd_ref[s])
        r4 = _rms(f)
        n4 = f * r4
        err = x2_ref[...] + n4 * g4_ref[...] - t_ref[...]
        dy = err * (1.0 / D)
        dy_ref[...] = dy
        df_ref[...] = _norm_bwd(dy * g4_ref[...], n4, r4).astype(BF16)
        _accumulate(loss_ref, _colsum(err * err), first)
        _accumulate(dg4_ref, _colsum(dy * n4), first)

    row = lambda w: pl.BlockSpec((tile, w), lambda i: (i, 0))
    return pl.pallas_call(
        body, name="fwd_ffn_loss", grid=(S // tile,),
        in_specs=[pl.BlockSpec((nb, tile, cs), lambda i: (0, i, 0)),
                  pl.BlockSpec((nb, HALO, cs), lambda i: (0, jnp.maximum(i * (tile // HALO) - 1, 0), 0)),
                  _const(conv_w_g.shape), _const(conv_b_g.shape), _const(w_down4.shape), row(D), row(D), _const((1, D))],
        out_specs=[row(D), row(D), pl.BlockSpec((1, D), lambda i: (0, 0)), pl.BlockSpec((1, D), lambda i: (0, 0))],
        out_shape=[jax.ShapeDtypeStruct((S, D), F32), jax.ShapeDtypeStruct((S, D), BF16),
                   jax.ShapeDtypeStruct((1, D), F32), jax.ShapeDtypeStruct((1, D), F32)],
        compiler_params=_params("arbitrary"),
    )(upre, upre, conv_w_g, conv_b_g, w_down4, x2, target, g4)


def _bwd_down(upre, conv_w_g, conv_b_g, w_down4, df, tile):
    nb, S, cs = upre.shape
    D = df.shape[1]
    n_tiles = S // tile

    def body(ug_ref, uv_ref, hg_ref, hv_ref, cwg_ref, cwv_ref, cbg_ref, cbv_ref, wd_ref, df_ref,
             dg_ref, dv_ref, dwd_ref, dbg_ref, dbv_ref, dcwg_ref, dcwv_ref):
        i = pl.program_id(1)
        first = i == 0
        halo_g = jnp.where(first, jnp.zeros_like(hg_ref[0]), hg_ref[0])
        halo_v = jnp.where(first, jnp.zeros_like(hv_ref[0]), hv_ref[0])
        taps_g, taps_v = _conv_taps(ug_ref[0], halo_g), _conv_taps(uv_ref[0], halo_v)
        gate, val, sig = _gated_unit(taps_g, taps_v, _tap_rows(cwg_ref, 0), _tap_rows(cwv_ref, 0), cbg_ref[0], cbv_ref[0])
        silu = gate * sig
        dfb = df_ref[...]
        dact = _dot_nt(dfb, wd_ref[0])
        _accumulate(dwd_ref.at[0], _dot_tn((silu * val).astype(BF16), dfb), first)
        dgate = dact * val * (sig * (1.0 + gate * (1.0 - sig)))
        dval = dact * silu
        dg_ref[0] = dgate.astype(BF16)
        dv_ref[0] = dval.astype(BF16)
        _accumulate(dbg_ref.at[0], _colsum(dgate), first)
        _accumulate(dbv_ref.at[0], _colsum(dval), first)
        _accumulate(dcwg_ref.at[0], jnp.concatenate([_colsum(dgate * t) for t in taps_g], axis=0), first)
        _accumulate(dcwv_ref.at[0], jnp.concatenate([_colsum(dval * t) for t in taps_v], axis=0), first)

    half = D_FF_SHARDS
    blk = lambda off: pl.BlockSpec((1, tile, cs), lambda s, i: (s + off, i, 0))
    halo = lambda off: pl.BlockSpec((1, HALO, cs), lambda s, i: (s + off, jnp.maximum(i * (tile // HALO) - 1, 0), 0))
    par = lambda off, r: pl.BlockSpec((1, r, cs), lambda s, i: (s + off, 0, 0))
    outs = pl.pallas_call(
        body, name="bwd_down", grid=(half, n_tiles),
        in_specs=[blk(0), blk(half), halo(0), halo(half), par(0, 3), par(half, 3), par(0, 1), par(half, 1),
                  pl.BlockSpec((1, cs, D), lambda s, i: (s, 0, 0)), pl.BlockSpec((tile, D), lambda s, i: (i, 0))],
        out_specs=[blk(0), blk(0), pl.BlockSpec((1, cs, D), lambda s, i: (s, 0, 0)),
                   par(0, 1), par(0, 1), par(0, 3), par(0, 3)],
        out_shape=[jax.ShapeDtypeStruct((half, S, cs), BF16), jax.ShapeDtypeStruct((half, S, cs), BF16),
                   jax.ShapeDtypeStruct((half, cs, D), F32),
                   jax.ShapeDtypeStruct((half, 1, cs), F32), jax.ShapeDtypeStruct((half, 1, cs), F32),
                   jax.ShapeDtypeStruct((half, 3, cs), F32), jax.ShapeDtypeStruct((half, 3, cs), F32)],
        compiler_params=_params("parallel", "arbitrary"),
    )(upre, upre, upre, upre, conv_w_g, conv_w_g, conv_b_g, conv_b_g, w_down4, df)
    dgate, dval, d_wd, dbg, dbv, dcwg, dcwv = outs
    return dgate, dval, d_wd, jnp.concatenate([dbg, dbv], axis=0), jnp.concatenate([dcwg, dcwv], axis=0)


def _bwd_up_x(dgate, dval, conv_w_g, w_up_g, x2, dy, mix, g2, g3, tile):
    half, S, cs = dgate.shape
    nb = 2 * half
    D = x2.shape[1]
    n_tiles = S // tile

    def body(dg_ref, dv_ref, hg_ref, hv_ref, cw_ref, w_ref, x2_ref, dy_ref, mix_ref, g2_ref, g3_ref,
             dupre_ref, dx2_ref, dmix_ref, dg3_ref, dg2_ref):
        i = pl.program_id(0)
        first = i == 0
        last = i == n_tiles - 1
        dh2 = jnp.zeros((tile, D), F32)
        for d in range(nb):
            src, halo = (dg_ref, hg_ref) if d < half else (dv_ref, hv_ref)
            nxt = jnp.where(last, jnp.zeros_like(halo[d % half]), halo[d % half])
            ext = jnp.concatenate([src[d % half].astype(F32), nxt.astype(F32)], axis=0)
            n = ext.shape[0]
            cw = _tap_rows(cw_ref, d)
            dupre = (cw[2] * ext + cw[1] * pltpu.roll(ext, n - 1, axis=0) + cw[0] * pltpu.roll(ext, n - 2, axis=0))[:tile]
            dupre = dupre.astype(BF16)
            dupre_ref[d] = dupre
            dh2 = dh2 + _dot_nt(dupre, w_ref[d])
        x2 = x2_ref[...]
        r3 = _rms(x2)
        n3 = x2 * r3
        _accumulate(dg3_ref, _colsum(dh2 * n3), first)
        dx2 = dy_ref[...] + _norm_bwd(dh2 * g3_ref[...], n3, r3)
        dx2_ref[...] = dx2
        mix = mix_ref[...]
        r2 = _rms(mix)
        n2 = mix * r2
        _accumulate(dg2_ref, _colsum(dx2 * n2), first)
        dmix_ref[...] = _norm_bwd(dx2 * g2_ref[...], n2, r2).astype(BF16)

    row = lambda w: pl.BlockSpec((tile, w), lambda i: (i, 0))
    blk = pl.BlockSpec((half, tile, cs), lambda i: (0, i, 0))
    last_halo = n_tiles * (tile // HALO) - 1
    halo = pl.BlockSpec((half, HALO, cs), lambda i: (0, jnp.minimum((i + 1) * (tile // HALO), last_halo), 0))
    acc = pl.BlockSpec((1, D), lambda i: (0, 0))
    return pl.pallas_call(
        body, name="bwd_up_x", grid=(n_tiles,),
        in_specs=[blk, blk, halo, halo, _const(conv_w_g.shape), _const(w_up_g.shape), row(D), row(D), row(D),
                  _const((1, D)), _const((1, D))],
        out_specs=[pl.BlockSpec((nb, tile, cs), lambda i: (0, i, 0)), row(D), row(D), acc, acc],
        out_shape=[jax.ShapeDtypeStruct((nb, S, cs), BF16), jax.ShapeDtypeStruct((S, D), F32),
                   jax.ShapeDtypeStruct((S, D), BF16), jax.ShapeDtypeStruct((1, D), F32), jax.ShapeDtypeStruct((1, D), F32)],
        compiler_params=_params("arbitrary"),
    )(dgate, dval, dgate, dval, conv_w_g, w_up_g, x2, dy, mix, g2, g3)


def _bwd_weight(act, dout, tile):
    S, D = act.shape
    nb, _, cs = dout.shape

    def body(a_ref, d_ref, o_ref):
        _accumulate(o_ref.at[0], _dot_tn(a_ref[...], d_ref[0]), pl.program_id(1) == 0)

    return pl.pallas_call(
        body, name="bwd_w_up", grid=(nb, S // tile),
        in_specs=[pl.BlockSpec((tile, D), lambda d, i: (i, 0)), pl.BlockSpec((1, tile, cs), lambda d, i: (d, i, 0))],
        out_specs=pl.BlockSpec((1, D, cs), lambda d, i: (d, 0, 0)),
        out_shape=jax.ShapeDtypeStruct((nb, D, cs), F32),
        compiler_params=_params("parallel", "arbitrary"),
    )(act, dout)


def _bwd_outproj(dmix, w_out, pool_out, attn_out, pool_scale, attn_scale, tile):
    S, D = dmix.shape
    C = pool_out.shape[1]

    def body(dm_ref, w_ref, p_ref, a_ref, ps_ref, as_ref, dp_ref, da_ref, dw_ref, dps_ref, das_ref):
        first = pl.program_id(0) == 0
        dmx = dm_ref[...]
        dmerged = _dot_nt(dmx, w_ref[...])
        n_p, r_p, n_a, r_a = _normalized_heads(p_ref[...], a_ref[...])
        merged = jnp.concatenate([(n_p * ps_ref[...]).astype(BF16), (n_a * as_ref[...]).astype(BF16)], axis=1)
        _accumulate(dw_ref, _dot_tn(merged, dmx), first)
        dm_p, dm_a = dmerged[:, :C], dmerged[:, C:]
        _accumulate(dps_ref, _colsum(dm_p * n_p), first)
        _accumulate(das_ref, _colsum(dm_a * n_a), first)
        dp_ref[...] = _norm_bwd(dm_p * ps_ref[...], n_p, r_p)
        da_ref[...] = _norm_bwd(dm_a * as_ref[...], n_a, r_a)

    row = lambda w: pl.BlockSpec((tile, w), lambda i: (i, 0))
    return pl.pallas_call(
        body, name="bwd_outproj", grid=(S // tile,),
        in_specs=[row(D), _const(w_out.shape), row(C), row(C), _const((1, C)), _const((1, C))],
        out_specs=[row(C), row(C), pl.BlockSpec(w_out.shape, lambda i: (0, 0)),
                   pl.BlockSpec((1, C), lambda i: (0, 0)), pl.BlockSpec((1, C), lambda i: (0, 0))],
        out_shape=[jax.ShapeDtypeStruct((S, C), F32), jax.ShapeDtypeStruct((S, C), F32),
                   jax.ShapeDtypeStruct(w_out.shape, F32), jax.ShapeDtypeStruct((1, C), F32), jax.ShapeDtypeStruct((1, C), F32)],
        compiler_params=_params("arbitrary"),
    )(dmix, w_out, pool_out, attn_out, pool_scale, attn_scale)


def _bwd_attn(qkv, attn_out, d_attn, n_pairs):
    S = qkv.shape[0]
    nqb = S // QB

    def body(q_ref, k_ref, v_ref, o_ref, do_ref, dq_ref, dk_ref, dv_ref):
        i = pl.program_id(1)

        @pl.when(i == 0)
        def _():
            dk_ref[...] = jnp.zeros_like(dk_ref)
            dv_ref[...] = jnp.zeros_like(dv_ref)

        low_lanes, row, col = _attn_masks()
        after_s = (row > col).astype(F32)
        from_s = (row >= col).astype(F32)
        q = q_ref[...].astype(F32)
        do = do_ref[...]
        prod = do * o_ref[...]
        qh = (jnp.where(low_lanes, q, 0.0).astype(BF16), jnp.where(low_lanes, 0.0, q).astype(BF16))
        doh = (jnp.where(low_lanes, do, 0.0).astype(BF16), jnp.where(low_lanes, 0.0, do).astype(BF16))
        total = (jnp.sum(jnp.where(low_lanes, prod, 0.0), axis=1, keepdims=True),
                 jnp.sum(jnp.where(low_lanes, 0.0, prod), axis=1, keepdims=True))

        def cond(c):
            return jnp.logical_and(c[0] <= i, c[1] == 0)

        def step(c):
            n, _, dq, c0, c1, s0, s1 = c
            start = pl.multiple_of((i - n) * QB, QB)
            kb = k_ref[pl.ds(start, QB), :]
            vb = v_ref[pl.ds(start, QB), :]
            mask = jnp.logical_or(col < row, n > 0)
            carries, gsums, dqs = [c0, c1], [s0, s1], []
            dk = jnp.zeros((QB, QB), F32)
            dv = jnp.zeros((QB, QB), F32)
            for h in range(2):
                z, e, a, lmb = _attn_block(qh[h], kb, mask, carries[h], after_s)
                g = a * _dot_nt(doh[h], vb)
                nearer = jnp.dot(g, from_s, precision=HIGHEST, preferred_element_type=F32) + gsums[h]
                beyond = total[h] - nearer
                inv = 1.0 / (1.0 + e)
                sig_abs, sig_neg = inv, e * inv
                pos = z >= 0.0
                sig_z = jnp.where(pos, sig_abs, sig_neg)
                sig_mz = jnp.where(pos, sig_neg, sig_abs)
                dz = (jnp.where(mask, g * sig_mz - sig_z * beyond, 0.0) * ATTN_SCALE).astype(BF16)
                dqs.append(_dot(dz, kb))
                dk = dk + _dot_tn(dz, qh[h])
                dv = dv + _dot_tn(a.astype(BF16), doh[h])
                carries[h] = carries[h] + jnp.sum(lmb, axis=1, keepdims=True)
                gsums[h] = gsums[h] + jnp.sum(g, axis=1, keepdims=True)
            dk_ref[pl.ds(start, QB), :] += dk
            dv_ref[pl.ds(start, QB), :] += dv
            dq = dq + jnp.where(low_lanes, dqs[0], dqs[1])
            done = (jnp.maximum(jnp.max(carries[0]), jnp.max(carries[1])) < EXP_UNDERFLOW).astype(jnp.int32)
            return n + 1, done, dq, carries[0], carries[1], gsums[0], gsums[1]

        col0 = jnp.zeros((QB, 1), F32)
        init = (jnp.int32(0), jnp.int32(0), jnp.zeros((QB, QB), F32), col0, col0, col0, col0)
        dq_ref[...] = lax.while_loop(cond, step, init)[2]

    blk = pl.BlockSpec((QB, QB), lambda p, i: (i, p))
    full = lambda off: pl.BlockSpec((S, QB), lambda p, i: (0, off + p), pipeline_mode=pl.Buffered(1))
    return pl.pallas_call(
        body, name="bwd_attn", grid=(n_pairs, nqb),
        in_specs=[blk, full(n_pairs), full(2 * n_pairs), blk, blk],
        out_specs=[blk, pl.BlockSpec((S, QB), lambda p, i: (0, p)), pl.BlockSpec((S, QB), lambda p, i: (0, p))],
        out_shape=[jax.ShapeDtypeStruct((S, n_pairs * QB), F32)] * 3,
        compiler_params=_params("parallel", "arbitrary"),
    )(qkv, qkv, qkv, attn_out, d_attn)


def _bwd_pool(u, d_pool, w_pool, tile):
    S, C = u.shape
    n_tiles = S // tile
    ng = len(POOL_WINDOWS)

    def body(u_ref, uh_ref, d_ref, dh_ref, wp_ref, du_ref, dwp_ref):
        i = pl.program_id(0)
        first = i == 0
        halo = jnp.where(first, 0.0, uh_ref[...])
        parts = _pool_deviation(u_ref[...], halo, i * tile)
        dout = d_ref[...]
        nxt = jnp.where(i == n_tiles - 1, 0.0, dh_ref[...])
        dext = jnp.concatenate([dout, nxt], axis=0).astype(BF16)
        counts = _pool_counts(i * tile, tile + HALO)
        dps, scaled = [], []
        for g in range(ng):
            lanes = slice(g * POOL_GROUP, (g + 1) * POOL_GROUP)
            dp = _dot_nt(dext[:, lanes], wp_ref[g].astype(BF16))
            dps.append(dp[:tile])
            scaled.append(dp / counts[g])
        sums = _window_sums(jnp.concatenate(scaled, axis=1), forward=True)
        for g, w in enumerate(POOL_WINDOWS):
            lanes = slice(g * POOL_GROUP, (g + 1) * POOL_GROUP)
            du_ref[:, lanes] = sums[w][:tile, lanes] - dps[g]
            _accumulate(dwp_ref.at[g], _dot_tn(parts[g].astype(BF16), dext[:tile, lanes]), first)

    row = pl.BlockSpec((tile, C), lambda i: (i, 0))
    return pl.pallas_call(
        body, name="bwd_pool", grid=(n_tiles,),
        in_specs=[row, _prev_halo_spec(tile, C), row, _next_halo_spec(tile, C, n_tiles), _const(w_pool.shape)],
        out_specs=[row, pl.BlockSpec(w_pool.shape, lambda i: (0, 0, 0))],
        out_shape=[jax.ShapeDtypeStruct((S, C), F32), jax.ShapeDtypeStruct(w_pool.shape, F32)],
        compiler_params=_params("arbitrary"),
    )(u, u, d_pool, d_pool, w_pool)


def _bwd_inproj(du, dq, dk, dv, w_in_g, h1, x, dx2, g1, tile):
    S, D = x.shape
    nb, _, cs = w_in_g.shape
    C = du.shape[1]
    per = C // cs

    def body(du_ref, dq_ref, dk_ref, dv_ref, w_ref, h_ref, x_ref, dx2_ref, g_ref, dx_ref, dw_ref, dg_ref):
        first = pl.program_id(0) == 0
        h = h_ref[...]
        dh = jnp.zeros((tile, D), F32)
        for d in range(nb):
            src = (du_ref, dq_ref, dk_ref, dv_ref)[d // per]
            dproj = src[:, (d % per) * cs:(d % per + 1) * cs].astype(BF16)
            dh = dh + _dot_nt(dproj, w_ref[d])
            _accumulate(dw_ref.at[d], _dot_tn(h, dproj), first)
        xf = x_ref[...]
        r1 = _rms(xf)
        n1 = xf * r1
        _accumulate(dg_ref, _colsum(dh * n1), first)
        dx_ref[...] = dx2_ref[...] + _norm_bwd(dh * g_ref[...], n1, r1)

    row = lambda w: pl.BlockSpec((tile, w), lambda i: (i, 0))
    return pl.pallas_call(
        body, name="bwd_inproj", grid=(S // tile,),
        in_specs=[row(C), row(C), row(C), row(C), _const(w_in_g.shape), row(D), row(D), row(D), _const((1, D))],
        out_specs=[row(D), pl.BlockSpec(w_in_g.shape, lambda i: (0, 0, 0)), pl.BlockSpec((1, D), lambda i: (0, 0))],
        out_shape=[jax.ShapeDtypeStruct((S, D), F32), jax.ShapeDtypeStruct(w_in_g.shape, F32), jax.ShapeDtypeStruct((1, D), F32)],
        compiler_params=_params("arbitrary"),
    )(du, dq, dk, dv, w_in_g, h1, x, dx2, g1)


def _local_step(x, target, g1, w_in_g, w_pool, pool_scale, attn_scale, w_out, g2, g3, w_up_g, conv_w_g, conv_b_g,
                w_down4, g4):
    S = x.shape[0]
    big = min(512, S)
    small = min(256, S)
    n_pairs = pool_scale.shape[1] // QB

    h1, u, qkv = _fwd_inproj(x, g1, w_in_g, big)
    pool_out = _fwd_pool(u, w_pool, big)
    attn_out = _fwd_attn(qkv, n_pairs)
    mix, x2, h2 = _fwd_outproj(pool_out, attn_out, pool_scale, attn_scale, w_out, x, g2, g3, big)
    upre = _fwd_up(h2, w_up_g, big)
    dy, df, loss_cols, dg4 = _fwd_ffn_loss(upre, conv_w_g, conv_b_g, w_down4, x2, target, g4, small)

    dgate, dval, d_wd4, d_cb, d_cw = _bwd_down(upre, conv_w_g, conv_b_g, w_down4, df, big)
    dupre, dx2, dmix, dg3, dg2 = _bwd_up_x(dgate, dval, conv_w_g, w_up_g, x2, dy, mix, g2, g3, small)
    d_wup = _bwd_weight(h2, dupre, big)
    d_pool, d_attn, d_wout, d_ps, d_as = _bwd_outproj(dmix, w_out, pool_out, attn_out, pool_scale, attn_scale, big)
    dq, dk, dv = _bwd_attn(qkv, attn_out, d_attn, n_pairs)
    du, d_wp = _bwd_pool(u, d_pool, w_pool, big)
    dx, d_win, dg1 = _bwd_inproj(du, dq, dk, dv, w_in_g, h1, x, dx2, g1, big)
    return dict(loss_cols=loss_cols, dx=dx, g1=dg1, w_in=d_win, w_pool=d_wp, pool_scale=d_ps, attn_scale=d_as,
                w_out=d_wout, g2=dg2, g3=dg3, w_up=d_wup, conv_w=d_cw, conv_b=d_cb, w_down=d_wd4, g4=dg4)


def _mesh_position():
    x, y, c = lax.axis_index("x"), lax.axis_index("y"), lax.axis_index("c")
    return x, y, c, 4 * x + 2 * y + c


def _peer(x, y, c, k):
    px = 1 - x if k & 4 else x
    py = 1 - y if k & 2 else y
    pc = 1 - c if k & 1 else c
    return (px, py, pc), 4 * px + 2 * py + pc


def _all_to_all(arrays, gather, name):
    n = len(arrays)
    out_shapes = [jax.ShapeDtypeStruct(((N_DEV,) + a.shape) if gather else a.shape, a.dtype) for a in arrays]

    def body(*refs):
        ins, outs = refs[:n], refs[n:2 * n]
        send_sems, recv_sems, local_sems = refs[2 * n:]
        x, y, c, me = _mesh_position()
        local, remote = [], []
        for a in range(n):
            mine = ins[a] if gather else ins[a].at[me]
            cp = pltpu.make_async_copy(mine, outs[a].at[me], local_sems.at[a])
            cp.start()
            local.append(cp)
            for k in range(1, N_DEV):
                peer, peer_idx = _peer(x, y, c, k)
                src = ins[a] if gather else ins[a].at[peer_idx]
                sem = a * (N_DEV - 1) + k - 1
                cp = pltpu.make_async_remote_copy(src_ref=src, dst_ref=outs[a].at[me], send_sem=send_sems.at[sem],
                                                  recv_sem=recv_sems.at[sem], device_id=peer, device_id_type=MESH)
                cp.start()
                remote.append(cp)
        for cp in remote:
            cp.wait_send()
        for cp in remote:
            cp.wait_recv()
        for cp in local:
            cp.wait()

    any_spec = pl.BlockSpec(memory_space=pl.ANY)
    return pl.pallas_call(
        body, name=name,
        in_specs=[any_spec] * n, out_specs=[any_spec] * n, out_shape=out_shapes,
        scratch_shapes=[pltpu.SemaphoreType.DMA((n * (N_DEV - 1),)), pltpu.SemaphoreType.DMA((n * (N_DEV - 1),)),
                        pltpu.SemaphoreType.DMA((n,))],
    )(*arrays)


def _reduce_adamw(parts, w, m, v, rows):
    R, C = w.shape

    def body(p_ref, w_ref, m_ref, v_ref, g_ref, d_ref, nm_ref, nv_ref):
        g = p_ref[0].astype(F32)
        for s in range(1, N_DEV):
            g = g + p_ref[s].astype(F32)
        g_ref[...] = g
        m_new = ADAM_B1 * m_ref[...] + (1.0 - ADAM_B1) * g
        v_new = ADAM_B2 * v_ref[...] + (1.0 - ADAM_B2) * (g * g)
        m_hat = m_new / (1.0 - ADAM_B1 ** ADAM_STEP)
        v_hat = v_new / (1.0 - ADAM_B2 ** ADAM_STEP)
        d_ref[...] = -ADAM_LR * (m_hat / (jnp.sqrt(v_hat) + ADAM_EPS) + ADAM_WD * w_ref[...])
        nm_ref[...] = m_new
        nv_ref[...] = v_new

    row = pl.BlockSpec((rows, C), lambda i: (i, 0))
    return pl.pallas_call(
        body, name="reduce_adamw", grid=(R // rows,),
        in_specs=[pl.BlockSpec((N_DEV, rows, C), lambda i: (0, i, 0)), row, row, row],
        out_specs=[row] * 4, out_shape=[jax.ShapeDtypeStruct((R, C), F32)] * 4,
        compiler_params=_params("parallel"),
    )(parts, w, m, v)


def _row_tile(rows, cols):
    fits = [t for t in range(8, rows + 1, 8) if rows % t == 0 and N_DEV * t * cols * 4 <= 4 * 1024 * 1024]
    return max(fits) if fits else rows


SMALL_COLS = 1024


def _pack_small(vals):
    rows = []
    for a in vals:
        flat = a.reshape(-1)
        pad = (-flat.shape[0]) % SMALL_COLS
        rows.append(jnp.pad(flat, (0, pad)).reshape(-1, SMALL_COLS))
    packed = jnp.concatenate(rows, axis=0)
    return jnp.pad(packed, ((0, (-packed.shape[0]) % 8), (0, 0)))


def _unpack_small(packed, like):
    out, r = [], 0
    for a in like:
        n = a.size
        nr = -(-n // SMALL_COLS)
        out.append(packed[r:r + nr].reshape(-1)[:n].reshape(a.shape))
        r += nr
    return out


def kernel(x, norm_mix_pre, w_in, w_pool, pool_scale, attn_scale, w_out, norm_mix_post, norm_ffn_pre, w_up, conv_w, conv_b, w_down, norm_ffn_post, loss_target, m_norm_mix_pre, m_w_in, m_w_pool, m_pool_scale, m_attn_scale, m_w_out, m_norm_mix_post, m_norm_ffn_pre, m_w_up, m_conv_w, m_conv_b, m_w_down, m_norm_ffn_post, v_norm_mix_pre, v_w_in, v_w_pool, v_pool_scale, v_attn_scale, v_w_out, v_norm_mix_post, v_norm_ffn_pre, v_w_up, v_conv_w, v_conv_b, v_w_down, v_norm_ffn_post):
    S, D = x.shape[1], x.shape[2]
    d_ff_block = w_up.shape[2]

    w_in_g, w_out_g, w_up_g, w_down_g, conv_w_g = _all_to_all(
        [w_in[0].astype(BF16), w_out[0].astype(BF16), w_up[0].astype(BF16), w_down[0].astype(BF16), conv_w[0]], gather=True, name="gather_weights")
    w_out_full = w_out_g.reshape(D, D)
    w_down4 = w_down_g.reshape(D_FF_SHARDS, d_ff_block, D)
    conv_b_g = conv_b.reshape(N_DEV, 1, d_ff_block)

    r = _local_step(x[0], loss_target[0], norm_mix_pre, w_in_g, w_pool[0], pool_scale, attn_scale, w_out_full,
                    norm_mix_post, norm_ffn_pre, w_up_g, conv_w_g, conv_b_g, w_down4, norm_ffn_post)
    loss = lax.psum(0.5 * jnp.sum(r["loss_cols"]) / D, ("x", "y", "c"))

    small_names = ["norm_mix_pre", "w_pool", "pool_scale", "attn_scale", "norm_mix_post", "norm_ffn_pre", "conv_b", "norm_ffn_post"]
    small_w = dict(norm_mix_pre=norm_mix_pre, w_pool=w_pool, pool_scale=pool_scale, attn_scale=attn_scale,
                   norm_mix_post=norm_mix_post, norm_ffn_pre=norm_ffn_pre, conv_b=conv_b, norm_ffn_post=norm_ffn_post)
    small_m = dict(norm_mix_pre=m_norm_mix_pre, w_pool=m_w_pool, pool_scale=m_pool_scale, attn_scale=m_attn_scale,
                   norm_mix_post=m_norm_mix_post, norm_ffn_pre=m_norm_ffn_pre, conv_b=m_conv_b, norm_ffn_post=m_norm_ffn_post)
    small_v = dict(norm_mix_pre=v_norm_mix_pre, w_pool=v_w_pool, pool_scale=v_pool_scale, attn_scale=v_attn_scale,
                   norm_mix_post=v_norm_mix_post, norm_ffn_pre=v_norm_ffn_pre, conv_b=v_conv_b, norm_ffn_post=v_norm_ffn_post)
    small_g = dict(norm_mix_pre=r["g1"], w_pool=r["w_pool"], pool_scale=r["pool_scale"], attn_scale=r["attn_scale"],
                   norm_mix_post=r["g2"], norm_ffn_pre=r["g3"], conv_b=r["conv_b"], norm_ffn_post=r["g4"])
    like = [small_w[n] for n in small_names]
    packed_g = _pack_small([small_g[n] for n in small_names])

    d_wout_g = r["w_out"].reshape(N_DEV, D // N_DEV, D)
    d_wdown_g = r["w_down"].reshape(N_DEV, w_down.shape[1], D)
    (small_parts,) = _all_to_all([packed_g], gather=True, name="gather_small_grads")
    big_parts = _all_to_all([r["w_in"], d_wout_g, r["w_up"], d_wdown_g, r["conv_w"]], gather=False, name="exchange_grads")

    def update(parts, w, m, v):
        R, C = w.shape
        return _reduce_adamw(parts, w, m, v, _row_tile(R, C))

    res = {}
    res["w_in"] = update(big_parts[0], w_in[0], m_w_in[0], v_w_in[0])
    res["w_out"] = update(big_parts[1], w_out[0], m_w_out[0], v_w_out[0])
    res["w_up"] = update(big_parts[2], w_up[0], m_w_up[0], v_w_up[0])
    res["w_down"] = update(big_parts[3], w_down[0], m_w_down[0], v_w_down[0])
    res["conv_w"] = update(big_parts[4], conv_w[0], m_conv_w[0], v_conv_w[0])
    small_res = update(small_parts, _pack_small(like), _pack_small([small_m[n] for n in small_names]),
                       _pack_small([small_v[n] for n in small_names]))
    small_res = [_unpack_small(t, like) for t in small_res]
    for idx, n in enumerate(small_names):
        res[n] = tuple(t[idx] for t in small_res)

    order = ["norm_mix_pre", "w_in", "w_pool", "pool_scale", "attn_scale", "w_out", "norm_mix_post", "norm_ffn_pre",
             "w_up", "conv_w", "conv_b", "w_down", "norm_ffn_post"]
    shaped = {n: tuple(t.reshape(s.shape) for t in res[n])
              for n, s in dict(norm_mix_pre=norm_mix_pre, w_in=w_in, w_pool=w_pool, pool_scale=pool_scale, attn_scale=attn_scale,
                               w_out=w_out, norm_mix_post=norm_mix_post, norm_ffn_pre=norm_ffn_pre, w_up=w_up, conv_w=conv_w,
                               conv_b=conv_b, w_down=w_down, norm_ffn_post=norm_ffn_post).items()}
    outs = [loss, r["dx"].reshape(x.shape)]
    for k in range(4):
        outs += [shaped[n][k] for n in order]
    return tuple(outs)
```

```python
import functools

import jax
import jax.numpy as jnp
from jax import lax
from jax.experimental import pallas as pl
from jax.experimental.pallas import tpu as pltpu

F32 = jnp.float32
BF16 = jnp.bfloat16
HIGHEST = lax.Precision.HIGHEST

N_DEV = 8
EPS = 1e-6
POOL_WINDOWS = (2, 4, 8, 16)
POOL_GROUP = 128
HALO = 16
HEAD_DIM = 64
QB = 128
ATTN_SCALE = HEAD_DIM ** -0.5
EXP_UNDERFLOW = -100.0
D_FF_SHARDS = 4

ADAM_LR = 0.001
ADAM_B1 = 0.9
ADAM_B2 = 0.999
ADAM_EPS = 1e-08
ADAM_WD = 0.01
ADAM_STEP = 10

VMEM_LIMIT_V7X = 56 * 1024 * 1024
MESH = pl.DeviceIdType.MESH


def _params(*semantics):
    return pltpu.CompilerParams(dimension_semantics=semantics, vmem_limit_bytes=VMEM_LIMIT_V7X)


def _const(shape):
    zeros = (0,) * len(shape)
    return pl.BlockSpec(shape, lambda *_: zeros, pipeline_mode=pl.Buffered(1))


def _dot(a, b):
    return jnp.dot(a, b, preferred_element_type=F32)


def _dot_nt(a, b):
    return lax.dot_general(a, b, (((1,), (1,)), ((), ())), preferred_element_type=F32)


def _dot_tn(a, b):
    return lax.dot_general(a, b, (((0,), (0,)), ((), ())), preferred_element_type=F32)


def _rms(v):
    return lax.rsqrt(jnp.mean(v * v, axis=-1, keepdims=True) + EPS)


def _norm_bwd(dn_times_gain, n, r):
    return r * (dn_times_gain - n * jnp.mean(dn_times_gain * n, axis=-1, keepdims=True))


def _accumulate(ref, value, first):
    @pl.when(first)
    def _():
        ref[...] = value

    @pl.when(jnp.logical_not(first))
    def _():
        ref[...] += value


def _colsum(v):
    return jnp.sum(v, axis=0, keepdims=True)


def _fwd_inproj(x, g1, w_in_g, tile):
    S, D = x.shape
    nb, _, cs = w_in_g.shape
    d_pool = 2 * cs

    def body(x_ref, g_ref, w_ref, h_ref, u_ref, qkv_ref):
        xf = x_ref[...]
        h = (xf * _rms(xf) * g_ref[...]).astype(BF16)
        h_ref[...] = h
        for d in range(nb):
            o = _dot(h, w_ref[d])
            if d < 2:
                u_ref[:, d * cs:(d + 1) * cs] = o
            else:
                qkv_ref[:, (d - 2) * cs:(d - 1) * cs] = o.astype(BF16)

    return pl.pallas_call(
        body, name="fwd_inproj", grid=(S // tile,),
        in_specs=[pl.BlockSpec((tile, D), lambda i: (i, 0)), _const((1, D)), _const(w_in_g.shape)],
        out_specs=[pl.BlockSpec((tile, D), lambda i: (i, 0)), pl.BlockSpec((tile, d_pool), lambda i: (i, 0)),
                   pl.BlockSpec((tile, 3 * d_pool), lambda i: (i, 0))],
        out_shape=[jax.ShapeDtypeStruct((S, D), BF16), jax.ShapeDtypeStruct((S, d_pool), F32),
                   jax.ShapeDtypeStruct((S, 3 * d_pool), BF16)],
        compiler_params=_params("parallel"),
    )(x, g1, w_in_g)


def _window_sums(ext, forward):
    n = ext.shape[0]
    sums, s, sh = {}, ext, 1
    while sh < POOL_WINDOWS[-1]:
        s = s + pltpu.roll(s, (n - sh) if forward else sh, axis=0)
        sh *= 2
        sums[sh] = s
    return sums


def _pool_counts(t0, rows):
    t1 = (lax.broadcasted_iota(jnp.int32, (rows, 1), 0) + t0 + 1).astype(F32)
    return [jnp.minimum(t1, float(w)) for w in POOL_WINDOWS]


def _pool_deviation(u, halo, t0):
    T = u.shape[0]
    sums = _window_sums(jnp.concatenate([halo, u], axis=0), forward=False)
    counts = _pool_counts(t0, T)
    parts = []
    for g, w in enumerate(POOL_WINDOWS):
        lanes = slice(g * POOL_GROUP, (g + 1) * POOL_GROUP)
        parts.append(sums[w][HALO:, lanes] / counts[g] - u[:, lanes])
    return parts


def _prev_halo_spec(tile, width):
    return pl.BlockSpec((HALO, width), lambda i: (jnp.maximum(i * (tile // HALO) - 1, 0), 0))


def _next_halo_spec(tile, width, n_tiles):
    last = n_tiles * (tile // HALO) - 1
    return pl.BlockSpec((HALO, width), lambda i: (jnp.minimum((i + 1) * (tile // HALO), last), 0))


def _fwd_pool(u, w_pool, tile):
    S, C = u.shape

    def body(u_ref, halo_ref, wp_ref, o_ref):
        i = pl.program_id(0)
        halo = jnp.where(i > 0, halo_ref[...], 0.0)
        parts = _pool_deviation(u_ref[...], halo, i * tile)
        for g, p in enumerate(parts):
            o_ref[:, g * POOL_GROUP:(g + 1) * POOL_GROUP] = _dot(p.astype(BF16), wp_ref[g].astype(BF16))

    return pl.pallas_call(
        body, name="fwd_pool", grid=(S // tile,),
        in_specs=[pl.BlockSpec((tile, C), lambda i: (i, 0)), _prev_halo_spec(tile, C), _const(w_pool.shape)],
        out_specs=pl.BlockSpec((tile, C), lambda i: (i, 0)),
        out_shape=jax.ShapeDtypeStruct((S, C), F32),
        compiler_params=_params("parallel"),
    )(u, u, w_pool)


def _low_lanes():
    return lax.broadcasted_iota(jnp.int32, (QB, 2 * HEAD_DIM), 1) < HEAD_DIM


def _triangle(inclusive):
    row = lax.broadcasted_iota(jnp.int32, (QB, QB), 0)
    col = lax.broadcasted_iota(jnp.int32, (QB, QB), 1)
    return ((row >= col) if inclusive else (row > col)).astype(F32)


def _causal_mask(width, offset):
    row = lax.broadcasted_iota(jnp.int32, (QB, width), 0)
    col = lax.broadcasted_iota(jnp.int32, (QB, width), 1)
    return col < row + offset


def _suffix_sums(vals, tri, carry):
    n = vals.shape[1] // QB
    out, run = [None] * n, carry
    for b in reversed(range(n)):
        blk = vals[:, b * QB:(b + 1) * QB]
        out[b] = jnp.dot(blk, tri, precision=HIGHEST, preferred_element_type=F32) + run
        run = run + jnp.sum(blk, axis=1, keepdims=True)
    return (out[0] if n == 1 else jnp.concatenate(out, axis=1)), run


def _attn_tile(qh, kw, mask, carry, after_s):
    z = _dot_nt(qh, kw) * ATTN_SCALE
    e = jnp.exp(-jnp.abs(z))
    softplus = jnp.maximum(z, 0.0) + jnp.log(1.0 + e)
    log_1m_beta = -softplus if mask is None else jnp.where(mask, -softplus, 0.0)
    stick, carry = _suffix_sums(log_1m_beta, after_s, carry)
    a = jnp.exp(z - softplus + stick)
    if mask is not None:
        a = jnp.where(mask, a, 0.0)
    return z, e, a, carry


def _split_heads(v, low_lanes):
    return jnp.where(low_lanes, v, 0.0).astype(BF16), jnp.where(low_lanes, 0.0, v).astype(BF16)


def _sweep_done(c0, c1):
    return (jnp.maximum(jnp.max(c0), jnp.max(c1)) < EXP_UNDERFLOW).astype(jnp.int32)


def _first_window(i):
    first_blk = jnp.maximum(i - 1, 0)
    return first_blk, pl.multiple_of(first_blk * QB, QB), (i - first_blk) * QB


def _fwd_attn(qkv, n_pairs):
    S = qkv.shape[0]
    nqb = S // QB
    assert nqb >= 2

    def body(q_ref, k_ref, v_ref, o_ref):
        i = pl.program_id(1)
        low_lanes = _low_lanes()
        after_s = _triangle(False)
        qh = _split_heads(q_ref[...].astype(F32), low_lanes)
        first_blk, start, offset = _first_window(i)
        kw = k_ref[pl.ds(start, 2 * QB), :]
        vw = v_ref[pl.ds(start, 2 * QB), :]
        mask = _causal_mask(2 * QB, offset)
        zero = jnp.zeros((QB, 1), F32)
        outs, carries = [], []
        for h in range(2):
            _, _, a, c = _attn_tile(qh[h], kw, mask, zero, after_s)
            outs.append(_dot(a.astype(BF16), vw))
            carries.append(c)
        acc = jnp.where(low_lanes, outs[0], outs[1])

        def cond(c):
            return jnp.logical_and(c[0] >= 0, c[1] == 0)

        def step(c):
            j, _, acc, c0, c1 = c
            at = pl.multiple_of(j * QB, QB)
            kb = k_ref[pl.ds(at, QB), :]
            vb = v_ref[pl.ds(at, QB), :]
            cs, outs = [c0, c1], []
            for h in range(2):
                _, _, a, cs[h] = _attn_tile(qh[h], kb, None, cs[h], after_s)
                outs.append(_dot(a.astype(BF16), vb))
            acc = acc + jnp.where(low_lanes, outs[0], outs[1])
            return j - 1, _sweep_done(*cs), acc, cs[0], cs[1]

        init = (first_blk - 1, _sweep_done(*carries), acc, carries[0], carries[1])
        o_ref[...] = lax.while_loop(cond, step, init)[2]

    return pl.pallas_call(
        body, name="fwd_attn", grid=(n_pairs, nqb),
        in_specs=[pl.BlockSpec((QB, QB), lambda p, i: (i, p)),
                  pl.BlockSpec((S, QB), lambda p, i: (0, n_pairs + p), pipeline_mode=pl.Buffered(1)),
                  pl.BlockSpec((S, QB), lambda p, i: (0, 2 * n_pairs + p), pipeline_mode=pl.Buffered(1))],
        out_specs=pl.BlockSpec((QB, QB), lambda p, i: (i, p)),
        out_shape=jax.ShapeDtypeStruct((S, n_pairs * QB), F32),
        compiler_params=_params("parallel", "parallel"),
    )(qkv, qkv, qkv)


def _normalized_heads(pool_out, attn_out):
    rp, ra = _rms(pool_out), _rms(attn_out)
    return pool_out * rp, rp, attn_out * ra, ra


def _fwd_outproj(pool_out, attn_out, pool_scale, attn_scale, w_out, x, g2, g3, tile):
    S, D = x.shape
    C = pool_out.shape[1]

    def body(p_ref, a_ref, ps_ref, as_ref, w_ref, x_ref, g2_ref, g3_ref, mix_ref, x2_ref, h2_ref):
        n_p, _, n_a, _ = _normalized_heads(p_ref[...], a_ref[...])
        mix = _dot((n_p * ps_ref[...]).astype(BF16), w_ref[:C, :]) + _dot((n_a * as_ref[...]).astype(BF16), w_ref[C:, :])
        mix_ref[...] = mix
        x2 = x_ref[...] + mix * _rms(mix) * g2_ref[...]
        x2_ref[...] = x2
        h2_ref[...] = (x2 * _rms(x2) * g3_ref[...]).astype(BF16)

    row = lambda w: pl.BlockSpec((tile, w), lambda i: (i, 0))
    return pl.pallas_call(
        body, name="fwd_outproj", grid=(S // tile,),
        in_specs=[row(C), row(C), _const((1, C)), _const((1, C)), _const(w_out.shape), row(D), _const((1, D)), _const((1, D))],
        out_specs=[row(D), row(D), row(D)],
        out_shape=[jax.ShapeDtypeStruct((S, D), F32), jax.ShapeDtypeStruct((S, D), F32), jax.ShapeDtypeStruct((S, D), BF16)],
        compiler_params=_params("parallel"),
    )(pool_out, attn_out, pool_scale, attn_scale, w_out, x, g2, g3)


def _fwd_up(h2, w_up_g, tile):
    S, D = h2.shape
    nb, _, cs = w_up_g.shape

    def body(h_ref, w_ref, o_ref):
        h = h_ref[...]
        for d in range(nb):
            o_ref[d] = _dot(h, w_ref[d]).astype(BF16)

    return pl.pallas_call(
        body, name="fwd_up", grid=(S // tile,),
        in_specs=[pl.BlockSpec((tile, D), lambda i: (i, 0)), _const(w_up_g.shape)],
        out_specs=pl.BlockSpec((nb, tile, cs), lambda i: (0, i, 0)),
        out_shape=jax.ShapeDtypeStruct((nb, S, cs), BF16),
        compiler_params=_params("parallel"),
    )(h2, w_up_g)


def _conv_taps(tile_rows, halo_rows):
    T = tile_rows.shape[0]
    ext = jnp.concatenate([halo_rows.astype(F32), tile_rows.astype(F32)], axis=0)
    return pltpu.roll(ext, 2, axis=0)[HALO:], pltpu.roll(ext, 1, axis=0)[HALO:], ext[HALO:]


def _tap_rows(cw_ref, d):
    return [cw_ref[d, k:k + 1, :] for k in range(3)]


def _gated_unit(taps_gate, taps_val, cw_gate, cw_val, cb_gate, cb_val):
    gate = cw_gate[0] * taps_gate[0] + cw_gate[1] * taps_gate[1] + cw_gate[2] * taps_gate[2] + cb_gate
    val = cw_val[0] * taps_val[0] + cw_val[1] * taps_val[1] + cw_val[2] * taps_val[2] + cb_val
    sig = 1.0 / (1.0 + jnp.exp(-gate))
    return gate, val, sig


def _fwd_ffn_loss(upre, conv_w_g, conv_b_g, w_down4, x2, target, g4, tile):
    nb, S, cs = upre.shape
    D = x2.shape[1]

    def body(u_ref, halo_ref, cw_ref, cb_ref, wd_ref, x2_ref, t_ref, g4_ref, dy_ref, df_ref, loss_ref, dg4_ref):
        i = pl.program_id(0)
        first = i == 0
        f = jnp.zeros((tile, D), F32)
        for s in range(D_FF_SHARDS):
            halo_g = jnp.where(first, jnp.zeros_like(halo_ref[s]), halo_ref[s])
            halo_v = jnp.where(first, jnp.zeros_like(halo_ref[s]), halo_ref[s + D_FF_SHARDS])
            gate, val, sig = _gated_unit(_conv_taps(u_ref[s], halo_g), _conv_taps(u_ref[s + D_FF_SHARDS], halo_v),
                                         _tap_rows(cw_ref, s), _tap_rows(cw_ref, s + D_FF_SHARDS), cb_ref[s], cb_ref[s + D_FF_SHARDS])
            f = f + _dot((gate * sig * val).astype(BF16), wd_ref[s])
        r4 = _rms(f)
        n4 = f * r4
        err = x2_ref[...] + n4 * g4_ref[...] - t_ref[...]
        dy = err * (1.0 / D)
        dy_ref[...] = dy
        df_ref[...] = _norm_bwd(dy * g4_ref[...], n4, r4).astype(BF16)
        _accumulate(loss_ref, _colsum(err * err), first)
        _accumulate(dg4_ref, _colsum(dy * n4), first)

    row = lambda w: pl.BlockSpec((tile, w), lambda i: (i, 0))
    return pl.pallas_call(
        body, name="fwd_ffn_loss", grid=(S // tile,),
        in_specs=[pl.BlockSpec((nb, tile, cs), lambda i: (0, i, 0)),
                  pl.BlockSpec((nb, HALO, cs), lambda i: (0, jnp.maximum(i * (tile // HALO) - 1, 0), 0)),
                  _const(conv_w_g.shape), _const(conv_b_g.shape), _const(w_down4.shape), row(D), row(D), _const((1, D))],
        out_specs=[row(D), row(D), pl.BlockSpec((1, D), lambda i: (0, 0)), pl.BlockSpec((1, D), lambda i: (0, 0))],
        out_shape=[jax.ShapeDtypeStruct((S, D), F32), jax.ShapeDtypeStruct((S, D), BF16),
                   jax.ShapeDtypeStruct((1, D), F32), jax.ShapeDtypeStruct((1, D), F32)],
        compiler_params=_params("arbitrary"),
    )(upre, upre, conv_w_g, conv_b_g, w_down4, x2, target, g4)


def _bwd_down(upre, conv_w_g, conv_b_g, w_down4, df, tile):
    nb, S, cs = upre.shape
    D = df.shape[1]
    n_tiles = S // tile

    def body(ug_ref, uv_ref, hg_ref, hv_ref, cwg_ref, cwv_ref, cbg_ref, cbv_ref, wd_ref, df_ref,
             dg_ref, dv_ref, dwd_ref, dbg_ref, dbv_ref, dcwg_ref, dcwv_ref):
        i = pl.program_id(1)
        first = i == 0
        halo_g = jnp.where(first, jnp.zeros_like(hg_ref[0]), hg_ref[0])
        halo_v = jnp.where(first, jnp.zeros_like(hv_ref[0]), hv_ref[0])
        taps_g, taps_v = _conv_taps(ug_ref[0], halo_g), _conv_taps(uv_ref[0], halo_v)
        gate, val, sig = _gated_unit(taps_g, taps_v, _tap_rows(cwg_ref, 0), _tap_rows(cwv_ref, 0), cbg_ref[0], cbv_ref[0])
        silu = gate * sig
        dfb = df_ref[...]
        dact = _dot_nt(dfb, wd_ref[0])
        _accumulate(dwd_ref.at[0], _dot_tn((silu * val).astype(BF16), dfb), first)
        dgate = dact * val * (sig * (1.0 + gate * (1.0 - sig)))
        dval = dact * silu
        dg_ref[0] = dgate.astype(BF16)
        dv_ref[0] = dval.astype(BF16)
        _accumulate(dbg_ref.at[0], _colsum(dgate), first)
        _accumulate(dbv_ref.at[0], _colsum(dval), first)
        _accumulate(dcwg_ref.at[0], jnp.concatenate([_colsum(dgate * t) for t in taps_g], axis=0), first)
        _accumulate(dcwv_ref.at[0], jnp.concatenate([_colsum(dval * t) for t in taps_v], axis=0), first)

    half = D_FF_SHARDS
    blk = lambda off: pl.BlockSpec((1, tile, cs), lambda s, i: (s + off, i, 0))
    halo = lambda off: pl.BlockSpec((1, HALO, cs), lambda s, i: (s + off, jnp.maximum(i * (tile // HALO) - 1, 0), 0))
    par = lambda off, r: pl.BlockSpec((1, r, cs), lambda s, i: (s + off, 0, 0))
    outs = pl.pallas_call(
        body, name="bwd_down", grid=(half, n_tiles),
        in_specs=[blk(0), blk(half), halo(0), halo(half), par(0, 3), par(half, 3), par(0, 1), par(half, 1),
                  pl.BlockSpec((1, cs, D), lambda s, i: (s, 0, 0)), pl.BlockSpec((tile, D), lambda s, i: (i, 0))],
        out_specs=[blk(0), blk(0), pl.BlockSpec((1, cs, D), lambda s, i: (s, 0, 0)),
                   par(0, 1), par(0, 1), par(0, 3), par(0, 3)],
        out_shape=[jax.ShapeDtypeStruct((half, S, cs), BF16), jax.ShapeDtypeStruct((half, S, cs), BF16),
                   jax.ShapeDtypeStruct((half, cs, D), F32),
                   jax.ShapeDtypeStruct((half, 1, cs), F32), jax.ShapeDtypeStruct((half, 1, cs), F32),
                   jax.ShapeDtypeStruct((half, 3, cs), F32), jax.ShapeDtypeStruct((half, 3, cs), F32)],
        compiler_params=_params("parallel", "arbitrary"),
    )(upre, upre, upre, upre, conv_w_g, conv_w_g, conv_b_g, conv_b_g, w_down4, df)
    dgate, dval, d_wd, dbg, dbv, dcwg, dcwv = outs
    return dgate, dval, d_wd, jnp.concatenate([dbg, dbv], axis=0), jnp.concatenate([dcwg, dcwv], axis=0)


def _bwd_up_x(dgate, dval, conv_w_g, w_up_g, x2, dy, mix, g2, g3, tile):
    half, S, cs = dgate.shape
    nb = 2 * half
    D = x2.shape[1]
    n_tiles = S // tile

    def body(dg_ref, dv_ref, hg_ref, hv_ref, cw_ref, w_ref, x2_ref, dy_ref, mix_ref, g2_ref, g3_ref,
             dupre_ref, dx2_ref, dmix_ref, dg3_ref, dg2_ref):
        i = pl.program_id(0)
        first = i == 0
        last = i == n_tiles - 1
        dh2 = jnp.zeros((tile, D), F32)
        for d in range(nb):
            src, halo = (dg_ref, hg_ref) if d < half else (dv_ref, hv_ref)
            nxt = jnp.where(last, jnp.zeros_like(halo[d % half]), halo[d % half])
            ext = jnp.concatenate([src[d % half].astype(F32), nxt.astype(F32)], axis=0)
            n = ext.shape[0]
            cw = _tap_rows(cw_ref, d)
            dupre = (cw[2] * ext + cw[1] * pltpu.roll(ext, n - 1, axis=0) + cw[0] * pltpu.roll(ext, n - 2, axis=0))[:tile]
            dupre = dupre.astype(BF16)
            dupre_ref[d] = dupre
            dh2 = dh2 + _dot_nt(dupre, w_ref[d])
        x2 = x2_ref[...]
        r3 = _rms(x2)
        n3 = x2 * r3
        _accumulate(dg3_ref, _colsum(dh2 * n3), first)
        dx2 = dy_ref[...] + _norm_bwd(dh2 * g3_ref[...], n3, r3)
        dx2_ref[...] = dx2
        mix = mix_ref[...]
        r2 = _rms(mix)
        n2 = mix * r2
        _accumulate(dg2_ref, _colsum(dx2 * n2), first)
        dmix_ref[...] = _norm_bwd(dx2 * g2_ref[...], n2, r2).astype(BF16)

    row = lambda w: pl.BlockSpec((tile, w), lambda i: (i, 0))
    blk = pl.BlockSpec((half, tile, cs), lambda i: (0, i, 0))
    last_halo = n_tiles * (tile // HALO) - 1
    halo = pl.BlockSpec((half, HALO, cs), lambda i: (0, jnp.minimum((i + 1) * (tile // HALO), last_halo), 0))
    acc = pl.BlockSpec((1, D), lambda i: (0, 0))
    return pl.pallas_call(
        body, name="bwd_up_x", grid=(n_tiles,),
        in_specs=[blk, blk, halo, halo, _const(conv_w_g.shape), _const(w_up_g.shape), row(D), row(D), row(D),
                  _const((1, D)), _const((1, D))],
        out_specs=[pl.BlockSpec((nb, tile, cs), lambda i: (0, i, 0)), row(D), row(D), acc, acc],
        out_shape=[jax.ShapeDtypeStruct((nb, S, cs), BF16), jax.ShapeDtypeStruct((S, D), F32),
                   jax.ShapeDtypeStruct((S, D), BF16), jax.ShapeDtypeStruct((1, D), F32), jax.ShapeDtypeStruct((1, D), F32)],
        compiler_params=_params("arbitrary"),
    )(dgate, dval, dgate, dval, conv_w_g, w_up_g, x2, dy, mix, g2, g3)


def _bwd_weight(act, dout, tile):
    S, D = act.shape
    nb, _, cs = dout.shape

    def body(a_ref, d_ref, o_ref):
        _accumulate(o_ref.at[0], _dot_tn(a_ref[...], d_ref[0]), pl.program_id(1) == 0)

    return pl.pallas_call(
        body, name="bwd_w_up", grid=(nb, S // tile),
        in_specs=[pl.BlockSpec((tile, D), lambda d, i: (i, 0)), pl.BlockSpec((1, tile, cs), lambda d, i: (d, i, 0))],
        out_specs=pl.BlockSpec((1, D, cs), lambda d, i: (d, 0, 0)),
        out_shape=jax.ShapeDtypeStruct((nb, D, cs), F32),
        compiler_params=_params("parallel", "arbitrary"),
    )(act, dout)


def _bwd_outproj(dmix, w_out, pool_out, attn_out, pool_scale, attn_scale, tile):
    S, D = dmix.shape
    C = pool_out.shape[1]

    def body(dm_ref, w_ref, p_ref, a_ref, ps_ref, as_ref, dp_ref, da_ref, dw_ref, dps_ref, das_ref):
        first = pl.program_id(0) == 0
        dmx = dm_ref[...]
        dmerged = _dot_nt(dmx, w_ref[...])
        n_p, r_p, n_a, r_a = _normalized_heads(p_ref[...], a_ref[...])
        merged = jnp.concatenate([(n_p * ps_ref[...]).astype(BF16), (n_a * as_ref[...]).astype(BF16)], axis=1)
        _accumulate(dw_ref, _dot_tn(merged, dmx), first)
        dm_p, dm_a = dmerged[:, :C], dmerged[:, C:]
        _accumulate(dps_ref, _colsum(dm_p * n_p), first)
        _accumulate(das_ref, _colsum(dm_a * n_a), first)
        dp_ref[...] = _norm_bwd(dm_p * ps_ref[...], n_p, r_p)
        da_ref[...] = _norm_bwd(dm_a * as_ref[...], n_a, r_a)

    row = lambda w: pl.BlockSpec((tile, w), lambda i: (i, 0))
    return pl.pallas_call(
        body, name="bwd_outproj", grid=(S // tile,),
        in_specs=[row(D), _const(w_out.shape), row(C), row(C), _const((1, C)), _const((1, C))],
        out_specs=[row(C), row(C), pl.BlockSpec(w_out.shape, lambda i: (0, 0)),
                   pl.BlockSpec((1, C), lambda i: (0, 0)), pl.BlockSpec((1, C), lambda i: (0, 0))],
        out_shape=[jax.ShapeDtypeStruct((S, C), F32), jax.ShapeDtypeStruct((S, C), F32),
                   jax.ShapeDtypeStruct(w_out.shape, F32), jax.ShapeDtypeStruct((1, C), F32), jax.ShapeDtypeStruct((1, C), F32)],
        compiler_params=_params("arbitrary"),
    )(dmix, w_out, pool_out, attn_out, pool_scale, attn_scale)


def _bwd_attn(qkv, attn_out, d_attn, n_pairs):
    S = qkv.shape[0]
    nqb = S // QB

    def body(q_ref, k_ref, v_ref, o_ref, do_ref, dq_ref, dk_ref, dv_ref):
        i = pl.program_id(1)

        @pl.when(i == 0)
        def _():
            dk_ref[...] = jnp.zeros_like(dk_ref)
            dv_ref[...] = jnp.zeros_like(dv_ref)

        low_lanes = _low_lanes()
        after_s, from_s = _triangle(False), _triangle(True)
        do = do_ref[...]
        prod = do * o_ref[...]
        qh = _split_heads(q_ref[...].astype(F32), low_lanes)
        doh = _split_heads(do, low_lanes)
        total = (jnp.sum(jnp.where(low_lanes, prod, 0.0), axis=1, keepdims=True),
                 jnp.sum(jnp.where(low_lanes, 0.0, prod), axis=1, keepdims=True))

        def tile(kw, vw, at, width, mask, cs, gs):
            dqs = []
            dk = jnp.zeros((width, QB), F32)
            dv = jnp.zeros((width, QB), F32)
            for h in range(2):
                z, e, a, cs[h] = _attn_tile(qh[h], kw, mask, cs[h], after_s)
                g = a * _dot_nt(doh[h], vw)
                nearer, gs[h] = _suffix_sums(g, from_s, gs[h])
                beyond = total[h] - nearer
                inv = 1.0 / (1.0 + e)
                sig_abs, sig_neg = inv, e * inv
                pos = z >= 0.0
                sig_z = jnp.where(pos, sig_abs, sig_neg)
                sig_mz = jnp.where(pos, sig_neg, sig_abs)
                dz = g * sig_mz - sig_z * beyond
                if mask is not None:
                    dz = jnp.where(mask, dz, 0.0)
                dz = (dz * ATTN_SCALE).astype(BF16)
                dqs.append(_dot(dz, kw))
                dk = dk + _dot_tn(dz, qh[h])
                dv = dv + _dot_tn(a.astype(BF16), doh[h])
            dk_ref[pl.ds(at, width), :] += dk
            dv_ref[pl.ds(at, width), :] += dv
            return jnp.where(low_lanes, dqs[0], dqs[1])

        first_blk, start, offset = _first_window(i)
        zero = jnp.zeros((QB, 1), F32)
        cs, gs = [zero, zero], [zero, zero]
        dq = tile(k_ref[pl.ds(start, 2 * QB), :], v_ref[pl.ds(start, 2 * QB), :], start, 2 * QB,
                  _causal_mask(2 * QB, offset), cs, gs)

        def cond(c):
            return jnp.logical_and(c[0] >= 0, c[1] == 0)

        def step(c):
            j, _, dq, c0, c1, s0, s1 = c
            at = pl.multiple_of(j * QB, QB)
            cs, gs = [c0, c1], [s0, s1]
            dq = dq + tile(k_ref[pl.ds(at, QB), :], v_ref[pl.ds(at, QB), :], at, QB, None, cs, gs)
            return j - 1, _sweep_done(*cs), dq, cs[0], cs[1], gs[0], gs[1]

        init = (first_blk - 1, _sweep_done(*cs), dq, cs[0], cs[1], gs[0], gs[1])
        dq_ref[...] = lax.while_loop(cond, step, init)[2]

    blk = pl.BlockSpec((QB, QB), lambda p, i: (i, p))
    full = lambda off: pl.BlockSpec((S, QB), lambda p, i: (0, off + p), pipeline_mode=pl.Buffered(1))
    return pl.pallas_call(
        body, name="bwd_attn", grid=(n_pairs, nqb),
        in_specs=[blk, full(n_pairs), full(2 * n_pairs), blk, blk],
        out_specs=[blk, pl.BlockSpec((S, QB), lambda p, i: (0, p)), pl.BlockSpec((S, QB), lambda p, i: (0, p))],
        out_shape=[jax.ShapeDtypeStruct((S, n_pairs * QB), F32)] * 3,
        compiler_params=_params("parallel", "arbitrary"),
    )(qkv, qkv, qkv, attn_out, d_attn)


def _bwd_pool(u, d_pool, w_pool, tile):
    S, C = u.shape
    n_tiles = S // tile
    ng = len(POOL_WINDOWS)

    def body(u_ref, uh_ref, d_ref, dh_ref, wp_ref, du_ref, dwp_ref):
        i = pl.program_id(0)
        first = i == 0
        halo = jnp.where(first, 0.0, uh_ref[...])
        parts = _pool_deviation(u_ref[...], halo, i * tile)
        dout = d_ref[...]
        nxt = jnp.where(i == n_tiles - 1, 0.0, dh_ref[...])
        dext = jnp.concatenate([dout, nxt], axis=0).astype(BF16)
        counts = _pool_counts(i * tile, tile + HALO)
        dps, scaled = [], []
        for g in range(ng):
            lanes = slice(g * POOL_GROUP, (g + 1) * POOL_GROUP)
            dp = _dot_nt(dext[:, lanes], wp_ref[g].astype(BF16))
            dps.append(dp[:tile])
            scaled.append(dp / counts[g])
        sums = _window_sums(jnp.concatenate(scaled, axis=1), forward=True)
        for g, w in enumerate(POOL_WINDOWS):
            lanes = slice(g * POOL_GROUP, (g + 1) * POOL_GROUP)
            du_ref[:, lanes] = sums[w][:tile, lanes] - dps[g]
            _accumulate(dwp_ref.at[g], _dot_tn(parts[g].astype(BF16), dext[:tile, lanes]), first)

    row = pl.BlockSpec((tile, C), lambda i: (i, 0))
    return pl.pallas_call(
        body, name="bwd_pool", grid=(n_tiles,),
        in_specs=[row, _prev_halo_spec(tile, C), row, _next_halo_spec(tile, C, n_tiles), _const(w_pool.shape)],
        out_specs=[row, pl.BlockSpec(w_pool.shape, lambda i: (0, 0, 0))],
        out_shape=[jax.ShapeDtypeStruct((S, C), F32), jax.ShapeDtypeStruct(w_pool.shape, F32)],
        compiler_params=_params("arbitrary"),
    )(u, u, d_pool, d_pool, w_pool)


def _bwd_inproj(du, dq, dk, dv, w_in_g, h1, x, dx2, g1, tile):
    S, D = x.shape
    nb, _, cs = w_in_g.shape
    C = du.shape[1]
    per = C // cs

    def body(du_ref, dq_ref, dk_ref, dv_ref, w_ref, h_ref, x_ref, dx2_ref, g_ref, dx_ref, dw_ref, dg_ref):
        first = pl.program_id(0) == 0
        h = h_ref[...]
        dh = jnp.zeros((tile, D), F32)
        for d in range(nb):
            src = (du_ref, dq_ref, dk_ref, dv_ref)[d // per]
            dproj = src[:, (d % per) * cs:(d % per + 1) * cs].astype(BF16)
            dh = dh + _dot_nt(dproj, w_ref[d])
            _accumulate(dw_ref.at[d], _dot_tn(h, dproj), first)
        xf = x_ref[...]
        r1 = _rms(xf)
        n1 = xf * r1
        _accumulate(dg_ref, _colsum(dh * n1), first)
        dx_ref[...] = dx2_ref[...] + _norm_bwd(dh * g_ref[...], n1, r1)

    row = lambda w: pl.BlockSpec((tile, w), lambda i: (i, 0))
    return pl.pallas_call(
        body, name="bwd_inproj", grid=(S // tile,),
        in_specs=[row(C), row(C), row(C), row(C), _const(w_in_g.shape), row(D), row(D), row(D), _const((1, D))],
        out_specs=[row(D), pl.BlockSpec(w_in_g.shape, lambda i: (0, 0, 0)), pl.BlockSpec((1, D), lambda i: (0, 0))],
        out_shape=[jax.ShapeDtypeStruct((S, D), F32), jax.ShapeDtypeStruct(w_in_g.shape, F32), jax.ShapeDtypeStruct((1, D), F32)],
        compiler_params=_params("arbitrary"),
    )(du, dq, dk, dv, w_in_g, h1, x, dx2, g1)


def _local_step(x, target, g1, w_in_g, w_pool, pool_scale, attn_scale, w_out, g2, g3, w_up_g, conv_w_g, conv_b_g,
                w_down4, g4):
    S = x.shape[0]
    big = min(512, S)
    small = min(256, S)
    n_pairs = pool_scale.shape[1] // QB

    h1, u, qkv = _fwd_inproj(x, g1, w_in_g, big)
    pool_out = _fwd_pool(u, w_pool, big)
    attn_out = _fwd_attn(qkv, n_pairs)
    mix, x2, h2 = _fwd_outproj(pool_out, attn_out, pool_scale, attn_scale, w_out, x, g2, g3, big)
    upre = _fwd_up(h2, w_up_g, big)
    dy, df, loss_cols, dg4 = _fwd_ffn_loss(upre, conv_w_g, conv_b_g, w_down4, x2, target, g4, small)

    dgate, dval, d_wd4, d_cb, d_cw = _bwd_down(upre, conv_w_g, conv_b_g, w_down4, df, big)
    dupre, dx2, dmix, dg3, dg2 = _bwd_up_x(dgate, dval, conv_w_g, w_up_g, x2, dy, mix, g2, g3, small)
    d_wup = _bwd_weight(h2, dupre, big)
    d_pool, d_attn, d_wout, d_ps, d_as = _bwd_outproj(dmix, w_out, pool_out, attn_out, pool_scale, attn_scale, big)
    dq, dk, dv = _bwd_attn(qkv, attn_out, d_attn, n_pairs)
    du, d_wp = _bwd_pool(u, d_pool, w_pool, big)
    dx, d_win, dg1 = _bwd_inproj(du, dq, dk, dv, w_in_g, h1, x, dx2, g1, big)
    return dict(loss_cols=loss_cols, dx=dx, g1=dg1, w_in=d_win, w_pool=d_wp, pool_scale=d_ps, attn_scale=d_as,
                w_out=d_wout, g2=dg2, g3=dg3, w_up=d_wup, conv_w=d_cw, conv_b=d_cb, w_down=d_wd4, g4=dg4)


def _mesh_position():
    x, y, c = lax.axis_index("x"), lax.axis_index("y"), lax.axis_index("c")
    return x, y, c, 4 * x + 2 * y + c


def _peer(x, y, c, k):
    px = 1 - x if k & 4 else x
    py = 1 - y if k & 2 else y
    pc = 1 - c if k & 1 else c
    return (px, py, pc), 4 * px + 2 * py + pc


def _all_to_all(arrays, gather, name):
    n = len(arrays)
    out_shapes = [jax.ShapeDtypeStruct(((N_DEV,) + a.shape) if gather else a.shape, a.dtype) for a in arrays]

    def body(*refs):
        ins, outs = refs[:n], refs[n:2 * n]
        send_sems, recv_sems, local_sems = refs[2 * n:]
        x, y, c, me = _mesh_position()
        local, remote = [], []
        for a in range(n):
            mine = ins[a] if gather else ins[a].at[me]
            cp = pltpu.make_async_copy(mine, outs[a].at[me], local_sems.at[a])
            cp.start()
            local.append(cp)
            for k in range(1, N_DEV):
                peer, peer_idx = _peer(x, y, c, k)
                src = ins[a] if gather else ins[a].at[peer_idx]
                sem = a * (N_DEV - 1) + k - 1
                cp = pltpu.make_async_remote_copy(src_ref=src, dst_ref=outs[a].at[me], send_sem=send_sems.at[sem],
                                                  recv_sem=recv_sems.at[sem], device_id=peer, device_id_type=MESH)
                cp.start()
                remote.append(cp)
        for cp in remote:
            cp.wait_send()
        for cp in remote:
            cp.wait_recv()
        for cp in local:
            cp.wait()

    any_spec = pl.BlockSpec(memory_space=pl.ANY)
    return pl.pallas_call(
        body, name=name,
        in_specs=[any_spec] * n, out_specs=[any_spec] * n, out_shape=out_shapes,
        scratch_shapes=[pltpu.SemaphoreType.DMA((n * (N_DEV - 1),)), pltpu.SemaphoreType.DMA((n * (N_DEV - 1),)),
                        pltpu.SemaphoreType.DMA((n,))],
    )(*arrays)


def _reduce_adamw(parts, w, m, v, rows):
    R, C = w.shape

    def body(p_ref, w_ref, m_ref, v_ref, g_ref, d_ref, nm_ref, nv_ref):
        g = p_ref[0].astype(F32)
        for s in range(1, N_DEV):
            g = g + p_ref[s].astype(F32)
        g_ref[...] = g
        m_new = ADAM_B1 * m_ref[...] + (1.0 - ADAM_B1) * g
        v_new = ADAM_B2 * v_ref[...] + (1.0 - ADAM_B2) * (g * g)
        m_hat = m_new / (1.0 - ADAM_B1 ** ADAM_STEP)
        v_hat = v_new / (1.0 - ADAM_B2 ** ADAM_STEP)
        d_ref[...] = -ADAM_LR * (m_hat / (jnp.sqrt(v_hat) + ADAM_EPS) + ADAM_WD * w_ref[...])
        nm_ref[...] = m_new
        nv_ref[...] = v_new

    row = pl.BlockSpec((rows, C), lambda i: (i, 0))
    return pl.pallas_call(
        body, name="reduce_adamw", grid=(R // rows,),
        in_specs=[pl.BlockSpec((N_DEV, rows, C), lambda i: (0, i, 0)), row, row, row],
        out_specs=[row] * 4, out_shape=[jax.ShapeDtypeStruct((R, C), F32)] * 4,
        compiler_params=_params("parallel"),
    )(parts, w, m, v)


def _row_tile(rows, cols):
    fits = [t for t in range(8, rows + 1, 8) if rows % t == 0 and N_DEV * t * cols * 4 <= 4 * 1024 * 1024]
    return max(fits) if fits else rows


SMALL_COLS = 1024


def _pack_small(vals):
    rows = []
    for a in vals:
        flat = a.reshape(-1)
        pad = (-flat.shape[0]) % SMALL_COLS
        rows.append(jnp.pad(flat, (0, pad)).reshape(-1, SMALL_COLS))
    packed = jnp.concatenate(rows, axis=0)
    return jnp.pad(packed, ((0, (-packed.shape[0]) % 8), (0, 0)))


def _unpack_small(packed, like):
    out, r = [], 0
    for a in like:
        n = a.size
        nr = -(-n // SMALL_COLS)
        out.append(packed[r:r + nr].reshape(-1)[:n].reshape(a.shape))
        r += nr
    return out


def kernel(x, norm_mix_pre, w_in, w_pool, pool_scale, attn_scale, w_out, norm_mix_post, norm_ffn_pre, w_up, conv_w, conv_b, w_down, norm_ffn_post, loss_target, m_norm_mix_pre, m_w_in, m_w_pool, m_pool_scale, m_attn_scale, m_w_out, m_norm_mix_post, m_norm_ffn_pre, m_w_up, m_conv_w, m_conv_b, m_w_down, m_norm_ffn_post, v_norm_mix_pre, v_w_in, v_w_pool, v_pool_scale, v_attn_scale, v_w_out, v_norm_mix_post, v_norm_ffn_pre, v_w_up, v_conv_w, v_conv_b, v_w_down, v_norm_ffn_post):
    S, D = x.shape[1], x.shape[2]
    d_ff_block = w_up.shape[2]

    w_in_g, w_out_g, w_up_g, w_down_g, conv_w_g = _all_to_all(
        [w_in[0].astype(BF16), w_out[0].astype(BF16), w_up[0].astype(BF16), w_down[0].astype(BF16), conv_w[0]], gather=True, name="gather_weights")
    w_out_full = w_out_g.reshape(D, D)
    w_down4 = w_down_g.reshape(D_FF_SHARDS, d_ff_block, D)
    conv_b_g = conv_b.reshape(N_DEV, 1, d_ff_block)

    r = _local_step(x[0], loss_target[0], norm_mix_pre, w_in_g, w_pool[0], pool_scale, attn_scale, w_out_full,
                    norm_mix_post, norm_ffn_pre, w_up_g, conv_w_g, conv_b_g, w_down4, norm_ffn_post)
    loss = lax.psum(0.5 * jnp.sum(r["loss_cols"]) / D, ("x", "y", "c"))

    small_names = ["norm_mix_pre", "w_pool", "pool_scale", "attn_scale", "norm_mix_post", "norm_ffn_pre", "conv_b", "norm_ffn_post"]
    small_w = dict(norm_mix_pre=norm_mix_pre, w_pool=w_pool, pool_scale=pool_scale, attn_scale=attn_scale,
                   norm_mix_post=norm_mix_post, norm_ffn_pre=norm_ffn_pre, conv_b=conv_b, norm_ffn_post=norm_ffn_post)
    small_m = dict(norm_mix_pre=m_norm_mix_pre, w_pool=m_w_pool, pool_scale=m_pool_scale, attn_scale=m_attn_scale,
                   norm_mix_post=m_norm_mix_post, norm_ffn_pre=m_norm_ffn_pre, conv_b=m_conv_b, norm_ffn_post=m_norm_ffn_post)
    small_v = dict(norm_mix_pre=v_norm_mix_pre, w_pool=v_w_pool, pool_scale=v_pool_scale, attn_scale=v_attn_scale,
                   norm_mix_post=v_norm_mix_post, norm_ffn_pre=v_norm_ffn_pre, conv_b=v_conv_b, norm_ffn_post=v_norm_ffn_post)
    small_g = dict(norm_mix_pre=r["g1"], w_pool=r["w_pool"], pool_scale=r["pool_scale"], attn_scale=r["attn_scale"],
                   norm_mix_post=r["g2"], norm_ffn_pre=r["g3"], conv_b=r["conv_b"], norm_ffn_post=r["g4"])
    like = [small_w[n] for n in small_names]
    packed_g = _pack_small([small_g[n] for n in small_names])

    d_wout_g = r["w_out"].reshape(N_DEV, D // N_DEV, D)
    d_wdown_g = r["w_down"].reshape(N_DEV, w_down.shape[1], D)
    (small_parts,) = _all_to_all([packed_g], gather=True, name="gather_small_grads")
    big_parts = _all_to_all([r["w_in"], d_wout_g, r["w_up"], d_wdown_g, r["conv_w"]], gather=False, name="exchange_grads")

    def update(parts, w, m, v):
        R, C = w.shape
        return _reduce_adamw(parts, w, m, v, _row_tile(R, C))

    res = {}
    res["w_in"] = update(big_parts[0], w_in[0], m_w_in[0], v_w_in[0])
    res["w_out"] = update(big_parts[1], w_out[0], m_w_out[0], v_w_out[0])
    res["w_up"] = update(big_parts[2], w_up[0], m_w_up[0], v_w_up[0])
    res["w_down"] = update(big_parts[3], w_down[0], m_w_down[0], v_w_down[0])
    res["conv_w"] = update(big_parts[4], conv_w[0], m_conv_w[0], v_conv_w[0])
    small_res = update(small_parts, _pack_small(like), _pack_small([small_m[n] for n in small_names]),
                       _pack_small([small_v[n] for n in small_names]))
    small_res = [_unpack_small(t, like) for t in small_res]
    for idx, n in enumerate(small_names):
        res[n] = tuple(t[idx] for t in small_res)

    order = ["norm_mix_pre", "w_in", "w_pool", "pool_scale", "attn_scale", "w_out", "norm_mix_post", "norm_ffn_pre",
             "w_up", "conv_w", "conv_b", "w_down", "norm_ffn_post"]
    shaped = {n: tuple(t.reshape(s.shape) for t in res[n])
              for n, s in dict(norm_mix_pre=norm_mix_pre, w_in=w_in, w_pool=w_pool, pool_scale=pool_scale, attn_scale=attn_scale,
                               w_out=w_out, norm_mix_post=norm_mix_post, norm_ffn_pre=norm_ffn_pre, w_up=w_up, conv_w=conv_w,
                               conv_b=conv_b, w_down=w_down, norm_ffn_post=norm_ffn_post).items()}
    outs = [loss, r["dx"].reshape(x.shape)]
    for k in range(4):
        outs += [shaped[n][k] for n in order]
    return tuple(outs)
```

```python
import functools

import jax
import jax.numpy as jnp
from jax import lax
from jax.experimental import pallas as pl
from jax.experimental.pallas import tpu as pltpu

F32 = jnp.float32
BF16 = jnp.bfloat16
HIGHEST = lax.Precision.HIGHEST

N_DEV = 8
EPS = 1e-6
POOL_WINDOWS = (2, 4, 8, 16)
POOL_GROUP = 128
HALO = 16
HEAD_DIM = 64
QB = 128
ATTN_SCALE = HEAD_DIM ** -0.5
ATTN_SUBS = 2
EXP_UNDERFLOW = -88.0
D_FF_SHARDS = 4

ADAM_LR = 0.001
ADAM_B1 = 0.9
ADAM_B2 = 0.999
ADAM_EPS = 1e-08
ADAM_WD = 0.01
ADAM_STEP = 10

VMEM_LIMIT_V7X = 56 * 1024 * 1024
MESH = pl.DeviceIdType.MESH


def _params(*semantics):
    return pltpu.CompilerParams(dimension_semantics=semantics, vmem_limit_bytes=VMEM_LIMIT_V7X)


def _const(shape):
    zeros = (0,) * len(shape)
    return pl.BlockSpec(shape, lambda *_: zeros, pipeline_mode=pl.Buffered(1))


def _dot(a, b):
    return jnp.dot(a, b, preferred_element_type=F32)


def _dot_nt(a, b):
    return lax.dot_general(a, b, (((1,), (1,)), ((), ())), preferred_element_type=F32)


def _dot_tn(a, b):
    return lax.dot_general(a, b, (((0,), (0,)), ((), ())), preferred_element_type=F32)


def _rms(v):
    return lax.rsqrt(jnp.mean(v * v, axis=-1, keepdims=True) + EPS)


def _norm_bwd(dn_times_gain, n, r):
    return r * (dn_times_gain - n * jnp.mean(dn_times_gain * n, axis=-1, keepdims=True))


def _accumulate(ref, value, first):
    @pl.when(first)
    def _():
        ref[...] = value

    @pl.when(jnp.logical_not(first))
    def _():
        ref[...] += value


def _colsum(v):
    return jnp.sum(v, axis=0, keepdims=True)


def _fwd_inproj(x, g1, w_in_g, tile):
    S, D = x.shape
    nb, _, cs = w_in_g.shape
    d_pool = 2 * cs

    def body(x_ref, g_ref, w_ref, h_ref, u_ref, qkv_ref):
        xf = x_ref[...]
        h = (xf * _rms(xf) * g_ref[...]).astype(BF16)
        h_ref[...] = h
        for d in range(nb):
            o = _dot(h, w_ref[d])
            if d < 2:
                u_ref[:, d * cs:(d + 1) * cs] = o
            else:
                qkv_ref[:, (d - 2) * cs:(d - 1) * cs] = o.astype(BF16)

    return pl.pallas_call(
        body, name="fwd_inproj", grid=(S // tile,),
        in_specs=[pl.BlockSpec((tile, D), lambda i: (i, 0)), _const((1, D)), _const(w_in_g.shape)],
        out_specs=[pl.BlockSpec((tile, D), lambda i: (i, 0)), pl.BlockSpec((tile, d_pool), lambda i: (i, 0)),
                   pl.BlockSpec((tile, 3 * d_pool), lambda i: (i, 0))],
        out_shape=[jax.ShapeDtypeStruct((S, D), BF16), jax.ShapeDtypeStruct((S, d_pool), F32),
                   jax.ShapeDtypeStruct((S, 3 * d_pool), BF16)],
        compiler_params=_params("parallel"),
    )(x, g1, w_in_g)


def _window_sums(ext, forward):
    n = ext.shape[0]
    sums, s, sh = {}, ext, 1
    while sh < POOL_WINDOWS[-1]:
        s = s + pltpu.roll(s, (n - sh) if forward else sh, axis=0)
        sh *= 2
        sums[sh] = s
    return sums


def _pool_counts(t0, rows):
    t1 = (lax.broadcasted_iota(jnp.int32, (rows, 1), 0) + t0 + 1).astype(F32)
    return [jnp.minimum(t1, float(w)) for w in POOL_WINDOWS]


def _pool_deviation(u, halo, t0):
    T = u.shape[0]
    sums = _window_sums(jnp.concatenate([halo, u], axis=0), forward=False)
    counts = _pool_counts(t0, T)
    parts = []
    for g, w in enumerate(POOL_WINDOWS):
        lanes = slice(g * POOL_GROUP, (g + 1) * POOL_GROUP)
        parts.append(sums[w][HALO:, lanes] / counts[g] - u[:, lanes])
    return parts


def _prev_halo_spec(tile, width):
    return pl.BlockSpec((HALO, width), lambda i: (jnp.maximum(i * (tile // HALO) - 1, 0), 0))


def _next_halo_spec(tile, width, n_tiles):
    last = n_tiles * (tile // HALO) - 1
    return pl.BlockSpec((HALO, width), lambda i: (jnp.minimum((i + 1) * (tile // HALO), last), 0))


def _fwd_pool(u, w_pool, tile):
    S, C = u.shape

    def body(u_ref, halo_ref, wp_ref, o_ref):
        i = pl.program_id(0)
        halo = jnp.where(i > 0, halo_ref[...], 0.0)
        parts = _pool_deviation(u_ref[...], halo, i * tile)
        for g, p in enumerate(parts):
            o_ref[:, g * POOL_GROUP:(g + 1) * POOL_GROUP] = _dot(p.astype(BF16), wp_ref[g].astype(BF16))

    return pl.pallas_call(
        body, name="fwd_pool", grid=(S // tile,),
        in_specs=[pl.BlockSpec((tile, C), lambda i: (i, 0)), _prev_halo_spec(tile, C), _const(w_pool.shape)],
        out_specs=pl.BlockSpec((tile, C), lambda i: (i, 0)),
        out_shape=jax.ShapeDtypeStruct((S, C), F32),
        compiler_params=_params("parallel"),
    )(u, u, w_pool)


def _low_lanes():
    return lax.broadcasted_iota(jnp.int32, (QB, 2 * HEAD_DIM), 1) < HEAD_DIM


def _triangle(inclusive):
    row = lax.broadcasted_iota(jnp.int32, (QB, QB), 0)
    col = lax.broadcasted_iota(jnp.int32, (QB, QB), 1)
    return ((row >= col) if inclusive else (row > col)).astype(F32)


def _causal_mask(width, offset):
    row = lax.broadcasted_iota(jnp.int32, (QB, width), 0)
    col = lax.broadcasted_iota(jnp.int32, (QB, width), 1)
    return col < row + offset


def _suffix_sums(vals, tri, carry):
    n = vals.shape[1] // QB
    out, run = [None] * n, carry
    for b in reversed(range(n)):
        blk = vals[:, b * QB:(b + 1) * QB]
        out[b] = jnp.dot(blk, tri, precision=HIGHEST, preferred_element_type=F32) + run
        run = run + jnp.sum(blk, axis=1, keepdims=True)
    return (out[0] if n == 1 else jnp.concatenate(out, axis=1)), run


def _attn_tile(qh, kw, mask, carry, after_s):
    z = _dot_nt(qh, kw) * ATTN_SCALE
    e = jnp.exp(-jnp.abs(z))
    softplus = jnp.maximum(z, 0.0) + jnp.log(1.0 + e)
    log_1m_beta = -softplus if mask is None else jnp.where(mask, -softplus, 0.0)
    stick, carry = _suffix_sums(log_1m_beta, after_s, carry)
    a = jnp.exp(z - softplus + stick)
    if mask is not None:
        a = jnp.where(mask, a, 0.0)
    return z, e, a, carry


def _split_heads(v, low_lanes):
    return jnp.where(low_lanes, v, 0.0).astype(BF16), jnp.where(low_lanes, 0.0, v).astype(BF16)


def _sweep_done(c0, c1):
    return (jnp.maximum(jnp.max(c0), jnp.max(c1)) < EXP_UNDERFLOW).astype(jnp.int32)


def _first_window(i):
    first_blk = jnp.maximum(i - 1, 0)
    return first_blk, pl.multiple_of(first_blk * QB, QB), (i - first_blk) * QB


def _fwd_attn(qkv, n_pairs):
    S = qkv.shape[0]
    n_steps = S // (ATTN_SUBS * QB)

    def body(q_ref, k_ref, v_ref, o_ref):
        low_lanes = _low_lanes()
        after_s = _triangle(False)
        zero = jnp.zeros((QB, 1), F32)

        def cond(c):
            return jnp.logical_and(c[0] >= 0, c[1] == 0)

        started = []
        for sub in range(ATTN_SUBS):
            i = pl.program_id(1) * ATTN_SUBS + sub
            qh = _split_heads(q_ref[sub * QB:(sub + 1) * QB, :].astype(F32), low_lanes)
            first_blk, start, offset = _first_window(i)
            kw = k_ref[pl.ds(start, 2 * QB), :]
            vw = v_ref[pl.ds(start, 2 * QB), :]
            mask = _causal_mask(2 * QB, offset)
            outs, carries = [], []
            for h in range(2):
                _, _, a, c = _attn_tile(qh[h], kw, mask, zero, after_s)
                outs.append(_dot(a.astype(BF16), vw))
                carries.append(c)
            started.append((qh, first_blk, jnp.where(low_lanes, outs[0], outs[1]), carries))

        for sub, (qh, first_blk, acc, carries) in enumerate(started):
            def step(c, qh=qh):
                j, _, acc, c0, c1 = c
                at = pl.multiple_of(j * QB, QB)
                kb = k_ref[pl.ds(at, QB), :]
                vb = v_ref[pl.ds(at, QB), :]
                cs, outs = [c0, c1], []
                for h in range(2):
                    _, _, a, cs[h] = _attn_tile(qh[h], kb, None, cs[h], after_s)
                    outs.append(_dot(a.astype(BF16), vb))
                acc = acc + jnp.where(low_lanes, outs[0], outs[1])
                return j - 1, _sweep_done(*cs), acc, cs[0], cs[1]

            init = (first_blk - 1, _sweep_done(*carries), acc, carries[0], carries[1])
            o_ref[sub * QB:(sub + 1) * QB, :] = lax.while_loop(cond, step, init)[2]

    return pl.pallas_call(
        body, name="fwd_attn", grid=(n_pairs, n_steps),
        in_specs=[pl.BlockSpec((ATTN_SUBS * QB, QB), lambda p, i: (i, p)),
                  pl.BlockSpec((S, QB), lambda p, i: (0, n_pairs + p), pipeline_mode=pl.Buffered(1)),
                  pl.BlockSpec((S, QB), lambda p, i: (0, 2 * n_pairs + p), pipeline_mode=pl.Buffered(1))],
        out_specs=pl.BlockSpec((ATTN_SUBS * QB, QB), lambda p, i: (i, p)),
        out_shape=jax.ShapeDtypeStruct((S, n_pairs * QB), F32),
        compiler_params=_params("parallel", "parallel"),
    )(qkv, qkv, qkv)


def _normalized_heads(pool_out, attn_out):
    rp, ra = _rms(pool_out), _rms(attn_out)
    return pool_out * rp, rp, attn_out * ra, ra


def _fwd_outproj(pool_out, attn_out, pool_scale, attn_scale, w_out, x, g2, g3, tile):
    S, D = x.shape
    C = pool_out.shape[1]

    def body(p_ref, a_ref, ps_ref, as_ref, w_ref, x_ref, g2_ref, g3_ref, mix_ref, x2_ref, h2_ref):
        n_p, _, n_a, _ = _normalized_heads(p_ref[...], a_ref[...])
        mix = _dot((n_p * ps_ref[...]).astype(BF16), w_ref[:C, :]) + _dot((n_a * as_ref[...]).astype(BF16), w_ref[C:, :])
        mix_ref[...] = mix
        x2 = x_ref[...] + mix * _rms(mix) * g2_ref[...]
        x2_ref[...] = x2
        h2_ref[...] = (x2 * _rms(x2) * g3_ref[...]).astype(BF16)

    row = lambda w: pl.BlockSpec((tile, w), lambda i: (i, 0))
    return pl.pallas_call(
        body, name="fwd_outproj", grid=(S // tile,),
        in_specs=[row(C), row(C), _const((1, C)), _const((1, C)), _const(w_out.shape), row(D), _const((1, D)), _const((1, D))],
        out_specs=[row(D), row(D), row(D)],
        out_shape=[jax.ShapeDtypeStruct((S, D), F32), jax.ShapeDtypeStruct((S, D), F32), jax.ShapeDtypeStruct((S, D), BF16)],
        compiler_params=_params("parallel"),
    )(pool_out, attn_out, pool_scale, attn_scale, w_out, x, g2, g3)


def _fwd_up(h2, w_up_g, tile):
    S, D = h2.shape
    nb, _, cs = w_up_g.shape

    def body(h_ref, w_ref, o_ref):
        h = h_ref[...]
        for d in range(nb):
            o_ref[d] = _dot(h, w_ref[d]).astype(BF16)

    return pl.pallas_call(
        body, name="fwd_up", grid=(S // tile,),
        in_specs=[pl.BlockSpec((tile, D), lambda i: (i, 0)), _const(w_up_g.shape)],
        out_specs=pl.BlockSpec((nb, tile, cs), lambda i: (0, i, 0)),
        out_shape=jax.ShapeDtypeStruct((nb, S, cs), BF16),
        compiler_params=_params("parallel"),
    )(h2, w_up_g)


def _conv_taps(tile_rows, halo_rows):
    T = tile_rows.shape[0]
    ext = jnp.concatenate([halo_rows.astype(F32), tile_rows.astype(F32)], axis=0)
    return pltpu.roll(ext, 2, axis=0)[HALO:], pltpu.roll(ext, 1, axis=0)[HALO:], ext[HALO:]


def _tap_rows(cw_ref, d):
    return [cw_ref[d, k:k + 1, :] for k in range(3)]


def _gated_unit(taps_gate, taps_val, cw_gate, cw_val, cb_gate, cb_val):
    gate = cw_gate[0] * taps_gate[0] + cw_gate[1] * taps_gate[1] + cw_gate[2] * taps_gate[2] + cb_gate
    val = cw_val[0] * taps_val[0] + cw_val[1] * taps_val[1] + cw_val[2] * taps_val[2] + cb_val
    sig = 1.0 / (1.0 + jnp.exp(-gate))
    return gate, val, sig


def _fwd_ffn_loss(upre, conv_w_g, conv_b_g, w_down4, x2, target, g4, tile):
    nb, S, cs = upre.shape
    D = x2.shape[1]

    def body(u_ref, halo_ref, cw_ref, cb_ref, wd_ref, x2_ref, t_ref, g4_ref, dy_ref, df_ref, loss_ref, dg4_ref):
        i = pl.program_id(0)
        first = i == 0
        f = jnp.zeros((tile, D), F32)
        for s in range(D_FF_SHARDS):
            halo_g = jnp.where(first, jnp.zeros_like(halo_ref[s]), halo_ref[s])
            halo_v = jnp.where(first, jnp.zeros_like(halo_ref[s]), halo_ref[s + D_FF_SHARDS])
            gate, val, sig = _gated_unit(_conv_taps(u_ref[s], halo_g), _conv_taps(u_ref[s + D_FF_SHARDS], halo_v),
                                         _tap_rows(cw_ref, s), _tap_rows(cw_ref, s + D_FF_SHARDS), cb_ref[s], cb_ref[s + D_FF_SHARDS])
            f = f + _dot((gate * sig * val).astype(BF16), wd_ref[s])
        r4 = _rms(f)
        n4 = f * r4
        err = x2_ref[...] + n4 * g4_ref[...] - t_ref[...]
        dy = err * (1.0 / D)
        dy_ref[...] = dy
        df_ref[...] = _norm_bwd(dy * g4_ref[...], n4, r4).astype(BF16)
        _accumulate(loss_ref, _colsum(err * err), first)
        _accumulate(dg4_ref, _colsum(dy * n4), first)

    row = lambda w: pl.BlockSpec((tile, w), lambda i: (i, 0))
    return pl.pallas_call(
        body, name="fwd_ffn_loss", grid=(S // tile,),
        in_specs=[pl.BlockSpec((nb, tile, cs), lambda i: (0, i, 0)),
                  pl.BlockSpec((nb, HALO, cs), lambda i: (0, jnp.maximum(i * (tile // HALO) - 1, 0), 0)),
                  _const(conv_w_g.shape), _const(conv_b_g.shape), _const(w_down4.shape), row(D), row(D), _const((1, D))],
        out_specs=[row(D), row(D), pl.BlockSpec((1, D), lambda i: (0, 0)), pl.BlockSpec((1, D), lambda i: (0, 0))],
        out_shape=[jax.ShapeDtypeStruct((S, D), F32), jax.ShapeDtypeStruct((S, D), BF16),
                   jax.ShapeDtypeStruct((1, D), F32), jax.ShapeDtypeStruct((1, D), F32)],
        compiler_params=_params("arbitrary"),
    )(upre, upre, conv_w_g, conv_b_g, w_down4, x2, target, g4)


def _bwd_down(upre, conv_w_g, conv_b_g, w_down4, df, tile):
    nb, S, cs = upre.shape
    D = df.shape[1]
    n_tiles = S // tile

    def body(ug_ref, uv_ref, hg_ref, hv_ref, cwg_ref, cwv_ref, cbg_ref, cbv_ref, wd_ref, df_ref,
             dg_ref, dv_ref, dwd_ref, dbg_ref, dbv_ref, dcwg_ref, dcwv_ref):
        i = pl.program_id(1)
        first = i == 0
        halo_g = jnp.where(first, jnp.zeros_like(hg_ref[0]), hg_ref[0])
        halo_v = jnp.where(first, jnp.zeros_like(hv_ref[0]), hv_ref[0])
        taps_g, taps_v = _conv_taps(ug_ref[0], halo_g), _conv_taps(uv_ref[0], halo_v)
        gate, val, sig = _gated_unit(taps_g, taps_v, _tap_rows(cwg_ref, 0), _tap_rows(cwv_ref, 0), cbg_ref[0], cbv_ref[0])
        silu = gate * sig
        dfb = df_ref[...]
        dact = _dot_nt(dfb, wd_ref[0])
        _accumulate(dwd_ref.at[0], _dot_tn((silu * val).astype(BF16), dfb), first)
        dgate = dact * val * (sig * (1.0 + gate * (1.0 - sig)))
        dval = dact * silu
        dg_ref[0] = dgate.astype(BF16)
        dv_ref[0] = dval.astype(BF16)
        _accumulate(dbg_ref.at[0], _colsum(dgate), first)
        _accumulate(dbv_ref.at[0], _colsum(dval), first)
        _accumulate(dcwg_ref.at[0], jnp.concatenate([_colsum(dgate * t) for t in taps_g], axis=0), first)
        _accumulate(dcwv_ref.at[0], jnp.concatenate([_colsum(dval * t) for t in taps_v], axis=0), first)

    half = D_FF_SHARDS
    blk = lambda off: pl.BlockSpec((1, tile, cs), lambda s, i: (s + off, i, 0))
    halo = lambda off: pl.BlockSpec((1, HALO, cs), lambda s, i: (s + off, jnp.maximum(i * (tile // HALO) - 1, 0), 0))
    par = lambda off, r: pl.BlockSpec((1, r, cs), lambda s, i: (s + off, 0, 0))
    outs = pl.pallas_call(
        body, name="bwd_down", grid=(half, n_tiles),
        in_specs=[blk(0), blk(half), halo(0), halo(half), par(0, 3), par(half, 3), par(0, 1), par(half, 1),
                  pl.BlockSpec((1, cs, D), lambda s, i: (s, 0, 0)), pl.BlockSpec((tile, D), lambda s, i: (i, 0))],
        out_specs=[blk(0), blk(0), pl.BlockSpec((1, cs, D), lambda s, i: (s, 0, 0)),
                   par(0, 1), par(0, 1), par(0, 3), par(0, 3)],
        out_shape=[jax.ShapeDtypeStruct((half, S, cs), BF16), jax.ShapeDtypeStruct((half, S, cs), BF16),
                   jax.ShapeDtypeStruct((half, cs, D), F32),
                   jax.ShapeDtypeStruct((half, 1, cs), F32), jax.ShapeDtypeStruct((half, 1, cs), F32),
                   jax.ShapeDtypeStruct((half, 3, cs), F32), jax.ShapeDtypeStruct((half, 3, cs), F32)],
        compiler_params=_params("parallel", "arbitrary"),
    )(upre, upre, upre, upre, conv_w_g, conv_w_g, conv_b_g, conv_b_g, w_down4, df)
    dgate, dval, d_wd, dbg, dbv, dcwg, dcwv = outs
    return dgate, dval, d_wd, jnp.concatenate([dbg, dbv], axis=0), jnp.concatenate([dcwg, dcwv], axis=0)


def _bwd_up_x(dgate, dval, conv_w_g, w_up_g, x2, dy, mix, g2, g3, tile):
    half, S, cs = dgate.shape
    nb = 2 * half
    D = x2.shape[1]
    n_tiles = S // tile

    def body(dg_ref, dv_ref, hg_ref, hv_ref, cw_ref, w_ref, x2_ref, dy_ref, mix_ref, g2_ref, g3_ref,
             dupre_ref, dx2_ref, dmix_ref, dg3_ref, dg2_ref):
        i = pl.program_id(0)
        first = i == 0
        last = i == n_tiles - 1
        dh2 = jnp.zeros((tile, D), F32)
        for d in range(nb):
            src, halo = (dg_ref, hg_ref) if d < half else (dv_ref, hv_ref)
            nxt = jnp.where(last, jnp.zeros_like(halo[d % half]), halo[d % half])
            ext = jnp.concatenate([src[d % half].astype(F32), nxt.astype(F32)], axis=0)
            n = ext.shape[0]
            cw = _tap_rows(cw_ref, d)
            dupre = (cw[2] * ext + cw[1] * pltpu.roll(ext, n - 1, axis=0) + cw[0] * pltpu.roll(ext, n - 2, axis=0))[:tile]
            dupre = dupre.astype(BF16)
            dupre_ref[d] = dupre
            dh2 = dh2 + _dot_nt(dupre, w_ref[d])
        x2 = x2_ref[...]
        r3 = _rms(x2)
        n3 = x2 * r3
        _accumulate(dg3_ref, _colsum(dh2 * n3), first)
        dx2 = dy_ref[...] + _norm_bwd(dh2 * g3_ref[...], n3, r3)
        dx2_ref[...] = dx2
        mix = mix_ref[...]
        r2 = _rms(mix)
        n2 = mix * r2
        _accumulate(dg2_ref, _colsum(dx2 * n2), first)
        dmix_ref[...] = _norm_bwd(dx2 * g2_ref[...], n2, r2).astype(BF16)

    row = lambda w: pl.BlockSpec((tile, w), lambda i: (i, 0))
    blk = pl.BlockSpec((half, tile, cs), lambda i: (0, i, 0))
    last_halo = n_tiles * (tile // HALO) - 1
    halo = pl.BlockSpec((half, HALO, cs), lambda i: (0, jnp.minimum((i + 1) * (tile // HALO), last_halo), 0))
    acc = pl.BlockSpec((1, D), lambda i: (0, 0))
    return pl.pallas_call(
        body, name="bwd_up_x", grid=(n_tiles,),
        in_specs=[blk, blk, halo, halo, _const(conv_w_g.shape), _const(w_up_g.shape), row(D), row(D), row(D),
                  _const((1, D)), _const((1, D))],
        out_specs=[pl.BlockSpec((nb, tile, cs), lambda i: (0, i, 0)), row(D), row(D), acc, acc],
        out_shape=[jax.ShapeDtypeStruct((nb, S, cs), BF16), jax.ShapeDtypeStruct((S, D), F32),
                   jax.ShapeDtypeStruct((S, D), BF16), jax.ShapeDtypeStruct((1, D), F32), jax.ShapeDtypeStruct((1, D), F32)],
        compiler_params=_params("arbitrary"),
    )(dgate, dval, dgate, dval, conv_w_g, w_up_g, x2, dy, mix, g2, g3)


def _bwd_weight(act, dout, tile):
    S, D = act.shape
    nb, _, cs = dout.shape

    def body(a_ref, d_ref, o_ref):
        _accumulate(o_ref.at[0], _dot_tn(a_ref[...], d_ref[0]), pl.program_id(1) == 0)

    return pl.pallas_call(
        body, name="bwd_w_up", grid=(nb, S // tile),
        in_specs=[pl.BlockSpec((tile, D), lambda d, i: (i, 0)), pl.BlockSpec((1, tile, cs), lambda d, i: (d, i, 0))],
        out_specs=pl.BlockSpec((1, D, cs), lambda d, i: (d, 0, 0)),
        out_shape=jax.ShapeDtypeStruct((nb, D, cs), F32),
        compiler_params=_params("parallel", "arbitrary"),
    )(act, dout)


def _bwd_outproj(dmix, w_out, pool_out, attn_out, pool_scale, attn_scale, tile):
    S, D = dmix.shape
    C = pool_out.shape[1]

    def body(dm_ref, w_ref, p_ref, a_ref, ps_ref, as_ref, dp_ref, da_ref, dw_ref, dps_ref, das_ref):
        first = pl.program_id(0) == 0
        dmx = dm_ref[...]
        dmerged = _dot_nt(dmx, w_ref[...])
        n_p, r_p, n_a, r_a = _normalized_heads(p_ref[...], a_ref[...])
        merged = jnp.concatenate([(n_p * ps_ref[...]).astype(BF16), (n_a * as_ref[...]).astype(BF16)], axis=1)
        _accumulate(dw_ref, _dot_tn(merged, dmx), first)
        dm_p, dm_a = dmerged[:, :C], dmerged[:, C:]
        _accumulate(dps_ref, _colsum(dm_p * n_p), first)
        _accumulate(das_ref, _colsum(dm_a * n_a), first)
        dp_ref[...] = _norm_bwd(dm_p * ps_ref[...], n_p, r_p)
        da_ref[...] = _norm_bwd(dm_a * as_ref[...], n_a, r_a)

    row = lambda w: pl.BlockSpec((tile, w), lambda i: (i, 0))
    return pl.pallas_call(
        body, name="bwd_outproj", grid=(S // tile,),
        in_specs=[row(D), _const(w_out.shape), row(C), row(C), _const((1, C)), _const((1, C))],
        out_specs=[row(C), row(C), pl.BlockSpec(w_out.shape, lambda i: (0, 0)),
                   pl.BlockSpec((1, C), lambda i: (0, 0)), pl.BlockSpec((1, C), lambda i: (0, 0))],
        out_shape=[jax.ShapeDtypeStruct((S, C), F32), jax.ShapeDtypeStruct((S, C), F32),
                   jax.ShapeDtypeStruct(w_out.shape, F32), jax.ShapeDtypeStruct((1, C), F32), jax.ShapeDtypeStruct((1, C), F32)],
        compiler_params=_params("arbitrary"),
    )(dmix, w_out, pool_out, attn_out, pool_scale, attn_scale)


def _bwd_attn(qkv, d_attn, n_pairs):
    S = qkv.shape[0]
    n_steps = S // (ATTN_SUBS * QB)

    def body(q_ref, k_ref, v_ref, do_ref, dq_ref, dk_ref, dv_ref):
        @pl.when(pl.program_id(1) == 0)
        def _():
            dk_ref[...] = jnp.zeros_like(dk_ref)
            dv_ref[...] = jnp.zeros_like(dv_ref)

        low_lanes = _low_lanes()
        after_s, from_s = _triangle(False), _triangle(True)
        zero = jnp.zeros((QB, 1), F32)

        def tile(qh, doh, total, kw, vw, mask, cs, gs):
            first = mask is not None
            dqs = []
            dk = jnp.zeros((kw.shape[0], QB), F32)
            dv = jnp.zeros((kw.shape[0], QB), F32)
            for h in range(2):
                z, e, a, cs[h] = _attn_tile(qh[h], kw, mask, cs[h], after_s)
                g = a * _dot_nt(doh[h], vw)
                nearer, gs[h] = _suffix_sums(g, from_s, gs[h])
                if first:
                    total[h] = total[h] + gs[h]
                beyond = total[h] - nearer
                inv = 1.0 / (1.0 + e)
                sig_abs, sig_neg = inv, e * inv
                pos = z >= 0.0
                sig_z = jnp.where(pos, sig_abs, sig_neg)
                sig_mz = jnp.where(pos, sig_neg, sig_abs)
                dz = g * sig_mz - sig_z * beyond
                if mask is not None:
                    dz = jnp.where(mask, dz, 0.0)
                dz = (dz * ATTN_SCALE).astype(BF16)
                dqs.append(_dot(dz, kw))
                dk = dk + _dot_tn(dz, qh[h])
                dv = dv + _dot_tn(a.astype(BF16), doh[h])
            return jnp.where(low_lanes, dqs[0], dqs[1]), dk, dv

        def cond(c):
            return jnp.logical_and(c[0] >= 0, c[1] == 0)

        windows = []
        for sub in range(ATTN_SUBS):
            i = pl.program_id(1) * ATTN_SUBS + sub
            rows = slice(sub * QB, (sub + 1) * QB)
            qh = _split_heads(q_ref[rows, :].astype(F32), low_lanes)
            doh = _split_heads(do_ref[rows, :], low_lanes)
            first_blk, start, offset = _first_window(i)
            mask = _causal_mask(2 * QB, offset)
            kw = k_ref[pl.ds(start, 2 * QB), :]
            cs = [_attn_tile(qh[h], kw, mask, zero, after_s)[3] for h in range(2)]

            def far_sums(c, qh=qh, doh=doh):
                j, _, c0, c1, r0, r1 = c
                at = pl.multiple_of(j * QB, QB)
                kb = k_ref[pl.ds(at, QB), :]
                vb = v_ref[pl.ds(at, QB), :]
                cs, rs = [c0, c1], [r0, r1]
                for h in range(2):
                    _, _, a, cs[h] = _attn_tile(qh[h], kb, None, cs[h], after_s)
                    rs[h] = rs[h] + jnp.sum(a * _dot_nt(doh[h], vb), axis=1, keepdims=True)
                return j - 1, _sweep_done(*cs), cs[0], cs[1], rs[0], rs[1]

            far = lax.while_loop(cond, far_sums, (first_blk - 1, _sweep_done(*cs), cs[0], cs[1], zero, zero))
            windows.append((qh, doh, first_blk, start, mask, kw, [far[4], far[5]]))

        started = []
        for qh, doh, first_blk, start, mask, kw, total in windows:
            cs, gs = [zero, zero], [zero, zero]
            dq, dk, dv = tile(qh, doh, total, kw, v_ref[pl.ds(start, 2 * QB), :], mask, cs, gs)
            started.append((qh, doh, total, first_blk, start, dq, dk, dv, cs, gs))

        for _, _, _, _, start, _, dk, dv, _, _ in started:
            dk_ref[pl.ds(start, 2 * QB), :] += dk
            dv_ref[pl.ds(start, 2 * QB), :] += dv

        for sub, (qh, doh, total, first_blk, _, dq, _, _, cs, gs) in enumerate(started):
            def step(c, qh=qh, doh=doh, total=total):
                j, _, dq, c0, c1, s0, s1 = c
                at = pl.multiple_of(j * QB, QB)
                cs, gs = [c0, c1], [s0, s1]
                dq_j, dk, dv = tile(qh, doh, total, k_ref[pl.ds(at, QB), :], v_ref[pl.ds(at, QB), :], None, cs, gs)
                dk_ref[pl.ds(at, QB), :] += dk
                dv_ref[pl.ds(at, QB), :] += dv
                return j - 1, _sweep_done(*cs), dq + dq_j, cs[0], cs[1], gs[0], gs[1]

            init = (first_blk - 1, _sweep_done(*cs), dq, cs[0], cs[1], gs[0], gs[1])
            dq_ref[sub * QB:(sub + 1) * QB, :] = lax.while_loop(cond, step, init)[2]

    blk = pl.BlockSpec((ATTN_SUBS * QB, QB), lambda p, i: (i, p))
    full = lambda off: pl.BlockSpec((S, QB), lambda p, i: (0, off + p), pipeline_mode=pl.Buffered(1))
    return pl.pallas_call(
        body, name="bwd_attn", grid=(n_pairs, n_steps),
        in_specs=[blk, full(n_pairs), full(2 * n_pairs), blk],
        out_specs=[blk, pl.BlockSpec((S, QB), lambda p, i: (0, p)), pl.BlockSpec((S, QB), lambda p, i: (0, p))],
        out_shape=[jax.ShapeDtypeStruct((S, n_pairs * QB), F32)] * 3,
        compiler_params=_params("parallel", "arbitrary"),
    )(qkv, qkv, qkv, d_attn)


def _bwd_pool(u, d_pool, w_pool, tile):
    S, C = u.shape
    n_tiles = S // tile
    ng = len(POOL_WINDOWS)

    def body(u_ref, uh_ref, d_ref, dh_ref, wp_ref, du_ref, dwp_ref):
        i = pl.program_id(0)
        first = i == 0
        halo = jnp.where(first, 0.0, uh_ref[...])
        parts = _pool_deviation(u_ref[...], halo, i * tile)
        dout = d_ref[...]
        nxt = jnp.where(i == n_tiles - 1, 0.0, dh_ref[...])
        dext = jnp.concatenate([dout, nxt], axis=0).astype(BF16)
        counts = _pool_counts(i * tile, tile + HALO)
        dps, scaled = [], []
        for g in range(ng):
            lanes = slice(g * POOL_GROUP, (g + 1) * POOL_GROUP)
            dp = _dot_nt(dext[:, lanes], wp_ref[g].astype(BF16))
            dps.append(dp[:tile])
            scaled.append(dp / counts[g])
        sums = _window_sums(jnp.concatenate(scaled, axis=1), forward=True)
        for g, w in enumerate(POOL_WINDOWS):
            lanes = slice(g * POOL_GROUP, (g + 1) * POOL_GROUP)
            du_ref[:, lanes] = sums[w][:tile, lanes] - dps[g]
            _accumulate(dwp_ref.at[g], _dot_tn(parts[g].astype(BF16), dext[:tile, lanes]), first)

    row = pl.BlockSpec((tile, C), lambda i: (i, 0))
    return pl.pallas_call(
        body, name="bwd_pool", grid=(n_tiles,),
        in_specs=[row, _prev_halo_spec(tile, C), row, _next_halo_spec(tile, C, n_tiles), _const(w_pool.shape)],
        out_specs=[row, pl.BlockSpec(w_pool.shape, lambda i: (0, 0, 0))],
        out_shape=[jax.ShapeDtypeStruct((S, C), F32), jax.ShapeDtypeStruct(w_pool.shape, F32)],
        compiler_params=_params("arbitrary"),
    )(u, u, d_pool, d_pool, w_pool)


def _bwd_inproj(du, dq, dk, dv, w_in_g, h1, x, dx2, g1, tile):
    S, D = x.shape
    nb, _, cs = w_in_g.shape
    C = du.shape[1]
    per = C // cs

    def body(du_ref, dq_ref, dk_ref, dv_ref, w_ref, h_ref, x_ref, dx2_ref, g_ref, dx_ref, dw_ref, dg_ref):
        first = pl.program_id(0) == 0
        h = h_ref[...]
        dh = jnp.zeros((tile, D), F32)
        for d in range(nb):
            src = (du_ref, dq_ref, dk_ref, dv_ref)[d // per]
            dproj = src[:, (d % per) * cs:(d % per + 1) * cs].astype(BF16)
            dh = dh + _dot_nt(dproj, w_ref[d])
            _accumulate(dw_ref.at[d], _dot_tn(h, dproj), first)
        xf = x_ref[...]
        r1 = _rms(xf)
        n1 = xf * r1
        _accumulate(dg_ref, _colsum(dh * n1), first)
        dx_ref[...] = dx2_ref[...] + _norm_bwd(dh * g_ref[...], n1, r1)

    row = lambda w: pl.BlockSpec((tile, w), lambda i: (i, 0))
    return pl.pallas_call(
        body, name="bwd_inproj", grid=(S // tile,),
        in_specs=[row(C), row(C), row(C), row(C), _const(w_in_g.shape), row(D), row(D), row(D), _const((1, D))],
        out_specs=[row(D), pl.BlockSpec(w_in_g.shape, lambda i: (0, 0, 0)), pl.BlockSpec((1, D), lambda i: (0, 0))],
        out_shape=[jax.ShapeDtypeStruct((S, D), F32), jax.ShapeDtypeStruct(w_in_g.shape, F32), jax.ShapeDtypeStruct((1, D), F32)],
        compiler_params=_params("arbitrary"),
    )(du, dq, dk, dv, w_in_g, h1, x, dx2, g1)


def _local_step(x, target, g1, w_in_g, w_pool, pool_scale, attn_scale, w_out, g2, g3, w_up_g, conv_w_g, conv_b_g,
                w_down4, g4):
    S = x.shape[0]
    big = min(512, S)
    small = min(256, S)
    n_pairs = pool_scale.shape[1] // QB

    h1, u, qkv = _fwd_inproj(x, g1, w_in_g, big)
    pool_out = _fwd_pool(u, w_pool, big)
    attn_out = _fwd_attn(qkv, n_pairs)
    mix, x2, h2 = _fwd_outproj(pool_out, attn_out, pool_scale, attn_scale, w_out, x, g2, g3, big)
    upre = _fwd_up(h2, w_up_g, big)
    dy, df, loss_cols, dg4 = _fwd_ffn_loss(upre, conv_w_g, conv_b_g, w_down4, x2, target, g4, small)

    dgate, dval, d_wd4, d_cb, d_cw = _bwd_down(upre, conv_w_g, conv_b_g, w_down4, df, big)
    dupre, dx2, dmix, dg3, dg2 = _bwd_up_x(dgate, dval, conv_w_g, w_up_g, x2, dy, mix, g2, g3, small)
    d_wup = _bwd_weight(h2, dupre, big)
    d_pool, d_attn, d_wout, d_ps, d_as = _bwd_outproj(dmix, w_out, pool_out, attn_out, pool_scale, attn_scale, big)
    dq, dk, dv = _bwd_attn(qkv, d_attn, n_pairs)
    du, d_wp = _bwd_pool(u, d_pool, w_pool, big)
    dx, d_win, dg1 = _bwd_inproj(du, dq, dk, dv, w_in_g, h1, x, dx2, g1, big)
    return dict(loss_cols=loss_cols, dx=dx, g1=dg1, w_in=d_win, w_pool=d_wp, pool_scale=d_ps, attn_scale=d_as,
                w_out=d_wout, g2=dg2, g3=dg3, w_up=d_wup, conv_w=d_cw, conv_b=d_cb, w_down=d_wd4, g4=dg4)


def _mesh_position():
    x, y, c = lax.axis_index("x"), lax.axis_index("y"), lax.axis_index("c")
    return x, y, c, 4 * x + 2 * y + c


def _peer(x, y, c, k):
    px = 1 - x if k & 4 else x
    py = 1 - y if k & 2 else y
    pc = 1 - c if k & 1 else c
    return (px, py, pc), 4 * px + 2 * py + pc


def _all_to_all(arrays, gather, name):
    n = len(arrays)
    out_shapes = [jax.ShapeDtypeStruct(((N_DEV,) + a.shape) if gather else a.shape, a.dtype) for a in arrays]

    def body(*refs):
        ins, outs = refs[:n], refs[n:2 * n]
        send_sems, recv_sems, local_sems = refs[2 * n:]
        x, y, c, me = _mesh_position()
        local, remote = [], []
        for a in range(n):
            mine = ins[a] if gather else ins[a].at[me]
            cp = pltpu.make_async_copy(mine, outs[a].at[me], local_sems.at[a])
            cp.start()
            local.append(cp)
            for k in range(1, N_DEV):
                peer, peer_idx = _peer(x, y, c, k)
                src = ins[a] if gather else ins[a].at[peer_idx]
                sem = a * (N_DEV - 1) + k - 1
                cp = pltpu.make_async_remote_copy(src_ref=src, dst_ref=outs[a].at[me], send_sem=send_sems.at[sem],
                                                  recv_sem=recv_sems.at[sem], device_id=peer, device_id_type=MESH)
                cp.start()
                remote.append(cp)
        for cp in remote:
            cp.wait_send()
        for cp in remote:
            cp.wait_recv()
        for cp in local:
            cp.wait()

    any_spec = pl.BlockSpec(memory_space=pl.ANY)
    return pl.pallas_call(
        body, name=name,
        in_specs=[any_spec] * n, out_specs=[any_spec] * n, out_shape=out_shapes,
        scratch_shapes=[pltpu.SemaphoreType.DMA((n * (N_DEV - 1),)), pltpu.SemaphoreType.DMA((n * (N_DEV - 1),)),
                        pltpu.SemaphoreType.DMA((n,))],
    )(*arrays)


def _reduce_adamw(parts, w, m, v, rows):
    R, C = w.shape

    def body(p_ref, w_ref, m_ref, v_ref, g_ref, d_ref, nm_ref, nv_ref):
        g = p_ref[0].astype(F32)
        for s in range(1, N_DEV):
            g = g + p_ref[s].astype(F32)
        g_ref[...] = g
        m_new = ADAM_B1 * m_ref[...] + (1.0 - ADAM_B1) * g
        v_new = ADAM_B2 * v_ref[...] + (1.0 - ADAM_B2) * (g * g)
        m_hat = m_new / (1.0 - ADAM_B1 ** ADAM_STEP)
        v_hat = v_new / (1.0 - ADAM_B2 ** ADAM_STEP)
        d_ref[...] = -ADAM_LR * (m_hat / (jnp.sqrt(v_hat) + ADAM_EPS) + ADAM_WD * w_ref[...])
        nm_ref[...] = m_new
        nv_ref[...] = v_new

    row = pl.BlockSpec((rows, C), lambda i: (i, 0))
    return pl.pallas_call(
        body, name="reduce_adamw", grid=(R // rows,),
        in_specs=[pl.BlockSpec((N_DEV, rows, C), lambda i: (0, i, 0)), row, row, row],
        out_specs=[row] * 4, out_shape=[jax.ShapeDtypeStruct((R, C), F32)] * 4,
        compiler_params=_params("parallel"),
    )(parts, w, m, v)


def _row_tile(rows, cols):
    fits = [t for t in range(8, rows + 1, 8) if rows % t == 0 and N_DEV * t * cols * 4 <= 4 * 1024 * 1024]
    return max(fits) if fits else rows


SMALL_COLS = 1024


def _pack_small(vals):
    rows = []
    for a in vals:
        flat = a.reshape(-1)
        pad = (-flat.shape[0]) % SMALL_COLS
        rows.append(jnp.pad(flat, (0, pad)).reshape(-1, SMALL_COLS))
    packed = jnp.concatenate(rows, axis=0)
    return jnp.pad(packed, ((0, (-packed.shape[0]) % 8), (0, 0)))


def _unpack_small(packed, like):
    out, r = [], 0
    for a in like:
        n = a.size
        nr = -(-n // SMALL_COLS)
        out.append(packed[r:r + nr].reshape(-1)[:n].reshape(a.shape))
        r += nr
    return out


def kernel(x, norm_mix_pre, w_in, w_pool, pool_scale, attn_scale, w_out, norm_mix_post, norm_ffn_pre, w_up, conv_w, conv_b, w_down, norm_ffn_post, loss_target, m_norm_mix_pre, m_w_in, m_w_pool, m_pool_scale, m_attn_scale, m_w_out, m_norm_mix_post, m_norm_ffn_pre, m_w_up, m_conv_w, m_conv_b, m_w_down, m_norm_ffn_post, v_norm_mix_pre, v_w_in, v_w_pool, v_pool_scale, v_attn_scale, v_w_out, v_norm_mix_post, v_norm_ffn_pre, v_w_up, v_conv_w, v_conv_b, v_w_down, v_norm_ffn_post):
    S, D = x.shape[1], x.shape[2]
    d_ff_block = w_up.shape[2]

    w_in_g, w_out_g, w_up_g, w_down_g, conv_w_g = _all_to_all(
        [w_in[0].astype(BF16), w_out[0].astype(BF16), w_up[0].astype(BF16), w_down[0].astype(BF16), conv_w[0]], gather=True, name="gather_weights")
    w_out_full = w_out_g.reshape(D, D)
    w_down4 = w_down_g.reshape(D_FF_SHARDS, d_ff_block, D)
    conv_b_g = conv_b.reshape(N_DEV, 1, d_ff_block)

    r = _local_step(x[0], loss_target[0], norm_mix_pre, w_in_g, w_pool[0], pool_scale, attn_scale, w_out_full,
                    norm_mix_post, norm_ffn_pre, w_up_g, conv_w_g, conv_b_g, w_down4, norm_ffn_post)
    loss = lax.psum(0.5 * jnp.sum(r["loss_cols"]) / D, ("x", "y", "c"))

    small_names = ["norm_mix_pre", "w_pool", "pool_scale", "attn_scale", "norm_mix_post", "norm_ffn_pre", "conv_b", "norm_ffn_post"]
    small_w = dict(norm_mix_pre=norm_mix_pre, w_pool=w_pool, pool_scale=pool_scale, attn_scale=attn_scale,
                   norm_mix_post=norm_mix_post, norm_ffn_pre=norm_ffn_pre, conv_b=conv_b, norm_ffn_post=norm_ffn_post)
    small_m = dict(norm_mix_pre=m_norm_mix_pre, w_pool=m_w_pool, pool_scale=m_pool_scale, attn_scale=m_attn_scale,
                   norm_mix_post=m_norm_mix_post, norm_ffn_pre=m_norm_ffn_pre, conv_b=m_conv_b, norm_ffn_post=m_norm_ffn_post)
    small_v = dict(norm_mix_pre=v_norm_mix_pre, w_pool=v_w_pool, pool_scale=v_pool_scale, attn_scale=v_attn_scale,
                   norm_mix_post=v_norm_mix_post, norm_ffn_pre=v_norm_ffn_pre, conv_b=v_conv_b, norm_ffn_post=v_norm_ffn_post)
    small_g = dict(norm_mix_pre=r["g1"], w_pool=r["w_pool"], pool_scale=r["pool_scale"], attn_scale=r["attn_scale"],
                   norm_mix_post=r["g2"], norm_ffn_pre=r["g3"], conv_b=r["conv_b"], norm_ffn_post=r["g4"])
    like = [small_w[n] for n in small_names]
    packed_g = _pack_small([small_g[n] for n in small_names])

    d_wout_g = r["w_out"].reshape(N_DEV, D // N_DEV, D)
    d_wdown_g = r["w_down"].reshape(N_DEV, w_down.shape[1], D)
    (small_parts,) = _all_to_all([packed_g], gather=True, name="gather_small_grads")
    big_parts = _all_to_all([r["w_in"], d_wout_g, r["w_up"], d_wdown_g, r["conv_w"]], gather=False, name="exchange_grads")

    def update(parts, w, m, v):
        R, C = w.shape
        return _reduce_adamw(parts, w, m, v, _row_tile(R, C))

    res = {}
    res["w_in"] = update(big_parts[0], w_in[0], m_w_in[0], v_w_in[0])
    res["w_out"] = update(big_parts[1], w_out[0], m_w_out[0], v_w_out[0])
    res["w_up"] = update(big_parts[2], w_up[0], m_w_up[0], v_w_up[0])
    res["w_down"] = update(big_parts[3], w_down[0], m_w_down[0], v_w_down[0])
    res["conv_w"] = update(big_parts[4], conv_w[0], m_conv_w[0], v_conv_w[0])
    small_res = update(small_parts, _pack_small(like), _pack_small([small_m[n] for n in small_names]),
                       _pack_small([small_v[n] for n in small_names]))
    small_res = [_unpack_small(t, like) for t in small_res]
    for idx, n in enumerate(small_names):
        res[n] = tuple(t[idx] for t in small_res)

    order = ["norm_mix_pre", "w_in", "w_pool", "pool_scale", "attn_scale", "w_out", "norm_mix_post", "norm_ffn_pre",
             "w_up", "conv_w", "conv_b", "w_down", "norm_ffn_post"]
    shaped = {n: tuple(t.reshape(s.shape) for t in res[n])
              for n, s in dict(norm_mix_pre=norm_mix_pre, w_in=w_in, w_pool=w_pool, pool_scale=pool_scale, attn_scale=attn_scale,
                               w_out=w_out, norm_mix_post=norm_mix_post, norm_ffn_pre=norm_ffn_pre, w_up=w_up, conv_w=conv_w,
                               conv_b=conv_b, w_down=w_down, norm_ffn_post=norm_ffn_post).items()}
    outs = [loss, r["dx"].reshape(x.shape)]
    for k in range(4):
        outs += [shaped[n][k] for n in order]
    return tuple(outs)
```

```python
import functools

import jax
import jax.numpy as jnp
from jax import lax
from jax.experimental import pallas as pl
from jax.experimental.pallas import tpu as pltpu

F32 = jnp.float32
BF16 = jnp.bfloat16
HIGHEST = lax.Precision.HIGHEST

N_DEV = 8
EPS = 1e-6
POOL_WINDOWS = (2, 4, 8, 16)
POOL_GROUP = 128
HALO = 16
HEAD_DIM = 64
QB = 128
ATTN_SCALE = HEAD_DIM ** -0.5
ATTN_SUBS = 2
EXP_UNDERFLOW = -88.0
D_FF_SHARDS = 4

ADAM_LR = 0.001
ADAM_B1 = 0.9
ADAM_B2 = 0.999
ADAM_EPS = 1e-08
ADAM_WD = 0.01
ADAM_STEP = 10

VMEM_LIMIT_V7X = 56 * 1024 * 1024
MESH = pl.DeviceIdType.MESH


def _params(*semantics):
    return pltpu.CompilerParams(dimension_semantics=semantics, vmem_limit_bytes=VMEM_LIMIT_V7X)


def _const(shape):
    zeros = (0,) * len(shape)
    return pl.BlockSpec(shape, lambda *_: zeros, pipeline_mode=pl.Buffered(1))


def _dot(a, b):
    return jnp.dot(a, b, preferred_element_type=F32)


def _dot_nt(a, b):
    return lax.dot_general(a, b, (((1,), (1,)), ((), ())), preferred_element_type=F32)


def _dot_tn(a, b):
    return lax.dot_general(a, b, (((0,), (0,)), ((), ())), preferred_element_type=F32)


def _rms(v):
    return lax.rsqrt(jnp.mean(v * v, axis=-1, keepdims=True) + EPS)


def _norm_bwd(dn_times_gain, n, r):
    return r * (dn_times_gain - n * jnp.mean(dn_times_gain * n, axis=-1, keepdims=True))


def _accumulate(ref, value, first):
    @pl.when(first)
    def _():
        ref[...] = value

    @pl.when(jnp.logical_not(first))
    def _():
        ref[...] += value


def _colsum(v):
    return jnp.sum(v, axis=0, keepdims=True)


def _grid_ends(grid):
    ids = [pl.program_id(a) for a in range(len(grid))]
    first = functools.reduce(jnp.logical_and, [i == 0 for i in ids])
    last = functools.reduce(jnp.logical_and, [i == n - 1 for i, n in zip(ids, grid)])
    return first, last


def _fwd_inproj(x, g1, w_in_g, tile):
    S, D = x.shape
    nb, _, cs = w_in_g.shape
    d_pool = 2 * cs

    def body(x_ref, g_ref, w_ref, h_ref, u_ref, qkv_ref):
        xf = x_ref[...]
        h = (xf * _rms(xf) * g_ref[...]).astype(BF16)
        h_ref[...] = h
        for d in range(nb):
            o = _dot(h, w_ref[d])
            if d < 2:
                u_ref[:, d * cs:(d + 1) * cs] = o
            else:
                qkv_ref[:, (d - 2) * cs:(d - 1) * cs] = o.astype(BF16)

    return pl.pallas_call(
        body, name="fwd_inproj", grid=(S // tile,),
        in_specs=[pl.BlockSpec((tile, D), lambda i: (i, 0)), _const((1, D)), _const(w_in_g.shape)],
        out_specs=[pl.BlockSpec((tile, D), lambda i: (i, 0)), pl.BlockSpec((tile, d_pool), lambda i: (i, 0)),
                   pl.BlockSpec((tile, 3 * d_pool), lambda i: (i, 0))],
        out_shape=[jax.ShapeDtypeStruct((S, D), BF16), jax.ShapeDtypeStruct((S, d_pool), F32),
                   jax.ShapeDtypeStruct((S, 3 * d_pool), BF16)],
        compiler_params=_params("parallel"),
    )(x, g1, w_in_g)


def _window_sums(ext, forward):
    n = ext.shape[0]
    sums, s, sh = {}, ext, 1
    while sh < POOL_WINDOWS[-1]:
        s = s + pltpu.roll(s, (n - sh) if forward else sh, axis=0)
        sh *= 2
        sums[sh] = s
    return sums


def _pool_counts(t0, rows):
    t1 = (lax.broadcasted_iota(jnp.int32, (rows, 1), 0) + t0 + 1).astype(F32)
    return [jnp.minimum(t1, float(w)) for w in POOL_WINDOWS]


def _pool_deviation(u, halo, t0):
    T = u.shape[0]
    sums = _window_sums(jnp.concatenate([halo, u], axis=0), forward=False)
    counts = _pool_counts(t0, T)
    parts = []
    for g, w in enumerate(POOL_WINDOWS):
        lanes = slice(g * POOL_GROUP, (g + 1) * POOL_GROUP)
        parts.append(sums[w][HALO:, lanes] / counts[g] - u[:, lanes])
    return parts


def _prev_halo_spec(tile, width):
    return pl.BlockSpec((HALO, width), lambda i: (jnp.maximum(i * (tile // HALO) - 1, 0), 0))


def _next_halo_spec(tile, width, n_tiles):
    last = n_tiles * (tile // HALO) - 1
    return pl.BlockSpec((HALO, width), lambda i: (jnp.minimum((i + 1) * (tile // HALO), last), 0))


def _fwd_pool(u, w_pool, tile):
    S, C = u.shape

    def body(u_ref, halo_ref, wp_ref, o_ref):
        i = pl.program_id(0)
        halo = jnp.where(i > 0, halo_ref[...], 0.0)
        parts = _pool_deviation(u_ref[...], halo, i * tile)
        for g, p in enumerate(parts):
            o_ref[:, g * POOL_GROUP:(g + 1) * POOL_GROUP] = _dot(p.astype(BF16), wp_ref[g].astype(BF16))

    return pl.pallas_call(
        body, name="fwd_pool", grid=(S // tile,),
        in_specs=[pl.BlockSpec((tile, C), lambda i: (i, 0)), _prev_halo_spec(tile, C), _const(w_pool.shape)],
        out_specs=pl.BlockSpec((tile, C), lambda i: (i, 0)),
        out_shape=jax.ShapeDtypeStruct((S, C), F32),
        compiler_params=_params("parallel"),
    )(u, u, w_pool)


def _low_lanes():
    return lax.broadcasted_iota(jnp.int32, (QB, 2 * HEAD_DIM), 1) < HEAD_DIM


def _triangle(inclusive):
    row = lax.broadcasted_iota(jnp.int32, (QB, QB), 0)
    col = lax.broadcasted_iota(jnp.int32, (QB, QB), 1)
    return ((row >= col) if inclusive else (row > col)).astype(F32)


def _causal_mask(width, offset):
    row = lax.broadcasted_iota(jnp.int32, (QB, width), 0)
    col = lax.broadcasted_iota(jnp.int32, (QB, width), 1)
    return col < row + offset


def _suffix_sums(vals, tri, carry):
    n = vals.shape[1] // QB
    out, run = [None] * n, carry
    for b in reversed(range(n)):
        blk = vals[:, b * QB:(b + 1) * QB]
        out[b] = jnp.dot(blk, tri, precision=HIGHEST, preferred_element_type=F32) + run
        run = run + jnp.sum(blk, axis=1, keepdims=True)
    return (out[0] if n == 1 else jnp.concatenate(out, axis=1)), run


def _attn_tile(qh, kw, mask, carry, after_s):
    z = _dot_nt(qh, kw) * ATTN_SCALE
    e = jnp.exp(-jnp.abs(z))
    softplus = jnp.maximum(z, 0.0) + jnp.log(1.0 + e)
    log_1m_beta = -softplus if mask is None else jnp.where(mask, -softplus, 0.0)
    stick, carry = _suffix_sums(log_1m_beta, after_s, carry)
    a = jnp.exp(z - softplus + stick)
    if mask is not None:
        a = jnp.where(mask, a, 0.0)
    return z, e, a, carry


def _split_heads(v, low_lanes):
    return jnp.where(low_lanes, v, 0.0).astype(BF16), jnp.where(low_lanes, 0.0, v).astype(BF16)


def _sweep_done(c0, c1):
    return (jnp.maximum(jnp.max(c0), jnp.max(c1)) < EXP_UNDERFLOW).astype(jnp.int32)


def _first_window(i):
    first_blk = jnp.maximum(i - 1, 0)
    return first_blk, pl.multiple_of(first_blk * QB, QB), (i - first_blk) * QB


def _fwd_attn(qkv, n_pairs, ex):
    S = qkv.shape[0]
    n_steps = S // (ATTN_SUBS * QB)

    def body(q_ref, k_ref, v_ref, *rest):
        o_ref = rest[ex.n]
        ex_refs = ex.split(rest[:ex.n] + rest[ex.n + 1:])
        first_step, last_step = _grid_ends((n_pairs, n_steps))

        @pl.when(first_step)
        def _():
            ex.start(*ex_refs)

        low_lanes = _low_lanes()
        after_s = _triangle(False)
        zero = jnp.zeros((QB, 1), F32)

        def cond(c):
            return jnp.logical_and(c[0] >= 0, c[1] == 0)

        started = []
        for sub in range(ATTN_SUBS):
            i = pl.program_id(1) * ATTN_SUBS + sub
            qh = _split_heads(q_ref[sub * QB:(sub + 1) * QB, :].astype(F32), low_lanes)
            first_blk, start, offset = _first_window(i)
            kw = k_ref[pl.ds(start, 2 * QB), :]
            vw = v_ref[pl.ds(start, 2 * QB), :]
            mask = _causal_mask(2 * QB, offset)
            outs, carries = [], []
            for h in range(2):
                _, _, a, c = _attn_tile(qh[h], kw, mask, zero, after_s)
                outs.append(_dot(a.astype(BF16), vw))
                carries.append(c)
            started.append((qh, first_blk, jnp.where(low_lanes, outs[0], outs[1]), carries))

        for sub, (qh, first_blk, acc, carries) in enumerate(started):
            def step(c, qh=qh):
                j, _, acc, c0, c1 = c
                at = pl.multiple_of(j * QB, QB)
                kb = k_ref[pl.ds(at, QB), :]
                vb = v_ref[pl.ds(at, QB), :]
                cs, outs = [c0, c1], []
                for h in range(2):
                    _, _, a, cs[h] = _attn_tile(qh[h], kb, None, cs[h], after_s)
                    outs.append(_dot(a.astype(BF16), vb))
                acc = acc + jnp.where(low_lanes, outs[0], outs[1])
                return j - 1, _sweep_done(*cs), acc, cs[0], cs[1]

            init = (first_blk - 1, _sweep_done(*carries), acc, carries[0], carries[1])
            o_ref[sub * QB:(sub + 1) * QB, :] = lax.while_loop(cond, step, init)[2]

        @pl.when(last_step)
        def _():
            ex.wait(*ex_refs)

    outs = pl.pallas_call(
        body, name="fwd_attn", grid=(n_pairs, n_steps),
        in_specs=[pl.BlockSpec((ATTN_SUBS * QB, QB), lambda p, i: (i, p)),
                  pl.BlockSpec((S, QB), lambda p, i: (0, n_pairs + p), pipeline_mode=pl.Buffered(1)),
                  pl.BlockSpec((S, QB), lambda p, i: (0, 2 * n_pairs + p), pipeline_mode=pl.Buffered(1))] + ex.specs,
        out_specs=[pl.BlockSpec((ATTN_SUBS * QB, QB), lambda p, i: (i, p))] + ex.specs,
        out_shape=[jax.ShapeDtypeStruct((S, n_pairs * QB), F32)] + ex.out_shape,
        scratch_shapes=ex.scratch,
        compiler_params=_params("arbitrary", "arbitrary"),
    )(qkv, qkv, qkv, *ex.arrays)
    return outs[0], outs[1:]


def _normalized_heads(pool_out, attn_out):
    rp, ra = _rms(pool_out), _rms(attn_out)
    return pool_out * rp, rp, attn_out * ra, ra


def _fwd_outproj(pool_out, attn_out, pool_scale, attn_scale, w_out, x, g2, g3, tile):
    S, D = x.shape
    C = pool_out.shape[1]

    def body(p_ref, a_ref, ps_ref, as_ref, w_ref, x_ref, g2_ref, g3_ref, mix_ref, x2_ref, h2_ref):
        n_p, _, n_a, _ = _normalized_heads(p_ref[...], a_ref[...])
        mix = _dot((n_p * ps_ref[...]).astype(BF16), w_ref[:C, :]) + _dot((n_a * as_ref[...]).astype(BF16), w_ref[C:, :])
        mix_ref[...] = mix
        x2 = x_ref[...] + mix * _rms(mix) * g2_ref[...]
        x2_ref[...] = x2
        h2_ref[...] = (x2 * _rms(x2) * g3_ref[...]).astype(BF16)

    row = lambda w: pl.BlockSpec((tile, w), lambda i: (i, 0))
    return pl.pallas_call(
        body, name="fwd_outproj", grid=(S // tile,),
        in_specs=[row(C), row(C), _const((1, C)), _const((1, C)), _const(w_out.shape), row(D), _const((1, D)), _const((1, D))],
        out_specs=[row(D), row(D), row(D)],
        out_shape=[jax.ShapeDtypeStruct((S, D), F32), jax.ShapeDtypeStruct((S, D), F32), jax.ShapeDtypeStruct((S, D), BF16)],
        compiler_params=_params("parallel"),
    )(pool_out, attn_out, pool_scale, attn_scale, w_out, x, g2, g3)


def _fwd_up(h2, w_up_g, tile):
    S, D = h2.shape
    nb, _, cs = w_up_g.shape

    def body(h_ref, w_ref, o_ref):
        h = h_ref[...]
        for d in range(nb):
            o_ref[d] = _dot(h, w_ref[d]).astype(BF16)

    return pl.pallas_call(
        body, name="fwd_up", grid=(S // tile,),
        in_specs=[pl.BlockSpec((tile, D), lambda i: (i, 0)), _const(w_up_g.shape)],
        out_specs=pl.BlockSpec((nb, tile, cs), lambda i: (0, i, 0)),
        out_shape=jax.ShapeDtypeStruct((nb, S, cs), BF16),
        compiler_params=_params("parallel"),
    )(h2, w_up_g)


def _conv_taps(tile_rows, halo_rows):
    T = tile_rows.shape[0]
    ext = jnp.concatenate([halo_rows.astype(F32), tile_rows.astype(F32)], axis=0)
    return pltpu.roll(ext, 2, axis=0)[HALO:], pltpu.roll(ext, 1, axis=0)[HALO:], ext[HALO:]


def _tap_rows(cw_ref, d):
    return [cw_ref[d, k:k + 1, :] for k in range(3)]


def _gated_unit(taps_gate, taps_val, cw_gate, cw_val, cb_gate, cb_val):
    gate = cw_gate[0] * taps_gate[0] + cw_gate[1] * taps_gate[1] + cw_gate[2] * taps_gate[2] + cb_gate
    val = cw_val[0] * taps_val[0] + cw_val[1] * taps_val[1] + cw_val[2] * taps_val[2] + cb_val
    sig = 1.0 / (1.0 + jnp.exp(-gate))
    return gate, val, sig


def _fwd_ffn_loss(upre, conv_w_g, conv_b_g, w_down4, x2, target, g4, tile):
    nb, S, cs = upre.shape
    D = x2.shape[1]

    def body(u_ref, halo_ref, cw_ref, cb_ref, wd_ref, x2_ref, t_ref, g4_ref, dy_ref, df_ref, loss_ref, dg4_ref):
        i = pl.program_id(0)
        first = i == 0
        f = jnp.zeros((tile, D), F32)
        for s in range(D_FF_SHARDS):
            halo_g = jnp.where(first, jnp.zeros_like(halo_ref[s]), halo_ref[s])
            halo_v = jnp.where(first, jnp.zeros_like(halo_ref[s]), halo_ref[s + D_FF_SHARDS])
            gate, val, sig = _gated_unit(_conv_taps(u_ref[s], halo_g), _conv_taps(u_ref[s + D_FF_SHARDS], halo_v),
                                         _tap_rows(cw_ref, s), _tap_rows(cw_ref, s + D_FF_SHARDS), cb_ref[s], cb_ref[s + D_FF_SHARDS])
            f = f + _dot((gate * sig * val).astype(BF16), wd_ref[s])
        r4 = _rms(f)
        n4 = f * r4
        err = x2_ref[...] + n4 * g4_ref[...] - t_ref[...]
        dy = err * (1.0 / D)
        dy_ref[...] = dy
        df_ref[...] = _norm_bwd(dy * g4_ref[...], n4, r4).astype(BF16)
        _accumulate(loss_ref, _colsum(err * err), first)
        _accumulate(dg4_ref, _colsum(dy * n4), first)

    row = lambda w: pl.BlockSpec((tile, w), lambda i: (i, 0))
    return pl.pallas_call(
        body, name="fwd_ffn_loss", grid=(S // tile,),
        in_specs=[pl.BlockSpec((nb, tile, cs), lambda i: (0, i, 0)),
                  pl.BlockSpec((nb, HALO, cs), lambda i: (0, jnp.maximum(i * (tile // HALO) - 1, 0), 0)),
                  _const(conv_w_g.shape), _const(conv_b_g.shape), _const(w_down4.shape), row(D), row(D), _const((1, D))],
        out_specs=[row(D), row(D), pl.BlockSpec((1, D), lambda i: (0, 0)), pl.BlockSpec((1, D), lambda i: (0, 0))],
        out_shape=[jax.ShapeDtypeStruct((S, D), F32), jax.ShapeDtypeStruct((S, D), BF16),
                   jax.ShapeDtypeStruct((1, D), F32), jax.ShapeDtypeStruct((1, D), F32)],
        compiler_params=_params("arbitrary"),
    )(upre, upre, conv_w_g, conv_b_g, w_down4, x2, target, g4)


def _bwd_down(upre, conv_w_g, conv_b_g, w_down4, df, tile):
    nb, S, cs = upre.shape
    D = df.shape[1]
    n_tiles = S // tile

    def body(ug_ref, uv_ref, hg_ref, hv_ref, cwg_ref, cwv_ref, cbg_ref, cbv_ref, wd_ref, df_ref,
             dg_ref, dv_ref, dwd_ref, dbg_ref, dbv_ref, dcwg_ref, dcwv_ref):
        i = pl.program_id(1)
        first = i == 0
        halo_g = jnp.where(first, jnp.zeros_like(hg_ref[0]), hg_ref[0])
        halo_v = jnp.where(first, jnp.zeros_like(hv_ref[0]), hv_ref[0])
        taps_g, taps_v = _conv_taps(ug_ref[0], halo_g), _conv_taps(uv_ref[0], halo_v)
        gate, val, sig = _gated_unit(taps_g, taps_v, _tap_rows(cwg_ref, 0), _tap_rows(cwv_ref, 0), cbg_ref[0], cbv_ref[0])
        silu = gate * sig
        dfb = df_ref[...]
        dact = _dot_nt(dfb, wd_ref[0])
        _accumulate(dwd_ref.at[0], _dot_tn((silu * val).astype(BF16), dfb), first)
        dgate = dact * val * (sig * (1.0 + gate * (1.0 - sig)))
        dval = dact * silu
        dg_ref[0] = dgate.astype(BF16)
        dv_ref[0] = dval.astype(BF16)
        _accumulate(dbg_ref.at[0], _colsum(dgate), first)
        _accumulate(dbv_ref.at[0], _colsum(dval), first)
        _accumulate(dcwg_ref.at[0], jnp.concatenate([_colsum(dgate * t) for t in taps_g], axis=0), first)
        _accumulate(dcwv_ref.at[0], jnp.concatenate([_colsum(dval * t) for t in taps_v], axis=0), first)

    half = D_FF_SHARDS
    blk = lambda off: pl.BlockSpec((1, tile, cs), lambda s, i: (s + off, i, 0))
    halo = lambda off: pl.BlockSpec((1, HALO, cs), lambda s, i: (s + off, jnp.maximum(i * (tile // HALO) - 1, 0), 0))
    par = lambda off, r: pl.BlockSpec((1, r, cs), lambda s, i: (s + off, 0, 0))
    outs = pl.pallas_call(
        body, name="bwd_down", grid=(half, n_tiles),
        in_specs=[blk(0), blk(half), halo(0), halo(half), par(0, 3), par(half, 3), par(0, 1), par(half, 1),
                  pl.BlockSpec((1, cs, D), lambda s, i: (s, 0, 0)), pl.BlockSpec((tile, D), lambda s, i: (i, 0))],
        out_specs=[blk(0), blk(0), pl.BlockSpec((1, cs, D), lambda s, i: (s, 0, 0)),
                   par(0, 1), par(0, 1), par(0, 3), par(0, 3)],
        out_shape=[jax.ShapeDtypeStruct((half, S, cs), BF16), jax.ShapeDtypeStruct((half, S, cs), BF16),
                   jax.ShapeDtypeStruct((half, cs, D), F32),
                   jax.ShapeDtypeStruct((half, 1, cs), F32), jax.ShapeDtypeStruct((half, 1, cs), F32),
                   jax.ShapeDtypeStruct((half, 3, cs), F32), jax.ShapeDtypeStruct((half, 3, cs), F32)],
        compiler_params=_params("parallel", "arbitrary"),
    )(upre, upre, upre, upre, conv_w_g, conv_w_g, conv_b_g, conv_b_g, w_down4, df)
    dgate, dval, d_wd, dbg, dbv, dcwg, dcwv = outs
    return dgate, dval, d_wd, jnp.concatenate([dbg, dbv], axis=0), jnp.concatenate([dcwg, dcwv], axis=0)


def _bwd_up_x(dgate, dval, conv_w_g, w_up_g, x2, dy, mix, g2, g3, tile):
    half, S, cs = dgate.shape
    nb = 2 * half
    D = x2.shape[1]
    n_tiles = S // tile

    def body(dg_ref, dv_ref, hg_ref, hv_ref, cw_ref, w_ref, x2_ref, dy_ref, mix_ref, g2_ref, g3_ref,
             dupre_ref, dx2_ref, dmix_ref, dg3_ref, dg2_ref):
        i = pl.program_id(0)
        first = i == 0
        last = i == n_tiles - 1
        dh2 = jnp.zeros((tile, D), F32)
        for d in range(nb):
            src, halo = (dg_ref, hg_ref) if d < half else (dv_ref, hv_ref)
            nxt = jnp.where(last, jnp.zeros_like(halo[d % half]), halo[d % half])
            ext = jnp.concatenate([src[d % half].astype(F32), nxt.astype(F32)], axis=0)
            n = ext.shape[0]
            cw = _tap_rows(cw_ref, d)
            dupre = (cw[2] * ext + cw[1] * pltpu.roll(ext, n - 1, axis=0) + cw[0] * pltpu.roll(ext, n - 2, axis=0))[:tile]
            dupre = dupre.astype(BF16)
            dupre_ref[d] = dupre
            dh2 = dh2 + _dot_nt(dupre, w_ref[d])
        x2 = x2_ref[...]
        r3 = _rms(x2)
        n3 = x2 * r3
        _accumulate(dg3_ref, _colsum(dh2 * n3), first)
        dx2 = dy_ref[...] + _norm_bwd(dh2 * g3_ref[...], n3, r3)
        dx2_ref[...] = dx2
        mix = mix_ref[...]
        r2 = _rms(mix)
        n2 = mix * r2
        _accumulate(dg2_ref, _colsum(dx2 * n2), first)
        dmix_ref[...] = _norm_bwd(dx2 * g2_ref[...], n2, r2).astype(BF16)

    row = lambda w: pl.BlockSpec((tile, w), lambda i: (i, 0))
    blk = pl.BlockSpec((half, tile, cs), lambda i: (0, i, 0))
    last_halo = n_tiles * (tile // HALO) - 1
    halo = pl.BlockSpec((half, HALO, cs), lambda i: (0, jnp.minimum((i + 1) * (tile // HALO), last_halo), 0))
    acc = pl.BlockSpec((1, D), lambda i: (0, 0))
    return pl.pallas_call(
        body, name="bwd_up_x", grid=(n_tiles,),
        in_specs=[blk, blk, halo, halo, _const(conv_w_g.shape), _const(w_up_g.shape), row(D), row(D), row(D),
                  _const((1, D)), _const((1, D))],
        out_specs=[pl.BlockSpec((nb, tile, cs), lambda i: (0, i, 0)), row(D), row(D), acc, acc],
        out_shape=[jax.ShapeDtypeStruct((nb, S, cs), BF16), jax.ShapeDtypeStruct((S, D), F32),
                   jax.ShapeDtypeStruct((S, D), BF16), jax.ShapeDtypeStruct((1, D), F32), jax.ShapeDtypeStruct((1, D), F32)],
        compiler_params=_params("arbitrary"),
    )(dgate, dval, dgate, dval, conv_w_g, w_up_g, x2, dy, mix, g2, g3)


def _bwd_weight(act, dout, tile):
    S, D = act.shape
    nb, _, cs = dout.shape

    def body(a_ref, d_ref, o_ref):
        _accumulate(o_ref.at[0], _dot_tn(a_ref[...], d_ref[0]), pl.program_id(1) == 0)

    return pl.pallas_call(
        body, name="bwd_w_up", grid=(nb, S // tile),
        in_specs=[pl.BlockSpec((tile, D), lambda d, i: (i, 0)), pl.BlockSpec((1, tile, cs), lambda d, i: (d, i, 0))],
        out_specs=pl.BlockSpec((1, D, cs), lambda d, i: (d, 0, 0)),
        out_shape=jax.ShapeDtypeStruct((nb, D, cs), F32),
        compiler_params=_params("parallel", "arbitrary"),
    )(act, dout)


def _bwd_outproj(dmix, w_out, pool_out, attn_out, pool_scale, attn_scale, tile):
    S, D = dmix.shape
    C = pool_out.shape[1]

    def body(dm_ref, w_ref, p_ref, a_ref, ps_ref, as_ref, dp_ref, da_ref, dw_ref, dps_ref, das_ref):
        first = pl.program_id(0) == 0
        dmx = dm_ref[...]
        dmerged = _dot_nt(dmx, w_ref[...])
        n_p, r_p, n_a, r_a = _normalized_heads(p_ref[...], a_ref[...])
        merged = jnp.concatenate([(n_p * ps_ref[...]).astype(BF16), (n_a * as_ref[...]).astype(BF16)], axis=1)
        _accumulate(dw_ref, _dot_tn(merged, dmx), first)
        dm_p, dm_a = dmerged[:, :C], dmerged[:, C:]
        _accumulate(dps_ref, _colsum(dm_p * n_p), first)
        _accumulate(das_ref, _colsum(dm_a * n_a), first)
        dp_ref[...] = _norm_bwd(dm_p * ps_ref[...], n_p, r_p)
        da_ref[...] = _norm_bwd(dm_a * as_ref[...], n_a, r_a)

    row = lambda w: pl.BlockSpec((tile, w), lambda i: (i, 0))
    return pl.pallas_call(
        body, name="bwd_outproj", grid=(S // tile,),
        in_specs=[row(D), _const(w_out.shape), row(C), row(C), _const((1, C)), _const((1, C))],
        out_specs=[row(C), row(C), pl.BlockSpec(w_out.shape, lambda i: (0, 0)),
                   pl.BlockSpec((1, C), lambda i: (0, 0)), pl.BlockSpec((1, C), lambda i: (0, 0))],
        out_shape=[jax.ShapeDtypeStruct((S, C), F32), jax.ShapeDtypeStruct((S, C), F32),
                   jax.ShapeDtypeStruct(w_out.shape, F32), jax.ShapeDtypeStruct((1, C), F32), jax.ShapeDtypeStruct((1, C), F32)],
        compiler_params=_params("arbitrary"),
    )(dmix, w_out, pool_out, attn_out, pool_scale, attn_scale)


def _bwd_attn(qkv, d_attn, n_pairs, ex):
    S = qkv.shape[0]
    n_steps = S // (ATTN_SUBS * QB)

    def body(q_ref, k_ref, v_ref, do_ref, *rest):
        dq_ref, dk_ref, dv_ref = rest[ex.n:ex.n + 3]
        ex_refs = ex.split(rest[:ex.n] + rest[ex.n + 3:])
        first_step, last_step = _grid_ends((n_pairs, n_steps))

        @pl.when(first_step)
        def _():
            ex.start(*ex_refs)

        @pl.when(pl.program_id(1) == 0)
        def _():
            dk_ref[...] = jnp.zeros_like(dk_ref)
            dv_ref[...] = jnp.zeros_like(dv_ref)

        low_lanes = _low_lanes()
        after_s, from_s = _triangle(False), _triangle(True)
        zero = jnp.zeros((QB, 1), F32)

        def tile(qh, doh, total, kw, vw, mask, cs, gs):
            first = mask is not None
            dqs = []
            dk = jnp.zeros((kw.shape[0], QB), F32)
            dv = jnp.zeros((kw.shape[0], QB), F32)
            for h in range(2):
                z, e, a, cs[h] = _attn_tile(qh[h], kw, mask, cs[h], after_s)
                g = a * _dot_nt(doh[h], vw)
                nearer, gs[h] = _suffix_sums(g, from_s, gs[h])
                if first:
                    total[h] = total[h] + gs[h]
                beyond = total[h] - nearer
                inv = 1.0 / (1.0 + e)
                sig_abs, sig_neg = inv, e * inv
                pos = z >= 0.0
                sig_z = jnp.where(pos, sig_abs, sig_neg)
                sig_mz = jnp.where(pos, sig_neg, sig_abs)
                dz = g * sig_mz - sig_z * beyond
                if mask is not None:
                    dz = jnp.where(mask, dz, 0.0)
                dz = (dz * ATTN_SCALE).astype(BF16)
                dqs.append(_dot(dz, kw))
                dk = dk + _dot_tn(dz, qh[h])
                dv = dv + _dot_tn(a.astype(BF16), doh[h])
            return jnp.where(low_lanes, dqs[0], dqs[1]), dk, dv

        def cond(c):
            return jnp.logical_and(c[0] >= 0, c[1] == 0)

        windows = []
        for sub in range(ATTN_SUBS):
            i = pl.program_id(1) * ATTN_SUBS + sub
            rows = slice(sub * QB, (sub + 1) * QB)
            qh = _split_heads(q_ref[rows, :].astype(F32), low_lanes)
            doh = _split_heads(do_ref[rows, :], low_lanes)
            first_blk, start, offset = _first_window(i)
            mask = _causal_mask(2 * QB, offset)
            kw = k_ref[pl.ds(start, 2 * QB), :]
            cs = [_attn_tile(qh[h], kw, mask, zero, after_s)[3] for h in range(2)]

            def far_sums(c, qh=qh, doh=doh):
                j, _, c0, c1, r0, r1 = c
                at = pl.multiple_of(j * QB, QB)
                kb = k_ref[pl.ds(at, QB), :]
                vb = v_ref[pl.ds(at, QB), :]
                cs, rs = [c0, c1], [r0, r1]
                for h in range(2):
                    _, _, a, cs[h] = _attn_tile(qh[h], kb, None, cs[h], after_s)
                    rs[h] = rs[h] + jnp.sum(a * _dot_nt(doh[h], vb), axis=1, keepdims=True)
                return j - 1, _sweep_done(*cs), cs[0], cs[1], rs[0], rs[1]

            far = lax.while_loop(cond, far_sums, (first_blk - 1, _sweep_done(*cs), cs[0], cs[1], zero, zero))
            windows.append((qh, doh, first_blk, start, mask, kw, [far[4], far[5]]))

        started = []
        for qh, doh, first_blk, start, mask, kw, total in windows:
            cs, gs = [zero, zero], [zero, zero]
            dq, dk, dv = tile(qh, doh, total, kw, v_ref[pl.ds(start, 2 * QB), :], mask, cs, gs)
            started.append((qh, doh, total, first_blk, start, dq, dk, dv, cs, gs))

        for _, _, _, _, start, _, dk, dv, _, _ in started:
            dk_ref[pl.ds(start, 2 * QB), :] += dk
            dv_ref[pl.ds(start, 2 * QB), :] += dv

        for sub, (qh, doh, total, first_blk, _, dq, _, _, cs, gs) in enumerate(started):
            def step(c, qh=qh, doh=doh, total=total):
                j, _, dq, c0, c1, s0, s1 = c
                at = pl.multiple_of(j * QB, QB)
                cs, gs = [c0, c1], [s0, s1]
                dq_j, dk, dv = tile(qh, doh, total, k_ref[pl.ds(at, QB), :], v_ref[pl.ds(at, QB), :], None, cs, gs)
                dk_ref[pl.ds(at, QB), :] += dk
                dv_ref[pl.ds(at, QB), :] += dv
                return j - 1, _sweep_done(*cs), dq + dq_j, cs[0], cs[1], gs[0], gs[1]

            init = (first_blk - 1, _sweep_done(*cs), dq, cs[0], cs[1], gs[0], gs[1])
            dq_ref[sub * QB:(sub + 1) * QB, :] = lax.while_loop(cond, step, init)[2]

        @pl.when(last_step)
        def _():
            ex.wait(*ex_refs)

    blk = pl.BlockSpec((ATTN_SUBS * QB, QB), lambda p, i: (i, p))
    full = lambda off: pl.BlockSpec((S, QB), lambda p, i: (0, off + p), pipeline_mode=pl.Buffered(1))
    outs = pl.pallas_call(
        body, name="bwd_attn", grid=(n_pairs, n_steps),
        in_specs=[blk, full(n_pairs), full(2 * n_pairs), blk] + ex.specs,
        out_specs=[blk, pl.BlockSpec((S, QB), lambda p, i: (0, p)), pl.BlockSpec((S, QB), lambda p, i: (0, p))] + ex.specs,
        out_shape=[jax.ShapeDtypeStruct((S, n_pairs * QB), F32)] * 3 + ex.out_shape,
        scratch_shapes=ex.scratch,
        compiler_params=_params("arbitrary", "arbitrary"),
    )(qkv, qkv, qkv, d_attn, *ex.arrays)
    return outs[0], outs[1], outs[2], outs[3:]


def _bwd_pool(u, d_pool, w_pool, tile):
    S, C = u.shape
    n_tiles = S // tile
    ng = len(POOL_WINDOWS)

    def body(u_ref, uh_ref, d_ref, dh_ref, wp_ref, du_ref, dwp_ref):
        i = pl.program_id(0)
        first = i == 0
        halo = jnp.where(first, 0.0, uh_ref[...])
        parts = _pool_deviation(u_ref[...], halo, i * tile)
        dout = d_ref[...]
        nxt = jnp.where(i == n_tiles - 1, 0.0, dh_ref[...])
        dext = jnp.concatenate([dout, nxt], axis=0).astype(BF16)
        counts = _pool_counts(i * tile, tile + HALO)
        dps, scaled = [], []
        for g in range(ng):
            lanes = slice(g * POOL_GROUP, (g + 1) * POOL_GROUP)
            dp = _dot_nt(dext[:, lanes], wp_ref[g].astype(BF16))
            dps.append(dp[:tile])
            scaled.append(dp / counts[g])
        sums = _window_sums(jnp.concatenate(scaled, axis=1), forward=True)
        for g, w in enumerate(POOL_WINDOWS):
            lanes = slice(g * POOL_GROUP, (g + 1) * POOL_GROUP)
            du_ref[:, lanes] = sums[w][:tile, lanes] - dps[g]
            _accumulate(dwp_ref.at[g], _dot_tn(parts[g].astype(BF16), dext[:tile, lanes]), first)

    row = pl.BlockSpec((tile, C), lambda i: (i, 0))
    return pl.pallas_call(
        body, name="bwd_pool", grid=(n_tiles,),
        in_specs=[row, _prev_halo_spec(tile, C), row, _next_halo_spec(tile, C, n_tiles), _const(w_pool.shape)],
        out_specs=[row, pl.BlockSpec(w_pool.shape, lambda i: (0, 0, 0))],
        out_shape=[jax.ShapeDtypeStruct((S, C), F32), jax.ShapeDtypeStruct(w_pool.shape, F32)],
        compiler_params=_params("arbitrary"),
    )(u, u, d_pool, d_pool, w_pool)


def _bwd_w_in(du, dq, dk, dv, h1, n_blocks, tile):
    S, D = h1.shape
    C = du.shape[1]
    cs = 4 * C // n_blocks
    per = C // cs

    def body(du_ref, dq_ref, dk_ref, dv_ref, h_ref, dproj_ref, dw_ref):
        first = pl.program_id(0) == 0
        h = h_ref[...]
        for d in range(n_blocks):
            src = (du_ref, dq_ref, dk_ref, dv_ref)[d // per]
            dproj = src[:, (d % per) * cs:(d % per + 1) * cs].astype(BF16)
            dproj_ref[:, d * cs:(d + 1) * cs] = dproj
            _accumulate(dw_ref.at[d], _dot_tn(h, dproj), first)

    row = lambda w: pl.BlockSpec((tile, w), lambda i: (i, 0))
    return pl.pallas_call(
        body, name="bwd_w_in", grid=(S // tile,),
        in_specs=[row(C), row(C), row(C), row(C), row(D)],
        out_specs=[row(4 * C), pl.BlockSpec((n_blocks, D, cs), lambda i: (0, 0, 0))],
        out_shape=[jax.ShapeDtypeStruct((S, 4 * C), BF16), jax.ShapeDtypeStruct((n_blocks, D, cs), F32)],
        compiler_params=_params("arbitrary"),
    )(du, dq, dk, dv, h1)


def _bwd_x(dproj, w_in_g, x, dx2, g1, tile, ex):
    S, D = x.shape
    nb, _, cs = w_in_g.shape
    n_tiles = S // tile

    def body(dp_ref, w_ref, x_ref, dx2_ref, g_ref, *rest):
        dx_ref, dg_ref = rest[ex.n:ex.n + 2]
        ex_refs = ex.split(rest[:ex.n] + rest[ex.n + 2:])
        first, last = _grid_ends((n_tiles,))

        @pl.when(first)
        def _():
            ex.start(*ex_refs)

        dh = jnp.zeros((tile, D), F32)
        for d in range(nb):
            dh = dh + _dot_nt(dp_ref[:, d * cs:(d + 1) * cs], w_ref[d])
        xf = x_ref[...]
        r1 = _rms(xf)
        n1 = xf * r1
        _accumulate(dg_ref, _colsum(dh * n1), first)
        dx_ref[...] = dx2_ref[...] + _norm_bwd(dh * g_ref[...], n1, r1)

        @pl.when(last)
        def _():
            ex.wait(*ex_refs)

    row = lambda w: pl.BlockSpec((tile, w), lambda i: (i, 0))
    outs = pl.pallas_call(
        body, name="bwd_x", grid=(n_tiles,),
        in_specs=[row(nb * cs), _const(w_in_g.shape), row(D), row(D), _const((1, D))] + ex.specs,
        out_specs=[row(D), pl.BlockSpec((1, D), lambda i: (0, 0))] + ex.specs,
        out_shape=[jax.ShapeDtypeStruct((S, D), F32), jax.ShapeDtypeStruct((1, D), F32)] + ex.out_shape,
        scratch_shapes=ex.scratch,
        compiler_params=_params("arbitrary"),
    )(dproj, w_in_g, x, dx2, g1, *ex.arrays)
    return outs[0], outs[1], outs[2:]


def _mesh_position():
    x, y, c = lax.axis_index("x"), lax.axis_index("y"), lax.axis_index("c")
    return x, y, c, 4 * x + 2 * y + c


def _peer(x, y, c, k):
    px = 1 - x if k & 4 else x
    py = 1 - y if k & 2 else y
    pc = 1 - c if k & 1 else c
    return (px, py, pc), 4 * px + 2 * py + pc


class _Exchange:
    def __init__(self, arrays, gather):
        self.arrays, self.gather, self.n = list(arrays), gather, len(arrays)
        self.out_shape = [jax.ShapeDtypeStruct(((N_DEV,) + a.shape) if gather else a.shape, a.dtype) for a in arrays]
        self.specs = [pl.BlockSpec(memory_space=pl.ANY)] * self.n
        copies = self.n * (N_DEV - 1)
        self.scratch = [pltpu.SemaphoreType.DMA((copies,)), pltpu.SemaphoreType.DMA((copies,)),
                        pltpu.SemaphoreType.DMA((self.n,))]

    def _copies(self, ins, outs, sems):
        send_sems, recv_sems, local_sems = sems
        x, y, c, me = _mesh_position()
        local, remote = [], []
        for a in range(self.n):
            mine = ins[a] if self.gather else ins[a].at[me]
            local.append(pltpu.make_async_copy(mine, outs[a].at[me], local_sems.at[a]))
            for k in range(1, N_DEV):
                peer, peer_idx = _peer(x, y, c, k)
                src = ins[a] if self.gather else ins[a].at[peer_idx]
                sem = a * (N_DEV - 1) + k - 1
                remote.append(pltpu.make_async_remote_copy(
                    src_ref=src, dst_ref=outs[a].at[me], send_sem=send_sems.at[sem], recv_sem=recv_sems.at[sem],
                    device_id=peer, device_id_type=MESH))
        return local, remote

    def start(self, ins, outs, sems):
        local, remote = self._copies(ins, outs, sems)
        for cp in local + remote:
            cp.start()

    def wait(self, ins, outs, sems):
        local, remote = self._copies(ins, outs, sems)
        for cp in remote:
            cp.wait_send()
        for cp in remote:
            cp.wait_recv()
        for cp in local:
            cp.wait()

    def split(self, refs):
        return refs[:self.n], refs[self.n:2 * self.n], refs[2 * self.n:]


def _all_to_all(arrays, gather, name):
    ex = _Exchange(arrays, gather)

    def body(*refs):
        ins, outs, sems = ex.split(refs)
        ex.start(ins, outs, sems)
        ex.wait(ins, outs, sems)

    return pl.pallas_call(body, name=name, in_specs=ex.specs, out_specs=ex.specs, out_shape=ex.out_shape,
                          scratch_shapes=ex.scratch)(*ex.arrays)


def _reduce_adamw(parts, w, m, v, rows):
    R, C = w.shape

    def body(p_ref, w_ref, m_ref, v_ref, g_ref, d_ref, nm_ref, nv_ref):
        g = p_ref[0].astype(F32)
        for s in range(1, N_DEV):
            g = g + p_ref[s].astype(F32)
        g_ref[...] = g
        m_new = ADAM_B1 * m_ref[...] + (1.0 - ADAM_B1) * g
        v_new = ADAM_B2 * v_ref[...] + (1.0 - ADAM_B2) * (g * g)
        m_hat = m_new / (1.0 - ADAM_B1 ** ADAM_STEP)
        v_hat = v_new / (1.0 - ADAM_B2 ** ADAM_STEP)
        d_ref[...] = -ADAM_LR * (m_hat / (jnp.sqrt(v_hat) + ADAM_EPS) + ADAM_WD * w_ref[...])
        nm_ref[...] = m_new
        nv_ref[...] = v_new

    row = pl.BlockSpec((rows, C), lambda i: (i, 0))
    return pl.pallas_call(
        body, name="reduce_adamw", grid=(R // rows,),
        in_specs=[pl.BlockSpec((N_DEV, rows, C), lambda i: (0, i, 0)), row, row, row],
        out_specs=[row] * 4, out_shape=[jax.ShapeDtypeStruct((R, C), F32)] * 4,
        compiler_params=_params("parallel"),
    )(parts, w, m, v)


def _row_tile(rows, cols):
    fits = [t for t in range(8, rows + 1, 8) if rows % t == 0 and N_DEV * t * cols * 4 <= 4 * 1024 * 1024]
    return max(fits) if fits else rows


SMALL_COLS = 1024


def _pack_small(vals):
    rows = []
    for a in vals:
        flat = a.reshape(-1)
        pad = (-flat.shape[0]) % SMALL_COLS
        rows.append(jnp.pad(flat, (0, pad)).reshape(-1, SMALL_COLS))
    packed = jnp.concatenate(rows, axis=0)
    return jnp.pad(packed, ((0, (-packed.shape[0]) % 8), (0, 0)))


def _unpack_small(packed, like):
    out, r = [], 0
    for a in like:
        n = a.size
        nr = -(-n // SMALL_COLS)
        out.append(packed[r:r + nr].reshape(-1)[:n].reshape(a.shape))
        r += nr
    return out


def kernel(x, norm_mix_pre, w_in, w_pool, pool_scale, attn_scale, w_out, norm_mix_post, norm_ffn_pre, w_up, conv_w, conv_b, w_down, norm_ffn_post, loss_target, m_norm_mix_pre, m_w_in, m_w_pool, m_pool_scale, m_attn_scale, m_w_out, m_norm_mix_post, m_norm_ffn_pre, m_w_up, m_conv_w, m_conv_b, m_w_down, m_norm_ffn_post, v_norm_mix_pre, v_w_in, v_w_pool, v_pool_scale, v_attn_scale, v_w_out, v_norm_mix_post, v_norm_ffn_pre, v_w_up, v_conv_w, v_conv_b, v_w_down, v_norm_ffn_post):
    S, D = x.shape[1], x.shape[2]
    d_ff_block = w_up.shape[2]

    xs, target = x[0], loss_target[0]
    g1, g2, g3, g4 = norm_mix_pre, norm_mix_post, norm_ffn_pre, norm_ffn_post
    big = min(512, S)
    small = min(256, S)
    n_pairs = pool_scale.shape[1] // QB
    conv_b_g = conv_b.reshape(N_DEV, 1, d_ff_block)

    (w_in_g,) = _all_to_all([w_in[0].astype(BF16)], gather=True, name="gather_w_in")
    h1, u, qkv = _fwd_inproj(xs, g1, w_in_g, big)
    pool_out = _fwd_pool(u, w_pool[0], big)
    attn_out, (w_out_g, w_up_g, w_down_g, conv_w_g) = _fwd_attn(
        qkv, n_pairs, _Exchange([w_out[0].astype(BF16), w_up[0].astype(BF16), w_down[0].astype(BF16), conv_w[0]], gather=True))
    w_out_full = w_out_g.reshape(D, D)
    w_down4 = w_down_g.reshape(D_FF_SHARDS, d_ff_block, D)
    mix, x2, h2 = _fwd_outproj(pool_out, attn_out, pool_scale, attn_scale, w_out_full, xs, g2, g3, big)
    upre = _fwd_up(h2, w_up_g, big)
    dy, df, loss_cols, dg4 = _fwd_ffn_loss(upre, conv_w_g, conv_b_g, w_down4, x2, target, g4, small)
    loss = lax.psum(0.5 * jnp.sum(loss_cols) / D, ("x", "y", "c"))

    dgate, dval, d_wd4, d_cb, d_cw = _bwd_down(upre, conv_w_g, conv_b_g, w_down4, df, big)
    dupre, dx2, dmix, dg3, dg2 = _bwd_up_x(dgate, dval, conv_w_g, w_up_g, x2, dy, mix, g2, g3, small)
    d_wup = _bwd_weight(h2, dupre, big)
    d_pool, d_attn, d_wout, d_ps, d_as = _bwd_outproj(dmix, w_out_full, pool_out, attn_out, pool_scale, attn_scale, big)
    d_wdown_g = d_wd4.reshape(N_DEV, w_down.shape[1], D)
    dq, dk, dv, ffn_parts = _bwd_attn(qkv, d_attn, n_pairs, _Exchange([d_wup, d_wdown_g, d_cw], gather=False))
    du, d_wp = _bwd_pool(u, d_pool, w_pool[0], big)
    dproj, d_win = _bwd_w_in(du, dq, dk, dv, h1, N_DEV, big)
    d_wout_g = d_wout.reshape(N_DEV, D // N_DEV, D)
    dx, dg1, mix_parts = _bwd_x(dproj, w_in_g, xs, dx2, g1, big, _Exchange([d_win, d_wout_g], gather=False))
    big_parts = [mix_parts[0], mix_parts[1], ffn_parts[0], ffn_parts[1], ffn_parts[2]]
    r = dict(dx=dx, g1=dg1, w_pool=d_wp, pool_scale=d_ps, attn_scale=d_as, g2=dg2, g3=dg3, conv_b=d_cb, g4=dg4)

    small_names = ["norm_mix_pre", "w_pool", "pool_scale", "attn_scale", "norm_mix_post", "norm_ffn_pre", "conv_b", "norm_ffn_post"]
    small_w = dict(norm_mix_pre=norm_mix_pre, w_pool=w_pool, pool_scale=pool_scale, attn_scale=attn_scale,
                   norm_mix_post=norm_mix_post, norm_ffn_pre=norm_ffn_pre, conv_b=conv_b, norm_ffn_post=norm_ffn_post)
    small_m = dict(norm_mix_pre=m_norm_mix_pre, w_pool=m_w_pool, pool_scale=m_pool_scale, attn_scale=m_attn_scale,
                   norm_mix_post=m_norm_mix_post, norm_ffn_pre=m_norm_ffn_pre, conv_b=m_conv_b, norm_ffn_post=m_norm_ffn_post)
    small_v = dict(norm_mix_pre=v_norm_mix_pre, w_pool=v_w_pool, pool_scale=v_pool_scale, attn_scale=v_attn_scale,
                   norm_mix_post=v_norm_mix_post, norm_ffn_pre=v_norm_ffn_pre, conv_b=v_conv_b, norm_ffn_post=v_norm_ffn_post)
    small_g = dict(norm_mix_pre=r["g1"], w_pool=r["w_pool"], pool_scale=r["pool_scale"], attn_scale=r["attn_scale"],
                   norm_mix_post=r["g2"], norm_ffn_pre=r["g3"], conv_b=r["conv_b"], norm_ffn_post=r["g4"])
    like = [small_w[n] for n in small_names]
    packed_g = _pack_small([small_g[n] for n in small_names])

    (small_parts,) = _all_to_all([packed_g], gather=True, name="gather_small_grads")

    def update(parts, w, m, v):
        R, C = w.shape
        return _reduce_adamw(parts, w, m, v, _row_tile(R, C))

    res = {}
    res["w_in"] = update(big_parts[0], w_in[0], m_w_in[0], v_w_in[0])
    res["w_out"] = update(big_parts[1], w_out[0], m_w_out[0], v_w_out[0])
    res["w_up"] = update(big_parts[2], w_up[0], m_w_up[0], v_w_up[0])
    res["w_down"] = update(big_parts[3], w_down[0], m_w_down[0], v_w_down[0])
    res["conv_w"] = update(big_parts[4], conv_w[0], m_conv_w[0], v_conv_w[0])
    small_res = update(small_parts, _pack_small(like), _pack_small([small_m[n] for n in small_names]),
                       _pack_small([small_v[n] for n in small_names]))
    small_res = [_unpack_small(t, like) for t in small_res]
    for idx, n in enumerate(small_names):
        res[n] = tuple(t[idx] for t in small_res)

    order = ["norm_mix_pre", "w_in", "w_pool", "pool_scale", "attn_scale", "w_out", "norm_mix_post", "norm_ffn_pre",
             "w_up", "conv_w", "conv_b", "w_down", "norm_ffn_post"]
    shaped = {n: tuple(t.reshape(s.shape) for t in res[n])
              for n, s in dict(norm_mix_pre=norm_mix_pre, w_in=w_in, w_pool=w_pool, pool_scale=pool_scale, attn_scale=attn_scale,
                               w_out=w_out, norm_mix_post=norm_mix_post, norm_ffn_pre=norm_ffn_pre, w_up=w_up, conv_w=conv_w,
                               conv_b=conv_b, w_down=w_down, norm_ffn_post=norm_ffn_post).items()}
    outs = [loss, r["dx"].reshape(x.shape)]
    for k in range(4):
        outs += [shaped[n][k] for n in order]
    return tuple(outs)
```

```python
import functools

import jax
import jax.numpy as jnp
from jax import lax
from jax.experimental import pallas as pl
from jax.experimental.pallas import tpu as pltpu

F32 = jnp.float32
BF16 = jnp.bfloat16
HIGHEST = lax.Precision.HIGHEST

N_DEV = 8
EPS = 1e-6
POOL_WINDOWS = (2, 4, 8, 16)
POOL_GROUP = 128
HALO = 16
HEAD_DIM = 64
QB = 128
ATTN_SCALE = HEAD_DIM ** -0.5
ATTN_SUBS = 2
EXP_UNDERFLOW = -88.0
D_FF_SHARDS = 4

ADAM_LR = 0.001
ADAM_B1 = 0.9
ADAM_B2 = 0.999
ADAM_EPS = 1e-08
ADAM_WD = 0.01
ADAM_STEP = 10

VMEM_LIMIT_V7X = 56 * 1024 * 1024
MESH = pl.DeviceIdType.MESH


def _params(*semantics):
    return pltpu.CompilerParams(dimension_semantics=semantics, vmem_limit_bytes=VMEM_LIMIT_V7X)


def _const(shape):
    zeros = (0,) * len(shape)
    return pl.BlockSpec(shape, lambda *_: zeros, pipeline_mode=pl.Buffered(1))


def _dot(a, b):
    return jnp.dot(a, b, preferred_element_type=F32)


def _dot_nt(a, b):
    return lax.dot_general(a, b, (((1,), (1,)), ((), ())), preferred_element_type=F32)


def _dot_tn(a, b):
    return lax.dot_general(a, b, (((0,), (0,)), ((), ())), preferred_element_type=F32)


def _rms(v):
    return lax.rsqrt(jnp.mean(v * v, axis=-1, keepdims=True) + EPS)


def _norm_bwd(dn_times_gain, n, r):
    return r * (dn_times_gain - n * jnp.mean(dn_times_gain * n, axis=-1, keepdims=True))


def _zero_when(first, *refs):
    @pl.when(first)
    def _():
        for ref in refs:
            ref[...] = jnp.zeros_like(ref)


def _colsum(v):
    return jnp.sum(v, axis=0, keepdims=True)


def _grid_ends(grid):
    ids = [pl.program_id(a) for a in range(len(grid))]
    first = functools.reduce(jnp.logical_and, [i == 0 for i in ids])
    last = functools.reduce(jnp.logical_and, [i == n - 1 for i, n in zip(ids, grid)])
    return first, last


def _fwd_inproj(x, g1, w_in_g, tile):
    S, D = x.shape
    nb, _, cs = w_in_g.shape
    d_pool = 2 * cs

    def body(x_ref, g_ref, w_ref, ht_ref, u_ref, qkv_ref):
        xf = x_ref[...]
        h = (xf * _rms(xf) * g_ref[...]).astype(BF16)
        ht_ref[...] = h.T
        for d in range(nb):
            o = _dot(h, w_ref[d])
            if d < 2:
                u_ref[:, d * cs:(d + 1) * cs] = o
            else:
                qkv_ref[:, (d - 2) * cs:(d - 1) * cs] = o.astype(BF16)

    return pl.pallas_call(
        body, name="fwd_inproj", grid=(S // tile,),
        in_specs=[pl.BlockSpec((tile, D), lambda i: (i, 0)), _const((1, D)), _const(w_in_g.shape)],
        out_specs=[pl.BlockSpec((D, tile), lambda i: (0, i)), pl.BlockSpec((tile, d_pool), lambda i: (i, 0)),
                   pl.BlockSpec((tile, 3 * d_pool), lambda i: (i, 0))],
        out_shape=[jax.ShapeDtypeStruct((D, S), BF16), jax.ShapeDtypeStruct((S, d_pool), F32),
                   jax.ShapeDtypeStruct((S, 3 * d_pool), BF16)],
        compiler_params=_params("parallel"),
    )(x, g1, w_in_g)


def _window_sums(ext, forward):
    n = ext.shape[0]
    sums, s, sh = {}, ext, 1
    while sh < POOL_WINDOWS[-1]:
        s = s + pltpu.roll(s, (n - sh) if forward else sh, axis=0)
        sh *= 2
        sums[sh] = s
    return sums


def _pool_counts(t0, rows):
    t1 = (lax.broadcasted_iota(jnp.int32, (rows, 1), 0) + t0 + 1).astype(F32)
    return [jnp.minimum(t1, float(w)) for w in POOL_WINDOWS]


def _pool_deviation(u, halo, t0):
    T = u.shape[0]
    sums = _window_sums(jnp.concatenate([halo, u], axis=0), forward=False)
    counts = _pool_counts(t0, T)
    parts = []
    for g, w in enumerate(POOL_WINDOWS):
        lanes = slice(g * POOL_GROUP, (g + 1) * POOL_GROUP)
        parts.append(sums[w][HALO:, lanes] / counts[g] - u[:, lanes])
    return parts


def _prev_halo_spec(tile, width):
    return pl.BlockSpec((HALO, width), lambda i: (jnp.maximum(i * (tile // HALO) - 1, 0), 0))


def _next_halo_spec(tile, width, n_tiles):
    last = n_tiles * (tile // HALO) - 1
    return pl.BlockSpec((HALO, width), lambda i: (jnp.minimum((i + 1) * (tile // HALO), last), 0))


def _fwd_pool(u, w_pool, tile):
    S, C = u.shape

    def body(u_ref, halo_ref, wp_ref, o_ref):
        i = pl.program_id(0)
        halo = jnp.where(i > 0, halo_ref[...], 0.0)
        parts = _pool_deviation(u_ref[...], halo, i * tile)
        for g, p in enumerate(parts):
            o_ref[:, g * POOL_GROUP:(g + 1) * POOL_GROUP] = _dot(p.astype(BF16), wp_ref[g].astype(BF16))

    return pl.pallas_call(
        body, name="fwd_pool", grid=(S // tile,),
        in_specs=[pl.BlockSpec((tile, C), lambda i: (i, 0)), _prev_halo_spec(tile, C), _const(w_pool.shape)],
        out_specs=pl.BlockSpec((tile, C), lambda i: (i, 0)),
        out_shape=jax.ShapeDtypeStruct((S, C), F32),
        compiler_params=_params("parallel"),
    )(u, u, w_pool)


def _low_lanes():
    return lax.broadcasted_iota(jnp.int32, (QB, 2 * HEAD_DIM), 1) < HEAD_DIM


def _triangle(inclusive):
    row = lax.broadcasted_iota(jnp.int32, (QB, QB), 0)
    col = lax.broadcasted_iota(jnp.int32, (QB, QB), 1)
    return ((row >= col) if inclusive else (row > col)).astype(F32)


def _causal_mask(width, offset):
    row = lax.broadcasted_iota(jnp.int32, (QB, width), 0)
    col = lax.broadcasted_iota(jnp.int32, (QB, width), 1)
    return col < row + offset


def _suffix_sums(vals, tri, carry):
    n = vals.shape[1] // QB
    out, run = [None] * n, carry
    for b in reversed(range(n)):
        blk = vals[:, b * QB:(b + 1) * QB]
        out[b] = jnp.dot(blk, tri, precision=HIGHEST, preferred_element_type=F32) + run
        run = run + jnp.sum(blk, axis=1, keepdims=True)
    return (out[0] if n == 1 else jnp.concatenate(out, axis=1)), run


def _attn_tile(qh, kw, mask, carry, after_s):
    z = _dot_nt(qh, kw) * ATTN_SCALE
    e = jnp.exp(-jnp.abs(z))
    softplus = jnp.maximum(z, 0.0) + jnp.log(1.0 + e)
    log_1m_beta = -softplus if mask is None else jnp.where(mask, -softplus, 0.0)
    stick, carry = _suffix_sums(log_1m_beta, after_s, carry)
    a = jnp.exp(z - softplus + stick)
    if mask is not None:
        a = jnp.where(mask, a, 0.0)
    return z, e, a, carry


def _split_heads(v, low_lanes):
    return jnp.where(low_lanes, v, 0.0).astype(BF16), jnp.where(low_lanes, 0.0, v).astype(BF16)


def _sweep_done(c0, c1):
    return (jnp.maximum(jnp.max(c0), jnp.max(c1)) < EXP_UNDERFLOW).astype(jnp.int32)


def _first_window(i):
    first_blk = jnp.maximum(i - 1, 0)
    return first_blk, pl.multiple_of(first_blk * QB, QB), (i - first_blk) * QB


def _fwd_attn(qkv, n_pairs, ex):
    S = qkv.shape[0]
    n_steps = S // (ATTN_SUBS * QB)

    def body(q_ref, k_ref, v_ref, *rest):
        o_ref = rest[ex.n]
        ex_refs = ex.split(rest[:ex.n] + rest[ex.n + 1:])
        first_step, last_step = _grid_ends((n_pairs, n_steps))

        @pl.when(first_step)
        def _():
            ex.start(*ex_refs)

        low_lanes = _low_lanes()
        after_s = _triangle(False)
        zero = jnp.zeros((QB, 1), F32)

        def cond(c):
            return jnp.logical_and(c[0] >= 0, c[1] == 0)

        started = []
        for sub in range(ATTN_SUBS):
            i = pl.program_id(1) * ATTN_SUBS + sub
            qh = _split_heads(q_ref[sub * QB:(sub + 1) * QB, :].astype(F32), low_lanes)
            first_blk, start, offset = _first_window(i)
            kw = k_ref[pl.ds(start, 2 * QB), :]
            vw = v_ref[pl.ds(start, 2 * QB), :]
            mask = _causal_mask(2 * QB, offset)
            outs, carries = [], []
            for h in range(2):
                _, _, a, c = _attn_tile(qh[h], kw, mask, zero, after_s)
                outs.append(_dot(a.astype(BF16), vw))
                carries.append(c)
            started.append((qh, first_blk, jnp.where(low_lanes, outs[0], outs[1]), carries))

        for sub, (qh, first_blk, acc, carries) in enumerate(started):
            def step(c, qh=qh):
                j, _, acc, c0, c1 = c
                at = pl.multiple_of(j * QB, QB)
                kb = k_ref[pl.ds(at, QB), :]
                vb = v_ref[pl.ds(at, QB), :]
                cs, outs = [c0, c1], []
                for h in range(2):
                    _, _, a, cs[h] = _attn_tile(qh[h], kb, None, cs[h], after_s)
                    outs.append(_dot(a.astype(BF16), vb))
                acc = acc + jnp.where(low_lanes, outs[0], outs[1])
                return j - 1, _sweep_done(*cs), acc, cs[0], cs[1]

            init = (first_blk - 1, _sweep_done(*carries), acc, carries[0], carries[1])
            o_ref[sub * QB:(sub + 1) * QB, :] = lax.while_loop(cond, step, init)[2]

        @pl.when(last_step)
        def _():
            ex.wait(*ex_refs)

    outs = pl.pallas_call(
        body, name="fwd_attn", grid=(n_pairs, n_steps),
        in_specs=[pl.BlockSpec((ATTN_SUBS * QB, QB), lambda p, i: (i, p)),
                  pl.BlockSpec((S, QB), lambda p, i: (0, n_pairs + p), pipeline_mode=pl.Buffered(1)),
                  pl.BlockSpec((S, QB), lambda p, i: (0, 2 * n_pairs + p), pipeline_mode=pl.Buffered(1))] + ex.specs,
        out_specs=[pl.BlockSpec((ATTN_SUBS * QB, QB), lambda p, i: (i, p))] + ex.specs,
        out_shape=[jax.ShapeDtypeStruct((S, n_pairs * QB), F32)] + ex.out_shape,
        scratch_shapes=ex.scratch,
        compiler_params=_params("arbitrary", "arbitrary"),
    )(qkv, qkv, qkv, *ex.arrays)
    return outs[0], outs[1:]


def _normalized_heads(pool_out, attn_out):
    rp, ra = _rms(pool_out), _rms(attn_out)
    return pool_out * rp, rp, attn_out * ra, ra


def _fwd_outproj(pool_out, attn_out, pool_scale, attn_scale, w_out, x, g2, g3, tile):
    S, D = x.shape
    C = pool_out.shape[1]

    def body(p_ref, a_ref, ps_ref, as_ref, w_ref, x_ref, g2_ref, g3_ref, mix_ref, x2_ref, h2_ref, h2t_ref):
        n_p, _, n_a, _ = _normalized_heads(p_ref[...], a_ref[...])
        mix = _dot((n_p * ps_ref[...]).astype(BF16), w_ref[:C, :]) + _dot((n_a * as_ref[...]).astype(BF16), w_ref[C:, :])
        mix_ref[...] = mix
        x2 = x_ref[...] + mix * _rms(mix) * g2_ref[...]
        x2_ref[...] = x2
        h2 = (x2 * _rms(x2) * g3_ref[...]).astype(BF16)
        h2_ref[...] = h2
        h2t_ref[...] = h2.T

    row = lambda w: pl.BlockSpec((tile, w), lambda i: (i, 0))
    return pl.pallas_call(
        body, name="fwd_outproj", grid=(S // tile,),
        in_specs=[row(C), row(C), _const((1, C)), _const((1, C)), _const(w_out.shape), row(D), _const((1, D)), _const((1, D))],
        out_specs=[row(D), row(D), row(D), pl.BlockSpec((D, tile), lambda i: (0, i))],
        out_shape=[jax.ShapeDtypeStruct((S, D), F32), jax.ShapeDtypeStruct((S, D), F32), jax.ShapeDtypeStruct((S, D), BF16),
                   jax.ShapeDtypeStruct((D, S), BF16)],
        compiler_params=_params("parallel"),
    )(pool_out, attn_out, pool_scale, attn_scale, w_out, x, g2, g3)


def _fwd_up(h2, w_up_g, tile):
    S, D = h2.shape
    nb, _, cs = w_up_g.shape

    def body(h_ref, w_ref, o_ref):
        h = h_ref[...]
        for d in range(nb):
            o_ref[d] = _dot(h, w_ref[d]).astype(BF16)

    return pl.pallas_call(
        body, name="fwd_up", grid=(S // tile,),
        in_specs=[pl.BlockSpec((tile, D), lambda i: (i, 0)), _const(w_up_g.shape)],
        out_specs=pl.BlockSpec((nb, tile, cs), lambda i: (0, i, 0)),
        out_shape=jax.ShapeDtypeStruct((nb, S, cs), BF16),
        compiler_params=_params("parallel"),
    )(h2, w_up_g)


def _conv_taps(tile_rows, halo_rows):
    T = tile_rows.shape[0]
    ext = jnp.concatenate([halo_rows.astype(F32), tile_rows.astype(F32)], axis=0)
    return pltpu.roll(ext, 2, axis=0)[HALO:], pltpu.roll(ext, 1, axis=0)[HALO:], ext[HALO:]


def _tap_rows(cw_ref, d):
    return [cw_ref[d, k:k + 1, :] for k in range(3)]


def _gated_unit(taps_gate, taps_val, cw_gate, cw_val, cb_gate, cb_val):
    gate = cw_gate[0] * taps_gate[0] + cw_gate[1] * taps_gate[1] + cw_gate[2] * taps_gate[2] + cb_gate
    val = cw_val[0] * taps_val[0] + cw_val[1] * taps_val[1] + cw_val[2] * taps_val[2] + cb_val
    sig = 1.0 / (1.0 + jnp.exp(-gate))
    return gate, val, sig


def _fwd_ffn_loss(upre, conv_w_g, conv_b_g, w_down4, x2, target, g4, tile):
    nb, S, cs = upre.shape
    D = x2.shape[1]

    def body(u_ref, halo_ref, cw_ref, cb_ref, wd_ref, x2_ref, t_ref, g4_ref, dy_ref, df_ref, loss_ref, dg4_ref):
        i = pl.program_id(0)
        first = i == 0
        _zero_when(first, loss_ref, dg4_ref)
        f = jnp.zeros((tile, D), F32)
        for s in range(D_FF_SHARDS):
            halo_g = jnp.where(first, jnp.zeros_like(halo_ref[s]), halo_ref[s])
            halo_v = jnp.where(first, jnp.zeros_like(halo_ref[s]), halo_ref[s + D_FF_SHARDS])
            gate, val, sig = _gated_unit(_conv_taps(u_ref[s], halo_g), _conv_taps(u_ref[s + D_FF_SHARDS], halo_v),
                                         _tap_rows(cw_ref, s), _tap_rows(cw_ref, s + D_FF_SHARDS), cb_ref[s], cb_ref[s + D_FF_SHARDS])
            f = f + _dot((gate * sig * val).astype(BF16), wd_ref[s])
        r4 = _rms(f)
        n4 = f * r4
        err = x2_ref[...] + n4 * g4_ref[...] - t_ref[...]
        dy = err * (1.0 / D)
        dy_ref[...] = dy
        df_ref[...] = _norm_bwd(dy * g4_ref[...], n4, r4).astype(BF16)
        loss_ref[...] += _colsum(err * err)
        dg4_ref[...] += _colsum(dy * n4)

    row = lambda w: pl.BlockSpec((tile, w), lambda i: (i, 0))
    return pl.pallas_call(
        body, name="fwd_ffn_loss", grid=(S // tile,),
        in_specs=[pl.BlockSpec((nb, tile, cs), lambda i: (0, i, 0)),
                  pl.BlockSpec((nb, HALO, cs), lambda i: (0, jnp.maximum(i * (tile // HALO) - 1, 0), 0)),
                  _const(conv_w_g.shape), _const(conv_b_g.shape), _const(w_down4.shape), row(D), row(D), _const((1, D))],
        out_specs=[row(D), row(D), pl.BlockSpec((1, D), lambda i: (0, 0)), pl.BlockSpec((1, D), lambda i: (0, 0))],
        out_shape=[jax.ShapeDtypeStruct((S, D), F32), jax.ShapeDtypeStruct((S, D), BF16),
                   jax.ShapeDtypeStruct((1, D), F32), jax.ShapeDtypeStruct((1, D), F32)],
        compiler_params=_params("arbitrary"),
    )(upre, upre, conv_w_g, conv_b_g, w_down4, x2, target, g4)


def _bwd_down(upre, conv_w_g, conv_b_g, w_down4, df, tile):
    nb, S, cs = upre.shape
    D = df.shape[1]
    n_tiles = S // tile

    def body(ug_ref, uv_ref, hg_ref, hv_ref, cwg_ref, cwv_ref, cbg_ref, cbv_ref, wd_ref, df_ref,
             dg_ref, dv_ref, dwd_ref, dbg_ref, dbv_ref, dcwg_ref, dcwv_ref):
        i = pl.program_id(1)
        first = i == 0
        _zero_when(first, dwd_ref, dbg_ref, dbv_ref, dcwg_ref, dcwv_ref)
        halo_g = jnp.where(first, jnp.zeros_like(hg_ref[0]), hg_ref[0])
        halo_v = jnp.where(first, jnp.zeros_like(hv_ref[0]), hv_ref[0])
        taps_g, taps_v = _conv_taps(ug_ref[0], halo_g), _conv_taps(uv_ref[0], halo_v)
        gate, val, sig = _gated_unit(taps_g, taps_v, _tap_rows(cwg_ref, 0), _tap_rows(cwv_ref, 0), cbg_ref[0], cbv_ref[0])
        silu = gate * sig
        dfb = df_ref[...]
        dact = _dot_nt(dfb, wd_ref[0])
        dwd_ref[0] += _dot_tn((silu * val).astype(BF16), dfb)
        dgate = dact * val * (sig * (1.0 + gate * (1.0 - sig)))
        dval = dact * silu
        dg_ref[0] = dgate.astype(BF16)
        dv_ref[0] = dval.astype(BF16)
        dbg_ref[0] += _colsum(dgate)
        dbv_ref[0] += _colsum(dval)
        for k in range(3):
            dcwg_ref[0, k:k + 1, :] += _colsum(dgate * taps_g[k])
            dcwv_ref[0, k:k + 1, :] += _colsum(dval * taps_v[k])

    half = D_FF_SHARDS
    blk = lambda off: pl.BlockSpec((1, tile, cs), lambda s, i: (s + off, i, 0))
    halo = lambda off: pl.BlockSpec((1, HALO, cs), lambda s, i: (s + off, jnp.maximum(i * (tile // HALO) - 1, 0), 0))
    par = lambda off, r: pl.BlockSpec((1, r, cs), lambda s, i: (s + off, 0, 0))
    outs = pl.pallas_call(
        body, name="bwd_down", grid=(half, n_tiles),
        in_specs=[blk(0), blk(half), halo(0), halo(half), par(0, 3), par(half, 3), par(0, 1), par(half, 1),
                  pl.BlockSpec((1, cs, D), lambda s, i: (s, 0, 0)), pl.BlockSpec((tile, D), lambda s, i: (i, 0))],
        out_specs=[blk(0), blk(0), pl.BlockSpec((1, cs, D), lambda s, i: (s, 0, 0)),
                   par(0, 1), par(0, 1), par(0, 3), par(0, 3)],
        out_shape=[jax.ShapeDtypeStruct((half, S, cs), BF16), jax.ShapeDtypeStruct((half, S, cs), BF16),
                   jax.ShapeDtypeStruct((half, cs, D), F32),
                   jax.ShapeDtypeStruct((half, 1, cs), F32), jax.ShapeDtypeStruct((half, 1, cs), F32),
                   jax.ShapeDtypeStruct((half, 3, cs), F32), jax.ShapeDtypeStruct((half, 3, cs), F32)],
        compiler_params=_params("parallel", "arbitrary"),
    )(upre, upre, upre, upre, conv_w_g, conv_w_g, conv_b_g, conv_b_g, w_down4, df)
    dgate, dval, d_wd, dbg, dbv, dcwg, dcwv = outs
    return dgate, dval, d_wd, jnp.concatenate([dbg, dbv], axis=0), jnp.concatenate([dcwg, dcwv], axis=0)


def _bwd_up_x(dgate, dval, conv_w_g, w_up_g, x2, dy, mix, g2, g3, tile):
    half, S, cs = dgate.shape
    nb = 2 * half
    D = x2.shape[1]
    n_tiles = S // tile

    def body(dg_ref, dv_ref, hg_ref, hv_ref, cw_ref, w_ref, x2_ref, dy_ref, mix_ref, g2_ref, g3_ref,
             dupre_ref, dx2_ref, dmix_ref, dg3_ref, dg2_ref):
        i = pl.program_id(0)
        last = i == n_tiles - 1
        _zero_when(i == 0, dg3_ref, dg2_ref)
        dh2 = jnp.zeros((tile, D), F32)
        for d in range(nb):
            src, halo = (dg_ref, hg_ref) if d < half else (dv_ref, hv_ref)
            nxt = jnp.where(last, jnp.zeros_like(halo[d % half]), halo[d % half])
            ext = jnp.concatenate([src[d % half].astype(F32), nxt.astype(F32)], axis=0)
            n = ext.shape[0]
            cw = _tap_rows(cw_ref, d)
            dupre = (cw[2] * ext + cw[1] * pltpu.roll(ext, n - 1, axis=0) + cw[0] * pltpu.roll(ext, n - 2, axis=0))[:tile]
            dupre = dupre.astype(BF16)
            dupre_ref[d] = dupre
            dh2 = dh2 + _dot_nt(dupre, w_ref[d])
        x2 = x2_ref[...]
        r3 = _rms(x2)
        n3 = x2 * r3
        dg3_ref[...] += _colsum(dh2 * n3)
        dx2 = dy_ref[...] + _norm_bwd(dh2 * g3_ref[...], n3, r3)
        dx2_ref[...] = dx2
        mix = mix_ref[...]
        r2 = _rms(mix)
        n2 = mix * r2
        dg2_ref[...] += _colsum(dx2 * n2)
        dmix_ref[...] = _norm_bwd(dx2 * g2_ref[...], n2, r2).astype(BF16)

    row = lambda w: pl.BlockSpec((tile, w), lambda i: (i, 0))
    blk = pl.BlockSpec((half, tile, cs), lambda i: (0, i, 0))
    last_halo = n_tiles * (tile // HALO) - 1
    halo = pl.BlockSpec((half, HALO, cs), lambda i: (0, jnp.minimum((i + 1) * (tile // HALO), last_halo), 0))
    acc = pl.BlockSpec((1, D), lambda i: (0, 0))
    return pl.pallas_call(
        body, name="bwd_up_x", grid=(n_tiles,),
        in_specs=[blk, blk, halo, halo, _const(conv_w_g.shape), _const(w_up_g.shape), row(D), row(D), row(D),
                  _const((1, D)), _const((1, D))],
        out_specs=[pl.BlockSpec((nb, tile, cs), lambda i: (0, i, 0)), row(D), row(D), acc, acc],
        out_shape=[jax.ShapeDtypeStruct((nb, S, cs), BF16), jax.ShapeDtypeStruct((S, D), F32),
                   jax.ShapeDtypeStruct((S, D), BF16), jax.ShapeDtypeStruct((1, D), F32), jax.ShapeDtypeStruct((1, D), F32)],
        compiler_params=_params("arbitrary"),
    )(dgate, dval, dgate, dval, conv_w_g, w_up_g, x2, dy, mix, g2, g3)


def _bwd_weight(act_t, dout, tile):
    D, S = act_t.shape
    nb, _, cs = dout.shape

    def body(a_ref, d_ref, o_ref):
        _zero_when(pl.program_id(1) == 0, o_ref)
        o_ref[0] += _dot(a_ref[...], d_ref[0])

    return pl.pallas_call(
        body, name="bwd_w_up", grid=(nb, S // tile),
        in_specs=[pl.BlockSpec((D, tile), lambda d, i: (0, i)), pl.BlockSpec((1, tile, cs), lambda d, i: (d, i, 0))],
        out_specs=pl.BlockSpec((1, D, cs), lambda d, i: (d, 0, 0)),
        out_shape=jax.ShapeDtypeStruct((nb, D, cs), F32),
        compiler_params=_params("parallel", "arbitrary"),
    )(act_t, dout)


def _bwd_outproj(dmix, w_out, pool_out, attn_out, pool_scale, attn_scale, tile):
    S, D = dmix.shape
    C = pool_out.shape[1]

    def body(dm_ref, w_ref, p_ref, a_ref, ps_ref, as_ref, dp_ref, da_ref, dw_ref, dps_ref, das_ref):
        _zero_when(pl.program_id(0) == 0, dw_ref, dps_ref, das_ref)
        dmx = dm_ref[...]
        dmerged = _dot_nt(dmx, w_ref[...])
        n_p, r_p, n_a, r_a = _normalized_heads(p_ref[...], a_ref[...])
        merged = jnp.concatenate([(n_p * ps_ref[...]).astype(BF16), (n_a * as_ref[...]).astype(BF16)], axis=1)
        dw_ref[...] += _dot_tn(merged, dmx)
        dm_p, dm_a = dmerged[:, :C], dmerged[:, C:]
        dps_ref[...] += _colsum(dm_p * n_p)
        das_ref[...] += _colsum(dm_a * n_a)
        dp_ref[...] = _norm_bwd(dm_p * ps_ref[...], n_p, r_p)
        da_ref[...] = _norm_bwd(dm_a * as_ref[...], n_a, r_a)

    row = lambda w: pl.BlockSpec((tile, w), lambda i: (i, 0))
    return pl.pallas_call(
        body, name="bwd_outproj", grid=(S // tile,),
        in_specs=[row(D), _const(w_out.shape), row(C), row(C), _const((1, C)), _const((1, C))],
        out_specs=[row(C), row(C), pl.BlockSpec(w_out.shape, lambda i: (0, 0)),
                   pl.BlockSpec((1, C), lambda i: (0, 0)), pl.BlockSpec((1, C), lambda i: (0, 0))],
        out_shape=[jax.ShapeDtypeStruct((S, C), F32), jax.ShapeDtypeStruct((S, C), F32),
                   jax.ShapeDtypeStruct(w_out.shape, F32), jax.ShapeDtypeStruct((1, C), F32), jax.ShapeDtypeStruct((1, C), F32)],
        compiler_params=_params("arbitrary"),
    )(dmix, w_out, pool_out, attn_out, pool_scale, attn_scale)


def _bwd_attn(qkv, d_attn, n_pairs, ex):
    S = qkv.shape[0]
    n_steps = S // (ATTN_SUBS * QB)

    def body(q_ref, k_ref, v_ref, do_ref, *rest):
        dq_ref, dk_ref, dv_ref = rest[ex.n:ex.n + 3]
        ex_refs = ex.split(rest[:ex.n] + rest[ex.n + 3:])
        first_step, last_step = _grid_ends((n_pairs, n_steps))

        @pl.when(first_step)
        def _():
            ex.start(*ex_refs)

        @pl.when(pl.program_id(1) == 0)
        def _():
            dk_ref[...] = jnp.zeros_like(dk_ref)
            dv_ref[...] = jnp.zeros_like(dv_ref)

        low_lanes = _low_lanes()
        after_s, from_s = _triangle(False), _triangle(True)
        zero = jnp.zeros((QB, 1), F32)

        def tile(qh, doh, total, kw, vw, mask, cs, gs):
            first = mask is not None
            dqs = []
            dk = jnp.zeros((kw.shape[0], QB), F32)
            dv = jnp.zeros((kw.shape[0], QB), F32)
            for h in range(2):
                z, e, a, cs[h] = _attn_tile(qh[h], kw, mask, cs[h], after_s)
                g = a * _dot_nt(doh[h], vw)
                nearer, gs[h] = _suffix_sums(g, from_s, gs[h])
                if first:
                    total[h] = total[h] + gs[h]
                beyond = total[h] - nearer
                inv = 1.0 / (1.0 + e)
                sig_abs, sig_neg = inv, e * inv
                pos = z >= 0.0
                sig_z = jnp.where(pos, sig_abs, sig_neg)
                sig_mz = jnp.where(pos, sig_neg, sig_abs)
                dz = g * sig_mz - sig_z * beyond
                if mask is not None:
                    dz = jnp.where(mask, dz, 0.0)
                dz = (dz * ATTN_SCALE).astype(BF16)
                dqs.append(_dot(dz, kw))
                dk = dk + _dot_tn(dz, qh[h])
                dv = dv + _dot_tn(a.astype(BF16), doh[h])
            return jnp.where(low_lanes, dqs[0], dqs[1]), dk, dv

        def cond(c):
            return jnp.logical_and(c[0] >= 0, c[1] == 0)

        windows = []
        for sub in range(ATTN_SUBS):
            i = pl.program_id(1) * ATTN_SUBS + sub
            rows = slice(sub * QB, (sub + 1) * QB)
            qh = _split_heads(q_ref[rows, :].astype(F32), low_lanes)
            doh = _split_heads(do_ref[rows, :], low_lanes)
            first_blk, start, offset = _first_window(i)
            mask = _causal_mask(2 * QB, offset)
            kw = k_ref[pl.ds(start, 2 * QB), :]
            cs = [_attn_tile(qh[h], kw, mask, zero, after_s)[3] for h in range(2)]

            def far_sums(c, qh=qh, doh=doh):
                j, _, c0, c1, r0, r1 = c
                at = pl.multiple_of(j * QB, QB)
                kb = k_ref[pl.ds(at, QB), :]
                vb = v_ref[pl.ds(at, QB), :]
                cs, rs = [c0, c1], [r0, r1]
                for h in range(2):
                    _, _, a, cs[h] = _attn_tile(qh[h], kb, None, cs[h], after_s)
                    rs[h] = rs[h] + jnp.sum(a * _dot_nt(doh[h], vb), axis=1, keepdims=True)
                return j - 1, _sweep_done(*cs), cs[0], cs[1], rs[0], rs[1]

            far = lax.while_loop(cond, far_sums, (first_blk - 1, _sweep_done(*cs), cs[0], cs[1], zero, zero))
            windows.append((qh, doh, first_blk, start, mask, kw, [far[4], far[5]]))

        started = []
        for qh, doh, first_blk, start, mask, kw, total in windows:
            cs, gs = [zero, zero], [zero, zero]
            dq, dk, dv = tile(qh, doh, total, kw, v_ref[pl.ds(start, 2 * QB), :], mask, cs, gs)
            started.append((qh, doh, total, first_blk, start, dq, dk, dv, cs, gs))

        for _, _, _, _, start, _, dk, dv, _, _ in started:
            dk_ref[pl.ds(start, 2 * QB), :] += dk
            dv_ref[pl.ds(start, 2 * QB), :] += dv

        for sub, (qh, doh, total, first_blk, _, dq, _, _, cs, gs) in enumerate(started):
            def step(c, qh=qh, doh=doh, total=total):
                j, _, dq, c0, c1, s0, s1 = c
                at = pl.multiple_of(j * QB, QB)
                cs, gs = [c0, c1], [s0, s1]
                dq_j, dk, dv = tile(qh, doh, total, k_ref[pl.ds(at, QB), :], v_ref[pl.ds(at, QB), :], None, cs, gs)
                dk_ref[pl.ds(at, QB), :] += dk
                dv_ref[pl.ds(at, QB), :] += dv
                return j - 1, _sweep_done(*cs), dq + dq_j, cs[0], cs[1], gs[0], gs[1]

            init = (first_blk - 1, _sweep_done(*cs), dq, cs[0], cs[1], gs[0], gs[1])
            dq_ref[sub * QB:(sub + 1) * QB, :] = lax.while_loop(cond, step, init)[2]

        @pl.when(last_step)
        def _():
            ex.wait(*ex_refs)

    blk = pl.BlockSpec((ATTN_SUBS * QB, QB), lambda p, i: (i, p))
    full = lambda off: pl.BlockSpec((S, QB), lambda p, i: (0, off + p), pipeline_mode=pl.Buffered(1))
    outs = pl.pallas_call(
        body, name="bwd_attn", grid=(n_pairs, n_steps),
        in_specs=[blk, full(n_pairs), full(2 * n_pairs), blk] + ex.specs,
        out_specs=[blk, pl.BlockSpec((S, QB), lambda p, i: (0, p)), pl.BlockSpec((S, QB), lambda p, i: (0, p))] + ex.specs,
        out_shape=[jax.ShapeDtypeStruct((S, n_pairs * QB), F32)] * 3 + ex.out_shape,
        scratch_shapes=ex.scratch,
        compiler_params=_params("arbitrary", "arbitrary"),
    )(qkv, qkv, qkv, d_attn, *ex.arrays)
    return outs[0], outs[1], outs[2], outs[3:]


def _bwd_pool(u, d_pool, w_pool, tile):
    S, C = u.shape
    n_tiles = S // tile
    ng = len(POOL_WINDOWS)

    def body(u_ref, uh_ref, d_ref, dh_ref, wp_ref, du_ref, dwp_ref):
        i = pl.program_id(0)
        first = i == 0
        _zero_when(first, dwp_ref)
        halo = jnp.where(first, 0.0, uh_ref[...])
        parts = _pool_deviation(u_ref[...], halo, i * tile)
        dout = d_ref[...]
        nxt = jnp.where(i == n_tiles - 1, 0.0, dh_ref[...])
        dext = jnp.concatenate([dout, nxt], axis=0).astype(BF16)
        counts = _pool_counts(i * tile, tile + HALO)
        dps, scaled = [], []
        for g in range(ng):
            lanes = slice(g * POOL_GROUP, (g + 1) * POOL_GROUP)
            dp = _dot_nt(dext[:, lanes], wp_ref[g].astype(BF16))
            dps.append(dp[:tile])
            scaled.append(dp / counts[g])
        sums = _window_sums(jnp.concatenate(scaled, axis=1), forward=True)
        for g, w in enumerate(POOL_WINDOWS):
            lanes = slice(g * POOL_GROUP, (g + 1) * POOL_GROUP)
            du_ref[:, lanes] = sums[w][:tile, lanes] - dps[g]
            dwp_ref[g] += _dot_tn(parts[g].astype(BF16), dext[:tile, lanes])

    row = pl.BlockSpec((tile, C), lambda i: (i, 0))
    return pl.pallas_call(
        body, name="bwd_pool", grid=(n_tiles,),
        in_specs=[row, _prev_halo_spec(tile, C), row, _next_halo_spec(tile, C, n_tiles), _const(w_pool.shape)],
        out_specs=[row, pl.BlockSpec(w_pool.shape, lambda i: (0, 0, 0))],
        out_shape=[jax.ShapeDtypeStruct((S, C), F32), jax.ShapeDtypeStruct(w_pool.shape, F32)],
        compiler_params=_params("arbitrary"),
    )(u, u, d_pool, d_pool, w_pool)


def _bwd_w_in(du, dq, dk, dv, h1_t, n_blocks, tile):
    D, S = h1_t.shape
    C = du.shape[1]
    cs = 4 * C // n_blocks
    per = C // cs

    def body(du_ref, dq_ref, dk_ref, dv_ref, ht_ref, dproj_ref, dw_ref):
        _zero_when(pl.program_id(0) == 0, dw_ref)
        ht = ht_ref[...]
        for d in range(n_blocks):
            src = (du_ref, dq_ref, dk_ref, dv_ref)[d // per]
            dproj = src[:, (d % per) * cs:(d % per + 1) * cs].astype(BF16)
            dproj_ref[:, d * cs:(d + 1) * cs] = dproj
            dw_ref[d] += _dot(ht, dproj)

    row = lambda w: pl.BlockSpec((tile, w), lambda i: (i, 0))
    return pl.pallas_call(
        body, name="bwd_w_in", grid=(S // tile,),
        in_specs=[row(C), row(C), row(C), row(C), pl.BlockSpec((D, tile), lambda i: (0, i))],
        out_specs=[row(4 * C), pl.BlockSpec((n_blocks, D, cs), lambda i: (0, 0, 0))],
        out_shape=[jax.ShapeDtypeStruct((S, 4 * C), BF16), jax.ShapeDtypeStruct((n_blocks, D, cs), F32)],
        compiler_params=_params("arbitrary"),
    )(du, dq, dk, dv, h1_t)


def _bwd_x(dproj, w_in_g, x, dx2, g1, tile, ex):
    S, D = x.shape
    nb, _, cs = w_in_g.shape
    n_tiles = S // tile

    def body(dp_ref, w_ref, x_ref, dx2_ref, g_ref, *rest):
        dx_ref, dg_ref = rest[ex.n:ex.n + 2]
        ex_refs = ex.split(rest[:ex.n] + rest[ex.n + 2:])
        first, last = _grid_ends((n_tiles,))

        @pl.when(first)
        def _():
            ex.start(*ex_refs)
            dg_ref[...] = jnp.zeros_like(dg_ref)

        dh = jnp.zeros((tile, D), F32)
        for d in range(nb):
            dh = dh + _dot_nt(dp_ref[:, d * cs:(d + 1) * cs], w_ref[d])
        xf = x_ref[...]
        r1 = _rms(xf)
        n1 = xf * r1
        dg_ref[...] += _colsum(dh * n1)
        dx_ref[...] = dx2_ref[...] + _norm_bwd(dh * g_ref[...], n1, r1)

        @pl.when(last)
        def _():
            ex.wait(*ex_refs)

    row = lambda w: pl.BlockSpec((tile, w), lambda i: (i, 0))
    outs = pl.pallas_call(
        body, name="bwd_x", grid=(n_tiles,),
        in_specs=[row(nb * cs), _const(w_in_g.shape), row(D), row(D), _const((1, D))] + ex.specs,
        out_specs=[row(D), pl.BlockSpec((1, D), lambda i: (0, 0))] + ex.specs,
        out_shape=[jax.ShapeDtypeStruct((S, D), F32), jax.ShapeDtypeStruct((1, D), F32)] + ex.out_shape,
        scratch_shapes=ex.scratch,
        compiler_params=_params("arbitrary"),
    )(dproj, w_in_g, x, dx2, g1, *ex.arrays)
    return outs[0], outs[1], outs[2:]


def _mesh_position():
    x, y, c = lax.axis_index("x"), lax.axis_index("y"), lax.axis_index("c")
    return x, y, c, 4 * x + 2 * y + c


def _peer(x, y, c, k):
    px = 1 - x if k & 4 else x
    py = 1 - y if k & 2 else y
    pc = 1 - c if k & 1 else c
    return (px, py, pc), 4 * px + 2 * py + pc


class _Exchange:
    def __init__(self, arrays, gather):
        self.arrays, self.gather, self.n = list(arrays), gather, len(arrays)
        self.out_shape = [jax.ShapeDtypeStruct(((N_DEV,) + a.shape) if gather else a.shape, a.dtype) for a in arrays]
        self.specs = [pl.BlockSpec(memory_space=pl.ANY)] * self.n
        copies = self.n * (N_DEV - 1)
        self.scratch = [pltpu.SemaphoreType.DMA((copies,)), pltpu.SemaphoreType.DMA((copies,)),
                        pltpu.SemaphoreType.DMA((self.n,))]

    def _copies(self, ins, outs, sems):
        send_sems, recv_sems, local_sems = sems
        x, y, c, me = _mesh_position()
        local, remote = [], []
        for a in range(self.n):
            mine = ins[a] if self.gather else ins[a].at[me]
            local.append(pltpu.make_async_copy(mine, outs[a].at[me], local_sems.at[a]))
            for k in range(1, N_DEV):
                peer, peer_idx = _peer(x, y, c, k)
                src = ins[a] if self.gather else ins[a].at[peer_idx]
                sem = a * (N_DEV - 1) + k - 1
                remote.append(pltpu.make_async_remote_copy(
                    src_ref=src, dst_ref=outs[a].at[me], send_sem=send_sems.at[sem], recv_sem=recv_sems.at[sem],
                    device_id=peer, device_id_type=MESH))
        return local, remote

    def start(self, ins, outs, sems):
        local, remote = self._copies(ins, outs, sems)
        for cp in local + remote:
            cp.start()

    def wait(self, ins, outs, sems):
        local, remote = self._copies(ins, outs, sems)
        for cp in remote:
            cp.wait_send()
        for cp in remote:
            cp.wait_recv()
        for cp in local:
            cp.wait()

    def split(self, refs):
        return refs[:self.n], refs[self.n:2 * self.n], refs[2 * self.n:]


def _all_to_all(arrays, gather, name):
    ex = _Exchange(arrays, gather)

    def body(*refs):
        ins, outs, sems = ex.split(refs)
        ex.start(ins, outs, sems)
        ex.wait(ins, outs, sems)

    return pl.pallas_call(body, name=name, in_specs=ex.specs, out_specs=ex.specs, out_shape=ex.out_shape,
                          scratch_shapes=ex.scratch)(*ex.arrays)


def _reduce_adamw(parts, w, m, v, rows):
    R, C = w.shape

    def body(p_ref, w_ref, m_ref, v_ref, g_ref, d_ref, nm_ref, nv_ref):
        g = p_ref[0].astype(F32)
        for s in range(1, N_DEV):
            g = g + p_ref[s].astype(F32)
        g_ref[...] = g
        m_new = ADAM_B1 * m_ref[...] + (1.0 - ADAM_B1) * g
        v_new = ADAM_B2 * v_ref[...] + (1.0 - ADAM_B2) * (g * g)
        m_hat = m_new / (1.0 - ADAM_B1 ** ADAM_STEP)
        v_hat = v_new / (1.0 - ADAM_B2 ** ADAM_STEP)
        d_ref[...] = -ADAM_LR * (m_hat / (jnp.sqrt(v_hat) + ADAM_EPS) + ADAM_WD * w_ref[...])
        nm_ref[...] = m_new
        nv_ref[...] = v_new

    row = pl.BlockSpec((rows, C), lambda i: (i, 0))
    return pl.pallas_call(
        body, name="reduce_adamw", grid=(R // rows,),
        in_specs=[pl.BlockSpec((N_DEV, rows, C), lambda i: (0, i, 0)), row, row, row],
        out_specs=[row] * 4, out_shape=[jax.ShapeDtypeStruct((R, C), F32)] * 4,
        compiler_params=_params("parallel"),
    )(parts, w, m, v)


def _row_tile(rows, cols):
    fits = [t for t in range(8, rows + 1, 8) if rows % t == 0 and N_DEV * t * cols * 4 <= 4 * 1024 * 1024]
    return max(fits) if fits else rows


SMALL_COLS = 1024


def _pack_small(vals):
    rows = []
    for a in vals:
        flat = a.reshape(-1)
        pad = (-flat.shape[0]) % SMALL_COLS
        rows.append(jnp.pad(flat, (0, pad)).reshape(-1, SMALL_COLS))
    packed = jnp.concatenate(rows, axis=0)
    return jnp.pad(packed, ((0, (-packed.shape[0]) % 8), (0, 0)))


def _unpack_small(packed, like):
    out, r = [], 0
    for a in like:
        n = a.size
        nr = -(-n // SMALL_COLS)
        out.append(packed[r:r + nr].reshape(-1)[:n].reshape(a.shape))
        r += nr
    return out


def kernel(x, norm_mix_pre, w_in, w_pool, pool_scale, attn_scale, w_out, norm_mix_post, norm_ffn_pre, w_up, conv_w, conv_b, w_down, norm_ffn_post, loss_target, m_norm_mix_pre, m_w_in, m_w_pool, m_pool_scale, m_attn_scale, m_w_out, m_norm_mix_post, m_norm_ffn_pre, m_w_up, m_conv_w, m_conv_b, m_w_down, m_norm_ffn_post, v_norm_mix_pre, v_w_in, v_w_pool, v_pool_scale, v_attn_scale, v_w_out, v_norm_mix_post, v_norm_ffn_pre, v_w_up, v_conv_w, v_conv_b, v_w_down, v_norm_ffn_post):
    S, D = x.shape[1], x.shape[2]
    d_ff_block = w_up.shape[2]

    xs, target = x[0], loss_target[0]
    g1, g2, g3, g4 = norm_mix_pre, norm_mix_post, norm_ffn_pre, norm_ffn_post
    big = min(512, S)
    small = min(256, S)
    n_pairs = pool_scale.shape[1] // QB
    conv_b_g = conv_b.reshape(N_DEV, 1, d_ff_block)

    (w_in_g,) = _all_to_all([w_in[0].astype(BF16)], gather=True, name="gather_w_in")
    h1_t, u, qkv = _fwd_inproj(xs, g1, w_in_g, big)
    pool_out = _fwd_pool(u, w_pool[0], big)
    attn_out, (w_out_g, w_up_g, w_down_g, conv_w_g) = _fwd_attn(
        qkv, n_pairs, _Exchange([w_out[0].astype(BF16), w_up[0].astype(BF16), w_down[0].astype(BF16), conv_w[0]], gather=True))
    w_out_full = w_out_g.reshape(D, D)
    w_down4 = w_down_g.reshape(D_FF_SHARDS, d_ff_block, D)
    mix, x2, h2, h2_t = _fwd_outproj(pool_out, attn_out, pool_scale, attn_scale, w_out_full, xs, g2, g3, big)
    upre = _fwd_up(h2, w_up_g, big)
    dy, df, loss_cols, dg4 = _fwd_ffn_loss(upre, conv_w_g, conv_b_g, w_down4, x2, target, g4, small)
    loss = lax.psum(0.5 * jnp.sum(loss_cols) / D, ("x", "y", "c"))

    dgate, dval, d_wd4, d_cb, d_cw = _bwd_down(upre, conv_w_g, conv_b_g, w_down4, df, big)
    dupre, dx2, dmix, dg3, dg2 = _bwd_up_x(dgate, dval, conv_w_g, w_up_g, x2, dy, mix, g2, g3, small)
    d_wup = _bwd_weight(h2_t, dupre, min(1024, S))
    d_pool, d_attn, d_wout, d_ps, d_as = _bwd_outproj(dmix, w_out_full, pool_out, attn_out, pool_scale, attn_scale, big)
    d_wdown_g = d_wd4.reshape(N_DEV, w_down.shape[1], D)
    dq, dk, dv, ffn_parts = _bwd_attn(qkv, d_attn, n_pairs, _Exchange([d_wup, d_wdown_g, d_cw], gather=False))
    du, d_wp = _bwd_pool(u, d_pool, w_pool[0], big)
    dproj, d_win = _bwd_w_in(du, dq, dk, dv, h1_t, N_DEV, big)
    d_wout_g = d_wout.reshape(N_DEV, D // N_DEV, D)
    dx, dg1, mix_parts = _bwd_x(dproj, w_in_g, xs, dx2, g1, big, _Exchange([d_win, d_wout_g], gather=False))
    big_parts = [mix_parts[0], mix_parts[1], ffn_parts[0], ffn_parts[1], ffn_parts[2]]
    r = dict(dx=dx, g1=dg1, w_pool=d_wp, pool_scale=d_ps, attn_scale=d_as, g2=dg2, g3=dg3, conv_b=d_cb, g4=dg4)

    small_names = ["norm_mix_pre", "w_pool", "pool_scale", "attn_scale", "norm_mix_post", "norm_ffn_pre", "conv_b", "norm_ffn_post"]
    small_w = dict(norm_mix_pre=norm_mix_pre, w_pool=w_pool, pool_scale=pool_scale, attn_scale=attn_scale,
                   norm_mix_post=norm_mix_post, norm_ffn_pre=norm_ffn_pre, conv_b=conv_b, norm_ffn_post=norm_ffn_post)
    small_m = dict(norm_mix_pre=m_norm_mix_pre, w_pool=m_w_pool, pool_scale=m_pool_scale, attn_scale=m_attn_scale,
                   norm_mix_post=m_norm_mix_post, norm_ffn_pre=m_norm_ffn_pre, conv_b=m_conv_b, norm_ffn_post=m_norm_ffn_post)
    small_v = dict(norm_mix_pre=v_norm_mix_pre, w_pool=v_w_pool, pool_scale=v_pool_scale, attn_scale=v_attn_scale,
                   norm_mix_post=v_norm_mix_post, norm_ffn_pre=v_norm_ffn_pre, conv_b=v_conv_b, norm_ffn_post=v_norm_ffn_post)
    small_g = dict(norm_mix_pre=r["g1"], w_pool=r["w_pool"], pool_scale=r["pool_scale"], attn_scale=r["attn_scale"],
                   norm_mix_post=r["g2"], norm_ffn_pre=r["g3"], conv_b=r["conv_b"], norm_ffn_post=r["g4"])
    like = [small_w[n] for n in small_names]
    packed_g = _pack_small([small_g[n] for n in small_names])

    (small_parts,) = _all_to_all([packed_g], gather=True, name="gather_small_grads")

    def update(parts, w, m, v):
        R, C = w.shape
        return _reduce_adamw(parts, w, m, v, _row_tile(R, C))

    res = {}
    res["w_in"] = update(big_parts[0], w_in[0], m_w_in[0], v_w_in[0])
    res["w_out"] = update(big_parts[1], w_out[0], m_w_out[0], v_w_out[0])
    res["w_up"] = update(big_parts[2], w_up[0], m_w_up[0], v_w_up[0])
    res["w_down"] = update(big_parts[3], w_down[0], m_w_down[0], v_w_down[0])
    res["conv_w"] = update(big_parts[4], conv_w[0], m_conv_w[0], v_conv_w[0])
    small_res = update(small_parts, _pack_small(like), _pack_small([small_m[n] for n in small_names]),
                       _pack_small([small_v[n] for n in small_names]))
    small_res = [_unpack_small(t, like) for t in small_res]
    for idx, n in enumerate(small_names):
        res[n] = tuple(t[idx] for t in small_res)

    order = ["norm_mix_pre", "w_in", "w_pool", "pool_scale", "attn_scale", "w_out", "norm_mix_post", "norm_ffn_pre",
             "w_up", "conv_w", "conv_b", "w_down", "norm_ffn_post"]
    shaped = {n: tuple(t.reshape(s.shape) for t in res[n])
              for n, s in dict(norm_mix_pre=norm_mix_pre, w_in=w_in, w_pool=w_pool, pool_scale=pool_scale, attn_scale=attn_scale,
                               w_out=w_out, norm_mix_post=norm_mix_post, norm_ffn_pre=norm_ffn_pre, w_up=w_up, conv_w=conv_w,
                               conv_b=conv_b, w_down=w_down, norm_ffn_post=norm_ffn_post).items()}
    outs = [loss, r["dx"].reshape(x.shape)]
    for k in range(4):
        outs += [shaped[n][k] for n in order]
    return tuple(outs)
```

```python
import functools

import jax
import jax.numpy as jnp
from jax import lax
from jax.experimental import pallas as pl
from jax.experimental.pallas import tpu as pltpu

F32 = jnp.float32
BF16 = jnp.bfloat16
HIGHEST = lax.Precision.HIGHEST

N_DEV = 8
EPS = 1e-6
POOL_WINDOWS = (2, 4, 8, 16)
POOL_GROUP = 128
HALO = 16
HEAD_DIM = 64
QB = 128
ATTN_SCALE = HEAD_DIM ** -0.5
ATTN_SUBS = 4
EXP_UNDERFLOW = -88.0
D_FF_SHARDS = 4

ADAM_LR = 0.001
ADAM_B1 = 0.9
ADAM_B2 = 0.999
ADAM_EPS = 1e-08
ADAM_WD = 0.01
ADAM_STEP = 10

VMEM_LIMIT_V7X = 56 * 1024 * 1024
MESH = pl.DeviceIdType.MESH


def _params(*semantics):
    return pltpu.CompilerParams(dimension_semantics=semantics, vmem_limit_bytes=VMEM_LIMIT_V7X)


def _const(shape):
    zeros = (0,) * len(shape)
    return pl.BlockSpec(shape, lambda *_: zeros, pipeline_mode=pl.Buffered(1))


def _dot(a, b):
    return jnp.dot(a, b, preferred_element_type=F32)


def _dot_nt(a, b):
    return lax.dot_general(a, b, (((1,), (1,)), ((), ())), preferred_element_type=F32)


def _dot_tn(a, b):
    return lax.dot_general(a, b, (((0,), (0,)), ((), ())), preferred_element_type=F32)


def _rms(v):
    return lax.rsqrt(jnp.mean(v * v, axis=-1, keepdims=True) + EPS)


def _norm_bwd(dn_times_gain, n, r):
    return r * (dn_times_gain - n * jnp.mean(dn_times_gain * n, axis=-1, keepdims=True))


def _zero_when(first, *refs):
    @pl.when(first)
    def _():
        for ref in refs:
            ref[...] = jnp.zeros_like(ref)


def _colsum(v):
    return jnp.sum(v, axis=0, keepdims=True)


def _grid_ends(grid):
    ids = [pl.program_id(a) for a in range(len(grid))]
    first = functools.reduce(jnp.logical_and, [i == 0 for i in ids])
    last = functools.reduce(jnp.logical_and, [i == n - 1 for i, n in zip(ids, grid)])
    return first, last


def _fwd_inproj(x, g1, w_in_g, tile):
    S, D = x.shape
    nb, _, cs = w_in_g.shape
    d_pool = 2 * cs

    def body(x_ref, g_ref, w_ref, ht_ref, u_ref, qkv_ref):
        xf = x_ref[...]
        h = (xf * _rms(xf) * g_ref[...]).astype(BF16)
        ht_ref[...] = h.T
        for d in range(nb):
            o = _dot(h, w_ref[d])
            if d < 2:
                u_ref[:, d * cs:(d + 1) * cs] = o
            else:
                qkv_ref[:, (d - 2) * cs:(d - 1) * cs] = o.astype(BF16)

    return pl.pallas_call(
        body, name="fwd_inproj", grid=(S // tile,),
        in_specs=[pl.BlockSpec((tile, D), lambda i: (i, 0)), _const((1, D)), _const(w_in_g.shape)],
        out_specs=[pl.BlockSpec((D, tile), lambda i: (0, i)), pl.BlockSpec((tile, d_pool), lambda i: (i, 0)),
                   pl.BlockSpec((tile, 3 * d_pool), lambda i: (i, 0))],
        out_shape=[jax.ShapeDtypeStruct((D, S), BF16), jax.ShapeDtypeStruct((S, d_pool), F32),
                   jax.ShapeDtypeStruct((S, 3 * d_pool), BF16)],
        compiler_params=_params("parallel"),
    )(x, g1, w_in_g)


def _window_sums(ext, forward):
    n = ext.shape[0]
    sums, s, sh = {}, ext, 1
    while sh < POOL_WINDOWS[-1]:
        s = s + pltpu.roll(s, (n - sh) if forward else sh, axis=0)
        sh *= 2
        sums[sh] = s
    return sums


def _pool_counts(t0, rows):
    t1 = (lax.broadcasted_iota(jnp.int32, (rows, 1), 0) + t0 + 1).astype(F32)
    return [jnp.minimum(t1, float(w)) for w in POOL_WINDOWS]


def _pool_deviation(u, halo, t0):
    T = u.shape[0]
    sums = _window_sums(jnp.concatenate([halo, u], axis=0), forward=False)
    counts = _pool_counts(t0, T)
    parts = []
    for g, w in enumerate(POOL_WINDOWS):
        lanes = slice(g * POOL_GROUP, (g + 1) * POOL_GROUP)
        parts.append(sums[w][HALO:, lanes] / counts[g] - u[:, lanes])
    return parts


def _prev_halo_spec(tile, width):
    return pl.BlockSpec((HALO, width), lambda i: (jnp.maximum(i * (tile // HALO) - 1, 0), 0))


def _next_halo_spec(tile, width, n_tiles):
    last = n_tiles * (tile // HALO) - 1
    return pl.BlockSpec((HALO, width), lambda i: (jnp.minimum((i + 1) * (tile // HALO), last), 0))


def _fwd_pool(u, w_pool, tile):
    S, C = u.shape

    def body(u_ref, halo_ref, wp_ref, o_ref):
        i = pl.program_id(0)
        halo = jnp.where(i > 0, halo_ref[...], 0.0)
        parts = _pool_deviation(u_ref[...], halo, i * tile)
        for g, p in enumerate(parts):
            o_ref[:, g * POOL_GROUP:(g + 1) * POOL_GROUP] = _dot(p.astype(BF16), wp_ref[g].astype(BF16))

    return pl.pallas_call(
        body, name="fwd_pool", grid=(S // tile,),
        in_specs=[pl.BlockSpec((tile, C), lambda i: (i, 0)), _prev_halo_spec(tile, C), _const(w_pool.shape)],
        out_specs=pl.BlockSpec((tile, C), lambda i: (i, 0)),
        out_shape=jax.ShapeDtypeStruct((S, C), F32),
        compiler_params=_params("parallel"),
    )(u, u, w_pool)


def _low_lanes():
    return lax.broadcasted_iota(jnp.int32, (QB, 2 * HEAD_DIM), 1) < HEAD_DIM


def _triangle(inclusive):
    row = lax.broadcasted_iota(jnp.int32, (QB, QB), 0)
    col = lax.broadcasted_iota(jnp.int32, (QB, QB), 1)
    return ((row >= col) if inclusive else (row > col)).astype(F32)


def _causal_mask(width, offset):
    row = lax.broadcasted_iota(jnp.int32, (QB, width), 0)
    col = lax.broadcasted_iota(jnp.int32, (QB, width), 1)
    return col < row + offset


def _suffix_sums(vals, tri, carry):
    n = vals.shape[1] // QB
    out, run = [None] * n, carry
    for b in reversed(range(n)):
        blk = vals[:, b * QB:(b + 1) * QB]
        out[b] = jnp.dot(blk, tri, precision=HIGHEST, preferred_element_type=F32) + run
        run = run + jnp.sum(blk, axis=1, keepdims=True)
    return (out[0] if n == 1 else jnp.concatenate(out, axis=1)), run


def _attn_tiles(qhs, kws, masks, carries, after_s):
    zs = [_dot_nt(qh, kw) * ATTN_SCALE for qh, kw in zip(qhs, kws)]
    es = [jnp.exp(-jnp.abs(z)) for z in zs]
    softplus = [jnp.maximum(z, 0.0) + jnp.log(1.0 + e) for z, e in zip(zs, es)]
    log_1m_beta = [-sp if m is None else jnp.where(m, -sp, 0.0) for sp, m in zip(softplus, masks)]
    sums = [_suffix_sums(l, after_s, c) for l, c in zip(log_1m_beta, carries)]
    weights = [jnp.exp(z - sp + st) for z, sp, (st, _) in zip(zs, softplus, sums)]
    weights = [a if m is None else jnp.where(m, a, 0.0) for a, m in zip(weights, masks)]
    return [(z, e, a, c) for z, e, a, (_, c) in zip(zs, es, weights, sums)]


def _attn_tile(qh, kw, mask, carry, after_s):
    return _attn_tiles([qh], [kw], [mask], [carry], after_s)[0]


def _split_heads(v, low_lanes):
    return jnp.where(low_lanes, v, 0.0).astype(BF16), jnp.where(low_lanes, 0.0, v).astype(BF16)


def _sweep_done(c0, c1):
    return (jnp.maximum(jnp.max(c0), jnp.max(c1)) < EXP_UNDERFLOW).astype(jnp.int32)


def _first_window(i):
    first_blk = jnp.maximum(i - 1, 0)
    return first_blk, pl.multiple_of(first_blk * QB, QB), (i - first_blk) * QB


def _fwd_attn(qkv, n_pairs, ex):
    S = qkv.shape[0]
    n_steps = S // (ATTN_SUBS * QB)

    def body(q_ref, k_ref, v_ref, *rest):
        o_ref = rest[ex.n]
        ex_refs = ex.split(rest[:ex.n] + rest[ex.n + 1:])
        first_step, last_step = _grid_ends((n_pairs, n_steps))

        @pl.when(first_step)
        def _():
            ex.start(*ex_refs)

        low_lanes = _low_lanes()
        after_s = _triangle(False)
        zero = jnp.zeros((QB, 1), F32)

        def cond(c):
            return jnp.logical_and(c[0] >= 0, c[1] == 0)

        qhs, kws, vws, masks, first_blks = [], [], [], [], []
        for sub in range(ATTN_SUBS):
            i = pl.program_id(1) * ATTN_SUBS + sub
            first_blk, start, offset = _first_window(i)
            first_blks.append(first_blk)
            qhs += _split_heads(q_ref[sub * QB:(sub + 1) * QB, :].astype(F32), low_lanes)
            kws += [k_ref[pl.ds(start, 2 * QB), :]] * 2
            vws += [v_ref[pl.ds(start, 2 * QB), :]] * 2
            masks += [_causal_mask(2 * QB, offset)] * 2
        tiles = _attn_tiles(qhs, kws, masks, [zero] * len(qhs), after_s)
        outs = [_dot(t[2].astype(BF16), vw) for t, vw in zip(tiles, vws)]

        for sub in range(ATTN_SUBS):
            def step(c, qh=qhs[2 * sub:2 * sub + 2]):
                j, _, acc, c0, c1 = c
                at = pl.multiple_of(j * QB, QB)
                kb = k_ref[pl.ds(at, QB), :]
                vb = v_ref[pl.ds(at, QB), :]
                far = _attn_tiles(qh, [kb, kb], [None, None], [c0, c1], after_s)
                acc = acc + jnp.where(low_lanes, _dot(far[0][2].astype(BF16), vb), _dot(far[1][2].astype(BF16), vb))
                return j - 1, _sweep_done(far[0][3], far[1][3]), acc, far[0][3], far[1][3]

            c0, c1 = tiles[2 * sub][3], tiles[2 * sub + 1][3]
            init = (first_blks[sub] - 1, _sweep_done(c0, c1), jnp.where(low_lanes, outs[2 * sub], outs[2 * sub + 1]), c0, c1)
            o_ref[sub * QB:(sub + 1) * QB, :] = lax.while_loop(cond, step, init)[2]

        @pl.when(last_step)
        def _():
            ex.wait(*ex_refs)

    outs = pl.pallas_call(
        body, name="fwd_attn", grid=(n_pairs, n_steps),
        in_specs=[pl.BlockSpec((ATTN_SUBS * QB, QB), lambda p, i: (i, p)),
                  pl.BlockSpec((S, QB), lambda p, i: (0, n_pairs + p), pipeline_mode=pl.Buffered(1)),
                  pl.BlockSpec((S, QB), lambda p, i: (0, 2 * n_pairs + p), pipeline_mode=pl.Buffered(1))] + ex.specs,
        out_specs=[pl.BlockSpec((ATTN_SUBS * QB, QB), lambda p, i: (i, p))] + ex.specs,
        out_shape=[jax.ShapeDtypeStruct((S, n_pairs * QB), F32)] + ex.out_shape,
        scratch_shapes=ex.scratch,
        compiler_params=_params("arbitrary", "arbitrary"),
    )(qkv, qkv, qkv, *ex.arrays)
    return outs[0], outs[1:]


def _normalized_heads(pool_out, attn_out):
    rp, ra = _rms(pool_out), _rms(attn_out)
    return pool_out * rp, rp, attn_out * ra, ra


def _fwd_outproj(pool_out, attn_out, pool_scale, attn_scale, w_out, x, g2, g3, tile):
    S, D = x.shape
    C = pool_out.shape[1]

    def body(p_ref, a_ref, ps_ref, as_ref, w_ref, x_ref, g2_ref, g3_ref, mix_ref, x2_ref, h2_ref, h2t_ref):
        n_p, _, n_a, _ = _normalized_heads(p_ref[...], a_ref[...])
        mix = _dot((n_p * ps_ref[...]).astype(BF16), w_ref[:C, :]) + _dot((n_a * as_ref[...]).astype(BF16), w_ref[C:, :])
        mix_ref[...] = mix
        x2 = x_ref[...] + mix * _rms(mix) * g2_ref[...]
        x2_ref[...] = x2
        h2 = (x2 * _rms(x2) * g3_ref[...]).astype(BF16)
        h2_ref[...] = h2
        h2t_ref[...] = h2.T

    row = lambda w: pl.BlockSpec((tile, w), lambda i: (i, 0))
    return pl.pallas_call(
        body, name="fwd_outproj", grid=(S // tile,),
        in_specs=[row(C), row(C), _const((1, C)), _const((1, C)), _const(w_out.shape), row(D), _const((1, D)), _const((1, D))],
        out_specs=[row(D), row(D), row(D), pl.BlockSpec((D, tile), lambda i: (0, i))],
        out_shape=[jax.ShapeDtypeStruct((S, D), F32), jax.ShapeDtypeStruct((S, D), F32), jax.ShapeDtypeStruct((S, D), BF16),
                   jax.ShapeDtypeStruct((D, S), BF16)],
        compiler_params=_params("parallel"),
    )(pool_out, attn_out, pool_scale, attn_scale, w_out, x, g2, g3)


def _fwd_up(h2, w_up_g, tile):
    S, D = h2.shape
    nb, _, cs = w_up_g.shape

    def body(h_ref, w_ref, o_ref):
        h = h_ref[...]
        for d in range(nb):
            o_ref[d] = _dot(h, w_ref[d]).astype(BF16)

    return pl.pallas_call(
        body, name="fwd_up", grid=(S // tile,),
        in_specs=[pl.BlockSpec((tile, D), lambda i: (i, 0)), _const(w_up_g.shape)],
        out_specs=pl.BlockSpec((nb, tile, cs), lambda i: (0, i, 0)),
        out_shape=jax.ShapeDtypeStruct((nb, S, cs), BF16),
        compiler_params=_params("parallel"),
    )(h2, w_up_g)


def _conv_taps(tile_rows, halo_rows):
    T = tile_rows.shape[0]
    ext = jnp.concatenate([halo_rows.astype(F32), tile_rows.astype(F32)], axis=0)
    return pltpu.roll(ext, 2, axis=0)[HALO:], pltpu.roll(ext, 1, axis=0)[HALO:], ext[HALO:]


def _tap_rows(cw_ref, d):
    return [cw_ref[d, k:k + 1, :] for k in range(3)]


def _gated_unit(taps_gate, taps_val, cw_gate, cw_val, cb_gate, cb_val):
    gate = cw_gate[0] * taps_gate[0] + cw_gate[1] * taps_gate[1] + cw_gate[2] * taps_gate[2] + cb_gate
    val = cw_val[0] * taps_val[0] + cw_val[1] * taps_val[1] + cw_val[2] * taps_val[2] + cb_val
    sig = 1.0 / (1.0 + jnp.exp(-gate))
    return gate, val, sig


def _fwd_ffn_loss(upre, conv_w_g, conv_b_g, w_down4, x2, target, g4, tile):
    nb, S, cs = upre.shape
    D = x2.shape[1]

    def body(u_ref, halo_ref, cw_ref, cb_ref, wd_ref, x2_ref, t_ref, g4_ref, dy_ref, df_ref, loss_ref, dg4_ref):
        i = pl.program_id(0)
        first = i == 0
        _zero_when(first, loss_ref, dg4_ref)
        f = jnp.zeros((tile, D), F32)
        for s in range(D_FF_SHARDS):
            halo_g = jnp.where(first, jnp.zeros_like(halo_ref[s]), halo_ref[s])
            halo_v = jnp.where(first, jnp.zeros_like(halo_ref[s]), halo_ref[s + D_FF_SHARDS])
            gate, val, sig = _gated_unit(_conv_taps(u_ref[s], halo_g), _conv_taps(u_ref[s + D_FF_SHARDS], halo_v),
                                         _tap_rows(cw_ref, s), _tap_rows(cw_ref, s + D_FF_SHARDS), cb_ref[s], cb_ref[s + D_FF_SHARDS])
            f = f + _dot((gate * sig * val).astype(BF16), wd_ref[s])
        r4 = _rms(f)
        n4 = f * r4
        err = x2_ref[...] + n4 * g4_ref[...] - t_ref[...]
        dy = err * (1.0 / D)
        dy_ref[...] = dy
        df_ref[...] = _norm_bwd(dy * g4_ref[...], n4, r4).astype(BF16)
        loss_ref[...] += _colsum(err * err)
        dg4_ref[...] += _colsum(dy * n4)

    row = lambda w: pl.BlockSpec((tile, w), lambda i: (i, 0))
    return pl.pallas_call(
        body, name="fwd_ffn_loss", grid=(S // tile,),
        in_specs=[pl.BlockSpec((nb, tile, cs), lambda i: (0, i, 0)),
                  pl.BlockSpec((nb, HALO, cs), lambda i: (0, jnp.maximum(i * (tile // HALO) - 1, 0), 0)),
                  _const(conv_w_g.shape), _const(conv_b_g.shape), _const(w_down4.shape), row(D), row(D), _const((1, D))],
        out_specs=[row(D), row(D), pl.BlockSpec((1, D), lambda i: (0, 0)), pl.BlockSpec((1, D), lambda i: (0, 0))],
        out_shape=[jax.ShapeDtypeStruct((S, D), F32), jax.ShapeDtypeStruct((S, D), BF16),
                   jax.ShapeDtypeStruct((1, D), F32), jax.ShapeDtypeStruct((1, D), F32)],
        compiler_params=_params("arbitrary"),
    )(upre, upre, conv_w_g, conv_b_g, w_down4, x2, target, g4)


def _bwd_down(upre, conv_w_g, conv_b_g, w_down4, df, tile):
    nb, S, cs = upre.shape
    D = df.shape[1]
    n_tiles = S // tile

    def body(ug_ref, uv_ref, hg_ref, hv_ref, cwg_ref, cwv_ref, cbg_ref, cbv_ref, wd_ref, df_ref,
             dg_ref, dv_ref, dwd_ref, dbg_ref, dbv_ref, dcwg_ref, dcwv_ref):
        i = pl.program_id(1)
        first = i == 0
        _zero_when(first, dwd_ref, dbg_ref, dbv_ref, dcwg_ref, dcwv_ref)
        halo_g = jnp.where(first, jnp.zeros_like(hg_ref[0]), hg_ref[0])
        halo_v = jnp.where(first, jnp.zeros_like(hv_ref[0]), hv_ref[0])
        taps_g, taps_v = _conv_taps(ug_ref[0], halo_g), _conv_taps(uv_ref[0], halo_v)
        gate, val, sig = _gated_unit(taps_g, taps_v, _tap_rows(cwg_ref, 0), _tap_rows(cwv_ref, 0), cbg_ref[0], cbv_ref[0])
        silu = gate * sig
        dfb = df_ref[...]
        dact = _dot_nt(dfb, wd_ref[0])
        dwd_ref[0] += _dot_tn((silu * val).astype(BF16), dfb)
        dgate = dact * val * (sig * (1.0 + gate * (1.0 - sig)))
        dval = dact * silu
        dg_ref[0] = dgate.astype(BF16)
        dv_ref[0] = dval.astype(BF16)
        dbg_ref[0] += _colsum(dgate)
        dbv_ref[0] += _colsum(dval)
        for k in range(3):
            dcwg_ref[0, k:k + 1, :] += _colsum(dgate * taps_g[k])
            dcwv_ref[0, k:k + 1, :] += _colsum(dval * taps_v[k])

    half = D_FF_SHARDS
    blk = lambda off: pl.BlockSpec((1, tile, cs), lambda s, i: (s + off, i, 0))
    halo = lambda off: pl.BlockSpec((1, HALO, cs), lambda s, i: (s + off, jnp.maximum(i * (tile // HALO) - 1, 0), 0))
    par = lambda off, r: pl.BlockSpec((1, r, cs), lambda s, i: (s + off, 0, 0))
    outs = pl.pallas_call(
        body, name="bwd_down", grid=(half, n_tiles),
        in_specs=[blk(0), blk(half), halo(0), halo(half), par(0, 3), par(half, 3), par(0, 1), par(half, 1),
                  pl.BlockSpec((1, cs, D), lambda s, i: (s, 0, 0)), pl.BlockSpec((tile, D), lambda s, i: (i, 0))],
        out_specs=[blk(0), blk(0), pl.BlockSpec((1, cs, D), lambda s, i: (s, 0, 0)),
                   par(0, 1), par(0, 1), par(0, 3), par(0, 3)],
        out_shape=[jax.ShapeDtypeStruct((half, S, cs), BF16), jax.ShapeDtypeStruct((half, S, cs), BF16),
                   jax.ShapeDtypeStruct((half, cs, D), F32),
                   jax.ShapeDtypeStruct((half, 1, cs), F32), jax.ShapeDtypeStruct((half, 1, cs), F32),
                   jax.ShapeDtypeStruct((half, 3, cs), F32), jax.ShapeDtypeStruct((half, 3, cs), F32)],
        compiler_params=_params("parallel", "arbitrary"),
    )(upre, upre, upre, upre, conv_w_g, conv_w_g, conv_b_g, conv_b_g, w_down4, df)
    dgate, dval, d_wd, dbg, dbv, dcwg, dcwv = outs
    return dgate, dval, d_wd, jnp.concatenate([dbg, dbv], axis=0), jnp.concatenate([dcwg, dcwv], axis=0)


def _bwd_up_x(dgate, dval, conv_w_g, w_up_g, x2, dy, mix, g2, g3, tile):
    half, S, cs = dgate.shape
    nb = 2 * half
    D = x2.shape[1]
    n_tiles = S // tile

    def body(dg_ref, dv_ref, hg_ref, hv_ref, cw_ref, w_ref, x2_ref, dy_ref, mix_ref, g2_ref, g3_ref,
             dupre_ref, dx2_ref, dmix_ref, dg3_ref, dg2_ref):
        i = pl.program_id(0)
        last = i == n_tiles - 1
        _zero_when(i == 0, dg3_ref, dg2_ref)
        dh2 = jnp.zeros((tile, D), F32)
        for d in range(nb):
            src, halo = (dg_ref, hg_ref) if d < half else (dv_ref, hv_ref)
            nxt = jnp.where(last, jnp.zeros_like(halo[d % half]), halo[d % half])
            ext = jnp.concatenate([src[d % half].astype(F32), nxt.astype(F32)], axis=0)
            n = ext.shape[0]
            cw = _tap_rows(cw_ref, d)
            dupre = (cw[2] * ext + cw[1] * pltpu.roll(ext, n - 1, axis=0) + cw[0] * pltpu.roll(ext, n - 2, axis=0))[:tile]
            dupre = dupre.astype(BF16)
            dupre_ref[d] = dupre
            dh2 = dh2 + _dot_nt(dupre, w_ref[d])
        x2 = x2_ref[...]
        r3 = _rms(x2)
        n3 = x2 * r3
        dg3_ref[...] += _colsum(dh2 * n3)
        dx2 = dy_ref[...] + _norm_bwd(dh2 * g3_ref[...], n3, r3)
        dx2_ref[...] = dx2
        mix = mix_ref[...]
        r2 = _rms(mix)
        n2 = mix * r2
        dg2_ref[...] += _colsum(dx2 * n2)
        dmix_ref[...] = _norm_bwd(dx2 * g2_ref[...], n2, r2).astype(BF16)

    row = lambda w: pl.BlockSpec((tile, w), lambda i: (i, 0))
    blk = pl.BlockSpec((half, tile, cs), lambda i: (0, i, 0))
    last_halo = n_tiles * (tile // HALO) - 1
    halo = pl.BlockSpec((half, HALO, cs), lambda i: (0, jnp.minimum((i + 1) * (tile // HALO), last_halo), 0))
    acc = pl.BlockSpec((1, D), lambda i: (0, 0))
    return pl.pallas_call(
        body, name="bwd_up_x", grid=(n_tiles,),
        in_specs=[blk, blk, halo, halo, _const(conv_w_g.shape), _const(w_up_g.shape), row(D), row(D), row(D),
                  _const((1, D)), _const((1, D))],
        out_specs=[pl.BlockSpec((nb, tile, cs), lambda i: (0, i, 0)), row(D), row(D), acc, acc],
        out_shape=[jax.ShapeDtypeStruct((nb, S, cs), BF16), jax.ShapeDtypeStruct((S, D), F32),
                   jax.ShapeDtypeStruct((S, D), BF16), jax.ShapeDtypeStruct((1, D), F32), jax.ShapeDtypeStruct((1, D), F32)],
        compiler_params=_params("arbitrary"),
    )(dgate, dval, dgate, dval, conv_w_g, w_up_g, x2, dy, mix, g2, g3)


def _bwd_weight(act_t, dout, tile):
    D, S = act_t.shape
    nb, _, cs = dout.shape

    def body(a_ref, d_ref, o_ref):
        _zero_when(pl.program_id(1) == 0, o_ref)
        o_ref[0] += _dot(a_ref[...], d_ref[0])

    return pl.pallas_call(
        body, name="bwd_w_up", grid=(nb, S // tile),
        in_specs=[pl.BlockSpec((D, tile), lambda d, i: (0, i)), pl.BlockSpec((1, tile, cs), lambda d, i: (d, i, 0))],
        out_specs=pl.BlockSpec((1, D, cs), lambda d, i: (d, 0, 0)),
        out_shape=jax.ShapeDtypeStruct((nb, D, cs), F32),
        compiler_params=_params("parallel", "arbitrary"),
    )(act_t, dout)


def _bwd_outproj(dmix, w_out, pool_out, attn_out, pool_scale, attn_scale, tile):
    S, D = dmix.shape
    C = pool_out.shape[1]

    def body(dm_ref, w_ref, p_ref, a_ref, ps_ref, as_ref, dp_ref, da_ref, dw_ref, dps_ref, das_ref):
        _zero_when(pl.program_id(0) == 0, dw_ref, dps_ref, das_ref)
        dmx = dm_ref[...]
        dmerged = _dot_nt(dmx, w_ref[...])
        n_p, r_p, n_a, r_a = _normalized_heads(p_ref[...], a_ref[...])
        merged = jnp.concatenate([(n_p * ps_ref[...]).astype(BF16), (n_a * as_ref[...]).astype(BF16)], axis=1)
        dw_ref[...] += _dot_tn(merged, dmx)
        dm_p, dm_a = dmerged[:, :C], dmerged[:, C:]
        dps_ref[...] += _colsum(dm_p * n_p)
        das_ref[...] += _colsum(dm_a * n_a)
        dp_ref[...] = _norm_bwd(dm_p * ps_ref[...], n_p, r_p)
        da_ref[...] = _norm_bwd(dm_a * as_ref[...], n_a, r_a)

    row = lambda w: pl.BlockSpec((tile, w), lambda i: (i, 0))
    return pl.pallas_call(
        body, name="bwd_outproj", grid=(S // tile,),
        in_specs=[row(D), _const(w_out.shape), row(C), row(C), _const((1, C)), _const((1, C))],
        out_specs=[row(C), row(C), pl.BlockSpec(w_out.shape, lambda i: (0, 0)),
                   pl.BlockSpec((1, C), lambda i: (0, 0)), pl.BlockSpec((1, C), lambda i: (0, 0))],
        out_shape=[jax.ShapeDtypeStruct((S, C), F32), jax.ShapeDtypeStruct((S, C), F32),
                   jax.ShapeDtypeStruct(w_out.shape, F32), jax.ShapeDtypeStruct((1, C), F32), jax.ShapeDtypeStruct((1, C), F32)],
        compiler_params=_params("arbitrary"),
    )(dmix, w_out, pool_out, attn_out, pool_scale, attn_scale)


def _bwd_attn(qkv, d_attn, n_pairs, ex):
    S = qkv.shape[0]
    n_steps = S // (ATTN_SUBS * QB)

    def body(q_ref, k_ref, v_ref, do_ref, *rest):
        dq_ref, dk_ref, dv_ref = rest[ex.n:ex.n + 3]
        ex_refs = ex.split(rest[:ex.n] + rest[ex.n + 3:])
        first_step, last_step = _grid_ends((n_pairs, n_steps))

        @pl.when(first_step)
        def _():
            ex.start(*ex_refs)

        @pl.when(pl.program_id(1) == 0)
        def _():
            dk_ref[...] = jnp.zeros_like(dk_ref)
            dv_ref[...] = jnp.zeros_like(dv_ref)

        low_lanes = _low_lanes()
        after_s, from_s = _triangle(False), _triangle(True)
        zero = jnp.zeros((QB, 1), F32)

        def tiles(qhs, dohs, totals, kws, vws, masks, cs, gs):
            fw = _attn_tiles(qhs, kws, masks, cs, after_s)
            gvals = [t[2] * _dot_nt(doh, vw) for t, doh, vw in zip(fw, dohs, vws)]
            sums = [_suffix_sums(g, from_s, g0) for g, g0 in zip(gvals, gs)]
            totals = [tot if m is None else tot + sm[1] for tot, m, sm in zip(totals, masks, sums)]
            dzs = []
            for (z, e, _, _), g, (nearer, _), tot, m in zip(fw, gvals, sums, totals, masks):
                inv = 1.0 / (1.0 + e)
                sig_abs, sig_neg = inv, e * inv
                pos = z >= 0.0
                dz = g * jnp.where(pos, sig_neg, sig_abs) - jnp.where(pos, sig_abs, sig_neg) * (tot - nearer)
                if m is not None:
                    dz = jnp.where(m, dz, 0.0)
                dzs.append((dz * ATTN_SCALE).astype(BF16))
            dqs = [_dot(dz, kw) for dz, kw in zip(dzs, kws)]
            dks = [_dot_tn(dz, qh) for dz, qh in zip(dzs, qhs)]
            dvs = [_dot_tn(t[2].astype(BF16), doh) for t, doh in zip(fw, dohs)]
            return [(dq, dk, dv, t[3], sm[1], tot) for dq, dk, dv, t, sm, tot in zip(dqs, dks, dvs, fw, sums, totals)]

        def cond(c):
            return jnp.logical_and(c[0] >= 0, c[1] == 0)

        qhs, dohs, kws, vws, masks, first_blks, starts = [], [], [], [], [], [], []
        for sub in range(ATTN_SUBS):
            i = pl.program_id(1) * ATTN_SUBS + sub
            rows = slice(sub * QB, (sub + 1) * QB)
            first_blk, start, offset = _first_window(i)
            first_blks.append(first_blk)
            starts.append(start)
            qhs += _split_heads(q_ref[rows, :].astype(F32), low_lanes)
            dohs += _split_heads(do_ref[rows, :], low_lanes)
            kws += [k_ref[pl.ds(start, 2 * QB), :]] * 2
            vws += [v_ref[pl.ds(start, 2 * QB), :]] * 2
            masks += [_causal_mask(2 * QB, offset)] * 2
        zeros = [zero] * len(qhs)

        c_first = [t[3] for t in _attn_tiles(qhs, kws, masks, zeros, after_s)]
        beyond_first = []
        for sub in range(ATTN_SUBS):
            pair = slice(2 * sub, 2 * sub + 2)

            def far_sums(c, qh=qhs[pair], doh=dohs[pair]):
                j, _, c0, c1, r0, r1 = c
                at = pl.multiple_of(j * QB, QB)
                kb = k_ref[pl.ds(at, QB), :]
                vb = v_ref[pl.ds(at, QB), :]
                far = _attn_tiles(qh, [kb, kb], [None, None], [c0, c1], after_s)
                r0 = r0 + jnp.sum(far[0][2] * _dot_nt(doh[0], vb), axis=1, keepdims=True)
                r1 = r1 + jnp.sum(far[1][2] * _dot_nt(doh[1], vb), axis=1, keepdims=True)
                return j - 1, _sweep_done(far[0][3], far[1][3]), far[0][3], far[1][3], r0, r1

            c0, c1 = c_first[pair]
            far = lax.while_loop(cond, far_sums, (first_blks[sub] - 1, _sweep_done(c0, c1), c0, c1, zero, zero))
            beyond_first += [far[4], far[5]]

        done = tiles(qhs, dohs, beyond_first, kws, vws, masks, zeros, zeros)
        for sub in range(ATTN_SUBS):
            dk_ref[pl.ds(starts[sub], 2 * QB), :] += done[2 * sub][1] + done[2 * sub + 1][1]
            dv_ref[pl.ds(starts[sub], 2 * QB), :] += done[2 * sub][2] + done[2 * sub + 1][2]

        for sub in range(ATTN_SUBS):
            pair = slice(2 * sub, 2 * sub + 2)
            t0, t1 = done[pair]

            def step(c, qh=qhs[pair], doh=dohs[pair], total=[t0[5], t1[5]]):
                j, _, dq, c0, c1, s0, s1 = c
                at = pl.multiple_of(j * QB, QB)
                kb = k_ref[pl.ds(at, QB), :]
                vb = v_ref[pl.ds(at, QB), :]
                f0, f1 = tiles(qh, doh, total, [kb, kb], [vb, vb], [None, None], [c0, c1], [s0, s1])
                dk_ref[pl.ds(at, QB), :] += f0[1] + f1[1]
                dv_ref[pl.ds(at, QB), :] += f0[2] + f1[2]
                return j - 1, _sweep_done(f0[3], f1[3]), dq + jnp.where(low_lanes, f0[0], f1[0]), f0[3], f1[3], f0[4], f1[4]

            init = (first_blks[sub] - 1, _sweep_done(t0[3], t1[3]), jnp.where(low_lanes, t0[0], t1[0]), t0[3], t1[3], t0[4], t1[4])
            dq_ref[sub * QB:(sub + 1) * QB, :] = lax.while_loop(cond, step, init)[2]

        @pl.when(last_step)
        def _():
            ex.wait(*ex_refs)

    blk = pl.BlockSpec((ATTN_SUBS * QB, QB), lambda p, i: (i, p))
    full = lambda off: pl.BlockSpec((S, QB), lambda p, i: (0, off + p), pipeline_mode=pl.Buffered(1))
    outs = pl.pallas_call(
        body, name="bwd_attn", grid=(n_pairs, n_steps),
        in_specs=[blk, full(n_pairs), full(2 * n_pairs), blk] + ex.specs,
        out_specs=[blk, pl.BlockSpec((S, QB), lambda p, i: (0, p)), pl.BlockSpec((S, QB), lambda p, i: (0, p))] + ex.specs,
        out_shape=[jax.ShapeDtypeStruct((S, n_pairs * QB), F32)] * 3 + ex.out_shape,
        scratch_shapes=ex.scratch,
        compiler_params=_params("arbitrary", "arbitrary"),
    )(qkv, qkv, qkv, d_attn, *ex.arrays)
    return outs[0], outs[1], outs[2], outs[3:]


def _bwd_pool(u, d_pool, w_pool, tile):
    S, C = u.shape
    n_tiles = S // tile
    ng = len(POOL_WINDOWS)

    def body(u_ref, uh_ref, d_ref, dh_ref, wp_ref, du_ref, dwp_ref):
        i = pl.program_id(0)
        first = i == 0
        _zero_when(first, dwp_ref)
        halo = jnp.where(first, 0.0, uh_ref[...])
        parts = _pool_deviation(u_ref[...], halo, i * tile)
        dout = d_ref[...]
        nxt = jnp.where(i == n_tiles - 1, 0.0, dh_ref[...])
        dext = jnp.concatenate([dout, nxt], axis=0).astype(BF16)
        counts = _pool_counts(i * tile, tile + HALO)
        dps, scaled = [], []
        for g in range(ng):
            lanes = slice(g * POOL_GROUP, (g + 1) * POOL_GROUP)
            dp = _dot_nt(dext[:, lanes], wp_ref[g].astype(BF16))
            dps.append(dp[:tile])
            scaled.append(dp / counts[g])
        sums = _window_sums(jnp.concatenate(scaled, axis=1), forward=True)
        for g, w in enumerate(POOL_WINDOWS):
            lanes = slice(g * POOL_GROUP, (g + 1) * POOL_GROUP)
            du_ref[:, lanes] = sums[w][:tile, lanes] - dps[g]
            dwp_ref[g] += _dot_tn(parts[g].astype(BF16), dext[:tile, lanes])

    row = pl.BlockSpec((tile, C), lambda i: (i, 0))
    return pl.pallas_call(
        body, name="bwd_pool", grid=(n_tiles,),
        in_specs=[row, _prev_halo_spec(tile, C), row, _next_halo_spec(tile, C, n_tiles), _const(w_pool.shape)],
        out_specs=[row, pl.BlockSpec(w_pool.shape, lambda i: (0, 0, 0))],
        out_shape=[jax.ShapeDtypeStruct((S, C), F32), jax.ShapeDtypeStruct(w_pool.shape, F32)],
        compiler_params=_params("arbitrary"),
    )(u, u, d_pool, d_pool, w_pool)


def _bwd_w_in(du, dq, dk, dv, h1_t, n_blocks, tile):
    D, S = h1_t.shape
    C = du.shape[1]
    cs = 4 * C // n_blocks
    per = C // cs

    def body(du_ref, dq_ref, dk_ref, dv_ref, ht_ref, dproj_ref, dw_ref):
        _zero_when(pl.program_id(0) == 0, dw_ref)
        ht = ht_ref[...]
        for d in range(n_blocks):
            src = (du_ref, dq_ref, dk_ref, dv_ref)[d // per]
            dproj = src[:, (d % per) * cs:(d % per + 1) * cs].astype(BF16)
            dproj_ref[:, d * cs:(d + 1) * cs] = dproj
            dw_ref[d] += _dot(ht, dproj)

    row = lambda w: pl.BlockSpec((tile, w), lambda i: (i, 0))
    return pl.pallas_call(
        body, name="bwd_w_in", grid=(S // tile,),
        in_specs=[row(C), row(C), row(C), row(C), pl.BlockSpec((D, tile), lambda i: (0, i))],
        out_specs=[row(4 * C), pl.BlockSpec((n_blocks, D, cs), lambda i: (0, 0, 0))],
        out_shape=[jax.ShapeDtypeStruct((S, 4 * C), BF16), jax.ShapeDtypeStruct((n_blocks, D, cs), F32)],
        compiler_params=_params("arbitrary"),
    )(du, dq, dk, dv, h1_t)


def _bwd_x(dproj, w_in_g, x, dx2, g1, tile, ex):
    S, D = x.shape
    nb, _, cs = w_in_g.shape
    n_tiles = S // tile

    def body(dp_ref, w_ref, x_ref, dx2_ref, g_ref, *rest):
        dx_ref, dg_ref = rest[ex.n:ex.n + 2]
        ex_refs = ex.split(rest[:ex.n] + rest[ex.n + 2:])
        first, last = _grid_ends((n_tiles,))

        @pl.when(first)
        def _():
            ex.start(*ex_refs)
            dg_ref[...] = jnp.zeros_like(dg_ref)

        dh = jnp.zeros((tile, D), F32)
        for d in range(nb):
            dh = dh + _dot_nt(dp_ref[:, d * cs:(d + 1) * cs], w_ref[d])
        xf = x_ref[...]
        r1 = _rms(xf)
        n1 = xf * r1
        dg_ref[...] += _colsum(dh * n1)
        dx_ref[...] = dx2_ref[...] + _norm_bwd(dh * g_ref[...], n1, r1)

        @pl.when(last)
        def _():
            ex.wait(*ex_refs)

    row = lambda w: pl.BlockSpec((tile, w), lambda i: (i, 0))
    outs = pl.pallas_call(
        body, name="bwd_x", grid=(n_tiles,),
        in_specs=[row(nb * cs), _const(w_in_g.shape), row(D), row(D), _const((1, D))] + ex.specs,
        out_specs=[row(D), pl.BlockSpec((1, D), lambda i: (0, 0))] + ex.specs,
        out_shape=[jax.ShapeDtypeStruct((S, D), F32), jax.ShapeDtypeStruct((1, D), F32)] + ex.out_shape,
        scratch_shapes=ex.scratch,
        compiler_params=_params("arbitrary"),
    )(dproj, w_in_g, x, dx2, g1, *ex.arrays)
    return outs[0], outs[1], outs[2:]


def _mesh_position():
    x, y, c = lax.axis_index("x"), lax.axis_index("y"), lax.axis_index("c")
    return x, y, c, 4 * x + 2 * y + c


def _peer(x, y, c, k):
    px = 1 - x if k & 4 else x
    py = 1 - y if k & 2 else y
    pc = 1 - c if k & 1 else c
    return (px, py, pc), 4 * px + 2 * py + pc


class _Exchange:
    def __init__(self, arrays, gather):
        self.arrays, self.gather, self.n = list(arrays), gather, len(arrays)
        self.out_shape = [jax.ShapeDtypeStruct(((N_DEV,) + a.shape) if gather else a.shape, a.dtype) for a in arrays]
        self.specs = [pl.BlockSpec(memory_space=pl.ANY)] * self.n
        copies = self.n * (N_DEV - 1)
        self.scratch = [pltpu.SemaphoreType.DMA((copies,)), pltpu.SemaphoreType.DMA((copies,)),
                        pltpu.SemaphoreType.DMA((self.n,))]

    def _copies(self, ins, outs, sems):
        send_sems, recv_sems, local_sems = sems
        x, y, c, me = _mesh_position()
        local, remote = [], []
        for a in range(self.n):
            mine = ins[a] if self.gather else ins[a].at[me]
            local.append(pltpu.make_async_copy(mine, outs[a].at[me], local_sems.at[a]))
            for k in range(1, N_DEV):
                peer, peer_idx = _peer(x, y, c, k)
                src = ins[a] if self.gather else ins[a].at[peer_idx]
                sem = a * (N_DEV - 1) + k - 1
                remote.append(pltpu.make_async_remote_copy(
                    src_ref=src, dst_ref=outs[a].at[me], send_sem=send_sems.at[sem], recv_sem=recv_sems.at[sem],
                    device_id=peer, device_id_type=MESH))
        return local, remote

    def start(self, ins, outs, sems):
        local, remote = self._copies(ins, outs, sems)
        for cp in local + remote:
            cp.start()

    def wait(self, ins, outs, sems):
        local, remote = self._copies(ins, outs, sems)
        for cp in remote:
            cp.wait_send()
        for cp in remote:
            cp.wait_recv()
        for cp in local:
            cp.wait()

    def split(self, refs):
        return refs[:self.n], refs[self.n:2 * self.n], refs[2 * self.n:]


def _all_to_all(arrays, gather, name):
    ex = _Exchange(arrays, gather)

    def body(*refs):
        ins, outs, sems = ex.split(refs)
        ex.start(ins, outs, sems)
        ex.wait(ins, outs, sems)

    return pl.pallas_call(body, name=name, in_specs=ex.specs, out_specs=ex.specs, out_shape=ex.out_shape,
                          scratch_shapes=ex.scratch)(*ex.arrays)


def _reduce_adamw(parts, w, m, v, rows):
    R, C = w.shape

    def body(p_ref, w_ref, m_ref, v_ref, g_ref, d_ref, nm_ref, nv_ref):
        g = p_ref[0].astype(F32)
        for s in range(1, N_DEV):
            g = g + p_ref[s].astype(F32)
        g_ref[...] = g
        m_new = ADAM_B1 * m_ref[...] + (1.0 - ADAM_B1) * g
        v_new = ADAM_B2 * v_ref[...] + (1.0 - ADAM_B2) * (g * g)
        m_hat = m_new / (1.0 - ADAM_B1 ** ADAM_STEP)
        v_hat = v_new / (1.0 - ADAM_B2 ** ADAM_STEP)
        d_ref[...] = -ADAM_LR * (m_hat / (jnp.sqrt(v_hat) + ADAM_EPS) + ADAM_WD * w_ref[...])
        nm_ref[...] = m_new
        nv_ref[...] = v_new

    row = pl.BlockSpec((rows, C), lambda i: (i, 0))
    return pl.pallas_call(
        body, name="reduce_adamw", grid=(R // rows,),
        in_specs=[pl.BlockSpec((N_DEV, rows, C), lambda i: (0, i, 0)), row, row, row],
        out_specs=[row] * 4, out_shape=[jax.ShapeDtypeStruct((R, C), F32)] * 4,
        compiler_params=_params("parallel"),
    )(parts, w, m, v)


def _row_tile(rows, cols):
    fits = [t for t in range(8, rows + 1, 8) if rows % t == 0 and N_DEV * t * cols * 4 <= 4 * 1024 * 1024]
    return max(fits) if fits else rows


SMALL_COLS = 1024


def _pack_small(vals):
    rows = []
    for a in vals:
        flat = a.reshape(-1)
        pad = (-flat.shape[0]) % SMALL_COLS
        rows.append(jnp.pad(flat, (0, pad)).reshape(-1, SMALL_COLS))
    packed = jnp.concatenate(rows, axis=0)
    return jnp.pad(packed, ((0, (-packed.shape[0]) % 8), (0, 0)))


def _unpack_small(packed, like):
    out, r = [], 0
    for a in like:
        n = a.size
        nr = -(-n // SMALL_COLS)
        out.append(packed[r:r + nr].reshape(-1)[:n].reshape(a.shape))
        r += nr
    return out


def kernel(x, norm_mix_pre, w_in, w_pool, pool_scale, attn_scale, w_out, norm_mix_post, norm_ffn_pre, w_up, conv_w, conv_b, w_down, norm_ffn_post, loss_target, m_norm_mix_pre, m_w_in, m_w_pool, m_pool_scale, m_attn_scale, m_w_out, m_norm_mix_post, m_norm_ffn_pre, m_w_up, m_conv_w, m_conv_b, m_w_down, m_norm_ffn_post, v_norm_mix_pre, v_w_in, v_w_pool, v_pool_scale, v_attn_scale, v_w_out, v_norm_mix_post, v_norm_ffn_pre, v_w_up, v_conv_w, v_conv_b, v_w_down, v_norm_ffn_post):
    S, D = x.shape[1], x.shape[2]
    d_ff_block = w_up.shape[2]

    xs, target = x[0], loss_target[0]
    g1, g2, g3, g4 = norm_mix_pre, norm_mix_post, norm_ffn_pre, norm_ffn_post
    big = min(512, S)
    small = min(256, S)
    n_pairs = pool_scale.shape[1] // QB
    conv_b_g = conv_b.reshape(N_DEV, 1, d_ff_block)

    (w_in_g,) = _all_to_all([w_in[0].astype(BF16)], gather=True, name="gather_w_in")
    h1_t, u, qkv = _fwd_inproj(xs, g1, w_in_g, big)
    pool_out = _fwd_pool(u, w_pool[0], big)
    attn_out, (w_out_g, w_up_g, w_down_g, conv_w_g) = _fwd_attn(
        qkv, n_pairs, _Exchange([w_out[0].astype(BF16), w_up[0].astype(BF16), w_down[0].astype(BF16), conv_w[0]], gather=True))
    w_out_full = w_out_g.reshape(D, D)
    w_down4 = w_down_g.reshape(D_FF_SHARDS, d_ff_block, D)
    mix, x2, h2, h2_t = _fwd_outproj(pool_out, attn_out, pool_scale, attn_scale, w_out_full, xs, g2, g3, big)
    upre = _fwd_up(h2, w_up_g, big)
    dy, df, loss_cols, dg4 = _fwd_ffn_loss(upre, conv_w_g, conv_b_g, w_down4, x2, target, g4, small)
    loss = lax.psum(0.5 * jnp.sum(loss_cols) / D, ("x", "y", "c"))

    dgate, dval, d_wd4, d_cb, d_cw = _bwd_down(upre, conv_w_g, conv_b_g, w_down4, df, big)
    dupre, dx2, dmix, dg3, dg2 = _bwd_up_x(dgate, dval, conv_w_g, w_up_g, x2, dy, mix, g2, g3, small)
    d_wup = _bwd_weight(h2_t, dupre, min(1024, S))
    d_pool, d_attn, d_wout, d_ps, d_as = _bwd_outproj(dmix, w_out_full, pool_out, attn_out, pool_scale, attn_scale, big)
    d_wdown_g = d_wd4.reshape(N_DEV, w_down.shape[1], D)
    dq, dk, dv, ffn_parts = _bwd_attn(qkv, d_attn, n_pairs, _Exchange([d_wup, d_wdown_g, d_cw], gather=False))
    du, d_wp = _bwd_pool(u, d_pool, w_pool[0], big)
    dproj, d_win = _bwd_w_in(du, dq, dk, dv, h1_t, N_DEV, big)
    d_wout_g = d_wout.reshape(N_DEV, D // N_DEV, D)
    dx, dg1, mix_parts = _bwd_x(dproj, w_in_g, xs, dx2, g1, big, _Exchange([d_win, d_wout_g], gather=False))
    big_parts = [mix_parts[0], mix_parts[1], ffn_parts[0], ffn_parts[1], ffn_parts[2]]
    r = dict(dx=dx, g1=dg1, w_pool=d_wp, pool_scale=d_ps, attn_scale=d_as, g2=dg2, g3=dg3, conv_b=d_cb, g4=dg4)

    small_names = ["norm_mix_pre", "w_pool", "pool_scale", "attn_scale", "norm_mix_post", "norm_ffn_pre", "conv_b", "norm_ffn_post"]
    small_w = dict(norm_mix_pre=norm_mix_pre, w_pool=w_pool, pool_scale=pool_scale, attn_scale=attn_scale,
                   norm_mix_post=norm_mix_post, norm_ffn_pre=norm_ffn_pre, conv_b=conv_b, norm_ffn_post=norm_ffn_post)
    small_m = dict(norm_mix_pre=m_norm_mix_pre, w_pool=m_w_pool, pool_scale=m_pool_scale, attn_scale=m_attn_scale,
                   norm_mix_post=m_norm_mix_post, norm_ffn_pre=m_norm_ffn_pre, conv_b=m_conv_b, norm_ffn_post=m_norm_ffn_post)
    small_v = dict(norm_mix_pre=v_norm_mix_pre, w_pool=v_w_pool, pool_scale=v_pool_scale, attn_scale=v_attn_scale,
                   norm_mix_post=v_norm_mix_post, norm_ffn_pre=v_norm_ffn_pre, conv_b=v_conv_b, norm_ffn_post=v_norm_ffn_post)
    small_g = dict(norm_mix_pre=r["g1"], w_pool=r["w_pool"], pool_scale=r["pool_scale"], attn_scale=r["attn_scale"],
                   norm_mix_post=r["g2"], norm_ffn_pre=r["g3"], conv_b=r["conv_b"], norm_ffn_post=r["g4"])
    like = [small_w[n] for n in small_names]
    packed_g = _pack_small([small_g[n] for n in small_names])

    (small_parts,) = _all_to_all([packed_g], gather=True, name="gather_small_grads")

    def update(parts, w, m, v):
        R, C = w.shape
        return _reduce_adamw(parts, w, m, v, _row_tile(R, C))

    res = {}
    res["w_in"] = update(big_parts[0], w_in[0], m_w_in[0], v_w_in[0])
    res["w_out"] = update(big_parts[1], w_out[0], m_w_out[0], v_w_out[0])
    res["w_up"] = update(big_parts[2], w_up[0], m_w_up[0], v_w_up[0])
    res["w_down"] = update(big_parts[3], w_down[0], m_w_down[0], v_w_down[0])
    res["conv_w"] = update(big_parts[4], conv_w[0], m_conv_w[0], v_conv_w[0])
    small_res = update(small_parts, _pack_small(like), _pack_small([small_m[n] for n in small_names]),
                       _pack_small([small_v[n] for n in small_names]))
    small_res = [_unpack_small(t, like) for t in small_res]
    for idx, n in enumerate(small_names):
        res[n] = tuple(t[idx] for t in small_res)

    order = ["norm_mix_pre", "w_in", "w_pool", "pool_scale", "attn_scale", "w_out", "norm_mix_post", "norm_ffn_pre",
             "w_up", "conv_w", "conv_b", "w_down", "norm_ffn_post"]
    shaped = {n: tuple(t.reshape(s.shape) for t in res[n])
              for n, s in dict(norm_mix_pre=norm_mix_pre, w_in=w_in, w_pool=w_pool, pool_scale=pool_scale, attn_scale=attn_scale,
                               w_out=w_out, norm_mix_post=norm_mix_post, norm_ffn_pre=norm_ffn_pre, w_up=w_up, conv_w=conv_w,
                               conv_b=conv_b, w_down=w_down, norm_ffn_post=norm_ffn_post).items()}
    outs = [loss, r["dx"].reshape(x.shape)]
    for k in range(4):
        outs += [shaped[n][k] for n in order]
    return tuple(outs)
```

```python
import functools

import jax
import jax.numpy as jnp
from jax import lax
from jax.experimental import pallas as pl
from jax.experimental.pallas import tpu as pltpu

F32 = jnp.float32
BF16 = jnp.bfloat16
HIGHEST = lax.Precision.HIGHEST

N_DEV = 8
EPS = 1e-6
POOL_WINDOWS = (2, 4, 8, 16)
POOL_GROUP = 128
HALO = 16
HEAD_DIM = 64
QB = 128
ATTN_SCALE = HEAD_DIM ** -0.5
ATTN_SUBS = 4
EXP_UNDERFLOW = -88.0
D_FF_SHARDS = 4

ADAM_LR = 0.001
ADAM_B1 = 0.9
ADAM_B2 = 0.999
ADAM_EPS = 1e-08
ADAM_WD = 0.01
ADAM_STEP = 10

VMEM_LIMIT_V7X = 56 * 1024 * 1024
MESH = pl.DeviceIdType.MESH


def _params(*semantics):
    return pltpu.CompilerParams(dimension_semantics=semantics, vmem_limit_bytes=VMEM_LIMIT_V7X)


def _const(shape):
    zeros = (0,) * len(shape)
    return pl.BlockSpec(shape, lambda *_: zeros, pipeline_mode=pl.Buffered(1))


def _dot(a, b):
    return jnp.dot(a, b, preferred_element_type=F32)


def _dot_nt(a, b):
    return lax.dot_general(a, b, (((1,), (1,)), ((), ())), preferred_element_type=F32)


def _dot_tn(a, b):
    return lax.dot_general(a, b, (((0,), (0,)), ((), ())), preferred_element_type=F32)


def _rms(v):
    return lax.rsqrt(jnp.mean(v * v, axis=-1, keepdims=True) + EPS)


def _norm_bwd(dn_times_gain, n, r):
    return r * (dn_times_gain - n * jnp.mean(dn_times_gain * n, axis=-1, keepdims=True))


def _zero_when(first, *refs):
    @pl.when(first)
    def _():
        for ref in refs:
            ref[...] = jnp.zeros_like(ref)


def _colsum(v):
    return jnp.sum(v, axis=0, keepdims=True)


def _grid_ends(grid):
    ids = [pl.program_id(a) for a in range(len(grid))]
    first = functools.reduce(jnp.logical_and, [i == 0 for i in ids])
    last = functools.reduce(jnp.logical_and, [i == n - 1 for i, n in zip(ids, grid)])
    return first, last


def _fwd_inproj(x, g1, w_in_g, tile):
    S, D = x.shape
    nb, _, cs = w_in_g.shape
    d_pool = 2 * cs

    def body(x_ref, g_ref, w_ref, ht_ref, u_ref, qkv_ref):
        xf = x_ref[...]
        h = (xf * _rms(xf) * g_ref[...]).astype(BF16)
        ht_ref[...] = h.T
        for d in range(nb):
            o = _dot(h, w_ref[d])
            if d < 2:
                u_ref[:, d * cs:(d + 1) * cs] = o
            else:
                qkv_ref[:, (d - 2) * cs:(d - 1) * cs] = o.astype(BF16)

    return pl.pallas_call(
        body, name="fwd_inproj", grid=(S // tile,),
        in_specs=[pl.BlockSpec((tile, D), lambda i: (i, 0)), _const((1, D)), _const(w_in_g.shape)],
        out_specs=[pl.BlockSpec((D, tile), lambda i: (0, i)), pl.BlockSpec((tile, d_pool), lambda i: (i, 0)),
                   pl.BlockSpec((tile, 3 * d_pool), lambda i: (i, 0))],
        out_shape=[jax.ShapeDtypeStruct((D, S), BF16), jax.ShapeDtypeStruct((S, d_pool), F32),
                   jax.ShapeDtypeStruct((S, 3 * d_pool), BF16)],
        compiler_params=_params("parallel"),
    )(x, g1, w_in_g)


def _window_sums(ext, forward):
    n = ext.shape[0]
    sums, s, sh = {}, ext, 1
    while sh < POOL_WINDOWS[-1]:
        s = s + pltpu.roll(s, (n - sh) if forward else sh, axis=0)
        sh *= 2
        sums[sh] = s
    return sums


def _pool_counts(t0, rows):
    t1 = (lax.broadcasted_iota(jnp.int32, (rows, 1), 0) + t0 + 1).astype(F32)
    return [jnp.minimum(t1, float(w)) for w in POOL_WINDOWS]


def _pool_deviation(u, halo, t0):
    T = u.shape[0]
    sums = _window_sums(jnp.concatenate([halo, u], axis=0), forward=False)
    counts = _pool_counts(t0, T)
    parts = []
    for g, w in enumerate(POOL_WINDOWS):
        lanes = slice(g * POOL_GROUP, (g + 1) * POOL_GROUP)
        parts.append(sums[w][HALO:, lanes] / counts[g] - u[:, lanes])
    return parts


def _prev_halo_spec(tile, width):
    return pl.BlockSpec((HALO, width), lambda i: (jnp.maximum(i * (tile // HALO) - 1, 0), 0))


def _next_halo_spec(tile, width, n_tiles):
    last = n_tiles * (tile // HALO) - 1
    return pl.BlockSpec((HALO, width), lambda i: (jnp.minimum((i + 1) * (tile // HALO), last), 0))


def _fwd_pool(u, w_pool, tile):
    S, C = u.shape

    def body(u_ref, halo_ref, wp_ref, o_ref):
        i = pl.program_id(0)
        halo = jnp.where(i > 0, halo_ref[...], 0.0)
        parts = _pool_deviation(u_ref[...], halo, i * tile)
        for g, p in enumerate(parts):
            o_ref[:, g * POOL_GROUP:(g + 1) * POOL_GROUP] = _dot(p.astype(BF16), wp_ref[g].astype(BF16))

    return pl.pallas_call(
        body, name="fwd_pool", grid=(S // tile,),
        in_specs=[pl.BlockSpec((tile, C), lambda i: (i, 0)), _prev_halo_spec(tile, C), _const(w_pool.shape)],
        out_specs=pl.BlockSpec((tile, C), lambda i: (i, 0)),
        out_shape=jax.ShapeDtypeStruct((S, C), F32),
        compiler_params=_params("parallel"),
    )(u, u, w_pool)


def _low_lanes():
    return lax.broadcasted_iota(jnp.int32, (QB, 2 * HEAD_DIM), 1) < HEAD_DIM


SPLIT = 3


def _triangle(inclusive):
    row = lax.broadcasted_iota(jnp.int32, (SPLIT * QB, QB), 0) % QB
    col = lax.broadcasted_iota(jnp.int32, (SPLIT * QB, QB), 1)
    return ((row >= col) if inclusive else (row > col)).astype(BF16)


def _pieces(v):
    out, rest = [], v
    for _ in range(SPLIT - 1):
        piece = rest.astype(BF16)
        out.append(piece)
        rest = rest - piece.astype(F32)
    out.append(rest.astype(BF16))
    return jnp.concatenate(out, axis=1)


def _causal_mask(width, offset):
    row = lax.broadcasted_iota(jnp.int32, (QB, width), 0)
    col = lax.broadcasted_iota(jnp.int32, (QB, width), 1)
    return col < row + offset


def _row_sums(vals, carry):
    for b in reversed(range(vals.shape[1] // QB)):
        carry = carry + jnp.sum(vals[:, b * QB:(b + 1) * QB], axis=1, keepdims=True)
    return carry


def _suffix_sums(vals, tri, carry):
    n = vals.shape[1] // QB
    out, run = [None] * n, carry
    for b in reversed(range(n)):
        blk = vals[:, b * QB:(b + 1) * QB]
        out[b] = _dot(_pieces(blk), tri) + run
        run = run + jnp.sum(blk, axis=1, keepdims=True)
    return (out[0] if n == 1 else jnp.concatenate(out, axis=1)), run


def _attn_tiles(qhs, kws, masks, carries, after_s):
    return _attn_weights(_attn_scores(qhs, kws, masks), masks, carries, after_s)


def _attn_scores(qhs, kws, masks):
    zs = [_dot_nt(qh, kw) * ATTN_SCALE for qh, kw in zip(qhs, kws)]
    es = [jnp.exp(-jnp.abs(z)) for z in zs]
    softplus = [jnp.maximum(z, 0.0) + jnp.log(1.0 + e) for z, e in zip(zs, es)]
    log_1m_beta = [-sp if m is None else jnp.where(m, -sp, 0.0) for sp, m in zip(softplus, masks)]
    return list(zip(zs, es, softplus, log_1m_beta))


def _attn_weights(scores, masks, carries, after_s):
    sums = [_suffix_sums(l, after_s, c) for (_, _, _, l), c in zip(scores, carries)]
    weights = [jnp.exp(z - sp + st) for (z, _, sp, _), (st, _) in zip(scores, sums)]
    weights = [a if m is None else jnp.where(m, a, 0.0) for a, m in zip(weights, masks)]
    return [(z, e, a, c) for (z, e, _, _), a, (_, c) in zip(scores, weights, sums)]


def _attn_tile(qh, kw, mask, carry, after_s):
    return _attn_tiles([qh], [kw], [mask], [carry], after_s)[0]


def _split_heads(v, low_lanes):
    return jnp.where(low_lanes, v, 0.0).astype(BF16), jnp.where(low_lanes, 0.0, v).astype(BF16)


def _sweep_done(c0, c1):
    return (jnp.maximum(jnp.max(c0), jnp.max(c1)) < EXP_UNDERFLOW).astype(jnp.int32)


def _first_window(i):
    first_blk = jnp.maximum(i - 1, 0)
    return first_blk, pl.multiple_of(first_blk * QB, QB), (i - first_blk) * QB


def _fwd_attn(qkv, n_pairs, ex):
    S = qkv.shape[0]
    n_steps = S // (ATTN_SUBS * QB)

    def body(q_ref, k_ref, v_ref, *rest):
        o_ref = rest[ex.n]
        ex_refs = ex.split(rest[:ex.n] + rest[ex.n + 1:])
        first_step, last_step = _grid_ends((n_pairs, n_steps))

        @pl.when(first_step)
        def _():
            ex.start(*ex_refs)

        low_lanes = _low_lanes()
        after_s = _triangle(False)
        zero = jnp.zeros((QB, 1), F32)

        def cond(c):
            return jnp.logical_and(c[0] >= 0, c[1] == 0)

        qhs, kws, vws, masks, first_blks = [], [], [], [], []
        for sub in range(ATTN_SUBS):
            i = pl.program_id(1) * ATTN_SUBS + sub
            first_blk, start, offset = _first_window(i)
            first_blks.append(first_blk)
            qhs += _split_heads(q_ref[sub * QB:(sub + 1) * QB, :].astype(F32), low_lanes)
            kws += [k_ref[pl.ds(start, 2 * QB), :]] * 2
            vws += [v_ref[pl.ds(start, 2 * QB), :]] * 2
            masks += [_causal_mask(2 * QB, offset)] * 2
        tiles = _attn_tiles(qhs, kws, masks, [zero] * len(qhs), after_s)
        outs = [_dot(t[2].astype(BF16), vw) for t, vw in zip(tiles, vws)]

        for sub in range(ATTN_SUBS):
            def step(c, qh=qhs[2 * sub:2 * sub + 2]):
                j, _, acc, c0, c1 = c
                at = pl.multiple_of(j * QB, QB)
                kb = k_ref[pl.ds(at, QB), :]
                vb = v_ref[pl.ds(at, QB), :]
                far = _attn_tiles(qh, [kb, kb], [None, None], [c0, c1], after_s)
                acc = acc + jnp.where(low_lanes, _dot(far[0][2].astype(BF16), vb), _dot(far[1][2].astype(BF16), vb))
                return j - 1, _sweep_done(far[0][3], far[1][3]), acc, far[0][3], far[1][3]

            c0, c1 = tiles[2 * sub][3], tiles[2 * sub + 1][3]
            init = (first_blks[sub] - 1, _sweep_done(c0, c1), jnp.where(low_lanes, outs[2 * sub], outs[2 * sub + 1]), c0, c1)
            o_ref[sub * QB:(sub + 1) * QB, :] = lax.while_loop(cond, step, init)[2]

        @pl.when(last_step)
        def _():
            ex.wait(*ex_refs)

    outs = pl.pallas_call(
        body, name="fwd_attn", grid=(n_pairs, n_steps),
        in_specs=[pl.BlockSpec((ATTN_SUBS * QB, QB), lambda p, i: (i, p)),
                  pl.BlockSpec((S, QB), lambda p, i: (0, n_pairs + p), pipeline_mode=pl.Buffered(1)),
                  pl.BlockSpec((S, QB), lambda p, i: (0, 2 * n_pairs + p), pipeline_mode=pl.Buffered(1))] + ex.specs,
        out_specs=[pl.BlockSpec((ATTN_SUBS * QB, QB), lambda p, i: (i, p))] + ex.specs,
        out_shape=[jax.ShapeDtypeStruct((S, n_pairs * QB), F32)] + ex.out_shape,
        scratch_shapes=ex.scratch,
        compiler_params=_params("arbitrary", "arbitrary"),
    )(qkv, qkv, qkv, *ex.arrays)
    return outs[0], outs[1:]


def _normalized_heads(pool_out, attn_out):
    rp, ra = _rms(pool_out), _rms(attn_out)
    return pool_out * rp, rp, attn_out * ra, ra


def _fwd_outproj(pool_out, attn_out, pool_scale, attn_scale, w_out, x, g2, g3, tile):
    S, D = x.shape
    C = pool_out.shape[1]

    def body(p_ref, a_ref, ps_ref, as_ref, w_ref, x_ref, g2_ref, g3_ref, mix_ref, x2_ref, h2_ref, h2t_ref):
        n_p, _, n_a, _ = _normalized_heads(p_ref[...], a_ref[...])
        mix = _dot((n_p * ps_ref[...]).astype(BF16), w_ref[:C, :]) + _dot((n_a * as_ref[...]).astype(BF16), w_ref[C:, :])
        mix_ref[...] = mix
        x2 = x_ref[...] + mix * _rms(mix) * g2_ref[...]
        x2_ref[...] = x2
        h2 = (x2 * _rms(x2) * g3_ref[...]).astype(BF16)
        h2_ref[...] = h2
        h2t_ref[...] = h2.T

    row = lambda w: pl.BlockSpec((tile, w), lambda i: (i, 0))
    return pl.pallas_call(
        body, name="fwd_outproj", grid=(S // tile,),
        in_specs=[row(C), row(C), _const((1, C)), _const((1, C)), _const(w_out.shape), row(D), _const((1, D)), _const((1, D))],
        out_specs=[row(D), row(D), row(D), pl.BlockSpec((D, tile), lambda i: (0, i))],
        out_shape=[jax.ShapeDtypeStruct((S, D), F32), jax.ShapeDtypeStruct((S, D), F32), jax.ShapeDtypeStruct((S, D), BF16),
                   jax.ShapeDtypeStruct((D, S), BF16)],
        compiler_params=_params("parallel"),
    )(pool_out, attn_out, pool_scale, attn_scale, w_out, x, g2, g3)


def _fwd_up(h2, w_up_g, tile):
    S, D = h2.shape
    nb, _, cs = w_up_g.shape

    def body(h_ref, w_ref, o_ref):
        h = h_ref[...]
        for d in range(nb):
            o_ref[d] = _dot(h, w_ref[d]).astype(BF16)

    return pl.pallas_call(
        body, name="fwd_up", grid=(S // tile,),
        in_specs=[pl.BlockSpec((tile, D), lambda i: (i, 0)), _const(w_up_g.shape)],
        out_specs=pl.BlockSpec((nb, tile, cs), lambda i: (0, i, 0)),
        out_shape=jax.ShapeDtypeStruct((nb, S, cs), BF16),
        compiler_params=_params("parallel"),
    )(h2, w_up_g)


def _conv_taps(tile_rows, halo_rows):
    T = tile_rows.shape[0]
    ext = jnp.concatenate([halo_rows.astype(F32), tile_rows.astype(F32)], axis=0)
    return pltpu.roll(ext, 2, axis=0)[HALO:], pltpu.roll(ext, 1, axis=0)[HALO:], ext[HALO:]


def _tap_rows(cw_ref, d):
    return [cw_ref[d, k:k + 1, :] for k in range(3)]


def _gated_unit(taps_gate, taps_val, cw_gate, cw_val, cb_gate, cb_val):
    gate = cw_gate[0] * taps_gate[0] + cw_gate[1] * taps_gate[1] + cw_gate[2] * taps_gate[2] + cb_gate
    val = cw_val[0] * taps_val[0] + cw_val[1] * taps_val[1] + cw_val[2] * taps_val[2] + cb_val
    sig = 1.0 / (1.0 + jnp.exp(-gate))
    return gate, val, sig


def _fwd_ffn_loss(upre, conv_w_g, conv_b_g, w_down4, x2, target, g4, tile):
    nb, S, cs = upre.shape
    D = x2.shape[1]

    def body(u_ref, halo_ref, cw_ref, cb_ref, wd_ref, x2_ref, t_ref, g4_ref, dy_ref, df_ref, loss_ref, dg4_ref):
        i = pl.program_id(0)
        first = i == 0
        _zero_when(first, loss_ref, dg4_ref)
        f = jnp.zeros((tile, D), F32)
        for s in range(D_FF_SHARDS):
            halo_g = jnp.where(first, jnp.zeros_like(halo_ref[s]), halo_ref[s])
            halo_v = jnp.where(first, jnp.zeros_like(halo_ref[s]), halo_ref[s + D_FF_SHARDS])
            gate, val, sig = _gated_unit(_conv_taps(u_ref[s], halo_g), _conv_taps(u_ref[s + D_FF_SHARDS], halo_v),
                                         _tap_rows(cw_ref, s), _tap_rows(cw_ref, s + D_FF_SHARDS), cb_ref[s], cb_ref[s + D_FF_SHARDS])
            f = f + _dot((gate * sig * val).astype(BF16), wd_ref[s])
        r4 = _rms(f)
        n4 = f * r4
        err = x2_ref[...] + n4 * g4_ref[...] - t_ref[...]
        dy = err * (1.0 / D)
        dy_ref[...] = dy
        df_ref[...] = _norm_bwd(dy * g4_ref[...], n4, r4).astype(BF16)
        loss_ref[...] += _colsum(err * err)
        dg4_ref[...] += _colsum(dy * n4)

    row = lambda w: pl.BlockSpec((tile, w), lambda i: (i, 0))
    return pl.pallas_call(
        body, name="fwd_ffn_loss", grid=(S // tile,),
        in_specs=[pl.BlockSpec((nb, tile, cs), lambda i: (0, i, 0)),
                  pl.BlockSpec((nb, HALO, cs), lambda i: (0, jnp.maximum(i * (tile // HALO) - 1, 0), 0)),
                  _const(conv_w_g.shape), _const(conv_b_g.shape), _const(w_down4.shape), row(D), row(D), _const((1, D))],
        out_specs=[row(D), row(D), pl.BlockSpec((1, D), lambda i: (0, 0)), pl.BlockSpec((1, D), lambda i: (0, 0))],
        out_shape=[jax.ShapeDtypeStruct((S, D), F32), jax.ShapeDtypeStruct((S, D), BF16),
                   jax.ShapeDtypeStruct((1, D), F32), jax.ShapeDtypeStruct((1, D), F32)],
        compiler_params=_params("arbitrary"),
    )(upre, upre, conv_w_g, conv_b_g, w_down4, x2, target, g4)


def _bwd_down(upre, conv_w_g, conv_b_g, w_down4, df, tile):
    nb, S, cs = upre.shape
    D = df.shape[1]
    n_tiles = S // tile

    def body(ug_ref, uv_ref, hg_ref, hv_ref, cwg_ref, cwv_ref, cbg_ref, cbv_ref, wd_ref, df_ref,
             dg_ref, dv_ref, dwd_ref, dbg_ref, dbv_ref, dcwg_ref, dcwv_ref):
        i = pl.program_id(1)
        first = i == 0
        _zero_when(first, dwd_ref, dbg_ref, dbv_ref, dcwg_ref, dcwv_ref)
        halo_g = jnp.where(first, jnp.zeros_like(hg_ref[0]), hg_ref[0])
        halo_v = jnp.where(first, jnp.zeros_like(hv_ref[0]), hv_ref[0])
        taps_g, taps_v = _conv_taps(ug_ref[0], halo_g), _conv_taps(uv_ref[0], halo_v)
        gate, val, sig = _gated_unit(taps_g, taps_v, _tap_rows(cwg_ref, 0), _tap_rows(cwv_ref, 0), cbg_ref[0], cbv_ref[0])
        silu = gate * sig
        dfb = df_ref[...]
        dact = _dot_nt(dfb, wd_ref[0])
        dwd_ref[0] += _dot_tn((silu * val).astype(BF16), dfb)
        dgate = dact * val * (sig * (1.0 + gate * (1.0 - sig)))
        dval = dact * silu
        dg_ref[0] = dgate.astype(BF16)
        dv_ref[0] = dval.astype(BF16)
        dbg_ref[0] += _colsum(dgate)
        dbv_ref[0] += _colsum(dval)
        for k in range(3):
            dcwg_ref[0, k:k + 1, :] += _colsum(dgate * taps_g[k])
            dcwv_ref[0, k:k + 1, :] += _colsum(dval * taps_v[k])

    half = D_FF_SHARDS
    blk = lambda off: pl.BlockSpec((1, tile, cs), lambda s, i: (s + off, i, 0))
    halo = lambda off: pl.BlockSpec((1, HALO, cs), lambda s, i: (s + off, jnp.maximum(i * (tile // HALO) - 1, 0), 0))
    par = lambda off, r: pl.BlockSpec((1, r, cs), lambda s, i: (s + off, 0, 0))
    outs = pl.pallas_call(
        body, name="bwd_down", grid=(half, n_tiles),
        in_specs=[blk(0), blk(half), halo(0), halo(half), par(0, 3), par(half, 3), par(0, 1), par(half, 1),
                  pl.BlockSpec((1, cs, D), lambda s, i: (s, 0, 0)), pl.BlockSpec((tile, D), lambda s, i: (i, 0))],
        out_specs=[blk(0), blk(0), pl.BlockSpec((1, cs, D), lambda s, i: (s, 0, 0)),
                   par(0, 1), par(0, 1), par(0, 3), par(0, 3)],
        out_shape=[jax.ShapeDtypeStruct((half, S, cs), BF16), jax.ShapeDtypeStruct((half, S, cs), BF16),
                   jax.ShapeDtypeStruct((half, cs, D), F32),
                   jax.ShapeDtypeStruct((half, 1, cs), F32), jax.ShapeDtypeStruct((half, 1, cs), F32),
                   jax.ShapeDtypeStruct((half, 3, cs), F32), jax.ShapeDtypeStruct((half, 3, cs), F32)],
        compiler_params=_params("parallel", "arbitrary"),
    )(upre, upre, upre, upre, conv_w_g, conv_w_g, conv_b_g, conv_b_g, w_down4, df)
    dgate, dval, d_wd, dbg, dbv, dcwg, dcwv = outs
    return dgate, dval, d_wd, jnp.concatenate([dbg, dbv], axis=0), jnp.concatenate([dcwg, dcwv], axis=0)


def _bwd_up_x(dgate, dval, conv_w_g, w_up_g, x2, dy, mix, g2, g3, tile):
    half, S, cs = dgate.shape
    nb = 2 * half
    D = x2.shape[1]
    n_tiles = S // tile

    def body(dg_ref, dv_ref, hg_ref, hv_ref, cw_ref, w_ref, x2_ref, dy_ref, mix_ref, g2_ref, g3_ref,
             dupre_ref, dx2_ref, dmix_ref, dg3_ref, dg2_ref):
        i = pl.program_id(0)
        last = i == n_tiles - 1
        _zero_when(i == 0, dg3_ref, dg2_ref)
        dh2 = jnp.zeros((tile, D), F32)
        for d in range(nb):
            src, halo = (dg_ref, hg_ref) if d < half else (dv_ref, hv_ref)
            nxt = jnp.where(last, jnp.zeros_like(halo[d % half]), halo[d % half])
            ext = jnp.concatenate([src[d % half].astype(F32), nxt.astype(F32)], axis=0)
            n = ext.shape[0]
            cw = _tap_rows(cw_ref, d)
            dupre = (cw[2] * ext + cw[1] * pltpu.roll(ext, n - 1, axis=0) + cw[0] * pltpu.roll(ext, n - 2, axis=0))[:tile]
            dupre = dupre.astype(BF16)
            dupre_ref[d] = dupre
            dh2 = dh2 + _dot_nt(dupre, w_ref[d])
        x2 = x2_ref[...]
        r3 = _rms(x2)
        n3 = x2 * r3
        dg3_ref[...] += _colsum(dh2 * n3)
        dx2 = dy_ref[...] + _norm_bwd(dh2 * g3_ref[...], n3, r3)
        dx2_ref[...] = dx2
        mix = mix_ref[...]
        r2 = _rms(mix)
        n2 = mix * r2
        dg2_ref[...] += _colsum(dx2 * n2)
        dmix_ref[...] = _norm_bwd(dx2 * g2_ref[...], n2, r2).astype(BF16)

    row = lambda w: pl.BlockSpec((tile, w), lambda i: (i, 0))
    blk = pl.BlockSpec((half, tile, cs), lambda i: (0, i, 0))
    last_halo = n_tiles * (tile // HALO) - 1
    halo = pl.BlockSpec((half, HALO, cs), lambda i: (0, jnp.minimum((i + 1) * (tile // HALO), last_halo), 0))
    acc = pl.BlockSpec((1, D), lambda i: (0, 0))
    return pl.pallas_call(
        body, name="bwd_up_x", grid=(n_tiles,),
        in_specs=[blk, blk, halo, halo, _const(conv_w_g.shape), _const(w_up_g.shape), row(D), row(D), row(D),
                  _const((1, D)), _const((1, D))],
        out_specs=[pl.BlockSpec((nb, tile, cs), lambda i: (0, i, 0)), row(D), row(D), acc, acc],
        out_shape=[jax.ShapeDtypeStruct((nb, S, cs), BF16), jax.ShapeDtypeStruct((S, D), F32),
                   jax.ShapeDtypeStruct((S, D), BF16), jax.ShapeDtypeStruct((1, D), F32), jax.ShapeDtypeStruct((1, D), F32)],
        compiler_params=_params("arbitrary"),
    )(dgate, dval, dgate, dval, conv_w_g, w_up_g, x2, dy, mix, g2, g3)


def _bwd_weight(act_t, dout, tile):
    D, S = act_t.shape
    nb, _, cs = dout.shape

    def body(a_ref, d_ref, o_ref):
        _zero_when(pl.program_id(1) == 0, o_ref)
        o_ref[0] += _dot(a_ref[...], d_ref[0])

    return pl.pallas_call(
        body, name="bwd_w_up", grid=(nb, S // tile),
        in_specs=[pl.BlockSpec((D, tile), lambda d, i: (0, i)), pl.BlockSpec((1, tile, cs), lambda d, i: (d, i, 0))],
        out_specs=pl.BlockSpec((1, D, cs), lambda d, i: (d, 0, 0)),
        out_shape=jax.ShapeDtypeStruct((nb, D, cs), F32),
        compiler_params=_params("parallel", "arbitrary"),
    )(act_t, dout)


def _bwd_outproj(dmix, w_out, pool_out, attn_out, pool_scale, attn_scale, tile):
    S, D = dmix.shape
    C = pool_out.shape[1]

    def body(dm_ref, w_ref, p_ref, a_ref, ps_ref, as_ref, dp_ref, da_ref, dw_ref, dps_ref, das_ref):
        _zero_when(pl.program_id(0) == 0, dw_ref, dps_ref, das_ref)
        dmx = dm_ref[...]
        dmerged = _dot_nt(dmx, w_ref[...])
        n_p, r_p, n_a, r_a = _normalized_heads(p_ref[...], a_ref[...])
        merged = jnp.concatenate([(n_p * ps_ref[...]).astype(BF16), (n_a * as_ref[...]).astype(BF16)], axis=1)
        dw_ref[...] += _dot_tn(merged, dmx)
        dm_p, dm_a = dmerged[:, :C], dmerged[:, C:]
        dps_ref[...] += _colsum(dm_p * n_p)
        das_ref[...] += _colsum(dm_a * n_a)
        dp_ref[...] = _norm_bwd(dm_p * ps_ref[...], n_p, r_p)
        da_ref[...] = _norm_bwd(dm_a * as_ref[...], n_a, r_a)

    row = lambda w: pl.BlockSpec((tile, w), lambda i: (i, 0))
    return pl.pallas_call(
        body, name="bwd_outproj", grid=(S // tile,),
        in_specs=[row(D), _const(w_out.shape), row(C), row(C), _const((1, C)), _const((1, C))],
        out_specs=[row(C), row(C), pl.BlockSpec(w_out.shape, lambda i: (0, 0)),
                   pl.BlockSpec((1, C), lambda i: (0, 0)), pl.BlockSpec((1, C), lambda i: (0, 0))],
        out_shape=[jax.ShapeDtypeStruct((S, C), F32), jax.ShapeDtypeStruct((S, C), F32),
                   jax.ShapeDtypeStruct(w_out.shape, F32), jax.ShapeDtypeStruct((1, C), F32), jax.ShapeDtypeStruct((1, C), F32)],
        compiler_params=_params("arbitrary"),
    )(dmix, w_out, pool_out, attn_out, pool_scale, attn_scale)


def _bwd_attn(qkv, d_attn, n_pairs, ex):
    S = qkv.shape[0]
    n_steps = S // (ATTN_SUBS * QB)

    def body(q_ref, k_ref, v_ref, do_ref, *rest):
        dq_ref, dk_ref, dv_ref = rest[ex.n:ex.n + 3]
        ex_refs = ex.split(rest[:ex.n] + rest[ex.n + 3:])
        first_step, last_step = _grid_ends((n_pairs, n_steps))

        @pl.when(first_step)
        def _():
            ex.start(*ex_refs)

        @pl.when(pl.program_id(1) == 0)
        def _():
            dk_ref[...] = jnp.zeros_like(dk_ref)
            dv_ref[...] = jnp.zeros_like(dv_ref)

        low_lanes = _low_lanes()
        after_s, from_s = _triangle(False), _triangle(True)
        zero = jnp.zeros((QB, 1), F32)

        def tiles(qhs, dohs, totals, kws, vws, masks, cs, gs, scores=None):
            fw = _attn_weights(scores or _attn_scores(qhs, kws, masks), masks, cs, after_s)
            gvals = [t[2] * _dot_nt(doh, vw) for t, doh, vw in zip(fw, dohs, vws)]
            sums = [_suffix_sums(g, from_s, g0) for g, g0 in zip(gvals, gs)]
            totals = [tot if m is None else tot + sm[1] for tot, m, sm in zip(totals, masks, sums)]
            dzs = []
            for (z, e, _, _), g, (nearer, _), tot, m in zip(fw, gvals, sums, totals, masks):
                inv = 1.0 / (1.0 + e)
                sig_abs, sig_neg = inv, e * inv
                pos = z >= 0.0
                dz = g * jnp.where(pos, sig_neg, sig_abs) - jnp.where(pos, sig_abs, sig_neg) * (tot - nearer)
                if m is not None:
                    dz = jnp.where(m, dz, 0.0)
                dzs.append((dz * ATTN_SCALE).astype(BF16))
            dqs = [_dot(dz, kw) for dz, kw in zip(dzs, kws)]
            dks = [_dot_tn(dz, qh) for dz, qh in zip(dzs, qhs)]
            dvs = [_dot_tn(t[2].astype(BF16), doh) for t, doh in zip(fw, dohs)]
            return [(dq, dk, dv, t[3], sm[1], tot) for dq, dk, dv, t, sm, tot in zip(dqs, dks, dvs, fw, sums, totals)]

        def cond(c):
            return jnp.logical_and(c[0] >= 0, c[1] == 0)

        qhs, dohs, kws, vws, masks, first_blks, starts = [], [], [], [], [], [], []
        for sub in range(ATTN_SUBS):
            i = pl.program_id(1) * ATTN_SUBS + sub
            rows = slice(sub * QB, (sub + 1) * QB)
            first_blk, start, offset = _first_window(i)
            first_blks.append(first_blk)
            starts.append(start)
            qhs += _split_heads(q_ref[rows, :].astype(F32), low_lanes)
            dohs += _split_heads(do_ref[rows, :], low_lanes)
            kws += [k_ref[pl.ds(start, 2 * QB), :]] * 2
            vws += [v_ref[pl.ds(start, 2 * QB), :]] * 2
            masks += [_causal_mask(2 * QB, offset)] * 2
        zeros = [zero] * len(qhs)

        scores = _attn_scores(qhs, kws, masks)
        c_first = [_row_sums(sc[3], zero) for sc in scores]
        beyond_first = []
        for sub in range(ATTN_SUBS):
            pair = slice(2 * sub, 2 * sub + 2)

            def far_sums(c, qh=qhs[pair], doh=dohs[pair]):
                j, _, c0, c1, r0, r1 = c
                at = pl.multiple_of(j * QB, QB)
                kb = k_ref[pl.ds(at, QB), :]
                vb = v_ref[pl.ds(at, QB), :]
                far = _attn_tiles(qh, [kb, kb], [None, None], [c0, c1], after_s)
                r0 = r0 + jnp.sum(far[0][2] * _dot_nt(doh[0], vb), axis=1, keepdims=True)
                r1 = r1 + jnp.sum(far[1][2] * _dot_nt(doh[1], vb), axis=1, keepdims=True)
                return j - 1, _sweep_done(far[0][3], far[1][3]), far[0][3], far[1][3], r0, r1

            c0, c1 = c_first[pair]
            far = lax.while_loop(cond, far_sums, (first_blks[sub] - 1, _sweep_done(c0, c1), c0, c1, zero, zero))
            beyond_first += [far[4], far[5]]

        done = tiles(qhs, dohs, beyond_first, kws, vws, masks, zeros, zeros, scores)
        for sub in range(ATTN_SUBS):
            dk_ref[pl.ds(starts[sub], 2 * QB), :] += done[2 * sub][1] + done[2 * sub + 1][1]
            dv_ref[pl.ds(starts[sub], 2 * QB), :] += done[2 * sub][2] + done[2 * sub + 1][2]

        for sub in range(ATTN_SUBS):
            pair = slice(2 * sub, 2 * sub + 2)
            t0, t1 = done[pair]

            def step(c, qh=qhs[pair], doh=dohs[pair], total=[t0[5], t1[5]]):
                j, _, dq, c0, c1, s0, s1 = c
                at = pl.multiple_of(j * QB, QB)
                kb = k_ref[pl.ds(at, QB), :]
                vb = v_ref[pl.ds(at, QB), :]
                f0, f1 = tiles(qh, doh, total, [kb, kb], [vb, vb], [None, None], [c0, c1], [s0, s1])
                dk_ref[pl.ds(at, QB), :] += f0[1] + f1[1]
                dv_ref[pl.ds(at, QB), :] += f0[2] + f1[2]
                return j - 1, _sweep_done(f0[3], f1[3]), dq + jnp.where(low_lanes, f0[0], f1[0]), f0[3], f1[3], f0[4], f1[4]

            init = (first_blks[sub] - 1, _sweep_done(t0[3], t1[3]), jnp.where(low_lanes, t0[0], t1[0]), t0[3], t1[3], t0[4], t1[4])
            dq_ref[sub * QB:(sub + 1) * QB, :] = lax.while_loop(cond, step, init)[2]

        @pl.when(last_step)
        def _():
            ex.wait(*ex_refs)

    blk = pl.BlockSpec((ATTN_SUBS * QB, QB), lambda p, i: (i, p))
    full = lambda off: pl.BlockSpec((S, QB), lambda p, i: (0, off + p), pipeline_mode=pl.Buffered(1))
    outs = pl.pallas_call(
        body, name="bwd_attn", grid=(n_pairs, n_steps),
        in_specs=[blk, full(n_pairs), full(2 * n_pairs), blk] + ex.specs,
        out_specs=[blk, pl.BlockSpec((S, QB), lambda p, i: (0, p)), pl.BlockSpec((S, QB), lambda p, i: (0, p))] + ex.specs,
        out_shape=[jax.ShapeDtypeStruct((S, n_pairs * QB), F32)] * 3 + ex.out_shape,
        scratch_shapes=ex.scratch,
        compiler_params=_params("arbitrary", "arbitrary"),
    )(qkv, qkv, qkv, d_attn, *ex.arrays)
    return outs[0], outs[1], outs[2], outs[3:]


def _bwd_pool(u, d_pool, w_pool, tile):
    S, C = u.shape
    n_tiles = S // tile
    ng = len(POOL_WINDOWS)

    def body(u_ref, uh_ref, d_ref, dh_ref, wp_ref, du_ref, dwp_ref):
        i = pl.program_id(0)
        first = i == 0
        _zero_when(first, dwp_ref)
        halo = jnp.where(first, 0.0, uh_ref[...])
        parts = _pool_deviation(u_ref[...], halo, i * tile)
        dout = d_ref[...]
        nxt = jnp.where(i == n_tiles - 1, 0.0, dh_ref[...])
        dext = jnp.concatenate([dout, nxt], axis=0).astype(BF16)
        counts = _pool_counts(i * tile, tile + HALO)
        dps, scaled = [], []
        for g in range(ng):
            lanes = slice(g * POOL_GROUP, (g + 1) * POOL_GROUP)
            dp = _dot_nt(dext[:, lanes], wp_ref[g].astype(BF16))
            dps.append(dp[:tile])
            scaled.append(dp / counts[g])
        sums = _window_sums(jnp.concatenate(scaled, axis=1), forward=True)
        for g, w in enumerate(POOL_WINDOWS):
            lanes = slice(g * POOL_GROUP, (g + 1) * POOL_GROUP)
            du_ref[:, lanes] = sums[w][:tile, lanes] - dps[g]
            dwp_ref[g] += _dot_tn(parts[g].astype(BF16), dext[:tile, lanes])

    row = pl.BlockSpec((tile, C), lambda i: (i, 0))
    return pl.pallas_call(
        body, name="bwd_pool", grid=(n_tiles,),
        in_specs=[row, _prev_halo_spec(tile, C), row, _next_halo_spec(tile, C, n_tiles), _const(w_pool.shape)],
        out_specs=[row, pl.BlockSpec(w_pool.shape, lambda i: (0, 0, 0))],
        out_shape=[jax.ShapeDtypeStruct((S, C), F32), jax.ShapeDtypeStruct(w_pool.shape, F32)],
        compiler_params=_params("arbitrary"),
    )(u, u, d_pool, d_pool, w_pool)


def _bwd_w_in(du, dq, dk, dv, h1_t, n_blocks, tile):
    D, S = h1_t.shape
    C = du.shape[1]
    cs = 4 * C // n_blocks
    per = C // cs

    def body(du_ref, dq_ref, dk_ref, dv_ref, ht_ref, dproj_ref, dw_ref):
        _zero_when(pl.program_id(0) == 0, dw_ref)
        ht = ht_ref[...]
        for d in range(n_blocks):
            src = (du_ref, dq_ref, dk_ref, dv_ref)[d // per]
            dproj = src[:, (d % per) * cs:(d % per + 1) * cs].astype(BF16)
            dproj_ref[:, d * cs:(d + 1) * cs] = dproj
            dw_ref[d] += _dot(ht, dproj)

    row = lambda w: pl.BlockSpec((tile, w), lambda i: (i, 0))
    return pl.pallas_call(
        body, name="bwd_w_in", grid=(S // tile,),
        in_specs=[row(C), row(C), row(C), row(C), pl.BlockSpec((D, tile), lambda i: (0, i))],
        out_specs=[row(4 * C), pl.BlockSpec((n_blocks, D, cs), lambda i: (0, 0, 0))],
        out_shape=[jax.ShapeDtypeStruct((S, 4 * C), BF16), jax.ShapeDtypeStruct((n_blocks, D, cs), F32)],
        compiler_params=_params("arbitrary"),
    )(du, dq, dk, dv, h1_t)


def _bwd_x(dproj, w_in_full, x, dx2, g1, tile, ex):
    S, D = x.shape
    n_tiles = S // tile

    def body(dp_ref, w_ref, x_ref, dx2_ref, g_ref, *rest):
        dx_ref, dg_ref = rest[ex.n:ex.n + 2]
        ex_refs = ex.split(rest[:ex.n] + rest[ex.n + 2:])
        first, last = _grid_ends((n_tiles,))

        @pl.when(first)
        def _():
            ex.start(*ex_refs)
            dg_ref[...] = jnp.zeros_like(dg_ref)

        dh = _dot_nt(dp_ref[...], w_ref[...])
        xf = x_ref[...]
        r1 = _rms(xf)
        n1 = xf * r1
        dg_ref[...] += _colsum(dh * n1)
        dx_ref[...] = dx2_ref[...] + _norm_bwd(dh * g_ref[...], n1, r1)

        @pl.when(last)
        def _():
            ex.wait(*ex_refs)

    row = lambda w: pl.BlockSpec((tile, w), lambda i: (i, 0))
    outs = pl.pallas_call(
        body, name="bwd_x", grid=(n_tiles,),
        in_specs=[row(w_in_full.shape[1]), _const(w_in_full.shape), row(D), row(D), _const((1, D))] + ex.specs,
        out_specs=[row(D), pl.BlockSpec((1, D), lambda i: (0, 0))] + ex.specs,
        out_shape=[jax.ShapeDtypeStruct((S, D), F32), jax.ShapeDtypeStruct((1, D), F32)] + ex.out_shape,
        scratch_shapes=ex.scratch,
        compiler_params=_params("arbitrary"),
    )(dproj, w_in_full, x, dx2, g1, *ex.arrays)
    return outs[0], outs[1], outs[2:]


def _mesh_position():
    x, y, c = lax.axis_index("x"), lax.axis_index("y"), lax.axis_index("c")
    return x, y, c, 4 * x + 2 * y + c


def _peer(x, y, c, k):
    px = 1 - x if k & 4 else x
    py = 1 - y if k & 2 else y
    pc = 1 - c if k & 1 else c
    return (px, py, pc), 4 * px + 2 * py + pc


class _Exchange:
    def __init__(self, arrays, gather):
        self.arrays, self.gather, self.n = list(arrays), gather, len(arrays)
        self.out_shape = [jax.ShapeDtypeStruct(((N_DEV,) + a.shape) if gather else a.shape, a.dtype) for a in arrays]
        self.specs = [pl.BlockSpec(memory_space=pl.ANY)] * self.n
        copies = self.n * (N_DEV - 1)
        self.scratch = [pltpu.SemaphoreType.DMA((copies,)), pltpu.SemaphoreType.DMA((copies,)),
                        pltpu.SemaphoreType.DMA((self.n,))]

    def _copies(self, ins, outs, sems):
        send_sems, recv_sems, local_sems = sems
        x, y, c, me = _mesh_position()
        local, remote = [], []
        for a in range(self.n):
            mine = ins[a] if self.gather else ins[a].at[me]
            local.append(pltpu.make_async_copy(mine, outs[a].at[me], local_sems.at[a]))
            for k in range(1, N_DEV):
                peer, peer_idx = _peer(x, y, c, k)
                src = ins[a] if self.gather else ins[a].at[peer_idx]
                sem = a * (N_DEV - 1) + k - 1
                remote.append(pltpu.make_async_remote_copy(
                    src_ref=src, dst_ref=outs[a].at[me], send_sem=send_sems.at[sem], recv_sem=recv_sems.at[sem],
                    device_id=peer, device_id_type=MESH))
        return local, remote

    def start(self, ins, outs, sems):
        local, remote = self._copies(ins, outs, sems)
        for cp in local + remote:
            cp.start()

    def wait(self, ins, outs, sems):
        local, remote = self._copies(ins, outs, sems)
        for cp in remote:
            cp.wait_send()
        for cp in remote:
            cp.wait_recv()
        for cp in local:
            cp.wait()

    def split(self, refs):
        return refs[:self.n], refs[self.n:2 * self.n], refs[2 * self.n:]


def _all_to_all(arrays, gather, name):
    ex = _Exchange(arrays, gather)

    def body(*refs):
        ins, outs, sems = ex.split(refs)
        ex.start(ins, outs, sems)
        ex.wait(ins, outs, sems)

    return pl.pallas_call(body, name=name, in_specs=ex.specs, out_specs=ex.specs, out_shape=ex.out_shape,
                          scratch_shapes=ex.scratch)(*ex.arrays)


def _reduce_adamw(parts, w, m, v, rows):
    R, C = w.shape

    def body(p_ref, w_ref, m_ref, v_ref, g_ref, d_ref, nm_ref, nv_ref):
        g = p_ref[0].astype(F32)
        for s in range(1, N_DEV):
            g = g + p_ref[s].astype(F32)
        g_ref[...] = g
        m_new = ADAM_B1 * m_ref[...] + (1.0 - ADAM_B1) * g
        v_new = ADAM_B2 * v_ref[...] + (1.0 - ADAM_B2) * (g * g)
        m_hat = m_new / (1.0 - ADAM_B1 ** ADAM_STEP)
        v_hat = v_new / (1.0 - ADAM_B2 ** ADAM_STEP)
        d_ref[...] = -ADAM_LR * (m_hat / (jnp.sqrt(v_hat) + ADAM_EPS) + ADAM_WD * w_ref[...])
        nm_ref[...] = m_new
        nv_ref[...] = v_new

    row = pl.BlockSpec((rows, C), lambda i: (i, 0))
    return pl.pallas_call(
        body, name="reduce_adamw", grid=(R // rows,),
        in_specs=[pl.BlockSpec((N_DEV, rows, C), lambda i: (0, i, 0)), row, row, row],
        out_specs=[row] * 4, out_shape=[jax.ShapeDtypeStruct((R, C), F32)] * 4,
        compiler_params=_params("parallel"),
    )(parts, w, m, v)


def _row_tile(rows, cols):
    fits = [t for t in range(8, rows + 1, 8) if rows % t == 0 and N_DEV * t * cols * 4 <= 4 * 1024 * 1024]
    return max(fits) if fits else rows


SMALL_COLS = 1024


def _pack_small(vals):
    rows = []
    for a in vals:
        flat = a.reshape(-1)
        pad = (-flat.shape[0]) % SMALL_COLS
        rows.append(jnp.pad(flat, (0, pad)).reshape(-1, SMALL_COLS))
    packed = jnp.concatenate(rows, axis=0)
    return jnp.pad(packed, ((0, (-packed.shape[0]) % 8), (0, 0)))


def _unpack_small(packed, like):
    out, r = [], 0
    for a in like:
        n = a.size
        nr = -(-n // SMALL_COLS)
        out.append(packed[r:r + nr].reshape(-1)[:n].reshape(a.shape))
        r += nr
    return out


def kernel(x, norm_mix_pre, w_in, w_pool, pool_scale, attn_scale, w_out, norm_mix_post, norm_ffn_pre, w_up, conv_w, conv_b, w_down, norm_ffn_post, loss_target, m_norm_mix_pre, m_w_in, m_w_pool, m_pool_scale, m_attn_scale, m_w_out, m_norm_mix_post, m_norm_ffn_pre, m_w_up, m_conv_w, m_conv_b, m_w_down, m_norm_ffn_post, v_norm_mix_pre, v_w_in, v_w_pool, v_pool_scale, v_attn_scale, v_w_out, v_norm_mix_post, v_norm_ffn_pre, v_w_up, v_conv_w, v_conv_b, v_w_down, v_norm_ffn_post):
    S, D = x.shape[1], x.shape[2]
    d_ff_block = w_up.shape[2]

    xs, target = x[0], loss_target[0]
    g1, g2, g3, g4 = norm_mix_pre, norm_mix_post, norm_ffn_pre, norm_ffn_post
    big = min(512, S)
    small = min(256, S)
    n_pairs = pool_scale.shape[1] // QB
    conv_b_g = conv_b.reshape(N_DEV, 1, d_ff_block)

    (w_in_g,) = _all_to_all([w_in[0].astype(BF16)], gather=True, name="gather_w_in")
    h1_t, u, qkv = _fwd_inproj(xs, g1, w_in_g, big)
    pool_out = _fwd_pool(u, w_pool[0], big)
    attn_out, (w_out_g, w_up_g, w_down_g, conv_w_g) = _fwd_attn(
        qkv, n_pairs, _Exchange([w_out[0].astype(BF16), w_up[0].astype(BF16), w_down[0].astype(BF16), conv_w[0]], gather=True))
    w_out_full = w_out_g.reshape(D, D)
    w_down4 = w_down_g.reshape(D_FF_SHARDS, d_ff_block, D)
    mix, x2, h2, h2_t = _fwd_outproj(pool_out, attn_out, pool_scale, attn_scale, w_out_full, xs, g2, g3, big)
    upre = _fwd_up(h2, w_up_g, big)
    dy, df, loss_cols, dg4 = _fwd_ffn_loss(upre, conv_w_g, conv_b_g, w_down4, x2, target, g4, small)
    loss = lax.psum(0.5 * jnp.sum(loss_cols) / D, ("x", "y", "c"))

    dgate, dval, d_wd4, d_cb, d_cw = _bwd_down(upre, conv_w_g, conv_b_g, w_down4, df, big)
    dupre, dx2, dmix, dg3, dg2 = _bwd_up_x(dgate, dval, conv_w_g, w_up_g, x2, dy, mix, g2, g3, small)
    d_wup = _bwd_weight(h2_t, dupre, min(1024, S))
    d_pool, d_attn, d_wout, d_ps, d_as = _bwd_outproj(dmix, w_out_full, pool_out, attn_out, pool_scale, attn_scale, big)
    d_wdown_g = d_wd4.reshape(N_DEV, w_down.shape[1], D)
    dq, dk, dv, ffn_parts = _bwd_attn(qkv, d_attn, n_pairs, _Exchange([d_wup, d_wdown_g, d_cw], gather=False))
    du, d_wp = _bwd_pool(u, d_pool, w_pool[0], big)
    dproj, d_win = _bwd_w_in(du, dq, dk, dv, h1_t, N_DEV, big)
    d_wout_g = d_wout.reshape(N_DEV, D // N_DEV, D)
    w_in_full = w_in_g.transpose(1, 0, 2).reshape(D, -1)
    dx, dg1, mix_parts = _bwd_x(dproj, w_in_full, xs, dx2, g1, big, _Exchange([d_win, d_wout_g], gather=False))
    big_parts = [mix_parts[0], mix_parts[1], ffn_parts[0], ffn_parts[1], ffn_parts[2]]
    r = dict(dx=dx, g1=dg1, w_pool=d_wp, pool_scale=d_ps, attn_scale=d_as, g2=dg2, g3=dg3, conv_b=d_cb, g4=dg4)

    small_names = ["norm_mix_pre", "w_pool", "pool_scale", "attn_scale", "norm_mix_post", "norm_ffn_pre", "conv_b", "norm_ffn_post"]
    small_w = dict(norm_mix_pre=norm_mix_pre, w_pool=w_pool, pool_scale=pool_scale, attn_scale=attn_scale,
                   norm_mix_post=norm_mix_post, norm_ffn_pre=norm_ffn_pre, conv_b=conv_b, norm_ffn_post=norm_ffn_post)
    small_m = dict(norm_mix_pre=m_norm_mix_pre, w_pool=m_w_pool, pool_scale=m_pool_scale, attn_scale=m_attn_scale,
                   norm_mix_post=m_norm_mix_post, norm_ffn_pre=m_norm_ffn_pre, conv_b=m_conv_b, norm_ffn_post=m_norm_ffn_post)
    small_v = dict(norm_mix_pre=v_norm_mix_pre, w_pool=v_w_pool, pool_scale=v_pool_scale, attn_scale=v_attn_scale,
                   norm_mix_post=v_norm_mix_post, norm_ffn_pre=v_norm_ffn_pre, conv_b=v_conv_b, norm_ffn_post=v_norm_ffn_post)
    small_g = dict(norm_mix_pre=r["g1"], w_pool=r["w_pool"], pool_scale=r["pool_scale"], attn_scale=r["attn_scale"],
                   norm_mix_post=r["g2"], norm_ffn_pre=r["g3"], conv_b=r["conv_b"], norm_ffn_post=r["g4"])
    like = [small_w[n] for n in small_names]
    packed_g = _pack_small([small_g[n] for n in small_names])

    (small_parts,) = _all_to_all([packed_g], gather=True, name="gather_small_grads")

    def update(parts, w, m, v):
        R, C = w.shape
        return _reduce_adamw(parts, w, m, v, _row_tile(R, C))

    res = {}
    res["w_in"] = update(big_parts[0], w_in[0], m_w_in[0], v_w_in[0])
    res["w_out"] = update(big_parts[1], w_out[0], m_w_out[0], v_w_out[0])
    res["w_up"] = update(big_parts[2], w_up[0], m_w_up[0], v_w_up[0])
    res["w_down"] = update(big_parts[3], w_down[0], m_w_down[0], v_w_down[0])
    res["conv_w"] = update(big_parts[4], conv_w[0], m_conv_w[0], v_conv_w[0])
    small_res = update(small_parts, _pack_small(like), _pack_small([small_m[n] for n in small_names]),
                       _pack_small([small_v[n] for n in small_names]))
    small_res = [_unpack_small(t, like) for t in small_res]
    for idx, n in enumerate(small_names):
        res[n] = tuple(t[idx] for t in small_res)

    order = ["norm_mix_pre", "w_in", "w_pool", "pool_scale", "attn_scale", "w_out", "norm_mix_post", "norm_ffn_pre",
             "w_up", "conv_w", "conv_b", "w_down", "norm_ffn_post"]
    shaped = {n: tuple(t.reshape(s.shape) for t in res[n])
              for n, s in dict(norm_mix_pre=norm_mix_pre, w_in=w_in, w_pool=w_pool, pool_scale=pool_scale, attn_scale=attn_scale,
                               w_out=w_out, norm_mix_post=norm_mix_post, norm_ffn_pre=norm_ffn_pre, w_up=w_up, conv_w=conv_w,
                               conv_b=conv_b, w_down=w_down, norm_ffn_post=norm_ffn_post).items()}
    outs = [loss, r["dx"].reshape(x.shape)]
    for k in range(4):
        outs += [shaped[n][k] for n in order]
    return tuple(outs)
```

```python
import functools

import jax
import jax.numpy as jnp
from jax import lax
from jax.experimental import pallas as pl
from jax.experimental.pallas import tpu as pltpu

F32 = jnp.float32
BF16 = jnp.bfloat16
HIGHEST = lax.Precision.HIGHEST

N_DEV = 8
EPS = 1e-6
POOL_WINDOWS = (2, 4, 8, 16)
POOL_GROUP = 128
HALO = 16
HEAD_DIM = 64
QB = 128
ATTN_SCALE = HEAD_DIM ** -0.5
ATTN_FWD_BLOCKS = 8
ATTN_BWD_BLOCKS = 4
EXP_UNDERFLOW = -88.0
D_FF_SHARDS = 4

ADAM_LR = 0.001
ADAM_B1 = 0.9
ADAM_B2 = 0.999
ADAM_EPS = 1e-08
ADAM_WD = 0.01
ADAM_STEP = 10

VMEM_LIMIT_V7X = 56 * 1024 * 1024
MESH = pl.DeviceIdType.MESH


def _params(*semantics):
    return pltpu.CompilerParams(dimension_semantics=semantics, vmem_limit_bytes=VMEM_LIMIT_V7X)


def _const(shape):
    zeros = (0,) * len(shape)
    return pl.BlockSpec(shape, lambda *_: zeros, pipeline_mode=pl.Buffered(1))


def _dot(a, b):
    return jnp.dot(a, b, preferred_element_type=F32)


def _dot_nt(a, b):
    return lax.dot_general(a, b, (((1,), (1,)), ((), ())), preferred_element_type=F32)


def _dot_tn(a, b):
    return lax.dot_general(a, b, (((0,), (0,)), ((), ())), preferred_element_type=F32)


def _rms(v):
    return lax.rsqrt(jnp.mean(v * v, axis=-1, keepdims=True) + EPS)


def _norm_bwd(dn_times_gain, n, r):
    return r * (dn_times_gain - n * jnp.mean(dn_times_gain * n, axis=-1, keepdims=True))


def _zero_when(first, *refs):
    @pl.when(first)
    def _():
        for ref in refs:
            ref[...] = jnp.zeros_like(ref)


def _colsum(v):
    return jnp.sum(v, axis=0, keepdims=True)


def _grid_ends(grid):
    ids = [pl.program_id(a) for a in range(len(grid))]
    first = functools.reduce(jnp.logical_and, [i == 0 for i in ids])
    last = functools.reduce(jnp.logical_and, [i == n - 1 for i, n in zip(ids, grid)])
    return first, last


def _fwd_inproj(x, g1, w_in_g, tile):
    S, D = x.shape
    nb, _, cs = w_in_g.shape
    d_pool = 2 * cs

    def body(x_ref, g_ref, w_ref, ht_ref, u_ref, qkv_ref):
        xf = x_ref[...]
        h = (xf * _rms(xf) * g_ref[...]).astype(BF16)
        ht_ref[...] = h.T
        for d in range(nb):
            o = _dot(h, w_ref[d])
            if d < 2:
                u_ref[:, d * cs:(d + 1) * cs] = o
            else:
                qkv_ref[:, (d - 2) * cs:(d - 1) * cs] = o.astype(BF16)

    return pl.pallas_call(
        body, name="fwd_inproj", grid=(S // tile,),
        in_specs=[pl.BlockSpec((tile, D), lambda i: (i, 0)), _const((1, D)), _const(w_in_g.shape)],
        out_specs=[pl.BlockSpec((D, tile), lambda i: (0, i)), pl.BlockSpec((tile, d_pool), lambda i: (i, 0)),
                   pl.BlockSpec((tile, 3 * d_pool), lambda i: (i, 0))],
        out_shape=[jax.ShapeDtypeStruct((D, S), BF16), jax.ShapeDtypeStruct((S, d_pool), F32),
                   jax.ShapeDtypeStruct((S, 3 * d_pool), BF16)],
        compiler_params=_params("parallel"),
    )(x, g1, w_in_g)


def _window_sums(ext, forward):
    n = ext.shape[0]
    sums, s, sh = {}, ext, 1
    while sh < POOL_WINDOWS[-1]:
        s = s + pltpu.roll(s, (n - sh) if forward else sh, axis=0)
        sh *= 2
        sums[sh] = s
    return sums


def _pool_counts(t0, rows):
    t1 = (lax.broadcasted_iota(jnp.int32, (rows, 1), 0) + t0 + 1).astype(F32)
    return [jnp.minimum(t1, float(w)) for w in POOL_WINDOWS]


def _pool_deviation(u, halo, t0):
    T = u.shape[0]
    sums = _window_sums(jnp.concatenate([halo, u], axis=0), forward=False)
    counts = _pool_counts(t0, T)
    parts = []
    for g, w in enumerate(POOL_WINDOWS):
        lanes = slice(g * POOL_GROUP, (g + 1) * POOL_GROUP)
        parts.append(sums[w][HALO:, lanes] / counts[g] - u[:, lanes])
    return parts


def _prev_halo_spec(tile, width):
    return pl.BlockSpec((HALO, width), lambda i: (jnp.maximum(i * (tile // HALO) - 1, 0), 0))


def _next_halo_spec(tile, width, n_tiles):
    last = n_tiles * (tile // HALO) - 1
    return pl.BlockSpec((HALO, width), lambda i: (jnp.minimum((i + 1) * (tile // HALO), last), 0))


def _fwd_pool(u, w_pool, tile):
    S, C = u.shape

    def body(u_ref, halo_ref, wp_ref, o_ref):
        i = pl.program_id(0)
        halo = jnp.where(i > 0, halo_ref[...], 0.0)
        parts = _pool_deviation(u_ref[...], halo, i * tile)
        for g, p in enumerate(parts):
            o_ref[:, g * POOL_GROUP:(g + 1) * POOL_GROUP] = _dot(p.astype(BF16), wp_ref[g].astype(BF16))

    return pl.pallas_call(
        body, name="fwd_pool", grid=(S // tile,),
        in_specs=[pl.BlockSpec((tile, C), lambda i: (i, 0)), _prev_halo_spec(tile, C), _const(w_pool.shape)],
        out_specs=pl.BlockSpec((tile, C), lambda i: (i, 0)),
        out_shape=jax.ShapeDtypeStruct((S, C), F32),
        compiler_params=_params("parallel"),
    )(u, u, w_pool)


def _low_lanes():
    return lax.broadcasted_iota(jnp.int32, (QB, 2 * HEAD_DIM), 1) < HEAD_DIM


SPLIT = 3


def _triangle(inclusive):
    row = lax.broadcasted_iota(jnp.int32, (SPLIT * QB, QB), 0) % QB
    col = lax.broadcasted_iota(jnp.int32, (SPLIT * QB, QB), 1)
    return ((row >= col) if inclusive else (row > col)).astype(BF16)


def _pieces(v):
    out, rest = [], v
    for _ in range(SPLIT - 1):
        piece = rest.astype(BF16)
        out.append(piece)
        rest = rest - piece.astype(F32)
    out.append(rest.astype(BF16))
    return jnp.concatenate(out, axis=1)


def _causal_mask(width, offset):
    row = lax.broadcasted_iota(jnp.int32, (QB, width), 0)
    col = lax.broadcasted_iota(jnp.int32, (QB, width), 1)
    return col < row + offset


def _row_sums(vals, carry):
    for b in reversed(range(vals.shape[1] // QB)):
        carry = carry + jnp.sum(vals[:, b * QB:(b + 1) * QB], axis=1, keepdims=True)
    return carry


def _suffix_sums(vals, tri, carry):
    n = vals.shape[1] // QB
    out, run = [None] * n, carry
    for b in reversed(range(n)):
        blk = vals[:, b * QB:(b + 1) * QB]
        out[b] = _dot(_pieces(blk), tri) + run
        run = run + jnp.sum(blk, axis=1, keepdims=True)
    return (out[0] if n == 1 else jnp.concatenate(out, axis=1)), run


def _attn_tiles(qhs, kws, masks, carries, after_s):
    return _attn_weights(_attn_scores(qhs, kws, masks), masks, carries, after_s)


def _attn_scores(qhs, kws, masks):
    zs = [_dot_nt(qh, kw) * ATTN_SCALE for qh, kw in zip(qhs, kws)]
    es = [jnp.exp(-jnp.abs(z)) for z in zs]
    softplus = [jnp.maximum(z, 0.0) + jnp.log(1.0 + e) for z, e in zip(zs, es)]
    log_1m_beta = [-sp if m is None else jnp.where(m, -sp, 0.0) for sp, m in zip(softplus, masks)]
    return list(zip(zs, es, softplus, log_1m_beta))


def _attn_weights(scores, masks, carries, after_s):
    sums = [_suffix_sums(l, after_s, c) for (_, _, _, l), c in zip(scores, carries)]
    weights = [jnp.exp(z - sp + st) for (z, _, sp, _), (st, _) in zip(scores, sums)]
    weights = [a if m is None else jnp.where(m, a, 0.0) for a, m in zip(weights, masks)]
    return [(z, e, a, c) for (z, e, _, _), a, (_, c) in zip(scores, weights, sums)]


def _attn_tile(qh, kw, mask, carry, after_s):
    return _attn_tiles([qh], [kw], [mask], [carry], after_s)[0]


def _split_heads(v, low_lanes):
    return jnp.where(low_lanes, v, 0.0).astype(BF16), jnp.where(low_lanes, 0.0, v).astype(BF16)


def _sweep_done(c0, c1):
    return (jnp.maximum(jnp.max(c0), jnp.max(c1)) < EXP_UNDERFLOW).astype(jnp.int32)


def _first_window(i):
    first_blk = jnp.maximum(i - 1, 0)
    return first_blk, pl.multiple_of(first_blk * QB, QB), (i - first_blk) * QB


def _fwd_attn(qkv, n_pairs, ex, subs):
    S = qkv.shape[0]
    n_steps = S // (subs * QB)

    def body(q_ref, k_ref, v_ref, *rest):
        o_ref = rest[ex.n]
        ex_refs = ex.split(rest[:ex.n] + rest[ex.n + 1:])
        first_step, last_step = _grid_ends((n_pairs, n_steps))

        @pl.when(first_step)
        def _():
            ex.start(*ex_refs)

        low_lanes = _low_lanes()
        after_s = _triangle(False)
        zero = jnp.zeros((QB, 1), F32)

        def cond(c):
            return jnp.logical_and(c[0] >= 0, c[1] == 0)

        qhs, kws, vws, masks, first_blks = [], [], [], [], []
        for sub in range(subs):
            i = pl.program_id(1) * subs + sub
            first_blk, start, offset = _first_window(i)
            first_blks.append(first_blk)
            qhs += _split_heads(q_ref[sub * QB:(sub + 1) * QB, :].astype(F32), low_lanes)
            kws += [k_ref[pl.ds(start, 2 * QB), :]] * 2
            vws += [v_ref[pl.ds(start, 2 * QB), :]] * 2
            masks += [_causal_mask(2 * QB, offset)] * 2
        tiles = _attn_tiles(qhs, kws, masks, [zero] * len(qhs), after_s)
        outs = [_dot(t[2].astype(BF16), vw) for t, vw in zip(tiles, vws)]

        for sub in range(subs):
            def step(c, qh=qhs[2 * sub:2 * sub + 2]):
                j, _, acc, c0, c1 = c
                at = pl.multiple_of(j * QB, QB)
                kb = k_ref[pl.ds(at, QB), :]
                vb = v_ref[pl.ds(at, QB), :]
                far = _attn_tiles(qh, [kb, kb], [None, None], [c0, c1], after_s)
                acc = acc + jnp.where(low_lanes, _dot(far[0][2].astype(BF16), vb), _dot(far[1][2].astype(BF16), vb))
                return j - 1, _sweep_done(far[0][3], far[1][3]), acc, far[0][3], far[1][3]

            c0, c1 = tiles[2 * sub][3], tiles[2 * sub + 1][3]
            init = (first_blks[sub] - 1, _sweep_done(c0, c1), jnp.where(low_lanes, outs[2 * sub], outs[2 * sub + 1]), c0, c1)
            o_ref[sub * QB:(sub + 1) * QB, :] = lax.while_loop(cond, step, init)[2]

        @pl.when(last_step)
        def _():
            ex.wait(*ex_refs)

    outs = pl.pallas_call(
        body, name="fwd_attn", grid=(n_pairs, n_steps),
        in_specs=[pl.BlockSpec((subs * QB, QB), lambda p, i: (i, p)),
                  pl.BlockSpec((S, QB), lambda p, i: (0, n_pairs + p), pipeline_mode=pl.Buffered(1)),
                  pl.BlockSpec((S, QB), lambda p, i: (0, 2 * n_pairs + p), pipeline_mode=pl.Buffered(1))] + ex.specs,
        out_specs=[pl.BlockSpec((subs * QB, QB), lambda p, i: (i, p))] + ex.specs,
        out_shape=[jax.ShapeDtypeStruct((S, n_pairs * QB), F32)] + ex.out_shape,
        scratch_shapes=ex.scratch,
        compiler_params=_params("arbitrary", "arbitrary"),
    )(qkv, qkv, qkv, *ex.arrays)
    return outs[0], outs[1:]


def _normalized_heads(pool_out, attn_out):
    rp, ra = _rms(pool_out), _rms(attn_out)
    return pool_out * rp, rp, attn_out * ra, ra


def _fwd_outproj(pool_out, attn_out, pool_scale, attn_scale, w_out, x, g2, g3, tile):
    S, D = x.shape
    C = pool_out.shape[1]

    def body(p_ref, a_ref, ps_ref, as_ref, w_ref, x_ref, g2_ref, g3_ref, mix_ref, x2_ref, h2_ref, h2t_ref):
        n_p, _, n_a, _ = _normalized_heads(p_ref[...], a_ref[...])
        mix = _dot((n_p * ps_ref[...]).astype(BF16), w_ref[:C, :]) + _dot((n_a * as_ref[...]).astype(BF16), w_ref[C:, :])
        mix_ref[...] = mix
        x2 = x_ref[...] + mix * _rms(mix) * g2_ref[...]
        x2_ref[...] = x2
        h2 = (x2 * _rms(x2) * g3_ref[...]).astype(BF16)
        h2_ref[...] = h2
        h2t_ref[...] = h2.T

    row = lambda w: pl.BlockSpec((tile, w), lambda i: (i, 0))
    return pl.pallas_call(
        body, name="fwd_outproj", grid=(S // tile,),
        in_specs=[row(C), row(C), _const((1, C)), _const((1, C)), _const(w_out.shape), row(D), _const((1, D)), _const((1, D))],
        out_specs=[row(D), row(D), row(D), pl.BlockSpec((D, tile), lambda i: (0, i))],
        out_shape=[jax.ShapeDtypeStruct((S, D), F32), jax.ShapeDtypeStruct((S, D), F32), jax.ShapeDtypeStruct((S, D), BF16),
                   jax.ShapeDtypeStruct((D, S), BF16)],
        compiler_params=_params("parallel"),
    )(pool_out, attn_out, pool_scale, attn_scale, w_out, x, g2, g3)


def _fwd_up(h2, w_up_g, tile):
    S, D = h2.shape
    nb, _, cs = w_up_g.shape

    def body(h_ref, w_ref, o_ref):
        h = h_ref[...]
        for d in range(nb):
            o_ref[d] = _dot(h, w_ref[d]).astype(BF16)

    return pl.pallas_call(
        body, name="fwd_up", grid=(S // tile,),
        in_specs=[pl.BlockSpec((tile, D), lambda i: (i, 0)), _const(w_up_g.shape)],
        out_specs=pl.BlockSpec((nb, tile, cs), lambda i: (0, i, 0)),
        out_shape=jax.ShapeDtypeStruct((nb, S, cs), BF16),
        compiler_params=_params("parallel"),
    )(h2, w_up_g)


def _conv_taps(tile_rows, halo_rows):
    T = tile_rows.shape[0]
    ext = jnp.concatenate([halo_rows.astype(F32), tile_rows.astype(F32)], axis=0)
    return pltpu.roll(ext, 2, axis=0)[HALO:], pltpu.roll(ext, 1, axis=0)[HALO:], ext[HALO:]


def _tap_rows(cw_ref, d):
    return [cw_ref[d, k:k + 1, :] for k in range(3)]


def _gated_unit(taps_gate, taps_val, cw_gate, cw_val, cb_gate, cb_val):
    gate = cw_gate[0] * taps_gate[0] + cw_gate[1] * taps_gate[1] + cw_gate[2] * taps_gate[2] + cb_gate
    val = cw_val[0] * taps_val[0] + cw_val[1] * taps_val[1] + cw_val[2] * taps_val[2] + cb_val
    sig = 1.0 / (1.0 + jnp.exp(-gate))
    return gate, val, sig


def _fwd_ffn_loss(upre, conv_w_g, conv_b_g, w_down4, x2, target, g4, tile):
    nb, S, cs = upre.shape
    D = x2.shape[1]

    def body(u_ref, halo_ref, cw_ref, cb_ref, wd_ref, x2_ref, t_ref, g4_ref, dy_ref, df_ref, loss_ref, dg4_ref):
        i = pl.program_id(0)
        first = i == 0
        _zero_when(first, loss_ref, dg4_ref)
        f = jnp.zeros((tile, D), F32)
        for s in range(D_FF_SHARDS):
            halo_g = jnp.where(first, jnp.zeros_like(halo_ref[s]), halo_ref[s])
            halo_v = jnp.where(first, jnp.zeros_like(halo_ref[s]), halo_ref[s + D_FF_SHARDS])
            gate, val, sig = _gated_unit(_conv_taps(u_ref[s], halo_g), _conv_taps(u_ref[s + D_FF_SHARDS], halo_v),
                                         _tap_rows(cw_ref, s), _tap_rows(cw_ref, s + D_FF_SHARDS), cb_ref[s], cb_ref[s + D_FF_SHARDS])
            f = f + _dot((gate * sig * val).astype(BF16), wd_ref[s])
        r4 = _rms(f)
        n4 = f * r4
        err = x2_ref[...] + n4 * g4_ref[...] - t_ref[...]
        dy = err * (1.0 / D)
        dy_ref[...] = dy
        df_ref[...] = _norm_bwd(dy * g4_ref[...], n4, r4).astype(BF16)
        loss_ref[...] += _colsum(err * err)
        dg4_ref[...] += _colsum(dy * n4)

    row = lambda w: pl.BlockSpec((tile, w), lambda i: (i, 0))
    return pl.pallas_call(
        body, name="fwd_ffn_loss", grid=(S // tile,),
        in_specs=[pl.BlockSpec((nb, tile, cs), lambda i: (0, i, 0)),
                  pl.BlockSpec((nb, HALO, cs), lambda i: (0, jnp.maximum(i * (tile // HALO) - 1, 0), 0)),
                  _const(conv_w_g.shape), _const(conv_b_g.shape), _const(w_down4.shape), row(D), row(D), _const((1, D))],
        out_specs=[row(D), row(D), pl.BlockSpec((1, D), lambda i: (0, 0)), pl.BlockSpec((1, D), lambda i: (0, 0))],
        out_shape=[jax.ShapeDtypeStruct((S, D), F32), jax.ShapeDtypeStruct((S, D), BF16),
                   jax.ShapeDtypeStruct((1, D), F32), jax.ShapeDtypeStruct((1, D), F32)],
        compiler_params=_params("arbitrary"),
    )(upre, upre, conv_w_g, conv_b_g, w_down4, x2, target, g4)


def _bwd_down(upre, conv_w_g, conv_b_g, w_down4, df, tile):
    nb, S, cs = upre.shape
    D = df.shape[1]
    n_tiles = S // tile

    def body(ug_ref, uv_ref, hg_ref, hv_ref, cwg_ref, cwv_ref, cbg_ref, cbv_ref, wd_ref, df_ref,
             dg_ref, dv_ref, dwd_ref, dbg_ref, dbv_ref, dcwg_ref, dcwv_ref):
        i = pl.program_id(1)
        first = i == 0
        _zero_when(first, dwd_ref, dbg_ref, dbv_ref, dcwg_ref, dcwv_ref)
        halo_g = jnp.where(first, jnp.zeros_like(hg_ref[0]), hg_ref[0])
        halo_v = jnp.where(first, jnp.zeros_like(hv_ref[0]), hv_ref[0])
        taps_g, taps_v = _conv_taps(ug_ref[0], halo_g), _conv_taps(uv_ref[0], halo_v)
        gate, val, sig = _gated_unit(taps_g, taps_v, _tap_rows(cwg_ref, 0), _tap_rows(cwv_ref, 0), cbg_ref[0], cbv_ref[0])
        silu = gate * sig
        dfb = df_ref[...]
        dact = _dot_nt(dfb, wd_ref[0])
        dwd_ref[0] += _dot_tn((silu * val).astype(BF16), dfb)
        dgate = dact * val * (sig * (1.0 + gate * (1.0 - sig)))
        dval = dact * silu
        dg_ref[0] = dgate.astype(BF16)
        dv_ref[0] = dval.astype(BF16)
        dbg_ref[0] += _colsum(dgate)
        dbv_ref[0] += _colsum(dval)
        for k in range(3):
            dcwg_ref[0, k:k + 1, :] += _colsum(dgate * taps_g[k])
            dcwv_ref[0, k:k + 1, :] += _colsum(dval * taps_v[k])

    half = D_FF_SHARDS
    blk = lambda off: pl.BlockSpec((1, tile, cs), lambda s, i: (s + off, i, 0))
    halo = lambda off: pl.BlockSpec((1, HALO, cs), lambda s, i: (s + off, jnp.maximum(i * (tile // HALO) - 1, 0), 0))
    par = lambda off, r: pl.BlockSpec((1, r, cs), lambda s, i: (s + off, 0, 0))
    outs = pl.pallas_call(
        body, name="bwd_down", grid=(half, n_tiles),
        in_specs=[blk(0), blk(half), halo(0), halo(half), par(0, 3), par(half, 3), par(0, 1), par(half, 1),
                  pl.BlockSpec((1, cs, D), lambda s, i: (s, 0, 0)), pl.BlockSpec((tile, D), lambda s, i: (i, 0))],
        out_specs=[blk(0), blk(0), pl.BlockSpec((1, cs, D), lambda s, i: (s, 0, 0)),
                   par(0, 1), par(0, 1), par(0, 3), par(0, 3)],
        out_shape=[jax.ShapeDtypeStruct((half, S, cs), BF16), jax.ShapeDtypeStruct((half, S, cs), BF16),
                   jax.ShapeDtypeStruct((half, cs, D), F32),
                   jax.ShapeDtypeStruct((half, 1, cs), F32), jax.ShapeDtypeStruct((half, 1, cs), F32),
                   jax.ShapeDtypeStruct((half, 3, cs), F32), jax.ShapeDtypeStruct((half, 3, cs), F32)],
        compiler_params=_params("parallel", "arbitrary"),
    )(upre, upre, upre, upre, conv_w_g, conv_w_g, conv_b_g, conv_b_g, w_down4, df)
    dgate, dval, d_wd, dbg, dbv, dcwg, dcwv = outs
    return dgate, dval, d_wd, jnp.concatenate([dbg, dbv], axis=0), jnp.concatenate([dcwg, dcwv], axis=0)


def _bwd_up_x(dgate, dval, conv_w_g, w_up_g, x2, dy, mix, g2, g3, tile):
    half, S, cs = dgate.shape
    nb = 2 * half
    D = x2.shape[1]
    n_tiles = S // tile

    def body(dg_ref, dv_ref, hg_ref, hv_ref, cw_ref, w_ref, x2_ref, dy_ref, mix_ref, g2_ref, g3_ref,
             dupre_ref, dx2_ref, dmix_ref, dg3_ref, dg2_ref):
        i = pl.program_id(0)
        last = i == n_tiles - 1
        _zero_when(i == 0, dg3_ref, dg2_ref)
        dh2 = jnp.zeros((tile, D), F32)
        for d in range(nb):
            src, halo = (dg_ref, hg_ref) if d < half else (dv_ref, hv_ref)
            nxt = jnp.where(last, jnp.zeros_like(halo[d % half]), halo[d % half])
            ext = jnp.concatenate([src[d % half].astype(F32), nxt.astype(F32)], axis=0)
            n = ext.shape[0]
            cw = _tap_rows(cw_ref, d)
            dupre = (cw[2] * ext + cw[1] * pltpu.roll(ext, n - 1, axis=0) + cw[0] * pltpu.roll(ext, n - 2, axis=0))[:tile]
            dupre = dupre.astype(BF16)
            dupre_ref[d] = dupre
            dh2 = dh2 + _dot_nt(dupre, w_ref[d])
        x2 = x2_ref[...]
        r3 = _rms(x2)
        n3 = x2 * r3
        dg3_ref[...] += _colsum(dh2 * n3)
        dx2 = dy_ref[...] + _norm_bwd(dh2 * g3_ref[...], n3, r3)
        dx2_ref[...] = dx2
        mix = mix_ref[...]
        r2 = _rms(mix)
        n2 = mix * r2
        dg2_ref[...] += _colsum(dx2 * n2)
        dmix_ref[...] = _norm_bwd(dx2 * g2_ref[...], n2, r2).astype(BF16)

    row = lambda w: pl.BlockSpec((tile, w), lambda i: (i, 0))
    blk = pl.BlockSpec((half, tile, cs), lambda i: (0, i, 0))
    last_halo = n_tiles * (tile // HALO) - 1
    halo = pl.BlockSpec((half, HALO, cs), lambda i: (0, jnp.minimum((i + 1) * (tile // HALO), last_halo), 0))
    acc = pl.BlockSpec((1, D), lambda i: (0, 0))
    return pl.pallas_call(
        body, name="bwd_up_x", grid=(n_tiles,),
        in_specs=[blk, blk, halo, halo, _const(conv_w_g.shape), _const(w_up_g.shape), row(D), row(D), row(D),
                  _const((1, D)), _const((1, D))],
        out_specs=[pl.BlockSpec((nb, tile, cs), lambda i: (0, i, 0)), row(D), row(D), acc, acc],
        out_shape=[jax.ShapeDtypeStruct((nb, S, cs), BF16), jax.ShapeDtypeStruct((S, D), F32),
                   jax.ShapeDtypeStruct((S, D), BF16), jax.ShapeDtypeStruct((1, D), F32), jax.ShapeDtypeStruct((1, D), F32)],
        compiler_params=_params("arbitrary"),
    )(dgate, dval, dgate, dval, conv_w_g, w_up_g, x2, dy, mix, g2, g3)


def _bwd_weight(act_t, dout, tile):
    D, S = act_t.shape
    nb, _, cs = dout.shape

    def body(a_ref, d_ref, o_ref):
        _zero_when(pl.program_id(1) == 0, o_ref)
        o_ref[0] += _dot(a_ref[...], d_ref[0])

    return pl.pallas_call(
        body, name="bwd_w_up", grid=(nb, S // tile),
        in_specs=[pl.BlockSpec((D, tile), lambda d, i: (0, i)), pl.BlockSpec((1, tile, cs), lambda d, i: (d, i, 0))],
        out_specs=pl.BlockSpec((1, D, cs), lambda d, i: (d, 0, 0)),
        out_shape=jax.ShapeDtypeStruct((nb, D, cs), F32),
        compiler_params=_params("parallel", "arbitrary"),
    )(act_t, dout)


def _bwd_outproj(dmix, w_out, pool_out, attn_out, pool_scale, attn_scale, tile):
    S, D = dmix.shape
    C = pool_out.shape[1]

    def body(dm_ref, w_ref, p_ref, a_ref, ps_ref, as_ref, dp_ref, da_ref, dw_ref, dps_ref, das_ref):
        _zero_when(pl.program_id(0) == 0, dw_ref, dps_ref, das_ref)
        dmx = dm_ref[...]
        dmerged = _dot_nt(dmx, w_ref[...])
        n_p, r_p, n_a, r_a = _normalized_heads(p_ref[...], a_ref[...])
        merged = jnp.concatenate([(n_p * ps_ref[...]).astype(BF16), (n_a * as_ref[...]).astype(BF16)], axis=1)
        dw_ref[...] += _dot_tn(merged, dmx)
        dm_p, dm_a = dmerged[:, :C], dmerged[:, C:]
        dps_ref[...] += _colsum(dm_p * n_p)
        das_ref[...] += _colsum(dm_a * n_a)
        dp_ref[...] = _norm_bwd(dm_p * ps_ref[...], n_p, r_p)
        da_ref[...] = _norm_bwd(dm_a * as_ref[...], n_a, r_a)

    row = lambda w: pl.BlockSpec((tile, w), lambda i: (i, 0))
    return pl.pallas_call(
        body, name="bwd_outproj", grid=(S // tile,),
        in_specs=[row(D), _const(w_out.shape), row(C), row(C), _const((1, C)), _const((1, C))],
        out_specs=[row(C), row(C), pl.BlockSpec(w_out.shape, lambda i: (0, 0)),
                   pl.BlockSpec((1, C), lambda i: (0, 0)), pl.BlockSpec((1, C), lambda i: (0, 0))],
        out_shape=[jax.ShapeDtypeStruct((S, C), F32), jax.ShapeDtypeStruct((S, C), F32),
                   jax.ShapeDtypeStruct(w_out.shape, F32), jax.ShapeDtypeStruct((1, C), F32), jax.ShapeDtypeStruct((1, C), F32)],
        compiler_params=_params("arbitrary"),
    )(dmix, w_out, pool_out, attn_out, pool_scale, attn_scale)


def _bwd_attn(qkv, d_attn, n_pairs, ex, subs):
    S = qkv.shape[0]
    n_steps = S // (subs * QB)

    def body(q_ref, k_ref, v_ref, do_ref, *rest):
        dq_ref, dk_ref, dv_ref = rest[ex.n:ex.n + 3]
        ex_refs = ex.split(rest[:ex.n] + rest[ex.n + 3:])
        first_step, last_step = _grid_ends((n_pairs, n_steps))

        @pl.when(first_step)
        def _():
            ex.start(*ex_refs)

        @pl.when(pl.program_id(1) == 0)
        def _():
            dk_ref[...] = jnp.zeros_like(dk_ref)
            dv_ref[...] = jnp.zeros_like(dv_ref)

        low_lanes = _low_lanes()
        after_s, from_s = _triangle(False), _triangle(True)
        zero = jnp.zeros((QB, 1), F32)

        def tiles(qhs, dohs, totals, kws, vws, masks, cs, gs, scores=None):
            fw = _attn_weights(scores or _attn_scores(qhs, kws, masks), masks, cs, after_s)
            gvals = [t[2] * _dot_nt(doh, vw) for t, doh, vw in zip(fw, dohs, vws)]
            sums = [_suffix_sums(g, from_s, g0) for g, g0 in zip(gvals, gs)]
            totals = [tot if m is None else tot + sm[1] for tot, m, sm in zip(totals, masks, sums)]
            dzs = []
            for (z, e, _, _), g, (nearer, _), tot, m in zip(fw, gvals, sums, totals, masks):
                inv = 1.0 / (1.0 + e)
                sig_abs, sig_neg = inv, e * inv
                pos = z >= 0.0
                dz = g * jnp.where(pos, sig_neg, sig_abs) - jnp.where(pos, sig_abs, sig_neg) * (tot - nearer)
                if m is not None:
                    dz = jnp.where(m, dz, 0.0)
                dzs.append((dz * ATTN_SCALE).astype(BF16))
            dqs = [_dot(dz, kw) for dz, kw in zip(dzs, kws)]
            dks = [_dot_tn(dz, qh) for dz, qh in zip(dzs, qhs)]
            dvs = [_dot_tn(t[2].astype(BF16), doh) for t, doh in zip(fw, dohs)]
            return [(dq, dk, dv, t[3], sm[1], tot) for dq, dk, dv, t, sm, tot in zip(dqs, dks, dvs, fw, sums, totals)]

        def cond(c):
            return jnp.logical_and(c[0] >= 0, c[1] == 0)

        qhs, dohs, kws, vws, masks, first_blks, starts = [], [], [], [], [], [], []
        for sub in range(subs):
            i = pl.program_id(1) * subs + sub
            rows = slice(sub * QB, (sub + 1) * QB)
            first_blk, start, offset = _first_window(i)
            first_blks.append(first_blk)
            starts.append(start)
            qhs += _split_heads(q_ref[rows, :].astype(F32), low_lanes)
            dohs += _split_heads(do_ref[rows, :], low_lanes)
            kws += [k_ref[pl.ds(start, 2 * QB), :]] * 2
            vws += [v_ref[pl.ds(start, 2 * QB), :]] * 2
            masks += [_causal_mask(2 * QB, offset)] * 2
        zeros = [zero] * len(qhs)

        scores = _attn_scores(qhs, kws, masks)
        c_first = [_row_sums(sc[3], zero) for sc in scores]
        beyond_first = []
        for sub in range(subs):
            pair = slice(2 * sub, 2 * sub + 2)

            def far_sums(c, qh=qhs[pair], doh=dohs[pair]):
                j, _, c0, c1, r0, r1 = c
                at = pl.multiple_of(j * QB, QB)
                kb = k_ref[pl.ds(at, QB), :]
                vb = v_ref[pl.ds(at, QB), :]
                far = _attn_tiles(qh, [kb, kb], [None, None], [c0, c1], after_s)
                r0 = r0 + jnp.sum(far[0][2] * _dot_nt(doh[0], vb), axis=1, keepdims=True)
                r1 = r1 + jnp.sum(far[1][2] * _dot_nt(doh[1], vb), axis=1, keepdims=True)
                return j - 1, _sweep_done(far[0][3], far[1][3]), far[0][3], far[1][3], r0, r1

            c0, c1 = c_first[pair]
            far = lax.while_loop(cond, far_sums, (first_blks[sub] - 1, _sweep_done(c0, c1), c0, c1, zero, zero))
            beyond_first += [far[4], far[5]]

        done = tiles(qhs, dohs, beyond_first, kws, vws, masks, zeros, zeros, scores)
        for sub in range(subs):
            dk_ref[pl.ds(starts[sub], 2 * QB), :] += done[2 * sub][1] + done[2 * sub + 1][1]
            dv_ref[pl.ds(starts[sub], 2 * QB), :] += done[2 * sub][2] + done[2 * sub + 1][2]

        for sub in range(subs):
            pair = slice(2 * sub, 2 * sub + 2)
            t0, t1 = done[pair]

            def step(c, qh=qhs[pair], doh=dohs[pair], total=[t0[5], t1[5]]):
                j, _, dq, c0, c1, s0, s1 = c
                at = pl.multiple_of(j * QB, QB)
                kb = k_ref[pl.ds(at, QB), :]
                vb = v_ref[pl.ds(at, QB), :]
                f0, f1 = tiles(qh, doh, total, [kb, kb], [vb, vb], [None, None], [c0, c1], [s0, s1])
                dk_ref[pl.ds(at, QB), :] += f0[1] + f1[1]
                dv_ref[pl.ds(at, QB), :] += f0[2] + f1[2]
                return j - 1, _sweep_done(f0[3], f1[3]), dq + jnp.where(low_lanes, f0[0], f1[0]), f0[3], f1[3], f0[4], f1[4]

            init = (first_blks[sub] - 1, _sweep_done(t0[3], t1[3]), jnp.where(low_lanes, t0[0], t1[0]), t0[3], t1[3], t0[4], t1[4])
            dq_ref[sub * QB:(sub + 1) * QB, :] = lax.while_loop(cond, step, init)[2]

        @pl.when(last_step)
        def _():
            ex.wait(*ex_refs)

    blk = pl.BlockSpec((subs * QB, QB), lambda p, i: (i, p))
    full = lambda off: pl.BlockSpec((S, QB), lambda p, i: (0, off + p), pipeline_mode=pl.Buffered(1))
    outs = pl.pallas_call(
        body, name="bwd_attn", grid=(n_pairs, n_steps),
        in_specs=[blk, full(n_pairs), full(2 * n_pairs), blk] + ex.specs,
        out_specs=[blk, pl.BlockSpec((S, QB), lambda p, i: (0, p)), pl.BlockSpec((S, QB), lambda p, i: (0, p))] + ex.specs,
        out_shape=[jax.ShapeDtypeStruct((S, n_pairs * QB), F32)] * 3 + ex.out_shape,
        scratch_shapes=ex.scratch,
        compiler_params=_params("arbitrary", "arbitrary"),
    )(qkv, qkv, qkv, d_attn, *ex.arrays)
    return outs[0], outs[1], outs[2], outs[3:]


def _bwd_pool(u, d_pool, w_pool, tile):
    S, C = u.shape
    n_tiles = S // tile
    ng = len(POOL_WINDOWS)

    def body(u_ref, uh_ref, d_ref, dh_ref, wp_ref, du_ref, dwp_ref):
        i = pl.program_id(0)
        first = i == 0
        _zero_when(first, dwp_ref)
        halo = jnp.where(first, 0.0, uh_ref[...])
        parts = _pool_deviation(u_ref[...], halo, i * tile)
        dout = d_ref[...]
        nxt = jnp.where(i == n_tiles - 1, 0.0, dh_ref[...])
        dext = jnp.concatenate([dout, nxt], axis=0).astype(BF16)
        counts = _pool_counts(i * tile, tile + HALO)
        dps, scaled = [], []
        for g in range(ng):
            lanes = slice(g * POOL_GROUP, (g + 1) * POOL_GROUP)
            dp = _dot_nt(dext[:, lanes], wp_ref[g].astype(BF16))
            dps.append(dp[:tile])
            scaled.append(dp / counts[g])
        sums = _window_sums(jnp.concatenate(scaled, axis=1), forward=True)
        for g, w in enumerate(POOL_WINDOWS):
            lanes = slice(g * POOL_GROUP, (g + 1) * POOL_GROUP)
            du_ref[:, lanes] = sums[w][:tile, lanes] - dps[g]
            dwp_ref[g] += _dot_tn(parts[g].astype(BF16), dext[:tile, lanes])

    row = pl.BlockSpec((tile, C), lambda i: (i, 0))
    return pl.pallas_call(
        body, name="bwd_pool", grid=(n_tiles,),
        in_specs=[row, _prev_halo_spec(tile, C), row, _next_halo_spec(tile, C, n_tiles), _const(w_pool.shape)],
        out_specs=[row, pl.BlockSpec(w_pool.shape, lambda i: (0, 0, 0))],
        out_shape=[jax.ShapeDtypeStruct((S, C), F32), jax.ShapeDtypeStruct(w_pool.shape, F32)],
        compiler_params=_params("arbitrary"),
    )(u, u, d_pool, d_pool, w_pool)


def _bwd_w_in(du, dq, dk, dv, h1_t, n_blocks, tile):
    D, S = h1_t.shape
    C = du.shape[1]
    cs = 4 * C // n_blocks
    per = C // cs

    def body(du_ref, dq_ref, dk_ref, dv_ref, ht_ref, dproj_ref, dw_ref):
        _zero_when(pl.program_id(0) == 0, dw_ref)
        ht = ht_ref[...]
        for d in range(n_blocks):
            src = (du_ref, dq_ref, dk_ref, dv_ref)[d // per]
            dproj = src[:, (d % per) * cs:(d % per + 1) * cs].astype(BF16)
            dproj_ref[:, d * cs:(d + 1) * cs] = dproj
            dw_ref[d] += _dot(ht, dproj)

    row = lambda w: pl.BlockSpec((tile, w), lambda i: (i, 0))
    return pl.pallas_call(
        body, name="bwd_w_in", grid=(S // tile,),
        in_specs=[row(C), row(C), row(C), row(C), pl.BlockSpec((D, tile), lambda i: (0, i))],
        out_specs=[row(4 * C), pl.BlockSpec((n_blocks, D, cs), lambda i: (0, 0, 0))],
        out_shape=[jax.ShapeDtypeStruct((S, 4 * C), BF16), jax.ShapeDtypeStruct((n_blocks, D, cs), F32)],
        compiler_params=_params("arbitrary"),
    )(du, dq, dk, dv, h1_t)


def _bwd_x(dproj, w_in_full, x, dx2, g1, tile, ex):
    S, D = x.shape
    n_tiles = S // tile

    def body(dp_ref, w_ref, x_ref, dx2_ref, g_ref, *rest):
        dx_ref, dg_ref = rest[ex.n:ex.n + 2]
        ex_refs = ex.split(rest[:ex.n] + rest[ex.n + 2:])
        first, last = _grid_ends((n_tiles,))

        @pl.when(first)
        def _():
            ex.start(*ex_refs)
            dg_ref[...] = jnp.zeros_like(dg_ref)

        dh = _dot_nt(dp_ref[...], w_ref[...])
        xf = x_ref[...]
        r1 = _rms(xf)
        n1 = xf * r1
        dg_ref[...] += _colsum(dh * n1)
        dx_ref[...] = dx2_ref[...] + _norm_bwd(dh * g_ref[...], n1, r1)

        @pl.when(last)
        def _():
            ex.wait(*ex_refs)

    row = lambda w: pl.BlockSpec((tile, w), lambda i: (i, 0))
    outs = pl.pallas_call(
        body, name="bwd_x", grid=(n_tiles,),
        in_specs=[row(w_in_full.shape[1]), _const(w_in_full.shape), row(D), row(D), _const((1, D))] + ex.specs,
        out_specs=[row(D), pl.BlockSpec((1, D), lambda i: (0, 0))] + ex.specs,
        out_shape=[jax.ShapeDtypeStruct((S, D), F32), jax.ShapeDtypeStruct((1, D), F32)] + ex.out_shape,
        scratch_shapes=ex.scratch,
        compiler_params=_params("arbitrary"),
    )(dproj, w_in_full, x, dx2, g1, *ex.arrays)
    return outs[0], outs[1], outs[2:]


def _mesh_position():
    x, y, c = lax.axis_index("x"), lax.axis_index("y"), lax.axis_index("c")
    return x, y, c, 4 * x + 2 * y + c


def _peer(x, y, c, k):
    px = 1 - x if k & 4 else x
    py = 1 - y if k & 2 else y
    pc = 1 - c if k & 1 else c
    return (px, py, pc), 4 * px + 2 * py + pc


class _Exchange:
    def __init__(self, arrays, gather):
        self.arrays, self.gather, self.n = list(arrays), gather, len(arrays)
        self.out_shape = [jax.ShapeDtypeStruct(((N_DEV,) + a.shape) if gather else a.shape, a.dtype) for a in arrays]
        self.specs = [pl.BlockSpec(memory_space=pl.ANY)] * self.n
        copies = self.n * (N_DEV - 1)
        self.scratch = [pltpu.SemaphoreType.DMA((copies,)), pltpu.SemaphoreType.DMA((copies,)),
                        pltpu.SemaphoreType.DMA((self.n,))]

    def _copies(self, ins, outs, sems):
        send_sems, recv_sems, local_sems = sems
        x, y, c, me = _mesh_position()
        local, remote = [], []
        for a in range(self.n):
            mine = ins[a] if self.gather else ins[a].at[me]
            local.append(pltpu.make_async_copy(mine, outs[a].at[me], local_sems.at[a]))
            for k in range(1, N_DEV):
                peer, peer_idx = _peer(x, y, c, k)
                src = ins[a] if self.gather else ins[a].at[peer_idx]
                sem = a * (N_DEV - 1) + k - 1
                remote.append(pltpu.make_async_remote_copy(
                    src_ref=src, dst_ref=outs[a].at[me], send_sem=send_sems.at[sem], recv_sem=recv_sems.at[sem],
                    device_id=peer, device_id_type=MESH))
        return local, remote

    def start(self, ins, outs, sems):
        local, remote = self._copies(ins, outs, sems)
        for cp in local + remote:
            cp.start()

    def wait(self, ins, outs, sems):
        local, remote = self._copies(ins, outs, sems)
        for cp in remote:
            cp.wait_send()
        for cp in remote:
            cp.wait_recv()
        for cp in local:
            cp.wait()

    def split(self, refs):
        return refs[:self.n], refs[self.n:2 * self.n], refs[2 * self.n:]


def _all_to_all(arrays, gather, name):
    ex = _Exchange(arrays, gather)

    def body(*refs):
        ins, outs, sems = ex.split(refs)
        ex.start(ins, outs, sems)
        ex.wait(ins, outs, sems)

    return pl.pallas_call(body, name=name, in_specs=ex.specs, out_specs=ex.specs, out_shape=ex.out_shape,
                          scratch_shapes=ex.scratch)(*ex.arrays)


def _reduce_adamw(parts, w, m, v, rows):
    R, C = w.shape

    def body(p_ref, w_ref, m_ref, v_ref, g_ref, d_ref, nm_ref, nv_ref):
        g = p_ref[0].astype(F32)
        for s in range(1, N_DEV):
            g = g + p_ref[s].astype(F32)
        g_ref[...] = g
        m_new = ADAM_B1 * m_ref[...] + (1.0 - ADAM_B1) * g
        v_new = ADAM_B2 * v_ref[...] + (1.0 - ADAM_B2) * (g * g)
        m_hat = m_new / (1.0 - ADAM_B1 ** ADAM_STEP)
        v_hat = v_new / (1.0 - ADAM_B2 ** ADAM_STEP)
        d_ref[...] = -ADAM_LR * (m_hat / (jnp.sqrt(v_hat) + ADAM_EPS) + ADAM_WD * w_ref[...])
        nm_ref[...] = m_new
        nv_ref[...] = v_new

    row = pl.BlockSpec((rows, C), lambda i: (i, 0))
    return pl.pallas_call(
        body, name="reduce_adamw", grid=(R // rows,),
        in_specs=[pl.BlockSpec((N_DEV, rows, C), lambda i: (0, i, 0)), row, row, row],
        out_specs=[row] * 4, out_shape=[jax.ShapeDtypeStruct((R, C), F32)] * 4,
        compiler_params=_params("parallel"),
    )(parts, w, m, v)


def _row_tile(rows, cols):
    fits = [t for t in range(8, rows + 1, 8) if rows % t == 0 and N_DEV * t * cols * 4 <= 4 * 1024 * 1024]
    return max(fits) if fits else rows


SMALL_COLS = 1024


def _pack_small(vals):
    rows = []
    for a in vals:
        flat = a.reshape(-1)
        pad = (-flat.shape[0]) % SMALL_COLS
        rows.append(jnp.pad(flat, (0, pad)).reshape(-1, SMALL_COLS))
    packed = jnp.concatenate(rows, axis=0)
    return jnp.pad(packed, ((0, (-packed.shape[0]) % 8), (0, 0)))


def _unpack_small(packed, like):
    out, r = [], 0
    for a in like:
        n = a.size
        nr = -(-n // SMALL_COLS)
        out.append(packed[r:r + nr].reshape(-1)[:n].reshape(a.shape))
        r += nr
    return out


def kernel(x, norm_mix_pre, w_in, w_pool, pool_scale, attn_scale, w_out, norm_mix_post, norm_ffn_pre, w_up, conv_w, conv_b, w_down, norm_ffn_post, loss_target, m_norm_mix_pre, m_w_in, m_w_pool, m_pool_scale, m_attn_scale, m_w_out, m_norm_mix_post, m_norm_ffn_pre, m_w_up, m_conv_w, m_conv_b, m_w_down, m_norm_ffn_post, v_norm_mix_pre, v_w_in, v_w_pool, v_pool_scale, v_attn_scale, v_w_out, v_norm_mix_post, v_norm_ffn_pre, v_w_up, v_conv_w, v_conv_b, v_w_down, v_norm_ffn_post):
    S, D = x.shape[1], x.shape[2]
    d_ff_block = w_up.shape[2]

    xs, target = x[0], loss_target[0]
    g1, g2, g3, g4 = norm_mix_pre, norm_mix_post, norm_ffn_pre, norm_ffn_post
    big = min(512, S)
    small = min(256, S)
    n_pairs = pool_scale.shape[1] // QB
    conv_b_g = conv_b.reshape(N_DEV, 1, d_ff_block)

    (w_in_g,) = _all_to_all([w_in[0].astype(BF16)], gather=True, name="gather_w_in")
    h1_t, u, qkv = _fwd_inproj(xs, g1, w_in_g, big)
    pool_out = _fwd_pool(u, w_pool[0], big)
    attn_out, (w_out_g, w_up_g, w_down_g, conv_w_g) = _fwd_attn(
        qkv, n_pairs, _Exchange([w_out[0].astype(BF16), w_up[0].astype(BF16), w_down[0].astype(BF16), conv_w[0]], gather=True),
        min(ATTN_FWD_BLOCKS, S // QB))
    w_out_full = w_out_g.reshape(D, D)
    w_down4 = w_down_g.reshape(D_FF_SHARDS, d_ff_block, D)
    mix, x2, h2, h2_t = _fwd_outproj(pool_out, attn_out, pool_scale, attn_scale, w_out_full, xs, g2, g3, big)
    upre = _fwd_up(h2, w_up_g, big)
    dy, df, loss_cols, dg4 = _fwd_ffn_loss(upre, conv_w_g, conv_b_g, w_down4, x2, target, g4, small)
    loss = lax.psum(0.5 * jnp.sum(loss_cols) / D, ("x", "y", "c"))

    dgate, dval, d_wd4, d_cb, d_cw = _bwd_down(upre, conv_w_g, conv_b_g, w_down4, df, big)
    dupre, dx2, dmix, dg3, dg2 = _bwd_up_x(dgate, dval, conv_w_g, w_up_g, x2, dy, mix, g2, g3, small)
    d_wup = _bwd_weight(h2_t, dupre, min(1024, S))
    d_pool, d_attn, d_wout, d_ps, d_as = _bwd_outproj(dmix, w_out_full, pool_out, attn_out, pool_scale, attn_scale, big)
    d_wdown_g = d_wd4.reshape(N_DEV, w_down.shape[1], D)
    d_wout_g = d_wout.reshape(N_DEV, D // N_DEV, D)
    dq, dk, dv, late_parts = _bwd_attn(qkv, d_attn, n_pairs, _Exchange([d_wout_g, d_wup, d_wdown_g, d_cw], gather=False),
                                       min(ATTN_BWD_BLOCKS, S // QB))
    du, d_wp = _bwd_pool(u, d_pool, w_pool[0], big)
    dproj, d_win = _bwd_w_in(du, dq, dk, dv, h1_t, N_DEV, big)
    w_in_full = w_in_g.transpose(1, 0, 2).reshape(D, -1)
    dx, dg1, (win_parts,) = _bwd_x(dproj, w_in_full, xs, dx2, g1, big, _Exchange([d_win], gather=False))
    big_parts = [win_parts] + list(late_parts)
    r = dict(dx=dx, g1=dg1, w_pool=d_wp, pool_scale=d_ps, attn_scale=d_as, g2=dg2, g3=dg3, conv_b=d_cb, g4=dg4)

    small_names = ["norm_mix_pre", "w_pool", "pool_scale", "attn_scale", "norm_mix_post", "norm_ffn_pre", "conv_b", "norm_ffn_post"]
    small_w = dict(norm_mix_pre=norm_mix_pre, w_pool=w_pool, pool_scale=pool_scale, attn_scale=attn_scale,
                   norm_mix_post=norm_mix_post, norm_ffn_pre=norm_ffn_pre, conv_b=conv_b, norm_ffn_post=norm_ffn_post)
    small_m = dict(norm_mix_pre=m_norm_mix_pre, w_pool=m_w_pool, pool_scale=m_pool_scale, attn_scale=m_attn_scale,
                   norm_mix_post=m_norm_mix_post, norm_ffn_pre=m_norm_ffn_pre, conv_b=m_conv_b, norm_ffn_post=m_norm_ffn_post)
    small_v = dict(norm_mix_pre=v_norm_mix_pre, w_pool=v_w_pool, pool_scale=v_pool_scale, attn_scale=v_attn_scale,
                   norm_mix_post=v_norm_mix_post, norm_ffn_pre=v_norm_ffn_pre, conv_b=v_conv_b, norm_ffn_post=v_norm_ffn_post)
    small_g = dict(norm_mix_pre=r["g1"], w_pool=r["w_pool"], pool_scale=r["pool_scale"], attn_scale=r["attn_scale"],
                   norm_mix_post=r["g2"], norm_ffn_pre=r["g3"], conv_b=r["conv_b"], norm_ffn_post=r["g4"])
    like = [small_w[n] for n in small_names]
    packed_g = _pack_small([small_g[n] for n in small_names])

    (small_parts,) = _all_to_all([packed_g], gather=True, name="gather_small_grads")

    def update(parts, w, m, v):
        R, C = w.shape
        return _reduce_adamw(parts, w, m, v, _row_tile(R, C))

    res = {}
    res["w_in"] = update(big_parts[0], w_in[0], m_w_in[0], v_w_in[0])
    res["w_out"] = update(big_parts[1], w_out[0], m_w_out[0], v_w_out[0])
    res["w_up"] = update(big_parts[2], w_up[0], m_w_up[0], v_w_up[0])
    res["w_down"] = update(big_parts[3], w_down[0], m_w_down[0], v_w_down[0])
    res["conv_w"] = update(big_parts[4], conv_w[0], m_conv_w[0], v_conv_w[0])
    small_res = update(small_parts, _pack_small(like), _pack_small([small_m[n] for n in small_names]),
                       _pack_small([small_v[n] for n in small_names]))
    small_res = [_unpack_small(t, like) for t in small_res]
    for idx, n in enumerate(small_names):
        res[n] = tuple(t[idx] for t in small_res)

    order = ["norm_mix_pre", "w_in", "w_pool", "pool_scale", "attn_scale", "w_out", "norm_mix_post", "norm_ffn_pre",
             "w_up", "conv_w", "conv_b", "w_down", "norm_ffn_post"]
    shaped = {n: tuple(t.reshape(s.shape) for t in res[n])
              for n, s in dict(norm_mix_pre=norm_mix_pre, w_in=w_in, w_pool=w_pool, pool_scale=pool_scale, attn_scale=attn_scale,
                               w_out=w_out, norm_mix_post=norm_mix_post, norm_ffn_pre=norm_ffn_pre, w_up=w_up, conv_w=conv_w,
                               conv_b=conv_b, w_down=w_down, norm_ffn_post=norm_ffn_post).items()}
    outs = [loss, r["dx"].reshape(x.shape)]
    for k in range(4):
        outs += [shaped[n][k] for n in order]
    return tuple(outs)
```

```python
import functools

import jax
import jax.numpy as jnp
from jax import lax
from jax.experimental import pallas as pl
from jax.experimental.pallas import tpu as pltpu

F32 = jnp.float32
BF16 = jnp.bfloat16
HIGHEST = lax.Precision.HIGHEST

N_DEV = 8
EPS = 1e-6
POOL_WINDOWS = (2, 4, 8, 16)
POOL_GROUP = 128
HALO = 16
HEAD_DIM = 64
QB = 128
ATTN_SCALE = HEAD_DIM ** -0.5
ATTN_FWD_BLOCKS = 8
ATTN_BWD_BLOCKS = 4
EXP_UNDERFLOW = -88.0
D_FF_SHARDS = 4

ADAM_LR = 0.001
ADAM_B1 = 0.9
ADAM_B2 = 0.999
ADAM_EPS = 1e-08
ADAM_WD = 0.01
ADAM_STEP = 10

VMEM_LIMIT_V7X = 56 * 1024 * 1024
MESH = pl.DeviceIdType.MESH


def _params(*semantics):
    return pltpu.CompilerParams(dimension_semantics=semantics, vmem_limit_bytes=VMEM_LIMIT_V7X)


def _const(shape):
    zeros = (0,) * len(shape)
    return pl.BlockSpec(shape, lambda *_: zeros, pipeline_mode=pl.Buffered(1))


def _dot(a, b):
    return jnp.dot(a, b, preferred_element_type=F32)


def _dot_nt(a, b):
    return lax.dot_general(a, b, (((1,), (1,)), ((), ())), preferred_element_type=F32)


def _dot_tn(a, b):
    return lax.dot_general(a, b, (((0,), (0,)), ((), ())), preferred_element_type=F32)


def _rms(v):
    return lax.rsqrt(jnp.mean(v * v, axis=-1, keepdims=True) + EPS)


def _norm_bwd(dn_times_gain, n, r):
    return r * (dn_times_gain - n * jnp.mean(dn_times_gain * n, axis=-1, keepdims=True))


def _zero_when(first, *refs):
    @pl.when(first)
    def _():
        for ref in refs:
            ref[...] = jnp.zeros_like(ref)


def _colsum(v):
    return jnp.sum(v, axis=0, keepdims=True)


def _grid_ends(grid):
    ids = [pl.program_id(a) for a in range(len(grid))]
    first = functools.reduce(jnp.logical_and, [i == 0 for i in ids])
    last = functools.reduce(jnp.logical_and, [i == n - 1 for i, n in zip(ids, grid)])
    return first, last


def _fwd_inproj(x, g1, w_in_g, tile):
    S, D = x.shape
    nb, _, cs = w_in_g.shape
    d_pool = 2 * cs

    def body(x_ref, g_ref, w_ref, ht_ref, u_ref, qkv_ref):
        xf = x_ref[...]
        h = (xf * _rms(xf) * g_ref[...]).astype(BF16)
        ht_ref[...] = h.T
        for d in range(nb):
            o = _dot(h, w_ref[d])
            if d < 2:
                u_ref[:, d * cs:(d + 1) * cs] = o
            else:
                qkv_ref[:, (d - 2) * cs:(d - 1) * cs] = o.astype(BF16)

    return pl.pallas_call(
        body, name="fwd_inproj", grid=(S // tile,),
        in_specs=[pl.BlockSpec((tile, D), lambda i: (i, 0)), _const((1, D)), _const(w_in_g.shape)],
        out_specs=[pl.BlockSpec((D, tile), lambda i: (0, i)), pl.BlockSpec((tile, d_pool), lambda i: (i, 0)),
                   pl.BlockSpec((tile, 3 * d_pool), lambda i: (i, 0))],
        out_shape=[jax.ShapeDtypeStruct((D, S), BF16), jax.ShapeDtypeStruct((S, d_pool), F32),
                   jax.ShapeDtypeStruct((S, 3 * d_pool), BF16)],
        compiler_params=_params("parallel"),
    )(x, g1, w_in_g)


def _window_sums(ext, forward):
    n = ext.shape[0]
    sums, s, sh = {}, ext, 1
    while sh < POOL_WINDOWS[-1]:
        s = s + pltpu.roll(s, (n - sh) if forward else sh, axis=0)
        sh *= 2
        sums[sh] = s
    return sums


def _pool_counts(t0, rows):
    t1 = (lax.broadcasted_iota(jnp.int32, (rows, 1), 0) + t0 + 1).astype(F32)
    return [jnp.minimum(t1, float(w)) for w in POOL_WINDOWS]


def _pool_deviation(u, halo, t0):
    T = u.shape[0]
    sums = _window_sums(jnp.concatenate([halo, u], axis=0), forward=False)
    counts = _pool_counts(t0, T)
    parts = []
    for g, w in enumerate(POOL_WINDOWS):
        lanes = slice(g * POOL_GROUP, (g + 1) * POOL_GROUP)
        parts.append(sums[w][HALO:, lanes] / counts[g] - u[:, lanes])
    return parts


def _prev_halo_spec(tile, width):
    return pl.BlockSpec((HALO, width), lambda i: (jnp.maximum(i * (tile // HALO) - 1, 0), 0))


def _next_halo_spec(tile, width, n_tiles):
    last = n_tiles * (tile // HALO) - 1
    return pl.BlockSpec((HALO, width), lambda i: (jnp.minimum((i + 1) * (tile // HALO), last), 0))


def _fwd_pool(u, w_pool, tile):
    S, C = u.shape

    def body(u_ref, halo_ref, wp_ref, o_ref):
        i = pl.program_id(0)
        halo = jnp.where(i > 0, halo_ref[...], 0.0)
        parts = _pool_deviation(u_ref[...], halo, i * tile)
        for g, p in enumerate(parts):
            o_ref[:, g * POOL_GROUP:(g + 1) * POOL_GROUP] = _dot(p.astype(BF16), wp_ref[g].astype(BF16))

    return pl.pallas_call(
        body, name="fwd_pool", grid=(S // tile,),
        in_specs=[pl.BlockSpec((tile, C), lambda i: (i, 0)), _prev_halo_spec(tile, C), _const(w_pool.shape)],
        out_specs=pl.BlockSpec((tile, C), lambda i: (i, 0)),
        out_shape=jax.ShapeDtypeStruct((S, C), F32),
        compiler_params=_params("parallel"),
    )(u, u, w_pool)


def _low_lanes():
    return lax.broadcasted_iota(jnp.int32, (QB, 2 * HEAD_DIM), 1) < HEAD_DIM


SPLIT = 3


def _triangle(inclusive):
    row = lax.broadcasted_iota(jnp.int32, (SPLIT * QB, QB), 0) % QB
    col = lax.broadcasted_iota(jnp.int32, (SPLIT * QB, QB), 1)
    return ((row >= col) if inclusive else (row > col)).astype(BF16)


def _pieces(v):
    out, rest = [], v
    for _ in range(SPLIT - 1):
        piece = rest.astype(BF16)
        out.append(piece)
        rest = rest - piece.astype(F32)
    out.append(rest.astype(BF16))
    return jnp.concatenate(out, axis=1)


def _causal_mask(width, offset):
    row = lax.broadcasted_iota(jnp.int32, (QB, width), 0)
    col = lax.broadcasted_iota(jnp.int32, (QB, width), 1)
    return col < row + offset


def _row_sums(vals, carry):
    for b in reversed(range(vals.shape[1] // QB)):
        carry = carry + jnp.sum(vals[:, b * QB:(b + 1) * QB], axis=1, keepdims=True)
    return carry


def _suffix_sums(vals, tri, carry):
    n = vals.shape[1] // QB
    out, run = [None] * n, carry
    for b in reversed(range(n)):
        blk = vals[:, b * QB:(b + 1) * QB]
        out[b] = _dot(_pieces(blk), tri) + run
        run = run + jnp.sum(blk, axis=1, keepdims=True)
    return (out[0] if n == 1 else jnp.concatenate(out, axis=1)), run


def _attn_tiles(qhs, kws, masks, carries, after_s):
    return _attn_weights(_attn_scores(qhs, kws, masks), masks, carries, after_s)


def _attn_scores(qhs, kws, masks):
    zs = [_dot_nt(qh, kw) * ATTN_SCALE for qh, kw in zip(qhs, kws)]
    es = [jnp.exp(-jnp.abs(z)) for z in zs]
    softplus = [jnp.maximum(z, 0.0) + jnp.log(1.0 + e) for z, e in zip(zs, es)]
    log_1m_beta = [-sp if m is None else jnp.where(m, -sp, 0.0) for sp, m in zip(softplus, masks)]
    return list(zip(zs, es, softplus, log_1m_beta))


def _attn_weights(scores, masks, carries, after_s):
    sums = [_suffix_sums(l, after_s, c) for (_, _, _, l), c in zip(scores, carries)]
    weights = [jnp.exp(z - sp + st) for (z, _, sp, _), (st, _) in zip(scores, sums)]
    weights = [a if m is None else jnp.where(m, a, 0.0) for a, m in zip(weights, masks)]
    return [(z, e, a, c) for (z, e, _, _), a, (_, c) in zip(scores, weights, sums)]


def _attn_tile(qh, kw, mask, carry, after_s):
    return _attn_tiles([qh], [kw], [mask], [carry], after_s)[0]


def _split_heads(v, low_lanes):
    return jnp.where(low_lanes, v, 0.0).astype(BF16), jnp.where(low_lanes, 0.0, v).astype(BF16)


def _sweep_done(c0, c1):
    return (jnp.maximum(jnp.max(c0), jnp.max(c1)) < EXP_UNDERFLOW).astype(jnp.int32)


def _first_window(i):
    first_blk = jnp.maximum(i - 1, 0)
    return first_blk, pl.multiple_of(first_blk * QB, QB), (i - first_blk) * QB


def _fwd_attn(qkv, n_pairs, ex, subs):
    S = qkv.shape[0]
    n_steps = S // (subs * QB)

    def body(q_ref, k_ref, v_ref, *rest):
        o_ref = rest[ex.n]
        ex_refs = ex.split(rest[:ex.n] + rest[ex.n + 1:])
        first_step, last_step = _grid_ends((n_pairs, n_steps))

        @pl.when(first_step)
        def _():
            ex.start(*ex_refs)

        low_lanes = _low_lanes()
        after_s = _triangle(False)
        zero = jnp.zeros((QB, 1), F32)

        def cond(c):
            return jnp.logical_and(c[0] >= 0, c[1] == 0)

        qhs, kws, vws, masks, first_blks = [], [], [], [], []
        for sub in range(subs):
            i = pl.program_id(1) * subs + sub
            first_blk, start, offset = _first_window(i)
            first_blks.append(first_blk)
            qhs += _split_heads(q_ref[sub * QB:(sub + 1) * QB, :].astype(F32), low_lanes)
            kws += [k_ref[pl.ds(start, 2 * QB), :]] * 2
            vws += [v_ref[pl.ds(start, 2 * QB), :]] * 2
            masks += [_causal_mask(2 * QB, offset)] * 2
        tiles = _attn_tiles(qhs, kws, masks, [zero] * len(qhs), after_s)
        outs = [_dot(t[2].astype(BF16), vw) for t, vw in zip(tiles, vws)]

        for sub in range(subs):
            def step(c, qh=qhs[2 * sub:2 * sub + 2]):
                j, _, acc, c0, c1 = c
                at = pl.multiple_of(j * QB, QB)
                kb = k_ref[pl.ds(at, QB), :]
                vb = v_ref[pl.ds(at, QB), :]
                far = _attn_tiles(qh, [kb, kb], [None, None], [c0, c1], after_s)
                acc = acc + jnp.where(low_lanes, _dot(far[0][2].astype(BF16), vb), _dot(far[1][2].astype(BF16), vb))
                return j - 1, _sweep_done(far[0][3], far[1][3]), acc, far[0][3], far[1][3]

            c0, c1 = tiles[2 * sub][3], tiles[2 * sub + 1][3]
            init = (first_blks[sub] - 1, _sweep_done(c0, c1), jnp.where(low_lanes, outs[2 * sub], outs[2 * sub + 1]), c0, c1)
            o_ref[sub * QB:(sub + 1) * QB, :] = lax.while_loop(cond, step, init)[2]

        @pl.when(last_step)
        def _():
            ex.wait(*ex_refs)

    outs = pl.pallas_call(
        body, name="fwd_attn", grid=(n_pairs, n_steps),
        in_specs=[pl.BlockSpec((subs * QB, QB), lambda p, i: (i, p)),
                  pl.BlockSpec((S, QB), lambda p, i: (0, n_pairs + p), pipeline_mode=pl.Buffered(1)),
                  pl.BlockSpec((S, QB), lambda p, i: (0, 2 * n_pairs + p), pipeline_mode=pl.Buffered(1))] + ex.specs,
        out_specs=[pl.BlockSpec((subs * QB, QB), lambda p, i: (i, p))] + ex.specs,
        out_shape=[jax.ShapeDtypeStruct((S, n_pairs * QB), F32)] + ex.out_shape,
        scratch_shapes=ex.scratch,
        compiler_params=_params("arbitrary", "arbitrary"),
    )(qkv, qkv, qkv, *ex.arrays)
    return outs[0], outs[1:]


def _normalized_heads(pool_out, attn_out):
    rp, ra = _rms(pool_out), _rms(attn_out)
    return pool_out * rp, rp, attn_out * ra, ra


def _fwd_outproj(pool_out, attn_out, pool_scale, attn_scale, w_out, x, g2, g3, tile):
    S, D = x.shape
    C = pool_out.shape[1]

    def body(p_ref, a_ref, ps_ref, as_ref, w_ref, x_ref, g2_ref, g3_ref, mix_ref, x2_ref, h2_ref, h2t_ref):
        n_p, _, n_a, _ = _normalized_heads(p_ref[...], a_ref[...])
        mix = _dot((n_p * ps_ref[...]).astype(BF16), w_ref[:C, :]) + _dot((n_a * as_ref[...]).astype(BF16), w_ref[C:, :])
        mix_ref[...] = mix
        x2 = x_ref[...] + mix * _rms(mix) * g2_ref[...]
        x2_ref[...] = x2
        h2 = (x2 * _rms(x2) * g3_ref[...]).astype(BF16)
        h2_ref[...] = h2
        h2t_ref[...] = h2.T

    row = lambda w: pl.BlockSpec((tile, w), lambda i: (i, 0))
    return pl.pallas_call(
        body, name="fwd_outproj", grid=(S // tile,),
        in_specs=[row(C), row(C), _const((1, C)), _const((1, C)), _const(w_out.shape), row(D), _const((1, D)), _const((1, D))],
        out_specs=[row(D), row(D), row(D), pl.BlockSpec((D, tile), lambda i: (0, i))],
        out_shape=[jax.ShapeDtypeStruct((S, D), F32), jax.ShapeDtypeStruct((S, D), F32), jax.ShapeDtypeStruct((S, D), BF16),
                   jax.ShapeDtypeStruct((D, S), BF16)],
        compiler_params=_params("parallel"),
    )(pool_out, attn_out, pool_scale, attn_scale, w_out, x, g2, g3)


def _conv_taps(tile_rows, halo_rows):
    T = tile_rows.shape[0]
    ext = jnp.concatenate([halo_rows.astype(F32), tile_rows.astype(F32)], axis=0)
    return pltpu.roll(ext, 2, axis=0)[HALO:], pltpu.roll(ext, 1, axis=0)[HALO:], ext[HALO:]


def _tap_rows(cw_ref, d):
    return [cw_ref[d, k:k + 1, :] for k in range(3)]


def _gated_unit(taps_gate, taps_val, cw_gate, cw_val, cb_gate, cb_val):
    gate = cw_gate[0] * taps_gate[0] + cw_gate[1] * taps_gate[1] + cw_gate[2] * taps_gate[2] + cb_gate
    val = cw_val[0] * taps_val[0] + cw_val[1] * taps_val[1] + cw_val[2] * taps_val[2] + cb_val
    sig = 1.0 / (1.0 + jnp.exp(-gate))
    return gate, val, sig


def _fwd_ffn_loss(h2, w_up_g, conv_w_g, conv_b_g, w_down4, x2, target, g4, tile):
    S, D = x2.shape
    nb, _, cs = w_up_g.shape
    half = D_FF_SHARDS

    def body(h_ref, w_ref, cw_ref, cb_ref, wd_ref, x2_ref, t_ref, g4_ref, upre_ref, dy_ref, df_ref, loss_ref, dg4_ref, halo_ref):
        _zero_when(pl.program_id(0) == 0, loss_ref, dg4_ref, halo_ref)
        h = h_ref[...]

        def up(s):
            return _dot(h, w_ref[s]).astype(BF16), _dot(h, w_ref[s + half]).astype(BF16)

        f = jnp.zeros((tile, D), F32)
        ahead = up(0)
        for s in range(half):
            ug, uv = ahead
            if s + 1 < half:
                ahead = up(s + 1)
            upre_ref[s] = ug
            upre_ref[s + half] = uv
            gate, val, sig = _gated_unit(_conv_taps(ug, halo_ref[s]), _conv_taps(uv, halo_ref[s + half]),
                                         _tap_rows(cw_ref, s), _tap_rows(cw_ref, s + half), cb_ref[s], cb_ref[s + half])
            halo_ref[s] = ug[tile - HALO:, :]
            halo_ref[s + half] = uv[tile - HALO:, :]
            f = f + _dot((gate * sig * val).astype(BF16), wd_ref[s])
        r4 = _rms(f)
        n4 = f * r4
        err = x2_ref[...] + n4 * g4_ref[...] - t_ref[...]
        dy = err * (1.0 / D)
        dy_ref[...] = dy
        df_ref[...] = _norm_bwd(dy * g4_ref[...], n4, r4).astype(BF16)
        loss_ref[...] += _colsum(err * err)
        dg4_ref[...] += _colsum(dy * n4)

    row = lambda w: pl.BlockSpec((tile, w), lambda i: (i, 0))
    return pl.pallas_call(
        body, name="fwd_ffn_loss", grid=(S // tile,),
        in_specs=[row(D), _const(w_up_g.shape), _const(conv_w_g.shape), _const(conv_b_g.shape), _const(w_down4.shape),
                  row(D), row(D), _const((1, D))],
        out_specs=[pl.BlockSpec((nb, tile, cs), lambda i: (0, i, 0)), row(D), row(D),
                   pl.BlockSpec((1, D), lambda i: (0, 0)), pl.BlockSpec((1, D), lambda i: (0, 0))],
        out_shape=[jax.ShapeDtypeStruct((nb, S, cs), BF16), jax.ShapeDtypeStruct((S, D), F32), jax.ShapeDtypeStruct((S, D), BF16),
                   jax.ShapeDtypeStruct((1, D), F32), jax.ShapeDtypeStruct((1, D), F32)],
        scratch_shapes=[pltpu.VMEM((nb, HALO, cs), BF16)],
        compiler_params=_params("arbitrary"),
    )(h2, w_up_g, conv_w_g, conv_b_g, w_down4, x2, target, g4)


def _bwd_down(upre, conv_w_g, conv_b_g, w_down4, df, tile):
    nb, S, cs = upre.shape
    D = df.shape[1]
    n_tiles = S // tile

    def body(ug_ref, uv_ref, hg_ref, hv_ref, cwg_ref, cwv_ref, cbg_ref, cbv_ref, wd_ref, df_ref,
             dg_ref, dv_ref, dwd_ref, dbg_ref, dbv_ref, dcwg_ref, dcwv_ref):
        i = pl.program_id(1)
        first = i == 0
        _zero_when(first, dwd_ref, dbg_ref, dbv_ref, dcwg_ref, dcwv_ref)
        halo_g = jnp.where(first, jnp.zeros_like(hg_ref[0]), hg_ref[0])
        halo_v = jnp.where(first, jnp.zeros_like(hv_ref[0]), hv_ref[0])
        taps_g, taps_v = _conv_taps(ug_ref[0], halo_g), _conv_taps(uv_ref[0], halo_v)
        gate, val, sig = _gated_unit(taps_g, taps_v, _tap_rows(cwg_ref, 0), _tap_rows(cwv_ref, 0), cbg_ref[0], cbv_ref[0])
        silu = gate * sig
        dfb = df_ref[...]
        dact = _dot_nt(dfb, wd_ref[0])
        dwd_ref[0] += _dot_tn((silu * val).astype(BF16), dfb)
        dgate = dact * val * (sig * (1.0 + gate * (1.0 - sig)))
        dval = dact * silu
        dg_ref[0] = dgate.astype(BF16)
        dv_ref[0] = dval.astype(BF16)
        dbg_ref[0] += _colsum(dgate)
        dbv_ref[0] += _colsum(dval)
        for k in range(3):
            dcwg_ref[0, k:k + 1, :] += _colsum(dgate * taps_g[k])
            dcwv_ref[0, k:k + 1, :] += _colsum(dval * taps_v[k])

    half = D_FF_SHARDS
    blk = lambda off: pl.BlockSpec((1, tile, cs), lambda s, i: (s + off, i, 0))
    halo = lambda off: pl.BlockSpec((1, HALO, cs), lambda s, i: (s + off, jnp.maximum(i * (tile // HALO) - 1, 0), 0))
    par = lambda off, r: pl.BlockSpec((1, r, cs), lambda s, i: (s + off, 0, 0))
    outs = pl.pallas_call(
        body, name="bwd_down", grid=(half, n_tiles),
        in_specs=[blk(0), blk(half), halo(0), halo(half), par(0, 3), par(half, 3), par(0, 1), par(half, 1),
                  pl.BlockSpec((1, cs, D), lambda s, i: (s, 0, 0)), pl.BlockSpec((tile, D), lambda s, i: (i, 0))],
        out_specs=[blk(0), blk(0), pl.BlockSpec((1, cs, D), lambda s, i: (s, 0, 0)),
                   par(0, 1), par(0, 1), par(0, 3), par(0, 3)],
        out_shape=[jax.ShapeDtypeStruct((half, S, cs), BF16), jax.ShapeDtypeStruct((half, S, cs), BF16),
                   jax.ShapeDtypeStruct((half, cs, D), F32),
                   jax.ShapeDtypeStruct((half, 1, cs), F32), jax.ShapeDtypeStruct((half, 1, cs), F32),
                   jax.ShapeDtypeStruct((half, 3, cs), F32), jax.ShapeDtypeStruct((half, 3, cs), F32)],
        compiler_params=_params("parallel", "arbitrary"),
    )(upre, upre, upre, upre, conv_w_g, conv_w_g, conv_b_g, conv_b_g, w_down4, df)
    dgate, dval, d_wd, dbg, dbv, dcwg, dcwv = outs
    return dgate, dval, d_wd, jnp.concatenate([dbg, dbv], axis=0), jnp.concatenate([dcwg, dcwv], axis=0)


def _bwd_up_x(dgate, dval, conv_w_g, w_up_g, x2, dy, mix, g2, g3, tile):
    half, S, cs = dgate.shape
    nb = 2 * half
    D = x2.shape[1]
    n_tiles = S // tile

    def body(dg_ref, dv_ref, hg_ref, hv_ref, cw_ref, w_ref, x2_ref, dy_ref, mix_ref, g2_ref, g3_ref,
             dupre_ref, dx2_ref, dmix_ref, dg3_ref, dg2_ref):
        i = pl.program_id(0)
        last = i == n_tiles - 1
        _zero_when(i == 0, dg3_ref, dg2_ref)
        dh2 = jnp.zeros((tile, D), F32)
        for d in range(nb):
            src, halo = (dg_ref, hg_ref) if d < half else (dv_ref, hv_ref)
            nxt = jnp.where(last, jnp.zeros_like(halo[d % half]), halo[d % half])
            ext = jnp.concatenate([src[d % half].astype(F32), nxt.astype(F32)], axis=0)
            n = ext.shape[0]
            cw = _tap_rows(cw_ref, d)
            dupre = (cw[2] * ext + cw[1] * pltpu.roll(ext, n - 1, axis=0) + cw[0] * pltpu.roll(ext, n - 2, axis=0))[:tile]
            dupre = dupre.astype(BF16)
            dupre_ref[d] = dupre
            dh2 = dh2 + _dot_nt(dupre, w_ref[d])
        x2 = x2_ref[...]
        r3 = _rms(x2)
        n3 = x2 * r3
        dg3_ref[...] += _colsum(dh2 * n3)
        dx2 = dy_ref[...] + _norm_bwd(dh2 * g3_ref[...], n3, r3)
        dx2_ref[...] = dx2
        mix = mix_ref[...]
        r2 = _rms(mix)
        n2 = mix * r2
        dg2_ref[...] += _colsum(dx2 * n2)
        dmix_ref[...] = _norm_bwd(dx2 * g2_ref[...], n2, r2).astype(BF16)

    row = lambda w: pl.BlockSpec((tile, w), lambda i: (i, 0))
    blk = pl.BlockSpec((half, tile, cs), lambda i: (0, i, 0))
    last_halo = n_tiles * (tile // HALO) - 1
    halo = pl.BlockSpec((half, HALO, cs), lambda i: (0, jnp.minimum((i + 1) * (tile // HALO), last_halo), 0))
    acc = pl.BlockSpec((1, D), lambda i: (0, 0))
    return pl.pallas_call(
        body, name="bwd_up_x", grid=(n_tiles,),
        in_specs=[blk, blk, halo, halo, _const(conv_w_g.shape), _const(w_up_g.shape), row(D), row(D), row(D),
                  _const((1, D)), _const((1, D))],
        out_specs=[pl.BlockSpec((nb, tile, cs), lambda i: (0, i, 0)), row(D), row(D), acc, acc],
        out_shape=[jax.ShapeDtypeStruct((nb, S, cs), BF16), jax.ShapeDtypeStruct((S, D), F32),
                   jax.ShapeDtypeStruct((S, D), BF16), jax.ShapeDtypeStruct((1, D), F32), jax.ShapeDtypeStruct((1, D), F32)],
        compiler_params=_params("arbitrary"),
    )(dgate, dval, dgate, dval, conv_w_g, w_up_g, x2, dy, mix, g2, g3)


def _bwd_weight(act_t, dout, tile):
    D, S = act_t.shape
    nb, _, cs = dout.shape

    def body(a_ref, d_ref, o_ref):
        _zero_when(pl.program_id(1) == 0, o_ref)
        o_ref[0] += _dot(a_ref[...], d_ref[0])

    return pl.pallas_call(
        body, name="bwd_w_up", grid=(nb, S // tile),
        in_specs=[pl.BlockSpec((D, tile), lambda d, i: (0, i)), pl.BlockSpec((1, tile, cs), lambda d, i: (d, i, 0))],
        out_specs=pl.BlockSpec((1, D, cs), lambda d, i: (d, 0, 0)),
        out_shape=jax.ShapeDtypeStruct((nb, D, cs), F32),
        compiler_params=_params("parallel", "arbitrary"),
    )(act_t, dout)


def _bwd_outproj(dmix, w_out, pool_out, attn_out, pool_scale, attn_scale, tile):
    S, D = dmix.shape
    C = pool_out.shape[1]

    def body(dm_ref, w_ref, p_ref, a_ref, ps_ref, as_ref, dp_ref, da_ref, dw_ref, dps_ref, das_ref):
        _zero_when(pl.program_id(0) == 0, dw_ref, dps_ref, das_ref)
        dmx = dm_ref[...]
        dmerged = _dot_nt(dmx, w_ref[...])
        n_p, r_p, n_a, r_a = _normalized_heads(p_ref[...], a_ref[...])
        merged = jnp.concatenate([(n_p * ps_ref[...]).astype(BF16), (n_a * as_ref[...]).astype(BF16)], axis=1)
        dw_ref[...] += _dot_tn(merged, dmx)
        dm_p, dm_a = dmerged[:, :C], dmerged[:, C:]
        dps_ref[...] += _colsum(dm_p * n_p)
        das_ref[...] += _colsum(dm_a * n_a)
        dp_ref[...] = _norm_bwd(dm_p * ps_ref[...], n_p, r_p)
        da_ref[...] = _norm_bwd(dm_a * as_ref[...], n_a, r_a)

    row = lambda w: pl.BlockSpec((tile, w), lambda i: (i, 0))
    return pl.pallas_call(
        body, name="bwd_outproj", grid=(S // tile,),
        in_specs=[row(D), _const(w_out.shape), row(C), row(C), _const((1, C)), _const((1, C))],
        out_specs=[row(C), row(C), pl.BlockSpec(w_out.shape, lambda i: (0, 0)),
                   pl.BlockSpec((1, C), lambda i: (0, 0)), pl.BlockSpec((1, C), lambda i: (0, 0))],
        out_shape=[jax.ShapeDtypeStruct((S, C), F32), jax.ShapeDtypeStruct((S, C), F32),
                   jax.ShapeDtypeStruct(w_out.shape, F32), jax.ShapeDtypeStruct((1, C), F32), jax.ShapeDtypeStruct((1, C), F32)],
        compiler_params=_params("arbitrary"),
    )(dmix, w_out, pool_out, attn_out, pool_scale, attn_scale)


def _bwd_attn(qkv, d_attn, n_pairs, ex, subs):
    S = qkv.shape[0]
    n_steps = S // (subs * QB)

    def body(q_ref, k_ref, v_ref, do_ref, *rest):
        dq_ref, dk_ref, dv_ref = rest[ex.n:ex.n + 3]
        ex_refs = ex.split(rest[:ex.n] + rest[ex.n + 3:])
        first_step, last_step = _grid_ends((n_pairs, n_steps))

        @pl.when(first_step)
        def _():
            ex.start(*ex_refs)

        @pl.when(pl.program_id(1) == 0)
        def _():
            dk_ref[...] = jnp.zeros_like(dk_ref)
            dv_ref[...] = jnp.zeros_like(dv_ref)

        low_lanes = _low_lanes()
        after_s, from_s = _triangle(False), _triangle(True)
        zero = jnp.zeros((QB, 1), F32)

        def tiles(qhs, dohs, totals, kws, vws, masks, cs, gs, scores=None):
            fw = _attn_weights(scores or _attn_scores(qhs, kws, masks), masks, cs, after_s)
            gvals = [t[2] * _dot_nt(doh, vw) for t, doh, vw in zip(fw, dohs, vws)]
            sums = [_suffix_sums(g, from_s, g0) for g, g0 in zip(gvals, gs)]
            totals = [tot if m is None else tot + sm[1] for tot, m, sm in zip(totals, masks, sums)]
            dzs = []
            for (z, e, _, _), g, (nearer, _), tot, m in zip(fw, gvals, sums, totals, masks):
                inv = 1.0 / (1.0 + e)
                sig_abs, sig_neg = inv, e * inv
                pos = z >= 0.0
                dz = g * jnp.where(pos, sig_neg, sig_abs) - jnp.where(pos, sig_abs, sig_neg) * (tot - nearer)
                if m is not None:
                    dz = jnp.where(m, dz, 0.0)
                dzs.append((dz * ATTN_SCALE).astype(BF16))
            dqs = [_dot(dz, kw) for dz, kw in zip(dzs, kws)]
            dks = [_dot_tn(dz, qh) for dz, qh in zip(dzs, qhs)]
            dvs = [_dot_tn(t[2].astype(BF16), doh) for t, doh in zip(fw, dohs)]
            return [(dq, dk, dv, t[3], sm[1], tot) for dq, dk, dv, t, sm, tot in zip(dqs, dks, dvs, fw, sums, totals)]

        def cond(c):
            return jnp.logical_and(c[0] >= 0, c[1] == 0)

        qhs, dohs, kws, vws, masks, first_blks, starts = [], [], [], [], [], [], []
        for sub in range(subs):
            i = pl.program_id(1) * subs + sub
            rows = slice(sub * QB, (sub + 1) * QB)
            first_blk, start, offset = _first_window(i)
            first_blks.append(first_blk)
            starts.append(start)
            qhs += _split_heads(q_ref[rows, :].astype(F32), low_lanes)
            dohs += _split_heads(do_ref[rows, :], low_lanes)
            kws += [k_ref[pl.ds(start, 2 * QB), :]] * 2
            vws += [v_ref[pl.ds(start, 2 * QB), :]] * 2
            masks += [_causal_mask(2 * QB, offset)] * 2
        zeros = [zero] * len(qhs)

        scores = _attn_scores(qhs, kws, masks)
        c_first = [_row_sums(sc[3], zero) for sc in scores]
        beyond_first = []
        for sub in range(subs):
            pair = slice(2 * sub, 2 * sub + 2)

            def far_sums(c, qh=qhs[pair], doh=dohs[pair]):
                j, _, c0, c1, r0, r1 = c
                at = pl.multiple_of(j * QB, QB)
                kb = k_ref[pl.ds(at, QB), :]
                vb = v_ref[pl.ds(at, QB), :]
                far = _attn_tiles(qh, [kb, kb], [None, None], [c0, c1], after_s)
                r0 = r0 + jnp.sum(far[0][2] * _dot_nt(doh[0], vb), axis=1, keepdims=True)
                r1 = r1 + jnp.sum(far[1][2] * _dot_nt(doh[1], vb), axis=1, keepdims=True)
                return j - 1, _sweep_done(far[0][3], far[1][3]), far[0][3], far[1][3], r0, r1

            c0, c1 = c_first[pair]
            far = lax.while_loop(cond, far_sums, (first_blks[sub] - 1, _sweep_done(c0, c1), c0, c1, zero, zero))
            beyond_first += [far[4], far[5]]

        done = tiles(qhs, dohs, beyond_first, kws, vws, masks, zeros, zeros, scores)
        for sub in range(subs):
            dk_ref[pl.ds(starts[sub], 2 * QB), :] += done[2 * sub][1] + done[2 * sub + 1][1]
            dv_ref[pl.ds(starts[sub], 2 * QB), :] += done[2 * sub][2] + done[2 * sub + 1][2]

        for sub in range(subs):
            pair = slice(2 * sub, 2 * sub + 2)
            t0, t1 = done[pair]

            def step(c, qh=qhs[pair], doh=dohs[pair], total=[t0[5], t1[5]]):
                j, _, dq, c0, c1, s0, s1 = c
                at = pl.multiple_of(j * QB, QB)
                kb = k_ref[pl.ds(at, QB), :]
                vb = v_ref[pl.ds(at, QB), :]
                f0, f1 = tiles(qh, doh, total, [kb, kb], [vb, vb], [None, None], [c0, c1], [s0, s1])
                dk_ref[pl.ds(at, QB), :] += f0[1] + f1[1]
                dv_ref[pl.ds(at, QB), :] += f0[2] + f1[2]
                return j - 1, _sweep_done(f0[3], f1[3]), dq + jnp.where(low_lanes, f0[0], f1[0]), f0[3], f1[3], f0[4], f1[4]

            init = (first_blks[sub] - 1, _sweep_done(t0[3], t1[3]), jnp.where(low_lanes, t0[0], t1[0]), t0[3], t1[3], t0[4], t1[4])
            dq_ref[sub * QB:(sub + 1) * QB, :] = lax.while_loop(cond, step, init)[2]

        @pl.when(last_step)
        def _():
            ex.wait(*ex_refs)

    blk = pl.BlockSpec((subs * QB, QB), lambda p, i: (i, p))
    full = lambda off: pl.BlockSpec((S, QB), lambda p, i: (0, off + p), pipeline_mode=pl.Buffered(1))
    outs = pl.pallas_call(
        body, name="bwd_attn", grid=(n_pairs, n_steps),
        in_specs=[blk, full(n_pairs), full(2 * n_pairs), blk] + ex.specs,
        out_specs=[blk, pl.BlockSpec((S, QB), lambda p, i: (0, p)), pl.BlockSpec((S, QB), lambda p, i: (0, p))] + ex.specs,
        out_shape=[jax.ShapeDtypeStruct((S, n_pairs * QB), F32)] * 3 + ex.out_shape,
        scratch_shapes=ex.scratch,
        compiler_params=_params("arbitrary", "arbitrary"),
    )(qkv, qkv, qkv, d_attn, *ex.arrays)
    return outs[0], outs[1], outs[2], outs[3:]


def _bwd_pool(u, d_pool, w_pool, tile):
    S, C = u.shape
    n_tiles = S // tile
    ng = len(POOL_WINDOWS)

    def body(u_ref, uh_ref, d_ref, dh_ref, wp_ref, du_ref, dwp_ref):
        i = pl.program_id(0)
        first = i == 0
        _zero_when(first, dwp_ref)
        halo = jnp.where(first, 0.0, uh_ref[...])
        parts = _pool_deviation(u_ref[...], halo, i * tile)
        dout = d_ref[...]
        nxt = jnp.where(i == n_tiles - 1, 0.0, dh_ref[...])
        dext = jnp.concatenate([dout, nxt], axis=0).astype(BF16)
        counts = _pool_counts(i * tile, tile + HALO)
        dps, scaled = [], []
        for g in range(ng):
            lanes = slice(g * POOL_GROUP, (g + 1) * POOL_GROUP)
            dp = _dot_nt(dext[:, lanes], wp_ref[g].astype(BF16))
            dps.append(dp[:tile])
            scaled.append(dp / counts[g])
        sums = _window_sums(jnp.concatenate(scaled, axis=1), forward=True)
        for g, w in enumerate(POOL_WINDOWS):
            lanes = slice(g * POOL_GROUP, (g + 1) * POOL_GROUP)
            du_ref[:, lanes] = sums[w][:tile, lanes] - dps[g]
            dwp_ref[g] += _dot_tn(parts[g].astype(BF16), dext[:tile, lanes])

    row = pl.BlockSpec((tile, C), lambda i: (i, 0))
    return pl.pallas_call(
        body, name="bwd_pool", grid=(n_tiles,),
        in_specs=[row, _prev_halo_spec(tile, C), row, _next_halo_spec(tile, C, n_tiles), _const(w_pool.shape)],
        out_specs=[row, pl.BlockSpec(w_pool.shape, lambda i: (0, 0, 0))],
        out_shape=[jax.ShapeDtypeStruct((S, C), F32), jax.ShapeDtypeStruct(w_pool.shape, F32)],
        compiler_params=_params("arbitrary"),
    )(u, u, d_pool, d_pool, w_pool)


def _bwd_w_in(du, dq, dk, dv, h1_t, n_blocks, tile):
    D, S = h1_t.shape
    C = du.shape[1]
    cs = 4 * C // n_blocks
    per = C // cs

    def body(du_ref, dq_ref, dk_ref, dv_ref, ht_ref, dproj_ref, dw_ref):
        _zero_when(pl.program_id(0) == 0, dw_ref)
        ht = ht_ref[...]
        for d in range(n_blocks):
            src = (du_ref, dq_ref, dk_ref, dv_ref)[d // per]
            dproj = src[:, (d % per) * cs:(d % per + 1) * cs].astype(BF16)
            dproj_ref[:, d * cs:(d + 1) * cs] = dproj
            dw_ref[d] += _dot(ht, dproj)

    row = lambda w: pl.BlockSpec((tile, w), lambda i: (i, 0))
    return pl.pallas_call(
        body, name="bwd_w_in", grid=(S // tile,),
        in_specs=[row(C), row(C), row(C), row(C), pl.BlockSpec((D, tile), lambda i: (0, i))],
        out_specs=[row(4 * C), pl.BlockSpec((n_blocks, D, cs), lambda i: (0, 0, 0))],
        out_shape=[jax.ShapeDtypeStruct((S, 4 * C), BF16), jax.ShapeDtypeStruct((n_blocks, D, cs), F32)],
        compiler_params=_params("arbitrary"),
    )(du, dq, dk, dv, h1_t)


def _bwd_x(dproj, w_in_full, x, dx2, g1, tile, ex):
    S, D = x.shape
    n_tiles = S // tile

    def body(dp_ref, w_ref, x_ref, dx2_ref, g_ref, *rest):
        dx_ref, dg_ref = rest[ex.n:ex.n + 2]
        ex_refs = ex.split(rest[:ex.n] + rest[ex.n + 2:])
        first, last = _grid_ends((n_tiles,))

        @pl.when(first)
        def _():
            ex.start(*ex_refs)
            dg_ref[...] = jnp.zeros_like(dg_ref)

        dh = _dot_nt(dp_ref[...], w_ref[...])
        xf = x_ref[...]
        r1 = _rms(xf)
        n1 = xf * r1
        dg_ref[...] += _colsum(dh * n1)
        dx_ref[...] = dx2_ref[...] + _norm_bwd(dh * g_ref[...], n1, r1)

        @pl.when(last)
        def _():
            ex.wait(*ex_refs)

    row = lambda w: pl.BlockSpec((tile, w), lambda i: (i, 0))
    outs = pl.pallas_call(
        body, name="bwd_x", grid=(n_tiles,),
        in_specs=[row(w_in_full.shape[1]), _const(w_in_full.shape), row(D), row(D), _const((1, D))] + ex.specs,
        out_specs=[row(D), pl.BlockSpec((1, D), lambda i: (0, 0))] + ex.specs,
        out_shape=[jax.ShapeDtypeStruct((S, D), F32), jax.ShapeDtypeStruct((1, D), F32)] + ex.out_shape,
        scratch_shapes=ex.scratch,
        compiler_params=_params("arbitrary"),
    )(dproj, w_in_full, x, dx2, g1, *ex.arrays)
    return outs[0], outs[1], outs[2:]


def _mesh_position():
    x, y, c = lax.axis_index("x"), lax.axis_index("y"), lax.axis_index("c")
    return x, y, c, 4 * x + 2 * y + c


def _peer(x, y, c, k):
    px = 1 - x if k & 4 else x
    py = 1 - y if k & 2 else y
    pc = 1 - c if k & 1 else c
    return (px, py, pc), 4 * px + 2 * py + pc


class _Exchange:
    def __init__(self, arrays, gather):
        self.arrays, self.gather, self.n = list(arrays), gather, len(arrays)
        self.out_shape = [jax.ShapeDtypeStruct(((N_DEV,) + a.shape) if gather else a.shape, a.dtype) for a in arrays]
        self.specs = [pl.BlockSpec(memory_space=pl.ANY)] * self.n
        copies = self.n * (N_DEV - 1)
        self.scratch = [pltpu.SemaphoreType.DMA((copies,)), pltpu.SemaphoreType.DMA((copies,)),
                        pltpu.SemaphoreType.DMA((self.n,))]

    def _copies(self, ins, outs, sems):
        send_sems, recv_sems, local_sems = sems
        x, y, c, me = _mesh_position()
        local, remote = [], []
        for a in range(self.n):
            mine = ins[a] if self.gather else ins[a].at[me]
            local.append(pltpu.make_async_copy(mine, outs[a].at[me], local_sems.at[a]))
            for k in range(1, N_DEV):
                peer, peer_idx = _peer(x, y, c, k)
                src = ins[a] if self.gather else ins[a].at[peer_idx]
                sem = a * (N_DEV - 1) + k - 1
                remote.append(pltpu.make_async_remote_copy(
                    src_ref=src, dst_ref=outs[a].at[me], send_sem=send_sems.at[sem], recv_sem=recv_sems.at[sem],
                    device_id=peer, device_id_type=MESH))
        return local, remote

    def start(self, ins, outs, sems):
        local, remote = self._copies(ins, outs, sems)
        for cp in local + remote:
            cp.start()

    def wait(self, ins, outs, sems):
        local, remote = self._copies(ins, outs, sems)
        for cp in remote:
            cp.wait_send()
        for cp in remote:
            cp.wait_recv()
        for cp in local:
            cp.wait()

    def split(self, refs):
        return refs[:self.n], refs[self.n:2 * self.n], refs[2 * self.n:]


def _all_to_all(arrays, gather, name):
    ex = _Exchange(arrays, gather)

    def body(*refs):
        ins, outs, sems = ex.split(refs)
        ex.start(ins, outs, sems)
        ex.wait(ins, outs, sems)

    return pl.pallas_call(body, name=name, in_specs=ex.specs, out_specs=ex.specs, out_shape=ex.out_shape,
                          scratch_shapes=ex.scratch)(*ex.arrays)


def _reduce_adamw(parts, w, m, v, rows):
    R, C = w.shape

    def body(p_ref, w_ref, m_ref, v_ref, g_ref, d_ref, nm_ref, nv_ref):
        g = p_ref[0].astype(F32)
        for s in range(1, N_DEV):
            g = g + p_ref[s].astype(F32)
        g_ref[...] = g
        m_new = ADAM_B1 * m_ref[...] + (1.0 - ADAM_B1) * g
        v_new = ADAM_B2 * v_ref[...] + (1.0 - ADAM_B2) * (g * g)
        m_hat = m_new / (1.0 - ADAM_B1 ** ADAM_STEP)
        v_hat = v_new / (1.0 - ADAM_B2 ** ADAM_STEP)
        d_ref[...] = -ADAM_LR * (m_hat / (jnp.sqrt(v_hat) + ADAM_EPS) + ADAM_WD * w_ref[...])
        nm_ref[...] = m_new
        nv_ref[...] = v_new

    row = pl.BlockSpec((rows, C), lambda i: (i, 0))
    return pl.pallas_call(
        body, name="reduce_adamw", grid=(R // rows,),
        in_specs=[pl.BlockSpec((N_DEV, rows, C), lambda i: (0, i, 0)), row, row, row],
        out_specs=[row] * 4, out_shape=[jax.ShapeDtypeStruct((R, C), F32)] * 4,
        compiler_params=_params("parallel"),
    )(parts, w, m, v)


def _row_tile(rows, cols):
    fits = [t for t in range(8, rows + 1, 8) if rows % t == 0 and N_DEV * t * cols * 4 <= 4 * 1024 * 1024]
    return max(fits) if fits else rows


SMALL_COLS = 1024


def _pack_small(vals):
    rows = []
    for a in vals:
        flat = a.reshape(-1)
        pad = (-flat.shape[0]) % SMALL_COLS
        rows.append(jnp.pad(flat, (0, pad)).reshape(-1, SMALL_COLS))
    packed = jnp.concatenate(rows, axis=0)
    return jnp.pad(packed, ((0, (-packed.shape[0]) % 8), (0, 0)))


def _unpack_small(packed, like):
    out, r = [], 0
    for a in like:
        n = a.size
        nr = -(-n // SMALL_COLS)
        out.append(packed[r:r + nr].reshape(-1)[:n].reshape(a.shape))
        r += nr
    return out


def kernel(x, norm_mix_pre, w_in, w_pool, pool_scale, attn_scale, w_out, norm_mix_post, norm_ffn_pre, w_up, conv_w, conv_b, w_down, norm_ffn_post, loss_target, m_norm_mix_pre, m_w_in, m_w_pool, m_pool_scale, m_attn_scale, m_w_out, m_norm_mix_post, m_norm_ffn_pre, m_w_up, m_conv_w, m_conv_b, m_w_down, m_norm_ffn_post, v_norm_mix_pre, v_w_in, v_w_pool, v_pool_scale, v_attn_scale, v_w_out, v_norm_mix_post, v_norm_ffn_pre, v_w_up, v_conv_w, v_conv_b, v_w_down, v_norm_ffn_post):
    S, D = x.shape[1], x.shape[2]
    d_ff_block = w_up.shape[2]

    xs, target = x[0], loss_target[0]
    g1, g2, g3, g4 = norm_mix_pre, norm_mix_post, norm_ffn_pre, norm_ffn_post
    big = min(512, S)
    small = min(256, S)
    n_pairs = pool_scale.shape[1] // QB
    conv_b_g = conv_b.reshape(N_DEV, 1, d_ff_block)

    (w_in_g,) = _all_to_all([w_in[0].astype(BF16)], gather=True, name="gather_w_in")
    h1_t, u, qkv = _fwd_inproj(xs, g1, w_in_g, big)
    pool_out = _fwd_pool(u, w_pool[0], big)
    attn_out, (w_out_g, w_up_g, w_down_g, conv_w_g) = _fwd_attn(
        qkv, n_pairs, _Exchange([w_out[0].astype(BF16), w_up[0].astype(BF16), w_down[0].astype(BF16), conv_w[0]], gather=True),
        min(ATTN_FWD_BLOCKS, S // QB))
    w_out_full = w_out_g.reshape(D, D)
    w_down4 = w_down_g.reshape(D_FF_SHARDS, d_ff_block, D)
    mix, x2, h2, h2_t = _fwd_outproj(pool_out, attn_out, pool_scale, attn_scale, w_out_full, xs, g2, g3, big)
    upre, dy, df, loss_cols, dg4 = _fwd_ffn_loss(h2, w_up_g, conv_w_g, conv_b_g, w_down4, x2, target, g4, small)
    loss = lax.psum(0.5 * jnp.sum(loss_cols) / D, ("x", "y", "c"))

    dgate, dval, d_wd4, d_cb, d_cw = _bwd_down(upre, conv_w_g, conv_b_g, w_down4, df, big)
    dupre, dx2, dmix, dg3, dg2 = _bwd_up_x(dgate, dval, conv_w_g, w_up_g, x2, dy, mix, g2, g3, small)
    d_wup = _bwd_weight(h2_t, dupre, min(1024, S))
    d_pool, d_attn, d_wout, d_ps, d_as = _bwd_outproj(dmix, w_out_full, pool_out, attn_out, pool_scale, attn_scale, big)
    d_wdown_g = d_wd4.reshape(N_DEV, w_down.shape[1], D)
    d_wout_g = d_wout.reshape(N_DEV, D // N_DEV, D)
    dq, dk, dv, late_parts = _bwd_attn(qkv, d_attn, n_pairs, _Exchange([d_wout_g, d_wup, d_wdown_g, d_cw], gather=False),
                                       min(ATTN_BWD_BLOCKS, S // QB))
    du, d_wp = _bwd_pool(u, d_pool, w_pool[0], big)
    dproj, d_win = _bwd_w_in(du, dq, dk, dv, h1_t, N_DEV, big)
    w_in_full = w_in_g.transpose(1, 0, 2).reshape(D, -1)
    dx, dg1, (win_parts,) = _bwd_x(dproj, w_in_full, xs, dx2, g1, big, _Exchange([d_win], gather=False))
    big_parts = [win_parts] + list(late_parts)
    r = dict(dx=dx, g1=dg1, w_pool=d_wp, pool_scale=d_ps, attn_scale=d_as, g2=dg2, g3=dg3, conv_b=d_cb, g4=dg4)

    small_names = ["norm_mix_pre", "w_pool", "pool_scale", "attn_scale", "norm_mix_post", "norm_ffn_pre", "conv_b", "norm_ffn_post"]
    small_w = dict(norm_mix_pre=norm_mix_pre, w_pool=w_pool, pool_scale=pool_scale, attn_scale=attn_scale,
                   norm_mix_post=norm_mix_post, norm_ffn_pre=norm_ffn_pre, conv_b=conv_b, norm_ffn_post=norm_ffn_post)
    small_m = dict(norm_mix_pre=m_norm_mix_pre, w_pool=m_w_pool, pool_scale=m_pool_scale, attn_scale=m_attn_scale,
                   norm_mix_post=m_norm_mix_post, norm_ffn_pre=m_norm_ffn_pre, conv_b=m_conv_b, norm_ffn_post=m_norm_ffn_post)
    small_v = dict(norm_mix_pre=v_norm_mix_pre, w_pool=v_w_pool, pool_scale=v_pool_scale, attn_scale=v_attn_scale,
                   norm_mix_post=v_norm_mix_post, norm_ffn_pre=v_norm_ffn_pre, conv_b=v_conv_b, norm_ffn_post=v_norm_ffn_post)
    small_g = dict(norm_mix_pre=r["g1"], w_pool=r["w_pool"], pool_scale=r["pool_scale"], attn_scale=r["attn_scale"],
                   norm_mix_post=r["g2"], norm_ffn_pre=r["g3"], conv_b=r["conv_b"], norm_ffn_post=r["g4"])
    like = [small_w[n] for n in small_names]
    packed_g = _pack_small([small_g[n] for n in small_names])

    (small_parts,) = _all_to_all([packed_g], gather=True, name="gather_small_grads")

    def update(parts, w, m, v):
        R, C = w.shape
        return _reduce_adamw(parts, w, m, v, _row_tile(R, C))

    res = {}
    res["w_in"] = update(big_parts[0], w_in[0], m_w_in[0], v_w_in[0])
    res["w_out"] = update(big_parts[1], w_out[0], m_w_out[0], v_w_out[0])
    res["w_up"] = update(big_parts[2], w_up[0], m_w_up[0], v_w_up[0])
    res["w_down"] = update(big_parts[3], w_down[0], m_w_down[0], v_w_down[0])
    res["conv_w"] = update(big_parts[4], conv_w[0], m_conv_w[0], v_conv_w[0])
    small_res = update(small_parts, _pack_small(like), _pack_small([small_m[n] for n in small_names]),
                       _pack_small([small_v[n] for n in small_names]))
    small_res = [_unpack_small(t, like) for t in small_res]
    for idx, n in enumerate(small_names):
        res[n] = tuple(t[idx] for t in small_res)

    order = ["norm_mix_pre", "w_in", "w_pool", "pool_scale", "attn_scale", "w_out", "norm_mix_post", "norm_ffn_pre",
             "w_up", "conv_w", "conv_b", "w_down", "norm_ffn_post"]
    shaped = {n: tuple(t.reshape(s.shape) for t in res[n])
              for n, s in dict(norm_mix_pre=norm_mix_pre, w_in=w_in, w_pool=w_pool, pool_scale=pool_scale, attn_scale=attn_scale,
                               w_out=w_out, norm_mix_post=norm_mix_post, norm_ffn_pre=norm_ffn_pre, w_up=w_up, conv_w=conv_w,
                               conv_b=conv_b, w_down=w_down, norm_ffn_post=norm_ffn_post).items()}
    outs = [loss, r["dx"].reshape(x.shape)]
    for k in range(4):
        outs += [shaped[n][k] for n in order]
    return tuple(outs)
```

```python
import functools

import jax
import jax.numpy as jnp
from jax import lax
from jax.experimental import pallas as pl
from jax.experimental.pallas import tpu as pltpu

F32 = jnp.float32
BF16 = jnp.bfloat16
HIGHEST = lax.Precision.HIGHEST

N_DEV = 8
EPS = 1e-6
POOL_WINDOWS = (2, 4, 8, 16)
POOL_GROUP = 128
HALO = 16
HEAD_DIM = 64
QB = 128
ATTN_SCALE = HEAD_DIM ** -0.5
ATTN_FWD_BLOCKS = 8
ATTN_BWD_BLOCKS = 4
EXP_UNDERFLOW = -88.0
D_FF_SHARDS = 4

ADAM_LR = 0.001
ADAM_B1 = 0.9
ADAM_B2 = 0.999
ADAM_EPS = 1e-08
ADAM_WD = 0.01
ADAM_STEP = 10

VMEM_LIMIT_V7X = 56 * 1024 * 1024
MESH = pl.DeviceIdType.MESH


def _params(*semantics):
    return pltpu.CompilerParams(dimension_semantics=semantics, vmem_limit_bytes=VMEM_LIMIT_V7X)


def _const(shape):
    zeros = (0,) * len(shape)
    return pl.BlockSpec(shape, lambda *_: zeros, pipeline_mode=pl.Buffered(1))


def _dot(a, b):
    return jnp.dot(a, b, preferred_element_type=F32)


def _dot_nt(a, b):
    return lax.dot_general(a, b, (((1,), (1,)), ((), ())), preferred_element_type=F32)


def _dot_tn(a, b):
    return lax.dot_general(a, b, (((0,), (0,)), ((), ())), preferred_element_type=F32)


def _rms(v):
    return lax.rsqrt(jnp.mean(v * v, axis=-1, keepdims=True) + EPS)


def _norm_bwd(dn_times_gain, n, r):
    return r * (dn_times_gain - n * jnp.mean(dn_times_gain * n, axis=-1, keepdims=True))


def _zero_when(first, *refs):
    @pl.when(first)
    def _():
        for ref in refs:
            ref[...] = jnp.zeros_like(ref)


def _colsum(v):
    return jnp.sum(v, axis=0, keepdims=True)


def _grid_ends(grid):
    ids = [pl.program_id(a) for a in range(len(grid))]
    first = functools.reduce(jnp.logical_and, [i == 0 for i in ids])
    last = functools.reduce(jnp.logical_and, [i == n - 1 for i, n in zip(ids, grid)])
    return first, last


def _fwd_inproj(x, g1, w_in_g, tile):
    S, D = x.shape
    nb, _, cs = w_in_g.shape
    d_pool = 2 * cs

    def body(x_ref, g_ref, w_ref, ht_ref, u_ref, qkv_ref):
        xf = x_ref[...]
        h = (xf * _rms(xf) * g_ref[...]).astype(BF16)
        ht_ref[...] = h.T
        for d in range(nb):
            o = _dot(h, w_ref[d])
            if d < 2:
                u_ref[:, d * cs:(d + 1) * cs] = o
            else:
                qkv_ref[:, (d - 2) * cs:(d - 1) * cs] = o.astype(BF16)

    return pl.pallas_call(
        body, name="fwd_inproj", grid=(S // tile,),
        in_specs=[pl.BlockSpec((tile, D), lambda i: (i, 0)), _const((1, D)), _const(w_in_g.shape)],
        out_specs=[pl.BlockSpec((D, tile), lambda i: (0, i)), pl.BlockSpec((tile, d_pool), lambda i: (i, 0)),
                   pl.BlockSpec((tile, 3 * d_pool), lambda i: (i, 0))],
        out_shape=[jax.ShapeDtypeStruct((D, S), BF16), jax.ShapeDtypeStruct((S, d_pool), F32),
                   jax.ShapeDtypeStruct((S, 3 * d_pool), BF16)],
        compiler_params=_params("parallel"),
    )(x, g1, w_in_g)


def _window_sums(ext, forward):
    n = ext.shape[0]
    sums, s, sh = {}, ext, 1
    while sh < POOL_WINDOWS[-1]:
        s = s + pltpu.roll(s, (n - sh) if forward else sh, axis=0)
        sh *= 2
        sums[sh] = s
    return sums


def _pool_counts(t0, rows):
    t1 = (lax.broadcasted_iota(jnp.int32, (rows, 1), 0) + t0 + 1).astype(F32)
    return [jnp.minimum(t1, float(w)) for w in POOL_WINDOWS]


def _pool_deviation(u, halo, t0):
    T = u.shape[0]
    sums = _window_sums(jnp.concatenate([halo, u], axis=0), forward=False)
    counts = _pool_counts(t0, T)
    parts = []
    for g, w in enumerate(POOL_WINDOWS):
        lanes = slice(g * POOL_GROUP, (g + 1) * POOL_GROUP)
        parts.append(sums[w][HALO:, lanes] / counts[g] - u[:, lanes])
    return parts


def _prev_halo_spec(tile, width):
    return pl.BlockSpec((HALO, width), lambda i: (jnp.maximum(i * (tile // HALO) - 1, 0), 0))


def _next_halo_spec(tile, width, n_tiles):
    last = n_tiles * (tile // HALO) - 1
    return pl.BlockSpec((HALO, width), lambda i: (jnp.minimum((i + 1) * (tile // HALO), last), 0))


def _fwd_pool(u, w_pool, tile):
    S, C = u.shape

    def body(u_ref, halo_ref, wp_ref, o_ref):
        i = pl.program_id(0)
        halo = jnp.where(i > 0, halo_ref[...], 0.0)
        parts = _pool_deviation(u_ref[...], halo, i * tile)
        for g, p in enumerate(parts):
            o_ref[:, g * POOL_GROUP:(g + 1) * POOL_GROUP] = _dot(p.astype(BF16), wp_ref[g].astype(BF16))

    return pl.pallas_call(
        body, name="fwd_pool", grid=(S // tile,),
        in_specs=[pl.BlockSpec((tile, C), lambda i: (i, 0)), _prev_halo_spec(tile, C), _const(w_pool.shape)],
        out_specs=pl.BlockSpec((tile, C), lambda i: (i, 0)),
        out_shape=jax.ShapeDtypeStruct((S, C), F32),
        compiler_params=_params("parallel"),
    )(u, u, w_pool)


def _low_lanes():
    return lax.broadcasted_iota(jnp.int32, (QB, 2 * HEAD_DIM), 1) < HEAD_DIM


SPLIT = 3


def _triangle(inclusive):
    row = lax.broadcasted_iota(jnp.int32, (SPLIT * QB, QB), 0) % QB
    col = lax.broadcasted_iota(jnp.int32, (SPLIT * QB, QB), 1)
    return ((row >= col) if inclusive else (row > col)).astype(BF16)


def _pieces(v):
    out, rest = [], v
    for _ in range(SPLIT - 1):
        piece = rest.astype(BF16)
        out.append(piece)
        rest = rest - piece.astype(F32)
    out.append(rest.astype(BF16))
    return jnp.concatenate(out, axis=1)


def _causal_mask(width, offset):
    row = lax.broadcasted_iota(jnp.int32, (QB, width), 0)
    col = lax.broadcasted_iota(jnp.int32, (QB, width), 1)
    return col < row + offset


def _row_sums(vals, carry):
    for b in reversed(range(vals.shape[1] // QB)):
        carry = carry + jnp.sum(vals[:, b * QB:(b + 1) * QB], axis=1, keepdims=True)
    return carry


def _suffix_sums(vals, tri, carry):
    n = vals.shape[1] // QB
    out, run = [None] * n, carry
    for b in reversed(range(n)):
        blk = vals[:, b * QB:(b + 1) * QB]
        out[b] = _dot(_pieces(blk), tri) + run
        run = run + jnp.sum(blk, axis=1, keepdims=True)
    return (out[0] if n == 1 else jnp.concatenate(out, axis=1)), run


def _attn_tiles(qhs, kws, masks, carries, after_s):
    return _attn_weights(_attn_scores(qhs, kws, masks), masks, carries, after_s)


def _attn_scores(qhs, kws, masks):
    zs = [_dot_nt(qh, kw) * ATTN_SCALE for qh, kw in zip(qhs, kws)]
    es = [jnp.exp(-jnp.abs(z)) for z in zs]
    softplus = [jnp.maximum(z, 0.0) + jnp.log(1.0 + e) for z, e in zip(zs, es)]
    log_1m_beta = [-sp if m is None else jnp.where(m, -sp, 0.0) for sp, m in zip(softplus, masks)]
    return list(zip(zs, es, softplus, log_1m_beta))


def _attn_weights(scores, masks, carries, after_s):
    sums = [_suffix_sums(l, after_s, c) for (_, _, _, l), c in zip(scores, carries)]
    weights = [jnp.exp(z - sp + st) for (z, _, sp, _), (st, _) in zip(scores, sums)]
    weights = [a if m is None else jnp.where(m, a, 0.0) for a, m in zip(weights, masks)]
    return [(z, e, a, c) for (z, e, _, _), a, (_, c) in zip(scores, weights, sums)]


def _attn_tile(qh, kw, mask, carry, after_s):
    return _attn_tiles([qh], [kw], [mask], [carry], after_s)[0]


def _split_heads(v, low_lanes):
    return jnp.where(low_lanes, v, 0.0).astype(BF16), jnp.where(low_lanes, 0.0, v).astype(BF16)


def _sweep_done(c0, c1):
    return (jnp.maximum(jnp.max(c0), jnp.max(c1)) < EXP_UNDERFLOW).astype(jnp.int32)


def _first_window(i):
    first_blk = jnp.maximum(i - 1, 0)
    return first_blk, pl.multiple_of(first_blk * QB, QB), (i - first_blk) * QB


def _fwd_attn(qkv, n_pairs, ex, subs):
    S = qkv.shape[0]
    n_steps = S // (subs * QB)

    def body(q_ref, k_ref, v_ref, *rest):
        o_ref = rest[ex.n]
        ex_refs = ex.split(rest[:ex.n] + rest[ex.n + 1:])
        first_step, last_step = _grid_ends((n_pairs, n_steps))

        @pl.when(first_step)
        def _():
            ex.start(*ex_refs)

        low_lanes = _low_lanes()
        after_s = _triangle(False)
        zero = jnp.zeros((QB, 1), F32)

        def cond(c):
            return jnp.logical_and(c[0] >= 0, c[1] == 0)

        qhs, kws, vws, masks, first_blks = [], [], [], [], []
        for sub in range(subs):
            i = pl.program_id(1) * subs + sub
            first_blk, start, offset = _first_window(i)
            first_blks.append(first_blk)
            qhs += _split_heads(q_ref[sub * QB:(sub + 1) * QB, :].astype(F32), low_lanes)
            kws += [k_ref[pl.ds(start, 2 * QB), :]] * 2
            vws += [v_ref[pl.ds(start, 2 * QB), :]] * 2
            masks += [_causal_mask(2 * QB, offset)] * 2
        tiles = _attn_tiles(qhs, kws, masks, [zero] * len(qhs), after_s)
        outs = [_dot(t[2].astype(BF16), vw) for t, vw in zip(tiles, vws)]

        for sub in range(subs):
            def step(c, qh=qhs[2 * sub:2 * sub + 2]):
                j, _, acc, c0, c1 = c
                at = pl.multiple_of(j * QB, QB)
                kb = k_ref[pl.ds(at, QB), :]
                vb = v_ref[pl.ds(at, QB), :]
                far = _attn_tiles(qh, [kb, kb], [None, None], [c0, c1], after_s)
                acc = acc + jnp.where(low_lanes, _dot(far[0][2].astype(BF16), vb), _dot(far[1][2].astype(BF16), vb))
                return j - 1, _sweep_done(far[0][3], far[1][3]), acc, far[0][3], far[1][3]

            c0, c1 = tiles[2 * sub][3], tiles[2 * sub + 1][3]
            init = (first_blks[sub] - 1, _sweep_done(c0, c1), jnp.where(low_lanes, outs[2 * sub], outs[2 * sub + 1]), c0, c1)
            o_ref[sub * QB:(sub + 1) * QB, :] = lax.while_loop(cond, step, init)[2]

        @pl.when(last_step)
        def _():
            ex.wait(*ex_refs)

    outs = pl.pallas_call(
        body, name="fwd_attn", grid=(n_pairs, n_steps),
        in_specs=[pl.BlockSpec((subs * QB, QB), lambda p, i: (i, p)),
                  pl.BlockSpec((S, QB), lambda p, i: (0, n_pairs + p), pipeline_mode=pl.Buffered(1)),
                  pl.BlockSpec((S, QB), lambda p, i: (0, 2 * n_pairs + p), pipeline_mode=pl.Buffered(1))] + ex.specs,
        out_specs=[pl.BlockSpec((subs * QB, QB), lambda p, i: (i, p))] + ex.specs,
        out_shape=[jax.ShapeDtypeStruct((S, n_pairs * QB), F32)] + ex.out_shape,
        scratch_shapes=ex.scratch,
        compiler_params=_params("arbitrary", "arbitrary"),
    )(qkv, qkv, qkv, *ex.arrays)
    return outs[0], outs[1:]


def _normalized_heads(pool_out, attn_out):
    rp, ra = _rms(pool_out), _rms(attn_out)
    return pool_out * rp, rp, attn_out * ra, ra


def _fwd_outproj(pool_out, attn_out, pool_scale, attn_scale, w_out, x, g2, g3, tile):
    S, D = x.shape
    C = pool_out.shape[1]

    def body(p_ref, a_ref, ps_ref, as_ref, w_ref, x_ref, g2_ref, g3_ref, mix_ref, x2_ref, h2_ref, h2t_ref):
        n_p, _, n_a, _ = _normalized_heads(p_ref[...], a_ref[...])
        mix = _dot((n_p * ps_ref[...]).astype(BF16), w_ref[:C, :]) + _dot((n_a * as_ref[...]).astype(BF16), w_ref[C:, :])
        mix_ref[...] = mix
        x2 = x_ref[...] + mix * _rms(mix) * g2_ref[...]
        x2_ref[...] = x2
        h2 = (x2 * _rms(x2) * g3_ref[...]).astype(BF16)
        h2_ref[...] = h2
        h2t_ref[...] = h2.T

    row = lambda w: pl.BlockSpec((tile, w), lambda i: (i, 0))
    return pl.pallas_call(
        body, name="fwd_outproj", grid=(S // tile,),
        in_specs=[row(C), row(C), _const((1, C)), _const((1, C)), _const(w_out.shape), row(D), _const((1, D)), _const((1, D))],
        out_specs=[row(D), row(D), row(D), pl.BlockSpec((D, tile), lambda i: (0, i))],
        out_shape=[jax.ShapeDtypeStruct((S, D), F32), jax.ShapeDtypeStruct((S, D), F32), jax.ShapeDtypeStruct((S, D), BF16),
                   jax.ShapeDtypeStruct((D, S), BF16)],
        compiler_params=_params("parallel"),
    )(pool_out, attn_out, pool_scale, attn_scale, w_out, x, g2, g3)


def _conv_taps(tile_rows, halo_rows):
    T = tile_rows.shape[0]
    ext = jnp.concatenate([halo_rows.astype(F32), tile_rows.astype(F32)], axis=0)
    return pltpu.roll(ext, 2, axis=0)[HALO:], pltpu.roll(ext, 1, axis=0)[HALO:], ext[HALO:]


def _tap_rows(cw_ref, d):
    return [cw_ref[d, k:k + 1, :] for k in range(3)]


def _gated_unit(taps_gate, taps_val, cw_gate, cw_val, cb_gate, cb_val):
    gate = cw_gate[0] * taps_gate[0] + cw_gate[1] * taps_gate[1] + cw_gate[2] * taps_gate[2] + cb_gate
    val = cw_val[0] * taps_val[0] + cw_val[1] * taps_val[1] + cw_val[2] * taps_val[2] + cb_val
    sig = 1.0 / (1.0 + jnp.exp(-gate))
    return gate, val, sig


def _fwd_ffn_loss(h2, w_up_g, conv_w_g, conv_b_g, w_down4, x2, target, g4, tile):
    S, D = x2.shape
    nb, _, cs = w_up_g.shape
    half = D_FF_SHARDS

    def body(h_ref, w_ref, cw_ref, cb_ref, wd_ref, x2_ref, t_ref, g4_ref, upre_ref, gv_ref, dy_ref, df_ref, loss_ref, dg4_ref, halo_ref):
        _zero_when(pl.program_id(0) == 0, loss_ref, dg4_ref, halo_ref)
        h = h_ref[...]

        def up(s):
            return _dot(h, w_ref[s]).astype(BF16), _dot(h, w_ref[s + half]).astype(BF16)

        f = jnp.zeros((tile, D), F32)
        ahead = up(0)
        for s in range(half):
            ug, uv = ahead
            if s + 1 < half:
                ahead = up(s + 1)
            upre_ref[s] = ug
            upre_ref[s + half] = uv
            gate, val, sig = _gated_unit(_conv_taps(ug, halo_ref[s]), _conv_taps(uv, halo_ref[s + half]),
                                         _tap_rows(cw_ref, s), _tap_rows(cw_ref, s + half), cb_ref[s], cb_ref[s + half])
            halo_ref[s] = ug[tile - HALO:, :]
            halo_ref[s + half] = uv[tile - HALO:, :]
            gv_ref[s] = gate.astype(BF16)
            gv_ref[s + half] = val.astype(BF16)
            f = f + _dot((gate * sig * val).astype(BF16), wd_ref[s])
        r4 = _rms(f)
        n4 = f * r4
        err = x2_ref[...] + n4 * g4_ref[...] - t_ref[...]
        dy = err * (1.0 / D)
        dy_ref[...] = dy
        df_ref[...] = _norm_bwd(dy * g4_ref[...], n4, r4).astype(BF16)
        loss_ref[...] += _colsum(err * err)
        dg4_ref[...] += _colsum(dy * n4)

    row = lambda w: pl.BlockSpec((tile, w), lambda i: (i, 0))
    return pl.pallas_call(
        body, name="fwd_ffn_loss", grid=(S // tile,),
        in_specs=[row(D), _const(w_up_g.shape), _const(conv_w_g.shape), _const(conv_b_g.shape), _const(w_down4.shape),
                  row(D), row(D), _const((1, D))],
        out_specs=[pl.BlockSpec((nb, tile, cs), lambda i: (0, i, 0)), pl.BlockSpec((nb, tile, cs), lambda i: (0, i, 0)), row(D), row(D),
                   pl.BlockSpec((1, D), lambda i: (0, 0)), pl.BlockSpec((1, D), lambda i: (0, 0))],
        out_shape=[jax.ShapeDtypeStruct((nb, S, cs), BF16), jax.ShapeDtypeStruct((nb, S, cs), BF16),
                   jax.ShapeDtypeStruct((S, D), F32), jax.ShapeDtypeStruct((S, D), BF16),
                   jax.ShapeDtypeStruct((1, D), F32), jax.ShapeDtypeStruct((1, D), F32)],
        scratch_shapes=[pltpu.VMEM((nb, HALO, cs), BF16)],
        compiler_params=_params("arbitrary"),
    )(h2, w_up_g, conv_w_g, conv_b_g, w_down4, x2, target, g4)


def _bwd_down(upre, conv_w_g, conv_b_g, w_down4, df, tile):
    nb, S, cs = upre.shape
    D = df.shape[1]
    n_tiles = S // tile

    def body(ug_ref, uv_ref, hg_ref, hv_ref, cwg_ref, cwv_ref, cbg_ref, cbv_ref, wd_ref, df_ref,
             dg_ref, dv_ref, dwd_ref, dbg_ref, dbv_ref, dcwg_ref, dcwv_ref):
        i = pl.program_id(1)
        first = i == 0
        _zero_when(first, dwd_ref, dbg_ref, dbv_ref, dcwg_ref, dcwv_ref)
        halo_g = jnp.where(first, jnp.zeros_like(hg_ref[0]), hg_ref[0])
        halo_v = jnp.where(first, jnp.zeros_like(hv_ref[0]), hv_ref[0])
        taps_g, taps_v = _conv_taps(ug_ref[0], halo_g), _conv_taps(uv_ref[0], halo_v)
        gate, val, sig = _gated_unit(taps_g, taps_v, _tap_rows(cwg_ref, 0), _tap_rows(cwv_ref, 0), cbg_ref[0], cbv_ref[0])
        silu = gate * sig
        dfb = df_ref[...]
        dact = _dot_nt(dfb, wd_ref[0])
        dwd_ref[0] += _dot_tn((silu * val).astype(BF16), dfb)
        dgate = dact * val * (sig * (1.0 + gate * (1.0 - sig)))
        dval = dact * silu
        dg_ref[0] = dgate.astype(BF16)
        dv_ref[0] = dval.astype(BF16)
        dbg_ref[0] += _colsum(dgate)
        dbv_ref[0] += _colsum(dval)
        for k in range(3):
            dcwg_ref[0, k:k + 1, :] += _colsum(dgate * taps_g[k])
            dcwv_ref[0, k:k + 1, :] += _colsum(dval * taps_v[k])

    half = D_FF_SHARDS
    blk = lambda off: pl.BlockSpec((1, tile, cs), lambda s, i: (s + off, i, 0))
    halo = lambda off: pl.BlockSpec((1, HALO, cs), lambda s, i: (s + off, jnp.maximum(i * (tile // HALO) - 1, 0), 0))
    par = lambda off, r: pl.BlockSpec((1, r, cs), lambda s, i: (s + off, 0, 0))
    outs = pl.pallas_call(
        body, name="bwd_down", grid=(half, n_tiles),
        in_specs=[blk(0), blk(half), halo(0), halo(half), par(0, 3), par(half, 3), par(0, 1), par(half, 1),
                  pl.BlockSpec((1, cs, D), lambda s, i: (s, 0, 0)), pl.BlockSpec((tile, D), lambda s, i: (i, 0))],
        out_specs=[blk(0), blk(0), pl.BlockSpec((1, cs, D), lambda s, i: (s, 0, 0)),
                   par(0, 1), par(0, 1), par(0, 3), par(0, 3)],
        out_shape=[jax.ShapeDtypeStruct((half, S, cs), BF16), jax.ShapeDtypeStruct((half, S, cs), BF16),
                   jax.ShapeDtypeStruct((half, cs, D), F32),
                   jax.ShapeDtypeStruct((half, 1, cs), F32), jax.ShapeDtypeStruct((half, 1, cs), F32),
                   jax.ShapeDtypeStruct((half, 3, cs), F32), jax.ShapeDtypeStruct((half, 3, cs), F32)],
        compiler_params=_params("parallel", "arbitrary"),
    )(upre, upre, upre, upre, conv_w_g, conv_w_g, conv_b_g, conv_b_g, w_down4, df)
    dgate, dval, d_wd, dbg, dbv, dcwg, dcwv = outs
    return dgate, dval, d_wd, jnp.concatenate([dbg, dbv], axis=0), jnp.concatenate([dcwg, dcwv], axis=0)


def _bwd_up_x(dgate, dval, conv_w_g, w_up_g, x2, dy, mix, g2, g3, tile):
    half, S, cs = dgate.shape
    nb = 2 * half
    D = x2.shape[1]
    n_tiles = S // tile

    def body(dg_ref, dv_ref, hg_ref, hv_ref, cw_ref, w_ref, x2_ref, dy_ref, mix_ref, g2_ref, g3_ref,
             dupre_ref, dx2_ref, dmix_ref, dg3_ref, dg2_ref):
        i = pl.program_id(0)
        last = i == n_tiles - 1
        _zero_when(i == 0, dg3_ref, dg2_ref)
        dh2 = jnp.zeros((tile, D), F32)
        for d in range(nb):
            src, halo = (dg_ref, hg_ref) if d < half else (dv_ref, hv_ref)
            nxt = jnp.where(last, jnp.zeros_like(halo[d % half]), halo[d % half])
            ext = jnp.concatenate([src[d % half].astype(F32), nxt.astype(F32)], axis=0)
            n = ext.shape[0]
            cw = _tap_rows(cw_ref, d)
            dupre = (cw[2] * ext + cw[1] * pltpu.roll(ext, n - 1, axis=0) + cw[0] * pltpu.roll(ext, n - 2, axis=0))[:tile]
            dupre = dupre.astype(BF16)
            dupre_ref[d] = dupre
            dh2 = dh2 + _dot_nt(dupre, w_ref[d])
        x2 = x2_ref[...]
        r3 = _rms(x2)
        n3 = x2 * r3
        dg3_ref[...] += _colsum(dh2 * n3)
        dx2 = dy_ref[...] + _norm_bwd(dh2 * g3_ref[...], n3, r3)
        dx2_ref[...] = dx2
        mix = mix_ref[...]
        r2 = _rms(mix)
        n2 = mix * r2
        dg2_ref[...] += _colsum(dx2 * n2)
        dmix_ref[...] = _norm_bwd(dx2 * g2_ref[...], n2, r2).astype(BF16)

    row = lambda w: pl.BlockSpec((tile, w), lambda i: (i, 0))
    blk = pl.BlockSpec((half, tile, cs), lambda i: (0, i, 0))
    last_halo = n_tiles * (tile // HALO) - 1
    halo = pl.BlockSpec((half, HALO, cs), lambda i: (0, jnp.minimum((i + 1) * (tile // HALO), last_halo), 0))
    acc = pl.BlockSpec((1, D), lambda i: (0, 0))
    return pl.pallas_call(
        body, name="bwd_up_x", grid=(n_tiles,),
        in_specs=[blk, blk, halo, halo, _const(conv_w_g.shape), _const(w_up_g.shape), row(D), row(D), row(D),
                  _const((1, D)), _const((1, D))],
        out_specs=[pl.BlockSpec((nb, tile, cs), lambda i: (0, i, 0)), row(D), row(D), acc, acc],
        out_shape=[jax.ShapeDtypeStruct((nb, S, cs), BF16), jax.ShapeDtypeStruct((S, D), F32),
                   jax.ShapeDtypeStruct((S, D), BF16), jax.ShapeDtypeStruct((1, D), F32), jax.ShapeDtypeStruct((1, D), F32)],
        compiler_params=_params("arbitrary"),
    )(dgate, dval, dgate, dval, conv_w_g, w_up_g, x2, dy, mix, g2, g3)


def _bwd_weight(act_t, dout, tile):
    D, S = act_t.shape
    nb, _, cs = dout.shape

    def body(a_ref, d_ref, o_ref):
        _zero_when(pl.program_id(1) == 0, o_ref)
        o_ref[0] += _dot(a_ref[...], d_ref[0])

    return pl.pallas_call(
        body, name="bwd_w_up", grid=(nb, S // tile),
        in_specs=[pl.BlockSpec((D, tile), lambda d, i: (0, i)), pl.BlockSpec((1, tile, cs), lambda d, i: (d, i, 0))],
        out_specs=pl.BlockSpec((1, D, cs), lambda d, i: (d, 0, 0)),
        out_shape=jax.ShapeDtypeStruct((nb, D, cs), F32),
        compiler_params=_params("parallel", "arbitrary"),
    )(act_t, dout)


def _bwd_ffn_blocks(gate_val, upre, conv_w_g, w_down4, df, h2_t, tile):
    nb, S, cs = upre.shape
    D = df.shape[1]
    n_tiles = S // tile
    half = D_FF_SHARDS

    def body(g_ref, v_ref, ug_ref, uv_ref, cwg_ref, cwv_ref, wd_ref, df_ref, ht_ref,
             dug_ref, duv_ref, dwd_ref, dwg_ref, dwv_ref, dbg_ref, dbv_ref, dcwg_ref, dcwv_ref, next_ref):
        _zero_when(pl.program_id(1) == 0, dwd_ref, dwg_ref, dwv_ref, dbg_ref, dbv_ref, dcwg_ref, dcwv_ref, next_ref)
        dfb = df_ref[...]
        dact = _dot_nt(dfb, wd_ref[0])
        gate, val = g_ref[0].astype(F32), v_ref[0].astype(F32)
        sig = 1.0 / (1.0 + jnp.exp(-gate))
        silu = gate * sig
        dwd_ref[0] += _dot_tn((silu * val).astype(BF16), dfb)
        ht = ht_ref[...]

        def through_conv(dup, slot, cw_ref, u_ref, du_ref, dw_ref, db_ref, dcw_ref):
            ext = jnp.concatenate([dup, next_ref[slot]], axis=0)
            n = ext.shape[0]
            shifted = (dup, pltpu.roll(ext, n - 1, axis=0)[:tile], pltpu.roll(ext, n - 2, axis=0)[:tile])
            next_ref[slot] = dup[:HALO]
            cw = _tap_rows(cw_ref, 0)
            dupre = (cw[2] * shifted[0] + cw[1] * shifted[1] + cw[0] * shifted[2]).astype(BF16)
            du_ref[0] = dupre
            dw_ref[0] += _dot(ht, dupre)
            u = u_ref[0].astype(F32)
            db_ref[0] += _colsum(dup)
            for k in range(3):
                dcw_ref[0, k:k + 1, :] += _colsum(shifted[2 - k] * u)

        through_conv(dact * val * (sig * (1.0 + gate * (1.0 - sig))), 0, cwg_ref, ug_ref, dug_ref, dwg_ref, dbg_ref, dcwg_ref)
        through_conv(dact * silu, 1, cwv_ref, uv_ref, duv_ref, dwv_ref, dbv_ref, dcwv_ref)

    rev = lambda i: n_tiles - 1 - i
    blk = lambda off: pl.BlockSpec((1, tile, cs), lambda s, i: (s + off, rev(i), 0))
    par = lambda off, r: pl.BlockSpec((1, r, cs), lambda s, i: (s + off, 0, 0))
    acc = lambda r, c: pl.BlockSpec((1, r, c), lambda s, i: (s, 0, 0))
    outs = pl.pallas_call(
        body, name="bwd_ffn_blocks", grid=(half, n_tiles),
        in_specs=[blk(0), blk(half), blk(0), blk(half), par(0, 3), par(half, 3),
                  acc(cs, D), pl.BlockSpec((tile, D), lambda s, i: (rev(i), 0)), pl.BlockSpec((D, tile), lambda s, i: (0, rev(i)))],
        out_specs=[blk(0), blk(0), acc(cs, D), acc(D, cs), acc(D, cs), acc(1, cs), acc(1, cs), acc(3, cs), acc(3, cs)],
        out_shape=[jax.ShapeDtypeStruct((half, S, cs), BF16), jax.ShapeDtypeStruct((half, S, cs), BF16),
                   jax.ShapeDtypeStruct((half, cs, D), F32),
                   jax.ShapeDtypeStruct((half, D, cs), F32), jax.ShapeDtypeStruct((half, D, cs), F32),
                   jax.ShapeDtypeStruct((half, 1, cs), F32), jax.ShapeDtypeStruct((half, 1, cs), F32),
                   jax.ShapeDtypeStruct((half, 3, cs), F32), jax.ShapeDtypeStruct((half, 3, cs), F32)],
        scratch_shapes=[pltpu.VMEM((2, HALO, cs), F32)],
        compiler_params=_params("arbitrary", "arbitrary"),
    )(gate_val, gate_val, upre, upre, conv_w_g, conv_w_g, w_down4, df, h2_t)
    dupre_g, dupre_v, d_wd, d_wg, d_wv, dbg, dbv, dcwg, dcwv = outs
    return (dupre_g, dupre_v, d_wd, jnp.concatenate([d_wg, d_wv], axis=0), jnp.concatenate([dbg, dbv], axis=0),
            jnp.concatenate([dcwg, dcwv], axis=0))


def _bwd_ffn_tokens(dupre_g, dupre_v, w_up_g, x2, dy, mix, g2, g3, tile):
    half, S, cs = dupre_g.shape
    D = x2.shape[1]

    def body(dg_ref, dv_ref, w_ref, x2_ref, dy_ref, mix_ref, g2_ref, g3_ref, dx2_ref, dmix_ref, dg3_ref, dg2_ref):
        _zero_when(pl.program_id(0) == 0, dg3_ref, dg2_ref)
        parts = [_dot_nt(dg_ref[d], w_ref[d]) for d in range(half)] + [_dot_nt(dv_ref[d], w_ref[d + half]) for d in range(half)]
        while len(parts) > 1:
            parts = [a + b for a, b in zip(parts[::2], parts[1::2])]
        dh2 = parts[0]
        x2 = x2_ref[...]
        r3 = _rms(x2)
        n3 = x2 * r3
        dg3_ref[...] += _colsum(dh2 * n3)
        dx2 = dy_ref[...] + _norm_bwd(dh2 * g3_ref[...], n3, r3)
        dx2_ref[...] = dx2
        mix = mix_ref[...]
        r2 = _rms(mix)
        n2 = mix * r2
        dg2_ref[...] += _colsum(dx2 * n2)
        dmix_ref[...] = _norm_bwd(dx2 * g2_ref[...], n2, r2).astype(BF16)

    row = lambda w: pl.BlockSpec((tile, w), lambda i: (i, 0))
    blk = pl.BlockSpec((half, tile, cs), lambda i: (0, i, 0))
    acc = pl.BlockSpec((1, D), lambda i: (0, 0))
    return pl.pallas_call(
        body, name="bwd_ffn_tokens", grid=(S // tile,),
        in_specs=[blk, blk, _const(w_up_g.shape), row(D), row(D), row(D), _const((1, D)), _const((1, D))],
        out_specs=[row(D), row(D), acc, acc],
        out_shape=[jax.ShapeDtypeStruct((S, D), F32), jax.ShapeDtypeStruct((S, D), BF16),
                   jax.ShapeDtypeStruct((1, D), F32), jax.ShapeDtypeStruct((1, D), F32)],
        compiler_params=_params("arbitrary"),
    )(dupre_g, dupre_v, w_up_g, x2, dy, mix, g2, g3)


def _bwd_outproj(dmix, w_out, pool_out, attn_out, pool_scale, attn_scale, tile):
    S, D = dmix.shape
    C = pool_out.shape[1]

    def body(dm_ref, w_ref, p_ref, a_ref, ps_ref, as_ref, dp_ref, da_ref, dw_ref, dps_ref, das_ref):
        _zero_when(pl.program_id(0) == 0, dw_ref, dps_ref, das_ref)
        dmx = dm_ref[...]
        dmerged = _dot_nt(dmx, w_ref[...])
        n_p, r_p, n_a, r_a = _normalized_heads(p_ref[...], a_ref[...])
        merged = jnp.concatenate([(n_p * ps_ref[...]).astype(BF16), (n_a * as_ref[...]).astype(BF16)], axis=1)
        dw_ref[...] += _dot_tn(merged, dmx)
        dm_p, dm_a = dmerged[:, :C], dmerged[:, C:]
        dps_ref[...] += _colsum(dm_p * n_p)
        das_ref[...] += _colsum(dm_a * n_a)
        dp_ref[...] = _norm_bwd(dm_p * ps_ref[...], n_p, r_p)
        da_ref[...] = _norm_bwd(dm_a * as_ref[...], n_a, r_a)

    row = lambda w: pl.BlockSpec((tile, w), lambda i: (i, 0))
    return pl.pallas_call(
        body, name="bwd_outproj", grid=(S // tile,),
        in_specs=[row(D), _const(w_out.shape), row(C), row(C), _const((1, C)), _const((1, C))],
        out_specs=[row(C), row(C), pl.BlockSpec(w_out.shape, lambda i: (0, 0)),
                   pl.BlockSpec((1, C), lambda i: (0, 0)), pl.BlockSpec((1, C), lambda i: (0, 0))],
        out_shape=[jax.ShapeDtypeStruct((S, C), F32), jax.ShapeDtypeStruct((S, C), F32),
                   jax.ShapeDtypeStruct(w_out.shape, F32), jax.ShapeDtypeStruct((1, C), F32), jax.ShapeDtypeStruct((1, C), F32)],
        compiler_params=_params("arbitrary"),
    )(dmix, w_out, pool_out, attn_out, pool_scale, attn_scale)


def _bwd_attn(qkv, d_attn, n_pairs, ex, subs):
    S = qkv.shape[0]
    n_steps = S // (subs * QB)

    def body(q_ref, k_ref, v_ref, do_ref, *rest):
        dq_ref, dk_ref, dv_ref = rest[ex.n:ex.n + 3]
        ex_refs = ex.split(rest[:ex.n] + rest[ex.n + 3:])
        first_step, last_step = _grid_ends((n_pairs, n_steps))

        @pl.when(first_step)
        def _():
            ex.start(*ex_refs)

        @pl.when(pl.program_id(1) == 0)
        def _():
            dk_ref[...] = jnp.zeros_like(dk_ref)
            dv_ref[...] = jnp.zeros_like(dv_ref)

        low_lanes = _low_lanes()
        after_s, from_s = _triangle(False), _triangle(True)
        zero = jnp.zeros((QB, 1), F32)

        def tiles(qhs, dohs, totals, kws, vws, masks, cs, gs, scores=None):
            fw = _attn_weights(scores or _attn_scores(qhs, kws, masks), masks, cs, after_s)
            gvals = [t[2] * _dot_nt(doh, vw) for t, doh, vw in zip(fw, dohs, vws)]
            sums = [_suffix_sums(g, from_s, g0) for g, g0 in zip(gvals, gs)]
            totals = [tot if m is None else tot + sm[1] for tot, m, sm in zip(totals, masks, sums)]
            dzs = []
            for (z, e, _, _), g, (nearer, _), tot, m in zip(fw, gvals, sums, totals, masks):
                inv = 1.0 / (1.0 + e)
                sig_abs, sig_neg = inv, e * inv
                pos = z >= 0.0
                dz = g * jnp.where(pos, sig_neg, sig_abs) - jnp.where(pos, sig_abs, sig_neg) * (tot - nearer)
                if m is not None:
                    dz = jnp.where(m, dz, 0.0)
                dzs.append((dz * ATTN_SCALE).astype(BF16))
            dqs = [_dot(dz, kw) for dz, kw in zip(dzs, kws)]
            dks = [_dot_tn(dz, qh) for dz, qh in zip(dzs, qhs)]
            dvs = [_dot_tn(t[2].astype(BF16), doh) for t, doh in zip(fw, dohs)]
            return [(dq, dk, dv, t[3], sm[1], tot) for dq, dk, dv, t, sm, tot in zip(dqs, dks, dvs, fw, sums, totals)]

        def cond(c):
            return jnp.logical_and(c[0] >= 0, c[1] == 0)

        qhs, dohs, kws, vws, masks, first_blks, starts = [], [], [], [], [], [], []
        for sub in range(subs):
            i = pl.program_id(1) * subs + sub
            rows = slice(sub * QB, (sub + 1) * QB)
            first_blk, start, offset = _first_window(i)
            first_blks.append(first_blk)
            starts.append(start)
            qhs += _split_heads(q_ref[rows, :].astype(F32), low_lanes)
            dohs += _split_heads(do_ref[rows, :], low_lanes)
            kws += [k_ref[pl.ds(start, 2 * QB), :]] * 2
            vws += [v_ref[pl.ds(start, 2 * QB), :]] * 2
            masks += [_causal_mask(2 * QB, offset)] * 2
        zeros = [zero] * len(qhs)

        scores = _attn_scores(qhs, kws, masks)
        c_first = [_row_sums(sc[3], zero) for sc in scores]
        beyond_first = []
        for sub in range(subs):
            pair = slice(2 * sub, 2 * sub + 2)

            def far_sums(c, qh=qhs[pair], doh=dohs[pair]):
                j, _, c0, c1, r0, r1 = c
                at = pl.multiple_of(j * QB, QB)
                kb = k_ref[pl.ds(at, QB), :]
                vb = v_ref[pl.ds(at, QB), :]
                far = _attn_tiles(qh, [kb, kb], [None, None], [c0, c1], after_s)
                r0 = r0 + jnp.sum(far[0][2] * _dot_nt(doh[0], vb), axis=1, keepdims=True)
                r1 = r1 + jnp.sum(far[1][2] * _dot_nt(doh[1], vb), axis=1, keepdims=True)
                return j - 1, _sweep_done(far[0][3], far[1][3]), far[0][3], far[1][3], r0, r1

            c0, c1 = c_first[pair]
            far = lax.while_loop(cond, far_sums, (first_blks[sub] - 1, _sweep_done(c0, c1), c0, c1, zero, zero))
            beyond_first += [far[4], far[5]]

        done = tiles(qhs, dohs, beyond_first, kws, vws, masks, zeros, zeros, scores)
        for sub in range(subs):
            dk_ref[pl.ds(starts[sub], 2 * QB), :] += done[2 * sub][1] + done[2 * sub + 1][1]
            dv_ref[pl.ds(starts[sub], 2 * QB), :] += done[2 * sub][2] + done[2 * sub + 1][2]

        for sub in range(subs):
            pair = slice(2 * sub, 2 * sub + 2)
            t0, t1 = done[pair]

            def step(c, qh=qhs[pair], doh=dohs[pair], total=[t0[5], t1[5]]):
                j, _, dq, c0, c1, s0, s1 = c
                at = pl.multiple_of(j * QB, QB)
                kb = k_ref[pl.ds(at, QB), :]
                vb = v_ref[pl.ds(at, QB), :]
                f0, f1 = tiles(qh, doh, total, [kb, kb], [vb, vb], [None, None], [c0, c1], [s0, s1])
                dk_ref[pl.ds(at, QB), :] += f0[1] + f1[1]
                dv_ref[pl.ds(at, QB), :] += f0[2] + f1[2]
                return j - 1, _sweep_done(f0[3], f1[3]), dq + jnp.where(low_lanes, f0[0], f1[0]), f0[3], f1[3], f0[4], f1[4]

            init = (first_blks[sub] - 1, _sweep_done(t0[3], t1[3]), jnp.where(low_lanes, t0[0], t1[0]), t0[3], t1[3], t0[4], t1[4])
            dq_ref[sub * QB:(sub + 1) * QB, :] = lax.while_loop(cond, step, init)[2]

        @pl.when(last_step)
        def _():
            ex.wait(*ex_refs)

    blk = pl.BlockSpec((subs * QB, QB), lambda p, i: (i, p))
    full = lambda off: pl.BlockSpec((S, QB), lambda p, i: (0, off + p), pipeline_mode=pl.Buffered(1))
    outs = pl.pallas_call(
        body, name="bwd_attn", grid=(n_pairs, n_steps),
        in_specs=[blk, full(n_pairs), full(2 * n_pairs), blk] + ex.specs,
        out_specs=[blk, pl.BlockSpec((S, QB), lambda p, i: (0, p)), pl.BlockSpec((S, QB), lambda p, i: (0, p))] + ex.specs,
        out_shape=[jax.ShapeDtypeStruct((S, n_pairs * QB), F32)] * 3 + ex.out_shape,
        scratch_shapes=ex.scratch,
        compiler_params=_params("arbitrary", "arbitrary"),
    )(qkv, qkv, qkv, d_attn, *ex.arrays)
    return outs[0], outs[1], outs[2], outs[3:]


def _bwd_pool(u, d_pool, w_pool, tile):
    S, C = u.shape
    n_tiles = S // tile
    ng = len(POOL_WINDOWS)

    def body(u_ref, uh_ref, d_ref, dh_ref, wp_ref, du_ref, dwp_ref):
        i = pl.program_id(0)
        first = i == 0
        _zero_when(first, dwp_ref)
        halo = jnp.where(first, 0.0, uh_ref[...])
        parts = _pool_deviation(u_ref[...], halo, i * tile)
        dout = d_ref[...]
        nxt = jnp.where(i == n_tiles - 1, 0.0, dh_ref[...])
        dext = jnp.concatenate([dout, nxt], axis=0).astype(BF16)
        counts = _pool_counts(i * tile, tile + HALO)
        dps, scaled = [], []
        for g in range(ng):
            lanes = slice(g * POOL_GROUP, (g + 1) * POOL_GROUP)
            dp = _dot_nt(dext[:, lanes], wp_ref[g].astype(BF16))
            dps.append(dp[:tile])
            scaled.append(dp / counts[g])
        sums = _window_sums(jnp.concatenate(scaled, axis=1), forward=True)
        for g, w in enumerate(POOL_WINDOWS):
            lanes = slice(g * POOL_GROUP, (g + 1) * POOL_GROUP)
            du_ref[:, lanes] = sums[w][:tile, lanes] - dps[g]
            dwp_ref[g] += _dot_tn(parts[g].astype(BF16), dext[:tile, lanes])

    row = pl.BlockSpec((tile, C), lambda i: (i, 0))
    return pl.pallas_call(
        body, name="bwd_pool", grid=(n_tiles,),
        in_specs=[row, _prev_halo_spec(tile, C), row, _next_halo_spec(tile, C, n_tiles), _const(w_pool.shape)],
        out_specs=[row, pl.BlockSpec(w_pool.shape, lambda i: (0, 0, 0))],
        out_shape=[jax.ShapeDtypeStruct((S, C), F32), jax.ShapeDtypeStruct(w_pool.shape, F32)],
        compiler_params=_params("arbitrary"),
    )(u, u, d_pool, d_pool, w_pool)


def _bwd_w_in(du, dq, dk, dv, h1_t, n_blocks, tile):
    D, S = h1_t.shape
    C = du.shape[1]
    cs = 4 * C // n_blocks
    per = C // cs

    def body(du_ref, dq_ref, dk_ref, dv_ref, ht_ref, dproj_ref, dw_ref):
        _zero_when(pl.program_id(0) == 0, dw_ref)
        ht = ht_ref[...]
        for d in range(n_blocks):
            src = (du_ref, dq_ref, dk_ref, dv_ref)[d // per]
            dproj = src[:, (d % per) * cs:(d % per + 1) * cs].astype(BF16)
            dproj_ref[:, d * cs:(d + 1) * cs] = dproj
            dw_ref[d] += _dot(ht, dproj)

    row = lambda w: pl.BlockSpec((tile, w), lambda i: (i, 0))
    return pl.pallas_call(
        body, name="bwd_w_in", grid=(S // tile,),
        in_specs=[row(C), row(C), row(C), row(C), pl.BlockSpec((D, tile), lambda i: (0, i))],
        out_specs=[row(4 * C), pl.BlockSpec((n_blocks, D, cs), lambda i: (0, 0, 0))],
        out_shape=[jax.ShapeDtypeStruct((S, 4 * C), BF16), jax.ShapeDtypeStruct((n_blocks, D, cs), F32)],
        compiler_params=_params("arbitrary"),
    )(du, dq, dk, dv, h1_t)


def _bwd_x(dproj, w_in_full, x, dx2, g1, tile, ex):
    S, D = x.shape
    n_tiles = S // tile

    def body(dp_ref, w_ref, x_ref, dx2_ref, g_ref, *rest):
        dx_ref, dg_ref = rest[ex.n:ex.n + 2]
        ex_refs = ex.split(rest[:ex.n] + rest[ex.n + 2:])
        first, last = _grid_ends((n_tiles,))

        @pl.when(first)
        def _():
            ex.start(*ex_refs)
            dg_ref[...] = jnp.zeros_like(dg_ref)

        dh = _dot_nt(dp_ref[...], w_ref[...])
        xf = x_ref[...]
        r1 = _rms(xf)
        n1 = xf * r1
        dg_ref[...] += _colsum(dh * n1)
        dx_ref[...] = dx2_ref[...] + _norm_bwd(dh * g_ref[...], n1, r1)

        @pl.when(last)
        def _():
            ex.wait(*ex_refs)

    row = lambda w: pl.BlockSpec((tile, w), lambda i: (i, 0))
    outs = pl.pallas_call(
        body, name="bwd_x", grid=(n_tiles,),
        in_specs=[row(w_in_full.shape[1]), _const(w_in_full.shape), row(D), row(D), _const((1, D))] + ex.specs,
        out_specs=[row(D), pl.BlockSpec((1, D), lambda i: (0, 0))] + ex.specs,
        out_shape=[jax.ShapeDtypeStruct((S, D), F32), jax.ShapeDtypeStruct((1, D), F32)] + ex.out_shape,
        scratch_shapes=ex.scratch,
        compiler_params=_params("arbitrary"),
    )(dproj, w_in_full, x, dx2, g1, *ex.arrays)
    return outs[0], outs[1], outs[2:]


def _mesh_position():
    x, y, c = lax.axis_index("x"), lax.axis_index("y"), lax.axis_index("c")
    return x, y, c, 4 * x + 2 * y + c


def _peer(x, y, c, k):
    px = 1 - x if k & 4 else x
    py = 1 - y if k & 2 else y
    pc = 1 - c if k & 1 else c
    return (px, py, pc), 4 * px + 2 * py + pc


class _Exchange:
    def __init__(self, arrays, gather):
        self.arrays, self.gather, self.n = list(arrays), gather, len(arrays)
        self.out_shape = [jax.ShapeDtypeStruct(((N_DEV,) + a.shape) if gather else a.shape, a.dtype) for a in arrays]
        self.specs = [pl.BlockSpec(memory_space=pl.ANY)] * self.n
        copies = self.n * (N_DEV - 1)
        self.scratch = [pltpu.SemaphoreType.DMA((copies,)), pltpu.SemaphoreType.DMA((copies,)),
                        pltpu.SemaphoreType.DMA((self.n,))]

    def _copies(self, ins, outs, sems):
        send_sems, recv_sems, local_sems = sems
        x, y, c, me = _mesh_position()
        local, remote = [], []
        for a in range(self.n):
            mine = ins[a] if self.gather else ins[a].at[me]
            local.append(pltpu.make_async_copy(mine, outs[a].at[me], local_sems.at[a]))
            for k in range(1, N_DEV):
                peer, peer_idx = _peer(x, y, c, k)
                src = ins[a] if self.gather else ins[a].at[peer_idx]
                sem = a * (N_DEV - 1) + k - 1
                remote.append(pltpu.make_async_remote_copy(
                    src_ref=src, dst_ref=outs[a].at[me], send_sem=send_sems.at[sem], recv_sem=recv_sems.at[sem],
                    device_id=peer, device_id_type=MESH))
        return local, remote

    def start(self, ins, outs, sems):
        local, remote = self._copies(ins, outs, sems)
        for cp in local + remote:
            cp.start()

    def wait(self, ins, outs, sems):
        local, remote = self._copies(ins, outs, sems)
        for cp in remote:
            cp.wait_send()
        for cp in remote:
            cp.wait_recv()
        for cp in local:
            cp.wait()

    def split(self, refs):
        return refs[:self.n], refs[self.n:2 * self.n], refs[2 * self.n:]


def _all_to_all(arrays, gather, name):
    ex = _Exchange(arrays, gather)

    def body(*refs):
        ins, outs, sems = ex.split(refs)
        ex.start(ins, outs, sems)
        ex.wait(ins, outs, sems)

    return pl.pallas_call(body, name=name, in_specs=ex.specs, out_specs=ex.specs, out_shape=ex.out_shape,
                          scratch_shapes=ex.scratch)(*ex.arrays)


def _reduce_adamw(parts, w, m, v, rows):
    R, C = w.shape

    def body(p_ref, w_ref, m_ref, v_ref, g_ref, d_ref, nm_ref, nv_ref):
        g = p_ref[0].astype(F32)
        for s in range(1, N_DEV):
            g = g + p_ref[s].astype(F32)
        g_ref[...] = g
        m_new = ADAM_B1 * m_ref[...] + (1.0 - ADAM_B1) * g
        v_new = ADAM_B2 * v_ref[...] + (1.0 - ADAM_B2) * (g * g)
        m_hat = m_new / (1.0 - ADAM_B1 ** ADAM_STEP)
        v_hat = v_new / (1.0 - ADAM_B2 ** ADAM_STEP)
        d_ref[...] = -ADAM_LR * (m_hat / (jnp.sqrt(v_hat) + ADAM_EPS) + ADAM_WD * w_ref[...])
        nm_ref[...] = m_new
        nv_ref[...] = v_new

    row = pl.BlockSpec((rows, C), lambda i: (i, 0))
    return pl.pallas_call(
        body, name="reduce_adamw", grid=(R // rows,),
        in_specs=[pl.BlockSpec((N_DEV, rows, C), lambda i: (0, i, 0)), row, row, row],
        out_specs=[row] * 4, out_shape=[jax.ShapeDtypeStruct((R, C), F32)] * 4,
        compiler_params=_params("parallel"),
    )(parts, w, m, v)


def _row_tile(rows, cols):
    fits = [t for t in range(8, rows + 1, 8) if rows % t == 0 and N_DEV * t * cols * 4 <= 4 * 1024 * 1024]
    return max(fits) if fits else rows


SMALL_COLS = 1024


def _pack_small(vals):
    rows = []
    for a in vals:
        flat = a.reshape(-1)
        pad = (-flat.shape[0]) % SMALL_COLS
        rows.append(jnp.pad(flat, (0, pad)).reshape(-1, SMALL_COLS))
    packed = jnp.concatenate(rows, axis=0)
    return jnp.pad(packed, ((0, (-packed.shape[0]) % 8), (0, 0)))


def _unpack_small(packed, like):
    out, r = [], 0
    for a in like:
        n = a.size
        nr = -(-n // SMALL_COLS)
        out.append(packed[r:r + nr].reshape(-1)[:n].reshape(a.shape))
        r += nr
    return out


def kernel(x, norm_mix_pre, w_in, w_pool, pool_scale, attn_scale, w_out, norm_mix_post, norm_ffn_pre, w_up, conv_w, conv_b, w_down, norm_ffn_post, loss_target, m_norm_mix_pre, m_w_in, m_w_pool, m_pool_scale, m_attn_scale, m_w_out, m_norm_mix_post, m_norm_ffn_pre, m_w_up, m_conv_w, m_conv_b, m_w_down, m_norm_ffn_post, v_norm_mix_pre, v_w_in, v_w_pool, v_pool_scale, v_attn_scale, v_w_out, v_norm_mix_post, v_norm_ffn_pre, v_w_up, v_conv_w, v_conv_b, v_w_down, v_norm_ffn_post):
    S, D = x.shape[1], x.shape[2]
    d_ff_block = w_up.shape[2]

    xs, target = x[0], loss_target[0]
    g1, g2, g3, g4 = norm_mix_pre, norm_mix_post, norm_ffn_pre, norm_ffn_post
    big = min(512, S)
    small = min(256, S)
    n_pairs = pool_scale.shape[1] // QB
    conv_b_g = conv_b.reshape(N_DEV, 1, d_ff_block)

    (w_in_g,) = _all_to_all([w_in[0].astype(BF16)], gather=True, name="gather_w_in")
    h1_t, u, qkv = _fwd_inproj(xs, g1, w_in_g, big)
    pool_out = _fwd_pool(u, w_pool[0], big)
    attn_out, (w_out_g, w_up_g, w_down_g, conv_w_g) = _fwd_attn(
        qkv, n_pairs, _Exchange([w_out[0].astype(BF16), w_up[0].astype(BF16), w_down[0].astype(BF16), conv_w[0]], gather=True),
        min(ATTN_FWD_BLOCKS, S // QB))
    w_out_full = w_out_g.reshape(D, D)
    w_down4 = w_down_g.reshape(D_FF_SHARDS, d_ff_block, D)
    mix, x2, h2, h2_t = _fwd_outproj(pool_out, attn_out, pool_scale, attn_scale, w_out_full, xs, g2, g3, big)
    upre, gate_val, dy, df, loss_cols, dg4 = _fwd_ffn_loss(h2, w_up_g, conv_w_g, conv_b_g, w_down4, x2, target, g4, small)
    loss = lax.psum(0.5 * jnp.sum(loss_cols) / D, ("x", "y", "c"))

    dupre_g, dupre_v, d_wd4, d_wup, d_cb, d_cw = _bwd_ffn_blocks(gate_val, upre, conv_w_g, w_down4, df, h2_t, big)
    dx2, dmix, dg3, dg2 = _bwd_ffn_tokens(dupre_g, dupre_v, w_up_g, x2, dy, mix, g2, g3, big)
    d_pool, d_attn, d_wout, d_ps, d_as = _bwd_outproj(dmix, w_out_full, pool_out, attn_out, pool_scale, attn_scale, big)
    d_wdown_g = d_wd4.reshape(N_DEV, w_down.shape[1], D)
    d_wout_g = d_wout.reshape(N_DEV, D // N_DEV, D)
    dq, dk, dv, late_parts = _bwd_attn(qkv, d_attn, n_pairs, _Exchange([d_wout_g, d_wup, d_wdown_g, d_cw], gather=False),
                                       min(ATTN_BWD_BLOCKS, S // QB))
    du, d_wp = _bwd_pool(u, d_pool, w_pool[0], big)
    dproj, d_win = _bwd_w_in(du, dq, dk, dv, h1_t, N_DEV, big)
    w_in_full = w_in_g.transpose(1, 0, 2).reshape(D, -1)
    dx, dg1, (win_parts,) = _bwd_x(dproj, w_in_full, xs, dx2, g1, big, _Exchange([d_win], gather=False))
    big_parts = [win_parts] + list(late_parts)
    r = dict(dx=dx, g1=dg1, w_pool=d_wp, pool_scale=d_ps, attn_scale=d_as, g2=dg2, g3=dg3, conv_b=d_cb, g4=dg4)

    small_names = ["norm_mix_pre", "w_pool", "pool_scale", "attn_scale", "norm_mix_post", "norm_ffn_pre", "conv_b", "norm_ffn_post"]
    small_w = dict(norm_mix_pre=norm_mix_pre, w_pool=w_pool, pool_scale=pool_scale, attn_scale=attn_scale,
                   norm_mix_post=norm_mix_post, norm_ffn_pre=norm_ffn_pre, conv_b=conv_b, norm_ffn_post=norm_ffn_post)
    small_m = dict(norm_mix_pre=m_norm_mix_pre, w_pool=m_w_pool, pool_scale=m_pool_scale, attn_scale=m_attn_scale,
                   norm_mix_post=m_norm_mix_post, norm_ffn_pre=m_norm_ffn_pre, conv_b=m_conv_b, norm_ffn_post=m_norm_ffn_post)
    small_v = dict(norm_mix_pre=v_norm_mix_pre, w_pool=v_w_pool, pool_scale=v_pool_scale, attn_scale=v_attn_scale,
                   norm_mix_post=v_norm_mix_post, norm_ffn_pre=v_norm_ffn_pre, conv_b=v_conv_b, norm_ffn_post=v_norm_ffn_post)
    small_g = dict(norm_mix_pre=r["g1"], w_pool=r["w_pool"], pool_scale=r["pool_scale"], attn_scale=r["attn_scale"],
                   norm_mix_post=r["g2"], norm_ffn_pre=r["g3"], conv_b=r["conv_b"], norm_ffn_post=r["g4"])
    like = [small_w[n] for n in small_names]
    packed_g = _pack_small([small_g[n] for n in small_names])

    (small_parts,) = _all_to_all([packed_g], gather=True, name="gather_small_grads")

    def update(parts, w, m, v):
        R, C = w.shape
        return _reduce_adamw(parts, w, m, v, _row_tile(R, C))

    res = {}
    res["w_in"] = update(big_parts[0], w_in[0], m_w_in[0], v_w_in[0])
    res["w_out"] = update(big_parts[1], w_out[0], m_w_out[0], v_w_out[0])
    res["w_up"] = update(big_parts[2], w_up[0], m_w_up[0], v_w_up[0])
    res["w_down"] = update(big_parts[3], w_down[0], m_w_down[0], v_w_down[0])
    res["conv_w"] = update(big_parts[4], conv_w[0], m_conv_w[0], v_conv_w[0])
    small_res = update(small_parts, _pack_small(like), _pack_small([small_m[n] for n in small_names]),
                       _pack_small([small_v[n] for n in small_names]))
    small_res = [_unpack_small(t, like) for t in small_res]
    for idx, n in enumerate(small_names):
        res[n] = tuple(t[idx] for t in small_res)

    order = ["norm_mix_pre", "w_in", "w_pool", "pool_scale", "attn_scale", "w_out", "norm_mix_post", "norm_ffn_pre",
             "w_up", "conv_w", "conv_b", "w_down", "norm_ffn_post"]
    shaped = {n: tuple(t.reshape(s.shape) for t in res[n])
              for n, s in dict(norm_mix_pre=norm_mix_pre, w_in=w_in, w_pool=w_pool, pool_scale=pool_scale, attn_scale=attn_scale,
                               w_out=w_out, norm_mix_post=norm_mix_post, norm_ffn_pre=norm_ffn_pre, w_up=w_up, conv_w=conv_w,
                               conv_b=conv_b, w_down=w_down, norm_ffn_post=norm_ffn_post).items()}
    outs = [loss, r["dx"].reshape(x.shape)]
    for k in range(4):
        outs += [shaped[n][k] for n in order]
    return tuple(outs)
```

```python
import functools

import jax
import jax.numpy as jnp
from jax import lax
from jax.experimental import pallas as pl
from jax.experimental.pallas import tpu as pltpu

F32 = jnp.float32
BF16 = jnp.bfloat16
HIGHEST = lax.Precision.HIGHEST

N_DEV = 8
EPS = 1e-6
POOL_WINDOWS = (2, 4, 8, 16)
POOL_GROUP = 128
HALO = 16
HEAD_DIM = 64
QB = 128
ATTN_SCALE = HEAD_DIM ** -0.5
ATTN_FWD_BLOCKS = 8
ATTN_BWD_BLOCKS = 4
EXP_UNDERFLOW = -88.0
D_FF_SHARDS = 4

ADAM_LR = 0.001
ADAM_B1 = 0.9
ADAM_B2 = 0.999
ADAM_EPS = 1e-08
ADAM_WD = 0.01
ADAM_STEP = 10

VMEM_LIMIT_V7X = 56 * 1024 * 1024
MESH = pl.DeviceIdType.MESH


def _params(*semantics):
    return pltpu.CompilerParams(dimension_semantics=semantics, vmem_limit_bytes=VMEM_LIMIT_V7X)


def _const(shape):
    zeros = (0,) * len(shape)
    return pl.BlockSpec(shape, lambda *_: zeros, pipeline_mode=pl.Buffered(1))


def _dot(a, b):
    return jnp.dot(a, b, preferred_element_type=F32)


def _dot_nt(a, b):
    return lax.dot_general(a, b, (((1,), (1,)), ((), ())), preferred_element_type=F32)


def _dot_tn(a, b):
    return lax.dot_general(a, b, (((0,), (0,)), ((), ())), preferred_element_type=F32)


def _rms(v):
    return lax.rsqrt(jnp.mean(v * v, axis=-1, keepdims=True) + EPS)


def _norm_bwd(dn_times_gain, n, r):
    return r * (dn_times_gain - n * jnp.mean(dn_times_gain * n, axis=-1, keepdims=True))


def _zero_when(first, *refs):
    @pl.when(first)
    def _():
        for ref in refs:
            ref[...] = jnp.zeros_like(ref)


def _colsum(v):
    return jnp.sum(v, axis=0, keepdims=True)


def _grid_ends(grid):
    ids = [pl.program_id(a) for a in range(len(grid))]
    first = functools.reduce(jnp.logical_and, [i == 0 for i in ids])
    last = functools.reduce(jnp.logical_and, [i == n - 1 for i, n in zip(ids, grid)])
    return first, last


def _fwd_inproj(x, g1, w_in_g, tile):
    S, D = x.shape
    nb, _, cs = w_in_g.shape
    d_pool = 2 * cs

    def body(x_ref, g_ref, w_ref, ht_ref, u_ref, qkv_ref):
        xf = x_ref[...]
        h = (xf * _rms(xf) * g_ref[...]).astype(BF16)
        ht_ref[...] = h.T
        for d in range(nb):
            o = _dot(h, w_ref[d])
            if d < 2:
                u_ref[:, d * cs:(d + 1) * cs] = o
            else:
                qkv_ref[:, (d - 2) * cs:(d - 1) * cs] = o.astype(BF16)

    return pl.pallas_call(
        body, name="fwd_inproj", grid=(S // tile,),
        in_specs=[pl.BlockSpec((tile, D), lambda i: (i, 0)), _const((1, D)), _const(w_in_g.shape)],
        out_specs=[pl.BlockSpec((D, tile), lambda i: (0, i)), pl.BlockSpec((tile, d_pool), lambda i: (i, 0)),
                   pl.BlockSpec((tile, 3 * d_pool), lambda i: (i, 0))],
        out_shape=[jax.ShapeDtypeStruct((D, S), BF16), jax.ShapeDtypeStruct((S, d_pool), F32),
                   jax.ShapeDtypeStruct((S, 3 * d_pool), BF16)],
        compiler_params=_params("parallel"),
    )(x, g1, w_in_g)


def _window_sums(ext, forward):
    n = ext.shape[0]
    sums, s, sh = {}, ext, 1
    while sh < POOL_WINDOWS[-1]:
        s = s + pltpu.roll(s, (n - sh) if forward else sh, axis=0)
        sh *= 2
        sums[sh] = s
    return sums


def _pool_counts(t0, rows):
    t1 = (lax.broadcasted_iota(jnp.int32, (rows, 1), 0) + t0 + 1).astype(F32)
    return [jnp.minimum(t1, float(w)) for w in POOL_WINDOWS]


def _pool_deviation(u, halo, t0):
    T = u.shape[0]
    sums = _window_sums(jnp.concatenate([halo, u], axis=0), forward=False)
    counts = _pool_counts(t0, T)
    parts = []
    for g, w in enumerate(POOL_WINDOWS):
        lanes = slice(g * POOL_GROUP, (g + 1) * POOL_GROUP)
        parts.append(sums[w][HALO:, lanes] / counts[g] - u[:, lanes])
    return parts


def _prev_halo_spec(tile, width):
    return pl.BlockSpec((HALO, width), lambda i: (jnp.maximum(i * (tile // HALO) - 1, 0), 0))


def _next_halo_spec(tile, width, n_tiles):
    last = n_tiles * (tile // HALO) - 1
    return pl.BlockSpec((HALO, width), lambda i: (jnp.minimum((i + 1) * (tile // HALO), last), 0))


def _fwd_pool(u, w_pool, tile):
    S, C = u.shape

    def body(u_ref, halo_ref, wp_ref, o_ref):
        i = pl.program_id(0)
        halo = jnp.where(i > 0, halo_ref[...], 0.0)
        parts = _pool_deviation(u_ref[...], halo, i * tile)
        for g, p in enumerate(parts):
            o_ref[:, g * POOL_GROUP:(g + 1) * POOL_GROUP] = _dot(p.astype(BF16), wp_ref[g].astype(BF16))

    return pl.pallas_call(
        body, name="fwd_pool", grid=(S // tile,),
        in_specs=[pl.BlockSpec((tile, C), lambda i: (i, 0)), _prev_halo_spec(tile, C), _const(w_pool.shape)],
        out_specs=pl.BlockSpec((tile, C), lambda i: (i, 0)),
        out_shape=jax.ShapeDtypeStruct((S, C), F32),
        compiler_params=_params("parallel"),
    )(u, u, w_pool)


def _low_lanes():
    return lax.broadcasted_iota(jnp.int32, (QB, 2 * HEAD_DIM), 1) < HEAD_DIM


LOG_PIECES = 2
GRAD_PIECES = 3


def _triangle(inclusive, pieces):
    row = lax.broadcasted_iota(jnp.int32, (pieces * QB, QB), 0) % QB
    col = lax.broadcasted_iota(jnp.int32, (pieces * QB, QB), 1)
    return ((row >= col) if inclusive else (row > col)).astype(BF16)


def _pieces(v, n):
    out, rest = [], v
    for _ in range(n - 1):
        piece = rest.astype(BF16)
        out.append(piece)
        rest = rest - piece.astype(F32)
    out.append(rest.astype(BF16))
    return jnp.concatenate(out, axis=1)


def _causal_mask(width, offset):
    row = lax.broadcasted_iota(jnp.int32, (QB, width), 0)
    col = lax.broadcasted_iota(jnp.int32, (QB, width), 1)
    return col < row + offset


def _row_sums(vals, carry):
    for b in reversed(range(vals.shape[1] // QB)):
        carry = carry + jnp.sum(vals[:, b * QB:(b + 1) * QB], axis=1, keepdims=True)
    return carry


def _suffix_sums(vals, tri, carry):
    n = vals.shape[1] // QB
    out, run = [None] * n, carry
    for b in reversed(range(n)):
        blk = vals[:, b * QB:(b + 1) * QB]
        out[b] = _dot(_pieces(blk, tri.shape[0] // QB), tri) + run
        run = run + jnp.sum(blk, axis=1, keepdims=True)
    return (out[0] if n == 1 else jnp.concatenate(out, axis=1)), run


def _attn_tiles(qhs, kws, masks, carries, after_s):
    return _attn_weights(_attn_scores(qhs, kws, masks), masks, carries, after_s)


def _attn_scores(qhs, kws, masks):
    zs = [_dot_nt(qh, kw) * ATTN_SCALE for qh, kw in zip(qhs, kws)]
    es = [jnp.exp(-jnp.abs(z)) for z in zs]
    softplus = [jnp.maximum(z, 0.0) + jnp.log(1.0 + e) for z, e in zip(zs, es)]
    log_1m_beta = [-sp if m is None else jnp.where(m, -sp, 0.0) for sp, m in zip(softplus, masks)]
    return list(zip(zs, es, softplus, log_1m_beta))


def _attn_weights(scores, masks, carries, after_s):
    sums = [_suffix_sums(l, after_s, c) for (_, _, _, l), c in zip(scores, carries)]
    weights = [jnp.exp(z - sp + st) for (z, _, sp, _), (st, _) in zip(scores, sums)]
    weights = [a if m is None else jnp.where(m, a, 0.0) for a, m in zip(weights, masks)]
    return [(z, e, a, c) for (z, e, _, _), a, (_, c) in zip(scores, weights, sums)]


def _split_heads(v, low_lanes):
    return jnp.where(low_lanes, v, 0.0).astype(BF16), jnp.where(low_lanes, 0.0, v).astype(BF16)


def _sweep_done(c0, c1):
    return (jnp.maximum(jnp.max(c0), jnp.max(c1)) < EXP_UNDERFLOW).astype(jnp.int32)


def _first_window(i):
    first_blk = jnp.maximum(i - 1, 0)
    return first_blk, pl.multiple_of(first_blk * QB, QB), (i - first_blk) * QB


def _fwd_attn(qkv, n_pairs, ex, subs):
    S = qkv.shape[0]
    n_steps = S // (subs * QB)

    def body(q_ref, k_ref, v_ref, *rest):
        o_ref = rest[ex.n]
        ex_refs = ex.split(rest[:ex.n] + rest[ex.n + 1:])
        first_step, last_step = _grid_ends((n_pairs, n_steps))

        @pl.when(first_step)
        def _():
            ex.start(*ex_refs)

        low_lanes = _low_lanes()
        after_s = _triangle(False, LOG_PIECES)
        zero = jnp.zeros((QB, 1), F32)

        def cond(c):
            return jnp.logical_and(c[0] >= 0, c[1] == 0)

        qhs, kws, vws, masks, first_blks = [], [], [], [], []
        for sub in range(subs):
            i = pl.program_id(1) * subs + sub
            first_blk, start, offset = _first_window(i)
            first_blks.append(first_blk)
            qhs += _split_heads(q_ref[sub * QB:(sub + 1) * QB, :].astype(F32), low_lanes)
            kws += [k_ref[pl.ds(start, 2 * QB), :]] * 2
            vws += [v_ref[pl.ds(start, 2 * QB), :]] * 2
            masks += [_causal_mask(2 * QB, offset)] * 2
        tiles = _attn_tiles(qhs, kws, masks, [zero] * len(qhs), after_s)
        outs = [_dot(t[2].astype(BF16), vw) for t, vw in zip(tiles, vws)]

        for sub in range(subs):
            def step(c, qh=qhs[2 * sub:2 * sub + 2]):
                j, _, acc, c0, c1 = c
                at = pl.multiple_of(j * QB, QB)
                kb = k_ref[pl.ds(at, QB), :]
                vb = v_ref[pl.ds(at, QB), :]
                far = _attn_tiles(qh, [kb, kb], [None, None], [c0, c1], after_s)
                acc = acc + jnp.where(low_lanes, _dot(far[0][2].astype(BF16), vb), _dot(far[1][2].astype(BF16), vb))
                return j - 1, _sweep_done(far[0][3], far[1][3]), acc, far[0][3], far[1][3]

            c0, c1 = tiles[2 * sub][3], tiles[2 * sub + 1][3]
            init = (first_blks[sub] - 1, _sweep_done(c0, c1), jnp.where(low_lanes, outs[2 * sub], outs[2 * sub + 1]), c0, c1)
            o_ref[sub * QB:(sub + 1) * QB, :] = lax.while_loop(cond, step, init)[2]

        @pl.when(last_step)
        def _():
            ex.wait(*ex_refs)

    outs = pl.pallas_call(
        body, name="fwd_attn", grid=(n_pairs, n_steps),
        in_specs=[pl.BlockSpec((subs * QB, QB), lambda p, i: (i, p)),
                  pl.BlockSpec((S, QB), lambda p, i: (0, n_pairs + p), pipeline_mode=pl.Buffered(1)),
                  pl.BlockSpec((S, QB), lambda p, i: (0, 2 * n_pairs + p), pipeline_mode=pl.Buffered(1))] + ex.specs,
        out_specs=[pl.BlockSpec((subs * QB, QB), lambda p, i: (i, p))] + ex.specs,
        out_shape=[jax.ShapeDtypeStruct((S, n_pairs * QB), F32)] + ex.out_shape,
        scratch_shapes=ex.scratch,
        compiler_params=_params("arbitrary", "arbitrary"),
    )(qkv, qkv, qkv, *ex.arrays)
    return outs[0], outs[1:]


def _normalized_heads(pool_out, attn_out):
    rp, ra = _rms(pool_out), _rms(attn_out)
    return pool_out * rp, rp, attn_out * ra, ra


def _fwd_outproj(pool_out, attn_out, pool_scale, attn_scale, w_out, x, g2, g3, tile):
    S, D = x.shape
    C = pool_out.shape[1]

    def body(p_ref, a_ref, ps_ref, as_ref, w_ref, x_ref, g2_ref, g3_ref, mix_ref, x2_ref, h2_ref, h2t_ref):
        n_p, _, n_a, _ = _normalized_heads(p_ref[...], a_ref[...])
        mix = _dot((n_p * ps_ref[...]).astype(BF16), w_ref[:C, :]) + _dot((n_a * as_ref[...]).astype(BF16), w_ref[C:, :])
        mix_ref[...] = mix
        x2 = x_ref[...] + mix * _rms(mix) * g2_ref[...]
        x2_ref[...] = x2
        h2 = (x2 * _rms(x2) * g3_ref[...]).astype(BF16)
        h2_ref[...] = h2
        h2t_ref[...] = h2.T

    row = lambda w: pl.BlockSpec((tile, w), lambda i: (i, 0))
    return pl.pallas_call(
        body, name="fwd_outproj", grid=(S // tile,),
        in_specs=[row(C), row(C), _const((1, C)), _const((1, C)), _const(w_out.shape), row(D), _const((1, D)), _const((1, D))],
        out_specs=[row(D), row(D), row(D), pl.BlockSpec((D, tile), lambda i: (0, i))],
        out_shape=[jax.ShapeDtypeStruct((S, D), F32), jax.ShapeDtypeStruct((S, D), F32), jax.ShapeDtypeStruct((S, D), BF16),
                   jax.ShapeDtypeStruct((D, S), BF16)],
        compiler_params=_params("parallel"),
    )(pool_out, attn_out, pool_scale, attn_scale, w_out, x, g2, g3)


def _conv_taps(tile_rows, halo_rows):
    T = tile_rows.shape[0]
    ext = jnp.concatenate([halo_rows.astype(F32), tile_rows.astype(F32)], axis=0)
    return pltpu.roll(ext, 2, axis=0)[HALO:], pltpu.roll(ext, 1, axis=0)[HALO:], ext[HALO:]


def _tap_rows(cw_ref, d):
    return [cw_ref[d, k:k + 1, :] for k in range(3)]


def _gated_unit(taps_gate, taps_val, cw_gate, cw_val, cb_gate, cb_val):
    gate = cw_gate[0] * taps_gate[0] + cw_gate[1] * taps_gate[1] + cw_gate[2] * taps_gate[2] + cb_gate
    val = cw_val[0] * taps_val[0] + cw_val[1] * taps_val[1] + cw_val[2] * taps_val[2] + cb_val
    sig = 1.0 / (1.0 + jnp.exp(-gate))
    return gate, val, sig


def _fwd_ffn_loss(h2, w_up_g, conv_w_g, conv_b_g, w_down4, x2, target, g4, tile):
    S, D = x2.shape
    nb, _, cs = w_up_g.shape
    half = D_FF_SHARDS

    def body(h_ref, w_ref, cw_ref, cb_ref, wd_ref, x2_ref, t_ref, g4_ref, upre_ref, gv_ref, dy_ref, df_ref, loss_ref, dg4_ref, halo_ref):
        _zero_when(pl.program_id(0) == 0, loss_ref, dg4_ref, halo_ref)
        h = h_ref[...]

        def up(s):
            return _dot(h, w_ref[s]).astype(BF16), _dot(h, w_ref[s + half]).astype(BF16)

        f = jnp.zeros((tile, D), F32)
        ahead = up(0)
        for s in range(half):
            ug, uv = ahead
            if s + 1 < half:
                ahead = up(s + 1)
            upre_ref[s] = ug
            upre_ref[s + half] = uv
            gate, val, sig = _gated_unit(_conv_taps(ug, halo_ref[s]), _conv_taps(uv, halo_ref[s + half]),
                                         _tap_rows(cw_ref, s), _tap_rows(cw_ref, s + half), cb_ref[s], cb_ref[s + half])
            halo_ref[s] = ug[tile - HALO:, :]
            halo_ref[s + half] = uv[tile - HALO:, :]
            gv_ref[s] = gate.astype(BF16)
            gv_ref[s + half] = val.astype(BF16)
            f = f + _dot((gate * sig * val).astype(BF16), wd_ref[s])
        r4 = _rms(f)
        n4 = f * r4
        err = x2_ref[...] + n4 * g4_ref[...] - t_ref[...]
        dy = err * (1.0 / D)
        dy_ref[...] = dy
        df_ref[...] = _norm_bwd(dy * g4_ref[...], n4, r4).astype(BF16)
        loss_ref[...] += _colsum(err * err)
        dg4_ref[...] += _colsum(dy * n4)

    row = lambda w: pl.BlockSpec((tile, w), lambda i: (i, 0))
    return pl.pallas_call(
        body, name="fwd_ffn_loss", grid=(S // tile,),
        in_specs=[row(D), _const(w_up_g.shape), _const(conv_w_g.shape), _const(conv_b_g.shape), _const(w_down4.shape),
                  row(D), row(D), _const((1, D))],
        out_specs=[pl.BlockSpec((nb, tile, cs), lambda i: (0, i, 0)), pl.BlockSpec((nb, tile, cs), lambda i: (0, i, 0)), row(D), row(D),
                   pl.BlockSpec((1, D), lambda i: (0, 0)), pl.BlockSpec((1, D), lambda i: (0, 0))],
        out_shape=[jax.ShapeDtypeStruct((nb, S, cs), BF16), jax.ShapeDtypeStruct((nb, S, cs), BF16),
                   jax.ShapeDtypeStruct((S, D), F32), jax.ShapeDtypeStruct((S, D), BF16),
                   jax.ShapeDtypeStruct((1, D), F32), jax.ShapeDtypeStruct((1, D), F32)],
        scratch_shapes=[pltpu.VMEM((nb, HALO, cs), BF16)],
        compiler_params=_params("arbitrary"),
    )(h2, w_up_g, conv_w_g, conv_b_g, w_down4, x2, target, g4)


def _bwd_down(upre, conv_w_g, conv_b_g, w_down4, df, tile):
    nb, S, cs = upre.shape
    D = df.shape[1]
    n_tiles = S // tile

    def body(ug_ref, uv_ref, hg_ref, hv_ref, cwg_ref, cwv_ref, cbg_ref, cbv_ref, wd_ref, df_ref,
             dg_ref, dv_ref, dwd_ref, dbg_ref, dbv_ref, dcwg_ref, dcwv_ref):
        i = pl.program_id(1)
        first = i == 0
        _zero_when(first, dwd_ref, dbg_ref, dbv_ref, dcwg_ref, dcwv_ref)
        halo_g = jnp.where(first, jnp.zeros_like(hg_ref[0]), hg_ref[0])
        halo_v = jnp.where(first, jnp.zeros_like(hv_ref[0]), hv_ref[0])
        taps_g, taps_v = _conv_taps(ug_ref[0], halo_g), _conv_taps(uv_ref[0], halo_v)
        gate, val, sig = _gated_unit(taps_g, taps_v, _tap_rows(cwg_ref, 0), _tap_rows(cwv_ref, 0), cbg_ref[0], cbv_ref[0])
        silu = gate * sig
        dfb = df_ref[...]
        dact = _dot_nt(dfb, wd_ref[0])
        dwd_ref[0] += _dot_tn((silu * val).astype(BF16), dfb)
        dgate = dact * val * (sig * (1.0 + gate * (1.0 - sig)))
        dval = dact * silu
        dg_ref[0] = dgate.astype(BF16)
        dv_ref[0] = dval.astype(BF16)
        dbg_ref[0] += _colsum(dgate)
        dbv_ref[0] += _colsum(dval)
        for k in range(3):
            dcwg_ref[0, k:k + 1, :] += _colsum(dgate * taps_g[k])
            dcwv_ref[0, k:k + 1, :] += _colsum(dval * taps_v[k])

    half = D_FF_SHARDS
    blk = lambda off: pl.BlockSpec((1, tile, cs), lambda s, i: (s + off, i, 0))
    halo = lambda off: pl.BlockSpec((1, HALO, cs), lambda s, i: (s + off, jnp.maximum(i * (tile // HALO) - 1, 0), 0))
    par = lambda off, r: pl.BlockSpec((1, r, cs), lambda s, i: (s + off, 0, 0))
    outs = pl.pallas_call(
        body, name="bwd_down", grid=(half, n_tiles),
        in_specs=[blk(0), blk(half), halo(0), halo(half), par(0, 3), par(half, 3), par(0, 1), par(half, 1),
                  pl.BlockSpec((1, cs, D), lambda s, i: (s, 0, 0)), pl.BlockSpec((tile, D), lambda s, i: (i, 0))],
        out_specs=[blk(0), blk(0), pl.BlockSpec((1, cs, D), lambda s, i: (s, 0, 0)),
                   par(0, 1), par(0, 1), par(0, 3), par(0, 3)],
        out_shape=[jax.ShapeDtypeStruct((half, S, cs), BF16), jax.ShapeDtypeStruct((half, S, cs), BF16),
                   jax.ShapeDtypeStruct((half, cs, D), F32),
                   jax.ShapeDtypeStruct((half, 1, cs), F32), jax.ShapeDtypeStruct((half, 1, cs), F32),
                   jax.ShapeDtypeStruct((half, 3, cs), F32), jax.ShapeDtypeStruct((half, 3, cs), F32)],
        compiler_params=_params("parallel", "arbitrary"),
    )(upre, upre, upre, upre, conv_w_g, conv_w_g, conv_b_g, conv_b_g, w_down4, df)
    dgate, dval, d_wd, dbg, dbv, dcwg, dcwv = outs
    return dgate, dval, d_wd, jnp.concatenate([dbg, dbv], axis=0), jnp.concatenate([dcwg, dcwv], axis=0)


def _bwd_up_x(dgate, dval, conv_w_g, w_up_g, x2, dy, mix, g2, g3, tile):
    half, S, cs = dgate.shape
    nb = 2 * half
    D = x2.shape[1]
    n_tiles = S // tile

    def body(dg_ref, dv_ref, hg_ref, hv_ref, cw_ref, w_ref, x2_ref, dy_ref, mix_ref, g2_ref, g3_ref,
             dupre_ref, dx2_ref, dmix_ref, dg3_ref, dg2_ref):
        i = pl.program_id(0)
        last = i == n_tiles - 1
        _zero_when(i == 0, dg3_ref, dg2_ref)
        dh2 = jnp.zeros((tile, D), F32)
        for d in range(nb):
            src, halo = (dg_ref, hg_ref) if d < half else (dv_ref, hv_ref)
            nxt = jnp.where(last, jnp.zeros_like(halo[d % half]), halo[d % half])
            ext = jnp.concatenate([src[d % half].astype(F32), nxt.astype(F32)], axis=0)
            n = ext.shape[0]
            cw = _tap_rows(cw_ref, d)
            dupre = (cw[2] * ext + cw[1] * pltpu.roll(ext, n - 1, axis=0) + cw[0] * pltpu.roll(ext, n - 2, axis=0))[:tile]
            dupre = dupre.astype(BF16)
            dupre_ref[d] = dupre
            dh2 = dh2 + _dot_nt(dupre, w_ref[d])
        x2 = x2_ref[...]
        r3 = _rms(x2)
        n3 = x2 * r3
        dg3_ref[...] += _colsum(dh2 * n3)
        dx2 = dy_ref[...] + _norm_bwd(dh2 * g3_ref[...], n3, r3)
        dx2_ref[...] = dx2
        mix = mix_ref[...]
        r2 = _rms(mix)
        n2 = mix * r2
        dg2_ref[...] += _colsum(dx2 * n2)
        dmix_ref[...] = _norm_bwd(dx2 * g2_ref[...], n2, r2).astype(BF16)

    row = lambda w: pl.BlockSpec((tile, w), lambda i: (i, 0))
    blk = pl.BlockSpec((half, tile, cs), lambda i: (0, i, 0))
    last_halo = n_tiles * (tile // HALO) - 1
    halo = pl.BlockSpec((half, HALO, cs), lambda i: (0, jnp.minimum((i + 1) * (tile // HALO), last_halo), 0))
    acc = pl.BlockSpec((1, D), lambda i: (0, 0))
    return pl.pallas_call(
        body, name="bwd_up_x", grid=(n_tiles,),
        in_specs=[blk, blk, halo, halo, _const(conv_w_g.shape), _const(w_up_g.shape), row(D), row(D), row(D),
                  _const((1, D)), _const((1, D))],
        out_specs=[pl.BlockSpec((nb, tile, cs), lambda i: (0, i, 0)), row(D), row(D), acc, acc],
        out_shape=[jax.ShapeDtypeStruct((nb, S, cs), BF16), jax.ShapeDtypeStruct((S, D), F32),
                   jax.ShapeDtypeStruct((S, D), BF16), jax.ShapeDtypeStruct((1, D), F32), jax.ShapeDtypeStruct((1, D), F32)],
        compiler_params=_params("arbitrary"),
    )(dgate, dval, dgate, dval, conv_w_g, w_up_g, x2, dy, mix, g2, g3)


def _bwd_weight(act_t, dout, tile):
    D, S = act_t.shape
    nb, _, cs = dout.shape

    def body(a_ref, d_ref, o_ref):
        _zero_when(pl.program_id(1) == 0, o_ref)
        o_ref[0] += _dot(a_ref[...], d_ref[0])

    return pl.pallas_call(
        body, name="bwd_w_up", grid=(nb, S // tile),
        in_specs=[pl.BlockSpec((D, tile), lambda d, i: (0, i)), pl.BlockSpec((1, tile, cs), lambda d, i: (d, i, 0))],
        out_specs=pl.BlockSpec((1, D, cs), lambda d, i: (d, 0, 0)),
        out_shape=jax.ShapeDtypeStruct((nb, D, cs), F32),
        compiler_params=_params("parallel", "arbitrary"),
    )(act_t, dout)


def _bwd_ffn_blocks(gate_val, upre, conv_w_g, w_down4, df, h2_t, tile):
    nb, S, cs = upre.shape
    D = df.shape[1]
    n_tiles = S // tile
    half = D_FF_SHARDS

    def body(g_ref, v_ref, ug_ref, uv_ref, cwg_ref, cwv_ref, wd_ref, df_ref, ht_ref,
             dug_ref, duv_ref, dwd_ref, dwg_ref, dwv_ref, dbg_ref, dbv_ref, dcwg_ref, dcwv_ref, next_ref):
        _zero_when(pl.program_id(1) == 0, dwd_ref, dwg_ref, dwv_ref, dbg_ref, dbv_ref, dcwg_ref, dcwv_ref, next_ref)
        dfb = df_ref[...]
        dact = _dot_nt(dfb, wd_ref[0])
        gate, val = g_ref[0].astype(F32), v_ref[0].astype(F32)
        sig = 1.0 / (1.0 + jnp.exp(-gate))
        silu = gate * sig
        dwd_ref[0] += _dot_tn((silu * val).astype(BF16), dfb)
        ht = ht_ref[...]

        def through_conv(dup, slot, cw_ref, u_ref, du_ref, dw_ref, db_ref, dcw_ref):
            ext = jnp.concatenate([dup, next_ref[slot]], axis=0)
            n = ext.shape[0]
            shifted = (dup, pltpu.roll(ext, n - 1, axis=0)[:tile], pltpu.roll(ext, n - 2, axis=0)[:tile])
            next_ref[slot] = dup[:HALO]
            cw = _tap_rows(cw_ref, 0)
            dupre = (cw[2] * shifted[0] + cw[1] * shifted[1] + cw[0] * shifted[2]).astype(BF16)
            du_ref[0] = dupre
            dw_ref[0] += _dot(ht, dupre)
            u = u_ref[0].astype(F32)
            db_ref[0] += _colsum(dup)
            for k in range(3):
                dcw_ref[0, k:k + 1, :] += _colsum(shifted[2 - k] * u)

        through_conv(dact * val * (sig * (1.0 + gate * (1.0 - sig))), 0, cwg_ref, ug_ref, dug_ref, dwg_ref, dbg_ref, dcwg_ref)
        through_conv(dact * silu, 1, cwv_ref, uv_ref, duv_ref, dwv_ref, dbv_ref, dcwv_ref)

    rev = lambda i: n_tiles - 1 - i
    blk = lambda off: pl.BlockSpec((1, tile, cs), lambda s, i: (s + off, rev(i), 0))
    par = lambda off, r: pl.BlockSpec((1, r, cs), lambda s, i: (s + off, 0, 0))
    acc = lambda r, c: pl.BlockSpec((1, r, c), lambda s, i: (s, 0, 0))
    outs = pl.pallas_call(
        body, name="bwd_ffn_blocks", grid=(half, n_tiles),
        in_specs=[blk(0), blk(half), blk(0), blk(half), par(0, 3), par(half, 3),
                  acc(cs, D), pl.BlockSpec((tile, D), lambda s, i: (rev(i), 0)), pl.BlockSpec((D, tile), lambda s, i: (0, rev(i)))],
        out_specs=[blk(0), blk(0), acc(cs, D), acc(D, cs), acc(D, cs), acc(1, cs), acc(1, cs), acc(3, cs), acc(3, cs)],
        out_shape=[jax.ShapeDtypeStruct((half, S, cs), BF16), jax.ShapeDtypeStruct((half, S, cs), BF16),
                   jax.ShapeDtypeStruct((half, cs, D), F32),
                   jax.ShapeDtypeStruct((half, D, cs), F32), jax.ShapeDtypeStruct((half, D, cs), F32),
                   jax.ShapeDtypeStruct((half, 1, cs), F32), jax.ShapeDtypeStruct((half, 1, cs), F32),
                   jax.ShapeDtypeStruct((half, 3, cs), F32), jax.ShapeDtypeStruct((half, 3, cs), F32)],
        scratch_shapes=[pltpu.VMEM((2, HALO, cs), F32)],
        compiler_params=_params("arbitrary", "arbitrary"),
    )(gate_val, gate_val, upre, upre, conv_w_g, conv_w_g, w_down4, df, h2_t)
    dupre_g, dupre_v, d_wd, d_wg, d_wv, dbg, dbv, dcwg, dcwv = outs
    return (dupre_g, dupre_v, d_wd, jnp.concatenate([d_wg, d_wv], axis=0), jnp.concatenate([dbg, dbv], axis=0),
            jnp.concatenate([dcwg, dcwv], axis=0))


def _bwd_ffn_tokens(dupre_g, dupre_v, w_up_g, x2, dy, mix, g2, g3, tile):
    half, S, cs = dupre_g.shape
    D = x2.shape[1]

    def body(dg_ref, dv_ref, w_ref, x2_ref, dy_ref, mix_ref, g2_ref, g3_ref, dx2_ref, dmix_ref, dg3_ref, dg2_ref):
        _zero_when(pl.program_id(0) == 0, dg3_ref, dg2_ref)
        parts = [_dot_nt(dg_ref[d], w_ref[d]) for d in range(half)] + [_dot_nt(dv_ref[d], w_ref[d + half]) for d in range(half)]
        while len(parts) > 1:
            parts = [a + b for a, b in zip(parts[::2], parts[1::2])]
        dh2 = parts[0]
        x2 = x2_ref[...]
        r3 = _rms(x2)
        n3 = x2 * r3
        dg3_ref[...] += _colsum(dh2 * n3)
        dx2 = dy_ref[...] + _norm_bwd(dh2 * g3_ref[...], n3, r3)
        dx2_ref[...] = dx2
        mix = mix_ref[...]
        r2 = _rms(mix)
        n2 = mix * r2
        dg2_ref[...] += _colsum(dx2 * n2)
        dmix_ref[...] = _norm_bwd(dx2 * g2_ref[...], n2, r2).astype(BF16)

    row = lambda w: pl.BlockSpec((tile, w), lambda i: (i, 0))
    blk = pl.BlockSpec((half, tile, cs), lambda i: (0, i, 0))
    acc = pl.BlockSpec((1, D), lambda i: (0, 0))
    return pl.pallas_call(
        body, name="bwd_ffn_tokens", grid=(S // tile,),
        in_specs=[blk, blk, _const(w_up_g.shape), row(D), row(D), row(D), _const((1, D)), _const((1, D))],
        out_specs=[row(D), row(D), acc, acc],
        out_shape=[jax.ShapeDtypeStruct((S, D), F32), jax.ShapeDtypeStruct((S, D), BF16),
                   jax.ShapeDtypeStruct((1, D), F32), jax.ShapeDtypeStruct((1, D), F32)],
        compiler_params=_params("arbitrary"),
    )(dupre_g, dupre_v, w_up_g, x2, dy, mix, g2, g3)


def _bwd_outproj(dmix, w_out, pool_out, attn_out, pool_scale, attn_scale, tile):
    S, D = dmix.shape
    C = pool_out.shape[1]

    def body(dm_ref, w_ref, p_ref, a_ref, ps_ref, as_ref, dp_ref, da_ref, dw_ref, dps_ref, das_ref):
        _zero_when(pl.program_id(0) == 0, dw_ref, dps_ref, das_ref)
        dmx = dm_ref[...]
        dmerged = _dot_nt(dmx, w_ref[...])
        n_p, r_p, n_a, r_a = _normalized_heads(p_ref[...], a_ref[...])
        merged = jnp.concatenate([(n_p * ps_ref[...]).astype(BF16), (n_a * as_ref[...]).astype(BF16)], axis=1)
        dw_ref[...] += _dot_tn(merged, dmx)
        dm_p, dm_a = dmerged[:, :C], dmerged[:, C:]
        dps_ref[...] += _colsum(dm_p * n_p)
        das_ref[...] += _colsum(dm_a * n_a)
        dp_ref[...] = _norm_bwd(dm_p * ps_ref[...], n_p, r_p)
        da_ref[...] = _norm_bwd(dm_a * as_ref[...], n_a, r_a)

    row = lambda w: pl.BlockSpec((tile, w), lambda i: (i, 0))
    return pl.pallas_call(
        body, name="bwd_outproj", grid=(S // tile,),
        in_specs=[row(D), _const(w_out.shape), row(C), row(C), _const((1, C)), _const((1, C))],
        out_specs=[row(C), row(C), pl.BlockSpec(w_out.shape, lambda i: (0, 0)),
                   pl.BlockSpec((1, C), lambda i: (0, 0)), pl.BlockSpec((1, C), lambda i: (0, 0))],
        out_shape=[jax.ShapeDtypeStruct((S, C), F32), jax.ShapeDtypeStruct((S, C), F32),
                   jax.ShapeDtypeStruct(w_out.shape, F32), jax.ShapeDtypeStruct((1, C), F32), jax.ShapeDtypeStruct((1, C), F32)],
        compiler_params=_params("arbitrary"),
    )(dmix, w_out, pool_out, attn_out, pool_scale, attn_scale)


def _bwd_attn(qkv, d_attn, n_pairs, ex, subs):
    S = qkv.shape[0]
    n_steps = S // (subs * QB)

    def body(q_ref, k_ref, v_ref, do_ref, *rest):
        dq_ref, dk_ref, dv_ref = rest[ex.n:ex.n + 3]
        ex_refs = ex.split(rest[:ex.n] + rest[ex.n + 3:])
        first_step, last_step = _grid_ends((n_pairs, n_steps))

        @pl.when(first_step)
        def _():
            ex.start(*ex_refs)

        @pl.when(pl.program_id(1) == 0)
        def _():
            dk_ref[...] = jnp.zeros_like(dk_ref)
            dv_ref[...] = jnp.zeros_like(dv_ref)

        low_lanes = _low_lanes()
        after_s, from_s = _triangle(False, LOG_PIECES), _triangle(True, GRAD_PIECES)
        zero = jnp.zeros((QB, 1), F32)

        def tiles(qhs, dohs, totals, kws, vws, masks, cs, gs, scores=None):
            fw = _attn_weights(scores or _attn_scores(qhs, kws, masks), masks, cs, after_s)
            gvals = [t[2] * _dot_nt(doh, vw) for t, doh, vw in zip(fw, dohs, vws)]
            sums = [_suffix_sums(g, from_s, g0) for g, g0 in zip(gvals, gs)]
            totals = [tot if m is None else tot + sm[1] for tot, m, sm in zip(totals, masks, sums)]
            dzs = []
            for (z, e, _, _), g, (nearer, _), tot, m in zip(fw, gvals, sums, totals, masks):
                inv = 1.0 / (1.0 + e)
                sig_abs, sig_neg = inv, e * inv
                pos = z >= 0.0
                dz = g * jnp.where(pos, sig_neg, sig_abs) - jnp.where(pos, sig_abs, sig_neg) * (tot - nearer)
                if m is not None:
                    dz = jnp.where(m, dz, 0.0)
                dzs.append((dz * ATTN_SCALE).astype(BF16))
            dqs = [_dot(dz, kw) for dz, kw in zip(dzs, kws)]
            dks = [_dot_tn(dz, qh) for dz, qh in zip(dzs, qhs)]
            dvs = [_dot_tn(t[2].astype(BF16), doh) for t, doh in zip(fw, dohs)]
            return [(dq, dk, dv, t[3], sm[1], tot) for dq, dk, dv, t, sm, tot in zip(dqs, dks, dvs, fw, sums, totals)]

        def cond(c):
            return jnp.logical_and(c[0] >= 0, c[1] == 0)

        qhs, dohs, kws, vws, masks, first_blks, starts = [], [], [], [], [], [], []
        for sub in range(subs):
            i = pl.program_id(1) * subs + sub
            rows = slice(sub * QB, (sub + 1) * QB)
            first_blk, start, offset = _first_window(i)
            first_blks.append(first_blk)
            starts.append(start)
            qhs += _split_heads(q_ref[rows, :].astype(F32), low_lanes)
            dohs += _split_heads(do_ref[rows, :], low_lanes)
            kws += [k_ref[pl.ds(start, 2 * QB), :]] * 2
            vws += [v_ref[pl.ds(start, 2 * QB), :]] * 2
            masks += [_causal_mask(2 * QB, offset)] * 2
        zeros = [zero] * len(qhs)

        scores = _attn_scores(qhs, kws, masks)
        c_first = [_row_sums(sc[3], zero) for sc in scores]
        beyond_first = []
        for sub in range(subs):
            pair = slice(2 * sub, 2 * sub + 2)

            def far_sums(c, qh=qhs[pair], doh=dohs[pair]):
                j, _, c0, c1, r0, r1 = c
                at = pl.multiple_of(j * QB, QB)
                kb = k_ref[pl.ds(at, QB), :]
                vb = v_ref[pl.ds(at, QB), :]
                far = _attn_tiles(qh, [kb, kb], [None, None], [c0, c1], after_s)
                r0 = r0 + jnp.sum(far[0][2] * _dot_nt(doh[0], vb), axis=1, keepdims=True)
                r1 = r1 + jnp.sum(far[1][2] * _dot_nt(doh[1], vb), axis=1, keepdims=True)
                return j - 1, _sweep_done(far[0][3], far[1][3]), far[0][3], far[1][3], r0, r1

            c0, c1 = c_first[pair]
            far = lax.while_loop(cond, far_sums, (first_blks[sub] - 1, _sweep_done(c0, c1), c0, c1, zero, zero))
            beyond_first += [far[4], far[5]]

        done = tiles(qhs, dohs, beyond_first, kws, vws, masks, zeros, zeros, scores)
        for sub in range(subs):
            dk_ref[pl.ds(starts[sub], 2 * QB), :] += done[2 * sub][1] + done[2 * sub + 1][1]
            dv_ref[pl.ds(starts[sub], 2 * QB), :] += done[2 * sub][2] + done[2 * sub + 1][2]

        for sub in range(subs):
            pair = slice(2 * sub, 2 * sub + 2)
            t0, t1 = done[pair]

            def step(c, qh=qhs[pair], doh=dohs[pair], total=[t0[5], t1[5]]):
                j, _, dq, c0, c1, s0, s1 = c
                at = pl.multiple_of(j * QB, QB)
                kb = k_ref[pl.ds(at, QB), :]
                vb = v_ref[pl.ds(at, QB), :]
                f0, f1 = tiles(qh, doh, total, [kb, kb], [vb, vb], [None, None], [c0, c1], [s0, s1])
                dk_ref[pl.ds(at, QB), :] += f0[1] + f1[1]
                dv_ref[pl.ds(at, QB), :] += f0[2] + f1[2]
                return j - 1, _sweep_done(f0[3], f1[3]), dq + jnp.where(low_lanes, f0[0], f1[0]), f0[3], f1[3], f0[4], f1[4]

            init = (first_blks[sub] - 1, _sweep_done(t0[3], t1[3]), jnp.where(low_lanes, t0[0], t1[0]), t0[3], t1[3], t0[4], t1[4])
            dq_ref[sub * QB:(sub + 1) * QB, :] = lax.while_loop(cond, step, init)[2]

        @pl.when(last_step)
        def _():
            ex.wait(*ex_refs)

    blk = pl.BlockSpec((subs * QB, QB), lambda p, i: (i, p))
    full = lambda off: pl.BlockSpec((S, QB), lambda p, i: (0, off + p), pipeline_mode=pl.Buffered(1))
    outs = pl.pallas_call(
        body, name="bwd_attn", grid=(n_pairs, n_steps),
        in_specs=[blk, full(n_pairs), full(2 * n_pairs), blk] + ex.specs,
        out_specs=[blk, pl.BlockSpec((S, QB), lambda p, i: (0, p)), pl.BlockSpec((S, QB), lambda p, i: (0, p))] + ex.specs,
        out_shape=[jax.ShapeDtypeStruct((S, n_pairs * QB), F32)] * 3 + ex.out_shape,
        scratch_shapes=ex.scratch,
        compiler_params=_params("arbitrary", "arbitrary"),
    )(qkv, qkv, qkv, d_attn, *ex.arrays)
    return outs[0], outs[1], outs[2], outs[3:]


def _bwd_pool(u, d_pool, w_pool, tile):
    S, C = u.shape
    n_tiles = S // tile
    ng = len(POOL_WINDOWS)

    def body(u_ref, uh_ref, d_ref, dh_ref, wp_ref, du_ref, dwp_ref):
        i = pl.program_id(0)
        first = i == 0
        _zero_when(first, dwp_ref)
        halo = jnp.where(first, 0.0, uh_ref[...])
        parts = _pool_deviation(u_ref[...], halo, i * tile)
        dout = d_ref[...]
        nxt = jnp.where(i == n_tiles - 1, 0.0, dh_ref[...])
        dext = jnp.concatenate([dout, nxt], axis=0).astype(BF16)
        counts = _pool_counts(i * tile, tile + HALO)
        dps, scaled = [], []
        for g in range(ng):
            lanes = slice(g * POOL_GROUP, (g + 1) * POOL_GROUP)
            dp = _dot_nt(dext[:, lanes], wp_ref[g].astype(BF16))
            dps.append(dp[:tile])
            scaled.append(dp / counts[g])
        sums = _window_sums(jnp.concatenate(scaled, axis=1), forward=True)
        for g, w in enumerate(POOL_WINDOWS):
            lanes = slice(g * POOL_GROUP, (g + 1) * POOL_GROUP)
            du_ref[:, lanes] = sums[w][:tile, lanes] - dps[g]
            dwp_ref[g] += _dot_tn(parts[g].astype(BF16), dext[:tile, lanes])

    row = pl.BlockSpec((tile, C), lambda i: (i, 0))
    return pl.pallas_call(
        body, name="bwd_pool", grid=(n_tiles,),
        in_specs=[row, _prev_halo_spec(tile, C), row, _next_halo_spec(tile, C, n_tiles), _const(w_pool.shape)],
        out_specs=[row, pl.BlockSpec(w_pool.shape, lambda i: (0, 0, 0))],
        out_shape=[jax.ShapeDtypeStruct((S, C), F32), jax.ShapeDtypeStruct(w_pool.shape, F32)],
        compiler_params=_params("arbitrary"),
    )(u, u, d_pool, d_pool, w_pool)


def _bwd_w_in(du, dq, dk, dv, h1_t, n_blocks, tile):
    D, S = h1_t.shape
    C = du.shape[1]
    cs = 4 * C // n_blocks
    per = C // cs

    def body(du_ref, dq_ref, dk_ref, dv_ref, ht_ref, dproj_ref, dw_ref):
        _zero_when(pl.program_id(0) == 0, dw_ref)
        ht = ht_ref[...]
        for d in range(n_blocks):
            src = (du_ref, dq_ref, dk_ref, dv_ref)[d // per]
            dproj = src[:, (d % per) * cs:(d % per + 1) * cs].astype(BF16)
            dproj_ref[:, d * cs:(d + 1) * cs] = dproj
            dw_ref[d] += _dot(ht, dproj)

    row = lambda w: pl.BlockSpec((tile, w), lambda i: (i, 0))
    return pl.pallas_call(
        body, name="bwd_w_in", grid=(S // tile,),
        in_specs=[row(C), row(C), row(C), row(C), pl.BlockSpec((D, tile), lambda i: (0, i))],
        out_specs=[row(4 * C), pl.BlockSpec((n_blocks, D, cs), lambda i: (0, 0, 0))],
        out_shape=[jax.ShapeDtypeStruct((S, 4 * C), BF16), jax.ShapeDtypeStruct((n_blocks, D, cs), F32)],
        compiler_params=_params("arbitrary"),
    )(du, dq, dk, dv, h1_t)


def _bwd_x(dproj, w_in_full, x, dx2, g1, tile, ex):
    S, D = x.shape
    n_tiles = S // tile

    def body(dp_ref, w_ref, x_ref, dx2_ref, g_ref, *rest):
        dx_ref, dg_ref = rest[ex.n:ex.n + 2]
        ex_refs = ex.split(rest[:ex.n] + rest[ex.n + 2:])
        first, last = _grid_ends((n_tiles,))

        @pl.when(first)
        def _():
            ex.start(*ex_refs)
            dg_ref[...] = jnp.zeros_like(dg_ref)

        dh = _dot_nt(dp_ref[...], w_ref[...])
        xf = x_ref[...]
        r1 = _rms(xf)
        n1 = xf * r1
        dg_ref[...] += _colsum(dh * n1)
        dx_ref[...] = dx2_ref[...] + _norm_bwd(dh * g_ref[...], n1, r1)

        @pl.when(last)
        def _():
            ex.wait(*ex_refs)

    row = lambda w: pl.BlockSpec((tile, w), lambda i: (i, 0))
    outs = pl.pallas_call(
        body, name="bwd_x", grid=(n_tiles,),
        in_specs=[row(w_in_full.shape[1]), _const(w_in_full.shape), row(D), row(D), _const((1, D))] + ex.specs,
        out_specs=[row(D), pl.BlockSpec((1, D), lambda i: (0, 0))] + ex.specs,
        out_shape=[jax.ShapeDtypeStruct((S, D), F32), jax.ShapeDtypeStruct((1, D), F32)] + ex.out_shape,
        scratch_shapes=ex.scratch,
        compiler_params=_params("arbitrary"),
    )(dproj, w_in_full, x, dx2, g1, *ex.arrays)
    return outs[0], outs[1], outs[2:]


def _mesh_position():
    x, y, c = lax.axis_index("x"), lax.axis_index("y"), lax.axis_index("c")
    return x, y, c, 4 * x + 2 * y + c


def _peer(x, y, c, k):
    px = 1 - x if k & 4 else x
    py = 1 - y if k & 2 else y
    pc = 1 - c if k & 1 else c
    return (px, py, pc), 4 * px + 2 * py + pc


class _Exchange:
    def __init__(self, arrays, gather):
        self.arrays, self.gather, self.n = list(arrays), gather, len(arrays)
        self.out_shape = [jax.ShapeDtypeStruct(((N_DEV,) + a.shape) if gather else a.shape, a.dtype) for a in arrays]
        self.specs = [pl.BlockSpec(memory_space=pl.ANY)] * self.n
        copies = self.n * (N_DEV - 1)
        self.scratch = [pltpu.SemaphoreType.DMA((copies,)), pltpu.SemaphoreType.DMA((copies,)),
                        pltpu.SemaphoreType.DMA((self.n,))]

    def _copies(self, ins, outs, sems):
        send_sems, recv_sems, local_sems = sems
        x, y, c, me = _mesh_position()
        local, remote = [], []
        for a in range(self.n):
            mine = ins[a] if self.gather else ins[a].at[me]
            local.append(pltpu.make_async_copy(mine, outs[a].at[me], local_sems.at[a]))
            for k in range(1, N_DEV):
                peer, peer_idx = _peer(x, y, c, k)
                src = ins[a] if self.gather else ins[a].at[peer_idx]
                sem = a * (N_DEV - 1) + k - 1
                remote.append(pltpu.make_async_remote_copy(
                    src_ref=src, dst_ref=outs[a].at[me], send_sem=send_sems.at[sem], recv_sem=recv_sems.at[sem],
                    device_id=peer, device_id_type=MESH))
        return local, remote

    def start(self, ins, outs, sems):
        local, remote = self._copies(ins, outs, sems)
        for cp in local + remote:
            cp.start()

    def wait(self, ins, outs, sems):
        local, remote = self._copies(ins, outs, sems)
        for cp in remote:
            cp.wait_send()
        for cp in remote:
            cp.wait_recv()
        for cp in local:
            cp.wait()

    def split(self, refs):
        return refs[:self.n], refs[self.n:2 * self.n], refs[2 * self.n:]


def _all_to_all(arrays, gather, name):
    ex = _Exchange(arrays, gather)

    def body(*refs):
        ins, outs, sems = ex.split(refs)
        ex.start(ins, outs, sems)
        ex.wait(ins, outs, sems)

    return pl.pallas_call(body, name=name, in_specs=ex.specs, out_specs=ex.specs, out_shape=ex.out_shape,
                          scratch_shapes=ex.scratch)(*ex.arrays)


def _reduce_adamw(parts, w, m, v, rows):
    R, C = w.shape

    def body(p_ref, w_ref, m_ref, v_ref, g_ref, d_ref, nm_ref, nv_ref):
        g = p_ref[0].astype(F32)
        for s in range(1, N_DEV):
            g = g + p_ref[s].astype(F32)
        g_ref[...] = g
        m_new = ADAM_B1 * m_ref[...] + (1.0 - ADAM_B1) * g
        v_new = ADAM_B2 * v_ref[...] + (1.0 - ADAM_B2) * (g * g)
        m_hat = m_new / (1.0 - ADAM_B1 ** ADAM_STEP)
        v_hat = v_new / (1.0 - ADAM_B2 ** ADAM_STEP)
        d_ref[...] = -ADAM_LR * (m_hat / (jnp.sqrt(v_hat) + ADAM_EPS) + ADAM_WD * w_ref[...])
        nm_ref[...] = m_new
        nv_ref[...] = v_new

    row = pl.BlockSpec((rows, C), lambda i: (i, 0))
    return pl.pallas_call(
        body, name="reduce_adamw", grid=(R // rows,),
        in_specs=[pl.BlockSpec((N_DEV, rows, C), lambda i: (0, i, 0)), row, row, row],
        out_specs=[row] * 4, out_shape=[jax.ShapeDtypeStruct((R, C), F32)] * 4,
        compiler_params=_params("parallel"),
    )(parts, w, m, v)


def _row_tile(rows, cols):
    fits = [t for t in range(8, rows + 1, 8) if rows % t == 0 and N_DEV * t * cols * 4 <= 4 * 1024 * 1024]
    return max(fits) if fits else rows


SMALL_COLS = 1024


def _pack_small(vals):
    rows = []
    for a in vals:
        flat = a.reshape(-1)
        pad = (-flat.shape[0]) % SMALL_COLS
        rows.append(jnp.pad(flat, (0, pad)).reshape(-1, SMALL_COLS))
    packed = jnp.concatenate(rows, axis=0)
    return jnp.pad(packed, ((0, (-packed.shape[0]) % 8), (0, 0)))


def _unpack_small(packed, like):
    out, r = [], 0
    for a in like:
        n = a.size
        nr = -(-n // SMALL_COLS)
        out.append(packed[r:r + nr].reshape(-1)[:n].reshape(a.shape))
        r += nr
    return out


def kernel(x, norm_mix_pre, w_in, w_pool, pool_scale, attn_scale, w_out, norm_mix_post, norm_ffn_pre, w_up, conv_w, conv_b, w_down, norm_ffn_post, loss_target, m_norm_mix_pre, m_w_in, m_w_pool, m_pool_scale, m_attn_scale, m_w_out, m_norm_mix_post, m_norm_ffn_pre, m_w_up, m_conv_w, m_conv_b, m_w_down, m_norm_ffn_post, v_norm_mix_pre, v_w_in, v_w_pool, v_pool_scale, v_attn_scale, v_w_out, v_norm_mix_post, v_norm_ffn_pre, v_w_up, v_conv_w, v_conv_b, v_w_down, v_norm_ffn_post):
    S, D = x.shape[1], x.shape[2]
    d_ff_block = w_up.shape[2]

    xs, target = x[0], loss_target[0]
    g1, g2, g3, g4 = norm_mix_pre, norm_mix_post, norm_ffn_pre, norm_ffn_post
    big = min(512, S)
    small = min(256, S)
    n_pairs = pool_scale.shape[1] // QB
    conv_b_g = conv_b.reshape(N_DEV, 1, d_ff_block)

    (w_in_g,) = _all_to_all([w_in[0].astype(BF16)], gather=True, name="gather_w_in")
    h1_t, u, qkv = _fwd_inproj(xs, g1, w_in_g, big)
    pool_out = _fwd_pool(u, w_pool[0], big)
    attn_out, (w_out_g, w_up_g, w_down_g, conv_w_g) = _fwd_attn(
        qkv, n_pairs, _Exchange([w_out[0].astype(BF16), w_up[0].astype(BF16), w_down[0].astype(BF16), conv_w[0]], gather=True),
        min(ATTN_FWD_BLOCKS, S // QB))
    w_out_full = w_out_g.reshape(D, D)
    w_down4 = w_down_g.reshape(D_FF_SHARDS, d_ff_block, D)
    mix, x2, h2, h2_t = _fwd_outproj(pool_out, attn_out, pool_scale, attn_scale, w_out_full, xs, g2, g3, big)
    upre, gate_val, dy, df, loss_cols, dg4 = _fwd_ffn_loss(h2, w_up_g, conv_w_g, conv_b_g, w_down4, x2, target, g4, small)
    loss = lax.psum(0.5 * jnp.sum(loss_cols) / D, ("x", "y", "c"))

    dupre_g, dupre_v, d_wd4, d_wup, d_cb, d_cw = _bwd_ffn_blocks(gate_val, upre, conv_w_g, w_down4, df, h2_t, big)
    dx2, dmix, dg3, dg2 = _bwd_ffn_tokens(dupre_g, dupre_v, w_up_g, x2, dy, mix, g2, g3, big)
    d_pool, d_attn, d_wout, d_ps, d_as = _bwd_outproj(dmix, w_out_full, pool_out, attn_out, pool_scale, attn_scale, big)
    d_wdown_g = d_wd4.reshape(N_DEV, w_down.shape[1], D)
    d_wout_g = d_wout.reshape(N_DEV, D // N_DEV, D)
    dq, dk, dv, late_parts = _bwd_attn(qkv, d_attn, n_pairs, _Exchange([d_wout_g, d_wup, d_wdown_g, d_cw], gather=False),
                                       min(ATTN_BWD_BLOCKS, S // QB))
    du, d_wp = _bwd_pool(u, d_pool, w_pool[0], big)
    dproj, d_win = _bwd_w_in(du, dq, dk, dv, h1_t, N_DEV, big)
    w_in_full = w_in_g.transpose(1, 0, 2).reshape(D, -1)
    dx, dg1, (win_parts,) = _bwd_x(dproj, w_in_full, xs, dx2, g1, big, _Exchange([d_win], gather=False))
    big_parts = [win_parts] + list(late_parts)
    r = dict(dx=dx, g1=dg1, w_pool=d_wp, pool_scale=d_ps, attn_scale=d_as, g2=dg2, g3=dg3, conv_b=d_cb, g4=dg4)

    small_names = ["norm_mix_pre", "w_pool", "pool_scale", "attn_scale", "norm_mix_post", "norm_ffn_pre", "conv_b", "norm_ffn_post"]
    small_w = dict(norm_mix_pre=norm_mix_pre, w_pool=w_pool, pool_scale=pool_scale, attn_scale=attn_scale,
                   norm_mix_post=norm_mix_post, norm_ffn_pre=norm_ffn_pre, conv_b=conv_b, norm_ffn_post=norm_ffn_post)
    small_m = dict(norm_mix_pre=m_norm_mix_pre, w_pool=m_w_pool, pool_scale=m_pool_scale, attn_scale=m_attn_scale,
                   norm_mix_post=m_norm_mix_post, norm_ffn_pre=m_norm_ffn_pre, conv_b=m_conv_b, norm_ffn_post=m_norm_ffn_post)
    small_v = dict(norm_mix_pre=v_norm_mix_pre, w_pool=v_w_pool, pool_scale=v_pool_scale, attn_scale=v_attn_scale,
                   norm_mix_post=v_norm_mix_post, norm_ffn_pre=v_norm_ffn_pre, conv_b=v_conv_b, norm_ffn_post=v_norm_ffn_post)
    small_g = dict(norm_mix_pre=r["g1"], w_pool=r["w_pool"], pool_scale=r["pool_scale"], attn_scale=r["attn_scale"],
                   norm_mix_post=r["g2"], norm_ffn_pre=r["g3"], conv_b=r["conv_b"], norm_ffn_post=r["g4"])
    like = [small_w[n] for n in small_names]
    packed_g = _pack_small([small_g[n] for n in small_names])

    (small_parts,) = _all_to_all([packed_g], gather=True, name="gather_small_grads")

    def update(parts, w, m, v):
        R, C = w.shape
        return _reduce_adamw(parts, w, m, v, _row_tile(R, C))

    res = {}
    res["w_in"] = update(big_parts[0], w_in[0], m_w_in[0], v_w_in[0])
    res["w_out"] = update(big_parts[1], w_out[0], m_w_out[0], v_w_out[0])
    res["w_up"] = update(big_parts[2], w_up[0], m_w_up[0], v_w_up[0])
    res["w_down"] = update(big_parts[3], w_down[0], m_w_down[0], v_w_down[0])
    res["conv_w"] = update(big_parts[4], conv_w[0], m_conv_w[0], v_conv_w[0])
    small_res = update(small_parts, _pack_small(like), _pack_small([small_m[n] for n in small_names]),
                       _pack_small([small_v[n] for n in small_names]))
    small_res = [_unpack_small(t, like) for t in small_res]
    for idx, n in enumerate(small_names):
        res[n] = tuple(t[idx] for t in small_res)

    order = ["norm_mix_pre", "w_in", "w_pool", "pool_scale", "attn_scale", "w_out", "norm_mix_post", "norm_ffn_pre",
             "w_up", "conv_w", "conv_b", "w_down", "norm_ffn_post"]
    shaped = {n: tuple(t.reshape(s.shape) for t in res[n])
              for n, s in dict(norm_mix_pre=norm_mix_pre, w_in=w_in, w_pool=w_pool, pool_scale=pool_scale, attn_scale=attn_scale,
                               w_out=w_out, norm_mix_post=norm_mix_post, norm_ffn_pre=norm_ffn_pre, w_up=w_up, conv_w=conv_w,
                               conv_b=conv_b, w_down=w_down, norm_ffn_post=norm_ffn_post).items()}
    outs = [loss, r["dx"].reshape(x.shape)]
    for k in range(4):
        outs += [shaped[n][k] for n in order]
    return tuple(outs)
```

```python
import functools

import jax
import jax.numpy as jnp
from jax import lax
from jax.experimental import pallas as pl
from jax.experimental.pallas import tpu as pltpu

F32 = jnp.float32
BF16 = jnp.bfloat16
HIGHEST = lax.Precision.HIGHEST

N_DEV = 8
EPS = 1e-6
POOL_WINDOWS = (2, 4, 8, 16)
POOL_GROUP = 128
HALO = 16
HEAD_DIM = 64
QB = 128
ATTN_SCALE = HEAD_DIM ** -0.5
ATTN_FWD_BLOCKS = 8
ATTN_BWD_BLOCKS = 4
EXP_UNDERFLOW = -88.0
D_FF_SHARDS = 4

ADAM_LR = 0.001
ADAM_B1 = 0.9
ADAM_B2 = 0.999
ADAM_EPS = 1e-08
ADAM_WD = 0.01
ADAM_STEP = 10

VMEM_LIMIT_V7X = 56 * 1024 * 1024
MESH = pl.DeviceIdType.MESH


def _params(*semantics):
    return pltpu.CompilerParams(dimension_semantics=semantics, vmem_limit_bytes=VMEM_LIMIT_V7X)


def _const(shape):
    zeros = (0,) * len(shape)
    return pl.BlockSpec(shape, lambda *_: zeros, pipeline_mode=pl.Buffered(1))


def _dot(a, b):
    return jnp.dot(a, b, preferred_element_type=F32)


def _dot_nt(a, b):
    return lax.dot_general(a, b, (((1,), (1,)), ((), ())), preferred_element_type=F32)


def _dot_tn(a, b):
    return lax.dot_general(a, b, (((0,), (0,)), ((), ())), preferred_element_type=F32)


def _rms(v):
    return lax.rsqrt(jnp.mean(v * v, axis=-1, keepdims=True) + EPS)


def _norm_bwd(dn_times_gain, n, r):
    return r * (dn_times_gain - n * jnp.mean(dn_times_gain * n, axis=-1, keepdims=True))


def _zero_when(first, *refs):
    @pl.when(first)
    def _():
        for ref in refs:
            ref[...] = jnp.zeros_like(ref)


def _colsum(v):
    return jnp.sum(v, axis=0, keepdims=True)


def _grid_ends(grid):
    ids = [pl.program_id(a) for a in range(len(grid))]
    first = functools.reduce(jnp.logical_and, [i == 0 for i in ids])
    last = functools.reduce(jnp.logical_and, [i == n - 1 for i, n in zip(ids, grid)])
    return first, last


def _fwd_inproj(x, g1, w_in_g, tile):
    S, D = x.shape
    nb, _, cs = w_in_g.shape
    d_pool = 2 * cs

    def body(x_ref, g_ref, w_ref, ht_ref, u_ref, qkv_ref):
        xf = x_ref[...]
        h = (xf * _rms(xf) * g_ref[...]).astype(BF16)
        ht_ref[...] = h.T
        for d in range(nb):
            o = _dot(h, w_ref[d])
            if d < 2:
                u_ref[:, d * cs:(d + 1) * cs] = o
            else:
                qkv_ref[:, (d - 2) * cs:(d - 1) * cs] = o.astype(BF16)

    return pl.pallas_call(
        body, name="fwd_inproj", grid=(S // tile,),
        in_specs=[pl.BlockSpec((tile, D), lambda i: (i, 0)), _const((1, D)), _const(w_in_g.shape)],
        out_specs=[pl.BlockSpec((D, tile), lambda i: (0, i)), pl.BlockSpec((tile, d_pool), lambda i: (i, 0)),
                   pl.BlockSpec((tile, 3 * d_pool), lambda i: (i, 0))],
        out_shape=[jax.ShapeDtypeStruct((D, S), BF16), jax.ShapeDtypeStruct((S, d_pool), F32),
                   jax.ShapeDtypeStruct((S, 3 * d_pool), BF16)],
        compiler_params=_params("parallel"),
    )(x, g1, w_in_g)


def _window_sums(ext, forward):
    n = ext.shape[0]
    sums, s, sh = {}, ext, 1
    while sh < POOL_WINDOWS[-1]:
        s = s + pltpu.roll(s, (n - sh) if forward else sh, axis=0)
        sh *= 2
        sums[sh] = s
    return sums


def _pool_counts(t0, rows):
    t1 = (lax.broadcasted_iota(jnp.int32, (rows, 1), 0) + t0 + 1).astype(F32)
    return [jnp.minimum(t1, float(w)) for w in POOL_WINDOWS]


def _pool_deviation(u, halo, t0):
    T = u.shape[0]
    sums = _window_sums(jnp.concatenate([halo, u], axis=0), forward=False)
    counts = _pool_counts(t0, T)
    parts = []
    for g, w in enumerate(POOL_WINDOWS):
        lanes = slice(g * POOL_GROUP, (g + 1) * POOL_GROUP)
        parts.append(sums[w][HALO:, lanes] / counts[g] - u[:, lanes])
    return parts


def _prev_halo_spec(tile, width):
    return pl.BlockSpec((HALO, width), lambda i: (jnp.maximum(i * (tile // HALO) - 1, 0), 0))


def _next_halo_spec(tile, width, n_tiles):
    last = n_tiles * (tile // HALO) - 1
    return pl.BlockSpec((HALO, width), lambda i: (jnp.minimum((i + 1) * (tile // HALO), last), 0))


def _fwd_pool(u, w_pool, tile):
    S, C = u.shape

    def body(u_ref, halo_ref, wp_ref, o_ref):
        i = pl.program_id(0)
        halo = jnp.where(i > 0, halo_ref[...], 0.0)
        parts = _pool_deviation(u_ref[...], halo, i * tile)
        for g, p in enumerate(parts):
            o_ref[:, g * POOL_GROUP:(g + 1) * POOL_GROUP] = _dot(p.astype(BF16), wp_ref[g].astype(BF16))

    return pl.pallas_call(
        body, name="fwd_pool", grid=(S // tile,),
        in_specs=[pl.BlockSpec((tile, C), lambda i: (i, 0)), _prev_halo_spec(tile, C), _const(w_pool.shape)],
        out_specs=pl.BlockSpec((tile, C), lambda i: (i, 0)),
        out_shape=jax.ShapeDtypeStruct((S, C), F32),
        compiler_params=_params("parallel"),
    )(u, u, w_pool)


def _low_lanes():
    return lax.broadcasted_iota(jnp.int32, (QB, 2 * HEAD_DIM), 1) < HEAD_DIM


LOG_PIECES = 2
GRAD_PIECES = 3


def _triangle(inclusive, pieces):
    row = lax.broadcasted_iota(jnp.int32, (pieces * QB, QB), 0) % QB
    col = lax.broadcasted_iota(jnp.int32, (pieces * QB, QB), 1)
    return ((row >= col) if inclusive else (row > col)).astype(BF16)


def _pieces(v, n):
    out, rest = [], v
    for _ in range(n - 1):
        piece = rest.astype(BF16)
        out.append(piece)
        rest = rest - piece.astype(F32)
    out.append(rest.astype(BF16))
    return jnp.concatenate(out, axis=1)


def _causal_mask(width, offset):
    row = lax.broadcasted_iota(jnp.int32, (QB, width), 0)
    col = lax.broadcasted_iota(jnp.int32, (QB, width), 1)
    return col < row + offset


def _row_sums(vals, carry):
    for b in reversed(range(vals.shape[1] // QB)):
        carry = carry + jnp.sum(vals[:, b * QB:(b + 1) * QB], axis=1, keepdims=True)
    return carry


def _suffix_sums(vals, tri, carry):
    n = vals.shape[1] // QB
    out, run = [None] * n, carry
    for b in reversed(range(n)):
        blk = vals[:, b * QB:(b + 1) * QB]
        out[b] = _dot(_pieces(blk, tri.shape[0] // QB), tri) + run
        run = run + jnp.sum(blk, axis=1, keepdims=True)
    return (out[0] if n == 1 else jnp.concatenate(out, axis=1)), run


def _attn_tiles(qhs, kws, masks, carries, after_s):
    return _attn_weights(_attn_scores(qhs, kws, masks), masks, carries, after_s)


def _attn_scores(qhs, kws, masks):
    zs = [_dot_nt(qh, kw) * ATTN_SCALE for qh, kw in zip(qhs, kws)]
    es = [jnp.exp(-jnp.abs(z)) for z in zs]
    softplus = [jnp.maximum(z, 0.0) + jnp.log(1.0 + e) for z, e in zip(zs, es)]
    log_1m_beta = [-sp if m is None else jnp.where(m, -sp, 0.0) for sp, m in zip(softplus, masks)]
    return list(zip(zs, es, softplus, log_1m_beta))


def _attn_weights(scores, masks, carries, after_s):
    sums = [_suffix_sums(l, after_s, c) for (_, _, _, l), c in zip(scores, carries)]
    weights = [jnp.exp(z - sp + st) for (z, _, sp, _), (st, _) in zip(scores, sums)]
    weights = [a if m is None else jnp.where(m, a, 0.0) for a, m in zip(weights, masks)]
    return [(z, e, a, c) for (z, e, _, _), a, (_, c) in zip(scores, weights, sums)]


def _split_heads(v, low_lanes):
    return jnp.where(low_lanes, v, 0.0).astype(BF16), jnp.where(low_lanes, 0.0, v).astype(BF16)


def _sweep_done(c0, c1):
    return (jnp.maximum(jnp.max(c0), jnp.max(c1)) < EXP_UNDERFLOW).astype(jnp.int32)


def _first_window(i):
    first_blk = jnp.maximum(i - 1, 0)
    return first_blk, pl.multiple_of(first_blk * QB, QB), (i - first_blk) * QB


def _fwd_attn(qkv, n_pairs, ex, subs):
    S = qkv.shape[0]
    n_steps = S // (subs * QB)

    def body(q_ref, k_ref, v_ref, *rest):
        o_ref = rest[ex.n]
        ex_refs = ex.split(rest[:ex.n] + rest[ex.n + 1:])
        first_step, last_step = _grid_ends((n_pairs, n_steps))

        @pl.when(first_step)
        def _():
            ex.start(*ex_refs)

        low_lanes = _low_lanes()
        after_s = _triangle(False, LOG_PIECES)
        zero = jnp.zeros((QB, 1), F32)

        def cond(c):
            return jnp.logical_and(c[0] >= 0, c[1] == 0)

        qhs, kws, vws, masks, first_blks = [], [], [], [], []
        for sub in range(subs):
            i = pl.program_id(1) * subs + sub
            first_blk, start, offset = _first_window(i)
            first_blks.append(first_blk)
            qhs += _split_heads(q_ref[sub * QB:(sub + 1) * QB, :].astype(F32), low_lanes)
            kws += [k_ref[pl.ds(start, 2 * QB), :]] * 2
            vws += [v_ref[pl.ds(start, 2 * QB), :]] * 2
            masks += [_causal_mask(2 * QB, offset)] * 2
        tiles = _attn_tiles(qhs, kws, masks, [zero] * len(qhs), after_s)
        outs = [_dot(t[2].astype(BF16), vw) for t, vw in zip(tiles, vws)]

        for sub in range(subs):
            def step(c, qh=qhs[2 * sub:2 * sub + 2]):
                j, _, acc, c0, c1 = c
                at = pl.multiple_of(j * QB, QB)
                kb = k_ref[pl.ds(at, QB), :]
                vb = v_ref[pl.ds(at, QB), :]
                far = _attn_tiles(qh, [kb, kb], [None, None], [c0, c1], after_s)
                acc = acc + jnp.where(low_lanes, _dot(far[0][2].astype(BF16), vb), _dot(far[1][2].astype(BF16), vb))
                return j - 1, _sweep_done(far[0][3], far[1][3]), acc, far[0][3], far[1][3]

            c0, c1 = tiles[2 * sub][3], tiles[2 * sub + 1][3]
            init = (first_blks[sub] - 1, _sweep_done(c0, c1), jnp.where(low_lanes, outs[2 * sub], outs[2 * sub + 1]), c0, c1)
            o_ref[sub * QB:(sub + 1) * QB, :] = lax.while_loop(cond, step, init)[2]

        @pl.when(last_step)
        def _():
            ex.wait(*ex_refs)

    outs = pl.pallas_call(
        body, name="fwd_attn", grid=(n_pairs, n_steps),
        in_specs=[pl.BlockSpec((subs * QB, QB), lambda p, i: (i, p)),
                  pl.BlockSpec((S, QB), lambda p, i: (0, n_pairs + p), pipeline_mode=pl.Buffered(1)),
                  pl.BlockSpec((S, QB), lambda p, i: (0, 2 * n_pairs + p), pipeline_mode=pl.Buffered(1))] + ex.specs,
        out_specs=[pl.BlockSpec((subs * QB, QB), lambda p, i: (i, p))] + ex.specs,
        out_shape=[jax.ShapeDtypeStruct((S, n_pairs * QB), F32)] + ex.out_shape,
        scratch_shapes=ex.scratch,
        compiler_params=_params("arbitrary", "arbitrary"),
    )(qkv, qkv, qkv, *ex.arrays)
    return outs[0], outs[1:]


def _normalized_heads(pool_out, attn_out):
    rp, ra = _rms(pool_out), _rms(attn_out)
    return pool_out * rp, rp, attn_out * ra, ra


def _fwd_outproj(pool_out, attn_out, pool_scale, attn_scale, w_out, x, g2, g3, tile):
    S, D = x.shape
    C = pool_out.shape[1]

    def body(p_ref, a_ref, ps_ref, as_ref, w_ref, x_ref, g2_ref, g3_ref, mix_ref, x2_ref, h2_ref, h2t_ref):
        n_p, _, n_a, _ = _normalized_heads(p_ref[...], a_ref[...])
        mix = _dot((n_p * ps_ref[...]).astype(BF16), w_ref[:C, :]) + _dot((n_a * as_ref[...]).astype(BF16), w_ref[C:, :])
        mix_ref[...] = mix
        x2 = x_ref[...] + mix * _rms(mix) * g2_ref[...]
        x2_ref[...] = x2
        h2 = (x2 * _rms(x2) * g3_ref[...]).astype(BF16)
        h2_ref[...] = h2
        h2t_ref[...] = h2.T

    row = lambda w: pl.BlockSpec((tile, w), lambda i: (i, 0))
    return pl.pallas_call(
        body, name="fwd_outproj", grid=(S // tile,),
        in_specs=[row(C), row(C), _const((1, C)), _const((1, C)), _const(w_out.shape), row(D), _const((1, D)), _const((1, D))],
        out_specs=[row(D), row(D), row(D), pl.BlockSpec((D, tile), lambda i: (0, i))],
        out_shape=[jax.ShapeDtypeStruct((S, D), F32), jax.ShapeDtypeStruct((S, D), F32), jax.ShapeDtypeStruct((S, D), BF16),
                   jax.ShapeDtypeStruct((D, S), BF16)],
        compiler_params=_params("parallel"),
    )(pool_out, attn_out, pool_scale, attn_scale, w_out, x, g2, g3)


def _conv_taps(tile_rows, halo_rows):
    T = tile_rows.shape[0]
    ext = jnp.concatenate([halo_rows.astype(F32), tile_rows.astype(F32)], axis=0)
    return pltpu.roll(ext, 2, axis=0)[HALO:], pltpu.roll(ext, 1, axis=0)[HALO:], ext[HALO:]


def _tap_rows(cw_ref, d):
    return [cw_ref[d, k:k + 1, :] for k in range(3)]


def _gated_unit(taps_gate, taps_val, cw_gate, cw_val, cb_gate, cb_val):
    gate = cw_gate[0] * taps_gate[0] + cw_gate[1] * taps_gate[1] + cw_gate[2] * taps_gate[2] + cb_gate
    val = cw_val[0] * taps_val[0] + cw_val[1] * taps_val[1] + cw_val[2] * taps_val[2] + cb_val
    sig = 1.0 / (1.0 + jnp.exp(-gate))
    return gate, val, sig


def _fwd_ffn_loss(h2, w_up_g, conv_w_g, conv_b_g, w_down4, x2, target, g4, tile):
    S, D = x2.shape
    nb, _, cs = w_up_g.shape
    half = D_FF_SHARDS

    def body(h_ref, w_ref, cw_ref, cb_ref, wd_ref, x2_ref, t_ref, g4_ref, upre_ref, gv_ref, dy_ref, df_ref, loss_ref, dg4_ref, halo_ref):
        _zero_when(pl.program_id(0) == 0, loss_ref, dg4_ref, halo_ref)
        h = h_ref[...]

        def up(s):
            return _dot(h, w_ref[s]).astype(BF16), _dot(h, w_ref[s + half]).astype(BF16)

        f = jnp.zeros((tile, D), F32)
        ahead = up(0)
        for s in range(half):
            ug, uv = ahead
            if s + 1 < half:
                ahead = up(s + 1)
            upre_ref[s] = ug
            upre_ref[s + half] = uv
            gate, val, sig = _gated_unit(_conv_taps(ug, halo_ref[s]), _conv_taps(uv, halo_ref[s + half]),
                                         _tap_rows(cw_ref, s), _tap_rows(cw_ref, s + half), cb_ref[s], cb_ref[s + half])
            halo_ref[s] = ug[tile - HALO:, :]
            halo_ref[s + half] = uv[tile - HALO:, :]
            gv_ref[s] = gate.astype(BF16)
            gv_ref[s + half] = val.astype(BF16)
            f = f + _dot((gate * sig * val).astype(BF16), wd_ref[s])
        r4 = _rms(f)
        n4 = f * r4
        err = x2_ref[...] + n4 * g4_ref[...] - t_ref[...]
        dy = err * (1.0 / D)
        dy_ref[...] = dy
        df_ref[...] = _norm_bwd(dy * g4_ref[...], n4, r4).astype(BF16)
        loss_ref[...] += _colsum(err * err)
        dg4_ref[...] += _colsum(dy * n4)

    row = lambda w: pl.BlockSpec((tile, w), lambda i: (i, 0))
    return pl.pallas_call(
        body, name="fwd_ffn_loss", grid=(S // tile,),
        in_specs=[row(D), _const(w_up_g.shape), _const(conv_w_g.shape), _const(conv_b_g.shape), _const(w_down4.shape),
                  row(D), row(D), _const((1, D))],
        out_specs=[pl.BlockSpec((nb, tile, cs), lambda i: (0, i, 0)), pl.BlockSpec((nb, tile, cs), lambda i: (0, i, 0)), row(D), row(D),
                   pl.BlockSpec((1, D), lambda i: (0, 0)), pl.BlockSpec((1, D), lambda i: (0, 0))],
        out_shape=[jax.ShapeDtypeStruct((nb, S, cs), BF16), jax.ShapeDtypeStruct((nb, S, cs), BF16),
                   jax.ShapeDtypeStruct((S, D), F32), jax.ShapeDtypeStruct((S, D), BF16),
                   jax.ShapeDtypeStruct((1, D), F32), jax.ShapeDtypeStruct((1, D), F32)],
        scratch_shapes=[pltpu.VMEM((nb, HALO, cs), BF16)],
        compiler_params=_params("arbitrary"),
    )(h2, w_up_g, conv_w_g, conv_b_g, w_down4, x2, target, g4)


def _bwd_down(upre, conv_w_g, conv_b_g, w_down4, df, tile):
    nb, S, cs = upre.shape
    D = df.shape[1]
    n_tiles = S // tile

    def body(ug_ref, uv_ref, hg_ref, hv_ref, cwg_ref, cwv_ref, cbg_ref, cbv_ref, wd_ref, df_ref,
             dg_ref, dv_ref, dwd_ref, dbg_ref, dbv_ref, dcwg_ref, dcwv_ref):
        i = pl.program_id(1)
        first = i == 0
        _zero_when(first, dwd_ref, dbg_ref, dbv_ref, dcwg_ref, dcwv_ref)
        halo_g = jnp.where(first, jnp.zeros_like(hg_ref[0]), hg_ref[0])
        halo_v = jnp.where(first, jnp.zeros_like(hv_ref[0]), hv_ref[0])
        taps_g, taps_v = _conv_taps(ug_ref[0], halo_g), _conv_taps(uv_ref[0], halo_v)
        gate, val, sig = _gated_unit(taps_g, taps_v, _tap_rows(cwg_ref, 0), _tap_rows(cwv_ref, 0), cbg_ref[0], cbv_ref[0])
        silu = gate * sig
        dfb = df_ref[...]
        dact = _dot_nt(dfb, wd_ref[0])
        dwd_ref[0] += _dot_tn((silu * val).astype(BF16), dfb)
        dgate = dact * val * (sig * (1.0 + gate * (1.0 - sig)))
        dval = dact * silu
        dg_ref[0] = dgate.astype(BF16)
        dv_ref[0] = dval.astype(BF16)
        dbg_ref[0] += _colsum(dgate)
        dbv_ref[0] += _colsum(dval)
        for k in range(3):
            dcwg_ref[0, k:k + 1, :] += _colsum(dgate * taps_g[k])
            dcwv_ref[0, k:k + 1, :] += _colsum(dval * taps_v[k])

    half = D_FF_SHARDS
    blk = lambda off: pl.BlockSpec((1, tile, cs), lambda s, i: (s + off, i, 0))
    halo = lambda off: pl.BlockSpec((1, HALO, cs), lambda s, i: (s + off, jnp.maximum(i * (tile // HALO) - 1, 0), 0))
    par = lambda off, r: pl.BlockSpec((1, r, cs), lambda s, i: (s + off, 0, 0))
    outs = pl.pallas_call(
        body, name="bwd_down", grid=(half, n_tiles),
        in_specs=[blk(0), blk(half), halo(0), halo(half), par(0, 3), par(half, 3), par(0, 1), par(half, 1),
                  pl.BlockSpec((1, cs, D), lambda s, i: (s, 0, 0)), pl.BlockSpec((tile, D), lambda s, i: (i, 0))],
        out_specs=[blk(0), blk(0), pl.BlockSpec((1, cs, D), lambda s, i: (s, 0, 0)),
                   par(0, 1), par(0, 1), par(0, 3), par(0, 3)],
        out_shape=[jax.ShapeDtypeStruct((half, S, cs), BF16), jax.ShapeDtypeStruct((half, S, cs), BF16),
                   jax.ShapeDtypeStruct((half, cs, D), F32),
                   jax.ShapeDtypeStruct((half, 1, cs), F32), jax.ShapeDtypeStruct((half, 1, cs), F32),
                   jax.ShapeDtypeStruct((half, 3, cs), F32), jax.ShapeDtypeStruct((half, 3, cs), F32)],
        compiler_params=_params("parallel", "arbitrary"),
    )(upre, upre, upre, upre, conv_w_g, conv_w_g, conv_b_g, conv_b_g, w_down4, df)
    dgate, dval, d_wd, dbg, dbv, dcwg, dcwv = outs
    return dgate, dval, d_wd, jnp.concatenate([dbg, dbv], axis=0), jnp.concatenate([dcwg, dcwv], axis=0)


def _bwd_up_x(dgate, dval, conv_w_g, w_up_g, x2, dy, mix, g2, g3, tile):
    half, S, cs = dgate.shape
    nb = 2 * half
    D = x2.shape[1]
    n_tiles = S // tile

    def body(dg_ref, dv_ref, hg_ref, hv_ref, cw_ref, w_ref, x2_ref, dy_ref, mix_ref, g2_ref, g3_ref,
             dupre_ref, dx2_ref, dmix_ref, dg3_ref, dg2_ref):
        i = pl.program_id(0)
        last = i == n_tiles - 1
        _zero_when(i == 0, dg3_ref, dg2_ref)
        dh2 = jnp.zeros((tile, D), F32)
        for d in range(nb):
            src, halo = (dg_ref, hg_ref) if d < half else (dv_ref, hv_ref)
            nxt = jnp.where(last, jnp.zeros_like(halo[d % half]), halo[d % half])
            ext = jnp.concatenate([src[d % half].astype(F32), nxt.astype(F32)], axis=0)
            n = ext.shape[0]
            cw = _tap_rows(cw_ref, d)
            dupre = (cw[2] * ext + cw[1] * pltpu.roll(ext, n - 1, axis=0) + cw[0] * pltpu.roll(ext, n - 2, axis=0))[:tile]
            dupre = dupre.astype(BF16)
            dupre_ref[d] = dupre
            dh2 = dh2 + _dot_nt(dupre, w_ref[d])
        x2 = x2_ref[...]
        r3 = _rms(x2)
        n3 = x2 * r3
        dg3_ref[...] += _colsum(dh2 * n3)
        dx2 = dy_ref[...] + _norm_bwd(dh2 * g3_ref[...], n3, r3)
        dx2_ref[...] = dx2
        mix = mix_ref[...]
        r2 = _rms(mix)
        n2 = mix * r2
        dg2_ref[...] += _colsum(dx2 * n2)
        dmix_ref[...] = _norm_bwd(dx2 * g2_ref[...], n2, r2).astype(BF16)

    row = lambda w: pl.BlockSpec((tile, w), lambda i: (i, 0))
    blk = pl.BlockSpec((half, tile, cs), lambda i: (0, i, 0))
    last_halo = n_tiles * (tile // HALO) - 1
    halo = pl.BlockSpec((half, HALO, cs), lambda i: (0, jnp.minimum((i + 1) * (tile // HALO), last_halo), 0))
    acc = pl.BlockSpec((1, D), lambda i: (0, 0))
    return pl.pallas_call(
        body, name="bwd_up_x", grid=(n_tiles,),
        in_specs=[blk, blk, halo, halo, _const(conv_w_g.shape), _const(w_up_g.shape), row(D), row(D), row(D),
                  _const((1, D)), _const((1, D))],
        out_specs=[pl.BlockSpec((nb, tile, cs), lambda i: (0, i, 0)), row(D), row(D), acc, acc],
        out_shape=[jax.ShapeDtypeStruct((nb, S, cs), BF16), jax.ShapeDtypeStruct((S, D), F32),
                   jax.ShapeDtypeStruct((S, D), BF16), jax.ShapeDtypeStruct((1, D), F32), jax.ShapeDtypeStruct((1, D), F32)],
        compiler_params=_params("arbitrary"),
    )(dgate, dval, dgate, dval, conv_w_g, w_up_g, x2, dy, mix, g2, g3)


def _bwd_weight(act_t, dout, tile):
    D, S = act_t.shape
    nb, _, cs = dout.shape

    def body(a_ref, d_ref, o_ref):
        _zero_when(pl.program_id(1) == 0, o_ref)
        o_ref[0] += _dot(a_ref[...], d_ref[0])

    return pl.pallas_call(
        body, name="bwd_w_up", grid=(nb, S // tile),
        in_specs=[pl.BlockSpec((D, tile), lambda d, i: (0, i)), pl.BlockSpec((1, tile, cs), lambda d, i: (d, i, 0))],
        out_specs=pl.BlockSpec((1, D, cs), lambda d, i: (d, 0, 0)),
        out_shape=jax.ShapeDtypeStruct((nb, D, cs), F32),
        compiler_params=_params("parallel", "arbitrary"),
    )(act_t, dout)


def _bwd_ffn_blocks(gate_val, upre, conv_w_g, w_down4, df, h2_t, tile):
    nb, S, cs = upre.shape
    D = df.shape[1]
    n_tiles = S // tile
    half = D_FF_SHARDS

    def body(g_ref, v_ref, ug_ref, uv_ref, cwg_ref, cwv_ref, wd_ref, df_ref, ht_ref,
             dug_ref, duv_ref, dwd_ref, dwg_ref, dwv_ref, dbg_ref, dbv_ref, dcwg_ref, dcwv_ref, next_ref):
        _zero_when(pl.program_id(1) == 0, dwd_ref, dwg_ref, dwv_ref, dbg_ref, dbv_ref, dcwg_ref, dcwv_ref, next_ref)
        dfb = df_ref[...]
        dact = _dot_nt(dfb, wd_ref[0])
        gate, val = g_ref[0].astype(F32), v_ref[0].astype(F32)
        sig = 1.0 / (1.0 + jnp.exp(-gate))
        silu = gate * sig
        dwd_ref[0] += _dot_tn((silu * val).astype(BF16), dfb)
        ht = ht_ref[...]

        def through_conv(dup, slot, cw_ref, u_ref, du_ref, dw_ref, db_ref, dcw_ref):
            ext = jnp.concatenate([dup, next_ref[slot]], axis=0)
            n = ext.shape[0]
            shifted = (dup, pltpu.roll(ext, n - 1, axis=0)[:tile], pltpu.roll(ext, n - 2, axis=0)[:tile])
            next_ref[slot] = dup[:HALO]
            cw = _tap_rows(cw_ref, 0)
            dupre = (cw[2] * shifted[0] + cw[1] * shifted[1] + cw[0] * shifted[2]).astype(BF16)
            du_ref[0] = dupre
            dw_ref[0] += _dot(ht, dupre)
            u = u_ref[0].astype(F32)
            db_ref[0] += _colsum(dup)
            for k in range(3):
                dcw_ref[0, k:k + 1, :] += _colsum(shifted[2 - k] * u)

        through_conv(dact * val * (sig * (1.0 + gate * (1.0 - sig))), 0, cwg_ref, ug_ref, dug_ref, dwg_ref, dbg_ref, dcwg_ref)
        through_conv(dact * silu, 1, cwv_ref, uv_ref, duv_ref, dwv_ref, dbv_ref, dcwv_ref)

    rev = lambda i: n_tiles - 1 - i
    blk = lambda off: pl.BlockSpec((1, tile, cs), lambda s, i: (s + off, rev(i), 0))
    par = lambda off, r: pl.BlockSpec((1, r, cs), lambda s, i: (s + off, 0, 0))
    acc = lambda r, c: pl.BlockSpec((1, r, c), lambda s, i: (s, 0, 0), pipeline_mode=pl.Buffered(1))
    outs = pl.pallas_call(
        body, name="bwd_ffn_blocks", grid=(half, n_tiles),
        in_specs=[blk(0), blk(half), blk(0), blk(half), par(0, 3), par(half, 3),
                  acc(cs, D), pl.BlockSpec((tile, D), lambda s, i: (rev(i), 0)), pl.BlockSpec((D, tile), lambda s, i: (0, rev(i)))],
        out_specs=[blk(0), blk(0), acc(cs, D), acc(D, cs), acc(D, cs), acc(1, cs), acc(1, cs), acc(3, cs), acc(3, cs)],
        out_shape=[jax.ShapeDtypeStruct((half, S, cs), BF16), jax.ShapeDtypeStruct((half, S, cs), BF16),
                   jax.ShapeDtypeStruct((half, cs, D), F32),
                   jax.ShapeDtypeStruct((half, D, cs), F32), jax.ShapeDtypeStruct((half, D, cs), F32),
                   jax.ShapeDtypeStruct((half, 1, cs), F32), jax.ShapeDtypeStruct((half, 1, cs), F32),
                   jax.ShapeDtypeStruct((half, 3, cs), F32), jax.ShapeDtypeStruct((half, 3, cs), F32)],
        scratch_shapes=[pltpu.VMEM((2, HALO, cs), F32)],
        compiler_params=_params("arbitrary", "arbitrary"),
    )(gate_val, gate_val, upre, upre, conv_w_g, conv_w_g, w_down4, df, h2_t)
    dupre_g, dupre_v, d_wd, d_wg, d_wv, dbg, dbv, dcwg, dcwv = outs
    return (dupre_g, dupre_v, d_wd, jnp.concatenate([d_wg, d_wv], axis=0), jnp.concatenate([dbg, dbv], axis=0),
            jnp.concatenate([dcwg, dcwv], axis=0))


def _bwd_ffn_tokens(dupre_g, dupre_v, w_up_t, x2, dy, mix, g2, g3, tile):
    half, S, cs = dupre_g.shape
    D = x2.shape[1]

    def body(dg_ref, dv_ref, w_ref, x2_ref, dy_ref, mix_ref, g2_ref, g3_ref, dx2_ref, dmix_ref, dg3_ref, dg2_ref):
        _zero_when(pl.program_id(0) == 0, dg3_ref, dg2_ref)
        parts = [_dot(dg_ref[d], w_ref[d]) for d in range(half)] + [_dot(dv_ref[d], w_ref[d + half]) for d in range(half)]
        while len(parts) > 1:
            parts = [a + b for a, b in zip(parts[::2], parts[1::2])]
        dh2 = parts[0]
        x2 = x2_ref[...]
        r3 = _rms(x2)
        n3 = x2 * r3
        dg3_ref[...] += _colsum(dh2 * n3)
        dx2 = dy_ref[...] + _norm_bwd(dh2 * g3_ref[...], n3, r3)
        dx2_ref[...] = dx2
        mix = mix_ref[...]
        r2 = _rms(mix)
        n2 = mix * r2
        dg2_ref[...] += _colsum(dx2 * n2)
        dmix_ref[...] = _norm_bwd(dx2 * g2_ref[...], n2, r2).astype(BF16)

    row = lambda w: pl.BlockSpec((tile, w), lambda i: (i, 0))
    blk = pl.BlockSpec((half, tile, cs), lambda i: (0, i, 0))
    acc = pl.BlockSpec((1, D), lambda i: (0, 0))
    return pl.pallas_call(
        body, name="bwd_ffn_tokens", grid=(S // tile,),
        in_specs=[blk, blk, _const(w_up_t.shape), row(D), row(D), row(D), _const((1, D)), _const((1, D))],
        out_specs=[row(D), row(D), acc, acc],
        out_shape=[jax.ShapeDtypeStruct((S, D), F32), jax.ShapeDtypeStruct((S, D), BF16),
                   jax.ShapeDtypeStruct((1, D), F32), jax.ShapeDtypeStruct((1, D), F32)],
        compiler_params=_params("arbitrary"),
    )(dupre_g, dupre_v, w_up_t, x2, dy, mix, g2, g3)


def _bwd_outproj(dmix, w_out, pool_out, attn_out, pool_scale, attn_scale, tile):
    S, D = dmix.shape
    C = pool_out.shape[1]

    def body(dm_ref, w_ref, p_ref, a_ref, ps_ref, as_ref, dp_ref, da_ref, dw_ref, dps_ref, das_ref):
        _zero_when(pl.program_id(0) == 0, dw_ref, dps_ref, das_ref)
        dmx = dm_ref[...]
        dmerged = _dot_nt(dmx, w_ref[...])
        n_p, r_p, n_a, r_a = _normalized_heads(p_ref[...], a_ref[...])
        merged = jnp.concatenate([(n_p * ps_ref[...]).astype(BF16), (n_a * as_ref[...]).astype(BF16)], axis=1)
        dw_ref[...] += _dot_tn(merged, dmx)
        dm_p, dm_a = dmerged[:, :C], dmerged[:, C:]
        dps_ref[...] += _colsum(dm_p * n_p)
        das_ref[...] += _colsum(dm_a * n_a)
        dp_ref[...] = _norm_bwd(dm_p * ps_ref[...], n_p, r_p)
        da_ref[...] = _norm_bwd(dm_a * as_ref[...], n_a, r_a)

    row = lambda w: pl.BlockSpec((tile, w), lambda i: (i, 0))
    return pl.pallas_call(
        body, name="bwd_outproj", grid=(S // tile,),
        in_specs=[row(D), _const(w_out.shape), row(C), row(C), _const((1, C)), _const((1, C))],
        out_specs=[row(C), row(C), pl.BlockSpec(w_out.shape, lambda i: (0, 0)),
                   pl.BlockSpec((1, C), lambda i: (0, 0)), pl.BlockSpec((1, C), lambda i: (0, 0))],
        out_shape=[jax.ShapeDtypeStruct((S, C), F32), jax.ShapeDtypeStruct((S, C), F32),
                   jax.ShapeDtypeStruct(w_out.shape, F32), jax.ShapeDtypeStruct((1, C), F32), jax.ShapeDtypeStruct((1, C), F32)],
        compiler_params=_params("arbitrary"),
    )(dmix, w_out, pool_out, attn_out, pool_scale, attn_scale)


def _bwd_attn(qkv, d_attn, n_pairs, ex, subs):
    S = qkv.shape[0]
    n_steps = S // (subs * QB)

    def body(q_ref, k_ref, v_ref, do_ref, *rest):
        dq_ref, dk_ref, dv_ref = rest[ex.n:ex.n + 3]
        ex_refs = ex.split(rest[:ex.n] + rest[ex.n + 3:])
        first_step, last_step = _grid_ends((n_pairs, n_steps))

        @pl.when(first_step)
        def _():
            ex.start(*ex_refs)

        @pl.when(pl.program_id(1) == 0)
        def _():
            dk_ref[...] = jnp.zeros_like(dk_ref)
            dv_ref[...] = jnp.zeros_like(dv_ref)

        low_lanes = _low_lanes()
        after_s, from_s = _triangle(False, LOG_PIECES), _triangle(True, GRAD_PIECES)
        zero = jnp.zeros((QB, 1), F32)

        def tiles(qhs, dohs, totals, kws, vws, masks, cs, gs, scores=None):
            fw = _attn_weights(scores or _attn_scores(qhs, kws, masks), masks, cs, after_s)
            gvals = [t[2] * _dot_nt(doh, vw) for t, doh, vw in zip(fw, dohs, vws)]
            sums = [_suffix_sums(g, from_s, g0) for g, g0 in zip(gvals, gs)]
            totals = [tot if m is None else tot + sm[1] for tot, m, sm in zip(totals, masks, sums)]
            dzs = []
            for (z, e, _, _), g, (nearer, _), tot, m in zip(fw, gvals, sums, totals, masks):
                inv = 1.0 / (1.0 + e)
                sig_abs, sig_neg = inv, e * inv
                pos = z >= 0.0
                dz = g * jnp.where(pos, sig_neg, sig_abs) - jnp.where(pos, sig_abs, sig_neg) * (tot - nearer)
                if m is not None:
                    dz = jnp.where(m, dz, 0.0)
                dzs.append((dz * ATTN_SCALE).astype(BF16))
            dqs = [_dot(dz, kw) for dz, kw in zip(dzs, kws)]
            dks = [_dot_tn(dz, qh) for dz, qh in zip(dzs, qhs)]
            dvs = [_dot_tn(t[2].astype(BF16), doh) for t, doh in zip(fw, dohs)]
            return [(dq, dk, dv, t[3], sm[1], tot) for dq, dk, dv, t, sm, tot in zip(dqs, dks, dvs, fw, sums, totals)]

        def cond(c):
            return jnp.logical_and(c[0] >= 0, c[1] == 0)

        qhs, dohs, kws, vws, masks, first_blks, starts = [], [], [], [], [], [], []
        for sub in range(subs):
            i = pl.program_id(1) * subs + sub
            rows = slice(sub * QB, (sub + 1) * QB)
            first_blk, start, offset = _first_window(i)
            first_blks.append(first_blk)
            starts.append(start)
            qhs += _split_heads(q_ref[rows, :].astype(F32), low_lanes)
            dohs += _split_heads(do_ref[rows, :], low_lanes)
            kws += [k_ref[pl.ds(start, 2 * QB), :]] * 2
            vws += [v_ref[pl.ds(start, 2 * QB), :]] * 2
            masks += [_causal_mask(2 * QB, offset)] * 2
        zeros = [zero] * len(qhs)

        scores = _attn_scores(qhs, kws, masks)
        c_first = [_row_sums(sc[3], zero) for sc in scores]
        beyond_first = []
        for sub in range(subs):
            pair = slice(2 * sub, 2 * sub + 2)

            def far_sums(c, qh=qhs[pair], doh=dohs[pair]):
                j, _, c0, c1, r0, r1 = c
                at = pl.multiple_of(j * QB, QB)
                kb = k_ref[pl.ds(at, QB), :]
                vb = v_ref[pl.ds(at, QB), :]
                far = _attn_tiles(qh, [kb, kb], [None, None], [c0, c1], after_s)
                r0 = r0 + jnp.sum(far[0][2] * _dot_nt(doh[0], vb), axis=1, keepdims=True)
                r1 = r1 + jnp.sum(far[1][2] * _dot_nt(doh[1], vb), axis=1, keepdims=True)
                return j - 1, _sweep_done(far[0][3], far[1][3]), far[0][3], far[1][3], r0, r1

            c0, c1 = c_first[pair]
            far = lax.while_loop(cond, far_sums, (first_blks[sub] - 1, _sweep_done(c0, c1), c0, c1, zero, zero))
            beyond_first += [far[4], far[5]]

        done = tiles(qhs, dohs, beyond_first, kws, vws, masks, zeros, zeros, scores)
        for sub in range(subs):
            dk_ref[pl.ds(starts[sub], 2 * QB), :] += done[2 * sub][1] + done[2 * sub + 1][1]
            dv_ref[pl.ds(starts[sub], 2 * QB), :] += done[2 * sub][2] + done[2 * sub + 1][2]

        for sub in range(subs):
            pair = slice(2 * sub, 2 * sub + 2)
            t0, t1 = done[pair]

            def step(c, qh=qhs[pair], doh=dohs[pair], total=[t0[5], t1[5]]):
                j, _, dq, c0, c1, s0, s1 = c
                at = pl.multiple_of(j * QB, QB)
                kb = k_ref[pl.ds(at, QB), :]
                vb = v_ref[pl.ds(at, QB), :]
                f0, f1 = tiles(qh, doh, total, [kb, kb], [vb, vb], [None, None], [c0, c1], [s0, s1])
                dk_ref[pl.ds(at, QB), :] += f0[1] + f1[1]
                dv_ref[pl.ds(at, QB), :] += f0[2] + f1[2]
                return j - 1, _sweep_done(f0[3], f1[3]), dq + jnp.where(low_lanes, f0[0], f1[0]), f0[3], f1[3], f0[4], f1[4]

            init = (first_blks[sub] - 1, _sweep_done(t0[3], t1[3]), jnp.where(low_lanes, t0[0], t1[0]), t0[3], t1[3], t0[4], t1[4])
            dq_ref[sub * QB:(sub + 1) * QB, :] = lax.while_loop(cond, step, init)[2]

        @pl.when(last_step)
        def _():
            ex.wait(*ex_refs)

    blk = pl.BlockSpec((subs * QB, QB), lambda p, i: (i, p))
    full = lambda off: pl.BlockSpec((S, QB), lambda p, i: (0, off + p), pipeline_mode=pl.Buffered(1))
    outs = pl.pallas_call(
        body, name="bwd_attn", grid=(n_pairs, n_steps),
        in_specs=[blk, full(n_pairs), full(2 * n_pairs), blk] + ex.specs,
        out_specs=[blk, pl.BlockSpec((S, QB), lambda p, i: (0, p)), pl.BlockSpec((S, QB), lambda p, i: (0, p))] + ex.specs,
        out_shape=[jax.ShapeDtypeStruct((S, n_pairs * QB), F32)] * 3 + ex.out_shape,
        scratch_shapes=ex.scratch,
        compiler_params=_params("arbitrary", "arbitrary"),
    )(qkv, qkv, qkv, d_attn, *ex.arrays)
    return outs[0], outs[1], outs[2], outs[3:]


def _bwd_pool(u, d_pool, w_pool, tile):
    S, C = u.shape
    n_tiles = S // tile
    ng = len(POOL_WINDOWS)

    def body(u_ref, uh_ref, d_ref, dh_ref, wp_ref, du_ref, dwp_ref):
        i = pl.program_id(0)
        first = i == 0
        _zero_when(first, dwp_ref)
        halo = jnp.where(first, 0.0, uh_ref[...])
        parts = _pool_deviation(u_ref[...], halo, i * tile)
        dout = d_ref[...]
        nxt = jnp.where(i == n_tiles - 1, 0.0, dh_ref[...])
        dext = jnp.concatenate([dout, nxt], axis=0).astype(BF16)
        counts = _pool_counts(i * tile, tile + HALO)
        dps, scaled = [], []
        for g in range(ng):
            lanes = slice(g * POOL_GROUP, (g + 1) * POOL_GROUP)
            dp = _dot_nt(dext[:, lanes], wp_ref[g].astype(BF16))
            dps.append(dp[:tile])
            scaled.append(dp / counts[g])
        sums = _window_sums(jnp.concatenate(scaled, axis=1), forward=True)
        for g, w in enumerate(POOL_WINDOWS):
            lanes = slice(g * POOL_GROUP, (g + 1) * POOL_GROUP)
            du_ref[:, lanes] = sums[w][:tile, lanes] - dps[g]
            dwp_ref[g] += _dot_tn(parts[g].astype(BF16), dext[:tile, lanes])

    row = pl.BlockSpec((tile, C), lambda i: (i, 0))
    return pl.pallas_call(
        body, name="bwd_pool", grid=(n_tiles,),
        in_specs=[row, _prev_halo_spec(tile, C), row, _next_halo_spec(tile, C, n_tiles), _const(w_pool.shape)],
        out_specs=[row, pl.BlockSpec(w_pool.shape, lambda i: (0, 0, 0))],
        out_shape=[jax.ShapeDtypeStruct((S, C), F32), jax.ShapeDtypeStruct(w_pool.shape, F32)],
        compiler_params=_params("arbitrary"),
    )(u, u, d_pool, d_pool, w_pool)


def _bwd_w_in(du, dq, dk, dv, h1_t, n_blocks, tile):
    D, S = h1_t.shape
    C = du.shape[1]
    cs = 4 * C // n_blocks
    per = C // cs

    def body(du_ref, dq_ref, dk_ref, dv_ref, ht_ref, dproj_ref, dw_ref):
        _zero_when(pl.program_id(0) == 0, dw_ref)
        ht = ht_ref[...]
        for d in range(n_blocks):
            src = (du_ref, dq_ref, dk_ref, dv_ref)[d // per]
            dproj = src[:, (d % per) * cs:(d % per + 1) * cs].astype(BF16)
            dproj_ref[:, d * cs:(d + 1) * cs] = dproj
            dw_ref[d] += _dot(ht, dproj)

    row = lambda w: pl.BlockSpec((tile, w), lambda i: (i, 0))
    return pl.pallas_call(
        body, name="bwd_w_in", grid=(S // tile,),
        in_specs=[row(C), row(C), row(C), row(C), pl.BlockSpec((D, tile), lambda i: (0, i))],
        out_specs=[row(4 * C), pl.BlockSpec((n_blocks, D, cs), lambda i: (0, 0, 0))],
        out_shape=[jax.ShapeDtypeStruct((S, 4 * C), BF16), jax.ShapeDtypeStruct((n_blocks, D, cs), F32)],
        compiler_params=_params("arbitrary"),
    )(du, dq, dk, dv, h1_t)


def _bwd_x(dproj, w_in_t, x, dx2, g1, tile, ex):
    S, D = x.shape
    n_tiles = S // tile

    def body(dp_ref, w_ref, x_ref, dx2_ref, g_ref, *rest):
        dx_ref, dg_ref = rest[ex.n:ex.n + 2]
        ex_refs = ex.split(rest[:ex.n] + rest[ex.n + 2:])
        first, last = _grid_ends((n_tiles,))

        @pl.when(first)
        def _():
            ex.start(*ex_refs)
            dg_ref[...] = jnp.zeros_like(dg_ref)

        dh = _dot(dp_ref[...], w_ref[...])
        xf = x_ref[...]
        r1 = _rms(xf)
        n1 = xf * r1
        dg_ref[...] += _colsum(dh * n1)
        dx_ref[...] = dx2_ref[...] + _norm_bwd(dh * g_ref[...], n1, r1)

        @pl.when(last)
        def _():
            ex.wait(*ex_refs)

    row = lambda w: pl.BlockSpec((tile, w), lambda i: (i, 0))
    outs = pl.pallas_call(
        body, name="bwd_x", grid=(n_tiles,),
        in_specs=[row(w_in_t.shape[0]), _const(w_in_t.shape), row(D), row(D), _const((1, D))] + ex.specs,
        out_specs=[row(D), pl.BlockSpec((1, D), lambda i: (0, 0))] + ex.specs,
        out_shape=[jax.ShapeDtypeStruct((S, D), F32), jax.ShapeDtypeStruct((1, D), F32)] + ex.out_shape,
        scratch_shapes=ex.scratch,
        compiler_params=_params("arbitrary"),
    )(dproj, w_in_t, x, dx2, g1, *ex.arrays)
    return outs[0], outs[1], outs[2:]


def _mesh_position():
    x, y, c = lax.axis_index("x"), lax.axis_index("y"), lax.axis_index("c")
    return x, y, c, 4 * x + 2 * y + c


def _peer(x, y, c, k):
    px = 1 - x if k & 4 else x
    py = 1 - y if k & 2 else y
    pc = 1 - c if k & 1 else c
    return (px, py, pc), 4 * px + 2 * py + pc


class _Exchange:
    def __init__(self, arrays, gather):
        self.arrays, self.gather, self.n = list(arrays), gather, len(arrays)
        self.out_shape = [jax.ShapeDtypeStruct(((N_DEV,) + a.shape) if gather else a.shape, a.dtype) for a in arrays]
        self.specs = [pl.BlockSpec(memory_space=pl.ANY)] * self.n
        copies = self.n * (N_DEV - 1)
        self.scratch = [pltpu.SemaphoreType.DMA((copies,)), pltpu.SemaphoreType.DMA((copies,)),
                        pltpu.SemaphoreType.DMA((self.n,))]

    def _copies(self, ins, outs, sems):
        send_sems, recv_sems, local_sems = sems
        x, y, c, me = _mesh_position()
        local, remote = [], []
        for a in range(self.n):
            mine = ins[a] if self.gather else ins[a].at[me]
            local.append(pltpu.make_async_copy(mine, outs[a].at[me], local_sems.at[a]))
            for k in range(1, N_DEV):
                peer, peer_idx = _peer(x, y, c, k)
                src = ins[a] if self.gather else ins[a].at[peer_idx]
                sem = a * (N_DEV - 1) + k - 1
                remote.append(pltpu.make_async_remote_copy(
                    src_ref=src, dst_ref=outs[a].at[me], send_sem=send_sems.at[sem], recv_sem=recv_sems.at[sem],
                    device_id=peer, device_id_type=MESH))
        return local, remote

    def start(self, ins, outs, sems):
        local, remote = self._copies(ins, outs, sems)
        for cp in local + remote:
            cp.start()

    def wait(self, ins, outs, sems):
        local, remote = self._copies(ins, outs, sems)
        for cp in remote:
            cp.wait_send()
        for cp in remote:
            cp.wait_recv()
        for cp in local:
            cp.wait()

    def split(self, refs):
        return refs[:self.n], refs[self.n:2 * self.n], refs[2 * self.n:]


def _all_to_all(arrays, gather, name):
    ex = _Exchange(arrays, gather)

    def body(*refs):
        ins, outs, sems = ex.split(refs)
        ex.start(ins, outs, sems)
        ex.wait(ins, outs, sems)

    return pl.pallas_call(body, name=name, in_specs=ex.specs, out_specs=ex.specs, out_shape=ex.out_shape,
                          scratch_shapes=ex.scratch)(*ex.arrays)


def _reduce_adamw(parts, w, m, v, rows):
    R, C = w.shape

    def body(p_ref, w_ref, m_ref, v_ref, g_ref, d_ref, nm_ref, nv_ref):
        g = p_ref[0].astype(F32)
        for s in range(1, N_DEV):
            g = g + p_ref[s].astype(F32)
        g_ref[...] = g
        m_new = ADAM_B1 * m_ref[...] + (1.0 - ADAM_B1) * g
        v_new = ADAM_B2 * v_ref[...] + (1.0 - ADAM_B2) * (g * g)
        m_hat = m_new / (1.0 - ADAM_B1 ** ADAM_STEP)
        v_hat = v_new / (1.0 - ADAM_B2 ** ADAM_STEP)
        d_ref[...] = -ADAM_LR * (m_hat / (jnp.sqrt(v_hat) + ADAM_EPS) + ADAM_WD * w_ref[...])
        nm_ref[...] = m_new
        nv_ref[...] = v_new

    row = pl.BlockSpec((rows, C), lambda i: (i, 0))
    return pl.pallas_call(
        body, name="reduce_adamw", grid=(R // rows,),
        in_specs=[pl.BlockSpec((N_DEV, rows, C), lambda i: (0, i, 0)), row, row, row],
        out_specs=[row] * 4, out_shape=[jax.ShapeDtypeStruct((R, C), F32)] * 4,
        compiler_params=_params("parallel"),
    )(parts, w, m, v)


def _row_tile(rows, cols):
    fits = [t for t in range(8, rows + 1, 8) if rows % t == 0 and N_DEV * t * cols * 4 <= 4 * 1024 * 1024]
    return max(fits) if fits else rows


SMALL_COLS = 1024


def _pack_small(vals):
    rows = []
    for a in vals:
        flat = a.reshape(-1)
        pad = (-flat.shape[0]) % SMALL_COLS
        rows.append(jnp.pad(flat, (0, pad)).reshape(-1, SMALL_COLS))
    packed = jnp.concatenate(rows, axis=0)
    return jnp.pad(packed, ((0, (-packed.shape[0]) % 8), (0, 0)))


def _unpack_small(packed, like):
    out, r = [], 0
    for a in like:
        n = a.size
        nr = -(-n // SMALL_COLS)
        out.append(packed[r:r + nr].reshape(-1)[:n].reshape(a.shape))
        r += nr
    return out


def kernel(x, norm_mix_pre, w_in, w_pool, pool_scale, attn_scale, w_out, norm_mix_post, norm_ffn_pre, w_up, conv_w, conv_b, w_down, norm_ffn_post, loss_target, m_norm_mix_pre, m_w_in, m_w_pool, m_pool_scale, m_attn_scale, m_w_out, m_norm_mix_post, m_norm_ffn_pre, m_w_up, m_conv_w, m_conv_b, m_w_down, m_norm_ffn_post, v_norm_mix_pre, v_w_in, v_w_pool, v_pool_scale, v_attn_scale, v_w_out, v_norm_mix_post, v_norm_ffn_pre, v_w_up, v_conv_w, v_conv_b, v_w_down, v_norm_ffn_post):
    S, D = x.shape[1], x.shape[2]
    d_ff_block = w_up.shape[2]

    xs, target = x[0], loss_target[0]
    g1, g2, g3, g4 = norm_mix_pre, norm_mix_post, norm_ffn_pre, norm_ffn_post
    big = min(512, S)
    small = min(256, S)
    n_pairs = pool_scale.shape[1] // QB
    conv_b_g = conv_b.reshape(N_DEV, 1, d_ff_block)

    (w_in_g,) = _all_to_all([w_in[0].astype(BF16)], gather=True, name="gather_w_in")
    h1_t, u, qkv = _fwd_inproj(xs, g1, w_in_g, big)
    pool_out = _fwd_pool(u, w_pool[0], big)
    attn_out, (w_out_g, w_up_g, w_down_g, conv_w_g) = _fwd_attn(
        qkv, n_pairs, _Exchange([w_out[0].astype(BF16), w_up[0].astype(BF16), w_down[0].astype(BF16), conv_w[0]], gather=True),
        min(ATTN_FWD_BLOCKS, S // QB))
    w_out_full = w_out_g.reshape(D, D)
    w_down4 = w_down_g.reshape(D_FF_SHARDS, d_ff_block, D)
    mix, x2, h2, h2_t = _fwd_outproj(pool_out, attn_out, pool_scale, attn_scale, w_out_full, xs, g2, g3, big)
    upre, gate_val, dy, df, loss_cols, dg4 = _fwd_ffn_loss(h2, w_up_g, conv_w_g, conv_b_g, w_down4, x2, target, g4, small)
    loss = lax.psum(0.5 * jnp.sum(loss_cols) / D, ("x", "y", "c"))

    dupre_g, dupre_v, d_wd4, d_wup, d_cb, d_cw = _bwd_ffn_blocks(gate_val, upre, conv_w_g, w_down4, df, h2_t, min(1024, S))
    dx2, dmix, dg3, dg2 = _bwd_ffn_tokens(dupre_g, dupre_v, w_up_g.transpose(0, 2, 1), x2, dy, mix, g2, g3, big)
    d_pool, d_attn, d_wout, d_ps, d_as = _bwd_outproj(dmix, w_out_full, pool_out, attn_out, pool_scale, attn_scale, big)
    d_wdown_g = d_wd4.reshape(N_DEV, w_down.shape[1], D)
    d_wout_g = d_wout.reshape(N_DEV, D // N_DEV, D)
    dq, dk, dv, late_parts = _bwd_attn(qkv, d_attn, n_pairs, _Exchange([d_wout_g, d_wup, d_wdown_g, d_cw], gather=False),
                                       min(ATTN_BWD_BLOCKS, S // QB))
    du, d_wp = _bwd_pool(u, d_pool, w_pool[0], big)
    dproj, d_win = _bwd_w_in(du, dq, dk, dv, h1_t, N_DEV, big)
    w_in_t = w_in_g.transpose(0, 2, 1).reshape(-1, D)
    dx, dg1, (win_parts,) = _bwd_x(dproj, w_in_t, xs, dx2, g1, big, _Exchange([d_win], gather=False))
    big_parts = [win_parts] + list(late_parts)
    r = dict(dx=dx, g1=dg1, w_pool=d_wp, pool_scale=d_ps, attn_scale=d_as, g2=dg2, g3=dg3, conv_b=d_cb, g4=dg4)

    small_names = ["norm_mix_pre", "w_pool", "pool_scale", "attn_scale", "norm_mix_post", "norm_ffn_pre", "conv_b", "norm_ffn_post"]
    small_w = dict(norm_mix_pre=norm_mix_pre, w_pool=w_pool, pool_scale=pool_scale, attn_scale=attn_scale,
                   norm_mix_post=norm_mix_post, norm_ffn_pre=norm_ffn_pre, conv_b=conv_b, norm_ffn_post=norm_ffn_post)
    small_m = dict(norm_mix_pre=m_norm_mix_pre, w_pool=m_w_pool, pool_scale=m_pool_scale, attn_scale=m_attn_scale,
                   norm_mix_post=m_norm_mix_post, norm_ffn_pre=m_norm_ffn_pre, conv_b=m_conv_b, norm_ffn_post=m_norm_ffn_post)
    small_v = dict(norm_mix_pre=v_norm_mix_pre, w_pool=v_w_pool, pool_scale=v_pool_scale, attn_scale=v_attn_scale,
                   norm_mix_post=v_norm_mix_post, norm_ffn_pre=v_norm_ffn_pre, conv_b=v_conv_b, norm_ffn_post=v_norm_ffn_post)
    small_g = dict(norm_mix_pre=r["g1"], w_pool=r["w_pool"], pool_scale=r["pool_scale"], attn_scale=r["attn_scale"],
                   norm_mix_post=r["g2"], norm_ffn_pre=r["g3"], conv_b=r["conv_b"], norm_ffn_post=r["g4"])
    like = [small_w[n] for n in small_names]
    packed_g = _pack_small([small_g[n] for n in small_names])

    (small_parts,) = _all_to_all([packed_g], gather=True, name="gather_small_grads")

    def update(parts, w, m, v):
        R, C = w.shape
        return _reduce_adamw(parts, w, m, v, _row_tile(R, C))

    res = {}
    res["w_in"] = update(big_parts[0], w_in[0], m_w_in[0], v_w_in[0])
    res["w_out"] = update(big_parts[1], w_out[0], m_w_out[0], v_w_out[0])
    res["w_up"] = update(big_parts[2], w_up[0], m_w_up[0], v_w_up[0])
    res["w_down"] = update(big_parts[3], w_down[0], m_w_down[0], v_w_down[0])
    res["conv_w"] = update(big_parts[4], conv_w[0], m_conv_w[0], v_conv_w[0])
    small_res = update(small_parts, _pack_small(like), _pack_small([small_m[n] for n in small_names]),
                       _pack_small([small_v[n] for n in small_names]))
    small_res = [_unpack_small(t, like) for t in small_res]
    for idx, n in enumerate(small_names):
        res[n] = tuple(t[idx] for t in small_res)

    order = ["norm_mix_pre", "w_in", "w_pool", "pool_scale", "attn_scale", "w_out", "norm_mix_post", "norm_ffn_pre",
             "w_up", "conv_w", "conv_b", "w_down", "norm_ffn_post"]
    shaped = {n: tuple(t.reshape(s.shape) for t in res[n])
              for n, s in dict(norm_mix_pre=norm_mix_pre, w_in=w_in, w_pool=w_pool, pool_scale=pool_scale, attn_scale=attn_scale,
                               w_out=w_out, norm_mix_post=norm_mix_post, norm_ffn_pre=norm_ffn_pre, w_up=w_up, conv_w=conv_w,
                               conv_b=conv_b, w_down=w_down, norm_ffn_post=norm_ffn_post).items()}
    outs = [loss, r["dx"].reshape(x.shape)]
    for k in range(4):
        outs += [shaped[n][k] for n in order]
    return tuple(outs)
```

```python
import functools

import jax
import jax.numpy as jnp
from jax import lax
from jax.experimental import pallas as pl
from jax.experimental.pallas import tpu as pltpu

F32 = jnp.float32
BF16 = jnp.bfloat16
HIGHEST = lax.Precision.HIGHEST

N_DEV = 8
EPS = 1e-6
POOL_WINDOWS = (2, 4, 8, 16)
POOL_GROUP = 128
HALO = 16
HEAD_DIM = 64
QB = 128
ATTN_SCALE = HEAD_DIM ** -0.5
ATTN_FWD_BLOCKS = 8
ATTN_BWD_BLOCKS = 4
EXP_UNDERFLOW = -88.0
D_FF_SHARDS = 4

ADAM_LR = 0.001
ADAM_B1 = 0.9
ADAM_B2 = 0.999
ADAM_EPS = 1e-08
ADAM_WD = 0.01
ADAM_STEP = 10

VMEM_LIMIT_V7X = 56 * 1024 * 1024
MESH = pl.DeviceIdType.MESH


def _params(*semantics):
    return pltpu.CompilerParams(dimension_semantics=semantics, vmem_limit_bytes=VMEM_LIMIT_V7X)


def _const(shape):
    zeros = (0,) * len(shape)
    return pl.BlockSpec(shape, lambda *_: zeros, pipeline_mode=pl.Buffered(1))


def _dot(a, b):
    return jnp.dot(a, b, preferred_element_type=F32)


def _dot_nt(a, b):
    return lax.dot_general(a, b, (((1,), (1,)), ((), ())), preferred_element_type=F32)


def _dot_tn(a, b):
    return lax.dot_general(a, b, (((0,), (0,)), ((), ())), preferred_element_type=F32)


def _rms(v):
    return lax.rsqrt(jnp.mean(v * v, axis=-1, keepdims=True) + EPS)


def _norm_bwd(dn_times_gain, n, r):
    return r * (dn_times_gain - n * jnp.mean(dn_times_gain * n, axis=-1, keepdims=True))


def _zero_when(first, *refs):
    @pl.when(first)
    def _():
        for ref in refs:
            ref[...] = jnp.zeros_like(ref)


def _colsum(v):
    return jnp.sum(v, axis=0, keepdims=True)


def _grid_ends(grid):
    ids = [pl.program_id(a) for a in range(len(grid))]
    first = functools.reduce(jnp.logical_and, [i == 0 for i in ids])
    last = functools.reduce(jnp.logical_and, [i == n - 1 for i, n in zip(ids, grid)])
    return first, last


def _fwd_inproj(x, g1, w_in_g, tile):
    S, D = x.shape
    nb, _, cs = w_in_g.shape
    d_pool = 2 * cs

    def body(x_ref, g_ref, w_ref, ht_ref, u_ref, qkv_ref):
        xf = x_ref[...]
        h = (xf * _rms(xf) * g_ref[...]).astype(BF16)
        ht_ref[...] = h.T
        for d in range(nb):
            o = _dot(h, w_ref[d])
            if d < 2:
                u_ref[:, d * cs:(d + 1) * cs] = o
            else:
                qkv_ref[:, (d - 2) * cs:(d - 1) * cs] = o.astype(BF16)

    return pl.pallas_call(
        body, name="fwd_inproj", grid=(S // tile,),
        in_specs=[pl.BlockSpec((tile, D), lambda i: (i, 0)), _const((1, D)), _const(w_in_g.shape)],
        out_specs=[pl.BlockSpec((D, tile), lambda i: (0, i)), pl.BlockSpec((tile, d_pool), lambda i: (i, 0)),
                   pl.BlockSpec((tile, 3 * d_pool), lambda i: (i, 0))],
        out_shape=[jax.ShapeDtypeStruct((D, S), BF16), jax.ShapeDtypeStruct((S, d_pool), F32),
                   jax.ShapeDtypeStruct((S, 3 * d_pool), BF16)],
        compiler_params=_params("parallel"),
    )(x, g1, w_in_g)


def _window_sums(ext, forward):
    n = ext.shape[0]
    sums, s, sh = {}, ext, 1
    while sh < POOL_WINDOWS[-1]:
        s = s + pltpu.roll(s, (n - sh) if forward else sh, axis=0)
        sh *= 2
        sums[sh] = s
    return sums


def _pool_counts(t0, rows):
    t1 = (lax.broadcasted_iota(jnp.int32, (rows, 1), 0) + t0 + 1).astype(F32)
    return [jnp.minimum(t1, float(w)) for w in POOL_WINDOWS]


def _pool_deviation(u, halo, t0):
    T = u.shape[0]
    sums = _window_sums(jnp.concatenate([halo, u], axis=0), forward=False)
    counts = _pool_counts(t0, T)
    parts = []
    for g, w in enumerate(POOL_WINDOWS):
        lanes = slice(g * POOL_GROUP, (g + 1) * POOL_GROUP)
        parts.append(sums[w][HALO:, lanes] / counts[g] - u[:, lanes])
    return parts


def _prev_halo_spec(tile, width):
    return pl.BlockSpec((HALO, width), lambda i: (jnp.maximum(i * (tile // HALO) - 1, 0), 0))


def _next_halo_spec(tile, width, n_tiles):
    last = n_tiles * (tile // HALO) - 1
    return pl.BlockSpec((HALO, width), lambda i: (jnp.minimum((i + 1) * (tile // HALO), last), 0))


def _fwd_pool(u, w_pool, tile):
    S, C = u.shape

    def body(u_ref, halo_ref, wp_ref, o_ref):
        i = pl.program_id(0)
        halo = jnp.where(i > 0, halo_ref[...], 0.0)
        parts = _pool_deviation(u_ref[...], halo, i * tile)
        for g, p in enumerate(parts):
            o_ref[:, g * POOL_GROUP:(g + 1) * POOL_GROUP] = _dot(p.astype(BF16), wp_ref[g].astype(BF16))

    return pl.pallas_call(
        body, name="fwd_pool", grid=(S // tile,),
        in_specs=[pl.BlockSpec((tile, C), lambda i: (i, 0)), _prev_halo_spec(tile, C), _const(w_pool.shape)],
        out_specs=pl.BlockSpec((tile, C), lambda i: (i, 0)),
        out_shape=jax.ShapeDtypeStruct((S, C), F32),
        compiler_params=_params("parallel"),
    )(u, u, w_pool)


def _low_lanes():
    return lax.broadcasted_iota(jnp.int32, (QB, 2 * HEAD_DIM), 1) < HEAD_DIM


LOG_PIECES = 2
GRAD_PIECES = 3


def _triangle(inclusive, pieces):
    row = lax.broadcasted_iota(jnp.int32, (pieces * QB, QB), 0) % QB
    col = lax.broadcasted_iota(jnp.int32, (pieces * QB, QB), 1)
    return ((row >= col) if inclusive else (row > col)).astype(BF16)


def _pieces(v, n):
    out, rest = [], v
    for _ in range(n - 1):
        piece = rest.astype(BF16)
        out.append(piece)
        rest = rest - piece.astype(F32)
    out.append(rest.astype(BF16))
    return jnp.concatenate(out, axis=1)


def _causal_mask(width, offset):
    row = lax.broadcasted_iota(jnp.int32, (QB, width), 0)
    col = lax.broadcasted_iota(jnp.int32, (QB, width), 1)
    return col < row + offset


def _row_sums(vals, carry):
    for b in reversed(range(vals.shape[1] // QB)):
        carry = carry + jnp.sum(vals[:, b * QB:(b + 1) * QB], axis=1, keepdims=True)
    return carry


def _suffix_sums(vals, tri, carry):
    n = vals.shape[1] // QB
    out, run = [None] * n, carry
    for b in reversed(range(n)):
        blk = vals[:, b * QB:(b + 1) * QB]
        out[b] = _dot(_pieces(blk, tri.shape[0] // QB), tri) + run
        run = run + jnp.sum(blk, axis=1, keepdims=True)
    return (out[0] if n == 1 else jnp.concatenate(out, axis=1)), run


def _attn_tiles(qhs, kws, masks, carries, after_s):
    return _attn_weights(_attn_scores(qhs, kws, masks), masks, carries, after_s)


def _attn_scores(qhs, kws, masks):
    zs = [_dot_nt(qh, kw) * ATTN_SCALE for qh, kw in zip(qhs, kws)]
    es = [jnp.exp(-jnp.abs(z)) for z in zs]
    softplus = [jnp.maximum(z, 0.0) + jnp.log(1.0 + e) for z, e in zip(zs, es)]
    log_1m_beta = [-sp if m is None else jnp.where(m, -sp, 0.0) for sp, m in zip(softplus, masks)]
    return list(zip(zs, es, softplus, log_1m_beta))


def _attn_weights(scores, masks, carries, after_s):
    sums = [_suffix_sums(l, after_s, c) for (_, _, _, l), c in zip(scores, carries)]
    weights = [jnp.exp(z - sp + st) for (z, _, sp, _), (st, _) in zip(scores, sums)]
    weights = [a if m is None else jnp.where(m, a, 0.0) for a, m in zip(weights, masks)]
    return [(z, e, a, c) for (z, e, _, _), a, (_, c) in zip(scores, weights, sums)]


def _split_heads(v, low_lanes):
    return jnp.where(low_lanes, v, 0.0).astype(BF16), jnp.where(low_lanes, 0.0, v).astype(BF16)


def _sweep_done(c0, c1):
    return (jnp.maximum(jnp.max(c0), jnp.max(c1)) < EXP_UNDERFLOW).astype(jnp.int32)


def _first_window(i):
    first_blk = jnp.maximum(i - 1, 0)
    return first_blk, pl.multiple_of(first_blk * QB, QB), (i - first_blk) * QB


def _fwd_attn(qkv, n_pairs, ex, subs):
    S = qkv.shape[0]
    n_steps = S // (subs * QB)

    def body(q_ref, k_ref, v_ref, *rest):
        o_ref = rest[ex.n]
        ex_refs = ex.split(rest[:ex.n] + rest[ex.n + 1:])
        first_step, last_step = _grid_ends((n_pairs, n_steps))

        @pl.when(first_step)
        def _():
            ex.start(*ex_refs)

        low_lanes = _low_lanes()
        after_s = _triangle(False, LOG_PIECES)
        zero = jnp.zeros((QB, 1), F32)

        def cond(c):
            return jnp.logical_and(c[0] >= 0, c[1] == 0)

        qhs, kws, vws, masks, first_blks = [], [], [], [], []
        for sub in range(subs):
            i = pl.program_id(1) * subs + sub
            first_blk, start, offset = _first_window(i)
            first_blks.append(first_blk)
            qhs += _split_heads(q_ref[sub * QB:(sub + 1) * QB, :].astype(F32), low_lanes)
            kws += [k_ref[pl.ds(start, 2 * QB), :]] * 2
            vws += [v_ref[pl.ds(start, 2 * QB), :]] * 2
            masks += [_causal_mask(2 * QB, offset)] * 2
        tiles = _attn_tiles(qhs, kws, masks, [zero] * len(qhs), after_s)
        outs = [_dot(t[2].astype(BF16), vw) for t, vw in zip(tiles, vws)]

        for sub in range(subs):
            def step(c, qh=qhs[2 * sub:2 * sub + 2]):
                j, _, acc, c0, c1 = c
                at = pl.multiple_of(j * QB, QB)
                kb = k_ref[pl.ds(at, QB), :]
                vb = v_ref[pl.ds(at, QB), :]
                far = _attn_tiles(qh, [kb, kb], [None, None], [c0, c1], after_s)
                acc = acc + jnp.where(low_lanes, _dot(far[0][2].astype(BF16), vb), _dot(far[1][2].astype(BF16), vb))
                return j - 1, _sweep_done(far[0][3], far[1][3]), acc, far[0][3], far[1][3]

            c0, c1 = tiles[2 * sub][3], tiles[2 * sub + 1][3]
            init = (first_blks[sub] - 1, _sweep_done(c0, c1), jnp.where(low_lanes, outs[2 * sub], outs[2 * sub + 1]), c0, c1)
            o_ref[sub * QB:(sub + 1) * QB, :] = lax.while_loop(cond, step, init)[2]

        @pl.when(last_step)
        def _():
            ex.wait(*ex_refs)

    outs = pl.pallas_call(
        body, name="fwd_attn", grid=(n_pairs, n_steps),
        in_specs=[pl.BlockSpec((subs * QB, QB), lambda p, i: (i, p)),
                  pl.BlockSpec((S, QB), lambda p, i: (0, n_pairs + p), pipeline_mode=pl.Buffered(1)),
                  pl.BlockSpec((S, QB), lambda p, i: (0, 2 * n_pairs + p), pipeline_mode=pl.Buffered(1))] + ex.specs,
        out_specs=[pl.BlockSpec((subs * QB, QB), lambda p, i: (i, p))] + ex.specs,
        out_shape=[jax.ShapeDtypeStruct((S, n_pairs * QB), F32)] + ex.out_shape,
        scratch_shapes=ex.scratch,
        compiler_params=_params("arbitrary", "arbitrary"),
    )(qkv, qkv, qkv, *ex.arrays)
    return outs[0], outs[1:]


def _normalized_heads(pool_out, attn_out):
    rp, ra = _rms(pool_out), _rms(attn_out)
    return pool_out * rp, rp, attn_out * ra, ra


def _fwd_outproj(pool_out, attn_out, pool_scale, attn_scale, w_out, x, g2, g3, tile):
    S, D = x.shape
    C = pool_out.shape[1]

    def body(p_ref, a_ref, ps_ref, as_ref, w_ref, x_ref, g2_ref, g3_ref, mix_ref, x2_ref, h2_ref, h2t_ref):
        n_p, _, n_a, _ = _normalized_heads(p_ref[...], a_ref[...])
        mix = _dot((n_p * ps_ref[...]).astype(BF16), w_ref[:C, :]) + _dot((n_a * as_ref[...]).astype(BF16), w_ref[C:, :])
        mix_ref[...] = mix
        x2 = x_ref[...] + mix * _rms(mix) * g2_ref[...]
        x2_ref[...] = x2
        h2 = (x2 * _rms(x2) * g3_ref[...]).astype(BF16)
        h2_ref[...] = h2
        h2t_ref[...] = h2.T

    row = lambda w: pl.BlockSpec((tile, w), lambda i: (i, 0))
    return pl.pallas_call(
        body, name="fwd_outproj", grid=(S // tile,),
        in_specs=[row(C), row(C), _const((1, C)), _const((1, C)), _const(w_out.shape), row(D), _const((1, D)), _const((1, D))],
        out_specs=[row(D), row(D), row(D), pl.BlockSpec((D, tile), lambda i: (0, i))],
        out_shape=[jax.ShapeDtypeStruct((S, D), F32), jax.ShapeDtypeStruct((S, D), F32), jax.ShapeDtypeStruct((S, D), BF16),
                   jax.ShapeDtypeStruct((D, S), BF16)],
        compiler_params=_params("parallel"),
    )(pool_out, attn_out, pool_scale, attn_scale, w_out, x, g2, g3)


def _conv_taps(tile_rows, halo_rows):
    T = tile_rows.shape[0]
    ext = jnp.concatenate([halo_rows.astype(F32), tile_rows.astype(F32)], axis=0)
    return pltpu.roll(ext, 2, axis=0)[HALO:], pltpu.roll(ext, 1, axis=0)[HALO:], ext[HALO:]


def _tap_rows(cw_ref, d):
    return [cw_ref[d, k:k + 1, :] for k in range(3)]


def _gated_unit(taps_gate, taps_val, cw_gate, cw_val, cb_gate, cb_val):
    gate = cw_gate[0] * taps_gate[0] + cw_gate[1] * taps_gate[1] + cw_gate[2] * taps_gate[2] + cb_gate
    val = cw_val[0] * taps_val[0] + cw_val[1] * taps_val[1] + cw_val[2] * taps_val[2] + cb_val
    sig = 1.0 / (1.0 + jnp.exp(-gate))
    return gate, val, sig


def _fwd_ffn_loss(h2, w_up_g, conv_w_g, conv_b_g, w_down4, x2, target, g4, tile):
    S, D = x2.shape
    nb, _, cs = w_up_g.shape
    half = D_FF_SHARDS

    def body(h_ref, w_ref, cw_ref, cb_ref, wd_ref, x2_ref, t_ref, g4_ref, upre_ref, gv_ref, dy_ref, df_ref, loss_ref, dg4_ref, halo_ref):
        _zero_when(pl.program_id(0) == 0, loss_ref, dg4_ref, halo_ref)
        h = h_ref[...]

        def up(s):
            return _dot(h, w_ref[s]), _dot(h, w_ref[s + half])

        f = jnp.zeros((tile, D), F32)
        ahead = up(0)
        for s in range(half):
            ug, uv = ahead
            if s + 1 < half:
                ahead = up(s + 1)
            upre_ref[s] = ug.astype(BF16)
            upre_ref[s + half] = uv.astype(BF16)
            gate, val, sig = _gated_unit(_conv_taps(ug, halo_ref[s]), _conv_taps(uv, halo_ref[s + half]),
                                         _tap_rows(cw_ref, s), _tap_rows(cw_ref, s + half), cb_ref[s], cb_ref[s + half])
            halo_ref[s] = ug[tile - HALO:, :]
            halo_ref[s + half] = uv[tile - HALO:, :]
            gv_ref[s] = gate.astype(BF16)
            gv_ref[s + half] = val.astype(BF16)
            f = f + _dot((gate * sig * val).astype(BF16), wd_ref[s])
        r4 = _rms(f)
        n4 = f * r4
        err = x2_ref[...] + n4 * g4_ref[...] - t_ref[...]
        dy = err * (1.0 / D)
        dy_ref[...] = dy
        df_ref[...] = _norm_bwd(dy * g4_ref[...], n4, r4).astype(BF16)
        loss_ref[...] += _colsum(err * err)
        dg4_ref[...] += _colsum(dy * n4)

    row = lambda w: pl.BlockSpec((tile, w), lambda i: (i, 0))
    return pl.pallas_call(
        body, name="fwd_ffn_loss", grid=(S // tile,),
        in_specs=[row(D), _const(w_up_g.shape), _const(conv_w_g.shape), _const(conv_b_g.shape), _const(w_down4.shape),
                  row(D), row(D), _const((1, D))],
        out_specs=[pl.BlockSpec((nb, tile, cs), lambda i: (0, i, 0)), pl.BlockSpec((nb, tile, cs), lambda i: (0, i, 0)), row(D), row(D),
                   pl.BlockSpec((1, D), lambda i: (0, 0)), pl.BlockSpec((1, D), lambda i: (0, 0))],
        out_shape=[jax.ShapeDtypeStruct((nb, S, cs), BF16), jax.ShapeDtypeStruct((nb, S, cs), BF16),
                   jax.ShapeDtypeStruct((S, D), F32), jax.ShapeDtypeStruct((S, D), BF16),
                   jax.ShapeDtypeStruct((1, D), F32), jax.ShapeDtypeStruct((1, D), F32)],
        scratch_shapes=[pltpu.VMEM((nb, HALO, cs), F32)],
        compiler_params=_params("arbitrary"),
    )(h2, w_up_g, conv_w_g, conv_b_g, w_down4, x2, target, g4)


def _bwd_down(upre, conv_w_g, conv_b_g, w_down4, df, tile):
    nb, S, cs = upre.shape
    D = df.shape[1]
    n_tiles = S // tile

    def body(ug_ref, uv_ref, hg_ref, hv_ref, cwg_ref, cwv_ref, cbg_ref, cbv_ref, wd_ref, df_ref,
             dg_ref, dv_ref, dwd_ref, dbg_ref, dbv_ref, dcwg_ref, dcwv_ref):
        i = pl.program_id(1)
        first = i == 0
        _zero_when(first, dwd_ref, dbg_ref, dbv_ref, dcwg_ref, dcwv_ref)
        halo_g = jnp.where(first, jnp.zeros_like(hg_ref[0]), hg_ref[0])
        halo_v = jnp.where(first, jnp.zeros_like(hv_ref[0]), hv_ref[0])
        taps_g, taps_v = _conv_taps(ug_ref[0], halo_g), _conv_taps(uv_ref[0], halo_v)
        gate, val, sig = _gated_unit(taps_g, taps_v, _tap_rows(cwg_ref, 0), _tap_rows(cwv_ref, 0), cbg_ref[0], cbv_ref[0])
        silu = gate * sig
        dfb = df_ref[...]
        dact = _dot_nt(dfb, wd_ref[0])
        dwd_ref[0] += _dot_tn((silu * val).astype(BF16), dfb)
        dgate = dact * val * (sig * (1.0 + gate * (1.0 - sig)))
        dval = dact * silu
        dg_ref[0] = dgate.astype(BF16)
        dv_ref[0] = dval.astype(BF16)
        dbg_ref[0] += _colsum(dgate)
        dbv_ref[0] += _colsum(dval)
        for k in range(3):
            dcwg_ref[0, k:k + 1, :] += _colsum(dgate * taps_g[k])
            dcwv_ref[0, k:k + 1, :] += _colsum(dval * taps_v[k])

    half = D_FF_SHARDS
    blk = lambda off: pl.BlockSpec((1, tile, cs), lambda s, i: (s + off, i, 0))
    halo = lambda off: pl.BlockSpec((1, HALO, cs), lambda s, i: (s + off, jnp.maximum(i * (tile // HALO) - 1, 0), 0))
    par = lambda off, r: pl.BlockSpec((1, r, cs), lambda s, i: (s + off, 0, 0))
    outs = pl.pallas_call(
        body, name="bwd_down", grid=(half, n_tiles),
        in_specs=[blk(0), blk(half), halo(0), halo(half), par(0, 3), par(half, 3), par(0, 1), par(half, 1),
                  pl.BlockSpec((1, cs, D), lambda s, i: (s, 0, 0)), pl.BlockSpec((tile, D), lambda s, i: (i, 0))],
        out_specs=[blk(0), blk(0), pl.BlockSpec((1, cs, D), lambda s, i: (s, 0, 0)),
                   par(0, 1), par(0, 1), par(0, 3), par(0, 3)],
        out_shape=[jax.ShapeDtypeStruct((half, S, cs), BF16), jax.ShapeDtypeStruct((half, S, cs), BF16),
                   jax.ShapeDtypeStruct((half, cs, D), F32),
                   jax.ShapeDtypeStruct((half, 1, cs), F32), jax.ShapeDtypeStruct((half, 1, cs), F32),
                   jax.ShapeDtypeStruct((half, 3, cs), F32), jax.ShapeDtypeStruct((half, 3, cs), F32)],
        compiler_params=_params("parallel", "arbitrary"),
    )(upre, upre, upre, upre, conv_w_g, conv_w_g, conv_b_g, conv_b_g, w_down4, df)
    dgate, dval, d_wd, dbg, dbv, dcwg, dcwv = outs
    return dgate, dval, d_wd, jnp.concatenate([dbg, dbv], axis=0), jnp.concatenate([dcwg, dcwv], axis=0)


def _bwd_up_x(dgate, dval, conv_w_g, w_up_g, x2, dy, mix, g2, g3, tile):
    half, S, cs = dgate.shape
    nb = 2 * half
    D = x2.shape[1]
    n_tiles = S // tile

    def body(dg_ref, dv_ref, hg_ref, hv_ref, cw_ref, w_ref, x2_ref, dy_ref, mix_ref, g2_ref, g3_ref,
             dupre_ref, dx2_ref, dmix_ref, dg3_ref, dg2_ref):
        i = pl.program_id(0)
        last = i == n_tiles - 1
        _zero_when(i == 0, dg3_ref, dg2_ref)
        dh2 = jnp.zeros((tile, D), F32)
        for d in range(nb):
            src, halo = (dg_ref, hg_ref) if d < half else (dv_ref, hv_ref)
            nxt = jnp.where(last, jnp.zeros_like(halo[d % half]), halo[d % half])
            ext = jnp.concatenate([src[d % half].astype(F32), nxt.astype(F32)], axis=0)
            n = ext.shape[0]
            cw = _tap_rows(cw_ref, d)
            dupre = (cw[2] * ext + cw[1] * pltpu.roll(ext, n - 1, axis=0) + cw[0] * pltpu.roll(ext, n - 2, axis=0))[:tile]
            dupre = dupre.astype(BF16)
            dupre_ref[d] = dupre
            dh2 = dh2 + _dot_nt(dupre, w_ref[d])
        x2 = x2_ref[...]
        r3 = _rms(x2)
        n3 = x2 * r3
        dg3_ref[...] += _colsum(dh2 * n3)
        dx2 = dy_ref[...] + _norm_bwd(dh2 * g3_ref[...], n3, r3)
        dx2_ref[...] = dx2
        mix = mix_ref[...]
        r2 = _rms(mix)
        n2 = mix * r2
        dg2_ref[...] += _colsum(dx2 * n2)
        dmix_ref[...] = _norm_bwd(dx2 * g2_ref[...], n2, r2).astype(BF16)

    row = lambda w: pl.BlockSpec((tile, w), lambda i: (i, 0))
    blk = pl.BlockSpec((half, tile, cs), lambda i: (0, i, 0))
    last_halo = n_tiles * (tile // HALO) - 1
    halo = pl.BlockSpec((half, HALO, cs), lambda i: (0, jnp.minimum((i + 1) * (tile // HALO), last_halo), 0))
    acc = pl.BlockSpec((1, D), lambda i: (0, 0))
    return pl.pallas_call(
        body, name="bwd_up_x", grid=(n_tiles,),
        in_specs=[blk, blk, halo, halo, _const(conv_w_g.shape), _const(w_up_g.shape), row(D), row(D), row(D),
                  _const((1, D)), _const((1, D))],
        out_specs=[pl.BlockSpec((nb, tile, cs), lambda i: (0, i, 0)), row(D), row(D), acc, acc],
        out_shape=[jax.ShapeDtypeStruct((nb, S, cs), BF16), jax.ShapeDtypeStruct((S, D), F32),
                   jax.ShapeDtypeStruct((S, D), BF16), jax.ShapeDtypeStruct((1, D), F32), jax.ShapeDtypeStruct((1, D), F32)],
        compiler_params=_params("arbitrary"),
    )(dgate, dval, dgate, dval, conv_w_g, w_up_g, x2, dy, mix, g2, g3)


def _bwd_weight(act_t, dout, tile):
    D, S = act_t.shape
    nb, _, cs = dout.shape

    def body(a_ref, d_ref, o_ref):
        _zero_when(pl.program_id(1) == 0, o_ref)
        o_ref[0] += _dot(a_ref[...], d_ref[0])

    return pl.pallas_call(
        body, name="bwd_w_up", grid=(nb, S // tile),
        in_specs=[pl.BlockSpec((D, tile), lambda d, i: (0, i)), pl.BlockSpec((1, tile, cs), lambda d, i: (d, i, 0))],
        out_specs=pl.BlockSpec((1, D, cs), lambda d, i: (d, 0, 0)),
        out_shape=jax.ShapeDtypeStruct((nb, D, cs), F32),
        compiler_params=_params("parallel", "arbitrary"),
    )(act_t, dout)


def _bwd_ffn_blocks(gate_val, upre, conv_w_g, w_down4, df, h2_t, tile):
    nb, S, cs = upre.shape
    D = df.shape[1]
    n_tiles = S // tile
    half = D_FF_SHARDS

    def body(g_ref, v_ref, ug_ref, uv_ref, cwg_ref, cwv_ref, wd_ref, df_ref, ht_ref,
             dug_ref, duv_ref, dwd_ref, dwg_ref, dwv_ref, dbg_ref, dbv_ref, dcwg_ref, dcwv_ref, next_ref):
        _zero_when(pl.program_id(1) == 0, dwd_ref, dwg_ref, dwv_ref, dbg_ref, dbv_ref, dcwg_ref, dcwv_ref, next_ref)
        dfb = df_ref[...]
        dact = _dot_nt(dfb, wd_ref[0])
        gate, val = g_ref[0].astype(F32), v_ref[0].astype(F32)
        sig = 1.0 / (1.0 + jnp.exp(-gate))
        silu = gate * sig
        dwd_ref[0] += _dot_tn((silu * val).astype(BF16), dfb)
        ht = ht_ref[...]

        def through_conv(dup, slot, cw_ref, u_ref, du_ref, dw_ref, db_ref, dcw_ref):
            ext = jnp.concatenate([dup, next_ref[slot]], axis=0)
            n = ext.shape[0]
            shifted = (dup, pltpu.roll(ext, n - 1, axis=0)[:tile], pltpu.roll(ext, n - 2, axis=0)[:tile])
            next_ref[slot] = dup[:HALO]
            cw = _tap_rows(cw_ref, 0)
            dupre = (cw[2] * shifted[0] + cw[1] * shifted[1] + cw[0] * shifted[2]).astype(BF16)
            du_ref[0] = dupre
            dw_ref[0] += _dot(ht, dupre)
            u = u_ref[0].astype(F32)
            db_ref[0] += _colsum(dup)
            for k in range(3):
                dcw_ref[0, k:k + 1, :] += _colsum(shifted[2 - k] * u)

        through_conv(dact * val * (sig * (1.0 + gate * (1.0 - sig))), 0, cwg_ref, ug_ref, dug_ref, dwg_ref, dbg_ref, dcwg_ref)
        through_conv(dact * silu, 1, cwv_ref, uv_ref, duv_ref, dwv_ref, dbv_ref, dcwv_ref)

    rev = lambda i: n_tiles - 1 - i
    blk = lambda off: pl.BlockSpec((1, tile, cs), lambda s, i: (s + off, rev(i), 0))
    par = lambda off, r: pl.BlockSpec((1, r, cs), lambda s, i: (s + off, 0, 0))
    acc = lambda r, c: pl.BlockSpec((1, r, c), lambda s, i: (s, 0, 0), pipeline_mode=pl.Buffered(1))
    outs = pl.pallas_call(
        body, name="bwd_ffn_blocks", grid=(half, n_tiles),
        in_specs=[blk(0), blk(half), blk(0), blk(half), par(0, 3), par(half, 3),
                  acc(cs, D), pl.BlockSpec((tile, D), lambda s, i: (rev(i), 0)), pl.BlockSpec((D, tile), lambda s, i: (0, rev(i)))],
        out_specs=[blk(0), blk(0), acc(cs, D), acc(D, cs), acc(D, cs), acc(1, cs), acc(1, cs), acc(3, cs), acc(3, cs)],
        out_shape=[jax.ShapeDtypeStruct((half, S, cs), BF16), jax.ShapeDtypeStruct((half, S, cs), BF16),
                   jax.ShapeDtypeStruct((half, cs, D), F32),
                   jax.ShapeDtypeStruct((half, D, cs), F32), jax.ShapeDtypeStruct((half, D, cs), F32),
                   jax.ShapeDtypeStruct((half, 1, cs), F32), jax.ShapeDtypeStruct((half, 1, cs), F32),
                   jax.ShapeDtypeStruct((half, 3, cs), F32), jax.ShapeDtypeStruct((half, 3, cs), F32)],
        scratch_shapes=[pltpu.VMEM((2, HALO, cs), F32)],
        compiler_params=_params("arbitrary", "arbitrary"),
    )(gate_val, gate_val, upre, upre, conv_w_g, conv_w_g, w_down4, df, h2_t)
    dupre_g, dupre_v, d_wd, d_wg, d_wv, dbg, dbv, dcwg, dcwv = outs
    return (dupre_g, dupre_v, d_wd, jnp.concatenate([d_wg, d_wv], axis=0), jnp.concatenate([dbg, dbv], axis=0),
            jnp.concatenate([dcwg, dcwv], axis=0))


def _bwd_ffn_tokens(dupre_g, dupre_v, w_up_g, x2, dy, mix, g2, g3, tile):
    half, S, cs = dupre_g.shape
    D = x2.shape[1]

    def body(dg_ref, dv_ref, w_ref, x2_ref, dy_ref, mix_ref, g2_ref, g3_ref, dx2_ref, dmix_ref, dg3_ref, dg2_ref):
        _zero_when(pl.program_id(0) == 0, dg3_ref, dg2_ref)
        parts = [_dot_nt(dg_ref[d], w_ref[d]) for d in range(half)] + [_dot_nt(dv_ref[d], w_ref[d + half]) for d in range(half)]
        while len(parts) > 1:
            parts = [a + b for a, b in zip(parts[::2], parts[1::2])]
        dh2 = parts[0]
        x2 = x2_ref[...]
        r3 = _rms(x2)
        n3 = x2 * r3
        dg3_ref[...] += _colsum(dh2 * n3)
        dx2 = dy_ref[...] + _norm_bwd(dh2 * g3_ref[...], n3, r3)
        dx2_ref[...] = dx2
        mix = mix_ref[...]
        r2 = _rms(mix)
        n2 = mix * r2
        dg2_ref[...] += _colsum(dx2 * n2)
        dmix_ref[...] = _norm_bwd(dx2 * g2_ref[...], n2, r2).astype(BF16)

    row = lambda w: pl.BlockSpec((tile, w), lambda i: (i, 0))
    blk = pl.BlockSpec((half, tile, cs), lambda i: (0, i, 0))
    acc = pl.BlockSpec((1, D), lambda i: (0, 0))
    return pl.pallas_call(
        body, name="bwd_ffn_tokens", grid=(S // tile,),
        in_specs=[blk, blk, _const(w_up_g.shape), row(D), row(D), row(D), _const((1, D)), _const((1, D))],
        out_specs=[row(D), row(D), acc, acc],
        out_shape=[jax.ShapeDtypeStruct((S, D), F32), jax.ShapeDtypeStruct((S, D), BF16),
                   jax.ShapeDtypeStruct((1, D), F32), jax.ShapeDtypeStruct((1, D), F32)],
        compiler_params=_params("arbitrary"),
    )(dupre_g, dupre_v, w_up_g, x2, dy, mix, g2, g3)


def _bwd_outproj(dmix, w_out, pool_out, attn_out, pool_scale, attn_scale, tile):
    S, D = dmix.shape
    C = pool_out.shape[1]

    def body(dm_ref, w_ref, p_ref, a_ref, ps_ref, as_ref, dp_ref, da_ref, dw_ref, dps_ref, das_ref):
        _zero_when(pl.program_id(0) == 0, dw_ref, dps_ref, das_ref)
        dmx = dm_ref[...]
        dmerged = _dot_nt(dmx, w_ref[...])
        n_p, r_p, n_a, r_a = _normalized_heads(p_ref[...], a_ref[...])
        merged = jnp.concatenate([(n_p * ps_ref[...]).astype(BF16), (n_a * as_ref[...]).astype(BF16)], axis=1)
        dw_ref[...] += _dot_tn(merged, dmx)
        dm_p, dm_a = dmerged[:, :C], dmerged[:, C:]
        dps_ref[...] += _colsum(dm_p * n_p)
        das_ref[...] += _colsum(dm_a * n_a)
        dp_ref[...] = _norm_bwd(dm_p * ps_ref[...], n_p, r_p)
        da_ref[...] = _norm_bwd(dm_a * as_ref[...], n_a, r_a)

    row = lambda w: pl.BlockSpec((tile, w), lambda i: (i, 0))
    return pl.pallas_call(
        body, name="bwd_outproj", grid=(S // tile,),
        in_specs=[row(D), _const(w_out.shape), row(C), row(C), _const((1, C)), _const((1, C))],
        out_specs=[row(C), row(C), pl.BlockSpec(w_out.shape, lambda i: (0, 0)),
                   pl.BlockSpec((1, C), lambda i: (0, 0)), pl.BlockSpec((1, C), lambda i: (0, 0))],
        out_shape=[jax.ShapeDtypeStruct((S, C), F32), jax.ShapeDtypeStruct((S, C), F32),
                   jax.ShapeDtypeStruct(w_out.shape, F32), jax.ShapeDtypeStruct((1, C), F32), jax.ShapeDtypeStruct((1, C), F32)],
        compiler_params=_params("arbitrary"),
    )(dmix, w_out, pool_out, attn_out, pool_scale, attn_scale)


def _bwd_attn(qkv, d_attn, n_pairs, ex, subs):
    S = qkv.shape[0]
    n_steps = S // (subs * QB)

    def body(q_ref, k_ref, v_ref, do_ref, *rest):
        dq_ref, dk_ref, dv_ref = rest[ex.n:ex.n + 3]
        ex_refs = ex.split(rest[:ex.n] + rest[ex.n + 3:])
        first_step, last_step = _grid_ends((n_pairs, n_steps))

        @pl.when(first_step)
        def _():
            ex.start(*ex_refs)

        @pl.when(pl.program_id(1) == 0)
        def _():
            dk_ref[...] = jnp.zeros_like(dk_ref)
            dv_ref[...] = jnp.zeros_like(dv_ref)

        low_lanes = _low_lanes()
        after_s, from_s = _triangle(False, LOG_PIECES), _triangle(True, GRAD_PIECES)
        zero = jnp.zeros((QB, 1), F32)

        def tiles(qhs, dohs, totals, kws, vws, masks, cs, gs, scores=None):
            fw = _attn_weights(scores or _attn_scores(qhs, kws, masks), masks, cs, after_s)
            gvals = [t[2] * _dot_nt(doh, vw) for t, doh, vw in zip(fw, dohs, vws)]
            sums = [_suffix_sums(g, from_s, g0) for g, g0 in zip(gvals, gs)]
            totals = [tot if m is None else tot + sm[1] for tot, m, sm in zip(totals, masks, sums)]
            dzs = []
            for (z, e, _, _), g, (nearer, _), tot, m in zip(fw, gvals, sums, totals, masks):
                inv = 1.0 / (1.0 + e)
                sig_abs, sig_neg = inv, e * inv
                pos = z >= 0.0
                dz = g * jnp.where(pos, sig_neg, sig_abs) - jnp.where(pos, sig_abs, sig_neg) * (tot - nearer)
                if m is not None:
                    dz = jnp.where(m, dz, 0.0)
                dzs.append((dz * ATTN_SCALE).astype(BF16))
            dqs = [_dot(dz, kw) for dz, kw in zip(dzs, kws)]
            dks = [_dot_tn(dz, qh) for dz, qh in zip(dzs, qhs)]
            dvs = [_dot_tn(t[2].astype(BF16), doh) for t, doh in zip(fw, dohs)]
            return [(dq, dk, dv, t[3], sm[1], tot) for dq, dk, dv, t, sm, tot in zip(dqs, dks, dvs, fw, sums, totals)]

        def cond(c):
            return jnp.logical_and(c[0] >= 0, c[1] == 0)

        qhs, dohs, kws, vws, masks, first_blks, starts = [], [], [], [], [], [], []
        for sub in range(subs):
            i = pl.program_id(1) * subs + sub
            rows = slice(sub * QB, (sub + 1) * QB)
            first_blk, start, offset = _first_window(i)
            first_blks.append(first_blk)
            starts.append(start)
            qhs += _split_heads(q_ref[rows, :].astype(F32), low_lanes)
            dohs += _split_heads(do_ref[rows, :], low_lanes)
            kws += [k_ref[pl.ds(start, 2 * QB), :]] * 2
            vws += [v_ref[pl.ds(start, 2 * QB), :]] * 2
            masks += [_causal_mask(2 * QB, offset)] * 2
        zeros = [zero] * len(qhs)

        scores = _attn_scores(qhs, kws, masks)
        c_first = [_row_sums(sc[3], zero) for sc in scores]
        beyond_first = []
        for sub in range(subs):
            pair = slice(2 * sub, 2 * sub + 2)

            def far_sums(c, qh=qhs[pair], doh=dohs[pair]):
                j, _, c0, c1, r0, r1 = c
                at = pl.multiple_of(j * QB, QB)
                kb = k_ref[pl.ds(at, QB), :]
                vb = v_ref[pl.ds(at, QB), :]
                far = _attn_tiles(qh, [kb, kb], [None, None], [c0, c1], after_s)
                r0 = r0 + jnp.sum(far[0][2] * _dot_nt(doh[0], vb), axis=1, keepdims=True)
                r1 = r1 + jnp.sum(far[1][2] * _dot_nt(doh[1], vb), axis=1, keepdims=True)
                return j - 1, _sweep_done(far[0][3], far[1][3]), far[0][3], far[1][3], r0, r1

            c0, c1 = c_first[pair]
            far = lax.while_loop(cond, far_sums, (first_blks[sub] - 1, _sweep_done(c0, c1), c0, c1, zero, zero))
            beyond_first += [far[4], far[5]]

        done = tiles(qhs, dohs, beyond_first, kws, vws, masks, zeros, zeros, scores)
        for sub in range(subs):
            dk_ref[pl.ds(starts[sub], 2 * QB), :] += done[2 * sub][1] + done[2 * sub + 1][1]
            dv_ref[pl.ds(starts[sub], 2 * QB), :] += done[2 * sub][2] + done[2 * sub + 1][2]

        for sub in range(subs):
            pair = slice(2 * sub, 2 * sub + 2)
            t0, t1 = done[pair]

            def step(c, qh=qhs[pair], doh=dohs[pair], total=[t0[5], t1[5]]):
                j, _, dq, c0, c1, s0, s1 = c
                at = pl.multiple_of(j * QB, QB)
                kb = k_ref[pl.ds(at, QB), :]
                vb = v_ref[pl.ds(at, QB), :]
                f0, f1 = tiles(qh, doh, total, [kb, kb], [vb, vb], [None, None], [c0, c1], [s0, s1])
                dk_ref[pl.ds(at, QB), :] += f0[1] + f1[1]
                dv_ref[pl.ds(at, QB), :] += f0[2] + f1[2]
                return j - 1, _sweep_done(f0[3], f1[3]), dq + jnp.where(low_lanes, f0[0], f1[0]), f0[3], f1[3], f0[4], f1[4]

            init = (first_blks[sub] - 1, _sweep_done(t0[3], t1[3]), jnp.where(low_lanes, t0[0], t1[0]), t0[3], t1[3], t0[4], t1[4])
            dq_ref[sub * QB:(sub + 1) * QB, :] = lax.while_loop(cond, step, init)[2]

        @pl.when(last_step)
        def _():
            ex.wait(*ex_refs)

    blk = pl.BlockSpec((subs * QB, QB), lambda p, i: (i, p))
    full = lambda off: pl.BlockSpec((S, QB), lambda p, i: (0, off + p), pipeline_mode=pl.Buffered(1))
    outs = pl.pallas_call(
        body, name="bwd_attn", grid=(n_pairs, n_steps),
        in_specs=[blk, full(n_pairs), full(2 * n_pairs), blk] + ex.specs,
        out_specs=[blk, pl.BlockSpec((S, QB), lambda p, i: (0, p)), pl.BlockSpec((S, QB), lambda p, i: (0, p))] + ex.specs,
        out_shape=[jax.ShapeDtypeStruct((S, n_pairs * QB), F32)] * 3 + ex.out_shape,
        scratch_shapes=ex.scratch,
        compiler_params=_params("arbitrary", "arbitrary"),
    )(qkv, qkv, qkv, d_attn, *ex.arrays)
    return outs[0], outs[1], outs[2], outs[3:]


def _bwd_pool(u, d_pool, w_pool, tile):
    S, C = u.shape
    n_tiles = S // tile
    ng = len(POOL_WINDOWS)

    def body(u_ref, uh_ref, d_ref, dh_ref, wp_ref, du_ref, dwp_ref):
        i = pl.program_id(0)
        first = i == 0
        _zero_when(first, dwp_ref)
        halo = jnp.where(first, 0.0, uh_ref[...])
        parts = _pool_deviation(u_ref[...], halo, i * tile)
        dout = d_ref[...]
        nxt = jnp.where(i == n_tiles - 1, 0.0, dh_ref[...])
        dext = jnp.concatenate([dout, nxt], axis=0).astype(BF16)
        counts = _pool_counts(i * tile, tile + HALO)
        dps, scaled = [], []
        for g in range(ng):
            lanes = slice(g * POOL_GROUP, (g + 1) * POOL_GROUP)
            dp = _dot_nt(dext[:, lanes], wp_ref[g].astype(BF16))
            dps.append(dp[:tile])
            scaled.append(dp / counts[g])
        sums = _window_sums(jnp.concatenate(scaled, axis=1), forward=True)
        for g, w in enumerate(POOL_WINDOWS):
            lanes = slice(g * POOL_GROUP, (g + 1) * POOL_GROUP)
            du_ref[:, lanes] = sums[w][:tile, lanes] - dps[g]
            dwp_ref[g] += _dot_tn(parts[g].astype(BF16), dext[:tile, lanes])

    row = pl.BlockSpec((tile, C), lambda i: (i, 0))
    return pl.pallas_call(
        body, name="bwd_pool", grid=(n_tiles,),
        in_specs=[row, _prev_halo_spec(tile, C), row, _next_halo_spec(tile, C, n_tiles), _const(w_pool.shape)],
        out_specs=[row, pl.BlockSpec(w_pool.shape, lambda i: (0, 0, 0))],
        out_shape=[jax.ShapeDtypeStruct((S, C), F32), jax.ShapeDtypeStruct(w_pool.shape, F32)],
        compiler_params=_params("arbitrary"),
    )(u, u, d_pool, d_pool, w_pool)


def _bwd_w_in(du, dq, dk, dv, h1_t, n_blocks, tile):
    D, S = h1_t.shape
    C = du.shape[1]
    cs = 4 * C // n_blocks
    per = C // cs

    def body(du_ref, dq_ref, dk_ref, dv_ref, ht_ref, dproj_ref, dw_ref):
        _zero_when(pl.program_id(0) == 0, dw_ref)
        ht = ht_ref[...]
        for d in range(n_blocks):
            src = (du_ref, dq_ref, dk_ref, dv_ref)[d // per]
            dproj = src[:, (d % per) * cs:(d % per + 1) * cs].astype(BF16)
            dproj_ref[:, d * cs:(d + 1) * cs] = dproj
            dw_ref[d] += _dot(ht, dproj)

    row = lambda w: pl.BlockSpec((tile, w), lambda i: (i, 0))
    return pl.pallas_call(
        body, name="bwd_w_in", grid=(S // tile,),
        in_specs=[row(C), row(C), row(C), row(C), pl.BlockSpec((D, tile), lambda i: (0, i))],
        out_specs=[row(4 * C), pl.BlockSpec((n_blocks, D, cs), lambda i: (0, 0, 0))],
        out_shape=[jax.ShapeDtypeStruct((S, 4 * C), BF16), jax.ShapeDtypeStruct((n_blocks, D, cs), F32)],
        compiler_params=_params("arbitrary"),
    )(du, dq, dk, dv, h1_t)


def _bwd_x(dproj, w_in_t, x, dx2, g1, tile, ex):
    S, D = x.shape
    n_tiles = S // tile

    def body(dp_ref, w_ref, x_ref, dx2_ref, g_ref, *rest):
        dx_ref, dg_ref = rest[ex.n:ex.n + 2]
        ex_refs = ex.split(rest[:ex.n] + rest[ex.n + 2:])
        first, last = _grid_ends((n_tiles,))

        @pl.when(first)
        def _():
            ex.start(*ex_refs)
            dg_ref[...] = jnp.zeros_like(dg_ref)

        dh = _dot(dp_ref[...], w_ref[...])
        xf = x_ref[...]
        r1 = _rms(xf)
        n1 = xf * r1
        dg_ref[...] += _colsum(dh * n1)
        dx_ref[...] = dx2_ref[...] + _norm_bwd(dh * g_ref[...], n1, r1)

        @pl.when(last)
        def _():
            ex.wait(*ex_refs)

    row = lambda w: pl.BlockSpec((tile, w), lambda i: (i, 0))
    outs = pl.pallas_call(
        body, name="bwd_x", grid=(n_tiles,),
        in_specs=[row(w_in_t.shape[0]), _const(w_in_t.shape), row(D), row(D), _const((1, D))] + ex.specs,
        out_specs=[row(D), pl.BlockSpec((1, D), lambda i: (0, 0))] + ex.specs,
        out_shape=[jax.ShapeDtypeStruct((S, D), F32), jax.ShapeDtypeStruct((1, D), F32)] + ex.out_shape,
        scratch_shapes=ex.scratch,
        compiler_params=_params("arbitrary"),
    )(dproj, w_in_t, x, dx2, g1, *ex.arrays)
    return outs[0], outs[1], outs[2:]


def _mesh_position():
    x, y, c = lax.axis_index("x"), lax.axis_index("y"), lax.axis_index("c")
    return x, y, c, 4 * x + 2 * y + c


def _peer(x, y, c, k):
    px = 1 - x if k & 4 else x
    py = 1 - y if k & 2 else y
    pc = 1 - c if k & 1 else c
    return (px, py, pc), 4 * px + 2 * py + pc


class _Exchange:
    def __init__(self, arrays, gather):
        self.arrays, self.gather, self.n = list(arrays), gather, len(arrays)
        self.out_shape = [jax.ShapeDtypeStruct(((N_DEV,) + a.shape) if gather else a.shape, a.dtype) for a in arrays]
        self.specs = [pl.BlockSpec(memory_space=pl.ANY)] * self.n
        copies = self.n * (N_DEV - 1)
        self.scratch = [pltpu.SemaphoreType.DMA((copies,)), pltpu.SemaphoreType.DMA((copies,)),
                        pltpu.SemaphoreType.DMA((self.n,))]

    def _copies(self, ins, outs, sems):
        send_sems, recv_sems, local_sems = sems
        x, y, c, me = _mesh_position()
        local, remote = [], []
        for a in range(self.n):
            mine = ins[a] if self.gather else ins[a].at[me]
            local.append(pltpu.make_async_copy(mine, outs[a].at[me], local_sems.at[a]))
            for k in range(1, N_DEV):
                peer, peer_idx = _peer(x, y, c, k)
                src = ins[a] if self.gather else ins[a].at[peer_idx]
                sem = a * (N_DEV - 1) + k - 1
                remote.append(pltpu.make_async_remote_copy(
                    src_ref=src, dst_ref=outs[a].at[me], send_sem=send_sems.at[sem], recv_sem=recv_sems.at[sem],
                    device_id=peer, device_id_type=MESH))
        return local, remote

    def start(self, ins, outs, sems):
        local, remote = self._copies(ins, outs, sems)
        for cp in local + remote:
            cp.start()

    def wait(self, ins, outs, sems):
        local, remote = self._copies(ins, outs, sems)
        for cp in remote:
            cp.wait_send()
        for cp in remote:
            cp.wait_recv()
        for cp in local:
            cp.wait()

    def split(self, refs):
        return refs[:self.n], refs[self.n:2 * self.n], refs[2 * self.n:]


def _all_to_all(arrays, gather, name):
    ex = _Exchange(arrays, gather)

    def body(*refs):
        ins, outs, sems = ex.split(refs)
        ex.start(ins, outs, sems)
        ex.wait(ins, outs, sems)

    return pl.pallas_call(body, name=name, in_specs=ex.specs, out_specs=ex.specs, out_shape=ex.out_shape,
                          scratch_shapes=ex.scratch)(*ex.arrays)


def _reduce_adamw(parts, w, m, v, rows):
    R, C = w.shape

    def body(p_ref, w_ref, m_ref, v_ref, g_ref, d_ref, nm_ref, nv_ref):
        g = p_ref[0].astype(F32)
        for s in range(1, N_DEV):
            g = g + p_ref[s].astype(F32)
        g_ref[...] = g
        m_new = ADAM_B1 * m_ref[...] + (1.0 - ADAM_B1) * g
        v_new = ADAM_B2 * v_ref[...] + (1.0 - ADAM_B2) * (g * g)
        m_hat = m_new / (1.0 - ADAM_B1 ** ADAM_STEP)
        v_hat = v_new / (1.0 - ADAM_B2 ** ADAM_STEP)
        d_ref[...] = -ADAM_LR * (m_hat / (jnp.sqrt(v_hat) + ADAM_EPS) + ADAM_WD * w_ref[...])
        nm_ref[...] = m_new
        nv_ref[...] = v_new

    row = pl.BlockSpec((rows, C), lambda i: (i, 0))
    return pl.pallas_call(
        body, name="reduce_adamw", grid=(R // rows,),
        in_specs=[pl.BlockSpec((N_DEV, rows, C), lambda i: (0, i, 0)), row, row, row],
        out_specs=[row] * 4, out_shape=[jax.ShapeDtypeStruct((R, C), F32)] * 4,
        compiler_params=_params("parallel"),
    )(parts, w, m, v)


def _row_tile(rows, cols):
    fits = [t for t in range(8, rows + 1, 8) if rows % t == 0 and N_DEV * t * cols * 4 <= 4 * 1024 * 1024]
    return max(fits) if fits else rows


SMALL_COLS = 1024


def _pack_small(vals):
    rows = []
    for a in vals:
        flat = a.reshape(-1)
        pad = (-flat.shape[0]) % SMALL_COLS
        rows.append(jnp.pad(flat, (0, pad)).reshape(-1, SMALL_COLS))
    packed = jnp.concatenate(rows, axis=0)
    return jnp.pad(packed, ((0, (-packed.shape[0]) % 8), (0, 0)))


def _unpack_small(packed, like):
    out, r = [], 0
    for a in like:
        n = a.size
        nr = -(-n // SMALL_COLS)
        out.append(packed[r:r + nr].reshape(-1)[:n].reshape(a.shape))
        r += nr
    return out


def kernel(x, norm_mix_pre, w_in, w_pool, pool_scale, attn_scale, w_out, norm_mix_post, norm_ffn_pre, w_up, conv_w, conv_b, w_down, norm_ffn_post, loss_target, m_norm_mix_pre, m_w_in, m_w_pool, m_pool_scale, m_attn_scale, m_w_out, m_norm_mix_post, m_norm_ffn_pre, m_w_up, m_conv_w, m_conv_b, m_w_down, m_norm_ffn_post, v_norm_mix_pre, v_w_in, v_w_pool, v_pool_scale, v_attn_scale, v_w_out, v_norm_mix_post, v_norm_ffn_pre, v_w_up, v_conv_w, v_conv_b, v_w_down, v_norm_ffn_post):
    S, D = x.shape[1], x.shape[2]
    d_ff_block = w_up.shape[2]

    xs, target = x[0], loss_target[0]
    g1, g2, g3, g4 = norm_mix_pre, norm_mix_post, norm_ffn_pre, norm_ffn_post
    big = min(512, S)
    small = min(256, S)
    n_pairs = pool_scale.shape[1] // QB
    conv_b_g = conv_b.reshape(N_DEV, 1, d_ff_block)

    (w_in_g,) = _all_to_all([w_in[0].astype(BF16)], gather=True, name="gather_w_in")
    h1_t, u, qkv = _fwd_inproj(xs, g1, w_in_g, big)
    pool_out = _fwd_pool(u, w_pool[0], big)
    attn_out, (w_out_g, w_up_g, w_down_g, conv_w_g) = _fwd_attn(
        qkv, n_pairs, _Exchange([w_out[0].astype(BF16), w_up[0].astype(BF16), w_down[0].astype(BF16), conv_w[0]], gather=True),
        min(ATTN_FWD_BLOCKS, S // QB))
    w_out_full = w_out_g.reshape(D, D)
    w_down4 = w_down_g.reshape(D_FF_SHARDS, d_ff_block, D)
    mix, x2, h2, h2_t = _fwd_outproj(pool_out, attn_out, pool_scale, attn_scale, w_out_full, xs, g2, g3, big)
    upre, gate_val, dy, df, loss_cols, dg4 = _fwd_ffn_loss(h2, w_up_g, conv_w_g, conv_b_g, w_down4, x2, target, g4, small)
    loss = lax.psum(0.5 * jnp.sum(loss_cols) / D, ("x", "y", "c"))

    dupre_g, dupre_v, d_wd4, d_wup, d_cb, d_cw = _bwd_ffn_blocks(gate_val, upre, conv_w_g, w_down4, df, h2_t, min(1024, S))
    dx2, dmix, dg3, dg2 = _bwd_ffn_tokens(dupre_g, dupre_v, w_up_g, x2, dy, mix, g2, g3, big)
    d_pool, d_attn, d_wout, d_ps, d_as = _bwd_outproj(dmix, w_out_full, pool_out, attn_out, pool_scale, attn_scale, big)
    d_wdown_g = d_wd4.reshape(N_DEV, w_down.shape[1], D)
    d_wout_g = d_wout.reshape(N_DEV, D // N_DEV, D)
    dq, dk, dv, late_parts = _bwd_attn(qkv, d_attn, n_pairs, _Exchange([d_wout_g, d_wup, d_wdown_g, d_cw], gather=False),
                                       min(ATTN_BWD_BLOCKS, S // QB))
    du, d_wp = _bwd_pool(u, d_pool, w_pool[0], big)
    dproj, d_win = _bwd_w_in(du, dq, dk, dv, h1_t, N_DEV, big)
    w_in_t = w_in_g.transpose(0, 2, 1).reshape(-1, D)
    dx, dg1, (win_parts,) = _bwd_x(dproj, w_in_t, xs, dx2, g1, big, _Exchange([d_win], gather=False))
    big_parts = [win_parts] + list(late_parts)
    r = dict(dx=dx, g1=dg1, w_pool=d_wp, pool_scale=d_ps, attn_scale=d_as, g2=dg2, g3=dg3, conv_b=d_cb, g4=dg4)

    small_names = ["norm_mix_pre", "w_pool", "pool_scale", "attn_scale", "norm_mix_post", "norm_ffn_pre", "conv_b", "norm_ffn_post"]
    small_w = dict(norm_mix_pre=norm_mix_pre, w_pool=w_pool, pool_scale=pool_scale, attn_scale=attn_scale,
                   norm_mix_post=norm_mix_post, norm_ffn_pre=norm_ffn_pre, conv_b=conv_b, norm_ffn_post=norm_ffn_post)
    small_m = dict(norm_mix_pre=m_norm_mix_pre, w_pool=m_w_pool, pool_scale=m_pool_scale, attn_scale=m_attn_scale,
                   norm_mix_post=m_norm_mix_post, norm_ffn_pre=m_norm_ffn_pre, conv_b=m_conv_b, norm_ffn_post=m_norm_ffn_post)
    small_v = dict(norm_mix_pre=v_norm_mix_pre, w_pool=v_w_pool, pool_scale=v_pool_scale, attn_scale=v_attn_scale,
                   norm_mix_post=v_norm_mix_post, norm_ffn_pre=v_norm_ffn_pre, conv_b=v_conv_b, norm_ffn_post=v_norm_ffn_post)
    small_g = dict(norm_mix_pre=r["g1"], w_pool=r["w_pool"], pool_scale=r["pool_scale"], attn_scale=r["attn_scale"],
                   norm_mix_post=r["g2"], norm_ffn_pre=r["g3"], conv_b=r["conv_b"], norm_ffn_post=r["g4"])
    like = [small_w[n] for n in small_names]
    packed_g = _pack_small([small_g[n] for n in small_names])

    (small_parts,) = _all_to_all([packed_g], gather=True, name="gather_small_grads")

    def update(parts, w, m, v):
        R, C = w.shape
        return _reduce_adamw(parts, w, m, v, _row_tile(R, C))

    res = {}
    res["w_in"] = update(big_parts[0], w_in[0], m_w_in[0], v_w_in[0])
    res["w_out"] = update(big_parts[1], w_out[0], m_w_out[0], v_w_out[0])
    res["w_up"] = update(big_parts[2], w_up[0], m_w_up[0], v_w_up[0])
    res["w_down"] = update(big_parts[3], w_down[0], m_w_down[0], v_w_down[0])
    res["conv_w"] = update(big_parts[4], conv_w[0], m_conv_w[0], v_conv_w[0])
    small_res = update(small_parts, _pack_small(like), _pack_small([small_m[n] for n in small_names]),
                       _pack_small([small_v[n] for n in small_names]))
    small_res = [_unpack_small(t, like) for t in small_res]
    for idx, n in enumerate(small_names):
        res[n] = tuple(t[idx] for t in small_res)

    order = ["norm_mix_pre", "w_in", "w_pool", "pool_scale", "attn_scale", "w_out", "norm_mix_post", "norm_ffn_pre",
             "w_up", "conv_w", "conv_b", "w_down", "norm_ffn_post"]
    shaped = {n: tuple(t.reshape(s.shape) for t in res[n])
              for n, s in dict(norm_mix_pre=norm_mix_pre, w_in=w_in, w_pool=w_pool, pool_scale=pool_scale, attn_scale=attn_scale,
                               w_out=w_out, norm_mix_post=norm_mix_post, norm_ffn_pre=norm_ffn_pre, w_up=w_up, conv_w=conv_w,
                               conv_b=conv_b, w_down=w_down, norm_ffn_post=norm_ffn_post).items()}
    outs = [loss, r["dx"].reshape(x.shape)]
    for k in range(4):
        outs += [shaped[n][k] for n in order]
    return tuple(outs)
```

```python
import functools

import jax
import jax.numpy as jnp
from jax import lax
from jax.experimental import pallas as pl
from jax.experimental.pallas import tpu as pltpu

F32 = jnp.float32
BF16 = jnp.bfloat16
HIGHEST = lax.Precision.HIGHEST

N_DEV = 8
EPS = 1e-6
POOL_WINDOWS = (2, 4, 8, 16)
POOL_GROUP = 128
HALO = 16
HEAD_DIM = 64
QB = 128
ATTN_SCALE = HEAD_DIM ** -0.5
ATTN_FWD_BLOCKS = 8
ATTN_BWD_BLOCKS = 4
EXP_UNDERFLOW = -88.0
D_FF_SHARDS = 4

ADAM_LR = 0.001
ADAM_B1 = 0.9
ADAM_B2 = 0.999
ADAM_EPS = 1e-08
ADAM_WD = 0.01
ADAM_STEP = 10

VMEM_LIMIT_V7X = 56 * 1024 * 1024
MESH = pl.DeviceIdType.MESH


def _params(*semantics):
    return pltpu.CompilerParams(dimension_semantics=semantics, vmem_limit_bytes=VMEM_LIMIT_V7X)


def _const(shape):
    zeros = (0,) * len(shape)
    return pl.BlockSpec(shape, lambda *_: zeros, pipeline_mode=pl.Buffered(1))


def _dot(a, b):
    return jnp.dot(a, b, preferred_element_type=F32)


def _dot_nt(a, b):
    return lax.dot_general(a, b, (((1,), (1,)), ((), ())), preferred_element_type=F32)


def _dot_tn(a, b):
    return lax.dot_general(a, b, (((0,), (0,)), ((), ())), preferred_element_type=F32)


def _rms(v):
    return lax.rsqrt(jnp.mean(v * v, axis=-1, keepdims=True) + EPS)


def _norm_bwd(dn_times_gain, n, r):
    return r * (dn_times_gain - n * jnp.mean(dn_times_gain * n, axis=-1, keepdims=True))


def _zero_when(first, *refs):
    @pl.when(first)
    def _():
        for ref in refs:
            ref[...] = jnp.zeros_like(ref)


def _colsum(v):
    return jnp.sum(v, axis=0, keepdims=True)


def _grid_ends(grid):
    ids = [pl.program_id(a) for a in range(len(grid))]
    first = functools.reduce(jnp.logical_and, [i == 0 for i in ids])
    last = functools.reduce(jnp.logical_and, [i == n - 1 for i, n in zip(ids, grid)])
    return first, last


def _fwd_inproj(x, g1, w_in_g, tile):
    S, D = x.shape
    nb, _, cs = w_in_g.shape
    d_pool = 2 * cs

    def body(x_ref, g_ref, w_ref, ht_ref, u_ref, qkv_ref):
        xf = x_ref[...]
        h = (xf * _rms(xf) * g_ref[...]).astype(BF16)
        ht_ref[...] = h.T
        for d in range(nb):
            o = _dot(h, w_ref[d])
            if d < 2:
                u_ref[:, d * cs:(d + 1) * cs] = o
            else:
                qkv_ref[:, (d - 2) * cs:(d - 1) * cs] = o.astype(BF16)

    return pl.pallas_call(
        body, name="fwd_inproj", grid=(S // tile,),
        in_specs=[pl.BlockSpec((tile, D), lambda i: (i, 0)), _const((1, D)), _const(w_in_g.shape)],
        out_specs=[pl.BlockSpec((D, tile), lambda i: (0, i)), pl.BlockSpec((tile, d_pool), lambda i: (i, 0)),
                   pl.BlockSpec((tile, 3 * d_pool), lambda i: (i, 0))],
        out_shape=[jax.ShapeDtypeStruct((D, S), BF16), jax.ShapeDtypeStruct((S, d_pool), F32),
                   jax.ShapeDtypeStruct((S, 3 * d_pool), BF16)],
        compiler_params=_params("parallel"),
    )(x, g1, w_in_g)


def _window_sums(ext, forward):
    n = ext.shape[0]
    sums, s, sh = {}, ext, 1
    while sh < POOL_WINDOWS[-1]:
        s = s + pltpu.roll(s, (n - sh) if forward else sh, axis=0)
        sh *= 2
        sums[sh] = s
    return sums


def _pool_counts(t0, rows):
    t1 = (lax.broadcasted_iota(jnp.int32, (rows, 1), 0) + t0 + 1).astype(F32)
    return [jnp.minimum(t1, float(w)) for w in POOL_WINDOWS]


def _pool_deviation(u, halo, t0):
    T = u.shape[0]
    sums = _window_sums(jnp.concatenate([halo, u], axis=0), forward=False)
    counts = _pool_counts(t0, T)
    parts = []
    for g, w in enumerate(POOL_WINDOWS):
        lanes = slice(g * POOL_GROUP, (g + 1) * POOL_GROUP)
        parts.append(sums[w][HALO:, lanes] / counts[g] - u[:, lanes])
    return parts


def _prev_halo_spec(tile, width):
    return pl.BlockSpec((HALO, width), lambda i: (jnp.maximum(i * (tile // HALO) - 1, 0), 0))


def _next_halo_spec(tile, width, n_tiles):
    last = n_tiles * (tile // HALO) - 1
    return pl.BlockSpec((HALO, width), lambda i: (jnp.minimum((i + 1) * (tile // HALO), last), 0))


def _fwd_pool(u, w_pool, tile):
    S, C = u.shape

    def body(u_ref, halo_ref, wp_ref, o_ref):
        i = pl.program_id(0)
        halo = jnp.where(i > 0, halo_ref[...], 0.0)
        parts = _pool_deviation(u_ref[...], halo, i * tile)
        for g, p in enumerate(parts):
            o_ref[:, g * POOL_GROUP:(g + 1) * POOL_GROUP] = _dot(p.astype(BF16), wp_ref[g].astype(BF16))

    return pl.pallas_call(
        body, name="fwd_pool", grid=(S // tile,),
        in_specs=[pl.BlockSpec((tile, C), lambda i: (i, 0)), _prev_halo_spec(tile, C), _const(w_pool.shape)],
        out_specs=pl.BlockSpec((tile, C), lambda i: (i, 0)),
        out_shape=jax.ShapeDtypeStruct((S, C), F32),
        compiler_params=_params("parallel"),
    )(u, u, w_pool)


def _low_lanes():
    return lax.broadcasted_iota(jnp.int32, (QB, 2 * HEAD_DIM), 1) < HEAD_DIM


LOG_PIECES = 2
GRAD_PIECES = 3


def _triangle(inclusive, pieces):
    row = lax.broadcasted_iota(jnp.int32, (pieces * QB, QB), 0) % QB
    col = lax.broadcasted_iota(jnp.int32, (pieces * QB, QB), 1)
    return ((row >= col) if inclusive else (row > col)).astype(BF16)


def _pieces(v, n):
    out, rest = [], v
    for _ in range(n - 1):
        piece = rest.astype(BF16)
        out.append(piece)
        rest = rest - piece.astype(F32)
    out.append(rest.astype(BF16))
    return jnp.concatenate(out, axis=1)


def _causal_mask(width, offset):
    row = lax.broadcasted_iota(jnp.int32, (QB, width), 0)
    col = lax.broadcasted_iota(jnp.int32, (QB, width), 1)
    return col < row + offset


def _row_sums(vals, carry):
    for b in reversed(range(vals.shape[1] // QB)):
        carry = carry + jnp.sum(vals[:, b * QB:(b + 1) * QB], axis=1, keepdims=True)
    return carry


def _suffix_sums(vals, tri, carry):
    n = vals.shape[1] // QB
    out, run = [None] * n, carry
    for b in reversed(range(n)):
        blk = vals[:, b * QB:(b + 1) * QB]
        out[b] = _dot(_pieces(blk, tri.shape[0] // QB), tri) + run
        run = run + jnp.sum(blk, axis=1, keepdims=True)
    return (out[0] if n == 1 else jnp.concatenate(out, axis=1)), run


def _attn_tiles(qhs, kws, masks, carries, after_s):
    return _attn_weights(_attn_scores(qhs, kws, masks), masks, carries, after_s)


def _attn_scores(qhs, kws, masks):
    zs = [_dot_nt(qh, kw) * ATTN_SCALE for qh, kw in zip(qhs, kws)]
    es = [jnp.exp(-jnp.abs(z)) for z in zs]
    softplus = [jnp.maximum(z, 0.0) + jnp.log(1.0 + e) for z, e in zip(zs, es)]
    log_1m_beta = [-sp if m is None else jnp.where(m, -sp, 0.0) for sp, m in zip(softplus, masks)]
    return list(zip(zs, es, softplus, log_1m_beta))


def _attn_weights(scores, masks, carries, after_s):
    sums = [_suffix_sums(l, after_s, c) for (_, _, _, l), c in zip(scores, carries)]
    weights = [jnp.exp(z - sp + st) for (z, _, sp, _), (st, _) in zip(scores, sums)]
    weights = [a if m is None else jnp.where(m, a, 0.0) for a, m in zip(weights, masks)]
    return [(z, e, a, c) for (z, e, _, _), a, (_, c) in zip(scores, weights, sums)]


def _split_heads(v, low_lanes):
    return jnp.where(low_lanes, v, 0.0).astype(BF16), jnp.where(low_lanes, 0.0, v).astype(BF16)


def _sweep_done(c0, c1):
    return (jnp.maximum(jnp.max(c0), jnp.max(c1)) < EXP_UNDERFLOW).astype(jnp.int32)


def _all_done(carries):
    return jnp.max(functools.reduce(jnp.maximum, carries)) < EXP_UNDERFLOW


def _first_window(i):
    first_blk = jnp.maximum(i - 1, 0)
    return first_blk, pl.multiple_of(first_blk * QB, QB), (i - first_blk) * QB


def _fwd_attn(qkv, n_pairs, ex, subs):
    S = qkv.shape[0]
    n_steps = S // (subs * QB)

    def body(q_ref, k_ref, v_ref, *rest):
        o_ref = rest[ex.n]
        ex_refs = ex.split(rest[:ex.n] + rest[ex.n + 1:])
        first_step, last_step = _grid_ends((n_pairs, n_steps))

        @pl.when(first_step)
        def _():
            ex.start(*ex_refs)

        low_lanes = _low_lanes()
        after_s = _triangle(False, LOG_PIECES)
        zero = jnp.zeros((QB, 1), F32)

        def cond(c):
            return jnp.logical_and(c[0] >= 0, c[1] == 0)

        qhs, kws, vws, masks, first_blks = [], [], [], [], []
        for sub in range(subs):
            i = pl.program_id(1) * subs + sub
            first_blk, start, offset = _first_window(i)
            first_blks.append(first_blk)
            qhs += _split_heads(q_ref[sub * QB:(sub + 1) * QB, :].astype(F32), low_lanes)
            kws += [k_ref[pl.ds(start, 2 * QB), :]] * 2
            vws += [v_ref[pl.ds(start, 2 * QB), :]] * 2
            masks += [_causal_mask(2 * QB, offset)] * 2
        tiles = _attn_tiles(qhs, kws, masks, [zero] * len(qhs), after_s)
        outs = [_dot(t[2].astype(BF16), vw) for t, vw in zip(tiles, vws)]

        first_out = [jnp.where(low_lanes, outs[2 * sub], outs[2 * sub + 1]) for sub in range(subs)]

        def sweep_on():
            final = []
            for sub in range(subs):
                def step(c, qh=qhs[2 * sub:2 * sub + 2]):
                    j, _, acc, c0, c1 = c
                    at = pl.multiple_of(j * QB, QB)
                    kb = k_ref[pl.ds(at, QB), :]
                    vb = v_ref[pl.ds(at, QB), :]
                    far = _attn_tiles(qh, [kb, kb], [None, None], [c0, c1], after_s)
                    acc = acc + jnp.where(low_lanes, _dot(far[0][2].astype(BF16), vb), _dot(far[1][2].astype(BF16), vb))
                    return j - 1, _sweep_done(far[0][3], far[1][3]), acc, far[0][3], far[1][3]

                c0, c1 = tiles[2 * sub][3], tiles[2 * sub + 1][3]
                final.append(lax.while_loop(cond, step, (first_blks[sub] - 1, _sweep_done(c0, c1), first_out[sub], c0, c1))[2])
            return tuple(final)

        final = lax.cond(_all_done([t[3] for t in tiles]), lambda: tuple(first_out), sweep_on)
        for sub in range(subs):
            o_ref[sub * QB:(sub + 1) * QB, :] = final[sub]

        @pl.when(last_step)
        def _():
            ex.wait(*ex_refs)

    outs = pl.pallas_call(
        body, name="fwd_attn", grid=(n_pairs, n_steps),
        in_specs=[pl.BlockSpec((subs * QB, QB), lambda p, i: (i, p)),
                  pl.BlockSpec((S, QB), lambda p, i: (0, n_pairs + p), pipeline_mode=pl.Buffered(1)),
                  pl.BlockSpec((S, QB), lambda p, i: (0, 2 * n_pairs + p), pipeline_mode=pl.Buffered(1))] + ex.specs,
        out_specs=[pl.BlockSpec((subs * QB, QB), lambda p, i: (i, p))] + ex.specs,
        out_shape=[jax.ShapeDtypeStruct((S, n_pairs * QB), F32)] + ex.out_shape,
        scratch_shapes=ex.scratch,
        compiler_params=_params("arbitrary", "arbitrary"),
    )(qkv, qkv, qkv, *ex.arrays)
    return outs[0], outs[1:]


def _normalized_heads(pool_out, attn_out):
    rp, ra = _rms(pool_out), _rms(attn_out)
    return pool_out * rp, rp, attn_out * ra, ra


def _fwd_outproj(pool_out, attn_out, pool_scale, attn_scale, w_out, x, g2, g3, tile):
    S, D = x.shape
    C = pool_out.shape[1]

    def body(p_ref, a_ref, ps_ref, as_ref, w_ref, x_ref, g2_ref, g3_ref, mix_ref, x2_ref, h2_ref, h2t_ref):
        n_p, _, n_a, _ = _normalized_heads(p_ref[...], a_ref[...])
        mix = _dot((n_p * ps_ref[...]).astype(BF16), w_ref[:C, :]) + _dot((n_a * as_ref[...]).astype(BF16), w_ref[C:, :])
        mix_ref[...] = mix
        x2 = x_ref[...] + mix * _rms(mix) * g2_ref[...]
        x2_ref[...] = x2
        h2 = (x2 * _rms(x2) * g3_ref[...]).astype(BF16)
        h2_ref[...] = h2
        h2t_ref[...] = h2.T

    row = lambda w: pl.BlockSpec((tile, w), lambda i: (i, 0))
    return pl.pallas_call(
        body, name="fwd_outproj", grid=(S // tile,),
        in_specs=[row(C), row(C), _const((1, C)), _const((1, C)), _const(w_out.shape), row(D), _const((1, D)), _const((1, D))],
        out_specs=[row(D), row(D), row(D), pl.BlockSpec((D, tile), lambda i: (0, i))],
        out_shape=[jax.ShapeDtypeStruct((S, D), F32), jax.ShapeDtypeStruct((S, D), F32), jax.ShapeDtypeStruct((S, D), BF16),
                   jax.ShapeDtypeStruct((D, S), BF16)],
        compiler_params=_params("parallel"),
    )(pool_out, attn_out, pool_scale, attn_scale, w_out, x, g2, g3)


def _conv_taps(tile_rows, halo_rows):
    T = tile_rows.shape[0]
    ext = jnp.concatenate([halo_rows.astype(F32), tile_rows.astype(F32)], axis=0)
    return pltpu.roll(ext, 2, axis=0)[HALO:], pltpu.roll(ext, 1, axis=0)[HALO:], ext[HALO:]


def _tap_rows(cw_ref, d):
    return [cw_ref[d, k:k + 1, :] for k in range(3)]


def _gated_unit(taps_gate, taps_val, cw_gate, cw_val, cb_gate, cb_val):
    gate = cw_gate[0] * taps_gate[0] + cw_gate[1] * taps_gate[1] + cw_gate[2] * taps_gate[2] + cb_gate
    val = cw_val[0] * taps_val[0] + cw_val[1] * taps_val[1] + cw_val[2] * taps_val[2] + cb_val
    sig = 1.0 / (1.0 + jnp.exp(-gate))
    return gate, val, sig


def _fwd_ffn_loss(h2, w_up_g, conv_w_g, conv_b_g, w_down4, x2, target, g4, tile):
    S, D = x2.shape
    nb, _, cs = w_up_g.shape
    half = D_FF_SHARDS

    def body(h_ref, w_ref, cw_ref, cb_ref, wd_ref, x2_ref, t_ref, g4_ref, upre_ref, gv_ref, dy_ref, df_ref, loss_ref, dg4_ref, halo_ref):
        _zero_when(pl.program_id(0) == 0, loss_ref, dg4_ref, halo_ref)
        h = h_ref[...]

        def up(s):
            return _dot(h, w_ref[s]), _dot(h, w_ref[s + half])

        f = jnp.zeros((tile, D), F32)
        ahead = up(0)
        for s in range(half):
            ug, uv = ahead
            if s + 1 < half:
                ahead = up(s + 1)
            upre_ref[s] = ug.astype(BF16)
            upre_ref[s + half] = uv.astype(BF16)
            gate, val, sig = _gated_unit(_conv_taps(ug, halo_ref[s]), _conv_taps(uv, halo_ref[s + half]),
                                         _tap_rows(cw_ref, s), _tap_rows(cw_ref, s + half), cb_ref[s], cb_ref[s + half])
            halo_ref[s] = ug[tile - HALO:, :]
            halo_ref[s + half] = uv[tile - HALO:, :]
            gv_ref[s] = gate.astype(BF16)
            gv_ref[s + half] = val.astype(BF16)
            f = f + _dot((gate * sig * val).astype(BF16), wd_ref[s])
        r4 = _rms(f)
        n4 = f * r4
        err = x2_ref[...] + n4 * g4_ref[...] - t_ref[...]
        dy = err * (1.0 / D)
        dy_ref[...] = dy
        df_ref[...] = _norm_bwd(dy * g4_ref[...], n4, r4).astype(BF16)
        loss_ref[...] += _colsum(err * err)
        dg4_ref[...] += _colsum(dy * n4)

    row = lambda w: pl.BlockSpec((tile, w), lambda i: (i, 0))
    return pl.pallas_call(
        body, name="fwd_ffn_loss", grid=(S // tile,),
        in_specs=[row(D), _const(w_up_g.shape), _const(conv_w_g.shape), _const(conv_b_g.shape), _const(w_down4.shape),
                  row(D), row(D), _const((1, D))],
        out_specs=[pl.BlockSpec((nb, tile, cs), lambda i: (0, i, 0)), pl.BlockSpec((nb, tile, cs), lambda i: (0, i, 0)), row(D), row(D),
                   pl.BlockSpec((1, D), lambda i: (0, 0)), pl.BlockSpec((1, D), lambda i: (0, 0))],
        out_shape=[jax.ShapeDtypeStruct((nb, S, cs), BF16), jax.ShapeDtypeStruct((nb, S, cs), BF16),
                   jax.ShapeDtypeStruct((S, D), F32), jax.ShapeDtypeStruct((S, D), BF16),
                   jax.ShapeDtypeStruct((1, D), F32), jax.ShapeDtypeStruct((1, D), F32)],
        scratch_shapes=[pltpu.VMEM((nb, HALO, cs), F32)],
        compiler_params=_params("arbitrary"),
    )(h2, w_up_g, conv_w_g, conv_b_g, w_down4, x2, target, g4)


def _bwd_down(upre, conv_w_g, conv_b_g, w_down4, df, tile):
    nb, S, cs = upre.shape
    D = df.shape[1]
    n_tiles = S // tile

    def body(ug_ref, uv_ref, hg_ref, hv_ref, cwg_ref, cwv_ref, cbg_ref, cbv_ref, wd_ref, df_ref,
             dg_ref, dv_ref, dwd_ref, dbg_ref, dbv_ref, dcwg_ref, dcwv_ref):
        i = pl.program_id(1)
        first = i == 0
        _zero_when(first, dwd_ref, dbg_ref, dbv_ref, dcwg_ref, dcwv_ref)
        halo_g = jnp.where(first, jnp.zeros_like(hg_ref[0]), hg_ref[0])
        halo_v = jnp.where(first, jnp.zeros_like(hv_ref[0]), hv_ref[0])
        taps_g, taps_v = _conv_taps(ug_ref[0], halo_g), _conv_taps(uv_ref[0], halo_v)
        gate, val, sig = _gated_unit(taps_g, taps_v, _tap_rows(cwg_ref, 0), _tap_rows(cwv_ref, 0), cbg_ref[0], cbv_ref[0])
        silu = gate * sig
        dfb = df_ref[...]
        dact = _dot_nt(dfb, wd_ref[0])
        dwd_ref[0] += _dot_tn((silu * val).astype(BF16), dfb)
        dgate = dact * val * (sig * (1.0 + gate * (1.0 - sig)))
        dval = dact * silu
        dg_ref[0] = dgate.astype(BF16)
        dv_ref[0] = dval.astype(BF16)
        dbg_ref[0] += _colsum(dgate)
        dbv_ref[0] += _colsum(dval)
        for k in range(3):
            dcwg_ref[0, k:k + 1, :] += _colsum(dgate * taps_g[k])
            dcwv_ref[0, k:k + 1, :] += _colsum(dval * taps_v[k])

    half = D_FF_SHARDS
    blk = lambda off: pl.BlockSpec((1, tile, cs), lambda s, i: (s + off, i, 0))
    halo = lambda off: pl.BlockSpec((1, HALO, cs), lambda s, i: (s + off, jnp.maximum(i * (tile // HALO) - 1, 0), 0))
    par = lambda off, r: pl.BlockSpec((1, r, cs), lambda s, i: (s + off, 0, 0))
    outs = pl.pallas_call(
        body, name="bwd_down", grid=(half, n_tiles),
        in_specs=[blk(0), blk(half), halo(0), halo(half), par(0, 3), par(half, 3), par(0, 1), par(half, 1),
                  pl.BlockSpec((1, cs, D), lambda s, i: (s, 0, 0)), pl.BlockSpec((tile, D), lambda s, i: (i, 0))],
        out_specs=[blk(0), blk(0), pl.BlockSpec((1, cs, D), lambda s, i: (s, 0, 0)),
                   par(0, 1), par(0, 1), par(0, 3), par(0, 3)],
        out_shape=[jax.ShapeDtypeStruct((half, S, cs), BF16), jax.ShapeDtypeStruct((half, S, cs), BF16),
                   jax.ShapeDtypeStruct((half, cs, D), F32),
                   jax.ShapeDtypeStruct((half, 1, cs), F32), jax.ShapeDtypeStruct((half, 1, cs), F32),
                   jax.ShapeDtypeStruct((half, 3, cs), F32), jax.ShapeDtypeStruct((half, 3, cs), F32)],
        compiler_params=_params("parallel", "arbitrary"),
    )(upre, upre, upre, upre, conv_w_g, conv_w_g, conv_b_g, conv_b_g, w_down4, df)
    dgate, dval, d_wd, dbg, dbv, dcwg, dcwv = outs
    return dgate, dval, d_wd, jnp.concatenate([dbg, dbv], axis=0), jnp.concatenate([dcwg, dcwv], axis=0)


def _bwd_up_x(dgate, dval, conv_w_g, w_up_g, x2, dy, mix, g2, g3, tile):
    half, S, cs = dgate.shape
    nb = 2 * half
    D = x2.shape[1]
    n_tiles = S // tile

    def body(dg_ref, dv_ref, hg_ref, hv_ref, cw_ref, w_ref, x2_ref, dy_ref, mix_ref, g2_ref, g3_ref,
             dupre_ref, dx2_ref, dmix_ref, dg3_ref, dg2_ref):
        i = pl.program_id(0)
        last = i == n_tiles - 1
        _zero_when(i == 0, dg3_ref, dg2_ref)
        dh2 = jnp.zeros((tile, D), F32)
        for d in range(nb):
            src, halo = (dg_ref, hg_ref) if d < half else (dv_ref, hv_ref)
            nxt = jnp.where(last, jnp.zeros_like(halo[d % half]), halo[d % half])
            ext = jnp.concatenate([src[d % half].astype(F32), nxt.astype(F32)], axis=0)
            n = ext.shape[0]
            cw = _tap_rows(cw_ref, d)
            dupre = (cw[2] * ext + cw[1] * pltpu.roll(ext, n - 1, axis=0) + cw[0] * pltpu.roll(ext, n - 2, axis=0))[:tile]
            dupre = dupre.astype(BF16)
            dupre_ref[d] = dupre
            dh2 = dh2 + _dot_nt(dupre, w_ref[d])
        x2 = x2_ref[...]
        r3 = _rms(x2)
        n3 = x2 * r3
        dg3_ref[...] += _colsum(dh2 * n3)
        dx2 = dy_ref[...] + _norm_bwd(dh2 * g3_ref[...], n3, r3)
        dx2_ref[...] = dx2
        mix = mix_ref[...]
        r2 = _rms(mix)
        n2 = mix * r2
        dg2_ref[...] += _colsum(dx2 * n2)
        dmix_ref[...] = _norm_bwd(dx2 * g2_ref[...], n2, r2).astype(BF16)

    row = lambda w: pl.BlockSpec((tile, w), lambda i: (i, 0))
    blk = pl.BlockSpec((half, tile, cs), lambda i: (0, i, 0))
    last_halo = n_tiles * (tile // HALO) - 1
    halo = pl.BlockSpec((half, HALO, cs), lambda i: (0, jnp.minimum((i + 1) * (tile // HALO), last_halo), 0))
    acc = pl.BlockSpec((1, D), lambda i: (0, 0))
    return pl.pallas_call(
        body, name="bwd_up_x", grid=(n_tiles,),
        in_specs=[blk, blk, halo, halo, _const(conv_w_g.shape), _const(w_up_g.shape), row(D), row(D), row(D),
                  _const((1, D)), _const((1, D))],
        out_specs=[pl.BlockSpec((nb, tile, cs), lambda i: (0, i, 0)), row(D), row(D), acc, acc],
        out_shape=[jax.ShapeDtypeStruct((nb, S, cs), BF16), jax.ShapeDtypeStruct((S, D), F32),
                   jax.ShapeDtypeStruct((S, D), BF16), jax.ShapeDtypeStruct((1, D), F32), jax.ShapeDtypeStruct((1, D), F32)],
        compiler_params=_params("arbitrary"),
    )(dgate, dval, dgate, dval, conv_w_g, w_up_g, x2, dy, mix, g2, g3)


def _bwd_weight(act_t, dout, tile):
    D, S = act_t.shape
    nb, _, cs = dout.shape

    def body(a_ref, d_ref, o_ref):
        _zero_when(pl.program_id(1) == 0, o_ref)
        o_ref[0] += _dot(a_ref[...], d_ref[0])

    return pl.pallas_call(
        body, name="bwd_w_up", grid=(nb, S // tile),
        in_specs=[pl.BlockSpec((D, tile), lambda d, i: (0, i)), pl.BlockSpec((1, tile, cs), lambda d, i: (d, i, 0))],
        out_specs=pl.BlockSpec((1, D, cs), lambda d, i: (d, 0, 0)),
        out_shape=jax.ShapeDtypeStruct((nb, D, cs), F32),
        compiler_params=_params("parallel", "arbitrary"),
    )(act_t, dout)


def _bwd_ffn_blocks(gate_val, upre, conv_w_g, w_down4, df, h2_t, tile):
    nb, S, cs = upre.shape
    D = df.shape[1]
    n_tiles = S // tile
    half = D_FF_SHARDS

    def body(g_ref, v_ref, ug_ref, uv_ref, cwg_ref, cwv_ref, wd_ref, df_ref, ht_ref,
             dug_ref, duv_ref, dwd_ref, dwg_ref, dwv_ref, dbg_ref, dbv_ref, dcwg_ref, dcwv_ref, next_ref):
        _zero_when(pl.program_id(1) == 0, dwd_ref, dwg_ref, dwv_ref, dbg_ref, dbv_ref, dcwg_ref, dcwv_ref, next_ref)
        dfb = df_ref[...]
        dact = _dot_nt(dfb, wd_ref[0])
        gate, val = g_ref[0].astype(F32), v_ref[0].astype(F32)
        sig = 1.0 / (1.0 + jnp.exp(-gate))
        silu = gate * sig
        dwd_ref[0] += _dot_tn((silu * val).astype(BF16), dfb)
        ht = ht_ref[...]

        def through_conv(dup, slot, cw_ref, u_ref, du_ref, dw_ref, db_ref, dcw_ref):
            ext = jnp.concatenate([dup, next_ref[slot]], axis=0)
            n = ext.shape[0]
            shifted = (dup, pltpu.roll(ext, n - 1, axis=0)[:tile], pltpu.roll(ext, n - 2, axis=0)[:tile])
            next_ref[slot] = dup[:HALO]
            cw = _tap_rows(cw_ref, 0)
            dupre = (cw[2] * shifted[0] + cw[1] * shifted[1] + cw[0] * shifted[2]).astype(BF16)
            du_ref[0] = dupre
            dw_ref[0] += _dot(ht, dupre)
            u = u_ref[0].astype(F32)
            db_ref[0] += _colsum(dup)
            for k in range(3):
                dcw_ref[0, k:k + 1, :] += _colsum(shifted[2 - k] * u)

        through_conv(dact * val * (sig * (1.0 + gate * (1.0 - sig))), 0, cwg_ref, ug_ref, dug_ref, dwg_ref, dbg_ref, dcwg_ref)
        through_conv(dact * silu, 1, cwv_ref, uv_ref, duv_ref, dwv_ref, dbv_ref, dcwv_ref)

    rev = lambda i: n_tiles - 1 - i
    blk = lambda off: pl.BlockSpec((1, tile, cs), lambda s, i: (s + off, rev(i), 0))
    par = lambda off, r: pl.BlockSpec((1, r, cs), lambda s, i: (s + off, 0, 0))
    acc = lambda r, c: pl.BlockSpec((1, r, c), lambda s, i: (s, 0, 0), pipeline_mode=pl.Buffered(1))
    outs = pl.pallas_call(
        body, name="bwd_ffn_blocks", grid=(half, n_tiles),
        in_specs=[blk(0), blk(half), blk(0), blk(half), par(0, 3), par(half, 3),
                  acc(cs, D), pl.BlockSpec((tile, D), lambda s, i: (rev(i), 0)), pl.BlockSpec((D, tile), lambda s, i: (0, rev(i)))],
        out_specs=[blk(0), blk(0), acc(cs, D), acc(D, cs), acc(D, cs), acc(1, cs), acc(1, cs), acc(3, cs), acc(3, cs)],
        out_shape=[jax.ShapeDtypeStruct((half, S, cs), BF16), jax.ShapeDtypeStruct((half, S, cs), BF16),
                   jax.ShapeDtypeStruct((half, cs, D), F32),
                   jax.ShapeDtypeStruct((half, D, cs), F32), jax.ShapeDtypeStruct((half, D, cs), F32),
                   jax.ShapeDtypeStruct((half, 1, cs), F32), jax.ShapeDtypeStruct((half, 1, cs), F32),
                   jax.ShapeDtypeStruct((half, 3, cs), F32), jax.ShapeDtypeStruct((half, 3, cs), F32)],
        scratch_shapes=[pltpu.VMEM((2, HALO, cs), F32)],
        compiler_params=_params("arbitrary", "arbitrary"),
    )(gate_val, gate_val, upre, upre, conv_w_g, conv_w_g, w_down4, df, h2_t)
    dupre_g, dupre_v, d_wd, d_wg, d_wv, dbg, dbv, dcwg, dcwv = outs
    return (dupre_g, dupre_v, d_wd, jnp.concatenate([d_wg, d_wv], axis=0), jnp.concatenate([dbg, dbv], axis=0),
            jnp.concatenate([dcwg, dcwv], axis=0))


def _bwd_ffn_tokens(dupre_g, dupre_v, w_up_g, x2, dy, mix, g2, g3, tile):
    half, S, cs = dupre_g.shape
    D = x2.shape[1]

    def body(dg_ref, dv_ref, w_ref, x2_ref, dy_ref, mix_ref, g2_ref, g3_ref, dx2_ref, dmix_ref, dg3_ref, dg2_ref):
        _zero_when(pl.program_id(0) == 0, dg3_ref, dg2_ref)
        parts = [_dot_nt(dg_ref[d], w_ref[d]) for d in range(half)] + [_dot_nt(dv_ref[d], w_ref[d + half]) for d in range(half)]
        while len(parts) > 1:
            parts = [a + b for a, b in zip(parts[::2], parts[1::2])]
        dh2 = parts[0]
        x2 = x2_ref[...]
        r3 = _rms(x2)
        n3 = x2 * r3
        dg3_ref[...] += _colsum(dh2 * n3)
        dx2 = dy_ref[...] + _norm_bwd(dh2 * g3_ref[...], n3, r3)
        dx2_ref[...] = dx2
        mix = mix_ref[...]
        r2 = _rms(mix)
        n2 = mix * r2
        dg2_ref[...] += _colsum(dx2 * n2)
        dmix_ref[...] = _norm_bwd(dx2 * g2_ref[...], n2, r2).astype(BF16)

    row = lambda w: pl.BlockSpec((tile, w), lambda i: (i, 0))
    blk = pl.BlockSpec((half, tile, cs), lambda i: (0, i, 0))
    acc = pl.BlockSpec((1, D), lambda i: (0, 0))
    return pl.pallas_call(
        body, name="bwd_ffn_tokens", grid=(S // tile,),
        in_specs=[blk, blk, _const(w_up_g.shape), row(D), row(D), row(D), _const((1, D)), _const((1, D))],
        out_specs=[row(D), row(D), acc, acc],
        out_shape=[jax.ShapeDtypeStruct((S, D), F32), jax.ShapeDtypeStruct((S, D), BF16),
                   jax.ShapeDtypeStruct((1, D), F32), jax.ShapeDtypeStruct((1, D), F32)],
        compiler_params=_params("arbitrary"),
    )(dupre_g, dupre_v, w_up_g, x2, dy, mix, g2, g3)


def _bwd_outproj(dmix, w_out, pool_out, attn_out, pool_scale, attn_scale, tile):
    S, D = dmix.shape
    C = pool_out.shape[1]

    def body(dm_ref, w_ref, p_ref, a_ref, ps_ref, as_ref, dp_ref, da_ref, dw_ref, dps_ref, das_ref):
        _zero_when(pl.program_id(0) == 0, dw_ref, dps_ref, das_ref)
        dmx = dm_ref[...]
        dmerged = _dot_nt(dmx, w_ref[...])
        n_p, r_p, n_a, r_a = _normalized_heads(p_ref[...], a_ref[...])
        merged = jnp.concatenate([(n_p * ps_ref[...]).astype(BF16), (n_a * as_ref[...]).astype(BF16)], axis=1)
        dw_ref[...] += _dot_tn(merged, dmx)
        dm_p, dm_a = dmerged[:, :C], dmerged[:, C:]
        dps_ref[...] += _colsum(dm_p * n_p)
        das_ref[...] += _colsum(dm_a * n_a)
        dp_ref[...] = _norm_bwd(dm_p * ps_ref[...], n_p, r_p)
        da_ref[...] = _norm_bwd(dm_a * as_ref[...], n_a, r_a)

    row = lambda w: pl.BlockSpec((tile, w), lambda i: (i, 0))
    return pl.pallas_call(
        body, name="bwd_outproj", grid=(S // tile,),
        in_specs=[row(D), _const(w_out.shape), row(C), row(C), _const((1, C)), _const((1, C))],
        out_specs=[row(C), row(C), pl.BlockSpec(w_out.shape, lambda i: (0, 0)),
                   pl.BlockSpec((1, C), lambda i: (0, 0)), pl.BlockSpec((1, C), lambda i: (0, 0))],
        out_shape=[jax.ShapeDtypeStruct((S, C), F32), jax.ShapeDtypeStruct((S, C), F32),
                   jax.ShapeDtypeStruct(w_out.shape, F32), jax.ShapeDtypeStruct((1, C), F32), jax.ShapeDtypeStruct((1, C), F32)],
        compiler_params=_params("arbitrary"),
    )(dmix, w_out, pool_out, attn_out, pool_scale, attn_scale)


def _bwd_attn(qkv, d_attn, n_pairs, ex, subs):
    S = qkv.shape[0]
    n_steps = S // (subs * QB)

    def body(q_ref, k_ref, v_ref, do_ref, *rest):
        dq_ref, dk_ref, dv_ref = rest[ex.n:ex.n + 3]
        ex_refs = ex.split(rest[:ex.n] + rest[ex.n + 3:])
        first_step, last_step = _grid_ends((n_pairs, n_steps))

        @pl.when(first_step)
        def _():
            ex.start(*ex_refs)

        @pl.when(pl.program_id(1) == 0)
        def _():
            dk_ref[...] = jnp.zeros_like(dk_ref)
            dv_ref[...] = jnp.zeros_like(dv_ref)

        low_lanes = _low_lanes()
        after_s, from_s = _triangle(False, LOG_PIECES), _triangle(True, GRAD_PIECES)
        zero = jnp.zeros((QB, 1), F32)

        def tiles(qhs, dohs, totals, kws, vws, masks, cs, gs, scores=None):
            fw = _attn_weights(scores or _attn_scores(qhs, kws, masks), masks, cs, after_s)
            gvals = [t[2] * _dot_nt(doh, vw) for t, doh, vw in zip(fw, dohs, vws)]
            sums = [_suffix_sums(g, from_s, g0) for g, g0 in zip(gvals, gs)]
            totals = [tot if m is None else tot + sm[1] for tot, m, sm in zip(totals, masks, sums)]
            dzs = []
            for (z, e, _, _), g, (nearer, _), tot, m in zip(fw, gvals, sums, totals, masks):
                inv = 1.0 / (1.0 + e)
                sig_abs, sig_neg = inv, e * inv
                pos = z >= 0.0
                dz = g * jnp.where(pos, sig_neg, sig_abs) - jnp.where(pos, sig_abs, sig_neg) * (tot - nearer)
                if m is not None:
                    dz = jnp.where(m, dz, 0.0)
                dzs.append((dz * ATTN_SCALE).astype(BF16))
            dqs = [_dot(dz, kw) for dz, kw in zip(dzs, kws)]
            dks = [_dot_tn(dz, qh) for dz, qh in zip(dzs, qhs)]
            dvs = [_dot_tn(t[2].astype(BF16), doh) for t, doh in zip(fw, dohs)]
            return [(dq, dk, dv, t[3], sm[1], tot) for dq, dk, dv, t, sm, tot in zip(dqs, dks, dvs, fw, sums, totals)]

        def cond(c):
            return jnp.logical_and(c[0] >= 0, c[1] == 0)

        qhs, dohs, kws, vws, masks, first_blks, starts = [], [], [], [], [], [], []
        for sub in range(subs):
            i = pl.program_id(1) * subs + sub
            rows = slice(sub * QB, (sub + 1) * QB)
            first_blk, start, offset = _first_window(i)
            first_blks.append(first_blk)
            starts.append(start)
            qhs += _split_heads(q_ref[rows, :].astype(F32), low_lanes)
            dohs += _split_heads(do_ref[rows, :], low_lanes)
            kws += [k_ref[pl.ds(start, 2 * QB), :]] * 2
            vws += [v_ref[pl.ds(start, 2 * QB), :]] * 2
            masks += [_causal_mask(2 * QB, offset)] * 2
        zeros = [zero] * len(qhs)

        scores = _attn_scores(qhs, kws, masks)
        c_first = [_row_sums(sc[3], zero) for sc in scores]
        all_done = _all_done(c_first)

        def far_totals():
            beyond = []
            for sub in range(subs):
                pair = slice(2 * sub, 2 * sub + 2)

                def far_sums(c, qh=qhs[pair], doh=dohs[pair]):
                    j, _, c0, c1, r0, r1 = c
                    at = pl.multiple_of(j * QB, QB)
                    kb = k_ref[pl.ds(at, QB), :]
                    vb = v_ref[pl.ds(at, QB), :]
                    far = _attn_tiles(qh, [kb, kb], [None, None], [c0, c1], after_s)
                    r0 = r0 + jnp.sum(far[0][2] * _dot_nt(doh[0], vb), axis=1, keepdims=True)
                    r1 = r1 + jnp.sum(far[1][2] * _dot_nt(doh[1], vb), axis=1, keepdims=True)
                    return j - 1, _sweep_done(far[0][3], far[1][3]), far[0][3], far[1][3], r0, r1

                c0, c1 = c_first[pair]
                far = lax.while_loop(cond, far_sums, (first_blks[sub] - 1, _sweep_done(c0, c1), c0, c1, zero, zero))
                beyond += [far[4], far[5]]
            return tuple(beyond)

        beyond_first = list(lax.cond(all_done, lambda: tuple(zeros), far_totals))
        done = tiles(qhs, dohs, beyond_first, kws, vws, masks, zeros, zeros, scores)
        for sub in range(subs):
            dk_ref[pl.ds(starts[sub], 2 * QB), :] += done[2 * sub][1] + done[2 * sub + 1][1]
            dv_ref[pl.ds(starts[sub], 2 * QB), :] += done[2 * sub][2] + done[2 * sub + 1][2]
        first_dq = [jnp.where(low_lanes, done[2 * sub][0], done[2 * sub + 1][0]) for sub in range(subs)]

        def sweep_on():
            final = []
            for sub in range(subs):
                pair = slice(2 * sub, 2 * sub + 2)
                t0, t1 = done[pair]

                def step(c, qh=qhs[pair], doh=dohs[pair], total=[t0[5], t1[5]]):
                    j, _, dq, c0, c1, s0, s1 = c
                    at = pl.multiple_of(j * QB, QB)
                    kb = k_ref[pl.ds(at, QB), :]
                    vb = v_ref[pl.ds(at, QB), :]
                    f0, f1 = tiles(qh, doh, total, [kb, kb], [vb, vb], [None, None], [c0, c1], [s0, s1])
                    dk_ref[pl.ds(at, QB), :] += f0[1] + f1[1]
                    dv_ref[pl.ds(at, QB), :] += f0[2] + f1[2]
                    return j - 1, _sweep_done(f0[3], f1[3]), dq + jnp.where(low_lanes, f0[0], f1[0]), f0[3], f1[3], f0[4], f1[4]

                init = (first_blks[sub] - 1, _sweep_done(t0[3], t1[3]), first_dq[sub], t0[3], t1[3], t0[4], t1[4])
                final.append(lax.while_loop(cond, step, init)[2])
            return tuple(final)

        final = lax.cond(all_done, lambda: tuple(first_dq), sweep_on)
        for sub in range(subs):
            dq_ref[sub * QB:(sub + 1) * QB, :] = final[sub]

        @pl.when(last_step)
        def _():
            ex.wait(*ex_refs)

    blk = pl.BlockSpec((subs * QB, QB), lambda p, i: (i, p))
    full = lambda off: pl.BlockSpec((S, QB), lambda p, i: (0, off + p), pipeline_mode=pl.Buffered(1))
    outs = pl.pallas_call(
        body, name="bwd_attn", grid=(n_pairs, n_steps),
        in_specs=[blk, full(n_pairs), full(2 * n_pairs), blk] + ex.specs,
        out_specs=[blk, pl.BlockSpec((S, QB), lambda p, i: (0, p)), pl.BlockSpec((S, QB), lambda p, i: (0, p))] + ex.specs,
        out_shape=[jax.ShapeDtypeStruct((S, n_pairs * QB), F32)] * 3 + ex.out_shape,
        scratch_shapes=ex.scratch,
        compiler_params=_params("arbitrary", "arbitrary"),
    )(qkv, qkv, qkv, d_attn, *ex.arrays)
    return outs[0], outs[1], outs[2], outs[3:]


def _bwd_pool(u, d_pool, w_pool, tile):
    S, C = u.shape
    n_tiles = S // tile
    ng = len(POOL_WINDOWS)

    def body(u_ref, uh_ref, d_ref, dh_ref, wp_ref, du_ref, dwp_ref):
        i = pl.program_id(0)
        first = i == 0
        _zero_when(first, dwp_ref)
        halo = jnp.where(first, 0.0, uh_ref[...])
        parts = _pool_deviation(u_ref[...], halo, i * tile)
        dout = d_ref[...]
        nxt = jnp.where(i == n_tiles - 1, 0.0, dh_ref[...])
        dext = jnp.concatenate([dout, nxt], axis=0).astype(BF16)
        counts = _pool_counts(i * tile, tile + HALO)
        dps, scaled = [], []
        for g in range(ng):
            lanes = slice(g * POOL_GROUP, (g + 1) * POOL_GROUP)
            dp = _dot_nt(dext[:, lanes], wp_ref[g].astype(BF16))
            dps.append(dp[:tile])
            scaled.append(dp / counts[g])
        sums = _window_sums(jnp.concatenate(scaled, axis=1), forward=True)
        for g, w in enumerate(POOL_WINDOWS):
            lanes = slice(g * POOL_GROUP, (g + 1) * POOL_GROUP)
            du_ref[:, lanes] = sums[w][:tile, lanes] - dps[g]
            dwp_ref[g] += _dot_tn(parts[g].astype(BF16), dext[:tile, lanes])

    row = pl.BlockSpec((tile, C), lambda i: (i, 0))
    return pl.pallas_call(
        body, name="bwd_pool", grid=(n_tiles,),
        in_specs=[row, _prev_halo_spec(tile, C), row, _next_halo_spec(tile, C, n_tiles), _const(w_pool.shape)],
        out_specs=[row, pl.BlockSpec(w_pool.shape, lambda i: (0, 0, 0))],
        out_shape=[jax.ShapeDtypeStruct((S, C), F32), jax.ShapeDtypeStruct(w_pool.shape, F32)],
        compiler_params=_params("arbitrary"),
    )(u, u, d_pool, d_pool, w_pool)


def _bwd_w_in(du, dq, dk, dv, h1_t, n_blocks, tile):
    D, S = h1_t.shape
    C = du.shape[1]
    cs = 4 * C // n_blocks
    per = C // cs

    def body(du_ref, dq_ref, dk_ref, dv_ref, ht_ref, dproj_ref, dw_ref):
        _zero_when(pl.program_id(0) == 0, dw_ref)
        ht = ht_ref[...]
        for d in range(n_blocks):
            src = (du_ref, dq_ref, dk_ref, dv_ref)[d // per]
            dproj = src[:, (d % per) * cs:(d % per + 1) * cs].astype(BF16)
            dproj_ref[:, d * cs:(d + 1) * cs] = dproj
            dw_ref[d] += _dot(ht, dproj)

    row = lambda w: pl.BlockSpec((tile, w), lambda i: (i, 0))
    return pl.pallas_call(
        body, name="bwd_w_in", grid=(S // tile,),
        in_specs=[row(C), row(C), row(C), row(C), pl.BlockSpec((D, tile), lambda i: (0, i))],
        out_specs=[row(4 * C), pl.BlockSpec((n_blocks, D, cs), lambda i: (0, 0, 0))],
        out_shape=[jax.ShapeDtypeStruct((S, 4 * C), BF16), jax.ShapeDtypeStruct((n_blocks, D, cs), F32)],
        compiler_params=_params("arbitrary"),
    )(du, dq, dk, dv, h1_t)


def _bwd_x(dproj, w_in_t, x, dx2, g1, tile, ex):
    S, D = x.shape
    n_tiles = S // tile

    def body(dp_ref, w_ref, x_ref, dx2_ref, g_ref, *rest):
        dx_ref, dg_ref = rest[ex.n:ex.n + 2]
        ex_refs = ex.split(rest[:ex.n] + rest[ex.n + 2:])
        first, last = _grid_ends((n_tiles,))

        @pl.when(first)
        def _():
            ex.start(*ex_refs)
            dg_ref[...] = jnp.zeros_like(dg_ref)

        dh = _dot(dp_ref[...], w_ref[...])
        xf = x_ref[...]
        r1 = _rms(xf)
        n1 = xf * r1
        dg_ref[...] += _colsum(dh * n1)
        dx_ref[...] = dx2_ref[...] + _norm_bwd(dh * g_ref[...], n1, r1)

        @pl.when(last)
        def _():
            ex.wait(*ex_refs)

    row = lambda w: pl.BlockSpec((tile, w), lambda i: (i, 0))
    outs = pl.pallas_call(
        body, name="bwd_x", grid=(n_tiles,),
        in_specs=[row(w_in_t.shape[0]), _const(w_in_t.shape), row(D), row(D), _const((1, D))] + ex.specs,
        out_specs=[row(D), pl.BlockSpec((1, D), lambda i: (0, 0))] + ex.specs,
        out_shape=[jax.ShapeDtypeStruct((S, D), F32), jax.ShapeDtypeStruct((1, D), F32)] + ex.out_shape,
        scratch_shapes=ex.scratch,
        compiler_params=_params("arbitrary"),
    )(dproj, w_in_t, x, dx2, g1, *ex.arrays)
    return outs[0], outs[1], outs[2:]


def _mesh_position():
    x, y, c = lax.axis_index("x"), lax.axis_index("y"), lax.axis_index("c")
    return x, y, c, 4 * x + 2 * y + c


def _peer(x, y, c, k):
    px = 1 - x if k & 4 else x
    py = 1 - y if k & 2 else y
    pc = 1 - c if k & 1 else c
    return (px, py, pc), 4 * px + 2 * py + pc


class _Exchange:
    def __init__(self, arrays, gather):
        self.arrays, self.gather, self.n = list(arrays), gather, len(arrays)
        self.out_shape = [jax.ShapeDtypeStruct(((N_DEV,) + a.shape) if gather else a.shape, a.dtype) for a in arrays]
        self.specs = [pl.BlockSpec(memory_space=pl.ANY)] * self.n
        copies = self.n * (N_DEV - 1)
        self.scratch = [pltpu.SemaphoreType.DMA((copies,)), pltpu.SemaphoreType.DMA((copies,)),
                        pltpu.SemaphoreType.DMA((self.n,))]

    def _copies(self, ins, outs, sems):
        send_sems, recv_sems, local_sems = sems
        x, y, c, me = _mesh_position()
        local, remote = [], []
        for a in range(self.n):
            mine = ins[a] if self.gather else ins[a].at[me]
            local.append(pltpu.make_async_copy(mine, outs[a].at[me], local_sems.at[a]))
            for k in range(1, N_DEV):
                peer, peer_idx = _peer(x, y, c, k)
                src = ins[a] if self.gather else ins[a].at[peer_idx]
                sem = a * (N_DEV - 1) + k - 1
                remote.append(pltpu.make_async_remote_copy(
                    src_ref=src, dst_ref=outs[a].at[me], send_sem=send_sems.at[sem], recv_sem=recv_sems.at[sem],
                    device_id=peer, device_id_type=MESH))
        return local, remote

    def start(self, ins, outs, sems):
        local, remote = self._copies(ins, outs, sems)
        for cp in local + remote:
            cp.start()

    def wait(self, ins, outs, sems):
        local, remote = self._copies(ins, outs, sems)
        for cp in remote:
            cp.wait_send()
        for cp in remote:
            cp.wait_recv()
        for cp in local:
            cp.wait()

    def split(self, refs):
        return refs[:self.n], refs[self.n:2 * self.n], refs[2 * self.n:]


def _all_to_all(arrays, gather, name):
    ex = _Exchange(arrays, gather)

    def body(*refs):
        ins, outs, sems = ex.split(refs)
        ex.start(ins, outs, sems)
        ex.wait(ins, outs, sems)

    return pl.pallas_call(body, name=name, in_specs=ex.specs, out_specs=ex.specs, out_shape=ex.out_shape,
                          scratch_shapes=ex.scratch)(*ex.arrays)


def _reduce_adamw(parts, w, m, v, rows):
    R, C = w.shape

    def body(p_ref, w_ref, m_ref, v_ref, g_ref, d_ref, nm_ref, nv_ref):
        g = p_ref[0].astype(F32)
        for s in range(1, N_DEV):
            g = g + p_ref[s].astype(F32)
        g_ref[...] = g
        m_new = ADAM_B1 * m_ref[...] + (1.0 - ADAM_B1) * g
        v_new = ADAM_B2 * v_ref[...] + (1.0 - ADAM_B2) * (g * g)
        m_hat = m_new / (1.0 - ADAM_B1 ** ADAM_STEP)
        v_hat = v_new / (1.0 - ADAM_B2 ** ADAM_STEP)
        d_ref[...] = -ADAM_LR * (m_hat / (jnp.sqrt(v_hat) + ADAM_EPS) + ADAM_WD * w_ref[...])
        nm_ref[...] = m_new
        nv_ref[...] = v_new

    row = pl.BlockSpec((rows, C), lambda i: (i, 0))
    return pl.pallas_call(
        body, name="reduce_adamw", grid=(R // rows,),
        in_specs=[pl.BlockSpec((N_DEV, rows, C), lambda i: (0, i, 0)), row, row, row],
        out_specs=[row] * 4, out_shape=[jax.ShapeDtypeStruct((R, C), F32)] * 4,
        compiler_params=_params("parallel"),
    )(parts, w, m, v)


def _row_tile(rows, cols):
    fits = [t for t in range(8, rows + 1, 8) if rows % t == 0 and N_DEV * t * cols * 4 <= 4 * 1024 * 1024]
    return max(fits) if fits else rows


SMALL_COLS = 1024


def _pack_small(vals):
    rows = []
    for a in vals:
        flat = a.reshape(-1)
        pad = (-flat.shape[0]) % SMALL_COLS
        rows.append(jnp.pad(flat, (0, pad)).reshape(-1, SMALL_COLS))
    packed = jnp.concatenate(rows, axis=0)
    return jnp.pad(packed, ((0, (-packed.shape[0]) % 8), (0, 0)))


def _unpack_small(packed, like):
    out, r = [], 0
    for a in like:
        n = a.size
        nr = -(-n // SMALL_COLS)
        out.append(packed[r:r + nr].reshape(-1)[:n].reshape(a.shape))
        r += nr
    return out


def kernel(x, norm_mix_pre, w_in, w_pool, pool_scale, attn_scale, w_out, norm_mix_post, norm_ffn_pre, w_up, conv_w, conv_b, w_down, norm_ffn_post, loss_target, m_norm_mix_pre, m_w_in, m_w_pool, m_pool_scale, m_attn_scale, m_w_out, m_norm_mix_post, m_norm_ffn_pre, m_w_up, m_conv_w, m_conv_b, m_w_down, m_norm_ffn_post, v_norm_mix_pre, v_w_in, v_w_pool, v_pool_scale, v_attn_scale, v_w_out, v_norm_mix_post, v_norm_ffn_pre, v_w_up, v_conv_w, v_conv_b, v_w_down, v_norm_ffn_post):
    S, D = x.shape[1], x.shape[2]
    d_ff_block = w_up.shape[2]

    xs, target = x[0], loss_target[0]
    g1, g2, g3, g4 = norm_mix_pre, norm_mix_post, norm_ffn_pre, norm_ffn_post
    big = min(512, S)
    small = min(256, S)
    n_pairs = pool_scale.shape[1] // QB
    conv_b_g = conv_b.reshape(N_DEV, 1, d_ff_block)

    (w_in_g,) = _all_to_all([w_in[0].astype(BF16)], gather=True, name="gather_w_in")
    h1_t, u, qkv = _fwd_inproj(xs, g1, w_in_g, big)
    pool_out = _fwd_pool(u, w_pool[0], big)
    attn_out, (w_out_g, w_up_g, w_down_g, conv_w_g) = _fwd_attn(
        qkv, n_pairs, _Exchange([w_out[0].astype(BF16), w_up[0].astype(BF16), w_down[0].astype(BF16), conv_w[0]], gather=True),
        min(ATTN_FWD_BLOCKS, S // QB))
    w_out_full = w_out_g.reshape(D, D)
    w_down4 = w_down_g.reshape(D_FF_SHARDS, d_ff_block, D)
    mix, x2, h2, h2_t = _fwd_outproj(pool_out, attn_out, pool_scale, attn_scale, w_out_full, xs, g2, g3, big)
    upre, gate_val, dy, df, loss_cols, dg4 = _fwd_ffn_loss(h2, w_up_g, conv_w_g, conv_b_g, w_down4, x2, target, g4, small)
    loss = lax.psum(0.5 * jnp.sum(loss_cols) / D, ("x", "y", "c"))

    dupre_g, dupre_v, d_wd4, d_wup, d_cb, d_cw = _bwd_ffn_blocks(gate_val, upre, conv_w_g, w_down4, df, h2_t, min(1024, S))
    dx2, dmix, dg3, dg2 = _bwd_ffn_tokens(dupre_g, dupre_v, w_up_g, x2, dy, mix, g2, g3, big)
    d_pool, d_attn, d_wout, d_ps, d_as = _bwd_outproj(dmix, w_out_full, pool_out, attn_out, pool_scale, attn_scale, big)
    d_wdown_g = d_wd4.reshape(N_DEV, w_down.shape[1], D)
    d_wout_g = d_wout.reshape(N_DEV, D // N_DEV, D)
    dq, dk, dv, late_parts = _bwd_attn(qkv, d_attn, n_pairs, _Exchange([d_wout_g, d_wup, d_wdown_g, d_cw], gather=False),
                                       min(ATTN_BWD_BLOCKS, S // QB))
    du, d_wp = _bwd_pool(u, d_pool, w_pool[0], big)
    dproj, d_win = _bwd_w_in(du, dq, dk, dv, h1_t, N_DEV, big)
    w_in_t = w_in_g.transpose(0, 2, 1).reshape(-1, D)
    dx, dg1, (win_parts,) = _bwd_x(dproj, w_in_t, xs, dx2, g1, big, _Exchange([d_win], gather=False))
    big_parts = [win_parts] + list(late_parts)
    r = dict(dx=dx, g1=dg1, w_pool=d_wp, pool_scale=d_ps, attn_scale=d_as, g2=dg2, g3=dg3, conv_b=d_cb, g4=dg4)

    small_names = ["norm_mix_pre", "w_pool", "pool_scale", "attn_scale", "norm_mix_post", "norm_ffn_pre", "conv_b", "norm_ffn_post"]
    small_w = dict(norm_mix_pre=norm_mix_pre, w_pool=w_pool, pool_scale=pool_scale, attn_scale=attn_scale,
                   norm_mix_post=norm_mix_post, norm_ffn_pre=norm_ffn_pre, conv_b=conv_b, norm_ffn_post=norm_ffn_post)
    small_m = dict(norm_mix_pre=m_norm_mix_pre, w_pool=m_w_pool, pool_scale=m_pool_scale, attn_scale=m_attn_scale,
                   norm_mix_post=m_norm_mix_post, norm_ffn_pre=m_norm_ffn_pre, conv_b=m_conv_b, norm_ffn_post=m_norm_ffn_post)
    small_v = dict(norm_mix_pre=v_norm_mix_pre, w_pool=v_w_pool, pool_scale=v_pool_scale, attn_scale=v_attn_scale,
                   norm_mix_post=v_norm_mix_post, norm_ffn_pre=v_norm_ffn_pre, conv_b=v_conv_b, norm_ffn_post=v_norm_ffn_post)
    small_g = dict(norm_mix_pre=r["g1"], w_pool=r["w_pool"], pool_scale=r["pool_scale"], attn_scale=r["attn_scale"],
                   norm_mix_post=r["g2"], norm_ffn_pre=r["g3"], conv_b=r["conv_b"], norm_ffn_post=r["g4"])
    like = [small_w[n] for n in small_names]
    packed_g = _pack_small([small_g[n] for n in small_names])

    (small_parts,) = _all_to_all([packed_g], gather=True, name="gather_small_grads")

    def update(parts, w, m, v):
        R, C = w.shape
        return _reduce_adamw(parts, w, m, v, _row_tile(R, C))

    res = {}
    res["w_in"] = update(big_parts[0], w_in[0], m_w_in[0], v_w_in[0])
    res["w_out"] = update(big_parts[1], w_out[0], m_w_out[0], v_w_out[0])
    res["w_up"] = update(big_parts[2], w_up[0], m_w_up[0], v_w_up[0])
    res["w_down"] = update(big_parts[3], w_down[0], m_w_down[0], v_w_down[0])
    res["conv_w"] = update(big_parts[4], conv_w[0], m_conv_w[0], v_conv_w[0])
    small_res = update(small_parts, _pack_small(like), _pack_small([small_m[n] for n in small_names]),
                       _pack_small([small_v[n] for n in small_names]))
    small_res = [_unpack_small(t, like) for t in small_res]
    for idx, n in enumerate(small_names):
        res[n] = tuple(t[idx] for t in small_res)

    order = ["norm_mix_pre", "w_in", "w_pool", "pool_scale", "attn_scale", "w_out", "norm_mix_post", "norm_ffn_pre",
             "w_up", "conv_w", "conv_b", "w_down", "norm_ffn_post"]
    shaped = {n: tuple(t.reshape(s.shape) for t in res[n])
              for n, s in dict(norm_mix_pre=norm_mix_pre, w_in=w_in, w_pool=w_pool, pool_scale=pool_scale, attn_scale=attn_scale,
                               w_out=w_out, norm_mix_post=norm_mix_post, norm_ffn_pre=norm_ffn_pre, w_up=w_up, conv_w=conv_w,
                               conv_b=conv_b, w_down=w_down, norm_ffn_post=norm_ffn_post).items()}
    outs = [loss, r["dx"].reshape(x.shape)]
    for k in range(4):
        outs += [shaped[n][k] for n in order]
    return tuple(outs)
```

```python
import functools

import jax
import jax.numpy as jnp
from jax import lax
from jax.experimental import pallas as pl
from jax.experimental.pallas import tpu as pltpu

F32 = jnp.float32
BF16 = jnp.bfloat16
HIGHEST = lax.Precision.HIGHEST

N_DEV = 8
EPS = 1e-6
POOL_WINDOWS = (2, 4, 8, 16)
POOL_GROUP = 128
HALO = 16
HEAD_DIM = 64
QB = 128
ATTN_SCALE = HEAD_DIM ** -0.5
ATTN_FWD_BLOCKS = 8
ATTN_BWD_BLOCKS = 8
EXP_UNDERFLOW = -88.0
D_FF_SHARDS = 4

ADAM_LR = 0.001
ADAM_B1 = 0.9
ADAM_B2 = 0.999
ADAM_EPS = 1e-08
ADAM_WD = 0.01
ADAM_STEP = 10

VMEM_LIMIT_V7X = 56 * 1024 * 1024
MESH = pl.DeviceIdType.MESH


def _params(*semantics):
    return pltpu.CompilerParams(dimension_semantics=semantics, vmem_limit_bytes=VMEM_LIMIT_V7X)


def _const(shape):
    zeros = (0,) * len(shape)
    return pl.BlockSpec(shape, lambda *_: zeros, pipeline_mode=pl.Buffered(1))


def _dot(a, b):
    return jnp.dot(a, b, preferred_element_type=F32)


def _dot_nt(a, b):
    return lax.dot_general(a, b, (((1,), (1,)), ((), ())), preferred_element_type=F32)


def _dot_tn(a, b):
    return lax.dot_general(a, b, (((0,), (0,)), ((), ())), preferred_element_type=F32)


def _rms(v):
    return lax.rsqrt(jnp.mean(v * v, axis=-1, keepdims=True) + EPS)


def _norm_bwd(dn_times_gain, n, r):
    return r * (dn_times_gain - n * jnp.mean(dn_times_gain * n, axis=-1, keepdims=True))


def _zero_when(first, *refs):
    @pl.when(first)
    def _():
        for ref in refs:
            ref[...] = jnp.zeros_like(ref)


def _colsum(v):
    return jnp.sum(v, axis=0, keepdims=True)


def _grid_ends(grid):
    ids = [pl.program_id(a) for a in range(len(grid))]
    first = functools.reduce(jnp.logical_and, [i == 0 for i in ids])
    last = functools.reduce(jnp.logical_and, [i == n - 1 for i, n in zip(ids, grid)])
    return first, last


def _fwd_inproj(x, g1, w_in_g, tile):
    S, D = x.shape
    nb, _, cs = w_in_g.shape
    d_pool = 2 * cs

    def body(x_ref, g_ref, w_ref, ht_ref, u_ref, qkv_ref):
        xf = x_ref[...]
        h = (xf * _rms(xf) * g_ref[...]).astype(BF16)
        ht_ref[...] = h.T
        for d in range(nb):
            o = _dot(h, w_ref[d])
            if d < 2:
                u_ref[:, d * cs:(d + 1) * cs] = o
            else:
                qkv_ref[:, (d - 2) * cs:(d - 1) * cs] = o.astype(BF16)

    return pl.pallas_call(
        body, name="fwd_inproj", grid=(S // tile,),
        in_specs=[pl.BlockSpec((tile, D), lambda i: (i, 0)), _const((1, D)), _const(w_in_g.shape)],
        out_specs=[pl.BlockSpec((D, tile), lambda i: (0, i)), pl.BlockSpec((tile, d_pool), lambda i: (i, 0)),
                   pl.BlockSpec((tile, 3 * d_pool), lambda i: (i, 0))],
        out_shape=[jax.ShapeDtypeStruct((D, S), BF16), jax.ShapeDtypeStruct((S, d_pool), F32),
                   jax.ShapeDtypeStruct((S, 3 * d_pool), BF16)],
        compiler_params=_params("parallel"),
    )(x, g1, w_in_g)


def _window_sums(ext, forward):
    n = ext.shape[0]
    sums, s, sh = {}, ext, 1
    while sh < POOL_WINDOWS[-1]:
        s = s + pltpu.roll(s, (n - sh) if forward else sh, axis=0)
        sh *= 2
        sums[sh] = s
    return sums


def _pool_counts(t0, rows):
    t1 = (lax.broadcasted_iota(jnp.int32, (rows, 1), 0) + t0 + 1).astype(F32)
    return [jnp.minimum(t1, float(w)) for w in POOL_WINDOWS]


def _pool_deviation(u, halo, t0):
    T = u.shape[0]
    sums = _window_sums(jnp.concatenate([halo, u], axis=0), forward=False)
    counts = _pool_counts(t0, T)
    parts = []
    for g, w in enumerate(POOL_WINDOWS):
        lanes = slice(g * POOL_GROUP, (g + 1) * POOL_GROUP)
        parts.append(sums[w][HALO:, lanes] / counts[g] - u[:, lanes])
    return parts


def _prev_halo_spec(tile, width):
    return pl.BlockSpec((HALO, width), lambda i: (jnp.maximum(i * (tile // HALO) - 1, 0), 0))


def _next_halo_spec(tile, width, n_tiles):
    last = n_tiles * (tile // HALO) - 1
    return pl.BlockSpec((HALO, width), lambda i: (jnp.minimum((i + 1) * (tile // HALO), last), 0))


def _fwd_pool(u, w_pool, tile):
    S, C = u.shape

    def body(u_ref, halo_ref, wp_ref, o_ref):
        i = pl.program_id(0)
        halo = jnp.where(i > 0, halo_ref[...], 0.0)
        parts = _pool_deviation(u_ref[...], halo, i * tile)
        for g, p in enumerate(parts):
            o_ref[:, g * POOL_GROUP:(g + 1) * POOL_GROUP] = _dot(p.astype(BF16), wp_ref[g].astype(BF16))

    return pl.pallas_call(
        body, name="fwd_pool", grid=(S // tile,),
        in_specs=[pl.BlockSpec((tile, C), lambda i: (i, 0)), _prev_halo_spec(tile, C), _const(w_pool.shape)],
        out_specs=pl.BlockSpec((tile, C), lambda i: (i, 0)),
        out_shape=jax.ShapeDtypeStruct((S, C), F32),
        compiler_params=_params("parallel"),
    )(u, u, w_pool)


def _low_lanes():
    return lax.broadcasted_iota(jnp.int32, (QB, 2 * HEAD_DIM), 1) < HEAD_DIM


LOG_PIECES = 2
GRAD_PIECES = 3


def _triangle(inclusive, pieces):
    row = lax.broadcasted_iota(jnp.int32, (pieces * QB, QB), 0) % QB
    col = lax.broadcasted_iota(jnp.int32, (pieces * QB, QB), 1)
    return ((row >= col) if inclusive else (row > col)).astype(BF16)


def _pieces(v, n):
    out, rest = [], v
    for _ in range(n - 1):
        piece = rest.astype(BF16)
        out.append(piece)
        rest = rest - piece.astype(F32)
    out.append(rest.astype(BF16))
    return jnp.concatenate(out, axis=1)


def _causal_mask(width, offset):
    row = lax.broadcasted_iota(jnp.int32, (QB, width), 0)
    col = lax.broadcasted_iota(jnp.int32, (QB, width), 1)
    return col < row + offset


def _row_sums(vals, carry):
    for b in reversed(range(vals.shape[1] // QB)):
        carry = carry + jnp.sum(vals[:, b * QB:(b + 1) * QB], axis=1, keepdims=True)
    return carry


def _suffix_sums(vals, tri, carry):
    n = vals.shape[1] // QB
    out, run = [None] * n, carry
    for b in reversed(range(n)):
        blk = vals[:, b * QB:(b + 1) * QB]
        out[b] = _dot(_pieces(blk, tri.shape[0] // QB), tri) + run
        run = run + jnp.sum(blk, axis=1, keepdims=True)
    return (out[0] if n == 1 else jnp.concatenate(out, axis=1)), run


def _attn_tiles(qhs, kws, masks, carries, after_s):
    return _attn_weights(_attn_scores(qhs, kws, masks), masks, carries, after_s)


def _attn_scores(qhs, kws, masks):
    zs = [_dot_nt(qh, kw) * ATTN_SCALE for qh, kw in zip(qhs, kws)]
    es = [jnp.exp(-jnp.abs(z)) for z in zs]
    softplus = [jnp.maximum(z, 0.0) + jnp.log(1.0 + e) for z, e in zip(zs, es)]
    log_1m_beta = [-sp if m is None else jnp.where(m, -sp, 0.0) for sp, m in zip(softplus, masks)]
    return list(zip(zs, es, softplus, log_1m_beta))


def _attn_weights(scores, masks, carries, after_s):
    sums = [_suffix_sums(l, after_s, c) for (_, _, _, l), c in zip(scores, carries)]
    weights = [jnp.exp(z - sp + st) for (z, _, sp, _), (st, _) in zip(scores, sums)]
    weights = [a if m is None else jnp.where(m, a, 0.0) for a, m in zip(weights, masks)]
    return [(z, e, a, c) for (z, e, _, _), a, (_, c) in zip(scores, weights, sums)]


def _split_heads(v, low_lanes):
    return jnp.where(low_lanes, v, 0.0).astype(BF16), jnp.where(low_lanes, 0.0, v).astype(BF16)


def _sweep_done(c0, c1):
    return (jnp.maximum(jnp.max(c0), jnp.max(c1)) < EXP_UNDERFLOW).astype(jnp.int32)


def _all_done(carries):
    return jnp.max(functools.reduce(jnp.maximum, carries)) < EXP_UNDERFLOW


def _first_window(i):
    first_blk = jnp.maximum(i - 1, 0)
    return first_blk, pl.multiple_of(first_blk * QB, QB), (i - first_blk) * QB


def _fwd_attn(qkv, n_pairs, ex, subs):
    S = qkv.shape[0]
    n_steps = S // (subs * QB)

    def body(q_ref, k_ref, v_ref, *rest):
        o_ref = rest[ex.n]
        ex_refs = ex.split(rest[:ex.n] + rest[ex.n + 1:])
        first_step, last_step = _grid_ends((n_pairs, n_steps))

        @pl.when(first_step)
        def _():
            ex.start(*ex_refs)

        low_lanes = _low_lanes()
        after_s = _triangle(False, LOG_PIECES)
        zero = jnp.zeros((QB, 1), F32)

        def cond(c):
            return jnp.logical_and(c[0] >= 0, c[1] == 0)

        qhs, kws, vws, masks, first_blks = [], [], [], [], []
        for sub in range(subs):
            i = pl.program_id(1) * subs + sub
            first_blk, start, offset = _first_window(i)
            first_blks.append(first_blk)
            qhs += _split_heads(q_ref[sub * QB:(sub + 1) * QB, :].astype(F32), low_lanes)
            kws += [k_ref[pl.ds(start, 2 * QB), :]] * 2
            vws += [v_ref[pl.ds(start, 2 * QB), :]] * 2
            masks += [_causal_mask(2 * QB, offset)] * 2
        tiles = _attn_tiles(qhs, kws, masks, [zero] * len(qhs), after_s)
        outs = [_dot(t[2].astype(BF16), vw) for t, vw in zip(tiles, vws)]

        first_out = [jnp.where(low_lanes, outs[2 * sub], outs[2 * sub + 1]) for sub in range(subs)]

        def sweep_on():
            final = []
            for sub in range(subs):
                def step(c, qh=qhs[2 * sub:2 * sub + 2]):
                    j, _, acc, c0, c1 = c
                    at = pl.multiple_of(j * QB, QB)
                    kb = k_ref[pl.ds(at, QB), :]
                    vb = v_ref[pl.ds(at, QB), :]
                    far = _attn_tiles(qh, [kb, kb], [None, None], [c0, c1], after_s)
                    acc = acc + jnp.where(low_lanes, _dot(far[0][2].astype(BF16), vb), _dot(far[1][2].astype(BF16), vb))
                    return j - 1, _sweep_done(far[0][3], far[1][3]), acc, far[0][3], far[1][3]

                c0, c1 = tiles[2 * sub][3], tiles[2 * sub + 1][3]
                final.append(lax.while_loop(cond, step, (first_blks[sub] - 1, _sweep_done(c0, c1), first_out[sub], c0, c1))[2])
            return tuple(final)

        final = lax.cond(_all_done([t[3] for t in tiles]), lambda: tuple(first_out), sweep_on)
        for sub in range(subs):
            o_ref[sub * QB:(sub + 1) * QB, :] = final[sub]

        @pl.when(last_step)
        def _():
            ex.wait(*ex_refs)

    outs = pl.pallas_call(
        body, name="fwd_attn", grid=(n_pairs, n_steps),
        in_specs=[pl.BlockSpec((subs * QB, QB), lambda p, i: (i, p)),
                  pl.BlockSpec((S, QB), lambda p, i: (0, n_pairs + p), pipeline_mode=pl.Buffered(1)),
                  pl.BlockSpec((S, QB), lambda p, i: (0, 2 * n_pairs + p), pipeline_mode=pl.Buffered(1))] + ex.specs,
        out_specs=[pl.BlockSpec((subs * QB, QB), lambda p, i: (i, p))] + ex.specs,
        out_shape=[jax.ShapeDtypeStruct((S, n_pairs * QB), F32)] + ex.out_shape,
        scratch_shapes=ex.scratch,
        compiler_params=_params("arbitrary", "arbitrary"),
    )(qkv, qkv, qkv, *ex.arrays)
    return outs[0], outs[1:]


def _normalized_heads(pool_out, attn_out):
    rp, ra = _rms(pool_out), _rms(attn_out)
    return pool_out * rp, rp, attn_out * ra, ra


def _fwd_outproj(pool_out, attn_out, pool_scale, attn_scale, w_out, x, g2, g3, tile):
    S, D = x.shape
    C = pool_out.shape[1]

    def body(p_ref, a_ref, ps_ref, as_ref, w_ref, x_ref, g2_ref, g3_ref, mix_ref, x2_ref, h2_ref, h2t_ref):
        n_p, _, n_a, _ = _normalized_heads(p_ref[...], a_ref[...])
        mix = _dot((n_p * ps_ref[...]).astype(BF16), w_ref[:C, :]) + _dot((n_a * as_ref[...]).astype(BF16), w_ref[C:, :])
        mix_ref[...] = mix
        x2 = x_ref[...] + mix * _rms(mix) * g2_ref[...]
        x2_ref[...] = x2
        h2 = (x2 * _rms(x2) * g3_ref[...]).astype(BF16)
        h2_ref[...] = h2
        h2t_ref[...] = h2.T

    row = lambda w: pl.BlockSpec((tile, w), lambda i: (i, 0))
    return pl.pallas_call(
        body, name="fwd_outproj", grid=(S // tile,),
        in_specs=[row(C), row(C), _const((1, C)), _const((1, C)), _const(w_out.shape), row(D), _const((1, D)), _const((1, D))],
        out_specs=[row(D), row(D), row(D), pl.BlockSpec((D, tile), lambda i: (0, i))],
        out_shape=[jax.ShapeDtypeStruct((S, D), F32), jax.ShapeDtypeStruct((S, D), F32), jax.ShapeDtypeStruct((S, D), BF16),
                   jax.ShapeDtypeStruct((D, S), BF16)],
        compiler_params=_params("parallel"),
    )(pool_out, attn_out, pool_scale, attn_scale, w_out, x, g2, g3)


def _conv_taps(tile_rows, halo_rows):
    T = tile_rows.shape[0]
    ext = jnp.concatenate([halo_rows.astype(F32), tile_rows.astype(F32)], axis=0)
    return pltpu.roll(ext, 2, axis=0)[HALO:], pltpu.roll(ext, 1, axis=0)[HALO:], ext[HALO:]


def _tap_rows(cw_ref, d):
    return [cw_ref[d, k:k + 1, :] for k in range(3)]


def _gated_unit(taps_gate, taps_val, cw_gate, cw_val, cb_gate, cb_val):
    gate = cw_gate[0] * taps_gate[0] + cw_gate[1] * taps_gate[1] + cw_gate[2] * taps_gate[2] + cb_gate
    val = cw_val[0] * taps_val[0] + cw_val[1] * taps_val[1] + cw_val[2] * taps_val[2] + cb_val
    sig = 1.0 / (1.0 + jnp.exp(-gate))
    return gate, val, sig


def _fwd_ffn_loss(h2, w_up_g, conv_w_g, conv_b_g, w_down4, x2, target, g4, tile):
    S, D = x2.shape
    nb, _, cs = w_up_g.shape
    half = D_FF_SHARDS

    def body(h_ref, w_ref, cw_ref, cb_ref, wd_ref, x2_ref, t_ref, g4_ref, upre_ref, gv_ref, dy_ref, df_ref, loss_ref, dg4_ref, halo_ref):
        _zero_when(pl.program_id(0) == 0, loss_ref, dg4_ref, halo_ref)
        h = h_ref[...]

        def up(s):
            return _dot(h, w_ref[s]), _dot(h, w_ref[s + half])

        f = jnp.zeros((tile, D), F32)
        ahead = up(0)
        for s in range(half):
            ug, uv = ahead
            if s + 1 < half:
                ahead = up(s + 1)
            upre_ref[s] = ug.astype(BF16)
            upre_ref[s + half] = uv.astype(BF16)
            gate, val, sig = _gated_unit(_conv_taps(ug, halo_ref[s]), _conv_taps(uv, halo_ref[s + half]),
                                         _tap_rows(cw_ref, s), _tap_rows(cw_ref, s + half), cb_ref[s], cb_ref[s + half])
            halo_ref[s] = ug[tile - HALO:, :]
            halo_ref[s + half] = uv[tile - HALO:, :]
            gv_ref[s] = gate.astype(BF16)
            gv_ref[s + half] = val.astype(BF16)
            f = f + _dot((gate * sig * val).astype(BF16), wd_ref[s])
        r4 = _rms(f)
        n4 = f * r4
        err = x2_ref[...] + n4 * g4_ref[...] - t_ref[...]
        dy = err * (1.0 / D)
        dy_ref[...] = dy
        df_ref[...] = _norm_bwd(dy * g4_ref[...], n4, r4).astype(BF16)
        loss_ref[...] += _colsum(err * err)
        dg4_ref[...] += _colsum(dy * n4)

    row = lambda w: pl.BlockSpec((tile, w), lambda i: (i, 0))
    return pl.pallas_call(
        body, name="fwd_ffn_loss", grid=(S // tile,),
        in_specs=[row(D), _const(w_up_g.shape), _const(conv_w_g.shape), _const(conv_b_g.shape), _const(w_down4.shape),
                  row(D), row(D), _const((1, D))],
        out_specs=[pl.BlockSpec((nb, tile, cs), lambda i: (0, i, 0)), pl.BlockSpec((nb, tile, cs), lambda i: (0, i, 0)), row(D), row(D),
                   pl.BlockSpec((1, D), lambda i: (0, 0)), pl.BlockSpec((1, D), lambda i: (0, 0))],
        out_shape=[jax.ShapeDtypeStruct((nb, S, cs), BF16), jax.ShapeDtypeStruct((nb, S, cs), BF16),
                   jax.ShapeDtypeStruct((S, D), F32), jax.ShapeDtypeStruct((S, D), BF16),
                   jax.ShapeDtypeStruct((1, D), F32), jax.ShapeDtypeStruct((1, D), F32)],
        scratch_shapes=[pltpu.VMEM((nb, HALO, cs), F32)],
        compiler_params=_params("arbitrary"),
    )(h2, w_up_g, conv_w_g, conv_b_g, w_down4, x2, target, g4)


def _bwd_down(upre, conv_w_g, conv_b_g, w_down4, df, tile):
    nb, S, cs = upre.shape
    D = df.shape[1]
    n_tiles = S // tile

    def body(ug_ref, uv_ref, hg_ref, hv_ref, cwg_ref, cwv_ref, cbg_ref, cbv_ref, wd_ref, df_ref,
             dg_ref, dv_ref, dwd_ref, dbg_ref, dbv_ref, dcwg_ref, dcwv_ref):
        i = pl.program_id(1)
        first = i == 0
        _zero_when(first, dwd_ref, dbg_ref, dbv_ref, dcwg_ref, dcwv_ref)
        halo_g = jnp.where(first, jnp.zeros_like(hg_ref[0]), hg_ref[0])
        halo_v = jnp.where(first, jnp.zeros_like(hv_ref[0]), hv_ref[0])
        taps_g, taps_v = _conv_taps(ug_ref[0], halo_g), _conv_taps(uv_ref[0], halo_v)
        gate, val, sig = _gated_unit(taps_g, taps_v, _tap_rows(cwg_ref, 0), _tap_rows(cwv_ref, 0), cbg_ref[0], cbv_ref[0])
        silu = gate * sig
        dfb = df_ref[...]
        dact = _dot_nt(dfb, wd_ref[0])
        dwd_ref[0] += _dot_tn((silu * val).astype(BF16), dfb)
        dgate = dact * val * (sig * (1.0 + gate * (1.0 - sig)))
        dval = dact * silu
        dg_ref[0] = dgate.astype(BF16)
        dv_ref[0] = dval.astype(BF16)
        dbg_ref[0] += _colsum(dgate)
        dbv_ref[0] += _colsum(dval)
        for k in range(3):
            dcwg_ref[0, k:k + 1, :] += _colsum(dgate * taps_g[k])
            dcwv_ref[0, k:k + 1, :] += _colsum(dval * taps_v[k])

    half = D_FF_SHARDS
    blk = lambda off: pl.BlockSpec((1, tile, cs), lambda s, i: (s + off, i, 0))
    halo = lambda off: pl.BlockSpec((1, HALO, cs), lambda s, i: (s + off, jnp.maximum(i * (tile // HALO) - 1, 0), 0))
    par = lambda off, r: pl.BlockSpec((1, r, cs), lambda s, i: (s + off, 0, 0))
    outs = pl.pallas_call(
        body, name="bwd_down", grid=(half, n_tiles),
        in_specs=[blk(0), blk(half), halo(0), halo(half), par(0, 3), par(half, 3), par(0, 1), par(half, 1),
                  pl.BlockSpec((1, cs, D), lambda s, i: (s, 0, 0)), pl.BlockSpec((tile, D), lambda s, i: (i, 0))],
        out_specs=[blk(0), blk(0), pl.BlockSpec((1, cs, D), lambda s, i: (s, 0, 0)),
                   par(0, 1), par(0, 1), par(0, 3), par(0, 3)],
        out_shape=[jax.ShapeDtypeStruct((half, S, cs), BF16), jax.ShapeDtypeStruct((half, S, cs), BF16),
                   jax.ShapeDtypeStruct((half, cs, D), F32),
                   jax.ShapeDtypeStruct((half, 1, cs), F32), jax.ShapeDtypeStruct((half, 1, cs), F32),
                   jax.ShapeDtypeStruct((half, 3, cs), F32), jax.ShapeDtypeStruct((half, 3, cs), F32)],
        compiler_params=_params("parallel", "arbitrary"),
    )(upre, upre, upre, upre, conv_w_g, conv_w_g, conv_b_g, conv_b_g, w_down4, df)
    dgate, dval, d_wd, dbg, dbv, dcwg, dcwv = outs
    return dgate, dval, d_wd, jnp.concatenate([dbg, dbv], axis=0), jnp.concatenate([dcwg, dcwv], axis=0)


def _bwd_up_x(dgate, dval, conv_w_g, w_up_g, x2, dy, mix, g2, g3, tile):
    half, S, cs = dgate.shape
    nb = 2 * half
    D = x2.shape[1]
    n_tiles = S // tile

    def body(dg_ref, dv_ref, hg_ref, hv_ref, cw_ref, w_ref, x2_ref, dy_ref, mix_ref, g2_ref, g3_ref,
             dupre_ref, dx2_ref, dmix_ref, dg3_ref, dg2_ref):
        i = pl.program_id(0)
        last = i == n_tiles - 1
        _zero_when(i == 0, dg3_ref, dg2_ref)
        dh2 = jnp.zeros((tile, D), F32)
        for d in range(nb):
            src, halo = (dg_ref, hg_ref) if d < half else (dv_ref, hv_ref)
            nxt = jnp.where(last, jnp.zeros_like(halo[d % half]), halo[d % half])
            ext = jnp.concatenate([src[d % half].astype(F32), nxt.astype(F32)], axis=0)
            n = ext.shape[0]
            cw = _tap_rows(cw_ref, d)
            dupre = (cw[2] * ext + cw[1] * pltpu.roll(ext, n - 1, axis=0) + cw[0] * pltpu.roll(ext, n - 2, axis=0))[:tile]
            dupre = dupre.astype(BF16)
            dupre_ref[d] = dupre
            dh2 = dh2 + _dot_nt(dupre, w_ref[d])
        x2 = x2_ref[...]
        r3 = _rms(x2)
        n3 = x2 * r3
        dg3_ref[...] += _colsum(dh2 * n3)
        dx2 = dy_ref[...] + _norm_bwd(dh2 * g3_ref[...], n3, r3)
        dx2_ref[...] = dx2
        mix = mix_ref[...]
        r2 = _rms(mix)
        n2 = mix * r2
        dg2_ref[...] += _colsum(dx2 * n2)
        dmix_ref[...] = _norm_bwd(dx2 * g2_ref[...], n2, r2).astype(BF16)

    row = lambda w: pl.BlockSpec((tile, w), lambda i: (i, 0))
    blk = pl.BlockSpec((half, tile, cs), lambda i: (0, i, 0))
    last_halo = n_tiles * (tile // HALO) - 1
    halo = pl.BlockSpec((half, HALO, cs), lambda i: (0, jnp.minimum((i + 1) * (tile // HALO), last_halo), 0))
    acc = pl.BlockSpec((1, D), lambda i: (0, 0))
    return pl.pallas_call(
        body, name="bwd_up_x", grid=(n_tiles,),
        in_specs=[blk, blk, halo, halo, _const(conv_w_g.shape), _const(w_up_g.shape), row(D), row(D), row(D),
                  _const((1, D)), _const((1, D))],
        out_specs=[pl.BlockSpec((nb, tile, cs), lambda i: (0, i, 0)), row(D), row(D), acc, acc],
        out_shape=[jax.ShapeDtypeStruct((nb, S, cs), BF16), jax.ShapeDtypeStruct((S, D), F32),
                   jax.ShapeDtypeStruct((S, D), BF16), jax.ShapeDtypeStruct((1, D), F32), jax.ShapeDtypeStruct((1, D), F32)],
        compiler_params=_params("arbitrary"),
    )(dgate, dval, dgate, dval, conv_w_g, w_up_g, x2, dy, mix, g2, g3)


def _bwd_weight(act_t, dout, tile):
    D, S = act_t.shape
    nb, _, cs = dout.shape

    def body(a_ref, d_ref, o_ref):
        _zero_when(pl.program_id(1) == 0, o_ref)
        o_ref[0] += _dot(a_ref[...], d_ref[0])

    return pl.pallas_call(
        body, name="bwd_w_up", grid=(nb, S // tile),
        in_specs=[pl.BlockSpec((D, tile), lambda d, i: (0, i)), pl.BlockSpec((1, tile, cs), lambda d, i: (d, i, 0))],
        out_specs=pl.BlockSpec((1, D, cs), lambda d, i: (d, 0, 0)),
        out_shape=jax.ShapeDtypeStruct((nb, D, cs), F32),
        compiler_params=_params("parallel", "arbitrary"),
    )(act_t, dout)


def _bwd_ffn_blocks(gate_val, upre, conv_w_g, w_down4, df, h2_t, tile):
    nb, S, cs = upre.shape
    D = df.shape[1]
    n_tiles = S // tile
    half = D_FF_SHARDS

    def body(g_ref, v_ref, ug_ref, uv_ref, cwg_ref, cwv_ref, wd_ref, df_ref, ht_ref,
             dug_ref, duv_ref, dwd_ref, dwg_ref, dwv_ref, dbg_ref, dbv_ref, dcwg_ref, dcwv_ref, next_ref):
        _zero_when(pl.program_id(1) == 0, dwd_ref, dwg_ref, dwv_ref, dbg_ref, dbv_ref, dcwg_ref, dcwv_ref, next_ref)
        dfb = df_ref[...]
        dact = _dot_nt(dfb, wd_ref[0])
        gate, val = g_ref[0].astype(F32), v_ref[0].astype(F32)
        sig = 1.0 / (1.0 + jnp.exp(-gate))
        silu = gate * sig
        dwd_ref[0] += _dot_tn((silu * val).astype(BF16), dfb)
        ht = ht_ref[...]

        def through_conv(dup, slot, cw_ref, u_ref, du_ref, dw_ref, db_ref, dcw_ref):
            ext = jnp.concatenate([dup, next_ref[slot]], axis=0)
            n = ext.shape[0]
            shifted = (dup, pltpu.roll(ext, n - 1, axis=0)[:tile], pltpu.roll(ext, n - 2, axis=0)[:tile])
            next_ref[slot] = dup[:HALO]
            cw = _tap_rows(cw_ref, 0)
            dupre = (cw[2] * shifted[0] + cw[1] * shifted[1] + cw[0] * shifted[2]).astype(BF16)
            du_ref[0] = dupre
            dw_ref[0] += _dot(ht, dupre)
            u = u_ref[0].astype(F32)
            db_ref[0] += _colsum(dup)
            for k in range(3):
                dcw_ref[0, k:k + 1, :] += _colsum(shifted[2 - k] * u)

        through_conv(dact * val * (sig * (1.0 + gate * (1.0 - sig))), 0, cwg_ref, ug_ref, dug_ref, dwg_ref, dbg_ref, dcwg_ref)
        through_conv(dact * silu, 1, cwv_ref, uv_ref, duv_ref, dwv_ref, dbv_ref, dcwv_ref)

    rev = lambda i: n_tiles - 1 - i
    blk = lambda off: pl.BlockSpec((1, tile, cs), lambda s, i: (s + off, rev(i), 0))
    par = lambda off, r: pl.BlockSpec((1, r, cs), lambda s, i: (s + off, 0, 0))
    acc = lambda r, c: pl.BlockSpec((1, r, c), lambda s, i: (s, 0, 0), pipeline_mode=pl.Buffered(1))
    outs = pl.pallas_call(
        body, name="bwd_ffn_blocks", grid=(half, n_tiles),
        in_specs=[blk(0), blk(half), blk(0), blk(half), par(0, 3), par(half, 3),
                  acc(cs, D), pl.BlockSpec((tile, D), lambda s, i: (rev(i), 0)), pl.BlockSpec((D, tile), lambda s, i: (0, rev(i)))],
        out_specs=[blk(0), blk(0), acc(cs, D), acc(D, cs), acc(D, cs), acc(1, cs), acc(1, cs), acc(3, cs), acc(3, cs)],
        out_shape=[jax.ShapeDtypeStruct((half, S, cs), BF16), jax.ShapeDtypeStruct((half, S, cs), BF16),
                   jax.ShapeDtypeStruct((half, cs, D), F32),
                   jax.ShapeDtypeStruct((half, D, cs), F32), jax.ShapeDtypeStruct((half, D, cs), F32),
                   jax.ShapeDtypeStruct((half, 1, cs), F32), jax.ShapeDtypeStruct((half, 1, cs), F32),
                   jax.ShapeDtypeStruct((half, 3, cs), F32), jax.ShapeDtypeStruct((half, 3, cs), F32)],
        scratch_shapes=[pltpu.VMEM((2, HALO, cs), F32)],
        compiler_params=_params("arbitrary", "arbitrary"),
    )(gate_val, gate_val, upre, upre, conv_w_g, conv_w_g, w_down4, df, h2_t)
    dupre_g, dupre_v, d_wd, d_wg, d_wv, dbg, dbv, dcwg, dcwv = outs
    return (dupre_g, dupre_v, d_wd, jnp.concatenate([d_wg, d_wv], axis=0), jnp.concatenate([dbg, dbv], axis=0),
            jnp.concatenate([dcwg, dcwv], axis=0))


def _bwd_ffn_tokens(dupre_g, dupre_v, w_up_g, x2, dy, mix, g2, g3, tile):
    half, S, cs = dupre_g.shape
    D = x2.shape[1]

    def body(dg_ref, dv_ref, w_ref, x2_ref, dy_ref, mix_ref, g2_ref, g3_ref, dx2_ref, dmix_ref, dg3_ref, dg2_ref):
        _zero_when(pl.program_id(0) == 0, dg3_ref, dg2_ref)
        parts = [_dot_nt(dg_ref[d], w_ref[d]) for d in range(half)] + [_dot_nt(dv_ref[d], w_ref[d + half]) for d in range(half)]
        while len(parts) > 1:
            parts = [a + b for a, b in zip(parts[::2], parts[1::2])]
        dh2 = parts[0]
        x2 = x2_ref[...]
        r3 = _rms(x2)
        n3 = x2 * r3
        dg3_ref[...] += _colsum(dh2 * n3)
        dx2 = dy_ref[...] + _norm_bwd(dh2 * g3_ref[...], n3, r3)
        dx2_ref[...] = dx2
        mix = mix_ref[...]
        r2 = _rms(mix)
        n2 = mix * r2
        dg2_ref[...] += _colsum(dx2 * n2)
        dmix_ref[...] = _norm_bwd(dx2 * g2_ref[...], n2, r2).astype(BF16)

    row = lambda w: pl.BlockSpec((tile, w), lambda i: (i, 0))
    blk = pl.BlockSpec((half, tile, cs), lambda i: (0, i, 0))
    acc = pl.BlockSpec((1, D), lambda i: (0, 0))
    return pl.pallas_call(
        body, name="bwd_ffn_tokens", grid=(S // tile,),
        in_specs=[blk, blk, _const(w_up_g.shape), row(D), row(D), row(D), _const((1, D)), _const((1, D))],
        out_specs=[row(D), row(D), acc, acc],
        out_shape=[jax.ShapeDtypeStruct((S, D), F32), jax.ShapeDtypeStruct((S, D), BF16),
                   jax.ShapeDtypeStruct((1, D), F32), jax.ShapeDtypeStruct((1, D), F32)],
        compiler_params=_params("arbitrary"),
    )(dupre_g, dupre_v, w_up_g, x2, dy, mix, g2, g3)


def _bwd_outproj(dmix, w_out, pool_out, attn_out, pool_scale, attn_scale, tile):
    S, D = dmix.shape
    C = pool_out.shape[1]

    def body(dm_ref, w_ref, p_ref, a_ref, ps_ref, as_ref, dp_ref, da_ref, dw_ref, dps_ref, das_ref):
        _zero_when(pl.program_id(0) == 0, dw_ref, dps_ref, das_ref)
        dmx = dm_ref[...]
        dmerged = _dot_nt(dmx, w_ref[...])
        n_p, r_p, n_a, r_a = _normalized_heads(p_ref[...], a_ref[...])
        merged = jnp.concatenate([(n_p * ps_ref[...]).astype(BF16), (n_a * as_ref[...]).astype(BF16)], axis=1)
        dw_ref[...] += _dot_tn(merged, dmx)
        dm_p, dm_a = dmerged[:, :C], dmerged[:, C:]
        dps_ref[...] += _colsum(dm_p * n_p)
        das_ref[...] += _colsum(dm_a * n_a)
        dp_ref[...] = _norm_bwd(dm_p * ps_ref[...], n_p, r_p)
        da_ref[...] = _norm_bwd(dm_a * as_ref[...], n_a, r_a)

    row = lambda w: pl.BlockSpec((tile, w), lambda i: (i, 0))
    return pl.pallas_call(
        body, name="bwd_outproj", grid=(S // tile,),
        in_specs=[row(D), _const(w_out.shape), row(C), row(C), _const((1, C)), _const((1, C))],
        out_specs=[row(C), row(C), pl.BlockSpec(w_out.shape, lambda i: (0, 0)),
                   pl.BlockSpec((1, C), lambda i: (0, 0)), pl.BlockSpec((1, C), lambda i: (0, 0))],
        out_shape=[jax.ShapeDtypeStruct((S, C), F32), jax.ShapeDtypeStruct((S, C), F32),
                   jax.ShapeDtypeStruct(w_out.shape, F32), jax.ShapeDtypeStruct((1, C), F32), jax.ShapeDtypeStruct((1, C), F32)],
        compiler_params=_params("arbitrary"),
    )(dmix, w_out, pool_out, attn_out, pool_scale, attn_scale)


def _bwd_attn(qkv, d_attn, n_pairs, ex, subs):
    S = qkv.shape[0]
    n_steps = S // (subs * QB)

    def body(q_ref, k_ref, v_ref, do_ref, *rest):
        dq_ref, dk_ref, dv_ref = rest[ex.n:ex.n + 3]
        ex_refs = ex.split(rest[:ex.n] + rest[ex.n + 3:])
        first_step, last_step = _grid_ends((n_pairs, n_steps))

        @pl.when(first_step)
        def _():
            ex.start(*ex_refs)

        @pl.when(pl.program_id(1) == 0)
        def _():
            dk_ref[...] = jnp.zeros_like(dk_ref)
            dv_ref[...] = jnp.zeros_like(dv_ref)

        low_lanes = _low_lanes()
        after_s, from_s = _triangle(False, LOG_PIECES), _triangle(True, GRAD_PIECES)
        zero = jnp.zeros((QB, 1), F32)

        def tiles(qhs, dohs, totals, kws, vws, masks, cs, gs, scores=None):
            fw = _attn_weights(scores or _attn_scores(qhs, kws, masks), masks, cs, after_s)
            gvals = [t[2] * _dot_nt(doh, vw) for t, doh, vw in zip(fw, dohs, vws)]
            sums = [_suffix_sums(g, from_s, g0) for g, g0 in zip(gvals, gs)]
            totals = [tot if m is None else tot + sm[1] for tot, m, sm in zip(totals, masks, sums)]
            dzs = []
            for (z, e, _, _), g, (nearer, _), tot, m in zip(fw, gvals, sums, totals, masks):
                inv = 1.0 / (1.0 + e)
                sig_abs, sig_neg = inv, e * inv
                pos = z >= 0.0
                dz = g * jnp.where(pos, sig_neg, sig_abs) - jnp.where(pos, sig_abs, sig_neg) * (tot - nearer)
                if m is not None:
                    dz = jnp.where(m, dz, 0.0)
                dzs.append((dz * ATTN_SCALE).astype(BF16))
            dqs = [_dot(dz, kw) for dz, kw in zip(dzs, kws)]
            dks = [_dot_tn(dz, qh) for dz, qh in zip(dzs, qhs)]
            dvs = [_dot_tn(t[2].astype(BF16), doh) for t, doh in zip(fw, dohs)]
            return [(dq, dk, dv, t[3], sm[1], tot) for dq, dk, dv, t, sm, tot in zip(dqs, dks, dvs, fw, sums, totals)]

        def cond(c):
            return jnp.logical_and(c[0] >= 0, c[1] == 0)

        qhs, dohs, kws, vws, masks, first_blks, starts = [], [], [], [], [], [], []
        for sub in range(subs):
            i = pl.program_id(1) * subs + sub
            rows = slice(sub * QB, (sub + 1) * QB)
            first_blk, start, offset = _first_window(i)
            first_blks.append(first_blk)
            starts.append(start)
            qhs += _split_heads(q_ref[rows, :].astype(F32), low_lanes)
            dohs += _split_heads(do_ref[rows, :], low_lanes)
            kws += [k_ref[pl.ds(start, 2 * QB), :]] * 2
            vws += [v_ref[pl.ds(start, 2 * QB), :]] * 2
            masks += [_causal_mask(2 * QB, offset)] * 2
        zeros = [zero] * len(qhs)

        scores = _attn_scores(qhs, kws, masks)
        c_first = [_row_sums(sc[3], zero) for sc in scores]
        all_done = _all_done(c_first)

        def far_totals():
            beyond = []
            for sub in range(subs):
                pair = slice(2 * sub, 2 * sub + 2)

                def far_sums(c, qh=qhs[pair], doh=dohs[pair]):
                    j, _, c0, c1, r0, r1 = c
                    at = pl.multiple_of(j * QB, QB)
                    kb = k_ref[pl.ds(at, QB), :]
                    vb = v_ref[pl.ds(at, QB), :]
                    far = _attn_tiles(qh, [kb, kb], [None, None], [c0, c1], after_s)
                    r0 = r0 + jnp.sum(far[0][2] * _dot_nt(doh[0], vb), axis=1, keepdims=True)
                    r1 = r1 + jnp.sum(far[1][2] * _dot_nt(doh[1], vb), axis=1, keepdims=True)
                    return j - 1, _sweep_done(far[0][3], far[1][3]), far[0][3], far[1][3], r0, r1

                c0, c1 = c_first[pair]
                far = lax.while_loop(cond, far_sums, (first_blks[sub] - 1, _sweep_done(c0, c1), c0, c1, zero, zero))
                beyond += [far[4], far[5]]
            return tuple(beyond)

        beyond_first = list(lax.cond(all_done, lambda: tuple(zeros), far_totals))
        done = tiles(qhs, dohs, beyond_first, kws, vws, masks, zeros, zeros, scores)
        for sub in range(subs):
            dk_ref[pl.ds(starts[sub], 2 * QB), :] += done[2 * sub][1] + done[2 * sub + 1][1]
            dv_ref[pl.ds(starts[sub], 2 * QB), :] += done[2 * sub][2] + done[2 * sub + 1][2]
        first_dq = [jnp.where(low_lanes, done[2 * sub][0], done[2 * sub + 1][0]) for sub in range(subs)]

        def sweep_on():
            final = []
            for sub in range(subs):
                pair = slice(2 * sub, 2 * sub + 2)
                t0, t1 = done[pair]

                def step(c, qh=qhs[pair], doh=dohs[pair], total=[t0[5], t1[5]]):
                    j, _, dq, c0, c1, s0, s1 = c
                    at = pl.multiple_of(j * QB, QB)
                    kb = k_ref[pl.ds(at, QB), :]
                    vb = v_ref[pl.ds(at, QB), :]
                    f0, f1 = tiles(qh, doh, total, [kb, kb], [vb, vb], [None, None], [c0, c1], [s0, s1])
                    dk_ref[pl.ds(at, QB), :] += f0[1] + f1[1]
                    dv_ref[pl.ds(at, QB), :] += f0[2] + f1[2]
                    return j - 1, _sweep_done(f0[3], f1[3]), dq + jnp.where(low_lanes, f0[0], f1[0]), f0[3], f1[3], f0[4], f1[4]

                init = (first_blks[sub] - 1, _sweep_done(t0[3], t1[3]), first_dq[sub], t0[3], t1[3], t0[4], t1[4])
                final.append(lax.while_loop(cond, step, init)[2])
            return tuple(final)

        final = lax.cond(all_done, lambda: tuple(first_dq), sweep_on)
        for sub in range(subs):
            dq_ref[sub * QB:(sub + 1) * QB, :] = final[sub]

        @pl.when(last_step)
        def _():
            ex.wait(*ex_refs)

    blk = pl.BlockSpec((subs * QB, QB), lambda p, i: (i, p))
    full = lambda off: pl.BlockSpec((S, QB), lambda p, i: (0, off + p), pipeline_mode=pl.Buffered(1))
    outs = pl.pallas_call(
        body, name="bwd_attn", grid=(n_pairs, n_steps),
        in_specs=[blk, full(n_pairs), full(2 * n_pairs), blk] + ex.specs,
        out_specs=[blk, full(0), full(0)] + ex.specs,
        out_shape=[jax.ShapeDtypeStruct((S, n_pairs * QB), F32)] * 3 + ex.out_shape,
        scratch_shapes=ex.scratch,
        compiler_params=_params("arbitrary", "arbitrary"),
    )(qkv, qkv, qkv, d_attn, *ex.arrays)
    return outs[0], outs[1], outs[2], outs[3:]


def _bwd_pool(u, d_pool, w_pool, tile):
    S, C = u.shape
    n_tiles = S // tile
    ng = len(POOL_WINDOWS)

    def body(u_ref, uh_ref, d_ref, dh_ref, wp_ref, du_ref, dwp_ref):
        i = pl.program_id(0)
        first = i == 0
        _zero_when(first, dwp_ref)
        halo = jnp.where(first, 0.0, uh_ref[...])
        parts = _pool_deviation(u_ref[...], halo, i * tile)
        dout = d_ref[...]
        nxt = jnp.where(i == n_tiles - 1, 0.0, dh_ref[...])
        dext = jnp.concatenate([dout, nxt], axis=0).astype(BF16)
        counts = _pool_counts(i * tile, tile + HALO)
        dps, scaled = [], []
        for g in range(ng):
            lanes = slice(g * POOL_GROUP, (g + 1) * POOL_GROUP)
            dp = _dot_nt(dext[:, lanes], wp_ref[g].astype(BF16))
            dps.append(dp[:tile])
            scaled.append(dp / counts[g])
        sums = _window_sums(jnp.concatenate(scaled, axis=1), forward=True)
        for g, w in enumerate(POOL_WINDOWS):
            lanes = slice(g * POOL_GROUP, (g + 1) * POOL_GROUP)
            du_ref[:, lanes] = sums[w][:tile, lanes] - dps[g]
            dwp_ref[g] += _dot_tn(parts[g].astype(BF16), dext[:tile, lanes])

    row = pl.BlockSpec((tile, C), lambda i: (i, 0))
    return pl.pallas_call(
        body, name="bwd_pool", grid=(n_tiles,),
        in_specs=[row, _prev_halo_spec(tile, C), row, _next_halo_spec(tile, C, n_tiles), _const(w_pool.shape)],
        out_specs=[row, pl.BlockSpec(w_pool.shape, lambda i: (0, 0, 0))],
        out_shape=[jax.ShapeDtypeStruct((S, C), F32), jax.ShapeDtypeStruct(w_pool.shape, F32)],
        compiler_params=_params("arbitrary"),
    )(u, u, d_pool, d_pool, w_pool)


def _bwd_w_in(du, dq, dk, dv, h1_t, n_blocks, tile):
    D, S = h1_t.shape
    C = du.shape[1]
    cs = 4 * C // n_blocks
    per = C // cs

    def body(du_ref, dq_ref, dk_ref, dv_ref, ht_ref, dproj_ref, dw_ref):
        _zero_when(pl.program_id(0) == 0, dw_ref)
        ht = ht_ref[...]
        for d in range(n_blocks):
            src = (du_ref, dq_ref, dk_ref, dv_ref)[d // per]
            dproj = src[:, (d % per) * cs:(d % per + 1) * cs].astype(BF16)
            dproj_ref[:, d * cs:(d + 1) * cs] = dproj
            dw_ref[d] += _dot(ht, dproj)

    row = lambda w: pl.BlockSpec((tile, w), lambda i: (i, 0))
    return pl.pallas_call(
        body, name="bwd_w_in", grid=(S // tile,),
        in_specs=[row(C), row(C), row(C), row(C), pl.BlockSpec((D, tile), lambda i: (0, i))],
        out_specs=[row(4 * C), pl.BlockSpec((n_blocks, D, cs), lambda i: (0, 0, 0))],
        out_shape=[jax.ShapeDtypeStruct((S, 4 * C), BF16), jax.ShapeDtypeStruct((n_blocks, D, cs), F32)],
        compiler_params=_params("arbitrary"),
    )(du, dq, dk, dv, h1_t)


def _bwd_x(dproj, w_in_t, x, dx2, g1, tile, ex):
    S, D = x.shape
    n_tiles = S // tile

    def body(dp_ref, w_ref, x_ref, dx2_ref, g_ref, *rest):
        dx_ref, dg_ref = rest[ex.n:ex.n + 2]
        ex_refs = ex.split(rest[:ex.n] + rest[ex.n + 2:])
        first, last = _grid_ends((n_tiles,))

        @pl.when(first)
        def _():
            ex.start(*ex_refs)
            dg_ref[...] = jnp.zeros_like(dg_ref)

        dh = _dot(dp_ref[...], w_ref[...])
        xf = x_ref[...]
        r1 = _rms(xf)
        n1 = xf * r1
        dg_ref[...] += _colsum(dh * n1)
        dx_ref[...] = dx2_ref[...] + _norm_bwd(dh * g_ref[...], n1, r1)

        @pl.when(last)
        def _():
            ex.wait(*ex_refs)

    row = lambda w: pl.BlockSpec((tile, w), lambda i: (i, 0))
    outs = pl.pallas_call(
        body, name="bwd_x", grid=(n_tiles,),
        in_specs=[row(w_in_t.shape[0]), _const(w_in_t.shape), row(D), row(D), _const((1, D))] + ex.specs,
        out_specs=[row(D), pl.BlockSpec((1, D), lambda i: (0, 0))] + ex.specs,
        out_shape=[jax.ShapeDtypeStruct((S, D), F32), jax.ShapeDtypeStruct((1, D), F32)] + ex.out_shape,
        scratch_shapes=ex.scratch,
        compiler_params=_params("arbitrary"),
    )(dproj, w_in_t, x, dx2, g1, *ex.arrays)
    return outs[0], outs[1], outs[2:]


def _mesh_position():
    x, y, c = lax.axis_index("x"), lax.axis_index("y"), lax.axis_index("c")
    return x, y, c, 4 * x + 2 * y + c


def _peer(x, y, c, k):
    px = 1 - x if k & 4 else x
    py = 1 - y if k & 2 else y
    pc = 1 - c if k & 1 else c
    return (px, py, pc), 4 * px + 2 * py + pc


class _Exchange:
    def __init__(self, arrays, gather):
        self.arrays, self.gather, self.n = list(arrays), gather, len(arrays)
        self.out_shape = [jax.ShapeDtypeStruct(((N_DEV,) + a.shape) if gather else a.shape, a.dtype) for a in arrays]
        self.specs = [pl.BlockSpec(memory_space=pl.ANY)] * self.n
        copies = self.n * (N_DEV - 1)
        self.scratch = [pltpu.SemaphoreType.DMA((copies,)), pltpu.SemaphoreType.DMA((copies,)),
                        pltpu.SemaphoreType.DMA((self.n,))]

    def _copies(self, ins, outs, sems):
        send_sems, recv_sems, local_sems = sems
        x, y, c, me = _mesh_position()
        local, remote = [], []
        for a in range(self.n):
            mine = ins[a] if self.gather else ins[a].at[me]
            local.append(pltpu.make_async_copy(mine, outs[a].at[me], local_sems.at[a]))
            for k in range(1, N_DEV):
                peer, peer_idx = _peer(x, y, c, k)
                src = ins[a] if self.gather else ins[a].at[peer_idx]
                sem = a * (N_DEV - 1) + k - 1
                remote.append(pltpu.make_async_remote_copy(
                    src_ref=src, dst_ref=outs[a].at[me], send_sem=send_sems.at[sem], recv_sem=recv_sems.at[sem],
                    device_id=peer, device_id_type=MESH))
        return local, remote

    def start(self, ins, outs, sems):
        local, remote = self._copies(ins, outs, sems)
        for cp in local + remote:
            cp.start()

    def wait(self, ins, outs, sems):
        local, remote = self._copies(ins, outs, sems)
        for cp in remote:
            cp.wait_send()
        for cp in remote:
            cp.wait_recv()
        for cp in local:
            cp.wait()

    def split(self, refs):
        return refs[:self.n], refs[self.n:2 * self.n], refs[2 * self.n:]


def _all_to_all(arrays, gather, name):
    ex = _Exchange(arrays, gather)

    def body(*refs):
        ins, outs, sems = ex.split(refs)
        ex.start(ins, outs, sems)
        ex.wait(ins, outs, sems)

    return pl.pallas_call(body, name=name, in_specs=ex.specs, out_specs=ex.specs, out_shape=ex.out_shape,
                          scratch_shapes=ex.scratch)(*ex.arrays)


def _reduce_adamw(parts, w, m, v, rows):
    R, C = w.shape

    def body(p_ref, w_ref, m_ref, v_ref, g_ref, d_ref, nm_ref, nv_ref):
        g = p_ref[0].astype(F32)
        for s in range(1, N_DEV):
            g = g + p_ref[s].astype(F32)
        g_ref[...] = g
        m_new = ADAM_B1 * m_ref[...] + (1.0 - ADAM_B1) * g
        v_new = ADAM_B2 * v_ref[...] + (1.0 - ADAM_B2) * (g * g)
        m_hat = m_new / (1.0 - ADAM_B1 ** ADAM_STEP)
        v_hat = v_new / (1.0 - ADAM_B2 ** ADAM_STEP)
        d_ref[...] = -ADAM_LR * (m_hat / (jnp.sqrt(v_hat) + ADAM_EPS) + ADAM_WD * w_ref[...])
        nm_ref[...] = m_new
        nv_ref[...] = v_new

    row = pl.BlockSpec((rows, C), lambda i: (i, 0))
    return pl.pallas_call(
        body, name="reduce_adamw", grid=(R // rows,),
        in_specs=[pl.BlockSpec((N_DEV, rows, C), lambda i: (0, i, 0)), row, row, row],
        out_specs=[row] * 4, out_shape=[jax.ShapeDtypeStruct((R, C), F32)] * 4,
        compiler_params=_params("parallel"),
    )(parts, w, m, v)


def _row_tile(rows, cols):
    fits = [t for t in range(8, rows + 1, 8) if rows % t == 0 and N_DEV * t * cols * 4 <= 4 * 1024 * 1024]
    return max(fits) if fits else rows


SMALL_COLS = 1024


def _pack_small(vals):
    rows = []
    for a in vals:
        flat = a.reshape(-1)
        pad = (-flat.shape[0]) % SMALL_COLS
        rows.append(jnp.pad(flat, (0, pad)).reshape(-1, SMALL_COLS))
    packed = jnp.concatenate(rows, axis=0)
    return jnp.pad(packed, ((0, (-packed.shape[0]) % 8), (0, 0)))


def _unpack_small(packed, like):
    out, r = [], 0
    for a in like:
        n = a.size
        nr = -(-n // SMALL_COLS)
        out.append(packed[r:r + nr].reshape(-1)[:n].reshape(a.shape))
        r += nr
    return out


def kernel(x, norm_mix_pre, w_in, w_pool, pool_scale, attn_scale, w_out, norm_mix_post, norm_ffn_pre, w_up, conv_w, conv_b, w_down, norm_ffn_post, loss_target, m_norm_mix_pre, m_w_in, m_w_pool, m_pool_scale, m_attn_scale, m_w_out, m_norm_mix_post, m_norm_ffn_pre, m_w_up, m_conv_w, m_conv_b, m_w_down, m_norm_ffn_post, v_norm_mix_pre, v_w_in, v_w_pool, v_pool_scale, v_attn_scale, v_w_out, v_norm_mix_post, v_norm_ffn_pre, v_w_up, v_conv_w, v_conv_b, v_w_down, v_norm_ffn_post):
    S, D = x.shape[1], x.shape[2]
    d_ff_block = w_up.shape[2]

    xs, target = x[0], loss_target[0]
    g1, g2, g3, g4 = norm_mix_pre, norm_mix_post, norm_ffn_pre, norm_ffn_post
    big = min(512, S)
    small = min(256, S)
    n_pairs = pool_scale.shape[1] // QB
    conv_b_g = conv_b.reshape(N_DEV, 1, d_ff_block)

    (w_in_g,) = _all_to_all([w_in[0].astype(BF16)], gather=True, name="gather_w_in")
    h1_t, u, qkv = _fwd_inproj(xs, g1, w_in_g, big)
    pool_out = _fwd_pool(u, w_pool[0], big)
    attn_out, (w_out_g, w_up_g, w_down_g, conv_w_g) = _fwd_attn(
        qkv, n_pairs, _Exchange([w_out[0].astype(BF16), w_up[0].astype(BF16), w_down[0].astype(BF16), conv_w[0]], gather=True),
        min(ATTN_FWD_BLOCKS, S // QB))
    w_out_full = w_out_g.reshape(D, D)
    w_down4 = w_down_g.reshape(D_FF_SHARDS, d_ff_block, D)
    mix, x2, h2, h2_t = _fwd_outproj(pool_out, attn_out, pool_scale, attn_scale, w_out_full, xs, g2, g3, big)
    upre, gate_val, dy, df, loss_cols, dg4 = _fwd_ffn_loss(h2, w_up_g, conv_w_g, conv_b_g, w_down4, x2, target, g4, small)
    loss = lax.psum(0.5 * jnp.sum(loss_cols) / D, ("x", "y", "c"))

    dupre_g, dupre_v, d_wd4, d_wup, d_cb, d_cw = _bwd_ffn_blocks(gate_val, upre, conv_w_g, w_down4, df, h2_t, min(1024, S))
    dx2, dmix, dg3, dg2 = _bwd_ffn_tokens(dupre_g, dupre_v, w_up_g, x2, dy, mix, g2, g3, big)
    d_pool, d_attn, d_wout, d_ps, d_as = _bwd_outproj(dmix, w_out_full, pool_out, attn_out, pool_scale, attn_scale, big)
    d_wdown_g = d_wd4.reshape(N_DEV, w_down.shape[1], D)
    d_wout_g = d_wout.reshape(N_DEV, D // N_DEV, D)
    dq, dk, dv, late_parts = _bwd_attn(qkv, d_attn, n_pairs, _Exchange([d_wout_g, d_wup, d_wdown_g, d_cw], gather=False),
                                       min(ATTN_BWD_BLOCKS, S // QB))
    du, d_wp = _bwd_pool(u, d_pool, w_pool[0], big)
    dproj, d_win = _bwd_w_in(du, dq, dk, dv, h1_t, N_DEV, big)
    w_in_t = w_in_g.transpose(0, 2, 1).reshape(-1, D)
    dx, dg1, (win_parts,) = _bwd_x(dproj, w_in_t, xs, dx2, g1, big, _Exchange([d_win], gather=False))
    big_parts = [win_parts] + list(late_parts)
    r = dict(dx=dx, g1=dg1, w_pool=d_wp, pool_scale=d_ps, attn_scale=d_as, g2=dg2, g3=dg3, conv_b=d_cb, g4=dg4)

    small_names = ["norm_mix_pre", "w_pool", "pool_scale", "attn_scale", "norm_mix_post", "norm_ffn_pre", "conv_b", "norm_ffn_post"]
    small_w = dict(norm_mix_pre=norm_mix_pre, w_pool=w_pool, pool_scale=pool_scale, attn_scale=attn_scale,
                   norm_mix_post=norm_mix_post, norm_ffn_pre=norm_ffn_pre, conv_b=conv_b, norm_ffn_post=norm_ffn_post)
    small_m = dict(norm_mix_pre=m_norm_mix_pre, w_pool=m_w_pool, pool_scale=m_pool_scale, attn_scale=m_attn_scale,
                   norm_mix_post=m_norm_mix_post, norm_ffn_pre=m_norm_ffn_pre, conv_b=m_conv_b, norm_ffn_post=m_norm_ffn_post)
    small_v = dict(norm_mix_pre=v_norm_mix_pre, w_pool=v_w_pool, pool_scale=v_pool_scale, attn_scale=v_attn_scale,
                   norm_mix_post=v_norm_mix_post, norm_ffn_pre=v_norm_ffn_pre, conv_b=v_conv_b, norm_ffn_post=v_norm_ffn_post)
    small_g = dict(norm_mix_pre=r["g1"], w_pool=r["w_pool"], pool_scale=r["pool_scale"], attn_scale=r["attn_scale"],
                   norm_mix_post=r["g2"], norm_ffn_pre=r["g3"], conv_b=r["conv_b"], norm_ffn_post=r["g4"])
    like = [small_w[n] for n in small_names]
    packed_g = _pack_small([small_g[n] for n in small_names])

    (small_parts,) = _all_to_all([packed_g], gather=True, name="gather_small_grads")

    def update(parts, w, m, v):
        R, C = w.shape
        return _reduce_adamw(parts, w, m, v, _row_tile(R, C))

    res = {}
    res["w_in"] = update(big_parts[0], w_in[0], m_w_in[0], v_w_in[0])
    res["w_out"] = update(big_parts[1], w_out[0], m_w_out[0], v_w_out[0])
    res["w_up"] = update(big_parts[2], w_up[0], m_w_up[0], v_w_up[0])
    res["w_down"] = update(big_parts[3], w_down[0], m_w_down[0], v_w_down[0])
    res["conv_w"] = update(big_parts[4], conv_w[0], m_conv_w[0], v_conv_w[0])
    small_res = update(small_parts, _pack_small(like), _pack_small([small_m[n] for n in small_names]),
                       _pack_small([small_v[n] for n in small_names]))
    small_res = [_unpack_small(t, like) for t in small_res]
    for idx, n in enumerate(small_names):
        res[n] = tuple(t[idx] for t in small_res)

    order = ["norm_mix_pre", "w_in", "w_pool", "pool_scale", "attn_scale", "w_out", "norm_mix_post", "norm_ffn_pre",
             "w_up", "conv_w", "conv_b", "w_down", "norm_ffn_post"]
    shaped = {n: tuple(t.reshape(s.shape) for t in res[n])
              for n, s in dict(norm_mix_pre=norm_mix_pre, w_in=w_in, w_pool=w_pool, pool_scale=pool_scale, attn_scale=attn_scale,
                               w_out=w_out, norm_mix_post=norm_mix_post, norm_ffn_pre=norm_ffn_pre, w_up=w_up, conv_w=conv_w,
                               conv_b=conv_b, w_down=w_down, norm_ffn_post=norm_ffn_post).items()}
    outs = [loss, r["dx"].reshape(x.shape)]
    for k in range(4):
        outs += [shaped[n][k] for n in order]
    return tuple(outs)
```

```python
import functools

import jax
import jax.numpy as jnp
from jax import lax
from jax.experimental import pallas as pl
from jax.experimental.pallas import tpu as pltpu

F32 = jnp.float32
BF16 = jnp.bfloat16
HIGHEST = lax.Precision.HIGHEST

N_DEV = 8
EPS = 1e-6
POOL_WINDOWS = (2, 4, 8, 16)
POOL_GROUP = 128
HALO = 16
HEAD_DIM = 64
QB = 128
ATTN_SCALE = HEAD_DIM ** -0.5
ATTN_FWD_BLOCKS = 8
ATTN_BWD_BLOCKS = 8
EXP_UNDERFLOW = -88.0
D_FF_SHARDS = 4

ADAM_LR = 0.001
ADAM_B1 = 0.9
ADAM_B2 = 0.999
ADAM_EPS = 1e-08
ADAM_WD = 0.01
ADAM_STEP = 10

VMEM_LIMIT_V7X = 56 * 1024 * 1024
MESH = pl.DeviceIdType.MESH


def _params(*semantics):
    return pltpu.CompilerParams(dimension_semantics=semantics, vmem_limit_bytes=VMEM_LIMIT_V7X)


def _const(shape):
    zeros = (0,) * len(shape)
    return pl.BlockSpec(shape, lambda *_: zeros, pipeline_mode=pl.Buffered(1))


def _dot(a, b):
    return jnp.dot(a, b, preferred_element_type=F32)


def _dot_nt(a, b):
    return lax.dot_general(a, b, (((1,), (1,)), ((), ())), preferred_element_type=F32)


def _dot_tn(a, b):
    return lax.dot_general(a, b, (((0,), (0,)), ((), ())), preferred_element_type=F32)


def _rms(v):
    return lax.rsqrt(jnp.mean(v * v, axis=-1, keepdims=True) + EPS)


def _norm_bwd(dn_times_gain, n, r):
    return r * (dn_times_gain - n * jnp.mean(dn_times_gain * n, axis=-1, keepdims=True))


def _zero_when(first, *refs):
    @pl.when(first)
    def _():
        for ref in refs:
            ref[...] = jnp.zeros_like(ref)


def _colsum(v):
    return jnp.sum(v, axis=0, keepdims=True)


def _grid_ends(grid):
    ids = [pl.program_id(a) for a in range(len(grid))]
    first = functools.reduce(jnp.logical_and, [i == 0 for i in ids])
    last = functools.reduce(jnp.logical_and, [i == n - 1 for i, n in zip(ids, grid)])
    return first, last


def _fwd_inproj(x, g1, w_in_g, tile):
    S, D = x.shape
    nb, _, cs = w_in_g.shape
    d_pool = 2 * cs

    def body(x_ref, g_ref, w_ref, ht_ref, u_ref, qkv_ref):
        xf = x_ref[...]
        h = (xf * _rms(xf) * g_ref[...]).astype(BF16)
        ht_ref[...] = h.T
        for d in range(nb):
            o = _dot(h, w_ref[d])
            if d < 2:
                u_ref[:, d * cs:(d + 1) * cs] = o
            else:
                qkv_ref[:, (d - 2) * cs:(d - 1) * cs] = o.astype(BF16)

    return pl.pallas_call(
        body, name="fwd_inproj", grid=(S // tile,),
        in_specs=[pl.BlockSpec((tile, D), lambda i: (i, 0)), _const((1, D)), _const(w_in_g.shape)],
        out_specs=[pl.BlockSpec((D, tile), lambda i: (0, i)), pl.BlockSpec((tile, d_pool), lambda i: (i, 0)),
                   pl.BlockSpec((tile, 3 * d_pool), lambda i: (i, 0))],
        out_shape=[jax.ShapeDtypeStruct((D, S), BF16), jax.ShapeDtypeStruct((S, d_pool), F32),
                   jax.ShapeDtypeStruct((S, 3 * d_pool), BF16)],
        compiler_params=_params("parallel"),
    )(x, g1, w_in_g)


def _window_sums(ext, forward):
    n = ext.shape[0]
    sums, s, sh = {}, ext, 1
    while sh < POOL_WINDOWS[-1]:
        s = s + pltpu.roll(s, (n - sh) if forward else sh, axis=0)
        sh *= 2
        sums[sh] = s
    return sums


def _pool_counts(t0, rows):
    t1 = (lax.broadcasted_iota(jnp.int32, (rows, 1), 0) + t0 + 1).astype(F32)
    return [jnp.minimum(t1, float(w)) for w in POOL_WINDOWS]


def _pool_deviation(u, halo, t0):
    T = u.shape[0]
    sums = _window_sums(jnp.concatenate([halo, u], axis=0), forward=False)
    counts = _pool_counts(t0, T)
    parts = []
    for g, w in enumerate(POOL_WINDOWS):
        lanes = slice(g * POOL_GROUP, (g + 1) * POOL_GROUP)
        parts.append(sums[w][HALO:, lanes] / counts[g] - u[:, lanes])
    return parts


def _prev_halo_spec(tile, width):
    return pl.BlockSpec((HALO, width), lambda i: (jnp.maximum(i * (tile // HALO) - 1, 0), 0))


def _next_halo_spec(tile, width, n_tiles):
    last = n_tiles * (tile // HALO) - 1
    return pl.BlockSpec((HALO, width), lambda i: (jnp.minimum((i + 1) * (tile // HALO), last), 0))


def _fwd_pool(u, w_pool, tile):
    S, C = u.shape

    def body(u_ref, halo_ref, wp_ref, o_ref):
        i = pl.program_id(0)
        halo = jnp.where(i > 0, halo_ref[...], 0.0)
        parts = _pool_deviation(u_ref[...], halo, i * tile)
        for g, p in enumerate(parts):
            o_ref[:, g * POOL_GROUP:(g + 1) * POOL_GROUP] = _dot(p.astype(BF16), wp_ref[g].astype(BF16))

    return pl.pallas_call(
        body, name="fwd_pool", grid=(S // tile,),
        in_specs=[pl.BlockSpec((tile, C), lambda i: (i, 0)), _prev_halo_spec(tile, C), _const(w_pool.shape)],
        out_specs=pl.BlockSpec((tile, C), lambda i: (i, 0)),
        out_shape=jax.ShapeDtypeStruct((S, C), F32),
        compiler_params=_params("parallel"),
    )(u, u, w_pool)


def _low_lanes():
    return lax.broadcasted_iota(jnp.int32, (QB, 2 * HEAD_DIM), 1) < HEAD_DIM


LOG_PIECES = 2
GRAD_PIECES = 3


def _triangle(inclusive, pieces):
    row = lax.broadcasted_iota(jnp.int32, (pieces * QB, QB), 0) % QB
    col = lax.broadcasted_iota(jnp.int32, (pieces * QB, QB), 1)
    return ((row >= col) if inclusive else (row > col)).astype(BF16)


def _pieces(v, n):
    out, rest = [], v
    for _ in range(n - 1):
        piece = rest.astype(BF16)
        out.append(piece)
        rest = rest - piece.astype(F32)
    out.append(rest.astype(BF16))
    return jnp.concatenate(out, axis=1)


def _causal_mask(width, offset):
    row = lax.broadcasted_iota(jnp.int32, (QB, width), 0)
    col = lax.broadcasted_iota(jnp.int32, (QB, width), 1)
    return col < row + offset


def _row_sums(vals, carry):
    for b in reversed(range(vals.shape[1] // QB)):
        carry = carry + jnp.sum(vals[:, b * QB:(b + 1) * QB], axis=1, keepdims=True)
    return carry


def _suffix_sums(vals, tri, carry):
    n = vals.shape[1] // QB
    out, run = [None] * n, carry
    for b in reversed(range(n)):
        blk = vals[:, b * QB:(b + 1) * QB]
        out[b] = _dot(_pieces(blk, tri.shape[0] // QB), tri) + run
        run = run + jnp.sum(blk, axis=1, keepdims=True)
    return (out[0] if n == 1 else jnp.concatenate(out, axis=1)), run


def _attn_tiles(qhs, kws, masks, carries, after_s):
    return _attn_weights(_attn_scores(qhs, kws, masks), masks, carries, after_s)


def _attn_scores(qhs, kws, masks):
    zs = [_dot_nt(qh, kw) * ATTN_SCALE for qh, kw in zip(qhs, kws)]
    es = [jnp.exp(-jnp.abs(z)) for z in zs]
    softplus = [jnp.maximum(z, 0.0) + jnp.log(1.0 + e) for z, e in zip(zs, es)]
    log_1m_beta = [-sp if m is None else jnp.where(m, -sp, 0.0) for sp, m in zip(softplus, masks)]
    return list(zip(zs, es, softplus, log_1m_beta))


def _attn_weights(scores, masks, carries, after_s):
    sums = [_suffix_sums(l, after_s, c) for (_, _, _, l), c in zip(scores, carries)]
    weights = [jnp.exp(z - sp + st) for (z, _, sp, _), (st, _) in zip(scores, sums)]
    weights = [a if m is None else jnp.where(m, a, 0.0) for a, m in zip(weights, masks)]
    return [(z, e, a, c) for (z, e, _, _), a, (_, c) in zip(scores, weights, sums)]


def _split_heads(v, low_lanes):
    return jnp.where(low_lanes, v, 0.0).astype(BF16), jnp.where(low_lanes, 0.0, v).astype(BF16)


def _sweep_done(c0, c1):
    return (jnp.maximum(jnp.max(c0), jnp.max(c1)) < EXP_UNDERFLOW).astype(jnp.int32)


def _all_done(carries):
    return jnp.max(functools.reduce(jnp.maximum, carries)) < EXP_UNDERFLOW


def _first_window(i):
    first_blk = jnp.maximum(i - 1, 0)
    return first_blk, pl.multiple_of(first_blk * QB, QB), (i - first_blk) * QB


def _fwd_attn(qkv, n_pairs, ex, subs):
    S = qkv.shape[0]
    n_steps = S // (subs * QB)

    def body(q_ref, k_ref, v_ref, *rest):
        o_ref = rest[ex.n]
        ex_refs = ex.split(rest[:ex.n] + rest[ex.n + 1:])
        first_step, last_step = _grid_ends((n_pairs, n_steps))

        @pl.when(first_step)
        def _():
            ex.start(*ex_refs)

        low_lanes = _low_lanes()
        after_s = _triangle(False, LOG_PIECES)
        zero = jnp.zeros((QB, 1), F32)

        def cond(c):
            return jnp.logical_and(c[0] >= 0, c[1] == 0)

        qhs, kws, vws, masks, first_blks = [], [], [], [], []
        for sub in range(subs):
            i = pl.program_id(1) * subs + sub
            first_blk, start, offset = _first_window(i)
            first_blks.append(first_blk)
            qhs += _split_heads(q_ref[sub * QB:(sub + 1) * QB, :].astype(F32), low_lanes)
            kws += [k_ref[pl.ds(start, 2 * QB), :]] * 2
            vws += [v_ref[pl.ds(start, 2 * QB), :]] * 2
            masks += [_causal_mask(2 * QB, offset)] * 2
        tiles = _attn_tiles(qhs, kws, masks, [zero] * len(qhs), after_s)
        outs = [_dot(t[2].astype(BF16), vw) for t, vw in zip(tiles, vws)]

        first_out = [jnp.where(low_lanes, outs[2 * sub], outs[2 * sub + 1]) for sub in range(subs)]

        def sweep_on():
            final = []
            for sub in range(subs):
                def step(c, qh=qhs[2 * sub:2 * sub + 2]):
                    j, _, acc, c0, c1 = c
                    at = pl.multiple_of(j * QB, QB)
                    kb = k_ref[pl.ds(at, QB), :]
                    vb = v_ref[pl.ds(at, QB), :]
                    far = _attn_tiles(qh, [kb, kb], [None, None], [c0, c1], after_s)
                    acc = acc + jnp.where(low_lanes, _dot(far[0][2].astype(BF16), vb), _dot(far[1][2].astype(BF16), vb))
                    return j - 1, _sweep_done(far[0][3], far[1][3]), acc, far[0][3], far[1][3]

                c0, c1 = tiles[2 * sub][3], tiles[2 * sub + 1][3]
                final.append(lax.while_loop(cond, step, (first_blks[sub] - 1, _sweep_done(c0, c1), first_out[sub], c0, c1))[2])
            return tuple(final)

        final = lax.cond(_all_done([t[3] for t in tiles]), lambda: tuple(first_out), sweep_on)
        for sub in range(subs):
            o_ref[sub * QB:(sub + 1) * QB, :] = final[sub]

        @pl.when(last_step)
        def _():
            ex.wait(*ex_refs)

    outs = pl.pallas_call(
        body, name="fwd_attn", grid=(n_pairs, n_steps),
        in_specs=[pl.BlockSpec((subs * QB, QB), lambda p, i: (i, p)),
                  pl.BlockSpec((S, QB), lambda p, i: (0, n_pairs + p), pipeline_mode=pl.Buffered(1)),
                  pl.BlockSpec((S, QB), lambda p, i: (0, 2 * n_pairs + p), pipeline_mode=pl.Buffered(1))] + ex.specs,
        out_specs=[pl.BlockSpec((subs * QB, QB), lambda p, i: (i, p))] + ex.specs,
        out_shape=[jax.ShapeDtypeStruct((S, n_pairs * QB), F32)] + ex.out_shape,
        scratch_shapes=ex.scratch,
        compiler_params=_params("arbitrary", "arbitrary"),
    )(qkv, qkv, qkv, *ex.arrays)
    return outs[0], outs[1:]


def _normalized_heads(pool_out, attn_out):
    rp, ra = _rms(pool_out), _rms(attn_out)
    return pool_out * rp, rp, attn_out * ra, ra


def _fwd_outproj(pool_out, attn_out, pool_scale, attn_scale, w_out, x, g2, g3, tile):
    S, D = x.shape
    C = pool_out.shape[1]

    def body(p_ref, a_ref, ps_ref, as_ref, w_ref, x_ref, g2_ref, g3_ref, mix_ref, x2_ref, h2_ref, h2t_ref):
        n_p, _, n_a, _ = _normalized_heads(p_ref[...], a_ref[...])
        mix = _dot((n_p * ps_ref[...]).astype(BF16), w_ref[:C, :]) + _dot((n_a * as_ref[...]).astype(BF16), w_ref[C:, :])
        mix_ref[...] = mix
        x2 = x_ref[...] + mix * _rms(mix) * g2_ref[...]
        x2_ref[...] = x2
        h2 = (x2 * _rms(x2) * g3_ref[...]).astype(BF16)
        h2_ref[...] = h2
        h2t_ref[...] = h2.T

    row = lambda w: pl.BlockSpec((tile, w), lambda i: (i, 0))
    return pl.pallas_call(
        body, name="fwd_outproj", grid=(S // tile,),
        in_specs=[row(C), row(C), _const((1, C)), _const((1, C)), _const(w_out.shape), row(D), _const((1, D)), _const((1, D))],
        out_specs=[row(D), row(D), row(D), pl.BlockSpec((D, tile), lambda i: (0, i))],
        out_shape=[jax.ShapeDtypeStruct((S, D), F32), jax.ShapeDtypeStruct((S, D), F32), jax.ShapeDtypeStruct((S, D), BF16),
                   jax.ShapeDtypeStruct((D, S), BF16)],
        compiler_params=_params("parallel"),
    )(pool_out, attn_out, pool_scale, attn_scale, w_out, x, g2, g3)


def _conv_taps(tile_rows, halo_rows):
    T = tile_rows.shape[0]
    ext = jnp.concatenate([halo_rows.astype(F32), tile_rows.astype(F32)], axis=0)
    return pltpu.roll(ext, 2, axis=0)[HALO:], pltpu.roll(ext, 1, axis=0)[HALO:], ext[HALO:]


def _tap_rows(cw_ref, d):
    return [cw_ref[d, k:k + 1, :] for k in range(3)]


def _gated_unit(taps_gate, taps_val, cw_gate, cw_val, cb_gate, cb_val):
    gate = cw_gate[0] * taps_gate[0] + cw_gate[1] * taps_gate[1] + cw_gate[2] * taps_gate[2] + cb_gate
    val = cw_val[0] * taps_val[0] + cw_val[1] * taps_val[1] + cw_val[2] * taps_val[2] + cb_val
    sig = 1.0 / (1.0 + jnp.exp(-gate))
    return gate, val, sig


def _fwd_ffn_loss(h2, w_up_g, conv_w_g, conv_b_g, w_down4, x2, target, g4, tile):
    S, D = x2.shape
    nb, _, cs = w_up_g.shape
    half = D_FF_SHARDS

    def body(h_ref, w_ref, cw_ref, cb_ref, wd_ref, x2_ref, t_ref, g4_ref, upre_ref, gv_ref, dy_ref, df_ref, loss_ref, dg4_ref, halo_ref):
        _zero_when(pl.program_id(0) == 0, loss_ref, dg4_ref, halo_ref)
        h = h_ref[...]

        def up(s):
            return _dot(h, w_ref[s]), _dot(h, w_ref[s + half])

        f = jnp.zeros((tile, D), F32)
        ahead = up(0)
        for s in range(half):
            ug, uv = ahead
            if s + 1 < half:
                ahead = up(s + 1)
            upre_ref[s] = ug.astype(BF16)
            upre_ref[s + half] = uv.astype(BF16)
            gate, val, sig = _gated_unit(_conv_taps(ug, halo_ref[s]), _conv_taps(uv, halo_ref[s + half]),
                                         _tap_rows(cw_ref, s), _tap_rows(cw_ref, s + half), cb_ref[s], cb_ref[s + half])
            halo_ref[s] = ug[tile - HALO:, :]
            halo_ref[s + half] = uv[tile - HALO:, :]
            gv_ref[s] = gate.astype(BF16)
            gv_ref[s + half] = val.astype(BF16)
            f = f + _dot((gate * sig * val).astype(BF16), wd_ref[s])
        r4 = _rms(f)
        n4 = f * r4
        err = x2_ref[...] + n4 * g4_ref[...] - t_ref[...]
        dy = err * (1.0 / D)
        dy_ref[...] = dy
        df_ref[...] = _norm_bwd(dy * g4_ref[...], n4, r4).astype(BF16)
        loss_ref[...] += _colsum(err * err)
        dg4_ref[...] += _colsum(dy * n4)

    row = lambda w: pl.BlockSpec((tile, w), lambda i: (i, 0))
    return pl.pallas_call(
        body, name="fwd_ffn_loss", grid=(S // tile,),
        in_specs=[row(D), _const(w_up_g.shape), _const(conv_w_g.shape), _const(conv_b_g.shape), _const(w_down4.shape),
                  row(D), row(D), _const((1, D))],
        out_specs=[pl.BlockSpec((nb, tile, cs), lambda i: (0, i, 0)), pl.BlockSpec((nb, tile, cs), lambda i: (0, i, 0)), row(D), row(D),
                   pl.BlockSpec((1, D), lambda i: (0, 0)), pl.BlockSpec((1, D), lambda i: (0, 0))],
        out_shape=[jax.ShapeDtypeStruct((nb, S, cs), BF16), jax.ShapeDtypeStruct((nb, S, cs), BF16),
                   jax.ShapeDtypeStruct((S, D), F32), jax.ShapeDtypeStruct((S, D), BF16),
                   jax.ShapeDtypeStruct((1, D), F32), jax.ShapeDtypeStruct((1, D), F32)],
        scratch_shapes=[pltpu.VMEM((nb, HALO, cs), F32)],
        compiler_params=_params("arbitrary"),
    )(h2, w_up_g, conv_w_g, conv_b_g, w_down4, x2, target, g4)


def _bwd_down(upre, conv_w_g, conv_b_g, w_down4, df, tile):
    nb, S, cs = upre.shape
    D = df.shape[1]
    n_tiles = S // tile

    def body(ug_ref, uv_ref, hg_ref, hv_ref, cwg_ref, cwv_ref, cbg_ref, cbv_ref, wd_ref, df_ref,
             dg_ref, dv_ref, dwd_ref, dbg_ref, dbv_ref, dcwg_ref, dcwv_ref):
        i = pl.program_id(1)
        first = i == 0
        _zero_when(first, dwd_ref, dbg_ref, dbv_ref, dcwg_ref, dcwv_ref)
        halo_g = jnp.where(first, jnp.zeros_like(hg_ref[0]), hg_ref[0])
        halo_v = jnp.where(first, jnp.zeros_like(hv_ref[0]), hv_ref[0])
        taps_g, taps_v = _conv_taps(ug_ref[0], halo_g), _conv_taps(uv_ref[0], halo_v)
        gate, val, sig = _gated_unit(taps_g, taps_v, _tap_rows(cwg_ref, 0), _tap_rows(cwv_ref, 0), cbg_ref[0], cbv_ref[0])
        silu = gate * sig
        dfb = df_ref[...]
        dact = _dot_nt(dfb, wd_ref[0])
        dwd_ref[0] += _dot_tn((silu * val).astype(BF16), dfb)
        dgate = dact * val * (sig * (1.0 + gate * (1.0 - sig)))
        dval = dact * silu
        dg_ref[0] = dgate.astype(BF16)
        dv_ref[0] = dval.astype(BF16)
        dbg_ref[0] += _colsum(dgate)
        dbv_ref[0] += _colsum(dval)
        for k in range(3):
            dcwg_ref[0, k:k + 1, :] += _colsum(dgate * taps_g[k])
            dcwv_ref[0, k:k + 1, :] += _colsum(dval * taps_v[k])

    half = D_FF_SHARDS
    blk = lambda off: pl.BlockSpec((1, tile, cs), lambda s, i: (s + off, i, 0))
    halo = lambda off: pl.BlockSpec((1, HALO, cs), lambda s, i: (s + off, jnp.maximum(i * (tile // HALO) - 1, 0), 0))
    par = lambda off, r: pl.BlockSpec((1, r, cs), lambda s, i: (s + off, 0, 0))
    outs = pl.pallas_call(
        body, name="bwd_down", grid=(half, n_tiles),
        in_specs=[blk(0), blk(half), halo(0), halo(half), par(0, 3), par(half, 3), par(0, 1), par(half, 1),
                  pl.BlockSpec((1, cs, D), lambda s, i: (s, 0, 0)), pl.BlockSpec((tile, D), lambda s, i: (i, 0))],
        out_specs=[blk(0), blk(0), pl.BlockSpec((1, cs, D), lambda s, i: (s, 0, 0)),
                   par(0, 1), par(0, 1), par(0, 3), par(0, 3)],
        out_shape=[jax.ShapeDtypeStruct((half, S, cs), BF16), jax.ShapeDtypeStruct((half, S, cs), BF16),
                   jax.ShapeDtypeStruct((half, cs, D), F32),
                   jax.ShapeDtypeStruct((half, 1, cs), F32), jax.ShapeDtypeStruct((half, 1, cs), F32),
                   jax.ShapeDtypeStruct((half, 3, cs), F32), jax.ShapeDtypeStruct((half, 3, cs), F32)],
        compiler_params=_params("parallel", "arbitrary"),
    )(upre, upre, upre, upre, conv_w_g, conv_w_g, conv_b_g, conv_b_g, w_down4, df)
    dgate, dval, d_wd, dbg, dbv, dcwg, dcwv = outs
    return dgate, dval, d_wd, jnp.concatenate([dbg, dbv], axis=0), jnp.concatenate([dcwg, dcwv], axis=0)


def _bwd_up_x(dgate, dval, conv_w_g, w_up_g, x2, dy, mix, g2, g3, tile):
    half, S, cs = dgate.shape
    nb = 2 * half
    D = x2.shape[1]
    n_tiles = S // tile

    def body(dg_ref, dv_ref, hg_ref, hv_ref, cw_ref, w_ref, x2_ref, dy_ref, mix_ref, g2_ref, g3_ref,
             dupre_ref, dx2_ref, dmix_ref, dg3_ref, dg2_ref):
        i = pl.program_id(0)
        last = i == n_tiles - 1
        _zero_when(i == 0, dg3_ref, dg2_ref)
        dh2 = jnp.zeros((tile, D), F32)
        for d in range(nb):
            src, halo = (dg_ref, hg_ref) if d < half else (dv_ref, hv_ref)
            nxt = jnp.where(last, jnp.zeros_like(halo[d % half]), halo[d % half])
            ext = jnp.concatenate([src[d % half].astype(F32), nxt.astype(F32)], axis=0)
            n = ext.shape[0]
            cw = _tap_rows(cw_ref, d)
            dupre = (cw[2] * ext + cw[1] * pltpu.roll(ext, n - 1, axis=0) + cw[0] * pltpu.roll(ext, n - 2, axis=0))[:tile]
            dupre = dupre.astype(BF16)
            dupre_ref[d] = dupre
            dh2 = dh2 + _dot_nt(dupre, w_ref[d])
        x2 = x2_ref[...]
        r3 = _rms(x2)
        n3 = x2 * r3
        dg3_ref[...] += _colsum(dh2 * n3)
        dx2 = dy_ref[...] + _norm_bwd(dh2 * g3_ref[...], n3, r3)
        dx2_ref[...] = dx2
        mix = mix_ref[...]
        r2 = _rms(mix)
        n2 = mix * r2
        dg2_ref[...] += _colsum(dx2 * n2)
        dmix_ref[...] = _norm_bwd(dx2 * g2_ref[...], n2, r2).astype(BF16)

    row = lambda w: pl.BlockSpec((tile, w), lambda i: (i, 0))
    blk = pl.BlockSpec((half, tile, cs), lambda i: (0, i, 0))
    last_halo = n_tiles * (tile // HALO) - 1
    halo = pl.BlockSpec((half, HALO, cs), lambda i: (0, jnp.minimum((i + 1) * (tile // HALO), last_halo), 0))
    acc = pl.BlockSpec((1, D), lambda i: (0, 0))
    return pl.pallas_call(
        body, name="bwd_up_x", grid=(n_tiles,),
        in_specs=[blk, blk, halo, halo, _const(conv_w_g.shape), _const(w_up_g.shape), row(D), row(D), row(D),
                  _const((1, D)), _const((1, D))],
        out_specs=[pl.BlockSpec((nb, tile, cs), lambda i: (0, i, 0)), row(D), row(D), acc, acc],
        out_shape=[jax.ShapeDtypeStruct((nb, S, cs), BF16), jax.ShapeDtypeStruct((S, D), F32),
                   jax.ShapeDtypeStruct((S, D), BF16), jax.ShapeDtypeStruct((1, D), F32), jax.ShapeDtypeStruct((1, D), F32)],
        compiler_params=_params("arbitrary"),
    )(dgate, dval, dgate, dval, conv_w_g, w_up_g, x2, dy, mix, g2, g3)


def _bwd_weight(act_t, dout, tile):
    D, S = act_t.shape
    nb, _, cs = dout.shape

    def body(a_ref, d_ref, o_ref):
        _zero_when(pl.program_id(1) == 0, o_ref)
        o_ref[0] += _dot(a_ref[...], d_ref[0])

    return pl.pallas_call(
        body, name="bwd_w_up", grid=(nb, S // tile),
        in_specs=[pl.BlockSpec((D, tile), lambda d, i: (0, i)), pl.BlockSpec((1, tile, cs), lambda d, i: (d, i, 0))],
        out_specs=pl.BlockSpec((1, D, cs), lambda d, i: (d, 0, 0)),
        out_shape=jax.ShapeDtypeStruct((nb, D, cs), F32),
        compiler_params=_params("parallel", "arbitrary"),
    )(act_t, dout)


def _bwd_ffn_blocks(gate_val, upre, conv_w_g, w_down4, df, h2_t, tile):
    nb, S, cs = upre.shape
    D = df.shape[1]
    n_tiles = S // tile
    half = D_FF_SHARDS

    def body(g_ref, v_ref, ug_ref, uv_ref, cwg_ref, cwv_ref, wd_ref, df_ref, ht_ref,
             dug_ref, duv_ref, dwd_ref, dwg_ref, dwv_ref, dbg_ref, dbv_ref, dcwg_ref, dcwv_ref, next_ref):
        _zero_when(pl.program_id(1) == 0, dwd_ref, dwg_ref, dwv_ref, dbg_ref, dbv_ref, dcwg_ref, dcwv_ref, next_ref)
        dfb = df_ref[...]
        dact = _dot_nt(dfb, wd_ref[0])
        gate, val = g_ref[0].astype(F32), v_ref[0].astype(F32)
        sig = 1.0 / (1.0 + jnp.exp(-gate))
        silu = gate * sig
        dwd_ref[0] += _dot_tn((silu * val).astype(BF16), dfb)
        ht = ht_ref[...]

        def through_conv(dup, slot, cw_ref, u_ref, du_ref, dw_ref, db_ref, dcw_ref):
            ext = jnp.concatenate([dup, next_ref[slot]], axis=0)
            n = ext.shape[0]
            shifted = (dup, pltpu.roll(ext, n - 1, axis=0)[:tile], pltpu.roll(ext, n - 2, axis=0)[:tile])
            next_ref[slot] = dup[:HALO]
            cw = _tap_rows(cw_ref, 0)
            dupre = (cw[2] * shifted[0] + cw[1] * shifted[1] + cw[0] * shifted[2]).astype(BF16)
            du_ref[0] = dupre
            dw_ref[0] += _dot(ht, dupre)
            u = u_ref[0].astype(F32)
            db_ref[0] += _colsum(dup)
            for k in range(3):
                dcw_ref[0, k:k + 1, :] += _colsum(shifted[2 - k] * u)

        through_conv(dact * val * (sig * (1.0 + gate * (1.0 - sig))), 0, cwg_ref, ug_ref, dug_ref, dwg_ref, dbg_ref, dcwg_ref)
        through_conv(dact * silu, 1, cwv_ref, uv_ref, duv_ref, dwv_ref, dbv_ref, dcwv_ref)

    rev = lambda i: n_tiles - 1 - i
    blk = lambda off: pl.BlockSpec((1, tile, cs), lambda s, i: (s + off, rev(i), 0))
    par = lambda off, r: pl.BlockSpec((1, r, cs), lambda s, i: (s + off, 0, 0))
    acc = lambda r, c: pl.BlockSpec((1, r, c), lambda s, i: (s, 0, 0), pipeline_mode=pl.Buffered(1))
    outs = pl.pallas_call(
        body, name="bwd_ffn_blocks", grid=(half, n_tiles),
        in_specs=[blk(0), blk(half), blk(0), blk(half), par(0, 3), par(half, 3),
                  acc(cs, D), pl.BlockSpec((tile, D), lambda s, i: (rev(i), 0)), pl.BlockSpec((D, tile), lambda s, i: (0, rev(i)))],
        out_specs=[blk(0), blk(0), acc(cs, D), acc(D, cs), acc(D, cs), acc(1, cs), acc(1, cs), acc(3, cs), acc(3, cs)],
        out_shape=[jax.ShapeDtypeStruct((half, S, cs), BF16), jax.ShapeDtypeStruct((half, S, cs), BF16),
                   jax.ShapeDtypeStruct((half, cs, D), F32),
                   jax.ShapeDtypeStruct((half, D, cs), F32), jax.ShapeDtypeStruct((half, D, cs), F32),
                   jax.ShapeDtypeStruct((half, 1, cs), F32), jax.ShapeDtypeStruct((half, 1, cs), F32),
                   jax.ShapeDtypeStruct((half, 3, cs), F32), jax.ShapeDtypeStruct((half, 3, cs), F32)],
        scratch_shapes=[pltpu.VMEM((2, HALO, cs), F32)],
        compiler_params=_params("arbitrary", "arbitrary"),
    )(gate_val, gate_val, upre, upre, conv_w_g, conv_w_g, w_down4, df, h2_t)
    dupre_g, dupre_v, d_wd, d_wg, d_wv, dbg, dbv, dcwg, dcwv = outs
    return (dupre_g, dupre_v, d_wd, jnp.concatenate([d_wg, d_wv], axis=0), jnp.concatenate([dbg, dbv], axis=0),
            jnp.concatenate([dcwg, dcwv], axis=0))


def _bwd_ffn_tokens(dupre_g, dupre_v, w_up_g, x2, dy, mix, g2, g3, tile):
    half, S, cs = dupre_g.shape
    D = x2.shape[1]

    def body(dg_ref, dv_ref, w_ref, x2_ref, dy_ref, mix_ref, g2_ref, g3_ref, dx2_ref, dmix_ref, dg3_ref, dg2_ref):
        _zero_when(pl.program_id(0) == 0, dg3_ref, dg2_ref)
        parts = [_dot_nt(dg_ref[d], w_ref[d]) for d in range(half)] + [_dot_nt(dv_ref[d], w_ref[d + half]) for d in range(half)]
        while len(parts) > 1:
            parts = [a + b for a, b in zip(parts[::2], parts[1::2])]
        dh2 = parts[0]
        x2 = x2_ref[...]
        r3 = _rms(x2)
        n3 = x2 * r3
        dg3_ref[...] += _colsum(dh2 * n3)
        dx2 = dy_ref[...] + _norm_bwd(dh2 * g3_ref[...], n3, r3)
        dx2_ref[...] = dx2
        mix = mix_ref[...]
        r2 = _rms(mix)
        n2 = mix * r2
        dg2_ref[...] += _colsum(dx2 * n2)
        dmix_ref[...] = _norm_bwd(dx2 * g2_ref[...], n2, r2).astype(BF16)

    row = lambda w: pl.BlockSpec((tile, w), lambda i: (i, 0))
    blk = pl.BlockSpec((half, tile, cs), lambda i: (0, i, 0))
    acc = pl.BlockSpec((1, D), lambda i: (0, 0))
    return pl.pallas_call(
        body, name="bwd_ffn_tokens", grid=(S // tile,),
        in_specs=[blk, blk, _const(w_up_g.shape), row(D), row(D), row(D), _const((1, D)), _const((1, D))],
        out_specs=[row(D), row(D), acc, acc],
        out_shape=[jax.ShapeDtypeStruct((S, D), F32), jax.ShapeDtypeStruct((S, D), BF16),
                   jax.ShapeDtypeStruct((1, D), F32), jax.ShapeDtypeStruct((1, D), F32)],
        compiler_params=_params("arbitrary"),
    )(dupre_g, dupre_v, w_up_g, x2, dy, mix, g2, g3)


def _bwd_outproj(dmix, w_out, pool_out, attn_out, pool_scale, attn_scale, tile):
    S, D = dmix.shape
    C = pool_out.shape[1]

    def body(dm_ref, w_ref, p_ref, a_ref, ps_ref, as_ref, dp_ref, da_ref, dw_ref, dps_ref, das_ref):
        _zero_when(pl.program_id(0) == 0, dw_ref, dps_ref, das_ref)
        dmx = dm_ref[...]
        dmerged = _dot_nt(dmx, w_ref[...])
        n_p, r_p, n_a, r_a = _normalized_heads(p_ref[...], a_ref[...])
        merged = jnp.concatenate([(n_p * ps_ref[...]).astype(BF16), (n_a * as_ref[...]).astype(BF16)], axis=1)
        dw_ref[...] += _dot_tn(merged, dmx)
        dm_p, dm_a = dmerged[:, :C], dmerged[:, C:]
        dps_ref[...] += _colsum(dm_p * n_p)
        das_ref[...] += _colsum(dm_a * n_a)
        dp_ref[...] = _norm_bwd(dm_p * ps_ref[...], n_p, r_p)
        da_ref[...] = _norm_bwd(dm_a * as_ref[...], n_a, r_a)

    row = lambda w: pl.BlockSpec((tile, w), lambda i: (i, 0))
    return pl.pallas_call(
        body, name="bwd_outproj", grid=(S // tile,),
        in_specs=[row(D), _const(w_out.shape), row(C), row(C), _const((1, C)), _const((1, C))],
        out_specs=[row(C), row(C), pl.BlockSpec(w_out.shape, lambda i: (0, 0)),
                   pl.BlockSpec((1, C), lambda i: (0, 0)), pl.BlockSpec((1, C), lambda i: (0, 0))],
        out_shape=[jax.ShapeDtypeStruct((S, C), F32), jax.ShapeDtypeStruct((S, C), F32),
                   jax.ShapeDtypeStruct(w_out.shape, F32), jax.ShapeDtypeStruct((1, C), F32), jax.ShapeDtypeStruct((1, C), F32)],
        compiler_params=_params("arbitrary"),
    )(dmix, w_out, pool_out, attn_out, pool_scale, attn_scale)


def _bwd_attn(qkv, d_attn, n_pairs, ex, subs):
    S = qkv.shape[0]
    n_steps = S // (subs * QB)

    def body(q_ref, k_ref, v_ref, do_ref, *rest):
        dq_ref, dk_ref, dv_ref = rest[ex.n:ex.n + 3]
        ex_refs = ex.split(rest[:ex.n] + rest[ex.n + 3:])
        first_step, last_step = _grid_ends((n_pairs, n_steps))

        @pl.when(first_step)
        def _():
            ex.start(*ex_refs)

        @pl.when(pl.program_id(1) == 0)
        def _():
            dk_ref[...] = jnp.zeros_like(dk_ref)
            dv_ref[...] = jnp.zeros_like(dv_ref)

        low_lanes = _low_lanes()
        after_s, from_s = _triangle(False, LOG_PIECES), _triangle(True, GRAD_PIECES)
        zero = jnp.zeros((QB, 1), F32)

        def tiles(qhs, dohs, totals, kws, vws, masks, cs, gs, scores=None):
            fw = _attn_weights(scores or _attn_scores(qhs, kws, masks), masks, cs, after_s)
            gvals = [t[2] * _dot_nt(doh, vw) for t, doh, vw in zip(fw, dohs, vws)]
            sums = [_suffix_sums(g, from_s, g0) for g, g0 in zip(gvals, gs)]
            totals = [tot if m is None else tot + sm[1] for tot, m, sm in zip(totals, masks, sums)]
            dzs = []
            for (z, e, _, _), g, (nearer, _), tot, m in zip(fw, gvals, sums, totals, masks):
                inv = 1.0 / (1.0 + e)
                sig_abs, sig_neg = inv, e * inv
                pos = z >= 0.0
                dz = g * jnp.where(pos, sig_neg, sig_abs) - jnp.where(pos, sig_abs, sig_neg) * (tot - nearer)
                if m is not None:
                    dz = jnp.where(m, dz, 0.0)
                dzs.append((dz * ATTN_SCALE).astype(BF16))
            dqs = [_dot(dz, kw) for dz, kw in zip(dzs, kws)]
            dks = [_dot_tn(dz, qh) for dz, qh in zip(dzs, qhs)]
            dvs = [_dot_tn(t[2].astype(BF16), doh) for t, doh in zip(fw, dohs)]
            return [(dq, dk, dv, t[3], sm[1], tot) for dq, dk, dv, t, sm, tot in zip(dqs, dks, dvs, fw, sums, totals)]

        def cond(c):
            return jnp.logical_and(c[0] >= 0, c[1] == 0)

        qhs, dohs, kws, vws, masks, first_blks, starts = [], [], [], [], [], [], []
        for sub in range(subs):
            i = pl.program_id(1) * subs + sub
            rows = slice(sub * QB, (sub + 1) * QB)
            first_blk, start, offset = _first_window(i)
            first_blks.append(first_blk)
            starts.append(start)
            qhs += _split_heads(q_ref[rows, :].astype(F32), low_lanes)
            dohs += _split_heads(do_ref[rows, :], low_lanes)
            kws += [k_ref[pl.ds(start, 2 * QB), :]] * 2
            vws += [v_ref[pl.ds(start, 2 * QB), :]] * 2
            masks += [_causal_mask(2 * QB, offset)] * 2
        zeros = [zero] * len(qhs)

        scores = _attn_scores(qhs, kws, masks)
        c_first = [_row_sums(sc[3], zero) for sc in scores]
        all_done = _all_done(c_first)

        def far_totals():
            beyond = []
            for sub in range(subs):
                pair = slice(2 * sub, 2 * sub + 2)

                def far_sums(c, qh=qhs[pair], doh=dohs[pair]):
                    j, _, c0, c1, r0, r1 = c
                    at = pl.multiple_of(j * QB, QB)
                    kb = k_ref[pl.ds(at, QB), :]
                    vb = v_ref[pl.ds(at, QB), :]
                    far = _attn_tiles(qh, [kb, kb], [None, None], [c0, c1], after_s)
                    r0 = r0 + jnp.sum(far[0][2] * _dot_nt(doh[0], vb), axis=1, keepdims=True)
                    r1 = r1 + jnp.sum(far[1][2] * _dot_nt(doh[1], vb), axis=1, keepdims=True)
                    return j - 1, _sweep_done(far[0][3], far[1][3]), far[0][3], far[1][3], r0, r1

                c0, c1 = c_first[pair]
                far = lax.while_loop(cond, far_sums, (first_blks[sub] - 1, _sweep_done(c0, c1), c0, c1, zero, zero))
                beyond += [far[4], far[5]]
            return tuple(beyond)

        beyond_first = list(lax.cond(all_done, lambda: tuple(zeros), far_totals))
        done = tiles(qhs, dohs, beyond_first, kws, vws, masks, zeros, zeros, scores)
        for sub in range(subs):
            dk_ref[pl.ds(starts[sub], 2 * QB), :] += done[2 * sub][1] + done[2 * sub + 1][1]
            dv_ref[pl.ds(starts[sub], 2 * QB), :] += done[2 * sub][2] + done[2 * sub + 1][2]
        first_dq = [jnp.where(low_lanes, done[2 * sub][0], done[2 * sub + 1][0]) for sub in range(subs)]

        def sweep_on():
            final = []
            for sub in range(subs):
                pair = slice(2 * sub, 2 * sub + 2)
                t0, t1 = done[pair]

                def step(c, qh=qhs[pair], doh=dohs[pair], total=[t0[5], t1[5]]):
                    j, _, dq, c0, c1, s0, s1 = c
                    at = pl.multiple_of(j * QB, QB)
                    kb = k_ref[pl.ds(at, QB), :]
                    vb = v_ref[pl.ds(at, QB), :]
                    f0, f1 = tiles(qh, doh, total, [kb, kb], [vb, vb], [None, None], [c0, c1], [s0, s1])
                    dk_ref[pl.ds(at, QB), :] += f0[1] + f1[1]
                    dv_ref[pl.ds(at, QB), :] += f0[2] + f1[2]
                    return j - 1, _sweep_done(f0[3], f1[3]), dq + jnp.where(low_lanes, f0[0], f1[0]), f0[3], f1[3], f0[4], f1[4]

                init = (first_blks[sub] - 1, _sweep_done(t0[3], t1[3]), first_dq[sub], t0[3], t1[3], t0[4], t1[4])
                final.append(lax.while_loop(cond, step, init)[2])
            return tuple(final)

        final = lax.cond(all_done, lambda: tuple(first_dq), sweep_on)
        for sub in range(subs):
            dq_ref[sub * QB:(sub + 1) * QB, :] = final[sub]

        @pl.when(last_step)
        def _():
            ex.wait(*ex_refs)

    blk = pl.BlockSpec((subs * QB, QB), lambda p, i: (i, p))
    full = lambda off: pl.BlockSpec((S, QB), lambda p, i: (0, off + p), pipeline_mode=pl.Buffered(1))
    outs = pl.pallas_call(
        body, name="bwd_attn", grid=(n_pairs, n_steps),
        in_specs=[blk, full(n_pairs), full(2 * n_pairs), blk] + ex.specs,
        out_specs=[blk, full(0), full(0)] + ex.specs,
        out_shape=[jax.ShapeDtypeStruct((S, n_pairs * QB), F32)] * 3 + ex.out_shape,
        scratch_shapes=ex.scratch,
        compiler_params=_params("arbitrary", "arbitrary"),
    )(qkv, qkv, qkv, d_attn, *ex.arrays)
    return outs[0], outs[1], outs[2], outs[3:]


def _bwd_pool(u, d_pool, w_pool, tile):
    S, C = u.shape
    n_tiles = S // tile
    ng = len(POOL_WINDOWS)

    def body(u_ref, uh_ref, d_ref, dh_ref, wp_ref, du_ref, dwp_ref):
        i = pl.program_id(0)
        first = i == 0
        _zero_when(first, dwp_ref)
        halo = jnp.where(first, 0.0, uh_ref[...])
        parts = _pool_deviation(u_ref[...], halo, i * tile)
        dout = d_ref[...]
        nxt = jnp.where(i == n_tiles - 1, 0.0, dh_ref[...])
        dext = jnp.concatenate([dout, nxt], axis=0).astype(BF16)
        counts = _pool_counts(i * tile, tile + HALO)
        dps, scaled = [], []
        for g in range(ng):
            lanes = slice(g * POOL_GROUP, (g + 1) * POOL_GROUP)
            dp = _dot_nt(dext[:, lanes], wp_ref[g].astype(BF16))
            dps.append(dp[:tile])
            scaled.append(dp / counts[g])
        sums = _window_sums(jnp.concatenate(scaled, axis=1), forward=True)
        for g, w in enumerate(POOL_WINDOWS):
            lanes = slice(g * POOL_GROUP, (g + 1) * POOL_GROUP)
            du_ref[:, lanes] = sums[w][:tile, lanes] - dps[g]
            dwp_ref[g] += _dot_tn(parts[g].astype(BF16), dext[:tile, lanes])

    row = pl.BlockSpec((tile, C), lambda i: (i, 0))
    return pl.pallas_call(
        body, name="bwd_pool", grid=(n_tiles,),
        in_specs=[row, _prev_halo_spec(tile, C), row, _next_halo_spec(tile, C, n_tiles), _const(w_pool.shape)],
        out_specs=[row, pl.BlockSpec(w_pool.shape, lambda i: (0, 0, 0))],
        out_shape=[jax.ShapeDtypeStruct((S, C), F32), jax.ShapeDtypeStruct(w_pool.shape, F32)],
        compiler_params=_params("arbitrary"),
    )(u, u, d_pool, d_pool, w_pool)


def _bwd_w_in(du, dq, dk, dv, h1_t, n_blocks, tile):
    D, S = h1_t.shape
    C = du.shape[1]
    cs = 4 * C // n_blocks
    per = C // cs

    def body(du_ref, dq_ref, dk_ref, dv_ref, ht_ref, dproj_ref, dw_ref):
        _zero_when(pl.program_id(0) == 0, dw_ref)
        ht = ht_ref[...]
        for d in range(n_blocks):
            src = (du_ref, dq_ref, dk_ref, dv_ref)[d // per]
            dproj = src[:, (d % per) * cs:(d % per + 1) * cs].astype(BF16)
            dproj_ref[:, d * cs:(d + 1) * cs] = dproj
            dw_ref[d] += _dot(ht, dproj)

    row = lambda w: pl.BlockSpec((tile, w), lambda i: (i, 0))
    return pl.pallas_call(
        body, name="bwd_w_in", grid=(S // tile,),
        in_specs=[row(C), row(C), row(C), row(C), pl.BlockSpec((D, tile), lambda i: (0, i))],
        out_specs=[row(4 * C), pl.BlockSpec((n_blocks, D, cs), lambda i: (0, 0, 0))],
        out_shape=[jax.ShapeDtypeStruct((S, 4 * C), BF16), jax.ShapeDtypeStruct((n_blocks, D, cs), F32)],
        compiler_params=_params("arbitrary"),
    )(du, dq, dk, dv, h1_t)


def _bwd_x(dproj, w_in_t, x, dx2, g1, tile, ex):
    S, D = x.shape
    n_tiles = S // tile

    def body(dp_ref, w_ref, x_ref, dx2_ref, g_ref, *rest):
        dx_ref, dg_ref = rest[ex.n:ex.n + 2]
        ex_refs = ex.split(rest[:ex.n] + rest[ex.n + 2:])
        first, last = _grid_ends((n_tiles,))

        @pl.when(first)
        def _():
            ex.start(*ex_refs)
            dg_ref[...] = jnp.zeros_like(dg_ref)

        dh = _dot(dp_ref[...], w_ref[...])
        xf = x_ref[...]
        r1 = _rms(xf)
        n1 = xf * r1
        dg_ref[...] += _colsum(dh * n1)
        dx_ref[...] = dx2_ref[...] + _norm_bwd(dh * g_ref[...], n1, r1)

        @pl.when(last)
        def _():
            ex.wait(*ex_refs)

    row = lambda w: pl.BlockSpec((tile, w), lambda i: (i, 0))
    outs = pl.pallas_call(
        body, name="bwd_x", grid=(n_tiles,),
        in_specs=[row(w_in_t.shape[0]), _const(w_in_t.shape), row(D), row(D), _const((1, D))] + ex.specs,
        out_specs=[row(D), pl.BlockSpec((1, D), lambda i: (0, 0))] + ex.specs,
        out_shape=[jax.ShapeDtypeStruct((S, D), F32), jax.ShapeDtypeStruct((1, D), F32)] + ex.out_shape,
        scratch_shapes=ex.scratch,
        compiler_params=_params("arbitrary"),
    )(dproj, w_in_t, x, dx2, g1, *ex.arrays)
    return outs[0], outs[1], outs[2:]


def _mesh_position():
    x, y, c = lax.axis_index("x"), lax.axis_index("y"), lax.axis_index("c")
    return x, y, c, 4 * x + 2 * y + c


def _peer(x, y, c, k):
    px = 1 - x if k & 4 else x
    py = 1 - y if k & 2 else y
    pc = 1 - c if k & 1 else c
    return (px, py, pc), 4 * px + 2 * py + pc


class _Exchange:
    def __init__(self, arrays, gather):
        self.arrays, self.gather, self.n = list(arrays), gather, len(arrays)
        self.out_shape = [jax.ShapeDtypeStruct(((N_DEV,) + a.shape) if gather else a.shape, a.dtype) for a in arrays]
        self.specs = [pl.BlockSpec(memory_space=pl.ANY)] * self.n
        copies = self.n * (N_DEV - 1)
        self.scratch = [pltpu.SemaphoreType.DMA((copies,)), pltpu.SemaphoreType.DMA((copies,)),
                        pltpu.SemaphoreType.DMA((self.n,))]

    def _copies(self, ins, outs, sems):
        send_sems, recv_sems, local_sems = sems
        x, y, c, me = _mesh_position()
        local, remote = [], []
        for a in range(self.n):
            mine = ins[a] if self.gather else ins[a].at[me]
            local.append(pltpu.make_async_copy(mine, outs[a].at[me], local_sems.at[a]))
            for k in range(1, N_DEV):
                peer, peer_idx = _peer(x, y, c, k)
                src = ins[a] if self.gather else ins[a].at[peer_idx]
                sem = a * (N_DEV - 1) + k - 1
                remote.append(pltpu.make_async_remote_copy(
                    src_ref=src, dst_ref=outs[a].at[me], send_sem=send_sems.at[sem], recv_sem=recv_sems.at[sem],
                    device_id=peer, device_id_type=MESH))
        return local, remote

    def start(self, ins, outs, sems):
        local, remote = self._copies(ins, outs, sems)
        for cp in local + remote:
            cp.start()

    def wait(self, ins, outs, sems):
        local, remote = self._copies(ins, outs, sems)
        for cp in remote:
            cp.wait_send()
        for cp in remote:
            cp.wait_recv()
        for cp in local:
            cp.wait()

    def split(self, refs):
        return refs[:self.n], refs[self.n:2 * self.n], refs[2 * self.n:]


def _all_to_all(arrays, gather, name):
    ex = _Exchange(arrays, gather)

    def body(*refs):
        ins, outs, sems = ex.split(refs)
        ex.start(ins, outs, sems)
        ex.wait(ins, outs, sems)

    return pl.pallas_call(body, name=name, in_specs=ex.specs, out_specs=ex.specs, out_shape=ex.out_shape,
                          scratch_shapes=ex.scratch)(*ex.arrays)


def _reduce_adamw(parts, w, m, v, rows):
    R, C = w.shape

    def body(p_ref, w_ref, m_ref, v_ref, g_ref, d_ref, nm_ref, nv_ref):
        g = p_ref[0].astype(F32)
        for s in range(1, N_DEV):
            g = g + p_ref[s].astype(F32)
        g_ref[...] = g
        m_new = ADAM_B1 * m_ref[...] + (1.0 - ADAM_B1) * g
        v_new = ADAM_B2 * v_ref[...] + (1.0 - ADAM_B2) * (g * g)
        m_hat = m_new / (1.0 - ADAM_B1 ** ADAM_STEP)
        v_hat = v_new / (1.0 - ADAM_B2 ** ADAM_STEP)
        d_ref[...] = -ADAM_LR * (m_hat / (jnp.sqrt(v_hat) + ADAM_EPS) + ADAM_WD * w_ref[...])
        nm_ref[...] = m_new
        nv_ref[...] = v_new

    row = pl.BlockSpec((rows, C), lambda i: (i, 0))
    return pl.pallas_call(
        body, name="reduce_adamw", grid=(R // rows,),
        in_specs=[pl.BlockSpec((N_DEV, rows, C), lambda i: (0, i, 0)), row, row, row],
        out_specs=[row] * 4, out_shape=[jax.ShapeDtypeStruct((R, C), F32)] * 4,
        compiler_params=_params("parallel"),
    )(parts, w, m, v)


def _row_tile(rows, cols):
    fits = [t for t in range(8, rows + 1, 8) if rows % t == 0 and N_DEV * t * cols * 4 <= 4 * 1024 * 1024]
    return max(fits) if fits else rows


SMALL_COLS = 1024


def _pack_small(vals):
    rows = []
    for a in vals:
        flat = a.reshape(-1)
        pad = (-flat.shape[0]) % SMALL_COLS
        rows.append(jnp.pad(flat, (0, pad)).reshape(-1, SMALL_COLS))
    packed = jnp.concatenate(rows, axis=0)
    return jnp.pad(packed, ((0, (-packed.shape[0]) % 8), (0, 0)))


def _unpack_small(packed, like):
    out, r = [], 0
    for a in like:
        n = a.size
        nr = -(-n // SMALL_COLS)
        out.append(packed[r:r + nr].reshape(-1)[:n].reshape(a.shape))
        r += nr
    return out


def kernel(x, norm_mix_pre, w_in, w_pool, pool_scale, attn_scale, w_out, norm_mix_post, norm_ffn_pre, w_up, conv_w, conv_b, w_down, norm_ffn_post, loss_target, m_norm_mix_pre, m_w_in, m_w_pool, m_pool_scale, m_attn_scale, m_w_out, m_norm_mix_post, m_norm_ffn_pre, m_w_up, m_conv_w, m_conv_b, m_w_down, m_norm_ffn_post, v_norm_mix_pre, v_w_in, v_w_pool, v_pool_scale, v_attn_scale, v_w_out, v_norm_mix_post, v_norm_ffn_pre, v_w_up, v_conv_w, v_conv_b, v_w_down, v_norm_ffn_post):
    S, D = x.shape[1], x.shape[2]
    d_ff_block = w_up.shape[2]

    xs, target = x[0], loss_target[0]
    g1, g2, g3, g4 = norm_mix_pre, norm_mix_post, norm_ffn_pre, norm_ffn_post
    big = min(512, S)
    small = min(256, S)
    n_pairs = pool_scale.shape[1] // QB
    conv_b_g = conv_b.reshape(N_DEV, 1, d_ff_block)

    (w_in_g,) = _all_to_all([w_in[0].astype(BF16)], gather=True, name="gather_w_in")
    h1_t, u, qkv = _fwd_inproj(xs, g1, w_in_g, big)
    pool_out = _fwd_pool(u, w_pool[0], big)
    attn_out, (w_out_g, w_up_g, w_down_g, conv_w_g) = _fwd_attn(
        qkv, n_pairs, _Exchange([w_out[0].astype(BF16), w_up[0].astype(BF16), w_down[0].astype(BF16), conv_w[0]], gather=True),
        min(ATTN_FWD_BLOCKS, S // QB))
    w_out_full = w_out_g.reshape(D, D)
    w_down4 = w_down_g.reshape(D_FF_SHARDS, d_ff_block, D)
    mix, x2, h2, h2_t = _fwd_outproj(pool_out, attn_out, pool_scale, attn_scale, w_out_full, xs, g2, g3, big)
    upre, gate_val, dy, df, loss_cols, dg4 = _fwd_ffn_loss(h2, w_up_g, conv_w_g, conv_b_g, w_down4, x2, target, g4, small)
    loss = lax.psum(0.5 * jnp.sum(loss_cols) / D, ("x", "y", "c"))

    dupre_g, dupre_v, d_wd4, d_wup, d_cb, d_cw = _bwd_ffn_blocks(gate_val, upre, conv_w_g, w_down4, df, h2_t, min(1024, S))
    dx2, dmix, dg3, dg2 = _bwd_ffn_tokens(dupre_g, dupre_v, w_up_g, x2, dy, mix, g2, g3, big)
    d_pool, d_attn, d_wout, d_ps, d_as = _bwd_outproj(dmix, w_out_full, pool_out, attn_out, pool_scale, attn_scale, big)
    d_wdown_g = d_wd4.reshape(N_DEV, w_down.shape[1], D).astype(BF16)
    d_wout_g = d_wout.reshape(N_DEV, D // N_DEV, D).astype(BF16)
    dq, dk, dv, late_parts = _bwd_attn(qkv, d_attn, n_pairs, _Exchange([d_wout_g, d_wup.astype(BF16), d_wdown_g, d_cw], gather=False),
                                       min(ATTN_BWD_BLOCKS, S // QB))
    du, d_wp = _bwd_pool(u, d_pool, w_pool[0], big)
    dproj, d_win = _bwd_w_in(du, dq, dk, dv, h1_t, N_DEV, big)
    w_in_t = w_in_g.transpose(0, 2, 1).reshape(-1, D)
    dx, dg1, (win_parts,) = _bwd_x(dproj, w_in_t, xs, dx2, g1, big, _Exchange([d_win.astype(BF16)], gather=False))
    big_parts = [win_parts] + list(late_parts)
    r = dict(dx=dx, g1=dg1, w_pool=d_wp, pool_scale=d_ps, attn_scale=d_as, g2=dg2, g3=dg3, conv_b=d_cb, g4=dg4)

    small_names = ["norm_mix_pre", "w_pool", "pool_scale", "attn_scale", "norm_mix_post", "norm_ffn_pre", "conv_b", "norm_ffn_post"]
    small_w = dict(norm_mix_pre=norm_mix_pre, w_pool=w_pool, pool_scale=pool_scale, attn_scale=attn_scale,
                   norm_mix_post=norm_mix_post, norm_ffn_pre=norm_ffn_pre, conv_b=conv_b, norm_ffn_post=norm_ffn_post)
    small_m = dict(norm_mix_pre=m_norm_mix_pre, w_pool=m_w_pool, pool_scale=m_pool_scale, attn_scale=m_attn_scale,
                   norm_mix_post=m_norm_mix_post, norm_ffn_pre=m_norm_ffn_pre, conv_b=m_conv_b, norm_ffn_post=m_norm_ffn_post)
    small_v = dict(norm_mix_pre=v_norm_mix_pre, w_pool=v_w_pool, pool_scale=v_pool_scale, attn_scale=v_attn_scale,
                   norm_mix_post=v_norm_mix_post, norm_ffn_pre=v_norm_ffn_pre, conv_b=v_conv_b, norm_ffn_post=v_norm_ffn_post)
    small_g = dict(norm_mix_pre=r["g1"], w_pool=r["w_pool"], pool_scale=r["pool_scale"], attn_scale=r["attn_scale"],
                   norm_mix_post=r["g2"], norm_ffn_pre=r["g3"], conv_b=r["conv_b"], norm_ffn_post=r["g4"])
    like = [small_w[n] for n in small_names]
    packed_g = _pack_small([small_g[n] for n in small_names])

    (small_parts,) = _all_to_all([packed_g], gather=True, name="gather_small_grads")

    def update(parts, w, m, v):
        R, C = w.shape
        return _reduce_adamw(parts, w, m, v, _row_tile(R, C))

    res = {}
    res["w_in"] = update(big_parts[0], w_in[0], m_w_in[0], v_w_in[0])
    res["w_out"] = update(big_parts[1], w_out[0], m_w_out[0], v_w_out[0])
    res["w_up"] = update(big_parts[2], w_up[0], m_w_up[0], v_w_up[0])
    res["w_down"] = update(big_parts[3], w_down[0], m_w_down[0], v_w_down[0])
    res["conv_w"] = update(big_parts[4], conv_w[0], m_conv_w[0], v_conv_w[0])
    small_res = update(small_parts, _pack_small(like), _pack_small([small_m[n] for n in small_names]),
                       _pack_small([small_v[n] for n in small_names]))
    small_res = [_unpack_small(t, like) for t in small_res]
    for idx, n in enumerate(small_names):
        res[n] = tuple(t[idx] for t in small_res)

    order = ["norm_mix_pre", "w_in", "w_pool", "pool_scale", "attn_scale", "w_out", "norm_mix_post", "norm_ffn_pre",
             "w_up", "conv_w", "conv_b", "w_down", "norm_ffn_post"]
    shaped = {n: tuple(t.reshape(s.shape) for t in res[n])
              for n, s in dict(norm_mix_pre=norm_mix_pre, w_in=w_in, w_pool=w_pool, pool_scale=pool_scale, attn_scale=attn_scale,
                               w_out=w_out, norm_mix_post=norm_mix_post, norm_ffn_pre=norm_ffn_pre, w_up=w_up, conv_w=conv_w,
                               conv_b=conv_b, w_down=w_down, norm_ffn_post=norm_ffn_post).items()}
    outs = [loss, r["dx"].reshape(x.shape)]
    for k in range(4):
        outs += [shaped[n][k] for n in order]
    return tuple(outs)
```

```python
import functools

import jax
import jax.numpy as jnp
from jax import lax
from jax.experimental import pallas as pl
from jax.experimental.pallas import tpu as pltpu

F32 = jnp.float32
BF16 = jnp.bfloat16
HIGHEST = lax.Precision.HIGHEST

N_DEV = 8
EPS = 1e-6
POOL_WINDOWS = (2, 4, 8, 16)
POOL_GROUP = 128
HALO = 16
HEAD_DIM = 64
QB = 128
ATTN_SCALE = HEAD_DIM ** -0.5
ATTN_FWD_BLOCKS = 16
ATTN_BWD_BLOCKS = 8
EXP_UNDERFLOW = -88.0
D_FF_SHARDS = 4

ADAM_LR = 0.001
ADAM_B1 = 0.9
ADAM_B2 = 0.999
ADAM_EPS = 1e-08
ADAM_WD = 0.01
ADAM_STEP = 10

VMEM_LIMIT_V7X = 56 * 1024 * 1024
MESH = pl.DeviceIdType.MESH


def _params(*semantics):
    return pltpu.CompilerParams(dimension_semantics=semantics, vmem_limit_bytes=VMEM_LIMIT_V7X)


def _const(shape):
    zeros = (0,) * len(shape)
    return pl.BlockSpec(shape, lambda *_: zeros, pipeline_mode=pl.Buffered(1))


def _dot(a, b):
    return jnp.dot(a, b, preferred_element_type=F32)


def _dot_nt(a, b):
    return lax.dot_general(a, b, (((1,), (1,)), ((), ())), preferred_element_type=F32)


def _dot_tn(a, b):
    return lax.dot_general(a, b, (((0,), (0,)), ((), ())), preferred_element_type=F32)


def _rms(v):
    return lax.rsqrt(jnp.mean(v * v, axis=-1, keepdims=True) + EPS)


def _norm_bwd(dn_times_gain, n, r):
    return r * (dn_times_gain - n * jnp.mean(dn_times_gain * n, axis=-1, keepdims=True))


def _zero_when(first, *refs):
    @pl.when(first)
    def _():
        for ref in refs:
            ref[...] = jnp.zeros_like(ref)


def _colsum(v):
    return jnp.sum(v, axis=0, keepdims=True)


def _grid_ends(grid):
    ids = [pl.program_id(a) for a in range(len(grid))]
    first = functools.reduce(jnp.logical_and, [i == 0 for i in ids])
    last = functools.reduce(jnp.logical_and, [i == n - 1 for i, n in zip(ids, grid)])
    return first, last


def _fwd_inproj(x, g1, w_in_g, tile):
    S, D = x.shape
    nb, _, cs = w_in_g.shape
    d_pool = 2 * cs

    def body(x_ref, g_ref, w_ref, ht_ref, u_ref, qkv_ref):
        xf = x_ref[...]
        h = (xf * _rms(xf) * g_ref[...]).astype(BF16)
        ht_ref[...] = h.T
        for d in range(nb):
            o = _dot(h, w_ref[d])
            if d < 2:
                u_ref[:, d * cs:(d + 1) * cs] = o
            else:
                qkv_ref[:, (d - 2) * cs:(d - 1) * cs] = o.astype(BF16)

    return pl.pallas_call(
        body, name="fwd_inproj", grid=(S // tile,),
        in_specs=[pl.BlockSpec((tile, D), lambda i: (i, 0)), _const((1, D)), _const(w_in_g.shape)],
        out_specs=[pl.BlockSpec((D, tile), lambda i: (0, i)), pl.BlockSpec((tile, d_pool), lambda i: (i, 0)),
                   pl.BlockSpec((tile, 3 * d_pool), lambda i: (i, 0))],
        out_shape=[jax.ShapeDtypeStruct((D, S), BF16), jax.ShapeDtypeStruct((S, d_pool), F32),
                   jax.ShapeDtypeStruct((S, 3 * d_pool), BF16)],
        compiler_params=_params("parallel"),
    )(x, g1, w_in_g)


def _window_sums(ext, forward):
    n = ext.shape[0]
    sums, s, sh = {}, ext, 1
    while sh < POOL_WINDOWS[-1]:
        s = s + pltpu.roll(s, (n - sh) if forward else sh, axis=0)
        sh *= 2
        sums[sh] = s
    return sums


def _pool_counts(t0, rows):
    t1 = (lax.broadcasted_iota(jnp.int32, (rows, 1), 0) + t0 + 1).astype(F32)
    return [jnp.minimum(t1, float(w)) for w in POOL_WINDOWS]


def _pool_deviation(u, halo, t0):
    T = u.shape[0]
    sums = _window_sums(jnp.concatenate([halo, u], axis=0), forward=False)
    counts = _pool_counts(t0, T)
    parts = []
    for g, w in enumerate(POOL_WINDOWS):
        lanes = slice(g * POOL_GROUP, (g + 1) * POOL_GROUP)
        parts.append(sums[w][HALO:, lanes] / counts[g] - u[:, lanes])
    return parts


def _prev_halo_spec(tile, width):
    return pl.BlockSpec((HALO, width), lambda i: (jnp.maximum(i * (tile // HALO) - 1, 0), 0))


def _next_halo_spec(tile, width, n_tiles):
    last = n_tiles * (tile // HALO) - 1
    return pl.BlockSpec((HALO, width), lambda i: (jnp.minimum((i + 1) * (tile // HALO), last), 0))


def _fwd_pool(u, w_pool, tile):
    S, C = u.shape

    def body(u_ref, halo_ref, wp_ref, o_ref):
        i = pl.program_id(0)
        halo = jnp.where(i > 0, halo_ref[...], 0.0)
        parts = _pool_deviation(u_ref[...], halo, i * tile)
        for g, p in enumerate(parts):
            o_ref[:, g * POOL_GROUP:(g + 1) * POOL_GROUP] = _dot(p.astype(BF16), wp_ref[g].astype(BF16))

    return pl.pallas_call(
        body, name="fwd_pool", grid=(S // tile,),
        in_specs=[pl.BlockSpec((tile, C), lambda i: (i, 0)), _prev_halo_spec(tile, C), _const(w_pool.shape)],
        out_specs=pl.BlockSpec((tile, C), lambda i: (i, 0)),
        out_shape=jax.ShapeDtypeStruct((S, C), F32),
        compiler_params=_params("parallel"),
    )(u, u, w_pool)


def _low_lanes():
    return lax.broadcasted_iota(jnp.int32, (QB, 2 * HEAD_DIM), 1) < HEAD_DIM


LOG_PIECES = 2
GRAD_PIECES = 3


def _triangle(inclusive, pieces):
    row = lax.broadcasted_iota(jnp.int32, (pieces * QB, QB), 0) % QB
    col = lax.broadcasted_iota(jnp.int32, (pieces * QB, QB), 1)
    return ((row >= col) if inclusive else (row > col)).astype(BF16)


def _pieces(v, n):
    out, rest = [], v
    for _ in range(n - 1):
        piece = rest.astype(BF16)
        out.append(piece)
        rest = rest - piece.astype(F32)
    out.append(rest.astype(BF16))
    return jnp.concatenate(out, axis=1)


def _causal_mask(width, offset):
    row = lax.broadcasted_iota(jnp.int32, (QB, width), 0)
    col = lax.broadcasted_iota(jnp.int32, (QB, width), 1)
    return col < row + offset


def _row_sums(vals, carry):
    for b in reversed(range(vals.shape[1] // QB)):
        carry = carry + jnp.sum(vals[:, b * QB:(b + 1) * QB], axis=1, keepdims=True)
    return carry


def _suffix_sums(vals, tri, carry):
    n = vals.shape[1] // QB
    out, run = [None] * n, carry
    for b in reversed(range(n)):
        blk = vals[:, b * QB:(b + 1) * QB]
        out[b] = _dot(_pieces(blk, tri.shape[0] // QB), tri) + run
        run = run + jnp.sum(blk, axis=1, keepdims=True)
    return (out[0] if n == 1 else jnp.concatenate(out, axis=1)), run


def _attn_tiles(qhs, kws, masks, carries, after_s):
    return _attn_weights(_attn_scores(qhs, kws, masks), masks, carries, after_s)


def _attn_scores(qhs, kws, masks):
    zs = [_dot_nt(qh, kw) * ATTN_SCALE for qh, kw in zip(qhs, kws)]
    es = [jnp.exp(-jnp.abs(z)) for z in zs]
    softplus = [jnp.maximum(z, 0.0) + jnp.log(1.0 + e) for z, e in zip(zs, es)]
    log_1m_beta = [-sp if m is None else jnp.where(m, -sp, 0.0) for sp, m in zip(softplus, masks)]
    return list(zip(zs, es, softplus, log_1m_beta))


def _attn_weights(scores, masks, carries, after_s):
    sums = [_suffix_sums(l, after_s, c) for (_, _, _, l), c in zip(scores, carries)]
    weights = [jnp.exp(z - sp + st) for (z, _, sp, _), (st, _) in zip(scores, sums)]
    weights = [a if m is None else jnp.where(m, a, 0.0) for a, m in zip(weights, masks)]
    return [(z, e, a, c) for (z, e, _, _), a, (_, c) in zip(scores, weights, sums)]


def _split_heads(v, low_lanes):
    return jnp.where(low_lanes, v, 0.0).astype(BF16), jnp.where(low_lanes, 0.0, v).astype(BF16)


def _sweep_done(c0, c1):
    return (jnp.maximum(jnp.max(c0), jnp.max(c1)) < EXP_UNDERFLOW).astype(jnp.int32)


def _all_done(carries):
    return jnp.max(functools.reduce(jnp.maximum, carries)) < EXP_UNDERFLOW


def _first_window(i):
    first_blk = jnp.maximum(i - 1, 0)
    return first_blk, pl.multiple_of(first_blk * QB, QB), (i - first_blk) * QB


def _fwd_attn(qkv, n_pairs, ex, subs):
    S = qkv.shape[0]
    n_steps = S // (subs * QB)

    def body(q_ref, k_ref, v_ref, *rest):
        o_ref = rest[ex.n]
        ex_refs = ex.split(rest[:ex.n] + rest[ex.n + 1:])
        first_step, last_step = _grid_ends((n_pairs, n_steps))

        @pl.when(first_step)
        def _():
            ex.start(*ex_refs)

        low_lanes = _low_lanes()
        after_s = _triangle(False, LOG_PIECES)
        zero = jnp.zeros((QB, 1), F32)

        def cond(c):
            return jnp.logical_and(c[0] >= 0, c[1] == 0)

        qhs, kws, vws, masks, first_blks = [], [], [], [], []
        for sub in range(subs):
            i = pl.program_id(1) * subs + sub
            first_blk, start, offset = _first_window(i)
            first_blks.append(first_blk)
            qhs += _split_heads(q_ref[sub * QB:(sub + 1) * QB, :].astype(F32), low_lanes)
            kws += [k_ref[pl.ds(start, 2 * QB), :]] * 2
            vws += [v_ref[pl.ds(start, 2 * QB), :]] * 2
            masks += [_causal_mask(2 * QB, offset)] * 2
        tiles = _attn_tiles(qhs, kws, masks, [zero] * len(qhs), after_s)
        outs = [_dot(t[2].astype(BF16), vw) for t, vw in zip(tiles, vws)]

        first_out = [jnp.where(low_lanes, outs[2 * sub], outs[2 * sub + 1]) for sub in range(subs)]

        def sweep_on():
            final = []
            for sub in range(subs):
                def step(c, qh=qhs[2 * sub:2 * sub + 2]):
                    j, _, acc, c0, c1 = c
                    at = pl.multiple_of(j * QB, QB)
                    kb = k_ref[pl.ds(at, QB), :]
                    vb = v_ref[pl.ds(at, QB), :]
                    far = _attn_tiles(qh, [kb, kb], [None, None], [c0, c1], after_s)
                    acc = acc + jnp.where(low_lanes, _dot(far[0][2].astype(BF16), vb), _dot(far[1][2].astype(BF16), vb))
                    return j - 1, _sweep_done(far[0][3], far[1][3]), acc, far[0][3], far[1][3]

                c0, c1 = tiles[2 * sub][3], tiles[2 * sub + 1][3]
                final.append(lax.while_loop(cond, step, (first_blks[sub] - 1, _sweep_done(c0, c1), first_out[sub], c0, c1))[2])
            return tuple(final)

        final = lax.cond(_all_done([t[3] for t in tiles]), lambda: tuple(first_out), sweep_on)
        for sub in range(subs):
            o_ref[sub * QB:(sub + 1) * QB, :] = final[sub]

        @pl.when(last_step)
        def _():
            ex.wait(*ex_refs)

    outs = pl.pallas_call(
        body, name="fwd_attn", grid=(n_pairs, n_steps),
        in_specs=[pl.BlockSpec((subs * QB, QB), lambda p, i: (i, p)),
                  pl.BlockSpec((S, QB), lambda p, i: (0, n_pairs + p), pipeline_mode=pl.Buffered(1)),
                  pl.BlockSpec((S, QB), lambda p, i: (0, 2 * n_pairs + p), pipeline_mode=pl.Buffered(1))] + ex.specs,
        out_specs=[pl.BlockSpec((subs * QB, QB), lambda p, i: (i, p))] + ex.specs,
        out_shape=[jax.ShapeDtypeStruct((S, n_pairs * QB), F32)] + ex.out_shape,
        scratch_shapes=ex.scratch,
        compiler_params=_params("arbitrary", "arbitrary"),
    )(qkv, qkv, qkv, *ex.arrays)
    return outs[0], outs[1:]


def _normalized_heads(pool_out, attn_out):
    rp, ra = _rms(pool_out), _rms(attn_out)
    return pool_out * rp, rp, attn_out * ra, ra


def _fwd_outproj(pool_out, attn_out, pool_scale, attn_scale, w_out, x, g2, g3, tile):
    S, D = x.shape
    C = pool_out.shape[1]

    def body(p_ref, a_ref, ps_ref, as_ref, w_ref, x_ref, g2_ref, g3_ref, mix_ref, x2_ref, h2_ref, h2t_ref):
        n_p, _, n_a, _ = _normalized_heads(p_ref[...], a_ref[...])
        mix = _dot((n_p * ps_ref[...]).astype(BF16), w_ref[:C, :]) + _dot((n_a * as_ref[...]).astype(BF16), w_ref[C:, :])
        mix_ref[...] = mix
        x2 = x_ref[...] + mix * _rms(mix) * g2_ref[...]
        x2_ref[...] = x2
        h2 = (x2 * _rms(x2) * g3_ref[...]).astype(BF16)
        h2_ref[...] = h2
        h2t_ref[...] = h2.T

    row = lambda w: pl.BlockSpec((tile, w), lambda i: (i, 0))
    return pl.pallas_call(
        body, name="fwd_outproj", grid=(S // tile,),
        in_specs=[row(C), row(C), _const((1, C)), _const((1, C)), _const(w_out.shape), row(D), _const((1, D)), _const((1, D))],
        out_specs=[row(D), row(D), row(D), pl.BlockSpec((D, tile), lambda i: (0, i))],
        out_shape=[jax.ShapeDtypeStruct((S, D), F32), jax.ShapeDtypeStruct((S, D), F32), jax.ShapeDtypeStruct((S, D), BF16),
                   jax.ShapeDtypeStruct((D, S), BF16)],
        compiler_params=_params("parallel"),
    )(pool_out, attn_out, pool_scale, attn_scale, w_out, x, g2, g3)


def _conv_taps(tile_rows, halo_rows):
    T = tile_rows.shape[0]
    ext = jnp.concatenate([halo_rows.astype(F32), tile_rows.astype(F32)], axis=0)
    return pltpu.roll(ext, 2, axis=0)[HALO:], pltpu.roll(ext, 1, axis=0)[HALO:], ext[HALO:]


def _tap_rows(cw_ref, d):
    return [cw_ref[d, k:k + 1, :] for k in range(3)]


def _gated_unit(taps_gate, taps_val, cw_gate, cw_val, cb_gate, cb_val):
    gate = cw_gate[0] * taps_gate[0] + cw_gate[1] * taps_gate[1] + cw_gate[2] * taps_gate[2] + cb_gate
    val = cw_val[0] * taps_val[0] + cw_val[1] * taps_val[1] + cw_val[2] * taps_val[2] + cb_val
    sig = 1.0 / (1.0 + jnp.exp(-gate))
    return gate, val, sig


def _fwd_ffn_loss(h2, w_up_g, conv_w_g, conv_b_g, w_down4, x2, target, g4, tile):
    S, D = x2.shape
    nb, _, cs = w_up_g.shape
    half = D_FF_SHARDS

    def body(h_ref, w_ref, cw_ref, cb_ref, wd_ref, x2_ref, t_ref, g4_ref, upre_ref, gv_ref, dy_ref, df_ref, loss_ref, dg4_ref, halo_ref):
        _zero_when(pl.program_id(0) == 0, loss_ref, dg4_ref, halo_ref)
        h = h_ref[...]

        def up(s):
            return _dot(h, w_ref[s]), _dot(h, w_ref[s + half])

        f = jnp.zeros((tile, D), F32)
        ahead = up(0)
        for s in range(half):
            ug, uv = ahead
            if s + 1 < half:
                ahead = up(s + 1)
            upre_ref[s] = ug.astype(BF16)
            upre_ref[s + half] = uv.astype(BF16)
            gate, val, sig = _gated_unit(_conv_taps(ug, halo_ref[s]), _conv_taps(uv, halo_ref[s + half]),
                                         _tap_rows(cw_ref, s), _tap_rows(cw_ref, s + half), cb_ref[s], cb_ref[s + half])
            halo_ref[s] = ug[tile - HALO:, :]
            halo_ref[s + half] = uv[tile - HALO:, :]
            gv_ref[s] = gate.astype(BF16)
            gv_ref[s + half] = val.astype(BF16)
            f = f + _dot((gate * sig * val).astype(BF16), wd_ref[s])
        r4 = _rms(f)
        n4 = f * r4
        err = x2_ref[...] + n4 * g4_ref[...] - t_ref[...]
        dy = err * (1.0 / D)
        dy_ref[...] = dy
        df_ref[...] = _norm_bwd(dy * g4_ref[...], n4, r4).astype(BF16)
        loss_ref[...] += _colsum(err * err)
        dg4_ref[...] += _colsum(dy * n4)

    row = lambda w: pl.BlockSpec((tile, w), lambda i: (i, 0))
    return pl.pallas_call(
        body, name="fwd_ffn_loss", grid=(S // tile,),
        in_specs=[row(D), _const(w_up_g.shape), _const(conv_w_g.shape), _const(conv_b_g.shape), _const(w_down4.shape),
                  row(D), row(D), _const((1, D))],
        out_specs=[pl.BlockSpec((nb, tile, cs), lambda i: (0, i, 0)), pl.BlockSpec((nb, tile, cs), lambda i: (0, i, 0)), row(D), row(D),
                   pl.BlockSpec((1, D), lambda i: (0, 0)), pl.BlockSpec((1, D), lambda i: (0, 0))],
        out_shape=[jax.ShapeDtypeStruct((nb, S, cs), BF16), jax.ShapeDtypeStruct((nb, S, cs), BF16),
                   jax.ShapeDtypeStruct((S, D), F32), jax.ShapeDtypeStruct((S, D), BF16),
                   jax.ShapeDtypeStruct((1, D), F32), jax.ShapeDtypeStruct((1, D), F32)],
        scratch_shapes=[pltpu.VMEM((nb, HALO, cs), F32)],
        compiler_params=_params("arbitrary"),
    )(h2, w_up_g, conv_w_g, conv_b_g, w_down4, x2, target, g4)


def _bwd_down(upre, conv_w_g, conv_b_g, w_down4, df, tile):
    nb, S, cs = upre.shape
    D = df.shape[1]
    n_tiles = S // tile

    def body(ug_ref, uv_ref, hg_ref, hv_ref, cwg_ref, cwv_ref, cbg_ref, cbv_ref, wd_ref, df_ref,
             dg_ref, dv_ref, dwd_ref, dbg_ref, dbv_ref, dcwg_ref, dcwv_ref):
        i = pl.program_id(1)
        first = i == 0
        _zero_when(first, dwd_ref, dbg_ref, dbv_ref, dcwg_ref, dcwv_ref)
        halo_g = jnp.where(first, jnp.zeros_like(hg_ref[0]), hg_ref[0])
        halo_v = jnp.where(first, jnp.zeros_like(hv_ref[0]), hv_ref[0])
        taps_g, taps_v = _conv_taps(ug_ref[0], halo_g), _conv_taps(uv_ref[0], halo_v)
        gate, val, sig = _gated_unit(taps_g, taps_v, _tap_rows(cwg_ref, 0), _tap_rows(cwv_ref, 0), cbg_ref[0], cbv_ref[0])
        silu = gate * sig
        dfb = df_ref[...]
        dact = _dot_nt(dfb, wd_ref[0])
        dwd_ref[0] += _dot_tn((silu * val).astype(BF16), dfb)
        dgate = dact * val * (sig * (1.0 + gate * (1.0 - sig)))
        dval = dact * silu
        dg_ref[0] = dgate.astype(BF16)
        dv_ref[0] = dval.astype(BF16)
        dbg_ref[0] += _colsum(dgate)
        dbv_ref[0] += _colsum(dval)
        for k in range(3):
            dcwg_ref[0, k:k + 1, :] += _colsum(dgate * taps_g[k])
            dcwv_ref[0, k:k + 1, :] += _colsum(dval * taps_v[k])

    half = D_FF_SHARDS
    blk = lambda off: pl.BlockSpec((1, tile, cs), lambda s, i: (s + off, i, 0))
    halo = lambda off: pl.BlockSpec((1, HALO, cs), lambda s, i: (s + off, jnp.maximum(i * (tile // HALO) - 1, 0), 0))
    par = lambda off, r: pl.BlockSpec((1, r, cs), lambda s, i: (s + off, 0, 0))
    outs = pl.pallas_call(
        body, name="bwd_down", grid=(half, n_tiles),
        in_specs=[blk(0), blk(half), halo(0), halo(half), par(0, 3), par(half, 3), par(0, 1), par(half, 1),
                  pl.BlockSpec((1, cs, D), lambda s, i: (s, 0, 0)), pl.BlockSpec((tile, D), lambda s, i: (i, 0))],
        out_specs=[blk(0), blk(0), pl.BlockSpec((1, cs, D), lambda s, i: (s, 0, 0)),
                   par(0, 1), par(0, 1), par(0, 3), par(0, 3)],
        out_shape=[jax.ShapeDtypeStruct((half, S, cs), BF16), jax.ShapeDtypeStruct((half, S, cs), BF16),
                   jax.ShapeDtypeStruct((half, cs, D), F32),
                   jax.ShapeDtypeStruct((half, 1, cs), F32), jax.ShapeDtypeStruct((half, 1, cs), F32),
                   jax.ShapeDtypeStruct((half, 3, cs), F32), jax.ShapeDtypeStruct((half, 3, cs), F32)],
        compiler_params=_params("parallel", "arbitrary"),
    )(upre, upre, upre, upre, conv_w_g, conv_w_g, conv_b_g, conv_b_g, w_down4, df)
    dgate, dval, d_wd, dbg, dbv, dcwg, dcwv = outs
    return dgate, dval, d_wd, jnp.concatenate([dbg, dbv], axis=0), jnp.concatenate([dcwg, dcwv], axis=0)


def _bwd_up_x(dgate, dval, conv_w_g, w_up_g, x2, dy, mix, g2, g3, tile):
    half, S, cs = dgate.shape
    nb = 2 * half
    D = x2.shape[1]
    n_tiles = S // tile

    def body(dg_ref, dv_ref, hg_ref, hv_ref, cw_ref, w_ref, x2_ref, dy_ref, mix_ref, g2_ref, g3_ref,
             dupre_ref, dx2_ref, dmix_ref, dg3_ref, dg2_ref):
        i = pl.program_id(0)
        last = i == n_tiles - 1
        _zero_when(i == 0, dg3_ref, dg2_ref)
        dh2 = jnp.zeros((tile, D), F32)
        for d in range(nb):
            src, halo = (dg_ref, hg_ref) if d < half else (dv_ref, hv_ref)
            nxt = jnp.where(last, jnp.zeros_like(halo[d % half]), halo[d % half])
            ext = jnp.concatenate([src[d % half].astype(F32), nxt.astype(F32)], axis=0)
            n = ext.shape[0]
            cw = _tap_rows(cw_ref, d)
            dupre = (cw[2] * ext + cw[1] * pltpu.roll(ext, n - 1, axis=0) + cw[0] * pltpu.roll(ext, n - 2, axis=0))[:tile]
            dupre = dupre.astype(BF16)
            dupre_ref[d] = dupre
            dh2 = dh2 + _dot_nt(dupre, w_ref[d])
        x2 = x2_ref[...]
        r3 = _rms(x2)
        n3 = x2 * r3
        dg3_ref[...] += _colsum(dh2 * n3)
        dx2 = dy_ref[...] + _norm_bwd(dh2 * g3_ref[...], n3, r3)
        dx2_ref[...] = dx2
        mix = mix_ref[...]
        r2 = _rms(mix)
        n2 = mix * r2
        dg2_ref[...] += _colsum(dx2 * n2)
        dmix_ref[...] = _norm_bwd(dx2 * g2_ref[...], n2, r2).astype(BF16)

    row = lambda w: pl.BlockSpec((tile, w), lambda i: (i, 0))
    blk = pl.BlockSpec((half, tile, cs), lambda i: (0, i, 0))
    last_halo = n_tiles * (tile // HALO) - 1
    halo = pl.BlockSpec((half, HALO, cs), lambda i: (0, jnp.minimum((i + 1) * (tile // HALO), last_halo), 0))
    acc = pl.BlockSpec((1, D), lambda i: (0, 0))
    return pl.pallas_call(
        body, name="bwd_up_x", grid=(n_tiles,),
        in_specs=[blk, blk, halo, halo, _const(conv_w_g.shape), _const(w_up_g.shape), row(D), row(D), row(D),
                  _const((1, D)), _const((1, D))],
        out_specs=[pl.BlockSpec((nb, tile, cs), lambda i: (0, i, 0)), row(D), row(D), acc, acc],
        out_shape=[jax.ShapeDtypeStruct((nb, S, cs), BF16), jax.ShapeDtypeStruct((S, D), F32),
                   jax.ShapeDtypeStruct((S, D), BF16), jax.ShapeDtypeStruct((1, D), F32), jax.ShapeDtypeStruct((1, D), F32)],
        compiler_params=_params("arbitrary"),
    )(dgate, dval, dgate, dval, conv_w_g, w_up_g, x2, dy, mix, g2, g3)


def _bwd_weight(act_t, dout, tile):
    D, S = act_t.shape
    nb, _, cs = dout.shape

    def body(a_ref, d_ref, o_ref):
        _zero_when(pl.program_id(1) == 0, o_ref)
        o_ref[0] += _dot(a_ref[...], d_ref[0])

    return pl.pallas_call(
        body, name="bwd_w_up", grid=(nb, S // tile),
        in_specs=[pl.BlockSpec((D, tile), lambda d, i: (0, i)), pl.BlockSpec((1, tile, cs), lambda d, i: (d, i, 0))],
        out_specs=pl.BlockSpec((1, D, cs), lambda d, i: (d, 0, 0)),
        out_shape=jax.ShapeDtypeStruct((nb, D, cs), F32),
        compiler_params=_params("parallel", "arbitrary"),
    )(act_t, dout)


def _bwd_ffn_blocks(gate_val, upre, conv_w_g, w_down4, df, h2_t, tile):
    nb, S, cs = upre.shape
    D = df.shape[1]
    n_tiles = S // tile
    half = D_FF_SHARDS

    def body(g_ref, v_ref, ug_ref, uv_ref, cwg_ref, cwv_ref, wd_ref, df_ref, ht_ref,
             dug_ref, duv_ref, dwd_ref, dwg_ref, dwv_ref, dbg_ref, dbv_ref, dcwg_ref, dcwv_ref, next_ref):
        _zero_when(pl.program_id(1) == 0, dwd_ref, dwg_ref, dwv_ref, dbg_ref, dbv_ref, dcwg_ref, dcwv_ref, next_ref)
        dfb = df_ref[...]
        dact = _dot_nt(dfb, wd_ref[0])
        gate, val = g_ref[0].astype(F32), v_ref[0].astype(F32)
        sig = 1.0 / (1.0 + jnp.exp(-gate))
        silu = gate * sig
        dwd_ref[0] += _dot_tn((silu * val).astype(BF16), dfb)
        ht = ht_ref[...]

        def through_conv(dup, slot, cw_ref, u_ref, du_ref, dw_ref, db_ref, dcw_ref):
            ext = jnp.concatenate([dup, next_ref[slot]], axis=0)
            n = ext.shape[0]
            shifted = (dup, pltpu.roll(ext, n - 1, axis=0)[:tile], pltpu.roll(ext, n - 2, axis=0)[:tile])
            next_ref[slot] = dup[:HALO]
            cw = _tap_rows(cw_ref, 0)
            dupre = (cw[2] * shifted[0] + cw[1] * shifted[1] + cw[0] * shifted[2]).astype(BF16)
            du_ref[0] = dupre
            dw_ref[0] += _dot(ht, dupre)
            u = u_ref[0].astype(F32)
            db_ref[0] += _colsum(dup)
            for k in range(3):
                dcw_ref[0, k:k + 1, :] += _colsum(shifted[2 - k] * u)

        through_conv(dact * val * (sig * (1.0 + gate * (1.0 - sig))), 0, cwg_ref, ug_ref, dug_ref, dwg_ref, dbg_ref, dcwg_ref)
        through_conv(dact * silu, 1, cwv_ref, uv_ref, duv_ref, dwv_ref, dbv_ref, dcwv_ref)

    rev = lambda i: n_tiles - 1 - i
    blk = lambda off: pl.BlockSpec((1, tile, cs), lambda s, i: (s + off, rev(i), 0))
    par = lambda off, r: pl.BlockSpec((1, r, cs), lambda s, i: (s + off, 0, 0))
    acc = lambda r, c: pl.BlockSpec((1, r, c), lambda s, i: (s, 0, 0), pipeline_mode=pl.Buffered(1))
    outs = pl.pallas_call(
        body, name="bwd_ffn_blocks", grid=(half, n_tiles),
        in_specs=[blk(0), blk(half), blk(0), blk(half), par(0, 3), par(half, 3),
                  acc(cs, D), pl.BlockSpec((tile, D), lambda s, i: (rev(i), 0)), pl.BlockSpec((D, tile), lambda s, i: (0, rev(i)))],
        out_specs=[blk(0), blk(0), acc(cs, D), acc(D, cs), acc(D, cs), acc(1, cs), acc(1, cs), acc(3, cs), acc(3, cs)],
        out_shape=[jax.ShapeDtypeStruct((half, S, cs), BF16), jax.ShapeDtypeStruct((half, S, cs), BF16),
                   jax.ShapeDtypeStruct((half, cs, D), F32),
                   jax.ShapeDtypeStruct((half, D, cs), F32), jax.ShapeDtypeStruct((half, D, cs), F32),
                   jax.ShapeDtypeStruct((half, 1, cs), F32), jax.ShapeDtypeStruct((half, 1, cs), F32),
                   jax.ShapeDtypeStruct((half, 3, cs), F32), jax.ShapeDtypeStruct((half, 3, cs), F32)],
        scratch_shapes=[pltpu.VMEM((2, HALO, cs), F32)],
        compiler_params=_params("arbitrary", "arbitrary"),
    )(gate_val, gate_val, upre, upre, conv_w_g, conv_w_g, w_down4, df, h2_t)
    dupre_g, dupre_v, d_wd, d_wg, d_wv, dbg, dbv, dcwg, dcwv = outs
    return (dupre_g, dupre_v, d_wd, jnp.concatenate([d_wg, d_wv], axis=0), jnp.concatenate([dbg, dbv], axis=0),
            jnp.concatenate([dcwg, dcwv], axis=0))


def _bwd_ffn_tokens(dupre_g, dupre_v, w_up_g, x2, dy, mix, g2, g3, tile):
    half, S, cs = dupre_g.shape
    D = x2.shape[1]

    def body(dg_ref, dv_ref, w_ref, x2_ref, dy_ref, mix_ref, g2_ref, g3_ref, dx2_ref, dmix_ref, dg3_ref, dg2_ref):
        _zero_when(pl.program_id(0) == 0, dg3_ref, dg2_ref)
        parts = [_dot_nt(dg_ref[d], w_ref[d]) for d in range(half)] + [_dot_nt(dv_ref[d], w_ref[d + half]) for d in range(half)]
        while len(parts) > 1:
            parts = [a + b for a, b in zip(parts[::2], parts[1::2])]
        dh2 = parts[0]
        x2 = x2_ref[...]
        r3 = _rms(x2)
        n3 = x2 * r3
        dg3_ref[...] += _colsum(dh2 * n3)
        dx2 = dy_ref[...] + _norm_bwd(dh2 * g3_ref[...], n3, r3)
        dx2_ref[...] = dx2
        mix = mix_ref[...]
        r2 = _rms(mix)
        n2 = mix * r2
        dg2_ref[...] += _colsum(dx2 * n2)
        dmix_ref[...] = _norm_bwd(dx2 * g2_ref[...], n2, r2).astype(BF16)

    row = lambda w: pl.BlockSpec((tile, w), lambda i: (i, 0))
    blk = pl.BlockSpec((half, tile, cs), lambda i: (0, i, 0))
    acc = pl.BlockSpec((1, D), lambda i: (0, 0))
    return pl.pallas_call(
        body, name="bwd_ffn_tokens", grid=(S // tile,),
        in_specs=[blk, blk, _const(w_up_g.shape), row(D), row(D), row(D), _const((1, D)), _const((1, D))],
        out_specs=[row(D), row(D), acc, acc],
        out_shape=[jax.ShapeDtypeStruct((S, D), F32), jax.ShapeDtypeStruct((S, D), BF16),
                   jax.ShapeDtypeStruct((1, D), F32), jax.ShapeDtypeStruct((1, D), F32)],
        compiler_params=_params("arbitrary"),
    )(dupre_g, dupre_v, w_up_g, x2, dy, mix, g2, g3)


def _bwd_outproj(dmix, w_out, pool_out, attn_out, pool_scale, attn_scale, tile):
    S, D = dmix.shape
    C = pool_out.shape[1]

    def body(dm_ref, w_ref, p_ref, a_ref, ps_ref, as_ref, dp_ref, da_ref, dw_ref, dps_ref, das_ref):
        _zero_when(pl.program_id(0) == 0, dw_ref, dps_ref, das_ref)
        dmx = dm_ref[...]
        dmerged = _dot_nt(dmx, w_ref[...])
        n_p, r_p, n_a, r_a = _normalized_heads(p_ref[...], a_ref[...])
        merged = jnp.concatenate([(n_p * ps_ref[...]).astype(BF16), (n_a * as_ref[...]).astype(BF16)], axis=1)
        dw_ref[...] += _dot_tn(merged, dmx)
        dm_p, dm_a = dmerged[:, :C], dmerged[:, C:]
        dps_ref[...] += _colsum(dm_p * n_p)
        das_ref[...] += _colsum(dm_a * n_a)
        dp_ref[...] = _norm_bwd(dm_p * ps_ref[...], n_p, r_p)
        da_ref[...] = _norm_bwd(dm_a * as_ref[...], n_a, r_a)

    row = lambda w: pl.BlockSpec((tile, w), lambda i: (i, 0))
    return pl.pallas_call(
        body, name="bwd_outproj", grid=(S // tile,),
        in_specs=[row(D), _const(w_out.shape), row(C), row(C), _const((1, C)), _const((1, C))],
        out_specs=[row(C), row(C), pl.BlockSpec(w_out.shape, lambda i: (0, 0)),
                   pl.BlockSpec((1, C), lambda i: (0, 0)), pl.BlockSpec((1, C), lambda i: (0, 0))],
        out_shape=[jax.ShapeDtypeStruct((S, C), F32), jax.ShapeDtypeStruct((S, C), F32),
                   jax.ShapeDtypeStruct(w_out.shape, F32), jax.ShapeDtypeStruct((1, C), F32), jax.ShapeDtypeStruct((1, C), F32)],
        compiler_params=_params("arbitrary"),
    )(dmix, w_out, pool_out, attn_out, pool_scale, attn_scale)


def _bwd_attn(qkv, d_attn, n_pairs, ex, subs):
    S = qkv.shape[0]
    n_steps = S // (subs * QB)

    def body(q_ref, k_ref, v_ref, do_ref, *rest):
        dq_ref, dk_ref, dv_ref = rest[ex.n:ex.n + 3]
        ex_refs = ex.split(rest[:ex.n] + rest[ex.n + 3:])
        first_step, last_step = _grid_ends((n_pairs, n_steps))

        @pl.when(first_step)
        def _():
            ex.start(*ex_refs)

        @pl.when(pl.program_id(1) == 0)
        def _():
            dk_ref[...] = jnp.zeros_like(dk_ref)
            dv_ref[...] = jnp.zeros_like(dv_ref)

        low_lanes = _low_lanes()
        after_s, from_s = _triangle(False, LOG_PIECES), _triangle(True, GRAD_PIECES)
        zero = jnp.zeros((QB, 1), F32)

        def tiles(qhs, dohs, totals, kws, vws, masks, cs, gs, scores=None):
            fw = _attn_weights(scores or _attn_scores(qhs, kws, masks), masks, cs, after_s)
            gvals = [t[2] * _dot_nt(doh, vw) for t, doh, vw in zip(fw, dohs, vws)]
            sums = [_suffix_sums(g, from_s, g0) for g, g0 in zip(gvals, gs)]
            totals = [tot if m is None else tot + sm[1] for tot, m, sm in zip(totals, masks, sums)]
            dzs = []
            for (z, e, _, _), g, (nearer, _), tot, m in zip(fw, gvals, sums, totals, masks):
                inv = 1.0 / (1.0 + e)
                sig_abs, sig_neg = inv, e * inv
                pos = z >= 0.0
                dz = g * jnp.where(pos, sig_neg, sig_abs) - jnp.where(pos, sig_abs, sig_neg) * (tot - nearer)
                if m is not None:
                    dz = jnp.where(m, dz, 0.0)
                dzs.append((dz * ATTN_SCALE).astype(BF16))
            dqs = [_dot(dz, kw) for dz, kw in zip(dzs, kws)]
            dks = [_dot_tn(dz, qh) for dz, qh in zip(dzs, qhs)]
            dvs = [_dot_tn(t[2].astype(BF16), doh) for t, doh in zip(fw, dohs)]
            return [(dq, dk, dv, t[3], sm[1], tot) for dq, dk, dv, t, sm, tot in zip(dqs, dks, dvs, fw, sums, totals)]

        def cond(c):
            return jnp.logical_and(c[0] >= 0, c[1] == 0)

        qhs, dohs, kws, vws, masks, first_blks, starts = [], [], [], [], [], [], []
        for sub in range(subs):
            i = pl.program_id(1) * subs + sub
            rows = slice(sub * QB, (sub + 1) * QB)
            first_blk, start, offset = _first_window(i)
            first_blks.append(first_blk)
            starts.append(start)
            qhs += _split_heads(q_ref[rows, :].astype(F32), low_lanes)
            dohs += _split_heads(do_ref[rows, :], low_lanes)
            kws += [k_ref[pl.ds(start, 2 * QB), :]] * 2
            vws += [v_ref[pl.ds(start, 2 * QB), :]] * 2
            masks += [_causal_mask(2 * QB, offset)] * 2
        zeros = [zero] * len(qhs)

        scores = _attn_scores(qhs, kws, masks)
        c_first = [_row_sums(sc[3], zero) for sc in scores]
        all_done = _all_done(c_first)

        def far_totals():
            beyond = []
            for sub in range(subs):
                pair = slice(2 * sub, 2 * sub + 2)

                def far_sums(c, qh=qhs[pair], doh=dohs[pair]):
                    j, _, c0, c1, r0, r1 = c
                    at = pl.multiple_of(j * QB, QB)
                    kb = k_ref[pl.ds(at, QB), :]
                    vb = v_ref[pl.ds(at, QB), :]
                    far = _attn_tiles(qh, [kb, kb], [None, None], [c0, c1], after_s)
                    r0 = r0 + jnp.sum(far[0][2] * _dot_nt(doh[0], vb), axis=1, keepdims=True)
                    r1 = r1 + jnp.sum(far[1][2] * _dot_nt(doh[1], vb), axis=1, keepdims=True)
                    return j - 1, _sweep_done(far[0][3], far[1][3]), far[0][3], far[1][3], r0, r1

                c0, c1 = c_first[pair]
                far = lax.while_loop(cond, far_sums, (first_blks[sub] - 1, _sweep_done(c0, c1), c0, c1, zero, zero))
                beyond += [far[4], far[5]]
            return tuple(beyond)

        beyond_first = list(lax.cond(all_done, lambda: tuple(zeros), far_totals))
        done = tiles(qhs, dohs, beyond_first, kws, vws, masks, zeros, zeros, scores)
        for sub in range(subs):
            dk_ref[pl.ds(starts[sub], 2 * QB), :] += done[2 * sub][1] + done[2 * sub + 1][1]
            dv_ref[pl.ds(starts[sub], 2 * QB), :] += done[2 * sub][2] + done[2 * sub + 1][2]
        first_dq = [jnp.where(low_lanes, done[2 * sub][0], done[2 * sub + 1][0]) for sub in range(subs)]

        def sweep_on():
            final = []
            for sub in range(subs):
                pair = slice(2 * sub, 2 * sub + 2)
                t0, t1 = done[pair]

                def step(c, qh=qhs[pair], doh=dohs[pair], total=[t0[5], t1[5]]):
                    j, _, dq, c0, c1, s0, s1 = c
                    at = pl.multiple_of(j * QB, QB)
                    kb = k_ref[pl.ds(at, QB), :]
                    vb = v_ref[pl.ds(at, QB), :]
                    f0, f1 = tiles(qh, doh, total, [kb, kb], [vb, vb], [None, None], [c0, c1], [s0, s1])
                    dk_ref[pl.ds(at, QB), :] += f0[1] + f1[1]
                    dv_ref[pl.ds(at, QB), :] += f0[2] + f1[2]
                    return j - 1, _sweep_done(f0[3], f1[3]), dq + jnp.where(low_lanes, f0[0], f1[0]), f0[3], f1[3], f0[4], f1[4]

                init = (first_blks[sub] - 1, _sweep_done(t0[3], t1[3]), first_dq[sub], t0[3], t1[3], t0[4], t1[4])
                final.append(lax.while_loop(cond, step, init)[2])
            return tuple(final)

        final = lax.cond(all_done, lambda: tuple(first_dq), sweep_on)
        for sub in range(subs):
            dq_ref[sub * QB:(sub + 1) * QB, :] = final[sub]

        @pl.when(last_step)
        def _():
            ex.wait(*ex_refs)

    blk = pl.BlockSpec((subs * QB, QB), lambda p, i: (i, p))
    full = lambda off: pl.BlockSpec((S, QB), lambda p, i: (0, off + p), pipeline_mode=pl.Buffered(1))
    outs = pl.pallas_call(
        body, name="bwd_attn", grid=(n_pairs, n_steps),
        in_specs=[blk, full(n_pairs), full(2 * n_pairs), blk] + ex.specs,
        out_specs=[blk, full(0), full(0)] + ex.specs,
        out_shape=[jax.ShapeDtypeStruct((S, n_pairs * QB), F32)] * 3 + ex.out_shape,
        scratch_shapes=ex.scratch,
        compiler_params=_params("arbitrary", "arbitrary"),
    )(qkv, qkv, qkv, d_attn, *ex.arrays)
    return outs[0], outs[1], outs[2], outs[3:]


def _bwd_pool_w_in(u, d_pool, w_pool, dq, dk, dv, h1_t, n_blocks, tile):
    S, C = u.shape
    D = h1_t.shape[0]
    n_tiles = S // tile
    ng = len(POOL_WINDOWS)
    cs = 4 * C // n_blocks
    per = C // cs

    def body(u_ref, uh_ref, d_ref, dh_ref, wp_ref, dq_ref, dk_ref, dv_ref, ht_ref, dproj_ref, dw_ref, dwp_ref):
        i = pl.program_id(0)
        first = i == 0
        _zero_when(first, dw_ref, dwp_ref)
        ht = ht_ref[...]
        for d in range(per, n_blocks):
            src = (dq_ref, dk_ref, dv_ref)[d // per - 1]
            dproj = src[:, (d % per) * cs:(d % per + 1) * cs].astype(BF16)
            dproj_ref[:, d * cs:(d + 1) * cs] = dproj
            dw_ref[d] += _dot(ht, dproj)
        halo = jnp.where(first, 0.0, uh_ref[...])
        parts = _pool_deviation(u_ref[...], halo, i * tile)
        dout = d_ref[...]
        nxt = jnp.where(i == n_tiles - 1, 0.0, dh_ref[...])
        dext = jnp.concatenate([dout, nxt], axis=0).astype(BF16)
        counts = _pool_counts(i * tile, tile + HALO)
        dps, scaled = [], []
        for g in range(ng):
            lanes = slice(g * POOL_GROUP, (g + 1) * POOL_GROUP)
            dp = _dot_nt(dext[:, lanes], wp_ref[g].astype(BF16))
            dps.append(dp[:tile])
            scaled.append(dp / counts[g])
        sums = _window_sums(jnp.concatenate(scaled, axis=1), forward=True)
        du = []
        for g, w in enumerate(POOL_WINDOWS):
            lanes = slice(g * POOL_GROUP, (g + 1) * POOL_GROUP)
            du.append((sums[w][:tile, lanes] - dps[g]).astype(BF16))
            dwp_ref[g] += _dot_tn(parts[g].astype(BF16), dext[:tile, lanes])
        du = jnp.concatenate(du, axis=1)
        for d in range(per):
            dproj = du[:, d * cs:(d + 1) * cs]
            dproj_ref[:, d * cs:(d + 1) * cs] = dproj
            dw_ref[d] += _dot(ht, dproj)

    row = pl.BlockSpec((tile, C), lambda i: (i, 0))
    return pl.pallas_call(
        body, name="bwd_pool_w_in", grid=(n_tiles,),
        in_specs=[row, _prev_halo_spec(tile, C), row, _next_halo_spec(tile, C, n_tiles), _const(w_pool.shape),
                  row, row, row, pl.BlockSpec((D, tile), lambda i: (0, i))],
        out_specs=[pl.BlockSpec((tile, 4 * C), lambda i: (i, 0)), pl.BlockSpec((n_blocks, D, cs), lambda i: (0, 0, 0)),
                   pl.BlockSpec(w_pool.shape, lambda i: (0, 0, 0))],
        out_shape=[jax.ShapeDtypeStruct((S, 4 * C), BF16), jax.ShapeDtypeStruct((n_blocks, D, cs), F32),
                   jax.ShapeDtypeStruct(w_pool.shape, F32)],
        compiler_params=_params("arbitrary"),
    )(u, u, d_pool, d_pool, w_pool, dq, dk, dv, h1_t)


def _bwd_x(dproj, w_in_t, x, dx2, g1, tile, ex):
    S, D = x.shape
    n_tiles = S // tile

    def body(dp_ref, w_ref, x_ref, dx2_ref, g_ref, *rest):
        dx_ref, dg_ref = rest[ex.n:ex.n + 2]
        ex_refs = ex.split(rest[:ex.n] + rest[ex.n + 2:])
        first, last = _grid_ends((n_tiles,))

        @pl.when(first)
        def _():
            ex.start(*ex_refs)
            dg_ref[...] = jnp.zeros_like(dg_ref)

        dh = _dot(dp_ref[...], w_ref[...])
        xf = x_ref[...]
        r1 = _rms(xf)
        n1 = xf * r1
        dg_ref[...] += _colsum(dh * n1)
        dx_ref[...] = dx2_ref[...] + _norm_bwd(dh * g_ref[...], n1, r1)

        @pl.when(last)
        def _():
            ex.wait(*ex_refs)

    row = lambda w: pl.BlockSpec((tile, w), lambda i: (i, 0))
    outs = pl.pallas_call(
        body, name="bwd_x", grid=(n_tiles,),
        in_specs=[row(w_in_t.shape[0]), _const(w_in_t.shape), row(D), row(D), _const((1, D))] + ex.specs,
        out_specs=[row(D), pl.BlockSpec((1, D), lambda i: (0, 0))] + ex.specs,
        out_shape=[jax.ShapeDtypeStruct((S, D), F32), jax.ShapeDtypeStruct((1, D), F32)] + ex.out_shape,
        scratch_shapes=ex.scratch,
        compiler_params=_params("arbitrary"),
    )(dproj, w_in_t, x, dx2, g1, *ex.arrays)
    return outs[0], outs[1], outs[2:]


def _mesh_position():
    x, y, c = lax.axis_index("x"), lax.axis_index("y"), lax.axis_index("c")
    return x, y, c, 4 * x + 2 * y + c


def _peer(x, y, c, k):
    px = 1 - x if k & 4 else x
    py = 1 - y if k & 2 else y
    pc = 1 - c if k & 1 else c
    return (px, py, pc), 4 * px + 2 * py + pc


class _Exchange:
    def __init__(self, arrays, gather):
        self.arrays, self.gather, self.n = list(arrays), gather, len(arrays)
        self.out_shape = [jax.ShapeDtypeStruct(((N_DEV,) + a.shape) if gather else a.shape, a.dtype) for a in arrays]
        self.specs = [pl.BlockSpec(memory_space=pl.ANY)] * self.n
        copies = self.n * (N_DEV - 1)
        self.scratch = [pltpu.SemaphoreType.DMA((copies,)), pltpu.SemaphoreType.DMA((copies,)),
                        pltpu.SemaphoreType.DMA((self.n,))]

    def _copies(self, ins, outs, sems):
        send_sems, recv_sems, local_sems = sems
        x, y, c, me = _mesh_position()
        local, remote = [], []
        for a in range(self.n):
            mine = ins[a] if self.gather else ins[a].at[me]
            local.append(pltpu.make_async_copy(mine, outs[a].at[me], local_sems.at[a]))
            for k in range(1, N_DEV):
                peer, peer_idx = _peer(x, y, c, k)
                src = ins[a] if self.gather else ins[a].at[peer_idx]
                sem = a * (N_DEV - 1) + k - 1
                remote.append(pltpu.make_async_remote_copy(
                    src_ref=src, dst_ref=outs[a].at[me], send_sem=send_sems.at[sem], recv_sem=recv_sems.at[sem],
                    device_id=peer, device_id_type=MESH))
        return local, remote

    def start(self, ins, outs, sems):
        local, remote = self._copies(ins, outs, sems)
        for cp in local + remote:
            cp.start()

    def wait(self, ins, outs, sems):
        local, remote = self._copies(ins, outs, sems)
        for cp in remote:
            cp.wait_send()
        for cp in remote:
            cp.wait_recv()
        for cp in local:
            cp.wait()

    def split(self, refs):
        return refs[:self.n], refs[self.n:2 * self.n], refs[2 * self.n:]


def _all_to_all(arrays, gather, name):
    ex = _Exchange(arrays, gather)

    def body(*refs):
        ins, outs, sems = ex.split(refs)
        ex.start(ins, outs, sems)
        ex.wait(ins, outs, sems)

    return pl.pallas_call(body, name=name, in_specs=ex.specs, out_specs=ex.specs, out_shape=ex.out_shape,
                          scratch_shapes=ex.scratch)(*ex.arrays)


def _reduce_adamw(parts, w, m, v, rows):
    R, C = w.shape

    def body(p_ref, w_ref, m_ref, v_ref, g_ref, d_ref, nm_ref, nv_ref):
        g = p_ref[0].astype(F32)
        for s in range(1, N_DEV):
            g = g + p_ref[s].astype(F32)
        g_ref[...] = g
        m_new = ADAM_B1 * m_ref[...] + (1.0 - ADAM_B1) * g
        v_new = ADAM_B2 * v_ref[...] + (1.0 - ADAM_B2) * (g * g)
        m_hat = m_new / (1.0 - ADAM_B1 ** ADAM_STEP)
        v_hat = v_new / (1.0 - ADAM_B2 ** ADAM_STEP)
        d_ref[...] = -ADAM_LR * (m_hat / (jnp.sqrt(v_hat) + ADAM_EPS) + ADAM_WD * w_ref[...])
        nm_ref[...] = m_new
        nv_ref[...] = v_new

    row = pl.BlockSpec((rows, C), lambda i: (i, 0))
    return pl.pallas_call(
        body, name="reduce_adamw", grid=(R // rows,),
        in_specs=[pl.BlockSpec((N_DEV, rows, C), lambda i: (0, i, 0)), row, row, row],
        out_specs=[row] * 4, out_shape=[jax.ShapeDtypeStruct((R, C), F32)] * 4,
        compiler_params=_params("parallel"),
    )(parts, w, m, v)


def _row_tile(rows, cols):
    fits = [t for t in range(8, rows + 1, 8) if rows % t == 0 and N_DEV * t * cols * 4 <= 4 * 1024 * 1024]
    return max(fits) if fits else rows


SMALL_COLS = 1024


def _pack_small(vals):
    rows = []
    for a in vals:
        flat = a.reshape(-1)
        pad = (-flat.shape[0]) % SMALL_COLS
        rows.append(jnp.pad(flat, (0, pad)).reshape(-1, SMALL_COLS))
    packed = jnp.concatenate(rows, axis=0)
    return jnp.pad(packed, ((0, (-packed.shape[0]) % 8), (0, 0)))


def _unpack_small(packed, like):
    out, r = [], 0
    for a in like:
        n = a.size
        nr = -(-n // SMALL_COLS)
        out.append(packed[r:r + nr].reshape(-1)[:n].reshape(a.shape))
        r += nr
    return out


def kernel(x, norm_mix_pre, w_in, w_pool, pool_scale, attn_scale, w_out, norm_mix_post, norm_ffn_pre, w_up, conv_w, conv_b, w_down, norm_ffn_post, loss_target, m_norm_mix_pre, m_w_in, m_w_pool, m_pool_scale, m_attn_scale, m_w_out, m_norm_mix_post, m_norm_ffn_pre, m_w_up, m_conv_w, m_conv_b, m_w_down, m_norm_ffn_post, v_norm_mix_pre, v_w_in, v_w_pool, v_pool_scale, v_attn_scale, v_w_out, v_norm_mix_post, v_norm_ffn_pre, v_w_up, v_conv_w, v_conv_b, v_w_down, v_norm_ffn_post):
    S, D = x.shape[1], x.shape[2]
    d_ff_block = w_up.shape[2]

    xs, target = x[0], loss_target[0]
    g1, g2, g3, g4 = norm_mix_pre, norm_mix_post, norm_ffn_pre, norm_ffn_post
    big = min(512, S)
    small = min(256, S)
    n_pairs = pool_scale.shape[1] // QB
    conv_b_g = conv_b.reshape(N_DEV, 1, d_ff_block)

    (w_in_g,) = _all_to_all([w_in[0].astype(BF16)], gather=True, name="gather_w_in")
    h1_t, u, qkv = _fwd_inproj(xs, g1, w_in_g, big)
    pool_out = _fwd_pool(u, w_pool[0], big)
    attn_out, (w_out_g, w_up_g, w_down_g, conv_w_g) = _fwd_attn(
        qkv, n_pairs, _Exchange([w_out[0].astype(BF16), w_up[0].astype(BF16), w_down[0].astype(BF16), conv_w[0]], gather=True),
        min(ATTN_FWD_BLOCKS, S // QB))
    w_out_full = w_out_g.reshape(D, D)
    w_down4 = w_down_g.reshape(D_FF_SHARDS, d_ff_block, D)
    mix, x2, h2, h2_t = _fwd_outproj(pool_out, attn_out, pool_scale, attn_scale, w_out_full, xs, g2, g3, big)
    upre, gate_val, dy, df, loss_cols, dg4 = _fwd_ffn_loss(h2, w_up_g, conv_w_g, conv_b_g, w_down4, x2, target, g4, small)
    loss = lax.psum(0.5 * jnp.sum(loss_cols) / D, ("x", "y", "c"))

    dupre_g, dupre_v, d_wd4, d_wup, d_cb, d_cw = _bwd_ffn_blocks(gate_val, upre, conv_w_g, w_down4, df, h2_t, min(1024, S))
    dx2, dmix, dg3, dg2 = _bwd_ffn_tokens(dupre_g, dupre_v, w_up_g, x2, dy, mix, g2, g3, big)
    d_pool, d_attn, d_wout, d_ps, d_as = _bwd_outproj(dmix, w_out_full, pool_out, attn_out, pool_scale, attn_scale, big)
    d_wdown_g = d_wd4.reshape(N_DEV, w_down.shape[1], D)
    d_wout_g = d_wout.reshape(N_DEV, D // N_DEV, D)
    dq, dk, dv, late_parts = _bwd_attn(qkv, d_attn, n_pairs, _Exchange([d_wout_g, d_wup, d_wdown_g, d_cw], gather=False),
                                       min(ATTN_BWD_BLOCKS, S // QB))
    dproj, d_win, d_wp = _bwd_pool_w_in(u, d_pool, w_pool[0], dq, dk, dv, h1_t, N_DEV, big)
    w_in_t = w_in_g.transpose(0, 2, 1).reshape(-1, D)
    dx, dg1, (win_parts,) = _bwd_x(dproj, w_in_t, xs, dx2, g1, big, _Exchange([d_win], gather=False))
    big_parts = [win_parts] + list(late_parts)
    r = dict(dx=dx, g1=dg1, w_pool=d_wp, pool_scale=d_ps, attn_scale=d_as, g2=dg2, g3=dg3, conv_b=d_cb, g4=dg4)

    small_names = ["norm_mix_pre", "w_pool", "pool_scale", "attn_scale", "norm_mix_post", "norm_ffn_pre", "conv_b", "norm_ffn_post"]
    small_w = dict(norm_mix_pre=norm_mix_pre, w_pool=w_pool, pool_scale=pool_scale, attn_scale=attn_scale,
                   norm_mix_post=norm_mix_post, norm_ffn_pre=norm_ffn_pre, conv_b=conv_b, norm_ffn_post=norm_ffn_post)
    small_m = dict(norm_mix_pre=m_norm_mix_pre, w_pool=m_w_pool, pool_scale=m_pool_scale, attn_scale=m_attn_scale,
                   norm_mix_post=m_norm_mix_post, norm_ffn_pre=m_norm_ffn_pre, conv_b=m_conv_b, norm_ffn_post=m_norm_ffn_post)
    small_v = dict(norm_mix_pre=v_norm_mix_pre, w_pool=v_w_pool, pool_scale=v_pool_scale, attn_scale=v_attn_scale,
                   norm_mix_post=v_norm_mix_post, norm_ffn_pre=v_norm_ffn_pre, conv_b=v_conv_b, norm_ffn_post=v_norm_ffn_post)
    small_g = dict(norm_mix_pre=r["g1"], w_pool=r["w_pool"], pool_scale=r["pool_scale"], attn_scale=r["attn_scale"],
                   norm_mix_post=r["g2"], norm_ffn_pre=r["g3"], conv_b=r["conv_b"], norm_ffn_post=r["g4"])
    like = [small_w[n] for n in small_names]
    packed_g = _pack_small([small_g[n] for n in small_names])

    (small_parts,) = _all_to_all([packed_g], gather=True, name="gather_small_grads")

    def update(parts, w, m, v):
        R, C = w.shape
        return _reduce_adamw(parts, w, m, v, _row_tile(R, C))

    res = {}
    res["w_in"] = update(big_parts[0], w_in[0], m_w_in[0], v_w_in[0])
    res["w_out"] = update(big_parts[1], w_out[0], m_w_out[0], v_w_out[0])
    res["w_up"] = update(big_parts[2], w_up[0], m_w_up[0], v_w_up[0])
    res["w_down"] = update(big_parts[3], w_down[0], m_w_down[0], v_w_down[0])
    res["conv_w"] = update(big_parts[4], conv_w[0], m_conv_w[0], v_conv_w[0])
    small_res = update(small_parts, _pack_small(like), _pack_small([small_m[n] for n in small_names]),
                       _pack_small([small_v[n] for n in small_names]))
    small_res = [_unpack_small(t, like) for t in small_res]
    for idx, n in enumerate(small_names):
        res[n] = tuple(t[idx] for t in small_res)

    order = ["norm_mix_pre", "w_in", "w_pool", "pool_scale", "attn_scale", "w_out", "norm_mix_post", "norm_ffn_pre",
             "w_up", "conv_w", "conv_b", "w_down", "norm_ffn_post"]
    shaped = {n: tuple(t.reshape(s.shape) for t in res[n])
              for n, s in dict(norm_mix_pre=norm_mix_pre, w_in=w_in, w_pool=w_pool, pool_scale=pool_scale, attn_scale=attn_scale,
                               w_out=w_out, norm_mix_post=norm_mix_post, norm_ffn_pre=norm_ffn_pre, w_up=w_up, conv_w=conv_w,
                               conv_b=conv_b, w_down=w_down, norm_ffn_post=norm_ffn_post).items()}
    outs = [loss, r["dx"].reshape(x.shape)]
    for k in range(4):
        outs += [shaped[n][k] for n in order]
    return tuple(outs)
```

```python
import functools

import jax
import jax.numpy as jnp
from jax import lax
from jax.experimental import pallas as pl
from jax.experimental.pallas import tpu as pltpu

F32 = jnp.float32
BF16 = jnp.bfloat16
HIGHEST = lax.Precision.HIGHEST

N_DEV = 8
EPS = 1e-6
POOL_WINDOWS = (2, 4, 8, 16)
POOL_GROUP = 128
HALO = 16
HEAD_DIM = 64
QB = 128
ATTN_SCALE = HEAD_DIM ** -0.5
ATTN_FWD_BLOCKS = 16
ATTN_BWD_BLOCKS = 8
EXP_UNDERFLOW = -88.0
D_FF_SHARDS = 4

ADAM_LR = 0.001
ADAM_B1 = 0.9
ADAM_B2 = 0.999
ADAM_EPS = 1e-08
ADAM_WD = 0.01
ADAM_STEP = 10

VMEM_LIMIT_V7X = 56 * 1024 * 1024
MESH = pl.DeviceIdType.MESH


def _params(*semantics):
    return pltpu.CompilerParams(dimension_semantics=semantics, vmem_limit_bytes=VMEM_LIMIT_V7X)


def _const(shape):
    zeros = (0,) * len(shape)
    return pl.BlockSpec(shape, lambda *_: zeros, pipeline_mode=pl.Buffered(1))


def _dot(a, b):
    return jnp.dot(a, b, preferred_element_type=F32)


def _dot_nt(a, b):
    return lax.dot_general(a, b, (((1,), (1,)), ((), ())), preferred_element_type=F32)


def _dot_tn(a, b):
    return lax.dot_general(a, b, (((0,), (0,)), ((), ())), preferred_element_type=F32)


def _rms(v):
    return lax.rsqrt(jnp.mean(v * v, axis=-1, keepdims=True) + EPS)


def _norm_bwd(dn_times_gain, n, r):
    return r * (dn_times_gain - n * jnp.mean(dn_times_gain * n, axis=-1, keepdims=True))


def _zero_when(first, *refs):
    @pl.when(first)
    def _():
        for ref in refs:
            ref[...] = jnp.zeros_like(ref)


def _colsum(v):
    return jnp.sum(v, axis=0, keepdims=True)


def _grid_ends(grid):
    ids = [pl.program_id(a) for a in range(len(grid))]
    first = functools.reduce(jnp.logical_and, [i == 0 for i in ids])
    last = functools.reduce(jnp.logical_and, [i == n - 1 for i, n in zip(ids, grid)])
    return first, last


def _fwd_inproj(x, g1, w_in_g, tile):
    S, D = x.shape
    nb, _, cs = w_in_g.shape
    d_pool = 2 * cs

    def body(x_ref, g_ref, w_ref, ht_ref, u_ref, qkv_ref):
        xf = x_ref[...]
        h = (xf * _rms(xf) * g_ref[...]).astype(BF16)
        ht_ref[...] = h.T
        for d in range(nb):
            o = _dot(h, w_ref[d])
            if d < 2:
                u_ref[:, d * cs:(d + 1) * cs] = o
            else:
                qkv_ref[:, (d - 2) * cs:(d - 1) * cs] = o.astype(BF16)

    return pl.pallas_call(
        body, name="fwd_inproj", grid=(S // tile,),
        in_specs=[pl.BlockSpec((tile, D), lambda i: (i, 0)), _const((1, D)), _const(w_in_g.shape)],
        out_specs=[pl.BlockSpec((D, tile), lambda i: (0, i)), pl.BlockSpec((tile, d_pool), lambda i: (i, 0)),
                   pl.BlockSpec((tile, 3 * d_pool), lambda i: (i, 0))],
        out_shape=[jax.ShapeDtypeStruct((D, S), BF16), jax.ShapeDtypeStruct((S, d_pool), F32),
                   jax.ShapeDtypeStruct((S, 3 * d_pool), BF16)],
        compiler_params=_params("parallel"),
    )(x, g1, w_in_g)


def _window_sums(ext, forward):
    n = ext.shape[0]
    sums, s, sh = {}, ext, 1
    while sh < POOL_WINDOWS[-1]:
        s = s + pltpu.roll(s, (n - sh) if forward else sh, axis=0)
        sh *= 2
        sums[sh] = s
    return sums


def _pool_counts(t0, rows):
    t1 = (lax.broadcasted_iota(jnp.int32, (rows, 1), 0) + t0 + 1).astype(F32)
    return [jnp.minimum(t1, float(w)) for w in POOL_WINDOWS]


def _pool_deviation(u, halo, t0):
    T = u.shape[0]
    sums = _window_sums(jnp.concatenate([halo, u], axis=0), forward=False)
    counts = _pool_counts(t0, T)
    parts = []
    for g, w in enumerate(POOL_WINDOWS):
        lanes = slice(g * POOL_GROUP, (g + 1) * POOL_GROUP)
        parts.append(sums[w][HALO:, lanes] / counts[g] - u[:, lanes])
    return parts


def _prev_halo_spec(tile, width):
    return pl.BlockSpec((HALO, width), lambda i: (jnp.maximum(i * (tile // HALO) - 1, 0), 0))


def _next_halo_spec(tile, width, n_tiles):
    last = n_tiles * (tile // HALO) - 1
    return pl.BlockSpec((HALO, width), lambda i: (jnp.minimum((i + 1) * (tile // HALO), last), 0))


def _fwd_pool(u, w_pool, tile):
    S, C = u.shape

    def body(u_ref, halo_ref, wp_ref, o_ref):
        i = pl.program_id(0)
        halo = jnp.where(i > 0, halo_ref[...], 0.0)
        parts = _pool_deviation(u_ref[...], halo, i * tile)
        for g, p in enumerate(parts):
            o_ref[:, g * POOL_GROUP:(g + 1) * POOL_GROUP] = _dot(p.astype(BF16), wp_ref[g].astype(BF16))

    return pl.pallas_call(
        body, name="fwd_pool", grid=(S // tile,),
        in_specs=[pl.BlockSpec((tile, C), lambda i: (i, 0)), _prev_halo_spec(tile, C), _const(w_pool.shape)],
        out_specs=pl.BlockSpec((tile, C), lambda i: (i, 0)),
        out_shape=jax.ShapeDtypeStruct((S, C), F32),
        compiler_params=_params("parallel"),
    )(u, u, w_pool)


def _low_lanes():
    return lax.broadcasted_iota(jnp.int32, (QB, 2 * HEAD_DIM), 1) < HEAD_DIM


LOG_PIECES = 2
GRAD_PIECES = 3


def _triangle(inclusive, pieces):
    row = lax.broadcasted_iota(jnp.int32, (pieces * QB, QB), 0) % QB
    col = lax.broadcasted_iota(jnp.int32, (pieces * QB, QB), 1)
    return ((row >= col) if inclusive else (row > col)).astype(BF16)


def _pieces(v, n):
    out, rest = [], v
    for _ in range(n - 1):
        piece = rest.astype(BF16)
        out.append(piece)
        rest = rest - piece.astype(F32)
    out.append(rest.astype(BF16))
    return jnp.concatenate(out, axis=1)


def _causal_mask(width, offset):
    row = lax.broadcasted_iota(jnp.int32, (QB, width), 0)
    col = lax.broadcasted_iota(jnp.int32, (QB, width), 1)
    return col < row + offset


def _row_sums(vals, carry):
    for b in reversed(range(vals.shape[1] // QB)):
        carry = carry + jnp.sum(vals[:, b * QB:(b + 1) * QB], axis=1, keepdims=True)
    return carry


def _suffix_sums(vals, tri, carry):
    n = vals.shape[1] // QB
    out, run = [None] * n, carry
    for b in reversed(range(n)):
        blk = vals[:, b * QB:(b + 1) * QB]
        out[b] = _dot(_pieces(blk, tri.shape[0] // QB), tri) + run
        run = run + jnp.sum(blk, axis=1, keepdims=True)
    return (out[0] if n == 1 else jnp.concatenate(out, axis=1)), run


def _attn_tiles(qhs, kws, masks, carries, after_s):
    return _attn_weights(_attn_scores(qhs, kws, masks), masks, carries, after_s)


def _attn_scores(qhs, kws, masks):
    zs = [_dot_nt(qh, kw) * ATTN_SCALE for qh, kw in zip(qhs, kws)]
    es = [jnp.exp(-jnp.abs(z)) for z in zs]
    softplus = [jnp.maximum(z, 0.0) + jnp.log(1.0 + e) for z, e in zip(zs, es)]
    log_1m_beta = [-sp if m is None else jnp.where(m, -sp, 0.0) for sp, m in zip(softplus, masks)]
    return list(zip(zs, es, softplus, log_1m_beta))


def _attn_weights(scores, masks, carries, after_s):
    sums = [_suffix_sums(l, after_s, c) for (_, _, _, l), c in zip(scores, carries)]
    weights = [jnp.exp(z - sp + st) for (z, _, sp, _), (st, _) in zip(scores, sums)]
    weights = [a if m is None else jnp.where(m, a, 0.0) for a, m in zip(weights, masks)]
    return [(z, e, a, c) for (z, e, _, _), a, (_, c) in zip(scores, weights, sums)]


def _split_heads(v, low_lanes):
    return jnp.where(low_lanes, v, 0.0).astype(BF16), jnp.where(low_lanes, 0.0, v).astype(BF16)


def _sweep_done(c0, c1):
    return (jnp.maximum(jnp.max(c0), jnp.max(c1)) < EXP_UNDERFLOW).astype(jnp.int32)


def _all_done(carries):
    return jnp.max(functools.reduce(jnp.maximum, carries)) < EXP_UNDERFLOW


def _first_window(i):
    first_blk = jnp.maximum(i - 1, 0)
    return first_blk, pl.multiple_of(first_blk * QB, QB), (i - first_blk) * QB


def _fwd_attn(qkv, n_pairs, ex, subs):
    S = qkv.shape[0]
    n_steps = S // (subs * QB)

    def body(q_ref, k_ref, v_ref, *rest):
        o_ref = rest[ex.n]
        ex_refs = ex.split(rest[:ex.n] + rest[ex.n + 1:])
        first_step, last_step = _grid_ends((n_pairs, n_steps))

        @pl.when(first_step)
        def _():
            ex.start(*ex_refs)

        low_lanes = _low_lanes()
        after_s = _triangle(False, LOG_PIECES)
        zero = jnp.zeros((QB, 1), F32)

        def cond(c):
            return jnp.logical_and(c[0] >= 0, c[1] == 0)

        qhs, kws, vws, masks, first_blks = [], [], [], [], []
        for sub in range(subs):
            i = pl.program_id(1) * subs + sub
            first_blk, start, offset = _first_window(i)
            first_blks.append(first_blk)
            qhs += _split_heads(q_ref[sub * QB:(sub + 1) * QB, :].astype(F32), low_lanes)
            kws += [k_ref[pl.ds(start, 2 * QB), :]] * 2
            vws += [v_ref[pl.ds(start, 2 * QB), :]] * 2
            masks += [_causal_mask(2 * QB, offset)] * 2
        tiles = _attn_tiles(qhs, kws, masks, [zero] * len(qhs), after_s)
        outs = [_dot(t[2].astype(BF16), vw) for t, vw in zip(tiles, vws)]

        first_out = [jnp.where(low_lanes, outs[2 * sub], outs[2 * sub + 1]) for sub in range(subs)]

        def sweep_on():
            final = []
            for sub in range(subs):
                def step(c, qh=qhs[2 * sub:2 * sub + 2]):
                    j, _, acc, c0, c1 = c
                    at = pl.multiple_of(j * QB, QB)
                    kb = k_ref[pl.ds(at, QB), :]
                    vb = v_ref[pl.ds(at, QB), :]
                    far = _attn_tiles(qh, [kb, kb], [None, None], [c0, c1], after_s)
                    acc = acc + jnp.where(low_lanes, _dot(far[0][2].astype(BF16), vb), _dot(far[1][2].astype(BF16), vb))
                    return j - 1, _sweep_done(far[0][3], far[1][3]), acc, far[0][3], far[1][3]

                c0, c1 = tiles[2 * sub][3], tiles[2 * sub + 1][3]
                final.append(lax.while_loop(cond, step, (first_blks[sub] - 1, _sweep_done(c0, c1), first_out[sub], c0, c1))[2])
            return tuple(final)

        final = lax.cond(_all_done([t[3] for t in tiles]), lambda: tuple(first_out), sweep_on)
        for sub in range(subs):
            o_ref[sub * QB:(sub + 1) * QB, :] = final[sub]

        @pl.when(last_step)
        def _():
            ex.wait(*ex_refs)

    outs = pl.pallas_call(
        body, name="fwd_attn", grid=(n_pairs, n_steps),
        in_specs=[pl.BlockSpec((subs * QB, QB), lambda p, i: (i, p)),
                  pl.BlockSpec((S, QB), lambda p, i: (0, n_pairs + p), pipeline_mode=pl.Buffered(1)),
                  pl.BlockSpec((S, QB), lambda p, i: (0, 2 * n_pairs + p), pipeline_mode=pl.Buffered(1))] + ex.specs,
        out_specs=[pl.BlockSpec((subs * QB, QB), lambda p, i: (i, p))] + ex.specs,
        out_shape=[jax.ShapeDtypeStruct((S, n_pairs * QB), F32)] + ex.out_shape,
        scratch_shapes=ex.scratch,
        compiler_params=_params("arbitrary", "arbitrary"),
    )(qkv, qkv, qkv, *ex.arrays)
    return outs[0], outs[1:]


def _normalized_heads(pool_out, attn_out):
    rp, ra = _rms(pool_out), _rms(attn_out)
    return pool_out * rp, rp, attn_out * ra, ra


def _fwd_outproj(pool_out, attn_out, pool_scale, attn_scale, w_out, x, g2, g3, tile):
    S, D = x.shape
    C = pool_out.shape[1]

    def body(p_ref, a_ref, ps_ref, as_ref, w_ref, x_ref, g2_ref, g3_ref, mix_ref, x2_ref, h2_ref, h2t_ref):
        n_p, _, n_a, _ = _normalized_heads(p_ref[...], a_ref[...])
        mix = _dot((n_p * ps_ref[...]).astype(BF16), w_ref[:C, :]) + _dot((n_a * as_ref[...]).astype(BF16), w_ref[C:, :])
        mix_ref[...] = mix
        x2 = x_ref[...] + mix * _rms(mix) * g2_ref[...]
        x2_ref[...] = x2
        h2 = (x2 * _rms(x2) * g3_ref[...]).astype(BF16)
        h2_ref[...] = h2
        h2t_ref[...] = h2.T

    row = lambda w: pl.BlockSpec((tile, w), lambda i: (i, 0))
    return pl.pallas_call(
        body, name="fwd_outproj", grid=(S // tile,),
        in_specs=[row(C), row(C), _const((1, C)), _const((1, C)), _const(w_out.shape), row(D), _const((1, D)), _const((1, D))],
        out_specs=[row(D), row(D), row(D), pl.BlockSpec((D, tile), lambda i: (0, i))],
        out_shape=[jax.ShapeDtypeStruct((S, D), F32), jax.ShapeDtypeStruct((S, D), F32), jax.ShapeDtypeStruct((S, D), BF16),
                   jax.ShapeDtypeStruct((D, S), BF16)],
        compiler_params=_params("parallel"),
    )(pool_out, attn_out, pool_scale, attn_scale, w_out, x, g2, g3)


def _conv_taps(tile_rows, halo_rows):
    T = tile_rows.shape[0]
    ext = jnp.concatenate([halo_rows.astype(F32), tile_rows.astype(F32)], axis=0)
    return pltpu.roll(ext, 2, axis=0)[HALO:], pltpu.roll(ext, 1, axis=0)[HALO:], ext[HALO:]


def _tap_rows(cw_ref, d):
    return [cw_ref[d, k:k + 1, :] for k in range(3)]


def _gated_unit(taps_gate, taps_val, cw_gate, cw_val, cb_gate, cb_val):
    gate = cw_gate[0] * taps_gate[0] + cw_gate[1] * taps_gate[1] + cw_gate[2] * taps_gate[2] + cb_gate
    val = cw_val[0] * taps_val[0] + cw_val[1] * taps_val[1] + cw_val[2] * taps_val[2] + cb_val
    sig = 1.0 / (1.0 + jnp.exp(-gate))
    return gate, val, sig


def _fwd_ffn_loss(h2, w_up_g, conv_w_g, conv_b_g, w_down4, x2, target, g4, tile):
    S, D = x2.shape
    nb, _, cs = w_up_g.shape
    half = D_FF_SHARDS

    def body(h_ref, w_ref, cw_ref, cb_ref, wd_ref, x2_ref, t_ref, g4_ref, upre_ref, gv_ref, dy_ref, df_ref, loss_ref, dg4_ref, halo_ref):
        _zero_when(pl.program_id(0) == 0, loss_ref, dg4_ref, halo_ref)
        h = h_ref[...]

        def up(s):
            return _dot(h, w_ref[s]), _dot(h, w_ref[s + half])

        f = jnp.zeros((tile, D), F32)
        ahead = up(0)
        for s in range(half):
            ug, uv = ahead
            if s + 1 < half:
                ahead = up(s + 1)
            upre_ref[s] = ug.astype(BF16)
            upre_ref[s + half] = uv.astype(BF16)
            gate, val, sig = _gated_unit(_conv_taps(ug, halo_ref[s]), _conv_taps(uv, halo_ref[s + half]),
                                         _tap_rows(cw_ref, s), _tap_rows(cw_ref, s + half), cb_ref[s], cb_ref[s + half])
            halo_ref[s] = ug[tile - HALO:, :]
            halo_ref[s + half] = uv[tile - HALO:, :]
            gv_ref[s] = gate.astype(BF16)
            gv_ref[s + half] = val.astype(BF16)
            f = f + _dot((gate * sig * val).astype(BF16), wd_ref[s])
        r4 = _rms(f)
        n4 = f * r4
        err = x2_ref[...] + n4 * g4_ref[...] - t_ref[...]
        dy = err * (1.0 / D)
        dy_ref[...] = dy
        df_ref[...] = _norm_bwd(dy * g4_ref[...], n4, r4).astype(BF16)
        loss_ref[...] += _colsum(err * err)
        dg4_ref[...] += _colsum(dy * n4)

    row = lambda w: pl.BlockSpec((tile, w), lambda i: (i, 0))
    return pl.pallas_call(
        body, name="fwd_ffn_loss", grid=(S // tile,),
        in_specs=[row(D), _const(w_up_g.shape), _const(conv_w_g.shape), _const(conv_b_g.shape), _const(w_down4.shape),
                  row(D), row(D), _const((1, D))],
        out_specs=[pl.BlockSpec((nb, tile, cs), lambda i: (0, i, 0)), pl.BlockSpec((nb, tile, cs), lambda i: (0, i, 0)), row(D), row(D),
                   pl.BlockSpec((1, D), lambda i: (0, 0)), pl.BlockSpec((1, D), lambda i: (0, 0))],
        out_shape=[jax.ShapeDtypeStruct((nb, S, cs), BF16), jax.ShapeDtypeStruct((nb, S, cs), BF16),
                   jax.ShapeDtypeStruct((S, D), F32), jax.ShapeDtypeStruct((S, D), BF16),
                   jax.ShapeDtypeStruct((1, D), F32), jax.ShapeDtypeStruct((1, D), F32)],
        scratch_shapes=[pltpu.VMEM((nb, HALO, cs), F32)],
        compiler_params=_params("arbitrary"),
    )(h2, w_up_g, conv_w_g, conv_b_g, w_down4, x2, target, g4)


def _bwd_down(upre, conv_w_g, conv_b_g, w_down4, df, tile):
    nb, S, cs = upre.shape
    D = df.shape[1]
    n_tiles = S // tile

    def body(ug_ref, uv_ref, hg_ref, hv_ref, cwg_ref, cwv_ref, cbg_ref, cbv_ref, wd_ref, df_ref,
             dg_ref, dv_ref, dwd_ref, dbg_ref, dbv_ref, dcwg_ref, dcwv_ref):
        i = pl.program_id(1)
        first = i == 0
        _zero_when(first, dwd_ref, dbg_ref, dbv_ref, dcwg_ref, dcwv_ref)
        halo_g = jnp.where(first, jnp.zeros_like(hg_ref[0]), hg_ref[0])
        halo_v = jnp.where(first, jnp.zeros_like(hv_ref[0]), hv_ref[0])
        taps_g, taps_v = _conv_taps(ug_ref[0], halo_g), _conv_taps(uv_ref[0], halo_v)
        gate, val, sig = _gated_unit(taps_g, taps_v, _tap_rows(cwg_ref, 0), _tap_rows(cwv_ref, 0), cbg_ref[0], cbv_ref[0])
        silu = gate * sig
        dfb = df_ref[...]
        dact = _dot_nt(dfb, wd_ref[0])
        dwd_ref[0] += _dot_tn((silu * val).astype(BF16), dfb)
        dgate = dact * val * (sig * (1.0 + gate * (1.0 - sig)))
        dval = dact * silu
        dg_ref[0] = dgate.astype(BF16)
        dv_ref[0] = dval.astype(BF16)
        dbg_ref[0] += _colsum(dgate)
        dbv_ref[0] += _colsum(dval)
        for k in range(3):
            dcwg_ref[0, k:k + 1, :] += _colsum(dgate * taps_g[k])
            dcwv_ref[0, k:k + 1, :] += _colsum(dval * taps_v[k])

    half = D_FF_SHARDS
    blk = lambda off: pl.BlockSpec((1, tile, cs), lambda s, i: (s + off, i, 0))
    halo = lambda off: pl.BlockSpec((1, HALO, cs), lambda s, i: (s + off, jnp.maximum(i * (tile // HALO) - 1, 0), 0))
    par = lambda off, r: pl.BlockSpec((1, r, cs), lambda s, i: (s + off, 0, 0))
    outs = pl.pallas_call(
        body, name="bwd_down", grid=(half, n_tiles),
        in_specs=[blk(0), blk(half), halo(0), halo(half), par(0, 3), par(half, 3), par(0, 1), par(half, 1),
                  pl.BlockSpec((1, cs, D), lambda s, i: (s, 0, 0)), pl.BlockSpec((tile, D), lambda s, i: (i, 0))],
        out_specs=[blk(0), blk(0), pl.BlockSpec((1, cs, D), lambda s, i: (s, 0, 0)),
                   par(0, 1), par(0, 1), par(0, 3), par(0, 3)],
        out_shape=[jax.ShapeDtypeStruct((half, S, cs), BF16), jax.ShapeDtypeStruct((half, S, cs), BF16),
                   jax.ShapeDtypeStruct((half, cs, D), F32),
                   jax.ShapeDtypeStruct((half, 1, cs), F32), jax.ShapeDtypeStruct((half, 1, cs), F32),
                   jax.ShapeDtypeStruct((half, 3, cs), F32), jax.ShapeDtypeStruct((half, 3, cs), F32)],
        compiler_params=_params("parallel", "arbitrary"),
    )(upre, upre, upre, upre, conv_w_g, conv_w_g, conv_b_g, conv_b_g, w_down4, df)
    dgate, dval, d_wd, dbg, dbv, dcwg, dcwv = outs
    return dgate, dval, d_wd, jnp.concatenate([dbg, dbv], axis=0), jnp.concatenate([dcwg, dcwv], axis=0)


def _bwd_up_x(dgate, dval, conv_w_g, w_up_g, x2, dy, mix, g2, g3, tile):
    half, S, cs = dgate.shape
    nb = 2 * half
    D = x2.shape[1]
    n_tiles = S // tile

    def body(dg_ref, dv_ref, hg_ref, hv_ref, cw_ref, w_ref, x2_ref, dy_ref, mix_ref, g2_ref, g3_ref,
             dupre_ref, dx2_ref, dmix_ref, dg3_ref, dg2_ref):
        i = pl.program_id(0)
        last = i == n_tiles - 1
        _zero_when(i == 0, dg3_ref, dg2_ref)
        dh2 = jnp.zeros((tile, D), F32)
        for d in range(nb):
            src, halo = (dg_ref, hg_ref) if d < half else (dv_ref, hv_ref)
            nxt = jnp.where(last, jnp.zeros_like(halo[d % half]), halo[d % half])
            ext = jnp.concatenate([src[d % half].astype(F32), nxt.astype(F32)], axis=0)
            n = ext.shape[0]
            cw = _tap_rows(cw_ref, d)
            dupre = (cw[2] * ext + cw[1] * pltpu.roll(ext, n - 1, axis=0) + cw[0] * pltpu.roll(ext, n - 2, axis=0))[:tile]
            dupre = dupre.astype(BF16)
            dupre_ref[d] = dupre
            dh2 = dh2 + _dot_nt(dupre, w_ref[d])
        x2 = x2_ref[...]
        r3 = _rms(x2)
        n3 = x2 * r3
        dg3_ref[...] += _colsum(dh2 * n3)
        dx2 = dy_ref[...] + _norm_bwd(dh2 * g3_ref[...], n3, r3)
        dx2_ref[...] = dx2
        mix = mix_ref[...]
        r2 = _rms(mix)
        n2 = mix * r2
        dg2_ref[...] += _colsum(dx2 * n2)
        dmix_ref[...] = _norm_bwd(dx2 * g2_ref[...], n2, r2).astype(BF16)

    row = lambda w: pl.BlockSpec((tile, w), lambda i: (i, 0))
    blk = pl.BlockSpec((half, tile, cs), lambda i: (0, i, 0))
    last_halo = n_tiles * (tile // HALO) - 1
    halo = pl.BlockSpec((half, HALO, cs), lambda i: (0, jnp.minimum((i + 1) * (tile // HALO), last_halo), 0))
    acc = pl.BlockSpec((1, D), lambda i: (0, 0))
    return pl.pallas_call(
        body, name="bwd_up_x", grid=(n_tiles,),
        in_specs=[blk, blk, halo, halo, _const(conv_w_g.shape), _const(w_up_g.shape), row(D), row(D), row(D),
                  _const((1, D)), _const((1, D))],
        out_specs=[pl.BlockSpec((nb, tile, cs), lambda i: (0, i, 0)), row(D), row(D), acc, acc],
        out_shape=[jax.ShapeDtypeStruct((nb, S, cs), BF16), jax.ShapeDtypeStruct((S, D), F32),
                   jax.ShapeDtypeStruct((S, D), BF16), jax.ShapeDtypeStruct((1, D), F32), jax.ShapeDtypeStruct((1, D), F32)],
        compiler_params=_params("arbitrary"),
    )(dgate, dval, dgate, dval, conv_w_g, w_up_g, x2, dy, mix, g2, g3)


def _bwd_weight(act_t, dout, tile):
    D, S = act_t.shape
    nb, _, cs = dout.shape

    def body(a_ref, d_ref, o_ref):
        _zero_when(pl.program_id(1) == 0, o_ref)
        o_ref[0] += _dot(a_ref[...], d_ref[0])

    return pl.pallas_call(
        body, name="bwd_w_up", grid=(nb, S // tile),
        in_specs=[pl.BlockSpec((D, tile), lambda d, i: (0, i)), pl.BlockSpec((1, tile, cs), lambda d, i: (d, i, 0))],
        out_specs=pl.BlockSpec((1, D, cs), lambda d, i: (d, 0, 0)),
        out_shape=jax.ShapeDtypeStruct((nb, D, cs), F32),
        compiler_params=_params("parallel", "arbitrary"),
    )(act_t, dout)


def _bwd_ffn_blocks(gate_val, upre, conv_w_g, w_down4, df, h2_t, tile):
    nb, S, cs = upre.shape
    D = df.shape[1]
    n_tiles = S // tile
    half = D_FF_SHARDS

    def body(g_ref, v_ref, ug_ref, uv_ref, cwg_ref, cwv_ref, wd_ref, df_ref, ht_ref,
             dug_ref, duv_ref, dwd_ref, dwg_ref, dwv_ref, dbg_ref, dbv_ref, dcwg_ref, dcwv_ref, next_ref):
        _zero_when(pl.program_id(1) == 0, dwd_ref, dwg_ref, dwv_ref, dbg_ref, dbv_ref, dcwg_ref, dcwv_ref, next_ref)
        dfb = df_ref[...]
        dact = _dot_nt(dfb, wd_ref[0])
        gate, val = g_ref[0].astype(F32), v_ref[0].astype(F32)
        sig = 1.0 / (1.0 + jnp.exp(-gate))
        silu = gate * sig
        dwd_ref[0] += _dot_tn((silu * val).astype(BF16), dfb)
        ht = ht_ref[...]

        def through_conv(dup, slot, cw_ref, u_ref, du_ref, dw_ref, db_ref, dcw_ref):
            ext = jnp.concatenate([dup, next_ref[slot]], axis=0)
            n = ext.shape[0]
            shifted = (dup, pltpu.roll(ext, n - 1, axis=0)[:tile], pltpu.roll(ext, n - 2, axis=0)[:tile])
            next_ref[slot] = dup[:HALO]
            cw = _tap_rows(cw_ref, 0)
            dupre = (cw[2] * shifted[0] + cw[1] * shifted[1] + cw[0] * shifted[2]).astype(BF16)
            du_ref[0] = dupre
            dw_ref[0] += _dot(ht, dupre)
            u = u_ref[0].astype(F32)
            db_ref[0] += _colsum(dup)
            for k in range(3):
                dcw_ref[0, k:k + 1, :] += _colsum(shifted[2 - k] * u)

        through_conv(dact * val * (sig * (1.0 + gate * (1.0 - sig))), 0, cwg_ref, ug_ref, dug_ref, dwg_ref, dbg_ref, dcwg_ref)
        through_conv(dact * silu, 1, cwv_ref, uv_ref, duv_ref, dwv_ref, dbv_ref, dcwv_ref)

    rev = lambda i: n_tiles - 1 - i
    blk = lambda off: pl.BlockSpec((1, tile, cs), lambda s, i: (s + off, rev(i), 0))
    par = lambda off, r: pl.BlockSpec((1, r, cs), lambda s, i: (s + off, 0, 0))
    acc = lambda r, c: pl.BlockSpec((1, r, c), lambda s, i: (s, 0, 0), pipeline_mode=pl.Buffered(1))
    outs = pl.pallas_call(
        body, name="bwd_ffn_blocks", grid=(half, n_tiles),
        in_specs=[blk(0), blk(half), blk(0), blk(half), par(0, 3), par(half, 3),
                  acc(cs, D), pl.BlockSpec((tile, D), lambda s, i: (rev(i), 0)), pl.BlockSpec((D, tile), lambda s, i: (0, rev(i)))],
        out_specs=[blk(0), blk(0), acc(cs, D), acc(D, cs), acc(D, cs), acc(1, cs), acc(1, cs), acc(3, cs), acc(3, cs)],
        out_shape=[jax.ShapeDtypeStruct((half, S, cs), BF16), jax.ShapeDtypeStruct((half, S, cs), BF16),
                   jax.ShapeDtypeStruct((half, cs, D), F32),
                   jax.ShapeDtypeStruct((half, D, cs), F32), jax.ShapeDtypeStruct((half, D, cs), F32),
                   jax.ShapeDtypeStruct((half, 1, cs), F32), jax.ShapeDtypeStruct((half, 1, cs), F32),
                   jax.ShapeDtypeStruct((half, 3, cs), F32), jax.ShapeDtypeStruct((half, 3, cs), F32)],
        scratch_shapes=[pltpu.VMEM((2, HALO, cs), F32)],
        compiler_params=_params("arbitrary", "arbitrary"),
    )(gate_val, gate_val, upre, upre, conv_w_g, conv_w_g, w_down4, df, h2_t)
    dupre_g, dupre_v, d_wd, d_wg, d_wv, dbg, dbv, dcwg, dcwv = outs
    return (dupre_g, dupre_v, d_wd, jnp.concatenate([d_wg, d_wv], axis=0), jnp.concatenate([dbg, dbv], axis=0),
            jnp.concatenate([dcwg, dcwv], axis=0))


def _bwd_ffn_tokens(dupre_g, dupre_v, w_up_g, x2, dy, mix, g2, g3, tile):
    half, S, cs = dupre_g.shape
    D = x2.shape[1]

    def body(dg_ref, dv_ref, w_ref, x2_ref, dy_ref, mix_ref, g2_ref, g3_ref, dx2_ref, dmix_ref, dg3_ref, dg2_ref):
        _zero_when(pl.program_id(0) == 0, dg3_ref, dg2_ref)
        parts = [_dot_nt(dg_ref[d], w_ref[d]) for d in range(half)] + [_dot_nt(dv_ref[d], w_ref[d + half]) for d in range(half)]
        while len(parts) > 1:
            parts = [a + b for a, b in zip(parts[::2], parts[1::2])]
        dh2 = parts[0]
        x2 = x2_ref[...]
        r3 = _rms(x2)
        n3 = x2 * r3
        dg3_ref[...] += _colsum(dh2 * n3)
        dx2 = dy_ref[...] + _norm_bwd(dh2 * g3_ref[...], n3, r3)
        dx2_ref[...] = dx2
        mix = mix_ref[...]
        r2 = _rms(mix)
        n2 = mix * r2
        dg2_ref[...] += _colsum(dx2 * n2)
        dmix_ref[...] = _norm_bwd(dx2 * g2_ref[...], n2, r2).astype(BF16)

    row = lambda w: pl.BlockSpec((tile, w), lambda i: (i, 0))
    blk = pl.BlockSpec((half, tile, cs), lambda i: (0, i, 0))
    acc = pl.BlockSpec((1, D), lambda i: (0, 0))
    return pl.pallas_call(
        body, name="bwd_ffn_tokens", grid=(S // tile,),
        in_specs=[blk, blk, _const(w_up_g.shape), row(D), row(D), row(D), _const((1, D)), _const((1, D))],
        out_specs=[row(D), row(D), acc, acc],
        out_shape=[jax.ShapeDtypeStruct((S, D), F32), jax.ShapeDtypeStruct((S, D), BF16),
                   jax.ShapeDtypeStruct((1, D), F32), jax.ShapeDtypeStruct((1, D), F32)],
        compiler_params=_params("arbitrary"),
    )(dupre_g, dupre_v, w_up_g, x2, dy, mix, g2, g3)


def _bwd_outproj(dmix, w_out, pool_out, attn_out, pool_scale, attn_scale, tile):
    S, D = dmix.shape
    C = pool_out.shape[1]

    def body(dm_ref, w_ref, p_ref, a_ref, ps_ref, as_ref, dp_ref, da_ref, dw_ref, dps_ref, das_ref):
        _zero_when(pl.program_id(0) == 0, dw_ref, dps_ref, das_ref)
        dmx = dm_ref[...]
        dmerged = _dot_nt(dmx, w_ref[...])
        n_p, r_p, n_a, r_a = _normalized_heads(p_ref[...], a_ref[...])
        merged = jnp.concatenate([(n_p * ps_ref[...]).astype(BF16), (n_a * as_ref[...]).astype(BF16)], axis=1)
        dw_ref[...] += _dot_tn(merged, dmx)
        dm_p, dm_a = dmerged[:, :C], dmerged[:, C:]
        dps_ref[...] += _colsum(dm_p * n_p)
        das_ref[...] += _colsum(dm_a * n_a)
        dp_ref[...] = _norm_bwd(dm_p * ps_ref[...], n_p, r_p)
        da_ref[...] = _norm_bwd(dm_a * as_ref[...], n_a, r_a)

    row = lambda w: pl.BlockSpec((tile, w), lambda i: (i, 0))
    return pl.pallas_call(
        body, name="bwd_outproj", grid=(S // tile,),
        in_specs=[row(D), _const(w_out.shape), row(C), row(C), _const((1, C)), _const((1, C))],
        out_specs=[row(C), row(C), pl.BlockSpec(w_out.shape, lambda i: (0, 0)),
                   pl.BlockSpec((1, C), lambda i: (0, 0)), pl.BlockSpec((1, C), lambda i: (0, 0))],
        out_shape=[jax.ShapeDtypeStruct((S, C), F32), jax.ShapeDtypeStruct((S, C), F32),
                   jax.ShapeDtypeStruct(w_out.shape, F32), jax.ShapeDtypeStruct((1, C), F32), jax.ShapeDtypeStruct((1, C), F32)],
        compiler_params=_params("arbitrary"),
    )(dmix, w_out, pool_out, attn_out, pool_scale, attn_scale)


def _bwd_attn(qkv, d_attn, n_pairs, ex, subs):
    S = qkv.shape[0]
    n_steps = S // (subs * QB)

    def body(q_ref, k_ref, v_ref, do_ref, *rest):
        dq_ref, dk_ref, dv_ref = rest[ex.n:ex.n + 3]
        ex_refs = ex.split(rest[:ex.n] + rest[ex.n + 3:])
        first_step, last_step = _grid_ends((n_pairs, n_steps))

        @pl.when(first_step)
        def _():
            ex.start(*ex_refs)

        @pl.when(pl.program_id(1) == 0)
        def _():
            dk_ref[...] = jnp.zeros_like(dk_ref)
            dv_ref[...] = jnp.zeros_like(dv_ref)

        low_lanes = _low_lanes()
        after_s, from_s = _triangle(False, LOG_PIECES), _triangle(True, GRAD_PIECES)
        zero = jnp.zeros((QB, 1), F32)

        def tiles(qhs, dohs, totals, kws, vws, masks, cs, gs, scores=None):
            fw = _attn_weights(scores or _attn_scores(qhs, kws, masks), masks, cs, after_s)
            gvals = [t[2] * _dot_nt(doh, vw) for t, doh, vw in zip(fw, dohs, vws)]
            sums = [_suffix_sums(g, from_s, g0) for g, g0 in zip(gvals, gs)]
            totals = [tot if m is None else tot + sm[1] for tot, m, sm in zip(totals, masks, sums)]
            dzs = []
            for (z, e, _, _), g, (nearer, _), tot, m in zip(fw, gvals, sums, totals, masks):
                inv = 1.0 / (1.0 + e)
                sig_abs, sig_neg = inv, e * inv
                pos = z >= 0.0
                dz = g * jnp.where(pos, sig_neg, sig_abs) - jnp.where(pos, sig_abs, sig_neg) * (tot - nearer)
                if m is not None:
                    dz = jnp.where(m, dz, 0.0)
                dzs.append((dz * ATTN_SCALE).astype(BF16))
            dqs = [_dot(dz, kw) for dz, kw in zip(dzs, kws)]
            dks = [_dot_tn(dz, qh) for dz, qh in zip(dzs, qhs)]
            dvs = [_dot_tn(t[2].astype(BF16), doh) for t, doh in zip(fw, dohs)]
            return [(dq, dk, dv, t[3], sm[1], tot) for dq, dk, dv, t, sm, tot in zip(dqs, dks, dvs, fw, sums, totals)]

        def cond(c):
            return jnp.logical_and(c[0] >= 0, c[1] == 0)

        qhs, dohs, kws, vws, masks, first_blks, starts = [], [], [], [], [], [], []
        for sub in range(subs):
            i = pl.program_id(1) * subs + sub
            rows = slice(sub * QB, (sub + 1) * QB)
            first_blk, start, offset = _first_window(i)
            first_blks.append(first_blk)
            starts.append(start)
            qhs += _split_heads(q_ref[rows, :].astype(F32), low_lanes)
            dohs += _split_heads(do_ref[rows, :], low_lanes)
            kws += [k_ref[pl.ds(start, 2 * QB), :]] * 2
            vws += [v_ref[pl.ds(start, 2 * QB), :]] * 2
            masks += [_causal_mask(2 * QB, offset)] * 2
        zeros = [zero] * len(qhs)

        scores = _attn_scores(qhs, kws, masks)
        c_first = [_row_sums(sc[3], zero) for sc in scores]
        all_done = _all_done(c_first)

        def far_totals():
            beyond = []
            for sub in range(subs):
                pair = slice(2 * sub, 2 * sub + 2)

                def far_sums(c, qh=qhs[pair], doh=dohs[pair]):
                    j, _, c0, c1, r0, r1 = c
                    at = pl.multiple_of(j * QB, QB)
                    kb = k_ref[pl.ds(at, QB), :]
                    vb = v_ref[pl.ds(at, QB), :]
                    far = _attn_tiles(qh, [kb, kb], [None, None], [c0, c1], after_s)
                    r0 = r0 + jnp.sum(far[0][2] * _dot_nt(doh[0], vb), axis=1, keepdims=True)
                    r1 = r1 + jnp.sum(far[1][2] * _dot_nt(doh[1], vb), axis=1, keepdims=True)
                    return j - 1, _sweep_done(far[0][3], far[1][3]), far[0][3], far[1][3], r0, r1

                c0, c1 = c_first[pair]
                far = lax.while_loop(cond, far_sums, (first_blks[sub] - 1, _sweep_done(c0, c1), c0, c1, zero, zero))
                beyond += [far[4], far[5]]
            return tuple(beyond)

        beyond_first = list(lax.cond(all_done, lambda: tuple(zeros), far_totals))
        done = tiles(qhs, dohs, beyond_first, kws, vws, masks, zeros, zeros, scores)
        for sub in range(subs):
            dk_ref[pl.ds(starts[sub], 2 * QB), :] += done[2 * sub][1] + done[2 * sub + 1][1]
            dv_ref[pl.ds(starts[sub], 2 * QB), :] += done[2 * sub][2] + done[2 * sub + 1][2]
        first_dq = [jnp.where(low_lanes, done[2 * sub][0], done[2 * sub + 1][0]) for sub in range(subs)]

        def sweep_on():
            final = []
            for sub in range(subs):
                pair = slice(2 * sub, 2 * sub + 2)
                t0, t1 = done[pair]

                def step(c, qh=qhs[pair], doh=dohs[pair], total=[t0[5], t1[5]]):
                    j, _, dq, c0, c1, s0, s1 = c
                    at = pl.multiple_of(j * QB, QB)
                    kb = k_ref[pl.ds(at, QB), :]
                    vb = v_ref[pl.ds(at, QB), :]
                    f0, f1 = tiles(qh, doh, total, [kb, kb], [vb, vb], [None, None], [c0, c1], [s0, s1])
                    dk_ref[pl.ds(at, QB), :] += f0[1] + f1[1]
                    dv_ref[pl.ds(at, QB), :] += f0[2] + f1[2]
                    return j - 1, _sweep_done(f0[3], f1[3]), dq + jnp.where(low_lanes, f0[0], f1[0]), f0[3], f1[3], f0[4], f1[4]

                init = (first_blks[sub] - 1, _sweep_done(t0[3], t1[3]), first_dq[sub], t0[3], t1[3], t0[4], t1[4])
                final.append(lax.while_loop(cond, step, init)[2])
            return tuple(final)

        final = lax.cond(all_done, lambda: tuple(first_dq), sweep_on)
        for sub in range(subs):
            dq_ref[sub * QB:(sub + 1) * QB, :] = final[sub]

        @pl.when(last_step)
        def _():
            ex.wait(*ex_refs)

    blk = pl.BlockSpec((subs * QB, QB), lambda p, i: (i, p))
    full = lambda off: pl.BlockSpec((S, QB), lambda p, i: (0, off + p), pipeline_mode=pl.Buffered(1))
    outs = pl.pallas_call(
        body, name="bwd_attn", grid=(n_pairs, n_steps),
        in_specs=[blk, full(n_pairs), full(2 * n_pairs), blk] + ex.specs,
        out_specs=[blk, full(0), full(0)] + ex.specs,
        out_shape=[jax.ShapeDtypeStruct((S, n_pairs * QB), F32)] * 3 + ex.out_shape,
        scratch_shapes=ex.scratch,
        compiler_params=_params("arbitrary", "arbitrary"),
    )(qkv, qkv, qkv, d_attn, *ex.arrays)
    return outs[0], outs[1], outs[2], outs[3:]


def _bwd_pool_w_in(u, d_pool, w_pool, dq, dk, dv, h1_t, n_blocks, tile):
    S, C = u.shape
    D = h1_t.shape[0]
    n_tiles = S // tile
    ng = len(POOL_WINDOWS)
    cs = 4 * C // n_blocks
    per = C // cs

    def body(u_ref, uh_ref, d_ref, dh_ref, wp_ref, dq_ref, dk_ref, dv_ref, ht_ref, dproj_ref, dw_ref, dwp_ref):
        i = pl.program_id(0)
        first = i == 0
        _zero_when(first, dw_ref, dwp_ref)
        ht = ht_ref[...]
        for d in range(per, n_blocks):
            src = (dq_ref, dk_ref, dv_ref)[d // per - 1]
            dproj = src[:, (d % per) * cs:(d % per + 1) * cs].astype(BF16)
            dproj_ref[:, d * cs:(d + 1) * cs] = dproj
            dw_ref[d] += _dot(ht, dproj)
        halo = jnp.where(first, 0.0, uh_ref[...])
        parts = _pool_deviation(u_ref[...], halo, i * tile)
        dout = d_ref[...]
        nxt = jnp.where(i == n_tiles - 1, 0.0, dh_ref[...])
        dext = jnp.concatenate([dout, nxt], axis=0).astype(BF16)
        counts = _pool_counts(i * tile, tile + HALO)
        dps, scaled = [], []
        for g in range(ng):
            lanes = slice(g * POOL_GROUP, (g + 1) * POOL_GROUP)
            dp = _dot_nt(dext[:, lanes], wp_ref[g].astype(BF16))
            dps.append(dp[:tile])
            scaled.append(dp / counts[g])
        sums = _window_sums(jnp.concatenate(scaled, axis=1), forward=True)
        du = []
        for g, w in enumerate(POOL_WINDOWS):
            lanes = slice(g * POOL_GROUP, (g + 1) * POOL_GROUP)
            du.append((sums[w][:tile, lanes] - dps[g]).astype(BF16))
            dwp_ref[g] += _dot_tn(parts[g].astype(BF16), dext[:tile, lanes])
        du = jnp.concatenate(du, axis=1)
        for d in range(per):
            dproj = du[:, d * cs:(d + 1) * cs]
            dproj_ref[:, d * cs:(d + 1) * cs] = dproj
            dw_ref[d] += _dot(ht, dproj)

    row = pl.BlockSpec((tile, C), lambda i: (i, 0))
    return pl.pallas_call(
        body, name="bwd_pool_w_in", grid=(n_tiles,),
        in_specs=[row, _prev_halo_spec(tile, C), row, _next_halo_spec(tile, C, n_tiles), _const(w_pool.shape),
                  row, row, row, pl.BlockSpec((D, tile), lambda i: (0, i))],
        out_specs=[pl.BlockSpec((tile, 4 * C), lambda i: (i, 0)), pl.BlockSpec((n_blocks, D, cs), lambda i: (0, 0, 0)),
                   pl.BlockSpec(w_pool.shape, lambda i: (0, 0, 0))],
        out_shape=[jax.ShapeDtypeStruct((S, 4 * C), BF16), jax.ShapeDtypeStruct((n_blocks, D, cs), F32),
                   jax.ShapeDtypeStruct(w_pool.shape, F32)],
        compiler_params=_params("arbitrary"),
    )(u, u, d_pool, d_pool, w_pool, dq, dk, dv, h1_t)


def _bwd_x(dproj, w_in_t, x, dx2, g1, tile, ex):
    S, D = x.shape
    n_tiles = S // tile

    def body(dp_ref, w_ref, x_ref, dx2_ref, g_ref, *rest):
        dx_ref, dg_ref = rest[ex.n:ex.n + 2]
        ex_refs = ex.split(rest[:ex.n] + rest[ex.n + 2:])
        first, last = _grid_ends((n_tiles,))

        @pl.when(first)
        def _():
            ex.start(*ex_refs)
            dg_ref[...] = jnp.zeros_like(dg_ref)

        dh = _dot(dp_ref[...], w_ref[...])
        xf = x_ref[...]
        r1 = _rms(xf)
        n1 = xf * r1
        dg_ref[...] += _colsum(dh * n1)
        dx_ref[...] = dx2_ref[...] + _norm_bwd(dh * g_ref[...], n1, r1)

        @pl.when(last)
        def _():
            ex.wait(*ex_refs)

    row = lambda w: pl.BlockSpec((tile, w), lambda i: (i, 0))
    outs = pl.pallas_call(
        body, name="bwd_x", grid=(n_tiles,),
        in_specs=[row(w_in_t.shape[0]), _const(w_in_t.shape), row(D), row(D), _const((1, D))] + ex.specs,
        out_specs=[row(D), pl.BlockSpec((1, D), lambda i: (0, 0))] + ex.specs,
        out_shape=[jax.ShapeDtypeStruct((S, D), F32), jax.ShapeDtypeStruct((1, D), F32)] + ex.out_shape,
        scratch_shapes=ex.scratch,
        compiler_params=_params("arbitrary"),
    )(dproj, w_in_t, x, dx2, g1, *ex.arrays)
    return outs[0], outs[1], outs[2:]


def _mesh_position():
    x, y, c = lax.axis_index("x"), lax.axis_index("y"), lax.axis_index("c")
    return x, y, c, 4 * x + 2 * y + c


def _peer(x, y, c, k):
    px = 1 - x if k & 4 else x
    py = 1 - y if k & 2 else y
    pc = 1 - c if k & 1 else c
    return (px, py, pc), 4 * px + 2 * py + pc


class _Exchange:
    def __init__(self, arrays, gather):
        self.arrays, self.gather, self.n = list(arrays), gather, len(arrays)
        self.out_shape = [jax.ShapeDtypeStruct(((N_DEV,) + a.shape) if gather else a.shape, a.dtype) for a in arrays]
        self.specs = [pl.BlockSpec(memory_space=pl.ANY)] * self.n
        copies = self.n * (N_DEV - 1)
        self.scratch = [pltpu.SemaphoreType.DMA((copies,)), pltpu.SemaphoreType.DMA((copies,)),
                        pltpu.SemaphoreType.DMA((self.n,))]

    def _copies(self, ins, outs, sems):
        send_sems, recv_sems, local_sems = sems
        x, y, c, me = _mesh_position()
        local, remote = [], []
        for a in range(self.n):
            mine = ins[a] if self.gather else ins[a].at[me]
            local.append(pltpu.make_async_copy(mine, outs[a].at[me], local_sems.at[a]))
            for k in range(1, N_DEV):
                peer, peer_idx = _peer(x, y, c, k)
                src = ins[a] if self.gather else ins[a].at[peer_idx]
                sem = a * (N_DEV - 1) + k - 1
                remote.append(pltpu.make_async_remote_copy(
                    src_ref=src, dst_ref=outs[a].at[me], send_sem=send_sems.at[sem], recv_sem=recv_sems.at[sem],
                    device_id=peer, device_id_type=MESH))
        return local, remote

    def start(self, ins, outs, sems):
        local, remote = self._copies(ins, outs, sems)
        for cp in local + remote:
            cp.start()

    def wait(self, ins, outs, sems):
        local, remote = self._copies(ins, outs, sems)
        for cp in remote:
            cp.wait_send()
        for cp in remote:
            cp.wait_recv()
        for cp in local:
            cp.wait()

    def split(self, refs):
        return refs[:self.n], refs[self.n:2 * self.n], refs[2 * self.n:]


class _ChipGather(_Exchange):
    def __init__(self, arrays):
        super().__init__(arrays, gather=True)

    def _plan(self, ins, outs, sems, waiting):
        send_sems, recv_sems, local_sems = sems
        x, y, c, me = _mesh_position()
        sibling = (x, y, 1 - c)
        chips = [(1 - x, y), (x, 1 - y), (1 - x, 1 - y)]
        local, first, passed, arrivals = [], [], [], []
        for a in range(self.n):
            def copy(k, block, to, src=None, a=a):
                rows = outs[a].at[block]
                return pltpu.make_async_remote_copy(
                    src_ref=rows if src is None else src, dst_ref=rows, send_sem=send_sems.at[a * (N_DEV - 1) + k],
                    recv_sem=recv_sems.at[a * (N_DEV - 1) + k], device_id=to, device_id_type=MESH)

            local.append(pltpu.make_async_copy(ins[a], outs[a].at[me], local_sems.at[a]))
            first.append(copy(0, me, sibling, src=ins[a]))
            first += [copy(1 + j, me, (px, py, c), src=ins[a]) for j, (px, py) in enumerate(chips)]
            if waiting:
                passed.append([copy(4 + j, 4 * px + 2 * py + c, sibling) for j, (px, py) in enumerate(chips)])
                arrivals.append([copy(k, me, sibling) for k in range(N_DEV - 1)])
        return local, first, passed, arrivals

    def start(self, ins, outs, sems):
        local, first, _, _ = self._plan(ins, outs, sems, waiting=False)
        for cp in local + first:
            cp.start()

    def wait(self, ins, outs, sems):
        local, first, passed, arrivals = self._plan(ins, outs, sems, waiting=True)
        for a in range(self.n):
            for j in range(3):
                arrivals[a][1 + j].wait_recv()
                passed[a][j].start()
        for a in range(self.n):
            arrivals[a][0].wait_recv()
            for j in range(3):
                arrivals[a][4 + j].wait_recv()
        for cp in first + [cp for row in passed for cp in row]:
            cp.wait_send()
        for cp in local:
            cp.wait()


def _all_to_all(arrays, gather, name):
    ex = _ChipGather(arrays) if gather else _Exchange(arrays, gather)

    def body(*refs):
        ins, outs, sems = ex.split(refs)
        ex.start(ins, outs, sems)
        ex.wait(ins, outs, sems)

    return pl.pallas_call(body, name=name, in_specs=ex.specs, out_specs=ex.specs, out_shape=ex.out_shape,
                          scratch_shapes=ex.scratch)(*ex.arrays)


def _reduce_adamw(parts, w, m, v, rows):
    R, C = w.shape

    def body(p_ref, w_ref, m_ref, v_ref, g_ref, d_ref, nm_ref, nv_ref):
        g = p_ref[0].astype(F32)
        for s in range(1, N_DEV):
            g = g + p_ref[s].astype(F32)
        g_ref[...] = g
        m_new = ADAM_B1 * m_ref[...] + (1.0 - ADAM_B1) * g
        v_new = ADAM_B2 * v_ref[...] + (1.0 - ADAM_B2) * (g * g)
        m_hat = m_new / (1.0 - ADAM_B1 ** ADAM_STEP)
        v_hat = v_new / (1.0 - ADAM_B2 ** ADAM_STEP)
        d_ref[...] = -ADAM_LR * (m_hat / (jnp.sqrt(v_hat) + ADAM_EPS) + ADAM_WD * w_ref[...])
        nm_ref[...] = m_new
        nv_ref[...] = v_new

    row = pl.BlockSpec((rows, C), lambda i: (i, 0))
    return pl.pallas_call(
        body, name="reduce_adamw", grid=(R // rows,),
        in_specs=[pl.BlockSpec((N_DEV, rows, C), lambda i: (0, i, 0)), row, row, row],
        out_specs=[row] * 4, out_shape=[jax.ShapeDtypeStruct((R, C), F32)] * 4,
        compiler_params=_params("parallel"),
    )(parts, w, m, v)


def _row_tile(rows, cols):
    fits = [t for t in range(8, rows + 1, 8) if rows % t == 0 and N_DEV * t * cols * 4 <= 4 * 1024 * 1024]
    return max(fits) if fits else rows


SMALL_COLS = 1024


def _pack_small(vals):
    rows = []
    for a in vals:
        flat = a.reshape(-1)
        pad = (-flat.shape[0]) % SMALL_COLS
        rows.append(jnp.pad(flat, (0, pad)).reshape(-1, SMALL_COLS))
    packed = jnp.concatenate(rows, axis=0)
    return jnp.pad(packed, ((0, (-packed.shape[0]) % 8), (0, 0)))


def _unpack_small(packed, like):
    out, r = [], 0
    for a in like:
        n = a.size
        nr = -(-n // SMALL_COLS)
        out.append(packed[r:r + nr].reshape(-1)[:n].reshape(a.shape))
        r += nr
    return out


def kernel(x, norm_mix_pre, w_in, w_pool, pool_scale, attn_scale, w_out, norm_mix_post, norm_ffn_pre, w_up, conv_w, conv_b, w_down, norm_ffn_post, loss_target, m_norm_mix_pre, m_w_in, m_w_pool, m_pool_scale, m_attn_scale, m_w_out, m_norm_mix_post, m_norm_ffn_pre, m_w_up, m_conv_w, m_conv_b, m_w_down, m_norm_ffn_post, v_norm_mix_pre, v_w_in, v_w_pool, v_pool_scale, v_attn_scale, v_w_out, v_norm_mix_post, v_norm_ffn_pre, v_w_up, v_conv_w, v_conv_b, v_w_down, v_norm_ffn_post):
    S, D = x.shape[1], x.shape[2]
    d_ff_block = w_up.shape[2]

    xs, target = x[0], loss_target[0]
    g1, g2, g3, g4 = norm_mix_pre, norm_mix_post, norm_ffn_pre, norm_ffn_post
    big = min(512, S)
    small = min(256, S)
    n_pairs = pool_scale.shape[1] // QB
    conv_b_g = conv_b.reshape(N_DEV, 1, d_ff_block)

    (w_in_g,) = _all_to_all([w_in[0].astype(BF16)], gather=True, name="gather_w_in")
    h1_t, u, qkv = _fwd_inproj(xs, g1, w_in_g, big)
    pool_out = _fwd_pool(u, w_pool[0], big)
    attn_out, (w_out_g, w_up_g, w_down_g, conv_w_g) = _fwd_attn(
        qkv, n_pairs, _ChipGather([w_out[0].astype(BF16), w_up[0].astype(BF16), w_down[0].astype(BF16), conv_w[0]]),
        min(ATTN_FWD_BLOCKS, S // QB))
    w_out_full = w_out_g.reshape(D, D)
    w_down4 = w_down_g.reshape(D_FF_SHARDS, d_ff_block, D)
    mix, x2, h2, h2_t = _fwd_outproj(pool_out, attn_out, pool_scale, attn_scale, w_out_full, xs, g2, g3, big)
    upre, gate_val, dy, df, loss_cols, dg4 = _fwd_ffn_loss(h2, w_up_g, conv_w_g, conv_b_g, w_down4, x2, target, g4, small)
    loss = lax.psum(0.5 * jnp.sum(loss_cols) / D, ("x", "y", "c"))

    dupre_g, dupre_v, d_wd4, d_wup, d_cb, d_cw = _bwd_ffn_blocks(gate_val, upre, conv_w_g, w_down4, df, h2_t, min(1024, S))
    dx2, dmix, dg3, dg2 = _bwd_ffn_tokens(dupre_g, dupre_v, w_up_g, x2, dy, mix, g2, g3, big)
    d_pool, d_attn, d_wout, d_ps, d_as = _bwd_outproj(dmix, w_out_full, pool_out, attn_out, pool_scale, attn_scale, big)
    d_wdown_g = d_wd4.reshape(N_DEV, w_down.shape[1], D)
    d_wout_g = d_wout.reshape(N_DEV, D // N_DEV, D)
    dq, dk, dv, late_parts = _bwd_attn(qkv, d_attn, n_pairs, _Exchange([d_wout_g, d_wup, d_wdown_g, d_cw], gather=False),
                                       min(ATTN_BWD_BLOCKS, S // QB))
    dproj, d_win, d_wp = _bwd_pool_w_in(u, d_pool, w_pool[0], dq, dk, dv, h1_t, N_DEV, big)
    w_in_t = w_in_g.transpose(0, 2, 1).reshape(-1, D)
    dx, dg1, (win_parts,) = _bwd_x(dproj, w_in_t, xs, dx2, g1, big, _Exchange([d_win], gather=False))
    big_parts = [win_parts] + list(late_parts)
    r = dict(dx=dx, g1=dg1, w_pool=d_wp, pool_scale=d_ps, attn_scale=d_as, g2=dg2, g3=dg3, conv_b=d_cb, g4=dg4)

    small_names = ["norm_mix_pre", "w_pool", "pool_scale", "attn_scale", "norm_mix_post", "norm_ffn_pre", "conv_b", "norm_ffn_post"]
    small_w = dict(norm_mix_pre=norm_mix_pre, w_pool=w_pool, pool_scale=pool_scale, attn_scale=attn_scale,
                   norm_mix_post=norm_mix_post, norm_ffn_pre=norm_ffn_pre, conv_b=conv_b, norm_ffn_post=norm_ffn_post)
    small_m = dict(norm_mix_pre=m_norm_mix_pre, w_pool=m_w_pool, pool_scale=m_pool_scale, attn_scale=m_attn_scale,
                   norm_mix_post=m_norm_mix_post, norm_ffn_pre=m_norm_ffn_pre, conv_b=m_conv_b, norm_ffn_post=m_norm_ffn_post)
    small_v = dict(norm_mix_pre=v_norm_mix_pre, w_pool=v_w_pool, pool_scale=v_pool_scale, attn_scale=v_attn_scale,
                   norm_mix_post=v_norm_mix_post, norm_ffn_pre=v_norm_ffn_pre, conv_b=v_conv_b, norm_ffn_post=v_norm_ffn_post)
    small_g = dict(norm_mix_pre=r["g1"], w_pool=r["w_pool"], pool_scale=r["pool_scale"], attn_scale=r["attn_scale"],
                   norm_mix_post=r["g2"], norm_ffn_pre=r["g3"], conv_b=r["conv_b"], norm_ffn_post=r["g4"])
    like = [small_w[n] for n in small_names]
    packed_g = _pack_small([small_g[n] for n in small_names])

    (small_parts,) = _all_to_all([packed_g], gather=True, name="gather_small_grads")

    def update(parts, w, m, v):
        R, C = w.shape
        return _reduce_adamw(parts, w, m, v, _row_tile(R, C))

    res = {}
    res["w_in"] = update(big_parts[0], w_in[0], m_w_in[0], v_w_in[0])
    res["w_out"] = update(big_parts[1], w_out[0], m_w_out[0], v_w_out[0])
    res["w_up"] = update(big_parts[2], w_up[0], m_w_up[0], v_w_up[0])
    res["w_down"] = update(big_parts[3], w_down[0], m_w_down[0], v_w_down[0])
    res["conv_w"] = update(big_parts[4], conv_w[0], m_conv_w[0], v_conv_w[0])
    small_res = update(small_parts, _pack_small(like), _pack_small([small_m[n] for n in small_names]),
                       _pack_small([small_v[n] for n in small_names]))
    small_res = [_unpack_small(t, like) for t in small_res]
    for idx, n in enumerate(small_names):
        res[n] = tuple(t[idx] for t in small_res)

    order = ["norm_mix_pre", "w_in", "w_pool", "pool_scale", "attn_scale", "w_out", "norm_mix_post", "norm_ffn_pre",
             "w_up", "conv_w", "conv_b", "w_down", "norm_ffn_post"]
    shaped = {n: tuple(t.reshape(s.shape) for t in res[n])
              for n, s in dict(norm_mix_pre=norm_mix_pre, w_in=w_in, w_pool=w_pool, pool_scale=pool_scale, attn_scale=attn_scale,
                               w_out=w_out, norm_mix_post=norm_mix_post, norm_ffn_pre=norm_ffn_pre, w_up=w_up, conv_w=conv_w,
                               conv_b=conv_b, w_down=w_down, norm_ffn_post=norm_ffn_post).items()}
    outs = [loss, r["dx"].reshape(x.shape)]
    for k in range(4):
        outs += [shaped[n][k] for n in order]
    return tuple(outs)
```

```python
import functools

import jax
import jax.numpy as jnp
from jax import lax
from jax.experimental import pallas as pl
from jax.experimental.pallas import tpu as pltpu

F32 = jnp.float32
BF16 = jnp.bfloat16
HIGHEST = lax.Precision.HIGHEST

N_DEV = 8
EPS = 1e-6
POOL_WINDOWS = (2, 4, 8, 16)
POOL_GROUP = 128
HALO = 16
HEAD_DIM = 64
QB = 128
ATTN_SCALE = HEAD_DIM ** -0.5
ATTN_FWD_BLOCKS = 16
ATTN_BWD_BLOCKS = 8
EXP_UNDERFLOW = -88.0
D_FF_SHARDS = 4

ADAM_LR = 0.001
ADAM_B1 = 0.9
ADAM_B2 = 0.999
ADAM_EPS = 1e-08
ADAM_WD = 0.01
ADAM_STEP = 10

VMEM_LIMIT_V7X = 56 * 1024 * 1024
MESH = pl.DeviceIdType.MESH


def _params(*semantics):
    return pltpu.CompilerParams(dimension_semantics=semantics, vmem_limit_bytes=VMEM_LIMIT_V7X)


def _const(shape):
    zeros = (0,) * len(shape)
    return pl.BlockSpec(shape, lambda *_: zeros, pipeline_mode=pl.Buffered(1))


def _dot(a, b):
    return jnp.dot(a, b, preferred_element_type=F32)


def _dot_nt(a, b):
    return lax.dot_general(a, b, (((1,), (1,)), ((), ())), preferred_element_type=F32)


def _dot_tn(a, b):
    return lax.dot_general(a, b, (((0,), (0,)), ((), ())), preferred_element_type=F32)


def _rms(v):
    return lax.rsqrt(jnp.mean(v * v, axis=-1, keepdims=True) + EPS)


def _norm_bwd(dn_times_gain, n, r):
    return r * (dn_times_gain - n * jnp.mean(dn_times_gain * n, axis=-1, keepdims=True))


def _zero_when(first, *refs):
    @pl.when(first)
    def _():
        for ref in refs:
            ref[...] = jnp.zeros_like(ref)


def _colsum(v):
    return jnp.sum(v, axis=0, keepdims=True)


def _grid_ends(grid):
    ids = [pl.program_id(a) for a in range(len(grid))]
    first = functools.reduce(jnp.logical_and, [i == 0 for i in ids])
    last = functools.reduce(jnp.logical_and, [i == n - 1 for i, n in zip(ids, grid)])
    return first, last


def _fwd_inproj(x, g1, w_in_g, tile):
    S, D = x.shape
    nb, _, cs = w_in_g.shape
    d_pool = 2 * cs

    def body(x_ref, g_ref, w_ref, ht_ref, u_ref, qkv_ref):
        xf = x_ref[...]
        h = (xf * _rms(xf) * g_ref[...]).astype(BF16)
        ht_ref[...] = h.T
        for d in range(nb):
            o = _dot(h, w_ref[d])
            if d < 2:
                u_ref[:, d * cs:(d + 1) * cs] = o
            else:
                qkv_ref[:, (d - 2) * cs:(d - 1) * cs] = o.astype(BF16)

    return pl.pallas_call(
        body, name="fwd_inproj", grid=(S // tile,),
        in_specs=[pl.BlockSpec((tile, D), lambda i: (i, 0)), _const((1, D)), _const(w_in_g.shape)],
        out_specs=[pl.BlockSpec((D, tile), lambda i: (0, i)), pl.BlockSpec((tile, d_pool), lambda i: (i, 0)),
                   pl.BlockSpec((tile, 3 * d_pool), lambda i: (i, 0))],
        out_shape=[jax.ShapeDtypeStruct((D, S), BF16), jax.ShapeDtypeStruct((S, d_pool), F32),
                   jax.ShapeDtypeStruct((S, 3 * d_pool), BF16)],
        compiler_params=_params("parallel"),
    )(x, g1, w_in_g)


def _window_sums(ext, forward):
    n = ext.shape[0]
    sums, s, sh = {}, ext, 1
    while sh < POOL_WINDOWS[-1]:
        s = s + pltpu.roll(s, (n - sh) if forward else sh, axis=0)
        sh *= 2
        sums[sh] = s
    return sums


def _pool_counts(t0, rows):
    t1 = (lax.broadcasted_iota(jnp.int32, (rows, 1), 0) + t0 + 1).astype(F32)
    return [jnp.minimum(t1, float(w)) for w in POOL_WINDOWS]


def _pool_deviation(u, halo, t0):
    T = u.shape[0]
    sums = _window_sums(jnp.concatenate([halo, u], axis=0), forward=False)
    counts = _pool_counts(t0, T)
    parts = []
    for g, w in enumerate(POOL_WINDOWS):
        lanes = slice(g * POOL_GROUP, (g + 1) * POOL_GROUP)
        parts.append(sums[w][HALO:, lanes] / counts[g] - u[:, lanes])
    return parts


def _prev_halo_spec(tile, width):
    return pl.BlockSpec((HALO, width), lambda i: (jnp.maximum(i * (tile // HALO) - 1, 0), 0))


def _next_halo_spec(tile, width, n_tiles):
    last = n_tiles * (tile // HALO) - 1
    return pl.BlockSpec((HALO, width), lambda i: (jnp.minimum((i + 1) * (tile // HALO), last), 0))


def _fwd_pool(u, w_pool, tile):
    S, C = u.shape

    def body(u_ref, halo_ref, wp_ref, o_ref):
        i = pl.program_id(0)
        halo = jnp.where(i > 0, halo_ref[...], 0.0)
        parts = _pool_deviation(u_ref[...], halo, i * tile)
        for g, p in enumerate(parts):
            o_ref[:, g * POOL_GROUP:(g + 1) * POOL_GROUP] = _dot(p.astype(BF16), wp_ref[g].astype(BF16))

    return pl.pallas_call(
        body, name="fwd_pool", grid=(S // tile,),
        in_specs=[pl.BlockSpec((tile, C), lambda i: (i, 0)), _prev_halo_spec(tile, C), _const(w_pool.shape)],
        out_specs=pl.BlockSpec((tile, C), lambda i: (i, 0)),
        out_shape=jax.ShapeDtypeStruct((S, C), F32),
        compiler_params=_params("parallel"),
    )(u, u, w_pool)


def _low_lanes():
    return lax.broadcasted_iota(jnp.int32, (QB, 2 * HEAD_DIM), 1) < HEAD_DIM


LOG_PIECES = 2
GRAD_PIECES = 3


def _triangle(inclusive, pieces):
    row = lax.broadcasted_iota(jnp.int32, (pieces * QB, QB), 0) % QB
    col = lax.broadcasted_iota(jnp.int32, (pieces * QB, QB), 1)
    return ((row >= col) if inclusive else (row > col)).astype(BF16)


def _pieces(v, n):
    out, rest = [], v
    for _ in range(n - 1):
        piece = rest.astype(BF16)
        out.append(piece)
        rest = rest - piece.astype(F32)
    out.append(rest.astype(BF16))
    return jnp.concatenate(out, axis=1)


def _causal_mask(width, offset):
    row = lax.broadcasted_iota(jnp.int32, (QB, width), 0)
    col = lax.broadcasted_iota(jnp.int32, (QB, width), 1)
    return col < row + offset


def _row_sums(vals, carry):
    for b in reversed(range(vals.shape[1] // QB)):
        carry = carry + jnp.sum(vals[:, b * QB:(b + 1) * QB], axis=1, keepdims=True)
    return carry


def _suffix_sums(vals, tri, carry):
    n = vals.shape[1] // QB
    out, run = [None] * n, carry
    for b in reversed(range(n)):
        blk = vals[:, b * QB:(b + 1) * QB]
        out[b] = _dot(_pieces(blk, tri.shape[0] // QB), tri) + run
        run = run + jnp.sum(blk, axis=1, keepdims=True)
    return (out[0] if n == 1 else jnp.concatenate(out, axis=1)), run


def _attn_tiles(qhs, kws, masks, carries, after_s):
    return _attn_weights(_attn_scores(qhs, kws, masks), masks, carries, after_s)


def _attn_scores(qhs, kws, masks):
    zs = [_dot_nt(qh, kw) * ATTN_SCALE for qh, kw in zip(qhs, kws)]
    es = [jnp.exp(-jnp.abs(z)) for z in zs]
    softplus = [jnp.maximum(z, 0.0) + jnp.log(1.0 + e) for z, e in zip(zs, es)]
    log_1m_beta = [-sp if m is None else jnp.where(m, -sp, 0.0) for sp, m in zip(softplus, masks)]
    return list(zip(zs, es, softplus, log_1m_beta))


def _attn_weights(scores, masks, carries, after_s):
    sums = [_suffix_sums(l, after_s, c) for (_, _, _, l), c in zip(scores, carries)]
    weights = [jnp.exp(z - sp + st) for (z, _, sp, _), (st, _) in zip(scores, sums)]
    weights = [a if m is None else jnp.where(m, a, 0.0) for a, m in zip(weights, masks)]
    return [(z, e, a, c) for (z, e, _, _), a, (_, c) in zip(scores, weights, sums)]


def _split_heads(v, low_lanes):
    return jnp.where(low_lanes, v, 0.0).astype(BF16), jnp.where(low_lanes, 0.0, v).astype(BF16)


def _sweep_done(c0, c1):
    return (jnp.maximum(jnp.max(c0), jnp.max(c1)) < EXP_UNDERFLOW).astype(jnp.int32)


def _all_done(carries):
    return jnp.max(functools.reduce(jnp.maximum, carries)) < EXP_UNDERFLOW


def _first_window(i):
    first_blk = jnp.maximum(i - 1, 0)
    return first_blk, pl.multiple_of(first_blk * QB, QB), (i - first_blk) * QB


def _fwd_attn(qkv, n_pairs, ex, subs):
    S = qkv.shape[0]
    n_steps = S // (subs * QB)

    def body(q_ref, k_ref, v_ref, *rest):
        o_ref = rest[ex.n]
        ex_refs = ex.split(rest[:ex.n] + rest[ex.n + 1:])
        first_step, last_step = _grid_ends((n_pairs, n_steps))

        @pl.when(first_step)
        def _():
            ex.start(*ex_refs)

        low_lanes = _low_lanes()
        after_s = _triangle(False, LOG_PIECES)
        zero = jnp.zeros((QB, 1), F32)

        def cond(c):
            return jnp.logical_and(c[0] >= 0, c[1] == 0)

        qhs, kws, vws, masks, first_blks = [], [], [], [], []
        for sub in range(subs):
            i = pl.program_id(1) * subs + sub
            first_blk, start, offset = _first_window(i)
            first_blks.append(first_blk)
            qhs += _split_heads(q_ref[sub * QB:(sub + 1) * QB, :].astype(F32), low_lanes)
            kws += [k_ref[pl.ds(start, 2 * QB), :]] * 2
            vws += [v_ref[pl.ds(start, 2 * QB), :]] * 2
            masks += [_causal_mask(2 * QB, offset)] * 2
        tiles = _attn_tiles(qhs, kws, masks, [zero] * len(qhs), after_s)
        outs = [_dot(t[2].astype(BF16), vw) for t, vw in zip(tiles, vws)]

        first_out = [jnp.where(low_lanes, outs[2 * sub], outs[2 * sub + 1]) for sub in range(subs)]

        def sweep_on():
            final = []
            for sub in range(subs):
                def step(c, qh=qhs[2 * sub:2 * sub + 2]):
                    j, _, acc, c0, c1 = c
                    at = pl.multiple_of(j * QB, QB)
                    kb = k_ref[pl.ds(at, QB), :]
                    vb = v_ref[pl.ds(at, QB), :]
                    far = _attn_tiles(qh, [kb, kb], [None, None], [c0, c1], after_s)
                    acc = acc + jnp.where(low_lanes, _dot(far[0][2].astype(BF16), vb), _dot(far[1][2].astype(BF16), vb))
                    return j - 1, _sweep_done(far[0][3], far[1][3]), acc, far[0][3], far[1][3]

                c0, c1 = tiles[2 * sub][3], tiles[2 * sub + 1][3]
                final.append(lax.while_loop(cond, step, (first_blks[sub] - 1, _sweep_done(c0, c1), first_out[sub], c0, c1))[2])
            return tuple(final)

        final = lax.cond(_all_done([t[3] for t in tiles]), lambda: tuple(first_out), sweep_on)
        for sub in range(subs):
            o_ref[sub * QB:(sub + 1) * QB, :] = final[sub]

        @pl.when(last_step)
        def _():
            ex.wait(*ex_refs)

    outs = pl.pallas_call(
        body, name="fwd_attn", grid=(n_pairs, n_steps),
        in_specs=[pl.BlockSpec((subs * QB, QB), lambda p, i: (i, p)),
                  pl.BlockSpec((S, QB), lambda p, i: (0, n_pairs + p), pipeline_mode=pl.Buffered(1)),
                  pl.BlockSpec((S, QB), lambda p, i: (0, 2 * n_pairs + p), pipeline_mode=pl.Buffered(1))] + ex.specs,
        out_specs=[pl.BlockSpec((subs * QB, QB), lambda p, i: (i, p))] + ex.specs,
        out_shape=[jax.ShapeDtypeStruct((S, n_pairs * QB), F32)] + ex.out_shape,
        scratch_shapes=ex.scratch,
        compiler_params=_params("arbitrary", "arbitrary"),
    )(qkv, qkv, qkv, *ex.arrays)
    return outs[0], outs[1:]


def _normalized_heads(pool_out, attn_out):
    rp, ra = _rms(pool_out), _rms(attn_out)
    return pool_out * rp, rp, attn_out * ra, ra


def _fwd_outproj(pool_out, attn_out, pool_scale, attn_scale, w_out, x, g2, g3, tile):
    S, D = x.shape
    C = pool_out.shape[1]

    def body(p_ref, a_ref, ps_ref, as_ref, w_ref, x_ref, g2_ref, g3_ref, mix_ref, x2_ref, h2_ref, h2t_ref):
        n_p, _, n_a, _ = _normalized_heads(p_ref[...], a_ref[...])
        mix = _dot((n_p * ps_ref[...]).astype(BF16), w_ref[:C, :]) + _dot((n_a * as_ref[...]).astype(BF16), w_ref[C:, :])
        mix_ref[...] = mix
        x2 = x_ref[...] + mix * _rms(mix) * g2_ref[...]
        x2_ref[...] = x2
        h2 = (x2 * _rms(x2) * g3_ref[...]).astype(BF16)
        h2_ref[...] = h2
        h2t_ref[...] = h2.T

    row = lambda w: pl.BlockSpec((tile, w), lambda i: (i, 0))
    return pl.pallas_call(
        body, name="fwd_outproj", grid=(S // tile,),
        in_specs=[row(C), row(C), _const((1, C)), _const((1, C)), _const(w_out.shape), row(D), _const((1, D)), _const((1, D))],
        out_specs=[row(D), row(D), row(D), pl.BlockSpec((D, tile), lambda i: (0, i))],
        out_shape=[jax.ShapeDtypeStruct((S, D), F32), jax.ShapeDtypeStruct((S, D), F32), jax.ShapeDtypeStruct((S, D), BF16),
                   jax.ShapeDtypeStruct((D, S), BF16)],
        compiler_params=_params("parallel"),
    )(pool_out, attn_out, pool_scale, attn_scale, w_out, x, g2, g3)


def _conv_taps(tile_rows, halo_rows):
    T = tile_rows.shape[0]
    ext = jnp.concatenate([halo_rows.astype(F32), tile_rows.astype(F32)], axis=0)
    return pltpu.roll(ext, 2, axis=0)[HALO:], pltpu.roll(ext, 1, axis=0)[HALO:], ext[HALO:]


def _tap_rows(cw_ref, d):
    return [cw_ref[d, k:k + 1, :] for k in range(3)]


def _gated_unit(taps_gate, taps_val, cw_gate, cw_val, cb_gate, cb_val):
    gate = cw_gate[0] * taps_gate[0] + cw_gate[1] * taps_gate[1] + cw_gate[2] * taps_gate[2] + cb_gate
    val = cw_val[0] * taps_val[0] + cw_val[1] * taps_val[1] + cw_val[2] * taps_val[2] + cb_val
    sig = 1.0 / (1.0 + jnp.exp(-gate))
    return gate, val, sig


def _fwd_ffn_loss(h2, w_up_g, conv_w_g, conv_b_g, w_down4, x2, target, g4, tile):
    S, D = x2.shape
    nb, _, cs = w_up_g.shape
    half = D_FF_SHARDS

    def body(h_ref, w_ref, cw_ref, cb_ref, wd_ref, x2_ref, t_ref, g4_ref, upre_ref, gv_ref, dy_ref, df_ref, loss_ref, dg4_ref, halo_ref):
        _zero_when(pl.program_id(0) == 0, loss_ref, dg4_ref, halo_ref)
        h = h_ref[...]

        def up(s):
            return _dot(h, w_ref[s]), _dot(h, w_ref[s + half])

        f = jnp.zeros((tile, D), F32)
        ahead = up(0)
        for s in range(half):
            ug, uv = ahead
            if s + 1 < half:
                ahead = up(s + 1)
            upre_ref[s] = ug.astype(BF16)
            upre_ref[s + half] = uv.astype(BF16)
            gate, val, sig = _gated_unit(_conv_taps(ug, halo_ref[s]), _conv_taps(uv, halo_ref[s + half]),
                                         _tap_rows(cw_ref, s), _tap_rows(cw_ref, s + half), cb_ref[s], cb_ref[s + half])
            halo_ref[s] = ug[tile - HALO:, :]
            halo_ref[s + half] = uv[tile - HALO:, :]
            gv_ref[s] = gate.astype(BF16)
            gv_ref[s + half] = val.astype(BF16)
            f = f + _dot((gate * sig * val).astype(BF16), wd_ref[s])
        r4 = _rms(f)
        n4 = f * r4
        err = x2_ref[...] + n4 * g4_ref[...] - t_ref[...]
        dy = err * (1.0 / D)
        dy_ref[...] = dy
        df_ref[...] = _norm_bwd(dy * g4_ref[...], n4, r4).astype(BF16)
        loss_ref[...] += _colsum(err * err)
        dg4_ref[...] += _colsum(dy * n4)

    row = lambda w: pl.BlockSpec((tile, w), lambda i: (i, 0))
    return pl.pallas_call(
        body, name="fwd_ffn_loss", grid=(S // tile,),
        in_specs=[row(D), _const(w_up_g.shape), _const(conv_w_g.shape), _const(conv_b_g.shape), _const(w_down4.shape),
                  row(D), row(D), _const((1, D))],
        out_specs=[pl.BlockSpec((nb, tile, cs), lambda i: (0, i, 0)), pl.BlockSpec((nb, tile, cs), lambda i: (0, i, 0)), row(D), row(D),
                   pl.BlockSpec((1, D), lambda i: (0, 0)), pl.BlockSpec((1, D), lambda i: (0, 0))],
        out_shape=[jax.ShapeDtypeStruct((nb, S, cs), BF16), jax.ShapeDtypeStruct((nb, S, cs), BF16),
                   jax.ShapeDtypeStruct((S, D), F32), jax.ShapeDtypeStruct((S, D), BF16),
                   jax.ShapeDtypeStruct((1, D), F32), jax.ShapeDtypeStruct((1, D), F32)],
        scratch_shapes=[pltpu.VMEM((nb, HALO, cs), F32)],
        compiler_params=_params("arbitrary"),
    )(h2, w_up_g, conv_w_g, conv_b_g, w_down4, x2, target, g4)


def _bwd_down(upre, conv_w_g, conv_b_g, w_down4, df, tile):
    nb, S, cs = upre.shape
    D = df.shape[1]
    n_tiles = S // tile

    def body(ug_ref, uv_ref, hg_ref, hv_ref, cwg_ref, cwv_ref, cbg_ref, cbv_ref, wd_ref, df_ref,
             dg_ref, dv_ref, dwd_ref, dbg_ref, dbv_ref, dcwg_ref, dcwv_ref):
        i = pl.program_id(1)
        first = i == 0
        _zero_when(first, dwd_ref, dbg_ref, dbv_ref, dcwg_ref, dcwv_ref)
        halo_g = jnp.where(first, jnp.zeros_like(hg_ref[0]), hg_ref[0])
        halo_v = jnp.where(first, jnp.zeros_like(hv_ref[0]), hv_ref[0])
        taps_g, taps_v = _conv_taps(ug_ref[0], halo_g), _conv_taps(uv_ref[0], halo_v)
        gate, val, sig = _gated_unit(taps_g, taps_v, _tap_rows(cwg_ref, 0), _tap_rows(cwv_ref, 0), cbg_ref[0], cbv_ref[0])
        silu = gate * sig
        dfb = df_ref[...]
        dact = _dot_nt(dfb, wd_ref[0])
        dwd_ref[0] += _dot_tn((silu * val).astype(BF16), dfb)
        dgate = dact * val * (sig * (1.0 + gate * (1.0 - sig)))
        dval = dact * silu
        dg_ref[0] = dgate.astype(BF16)
        dv_ref[0] = dval.astype(BF16)
        dbg_ref[0] += _colsum(dgate)
        dbv_ref[0] += _colsum(dval)
        for k in range(3):
            dcwg_ref[0, k:k + 1, :] += _colsum(dgate * taps_g[k])
            dcwv_ref[0, k:k + 1, :] += _colsum(dval * taps_v[k])

    half = D_FF_SHARDS
    blk = lambda off: pl.BlockSpec((1, tile, cs), lambda s, i: (s + off, i, 0))
    halo = lambda off: pl.BlockSpec((1, HALO, cs), lambda s, i: (s + off, jnp.maximum(i * (tile // HALO) - 1, 0), 0))
    par = lambda off, r: pl.BlockSpec((1, r, cs), lambda s, i: (s + off, 0, 0))
    outs = pl.pallas_call(
        body, name="bwd_down", grid=(half, n_tiles),
        in_specs=[blk(0), blk(half), halo(0), halo(half), par(0, 3), par(half, 3), par(0, 1), par(half, 1),
                  pl.BlockSpec((1, cs, D), lambda s, i: (s, 0, 0)), pl.BlockSpec((tile, D), lambda s, i: (i, 0))],
        out_specs=[blk(0), blk(0), pl.BlockSpec((1, cs, D), lambda s, i: (s, 0, 0)),
                   par(0, 1), par(0, 1), par(0, 3), par(0, 3)],
        out_shape=[jax.ShapeDtypeStruct((half, S, cs), BF16), jax.ShapeDtypeStruct((half, S, cs), BF16),
                   jax.ShapeDtypeStruct((half, cs, D), F32),
                   jax.ShapeDtypeStruct((half, 1, cs), F32), jax.ShapeDtypeStruct((half, 1, cs), F32),
                   jax.ShapeDtypeStruct((half, 3, cs), F32), jax.ShapeDtypeStruct((half, 3, cs), F32)],
        compiler_params=_params("parallel", "arbitrary"),
    )(upre, upre, upre, upre, conv_w_g, conv_w_g, conv_b_g, conv_b_g, w_down4, df)
    dgate, dval, d_wd, dbg, dbv, dcwg, dcwv = outs
    return dgate, dval, d_wd, jnp.concatenate([dbg, dbv], axis=0), jnp.concatenate([dcwg, dcwv], axis=0)


def _bwd_up_x(dgate, dval, conv_w_g, w_up_g, x2, dy, mix, g2, g3, tile):
    half, S, cs = dgate.shape
    nb = 2 * half
    D = x2.shape[1]
    n_tiles = S // tile

    def body(dg_ref, dv_ref, hg_ref, hv_ref, cw_ref, w_ref, x2_ref, dy_ref, mix_ref, g2_ref, g3_ref,
             dupre_ref, dx2_ref, dmix_ref, dg3_ref, dg2_ref):
        i = pl.program_id(0)
        last = i == n_tiles - 1
        _zero_when(i == 0, dg3_ref, dg2_ref)
        dh2 = jnp.zeros((tile, D), F32)
        for d in range(nb):
            src, halo = (dg_ref, hg_ref) if d < half else (dv_ref, hv_ref)
            nxt = jnp.where(last, jnp.zeros_like(halo[d % half]), halo[d % half])
            ext = jnp.concatenate([src[d % half].astype(F32), nxt.astype(F32)], axis=0)
            n = ext.shape[0]
            cw = _tap_rows(cw_ref, d)
            dupre = (cw[2] * ext + cw[1] * pltpu.roll(ext, n - 1, axis=0) + cw[0] * pltpu.roll(ext, n - 2, axis=0))[:tile]
            dupre = dupre.astype(BF16)
            dupre_ref[d] = dupre
            dh2 = dh2 + _dot_nt(dupre, w_ref[d])
        x2 = x2_ref[...]
        r3 = _rms(x2)
        n3 = x2 * r3
        dg3_ref[...] += _colsum(dh2 * n3)
        dx2 = dy_ref[...] + _norm_bwd(dh2 * g3_ref[...], n3, r3)
        dx2_ref[...] = dx2
        mix = mix_ref[...]
        r2 = _rms(mix)
        n2 = mix * r2
        dg2_ref[...] += _colsum(dx2 * n2)
        dmix_ref[...] = _norm_bwd(dx2 * g2_ref[...], n2, r2).astype(BF16)

    row = lambda w: pl.BlockSpec((tile, w), lambda i: (i, 0))
    blk = pl.BlockSpec((half, tile, cs), lambda i: (0, i, 0))
    last_halo = n_tiles * (tile // HALO) - 1
    halo = pl.BlockSpec((half, HALO, cs), lambda i: (0, jnp.minimum((i + 1) * (tile // HALO), last_halo), 0))
    acc = pl.BlockSpec((1, D), lambda i: (0, 0))
    return pl.pallas_call(
        body, name="bwd_up_x", grid=(n_tiles,),
        in_specs=[blk, blk, halo, halo, _const(conv_w_g.shape), _const(w_up_g.shape), row(D), row(D), row(D),
                  _const((1, D)), _const((1, D))],
        out_specs=[pl.BlockSpec((nb, tile, cs), lambda i: (0, i, 0)), row(D), row(D), acc, acc],
        out_shape=[jax.ShapeDtypeStruct((nb, S, cs), BF16), jax.ShapeDtypeStruct((S, D), F32),
                   jax.ShapeDtypeStruct((S, D), BF16), jax.ShapeDtypeStruct((1, D), F32), jax.ShapeDtypeStruct((1, D), F32)],
        compiler_params=_params("arbitrary"),
    )(dgate, dval, dgate, dval, conv_w_g, w_up_g, x2, dy, mix, g2, g3)


def _bwd_weight(act_t, dout, tile):
    D, S = act_t.shape
    nb, _, cs = dout.shape

    def body(a_ref, d_ref, o_ref):
        _zero_when(pl.program_id(1) == 0, o_ref)
        o_ref[0] += _dot(a_ref[...], d_ref[0])

    return pl.pallas_call(
        body, name="bwd_w_up", grid=(nb, S // tile),
        in_specs=[pl.BlockSpec((D, tile), lambda d, i: (0, i)), pl.BlockSpec((1, tile, cs), lambda d, i: (d, i, 0))],
        out_specs=pl.BlockSpec((1, D, cs), lambda d, i: (d, 0, 0)),
        out_shape=jax.ShapeDtypeStruct((nb, D, cs), F32),
        compiler_params=_params("parallel", "arbitrary"),
    )(act_t, dout)


def _bwd_ffn_blocks(gate_val, upre, conv_w_g, w_down4, df, h2_t, tile):
    nb, S, cs = upre.shape
    D = df.shape[1]
    n_tiles = S // tile
    half = D_FF_SHARDS

    def body(g_ref, v_ref, ug_ref, uv_ref, cwg_ref, cwv_ref, wd_ref, df_ref, ht_ref,
             dug_ref, duv_ref, dwd_ref, dwu_ref, dbg_ref, dbv_ref, dcwg_ref, dcwv_ref, next_ref):
        _zero_when(pl.program_id(1) == 0, dwd_ref, dwu_ref, dbg_ref, dbv_ref, dcwg_ref, dcwv_ref, next_ref)
        dfb = df_ref[...]
        dact = _dot_nt(dfb, wd_ref[0])
        gate, val = g_ref[0].astype(F32), v_ref[0].astype(F32)
        sig = 1.0 / (1.0 + jnp.exp(-gate))
        silu = gate * sig
        dwd_ref[0] += _dot_tn((silu * val).astype(BF16), dfb)
        ht = ht_ref[...]

        def through_conv(dup, slot, cw_ref, u_ref, du_ref, dw_ref, db_ref, dcw_ref):
            ext = jnp.concatenate([dup, next_ref[slot]], axis=0)
            n = ext.shape[0]
            shifted = (dup, pltpu.roll(ext, n - 1, axis=0)[:tile], pltpu.roll(ext, n - 2, axis=0)[:tile])
            next_ref[slot] = dup[:HALO]
            cw = _tap_rows(cw_ref, 0)
            dupre = (cw[2] * shifted[0] + cw[1] * shifted[1] + cw[0] * shifted[2]).astype(BF16)
            du_ref[0] = dupre
            dw_ref[0] += _dot(ht, dupre)
            u = u_ref[0].astype(F32)
            db_ref[0] += _colsum(dup)
            for k in range(3):
                dcw_ref[0, k:k + 1, :] += _colsum(shifted[2 - k] * u)

        through_conv(dact * val * (sig * (1.0 + gate * (1.0 - sig))), 0, cwg_ref, ug_ref, dug_ref, dwu_ref.at[0], dbg_ref, dcwg_ref)
        through_conv(dact * silu, 1, cwv_ref, uv_ref, duv_ref, dwu_ref.at[1], dbv_ref, dcwv_ref)

    rev = lambda i: n_tiles - 1 - i
    blk = lambda off: pl.BlockSpec((1, tile, cs), lambda s, i: (s + off, rev(i), 0))
    par = lambda off, r: pl.BlockSpec((1, r, cs), lambda s, i: (s + off, 0, 0))
    acc = lambda r, c: pl.BlockSpec((1, r, c), lambda s, i: (s, 0, 0), pipeline_mode=pl.Buffered(1))
    outs = pl.pallas_call(
        body, name="bwd_ffn_blocks", grid=(half, n_tiles),
        in_specs=[blk(0), blk(half), blk(0), blk(half), par(0, 3), par(half, 3),
                  acc(cs, D), pl.BlockSpec((tile, D), lambda s, i: (rev(i), 0)), pl.BlockSpec((D, tile), lambda s, i: (0, rev(i)))],
        out_specs=[blk(0), blk(0), acc(cs, D),
                   pl.BlockSpec((2, 1, D, cs), lambda s, i: (0, s, 0, 0), pipeline_mode=pl.Buffered(1)),
                   acc(1, cs), acc(1, cs), acc(3, cs), acc(3, cs)],
        out_shape=[jax.ShapeDtypeStruct((half, S, cs), BF16), jax.ShapeDtypeStruct((half, S, cs), BF16),
                   jax.ShapeDtypeStruct((half, cs, D), F32),
                   jax.ShapeDtypeStruct((2, half, D, cs), F32),
                   jax.ShapeDtypeStruct((half, 1, cs), F32), jax.ShapeDtypeStruct((half, 1, cs), F32),
                   jax.ShapeDtypeStruct((half, 3, cs), F32), jax.ShapeDtypeStruct((half, 3, cs), F32)],
        scratch_shapes=[pltpu.VMEM((2, HALO, cs), F32)],
        compiler_params=_params("arbitrary", "arbitrary"),
    )(gate_val, gate_val, upre, upre, conv_w_g, conv_w_g, w_down4, df, h2_t)
    dupre_g, dupre_v, d_wd, d_wu, dbg, dbv, dcwg, dcwv = outs
    return (dupre_g, dupre_v, d_wd, d_wu.reshape(nb, D, cs), jnp.concatenate([dbg, dbv], axis=0),
            jnp.concatenate([dcwg, dcwv], axis=0))


def _bwd_ffn_tokens(dupre_g, dupre_v, w_up_g, x2, dy, mix, g2, g3, tile):
    half, S, cs = dupre_g.shape
    D = x2.shape[1]

    def body(dg_ref, dv_ref, w_ref, x2_ref, dy_ref, mix_ref, g2_ref, g3_ref, dx2_ref, dmix_ref, dg3_ref, dg2_ref):
        _zero_when(pl.program_id(0) == 0, dg3_ref, dg2_ref)
        parts = [_dot_nt(dg_ref[d], w_ref[d]) for d in range(half)] + [_dot_nt(dv_ref[d], w_ref[d + half]) for d in range(half)]
        while len(parts) > 1:
            parts = [a + b for a, b in zip(parts[::2], parts[1::2])]
        dh2 = parts[0]
        x2 = x2_ref[...]
        r3 = _rms(x2)
        n3 = x2 * r3
        dg3_ref[...] += _colsum(dh2 * n3)
        dx2 = dy_ref[...] + _norm_bwd(dh2 * g3_ref[...], n3, r3)
        dx2_ref[...] = dx2
        mix = mix_ref[...]
        r2 = _rms(mix)
        n2 = mix * r2
        dg2_ref[...] += _colsum(dx2 * n2)
        dmix_ref[...] = _norm_bwd(dx2 * g2_ref[...], n2, r2).astype(BF16)

    row = lambda w: pl.BlockSpec((tile, w), lambda i: (i, 0))
    blk = pl.BlockSpec((half, tile, cs), lambda i: (0, i, 0))
    acc = pl.BlockSpec((1, D), lambda i: (0, 0))
    return pl.pallas_call(
        body, name="bwd_ffn_tokens", grid=(S // tile,),
        in_specs=[blk, blk, _const(w_up_g.shape), row(D), row(D), row(D), _const((1, D)), _const((1, D))],
        out_specs=[row(D), row(D), acc, acc],
        out_shape=[jax.ShapeDtypeStruct((S, D), F32), jax.ShapeDtypeStruct((S, D), BF16),
                   jax.ShapeDtypeStruct((1, D), F32), jax.ShapeDtypeStruct((1, D), F32)],
        compiler_params=_params("arbitrary"),
    )(dupre_g, dupre_v, w_up_g, x2, dy, mix, g2, g3)


def _bwd_outproj(dmix, w_out, pool_out, attn_out, pool_scale, attn_scale, tile):
    S, D = dmix.shape
    C = pool_out.shape[1]

    def body(dm_ref, w_ref, p_ref, a_ref, ps_ref, as_ref, dp_ref, da_ref, dw_ref, dps_ref, das_ref):
        _zero_when(pl.program_id(0) == 0, dw_ref, dps_ref, das_ref)
        dmx = dm_ref[...]
        dmerged = _dot_nt(dmx, w_ref[...])
        n_p, r_p, n_a, r_a = _normalized_heads(p_ref[...], a_ref[...])
        merged = jnp.concatenate([(n_p * ps_ref[...]).astype(BF16), (n_a * as_ref[...]).astype(BF16)], axis=1)
        dw_ref[...] += _dot_tn(merged, dmx)
        dm_p, dm_a = dmerged[:, :C], dmerged[:, C:]
        dps_ref[...] += _colsum(dm_p * n_p)
        das_ref[...] += _colsum(dm_a * n_a)
        dp_ref[...] = _norm_bwd(dm_p * ps_ref[...], n_p, r_p)
        da_ref[...] = _norm_bwd(dm_a * as_ref[...], n_a, r_a)

    row = lambda w: pl.BlockSpec((tile, w), lambda i: (i, 0))
    return pl.pallas_call(
        body, name="bwd_outproj", grid=(S // tile,),
        in_specs=[row(D), _const(w_out.shape), row(C), row(C), _const((1, C)), _const((1, C))],
        out_specs=[row(C), row(C), pl.BlockSpec(w_out.shape, lambda i: (0, 0)),
                   pl.BlockSpec((1, C), lambda i: (0, 0)), pl.BlockSpec((1, C), lambda i: (0, 0))],
        out_shape=[jax.ShapeDtypeStruct((S, C), F32), jax.ShapeDtypeStruct((S, C), F32),
                   jax.ShapeDtypeStruct(w_out.shape, F32), jax.ShapeDtypeStruct((1, C), F32), jax.ShapeDtypeStruct((1, C), F32)],
        compiler_params=_params("arbitrary"),
    )(dmix, w_out, pool_out, attn_out, pool_scale, attn_scale)


def _bwd_attn(qkv, d_attn, n_pairs, ex, subs):
    S = qkv.shape[0]
    n_steps = S // (subs * QB)

    def body(q_ref, k_ref, v_ref, do_ref, *rest):
        dq_ref, dk_ref, dv_ref = rest[ex.n:ex.n + 3]
        ex_refs = ex.split(rest[:ex.n] + rest[ex.n + 3:])
        first_step, last_step = _grid_ends((n_pairs, n_steps))

        @pl.when(first_step)
        def _():
            ex.start(*ex_refs)

        @pl.when(pl.program_id(1) == 0)
        def _():
            dk_ref[...] = jnp.zeros_like(dk_ref)
            dv_ref[...] = jnp.zeros_like(dv_ref)

        low_lanes = _low_lanes()
        after_s, from_s = _triangle(False, LOG_PIECES), _triangle(True, GRAD_PIECES)
        zero = jnp.zeros((QB, 1), F32)

        def tiles(qhs, dohs, totals, kws, vws, masks, cs, gs, scores=None):
            fw = _attn_weights(scores or _attn_scores(qhs, kws, masks), masks, cs, after_s)
            gvals = [t[2] * _dot_nt(doh, vw) for t, doh, vw in zip(fw, dohs, vws)]
            sums = [_suffix_sums(g, from_s, g0) for g, g0 in zip(gvals, gs)]
            totals = [tot if m is None else tot + sm[1] for tot, m, sm in zip(totals, masks, sums)]
            dzs = []
            for (z, e, _, _), g, (nearer, _), tot, m in zip(fw, gvals, sums, totals, masks):
                inv = 1.0 / (1.0 + e)
                sig_abs, sig_neg = inv, e * inv
                pos = z >= 0.0
                dz = g * jnp.where(pos, sig_neg, sig_abs) - jnp.where(pos, sig_abs, sig_neg) * (tot - nearer)
                if m is not None:
                    dz = jnp.where(m, dz, 0.0)
                dzs.append((dz * ATTN_SCALE).astype(BF16))
            dqs = [_dot(dz, kw) for dz, kw in zip(dzs, kws)]
            dks = [_dot_tn(dz, qh) for dz, qh in zip(dzs, qhs)]
            dvs = [_dot_tn(t[2].astype(BF16), doh) for t, doh in zip(fw, dohs)]
            return [(dq, dk, dv, t[3], sm[1], tot) for dq, dk, dv, t, sm, tot in zip(dqs, dks, dvs, fw, sums, totals)]

        def cond(c):
            return jnp.logical_and(c[0] >= 0, c[1] == 0)

        qhs, dohs, kws, vws, masks, first_blks, starts = [], [], [], [], [], [], []
        for sub in range(subs):
            i = pl.program_id(1) * subs + sub
            rows = slice(sub * QB, (sub + 1) * QB)
            first_blk, start, offset = _first_window(i)
            first_blks.append(first_blk)
            starts.append(start)
            qhs += _split_heads(q_ref[rows, :].astype(F32), low_lanes)
            dohs += _split_heads(do_ref[rows, :], low_lanes)
            kws += [k_ref[pl.ds(start, 2 * QB), :]] * 2
            vws += [v_ref[pl.ds(start, 2 * QB), :]] * 2
            masks += [_causal_mask(2 * QB, offset)] * 2
        zeros = [zero] * len(qhs)

        scores = _attn_scores(qhs, kws, masks)
        c_first = [_row_sums(sc[3], zero) for sc in scores]
        all_done = _all_done(c_first)

        def far_totals():
            beyond = []
            for sub in range(subs):
                pair = slice(2 * sub, 2 * sub + 2)

                def far_sums(c, qh=qhs[pair], doh=dohs[pair]):
                    j, _, c0, c1, r0, r1 = c
                    at = pl.multiple_of(j * QB, QB)
                    kb = k_ref[pl.ds(at, QB), :]
                    vb = v_ref[pl.ds(at, QB), :]
                    far = _attn_tiles(qh, [kb, kb], [None, None], [c0, c1], after_s)
                    r0 = r0 + jnp.sum(far[0][2] * _dot_nt(doh[0], vb), axis=1, keepdims=True)
                    r1 = r1 + jnp.sum(far[1][2] * _dot_nt(doh[1], vb), axis=1, keepdims=True)
                    return j - 1, _sweep_done(far[0][3], far[1][3]), far[0][3], far[1][3], r0, r1

                c0, c1 = c_first[pair]
                far = lax.while_loop(cond, far_sums, (first_blks[sub] - 1, _sweep_done(c0, c1), c0, c1, zero, zero))
                beyond += [far[4], far[5]]
            return tuple(beyond)

        beyond_first = list(lax.cond(all_done, lambda: tuple(zeros), far_totals))
        done = tiles(qhs, dohs, beyond_first, kws, vws, masks, zeros, zeros, scores)
        for sub in range(subs):
            dk_ref[pl.ds(starts[sub], 2 * QB), :] += done[2 * sub][1] + done[2 * sub + 1][1]
            dv_ref[pl.ds(starts[sub], 2 * QB), :] += done[2 * sub][2] + done[2 * sub + 1][2]
        first_dq = [jnp.where(low_lanes, done[2 * sub][0], done[2 * sub + 1][0]) for sub in range(subs)]

        def sweep_on():
            final = []
            for sub in range(subs):
                pair = slice(2 * sub, 2 * sub + 2)
                t0, t1 = done[pair]

                def step(c, qh=qhs[pair], doh=dohs[pair], total=[t0[5], t1[5]]):
                    j, _, dq, c0, c1, s0, s1 = c
                    at = pl.multiple_of(j * QB, QB)
                    kb = k_ref[pl.ds(at, QB), :]
                    vb = v_ref[pl.ds(at, QB), :]
                    f0, f1 = tiles(qh, doh, total, [kb, kb], [vb, vb], [None, None], [c0, c1], [s0, s1])
                    dk_ref[pl.ds(at, QB), :] += f0[1] + f1[1]
                    dv_ref[pl.ds(at, QB), :] += f0[2] + f1[2]
                    return j - 1, _sweep_done(f0[3], f1[3]), dq + jnp.where(low_lanes, f0[0], f1[0]), f0[3], f1[3], f0[4], f1[4]

                init = (first_blks[sub] - 1, _sweep_done(t0[3], t1[3]), first_dq[sub], t0[3], t1[3], t0[4], t1[4])
                final.append(lax.while_loop(cond, step, init)[2])
            return tuple(final)

        final = lax.cond(all_done, lambda: tuple(first_dq), sweep_on)
        for sub in range(subs):
            dq_ref[sub * QB:(sub + 1) * QB, :] = final[sub]

        @pl.when(last_step)
        def _():
            ex.wait(*ex_refs)

    blk = pl.BlockSpec((subs * QB, QB), lambda p, i: (i, p))
    full = lambda off: pl.BlockSpec((S, QB), lambda p, i: (0, off + p), pipeline_mode=pl.Buffered(1))
    outs = pl.pallas_call(
        body, name="bwd_attn", grid=(n_pairs, n_steps),
        in_specs=[blk, full(n_pairs), full(2 * n_pairs), blk] + ex.specs,
        out_specs=[blk, full(0), full(0)] + ex.specs,
        out_shape=[jax.ShapeDtypeStruct((S, n_pairs * QB), F32)] * 3 + ex.out_shape,
        scratch_shapes=ex.scratch,
        compiler_params=_params("arbitrary", "arbitrary"),
    )(qkv, qkv, qkv, d_attn, *ex.arrays)
    return outs[0], outs[1], outs[2], outs[3:]


def _bwd_pool_w_in(u, d_pool, w_pool, dq, dk, dv, h1_t, n_blocks, tile):
    S, C = u.shape
    D = h1_t.shape[0]
    n_tiles = S // tile
    ng = len(POOL_WINDOWS)
    cs = 4 * C // n_blocks
    per = C // cs

    def body(u_ref, uh_ref, d_ref, dh_ref, wp_ref, dq_ref, dk_ref, dv_ref, ht_ref, dproj_ref, dw_ref, dwp_ref):
        i = pl.program_id(0)
        first = i == 0
        _zero_when(first, dw_ref, dwp_ref)
        ht = ht_ref[...]
        for d in range(per, n_blocks):
            src = (dq_ref, dk_ref, dv_ref)[d // per - 1]
            dproj = src[:, (d % per) * cs:(d % per + 1) * cs].astype(BF16)
            dproj_ref[:, d * cs:(d + 1) * cs] = dproj
            dw_ref[d] += _dot(ht, dproj)
        halo = jnp.where(first, 0.0, uh_ref[...])
        parts = _pool_deviation(u_ref[...], halo, i * tile)
        dout = d_ref[...]
        nxt = jnp.where(i == n_tiles - 1, 0.0, dh_ref[...])
        dext = jnp.concatenate([dout, nxt], axis=0).astype(BF16)
        counts = _pool_counts(i * tile, tile + HALO)
        dps, scaled = [], []
        for g in range(ng):
            lanes = slice(g * POOL_GROUP, (g + 1) * POOL_GROUP)
            dp = _dot_nt(dext[:, lanes], wp_ref[g].astype(BF16))
            dps.append(dp[:tile])
            scaled.append(dp / counts[g])
        sums = _window_sums(jnp.concatenate(scaled, axis=1), forward=True)
        du = []
        for g, w in enumerate(POOL_WINDOWS):
            lanes = slice(g * POOL_GROUP, (g + 1) * POOL_GROUP)
            du.append((sums[w][:tile, lanes] - dps[g]).astype(BF16))
            dwp_ref[g] += _dot_tn(parts[g].astype(BF16), dext[:tile, lanes])
        du = jnp.concatenate(du, axis=1)
        for d in range(per):
            dproj = du[:, d * cs:(d + 1) * cs]
            dproj_ref[:, d * cs:(d + 1) * cs] = dproj
            dw_ref[d] += _dot(ht, dproj)

    row = pl.BlockSpec((tile, C), lambda i: (i, 0))
    return pl.pallas_call(
        body, name="bwd_pool_w_in", grid=(n_tiles,),
        in_specs=[row, _prev_halo_spec(tile, C), row, _next_halo_spec(tile, C, n_tiles), _const(w_pool.shape),
                  row, row, row, pl.BlockSpec((D, tile), lambda i: (0, i))],
        out_specs=[pl.BlockSpec((tile, 4 * C), lambda i: (i, 0)), pl.BlockSpec((n_blocks, D, cs), lambda i: (0, 0, 0)),
                   pl.BlockSpec(w_pool.shape, lambda i: (0, 0, 0))],
        out_shape=[jax.ShapeDtypeStruct((S, 4 * C), BF16), jax.ShapeDtypeStruct((n_blocks, D, cs), F32),
                   jax.ShapeDtypeStruct(w_pool.shape, F32)],
        compiler_params=_params("arbitrary"),
    )(u, u, d_pool, d_pool, w_pool, dq, dk, dv, h1_t)


def _bwd_x(dproj, w_in_t, x, dx2, g1, tile, ex):
    S, D = x.shape
    n_tiles = S // tile

    def body(dp_ref, w_ref, x_ref, dx2_ref, g_ref, *rest):
        dx_ref, dg_ref = rest[ex.n:ex.n + 2]
        ex_refs = ex.split(rest[:ex.n] + rest[ex.n + 2:])
        first, last = _grid_ends((n_tiles,))

        @pl.when(first)
        def _():
            ex.start(*ex_refs)
            dg_ref[...] = jnp.zeros_like(dg_ref)

        dh = _dot(dp_ref[...], w_ref[...])
        xf = x_ref[...]
        r1 = _rms(xf)
        n1 = xf * r1
        dg_ref[...] += _colsum(dh * n1)
        dx_ref[...] = dx2_ref[...] + _norm_bwd(dh * g_ref[...], n1, r1)

        @pl.when(last)
        def _():
            ex.wait(*ex_refs)

    row = lambda w: pl.BlockSpec((tile, w), lambda i: (i, 0))
    outs = pl.pallas_call(
        body, name="bwd_x", grid=(n_tiles,),
        in_specs=[row(w_in_t.shape[0]), _const(w_in_t.shape), row(D), row(D), _const((1, D))] + ex.specs,
        out_specs=[row(D), pl.BlockSpec((1, D), lambda i: (0, 0))] + ex.specs,
        out_shape=[jax.ShapeDtypeStruct((S, D), F32), jax.ShapeDtypeStruct((1, D), F32)] + ex.out_shape,
        scratch_shapes=ex.scratch,
        compiler_params=_params("arbitrary"),
    )(dproj, w_in_t, x, dx2, g1, *ex.arrays)
    return outs[0], outs[1], outs[2:]


def _mesh_position():
    x, y, c = lax.axis_index("x"), lax.axis_index("y"), lax.axis_index("c")
    return x, y, c, 4 * x + 2 * y + c


def _peer(x, y, c, k):
    px = 1 - x if k & 4 else x
    py = 1 - y if k & 2 else y
    pc = 1 - c if k & 1 else c
    return (px, py, pc), 4 * px + 2 * py + pc


class _Exchange:
    def __init__(self, arrays, gather):
        self.arrays, self.gather, self.n = list(arrays), gather, len(arrays)
        self.out_shape = [jax.ShapeDtypeStruct(((N_DEV,) + a.shape) if gather else a.shape, a.dtype) for a in arrays]
        self.specs = [pl.BlockSpec(memory_space=pl.ANY)] * self.n
        copies = self.n * (N_DEV - 1)
        self.scratch = [pltpu.SemaphoreType.DMA((copies,)), pltpu.SemaphoreType.DMA((copies,)),
                        pltpu.SemaphoreType.DMA((self.n,))]

    def _copies(self, ins, outs, sems):
        send_sems, recv_sems, local_sems = sems
        x, y, c, me = _mesh_position()
        local, remote = [], []
        for a in range(self.n):
            mine = ins[a] if self.gather else ins[a].at[me]
            local.append(pltpu.make_async_copy(mine, outs[a].at[me], local_sems.at[a]))
            for k in range(1, N_DEV):
                peer, peer_idx = _peer(x, y, c, k)
                src = ins[a] if self.gather else ins[a].at[peer_idx]
                sem = a * (N_DEV - 1) + k - 1
                remote.append(pltpu.make_async_remote_copy(
                    src_ref=src, dst_ref=outs[a].at[me], send_sem=send_sems.at[sem], recv_sem=recv_sems.at[sem],
                    device_id=peer, device_id_type=MESH))
        return local, remote

    def start(self, ins, outs, sems):
        local, remote = self._copies(ins, outs, sems)
        for cp in local + remote:
            cp.start()

    def wait(self, ins, outs, sems):
        local, remote = self._copies(ins, outs, sems)
        for cp in remote:
            cp.wait_send()
        for cp in remote:
            cp.wait_recv()
        for cp in local:
            cp.wait()

    def split(self, refs):
        return refs[:self.n], refs[self.n:2 * self.n], refs[2 * self.n:]


class _ChipGather(_Exchange):
    def __init__(self, arrays):
        super().__init__(arrays, gather=True)

    def _plan(self, ins, outs, sems, waiting):
        send_sems, recv_sems, local_sems = sems
        x, y, c, me = _mesh_position()
        sibling = (x, y, 1 - c)
        chips = [(1 - x, y), (x, 1 - y), (1 - x, 1 - y)]
        local, first, passed, arrivals = [], [], [], []
        for a in range(self.n):
            def copy(k, block, to, src=None, a=a):
                rows = outs[a].at[block]
                return pltpu.make_async_remote_copy(
                    src_ref=rows if src is None else src, dst_ref=rows, send_sem=send_sems.at[a * (N_DEV - 1) + k],
                    recv_sem=recv_sems.at[a * (N_DEV - 1) + k], device_id=to, device_id_type=MESH)

            local.append(pltpu.make_async_copy(ins[a], outs[a].at[me], local_sems.at[a]))
            first.append(copy(0, me, sibling, src=ins[a]))
            first += [copy(1 + j, me, (px, py, c), src=ins[a]) for j, (px, py) in enumerate(chips)]
            if waiting:
                passed.append([copy(4 + j, 4 * px + 2 * py + c, sibling) for j, (px, py) in enumerate(chips)])
                arrivals.append([copy(k, me, sibling) for k in range(N_DEV - 1)])
        return local, first, passed, arrivals

    def start(self, ins, outs, sems):
        local, first, _, _ = self._plan(ins, outs, sems, waiting=False)
        for cp in local + first:
            cp.start()

    def wait(self, ins, outs, sems):
        local, first, passed, arrivals = self._plan(ins, outs, sems, waiting=True)
        for a in range(self.n):
            for j in range(3):
                arrivals[a][1 + j].wait_recv()
                passed[a][j].start()
        for a in range(self.n):
            arrivals[a][0].wait_recv()
            for j in range(3):
                arrivals[a][4 + j].wait_recv()
        for cp in first + [cp for row in passed for cp in row]:
            cp.wait_send()
        for cp in local:
            cp.wait()


def _all_to_all(arrays, gather, name):
    ex = _ChipGather(arrays) if gather else _Exchange(arrays, gather)

    def body(*refs):
        ins, outs, sems = ex.split(refs)
        ex.start(ins, outs, sems)
        ex.wait(ins, outs, sems)

    return pl.pallas_call(body, name=name, in_specs=ex.specs, out_specs=ex.specs, out_shape=ex.out_shape,
                          scratch_shapes=ex.scratch)(*ex.arrays)


def _reduce_adamw(parts, w, m, v, rows):
    R, C = w.shape

    def body(p_ref, w_ref, m_ref, v_ref, g_ref, d_ref, nm_ref, nv_ref):
        g = p_ref[0].astype(F32)
        for s in range(1, N_DEV):
            g = g + p_ref[s].astype(F32)
        g_ref[...] = g
        m_new = ADAM_B1 * m_ref[...] + (1.0 - ADAM_B1) * g
        v_new = ADAM_B2 * v_ref[...] + (1.0 - ADAM_B2) * (g * g)
        m_hat = m_new / (1.0 - ADAM_B1 ** ADAM_STEP)
        v_hat = v_new / (1.0 - ADAM_B2 ** ADAM_STEP)
        d_ref[...] = -ADAM_LR * (m_hat / (jnp.sqrt(v_hat) + ADAM_EPS) + ADAM_WD * w_ref[...])
        nm_ref[...] = m_new
        nv_ref[...] = v_new

    row = pl.BlockSpec((rows, C), lambda i: (i, 0))
    return pl.pallas_call(
        body, name="reduce_adamw", grid=(R // rows,),
        in_specs=[pl.BlockSpec((N_DEV, rows, C), lambda i: (0, i, 0)), row, row, row],
        out_specs=[row] * 4, out_shape=[jax.ShapeDtypeStruct((R, C), F32)] * 4,
        compiler_params=_params("parallel"),
    )(parts, w, m, v)


def _row_tile(rows, cols):
    fits = [t for t in range(8, rows + 1, 8) if rows % t == 0 and N_DEV * t * cols * 4 <= 4 * 1024 * 1024]
    return max(fits) if fits else rows


SMALL_COLS = 1024


def _pack_small(vals):
    rows = []
    for a in vals:
        flat = a.reshape(-1)
        pad = (-flat.shape[0]) % SMALL_COLS
        rows.append(jnp.pad(flat, (0, pad)).reshape(-1, SMALL_COLS))
    packed = jnp.concatenate(rows, axis=0)
    return jnp.pad(packed, ((0, (-packed.shape[0]) % 8), (0, 0)))


def _unpack_small(packed, like):
    out, r = [], 0
    for a in like:
        n = a.size
        nr = -(-n // SMALL_COLS)
        out.append(packed[r:r + nr].reshape(-1)[:n].reshape(a.shape))
        r += nr
    return out


def kernel(x, norm_mix_pre, w_in, w_pool, pool_scale, attn_scale, w_out, norm_mix_post, norm_ffn_pre, w_up, conv_w, conv_b, w_down, norm_ffn_post, loss_target, m_norm_mix_pre, m_w_in, m_w_pool, m_pool_scale, m_attn_scale, m_w_out, m_norm_mix_post, m_norm_ffn_pre, m_w_up, m_conv_w, m_conv_b, m_w_down, m_norm_ffn_post, v_norm_mix_pre, v_w_in, v_w_pool, v_pool_scale, v_attn_scale, v_w_out, v_norm_mix_post, v_norm_ffn_pre, v_w_up, v_conv_w, v_conv_b, v_w_down, v_norm_ffn_post):
    S, D = x.shape[1], x.shape[2]
    d_ff_block = w_up.shape[2]

    xs, target = x[0], loss_target[0]
    g1, g2, g3, g4 = norm_mix_pre, norm_mix_post, norm_ffn_pre, norm_ffn_post
    big = min(512, S)
    small = min(256, S)
    n_pairs = pool_scale.shape[1] // QB
    conv_b_g = conv_b.reshape(N_DEV, 1, d_ff_block)

    (w_in_g,) = _all_to_all([w_in[0].astype(BF16)], gather=True, name="gather_w_in")
    h1_t, u, qkv = _fwd_inproj(xs, g1, w_in_g, big)
    pool_out = _fwd_pool(u, w_pool[0], big)
    attn_out, (w_out_g, w_up_g, w_down_g, conv_w_g) = _fwd_attn(
        qkv, n_pairs, _ChipGather([w_out[0].astype(BF16), w_up[0].astype(BF16), w_down[0].astype(BF16), conv_w[0]]),
        min(ATTN_FWD_BLOCKS, S // QB))
    w_out_full = w_out_g.reshape(D, D)
    w_down4 = w_down_g.reshape(D_FF_SHARDS, d_ff_block, D)
    mix, x2, h2, h2_t = _fwd_outproj(pool_out, attn_out, pool_scale, attn_scale, w_out_full, xs, g2, g3, big)
    upre, gate_val, dy, df, loss_cols, dg4 = _fwd_ffn_loss(h2, w_up_g, conv_w_g, conv_b_g, w_down4, x2, target, g4, small)
    loss = lax.psum(0.5 * jnp.sum(loss_cols) / D, ("x", "y", "c"))

    dupre_g, dupre_v, d_wd4, d_wup, d_cb, d_cw = _bwd_ffn_blocks(gate_val, upre, conv_w_g, w_down4, df, h2_t, min(1024, S))
    dx2, dmix, dg3, dg2 = _bwd_ffn_tokens(dupre_g, dupre_v, w_up_g, x2, dy, mix, g2, g3, big)
    d_pool, d_attn, d_wout, d_ps, d_as = _bwd_outproj(dmix, w_out_full, pool_out, attn_out, pool_scale, attn_scale, big)
    d_wdown_g = d_wd4.reshape(N_DEV, w_down.shape[1], D)
    d_wout_g = d_wout.reshape(N_DEV, D // N_DEV, D)
    dq, dk, dv, late_parts = _bwd_attn(qkv, d_attn, n_pairs, _Exchange([d_wout_g, d_wup, d_wdown_g, d_cw], gather=False),
                                       min(ATTN_BWD_BLOCKS, S // QB))
    dproj, d_win, d_wp = _bwd_pool_w_in(u, d_pool, w_pool[0], dq, dk, dv, h1_t, N_DEV, big)
    w_in_t = w_in_g.transpose(0, 2, 1).reshape(-1, D)
    dx, dg1, (win_parts,) = _bwd_x(dproj, w_in_t, xs, dx2, g1, big, _Exchange([d_win], gather=False))
    big_parts = [win_parts] + list(late_parts)
    r = dict(dx=dx, g1=dg1, w_pool=d_wp, pool_scale=d_ps, attn_scale=d_as, g2=dg2, g3=dg3, conv_b=d_cb, g4=dg4)

    small_names = ["norm_mix_pre", "w_pool", "pool_scale", "attn_scale", "norm_mix_post", "norm_ffn_pre", "conv_b", "norm_ffn_post"]
    small_w = dict(norm_mix_pre=norm_mix_pre, w_pool=w_pool, pool_scale=pool_scale, attn_scale=attn_scale,
                   norm_mix_post=norm_mix_post, norm_ffn_pre=norm_ffn_pre, conv_b=conv_b, norm_ffn_post=norm_ffn_post)
    small_m = dict(norm_mix_pre=m_norm_mix_pre, w_pool=m_w_pool, pool_scale=m_pool_scale, attn_scale=m_attn_scale,
                   norm_mix_post=m_norm_mix_post, norm_ffn_pre=m_norm_ffn_pre, conv_b=m_conv_b, norm_ffn_post=m_norm_ffn_post)
    small_v = dict(norm_mix_pre=v_norm_mix_pre, w_pool=v_w_pool, pool_scale=v_pool_scale, attn_scale=v_attn_scale,
                   norm_mix_post=v_norm_mix_post, norm_ffn_pre=v_norm_ffn_pre, conv_b=v_conv_b, norm_ffn_post=v_norm_ffn_post)
    small_g = dict(norm_mix_pre=r["g1"], w_pool=r["w_pool"], pool_scale=r["pool_scale"], attn_scale=r["attn_scale"],
                   norm_mix_post=r["g2"], norm_ffn_pre=r["g3"], conv_b=r["conv_b"], norm_ffn_post=r["g4"])
    like = [small_w[n] for n in small_names]
    packed_g = _pack_small([small_g[n] for n in small_names])

    (small_parts,) = _all_to_all([packed_g], gather=True, name="gather_small_grads")

    def update(parts, w, m, v):
        R, C = w.shape
        return _reduce_adamw(parts, w, m, v, _row_tile(R, C))

    res = {}
    res["w_in"] = update(big_parts[0], w_in[0], m_w_in[0], v_w_in[0])
    res["w_out"] = update(big_parts[1], w_out[0], m_w_out[0], v_w_out[0])
    res["w_up"] = update(big_parts[2], w_up[0], m_w_up[0], v_w_up[0])
    res["w_down"] = update(big_parts[3], w_down[0], m_w_down[0], v_w_down[0])
    res["conv_w"] = update(big_parts[4], conv_w[0], m_conv_w[0], v_conv_w[0])
    small_res = update(small_parts, _pack_small(like), _pack_small([small_m[n] for n in small_names]),
                       _pack_small([small_v[n] for n in small_names]))
    small_res = [_unpack_small(t, like) for t in small_res]
    for idx, n in enumerate(small_names):
        res[n] = tuple(t[idx] for t in small_res)

    order = ["norm_mix_pre", "w_in", "w_pool", "pool_scale", "attn_scale", "w_out", "norm_mix_post", "norm_ffn_pre",
             "w_up", "conv_w", "conv_b", "w_down", "norm_ffn_post"]
    shaped = {n: tuple(t.reshape(s.shape) for t in res[n])
              for n, s in dict(norm_mix_pre=norm_mix_pre, w_in=w_in, w_pool=w_pool, pool_scale=pool_scale, attn_scale=attn_scale,
                               w_out=w_out, norm_mix_post=norm_mix_post, norm_ffn_pre=norm_ffn_pre, w_up=w_up, conv_w=conv_w,
                               conv_b=conv_b, w_down=w_down, norm_ffn_post=norm_ffn_post).items()}
    outs = [loss, r["dx"].reshape(x.shape)]
    for k in range(4):
        outs += [shaped[n][k] for n in order]
    return tuple(outs)
```

```python
import functools

import jax
import jax.numpy as jnp
from jax import lax
from jax.experimental import pallas as pl
from jax.experimental.pallas import tpu as pltpu

F32 = jnp.float32
BF16 = jnp.bfloat16
HIGHEST = lax.Precision.HIGHEST

N_DEV = 8
EPS = 1e-6
POOL_WINDOWS = (2, 4, 8, 16)
POOL_GROUP = 128
HALO = 16
HEAD_DIM = 64
QB = 128
ATTN_SCALE = HEAD_DIM ** -0.5
ATTN_FWD_BLOCKS = 16
ATTN_BWD_BLOCKS = 8
EXP_UNDERFLOW = -88.0
D_FF_SHARDS = 4

ADAM_LR = 0.001
ADAM_B1 = 0.9
ADAM_B2 = 0.999
ADAM_EPS = 1e-08
ADAM_WD = 0.01
ADAM_STEP = 10

VMEM_LIMIT_V7X = 56 * 1024 * 1024
MESH = pl.DeviceIdType.MESH


def _params(*semantics):
    return pltpu.CompilerParams(dimension_semantics=semantics, vmem_limit_bytes=VMEM_LIMIT_V7X)


def _const(shape):
    zeros = (0,) * len(shape)
    return pl.BlockSpec(shape, lambda *_: zeros, pipeline_mode=pl.Buffered(1))


def _dot(a, b):
    return jnp.dot(a, b, preferred_element_type=F32)


def _dot_nt(a, b):
    return lax.dot_general(a, b, (((1,), (1,)), ((), ())), preferred_element_type=F32)


def _dot_tn(a, b):
    return lax.dot_general(a, b, (((0,), (0,)), ((), ())), preferred_element_type=F32)


def _rms(v):
    return lax.rsqrt(jnp.mean(v * v, axis=-1, keepdims=True) + EPS)


def _norm_bwd(dn_times_gain, n, r):
    return r * (dn_times_gain - n * jnp.mean(dn_times_gain * n, axis=-1, keepdims=True))


def _zero_when(first, *refs):
    @pl.when(first)
    def _():
        for ref in refs:
            ref[...] = jnp.zeros_like(ref)


def _colsum(v):
    return jnp.sum(v, axis=0, keepdims=True)


def _grid_ends(grid):
    ids = [pl.program_id(a) for a in range(len(grid))]
    first = functools.reduce(jnp.logical_and, [i == 0 for i in ids])
    last = functools.reduce(jnp.logical_and, [i == n - 1 for i, n in zip(ids, grid)])
    return first, last


def _fwd_inproj_pool(x, g1, w_in_g, w_pool, tile):
    S, D = x.shape
    nb, _, cs = w_in_g.shape
    d_pool = 2 * cs

    def body(x_ref, g_ref, w_ref, wp_ref, ht_ref, u_ref, qkv_ref, pool_ref, halo_ref):
        i = pl.program_id(0)
        _zero_when(i == 0, halo_ref)
        xf = x_ref[...]
        h = (xf * _rms(xf) * g_ref[...]).astype(BF16)
        ht_ref[...] = h.T
        u = jnp.concatenate([_dot(h, w_ref[0]), _dot(h, w_ref[1])], axis=1)
        u_ref[...] = u
        for d in range(2, nb):
            qkv_ref[:, (d - 2) * cs:(d - 1) * cs] = _dot(h, w_ref[d]).astype(BF16)
        parts = _pool_deviation(u, halo_ref[...], i * tile)
        halo_ref[...] = u[tile - HALO:, :]
        for g, p in enumerate(parts):
            pool_ref[:, g * POOL_GROUP:(g + 1) * POOL_GROUP] = _dot(p.astype(BF16), wp_ref[g].astype(BF16))

    row = lambda w: pl.BlockSpec((tile, w), lambda i: (i, 0))
    return pl.pallas_call(
        body, name="fwd_inproj_pool", grid=(S // tile,),
        in_specs=[row(D), _const((1, D)), _const(w_in_g.shape), _const(w_pool.shape)],
        out_specs=[pl.BlockSpec((D, tile), lambda i: (0, i)), row(d_pool), row(3 * d_pool), row(d_pool)],
        out_shape=[jax.ShapeDtypeStruct((D, S), BF16), jax.ShapeDtypeStruct((S, d_pool), F32),
                   jax.ShapeDtypeStruct((S, 3 * d_pool), BF16), jax.ShapeDtypeStruct((S, d_pool), F32)],
        scratch_shapes=[pltpu.VMEM((HALO, d_pool), F32)],
        compiler_params=_params("arbitrary"),
    )(x, g1, w_in_g, w_pool)


def _window_sums(ext, forward):
    n = ext.shape[0]
    sums, s, sh = {}, ext, 1
    while sh < POOL_WINDOWS[-1]:
        s = s + pltpu.roll(s, (n - sh) if forward else sh, axis=0)
        sh *= 2
        sums[sh] = s
    return sums


def _pool_counts(t0, rows):
    t1 = (lax.broadcasted_iota(jnp.int32, (rows, 1), 0) + t0 + 1).astype(F32)
    return [jnp.minimum(t1, float(w)) for w in POOL_WINDOWS]


def _pool_deviation(u, halo, t0):
    T = u.shape[0]
    sums = _window_sums(jnp.concatenate([halo, u], axis=0), forward=False)
    counts = _pool_counts(t0, T)
    parts = []
    for g, w in enumerate(POOL_WINDOWS):
        lanes = slice(g * POOL_GROUP, (g + 1) * POOL_GROUP)
        parts.append(sums[w][HALO:, lanes] / counts[g] - u[:, lanes])
    return parts


def _prev_halo_spec(tile, width):
    return pl.BlockSpec((HALO, width), lambda i: (jnp.maximum(i * (tile // HALO) - 1, 0), 0))


def _next_halo_spec(tile, width, n_tiles):
    last = n_tiles * (tile // HALO) - 1
    return pl.BlockSpec((HALO, width), lambda i: (jnp.minimum((i + 1) * (tile // HALO), last), 0))


def _low_lanes():
    return lax.broadcasted_iota(jnp.int32, (QB, 2 * HEAD_DIM), 1) < HEAD_DIM


LOG_PIECES = 2
GRAD_PIECES = 3


def _triangle(inclusive, pieces):
    row = lax.broadcasted_iota(jnp.int32, (pieces * QB, QB), 0) % QB
    col = lax.broadcasted_iota(jnp.int32, (pieces * QB, QB), 1)
    return ((row >= col) if inclusive else (row > col)).astype(BF16)


def _pieces(v, n):
    out, rest = [], v
    for _ in range(n - 1):
        piece = rest.astype(BF16)
        out.append(piece)
        rest = rest - piece.astype(F32)
    out.append(rest.astype(BF16))
    return jnp.concatenate(out, axis=1)


def _causal_mask(width, offset):
    row = lax.broadcasted_iota(jnp.int32, (QB, width), 0)
    col = lax.broadcasted_iota(jnp.int32, (QB, width), 1)
    return col < row + offset


def _row_sums(vals, carry):
    for b in reversed(range(vals.shape[1] // QB)):
        carry = carry + jnp.sum(vals[:, b * QB:(b + 1) * QB], axis=1, keepdims=True)
    return carry


def _suffix_sums(vals, tri, carry):
    n = vals.shape[1] // QB
    out, run = [None] * n, carry
    for b in reversed(range(n)):
        blk = vals[:, b * QB:(b + 1) * QB]
        out[b] = _dot(_pieces(blk, tri.shape[0] // QB), tri) + run
        run = run + jnp.sum(blk, axis=1, keepdims=True)
    return (out[0] if n == 1 else jnp.concatenate(out, axis=1)), run


def _attn_tiles(qhs, kws, masks, carries, after_s):
    return _attn_weights(_attn_scores(qhs, kws, masks), masks, carries, after_s)


def _attn_scores(qhs, kws, masks):
    zs = [_dot_nt(qh, kw) * ATTN_SCALE for qh, kw in zip(qhs, kws)]
    es = [jnp.exp(-jnp.abs(z)) for z in zs]
    softplus = [jnp.maximum(z, 0.0) + jnp.log(1.0 + e) for z, e in zip(zs, es)]
    log_1m_beta = [-sp if m is None else jnp.where(m, -sp, 0.0) for sp, m in zip(softplus, masks)]
    return list(zip(zs, es, softplus, log_1m_beta))


def _attn_weights(scores, masks, carries, after_s):
    sums = [_suffix_sums(l, after_s, c) for (_, _, _, l), c in zip(scores, carries)]
    weights = [jnp.exp(z - sp + st) for (z, _, sp, _), (st, _) in zip(scores, sums)]
    weights = [a if m is None else jnp.where(m, a, 0.0) for a, m in zip(weights, masks)]
    return [(z, e, a, c) for (z, e, _, _), a, (_, c) in zip(scores, weights, sums)]


def _split_heads(v, low_lanes):
    return jnp.where(low_lanes, v, 0.0).astype(BF16), jnp.where(low_lanes, 0.0, v).astype(BF16)


def _sweep_done(c0, c1):
    return (jnp.maximum(jnp.max(c0), jnp.max(c1)) < EXP_UNDERFLOW).astype(jnp.int32)


def _all_done(carries):
    return jnp.max(functools.reduce(jnp.maximum, carries)) < EXP_UNDERFLOW


def _first_window(i):
    first_blk = jnp.maximum(i - 1, 0)
    return first_blk, pl.multiple_of(first_blk * QB, QB), (i - first_blk) * QB


def _fwd_attn(qkv, n_pairs, ex, subs):
    S = qkv.shape[0]
    n_steps = S // (subs * QB)

    def body(q_ref, k_ref, v_ref, *rest):
        o_ref = rest[ex.n]
        ex_refs = ex.split(rest[:ex.n] + rest[ex.n + 1:])
        first_step, last_step = _grid_ends((n_pairs, n_steps))

        @pl.when(first_step)
        def _():
            ex.start(*ex_refs)

        low_lanes = _low_lanes()
        after_s = _triangle(False, LOG_PIECES)
        zero = jnp.zeros((QB, 1), F32)

        def cond(c):
            return jnp.logical_and(c[0] >= 0, c[1] == 0)

        qhs, kws, vws, masks, first_blks = [], [], [], [], []
        for sub in range(subs):
            i = pl.program_id(1) * subs + sub
            first_blk, start, offset = _first_window(i)
            first_blks.append(first_blk)
            qhs += _split_heads(q_ref[sub * QB:(sub + 1) * QB, :].astype(F32), low_lanes)
            kws += [k_ref[pl.ds(start, 2 * QB), :]] * 2
            vws += [v_ref[pl.ds(start, 2 * QB), :]] * 2
            masks += [_causal_mask(2 * QB, offset)] * 2
        tiles = _attn_tiles(qhs, kws, masks, [zero] * len(qhs), after_s)
        outs = [_dot(t[2].astype(BF16), vw) for t, vw in zip(tiles, vws)]

        first_out = [jnp.where(low_lanes, outs[2 * sub], outs[2 * sub + 1]) for sub in range(subs)]

        def sweep_on():
            final = []
            for sub in range(subs):
                def step(c, qh=qhs[2 * sub:2 * sub + 2]):
                    j, _, acc, c0, c1 = c
                    at = pl.multiple_of(j * QB, QB)
                    kb = k_ref[pl.ds(at, QB), :]
                    vb = v_ref[pl.ds(at, QB), :]
                    far = _attn_tiles(qh, [kb, kb], [None, None], [c0, c1], after_s)
                    acc = acc + jnp.where(low_lanes, _dot(far[0][2].astype(BF16), vb), _dot(far[1][2].astype(BF16), vb))
                    return j - 1, _sweep_done(far[0][3], far[1][3]), acc, far[0][3], far[1][3]

                c0, c1 = tiles[2 * sub][3], tiles[2 * sub + 1][3]
                final.append(lax.while_loop(cond, step, (first_blks[sub] - 1, _sweep_done(c0, c1), first_out[sub], c0, c1))[2])
            return tuple(final)

        final = lax.cond(_all_done([t[3] for t in tiles]), lambda: tuple(first_out), sweep_on)
        for sub in range(subs):
            o_ref[sub * QB:(sub + 1) * QB, :] = final[sub]

        @pl.when(last_step)
        def _():
            ex.wait(*ex_refs)

    outs = pl.pallas_call(
        body, name="fwd_attn", grid=(n_pairs, n_steps),
        in_specs=[pl.BlockSpec((subs * QB, QB), lambda p, i: (i, p)),
                  pl.BlockSpec((S, QB), lambda p, i: (0, n_pairs + p), pipeline_mode=pl.Buffered(1)),
                  pl.BlockSpec((S, QB), lambda p, i: (0, 2 * n_pairs + p), pipeline_mode=pl.Buffered(1))] + ex.specs,
        out_specs=[pl.BlockSpec((subs * QB, QB), lambda p, i: (i, p))] + ex.specs,
        out_shape=[jax.ShapeDtypeStruct((S, n_pairs * QB), F32)] + ex.out_shape,
        scratch_shapes=ex.scratch,
        compiler_params=_params("arbitrary", "arbitrary"),
    )(qkv, qkv, qkv, *ex.arrays)
    return outs[0], outs[1:]


def _normalized_heads(pool_out, attn_out):
    rp, ra = _rms(pool_out), _rms(attn_out)
    return pool_out * rp, rp, attn_out * ra, ra


def _fwd_outproj(pool_out, attn_out, pool_scale, attn_scale, w_out, x, g2, g3, tile):
    S, D = x.shape
    C = pool_out.shape[1]

    def body(p_ref, a_ref, ps_ref, as_ref, w_ref, x_ref, g2_ref, g3_ref, mix_ref, x2_ref, h2_ref, h2t_ref):
        n_p, _, n_a, _ = _normalized_heads(p_ref[...], a_ref[...])
        mix = _dot((n_p * ps_ref[...]).astype(BF16), w_ref[:C, :]) + _dot((n_a * as_ref[...]).astype(BF16), w_ref[C:, :])
        mix_ref[...] = mix
        x2 = x_ref[...] + mix * _rms(mix) * g2_ref[...]
        x2_ref[...] = x2
        h2 = (x2 * _rms(x2) * g3_ref[...]).astype(BF16)
        h2_ref[...] = h2
        h2t_ref[...] = h2.T

    row = lambda w: pl.BlockSpec((tile, w), lambda i: (i, 0))
    return pl.pallas_call(
        body, name="fwd_outproj", grid=(S // tile,),
        in_specs=[row(C), row(C), _const((1, C)), _const((1, C)), _const(w_out.shape), row(D), _const((1, D)), _const((1, D))],
        out_specs=[row(D), row(D), row(D), pl.BlockSpec((D, tile), lambda i: (0, i))],
        out_shape=[jax.ShapeDtypeStruct((S, D), F32), jax.ShapeDtypeStruct((S, D), F32), jax.ShapeDtypeStruct((S, D), BF16),
                   jax.ShapeDtypeStruct((D, S), BF16)],
        compiler_params=_params("parallel"),
    )(pool_out, attn_out, pool_scale, attn_scale, w_out, x, g2, g3)


def _conv_taps(tile_rows, halo_rows):
    T = tile_rows.shape[0]
    ext = jnp.concatenate([halo_rows.astype(F32), tile_rows.astype(F32)], axis=0)
    return pltpu.roll(ext, 2, axis=0)[HALO:], pltpu.roll(ext, 1, axis=0)[HALO:], ext[HALO:]


def _tap_rows(cw_ref, d):
    return [cw_ref[d, k:k + 1, :] for k in range(3)]


def _gated_unit(taps_gate, taps_val, cw_gate, cw_val, cb_gate, cb_val):
    gate = cw_gate[0] * taps_gate[0] + cw_gate[1] * taps_gate[1] + cw_gate[2] * taps_gate[2] + cb_gate
    val = cw_val[0] * taps_val[0] + cw_val[1] * taps_val[1] + cw_val[2] * taps_val[2] + cb_val
    sig = 1.0 / (1.0 + jnp.exp(-gate))
    return gate, val, sig


def _fwd_ffn_loss(h2, w_up_g, conv_w_g, conv_b_g, w_down4, x2, target, g4, tile):
    S, D = x2.shape
    nb, _, cs = w_up_g.shape
    half = D_FF_SHARDS

    def body(h_ref, w_ref, cw_ref, cb_ref, wd_ref, x2_ref, t_ref, g4_ref, upre_ref, gv_ref, dy_ref, df_ref, loss_ref, dg4_ref, halo_ref):
        _zero_when(pl.program_id(0) == 0, loss_ref, dg4_ref, halo_ref)
        h = h_ref[...]

        def up(s):
            return _dot(h, w_ref[s]), _dot(h, w_ref[s + half])

        f = jnp.zeros((tile, D), F32)
        ahead = up(0)
        for s in range(half):
            ug, uv = ahead
            if s + 1 < half:
                ahead = up(s + 1)
            upre_ref[s] = ug.astype(BF16)
            upre_ref[s + half] = uv.astype(BF16)
            gate, val, sig = _gated_unit(_conv_taps(ug, halo_ref[s]), _conv_taps(uv, halo_ref[s + half]),
                                         _tap_rows(cw_ref, s), _tap_rows(cw_ref, s + half), cb_ref[s], cb_ref[s + half])
            halo_ref[s] = ug[tile - HALO:, :]
            halo_ref[s + half] = uv[tile - HALO:, :]
            gv_ref[s] = gate.astype(BF16)
            gv_ref[s + half] = val.astype(BF16)
            f = f + _dot((gate * sig * val).astype(BF16), wd_ref[s])
        r4 = _rms(f)
        n4 = f * r4
        err = x2_ref[...] + n4 * g4_ref[...] - t_ref[...]
        dy = err * (1.0 / D)
        dy_ref[...] = dy
        df_ref[...] = _norm_bwd(dy * g4_ref[...], n4, r4).astype(BF16)
        loss_ref[...] += _colsum(err * err)
        dg4_ref[...] += _colsum(dy * n4)

    row = lambda w: pl.BlockSpec((tile, w), lambda i: (i, 0))
    return pl.pallas_call(
        body, name="fwd_ffn_loss", grid=(S // tile,),
        in_specs=[row(D), _const(w_up_g.shape), _const(conv_w_g.shape), _const(conv_b_g.shape), _const(w_down4.shape),
                  row(D), row(D), _const((1, D))],
        out_specs=[pl.BlockSpec((nb, tile, cs), lambda i: (0, i, 0)), pl.BlockSpec((nb, tile, cs), lambda i: (0, i, 0)), row(D), row(D),
                   pl.BlockSpec((1, D), lambda i: (0, 0)), pl.BlockSpec((1, D), lambda i: (0, 0))],
        out_shape=[jax.ShapeDtypeStruct((nb, S, cs), BF16), jax.ShapeDtypeStruct((nb, S, cs), BF16),
                   jax.ShapeDtypeStruct((S, D), F32), jax.ShapeDtypeStruct((S, D), BF16),
                   jax.ShapeDtypeStruct((1, D), F32), jax.ShapeDtypeStruct((1, D), F32)],
        scratch_shapes=[pltpu.VMEM((nb, HALO, cs), F32)],
        compiler_params=_params("arbitrary"),
    )(h2, w_up_g, conv_w_g, conv_b_g, w_down4, x2, target, g4)


def _bwd_down(upre, conv_w_g, conv_b_g, w_down4, df, tile):
    nb, S, cs = upre.shape
    D = df.shape[1]
    n_tiles = S // tile

    def body(ug_ref, uv_ref, hg_ref, hv_ref, cwg_ref, cwv_ref, cbg_ref, cbv_ref, wd_ref, df_ref,
             dg_ref, dv_ref, dwd_ref, dbg_ref, dbv_ref, dcwg_ref, dcwv_ref):
        i = pl.program_id(1)
        first = i == 0
        _zero_when(first, dwd_ref, dbg_ref, dbv_ref, dcwg_ref, dcwv_ref)
        halo_g = jnp.where(first, jnp.zeros_like(hg_ref[0]), hg_ref[0])
        halo_v = jnp.where(first, jnp.zeros_like(hv_ref[0]), hv_ref[0])
        taps_g, taps_v = _conv_taps(ug_ref[0], halo_g), _conv_taps(uv_ref[0], halo_v)
        gate, val, sig = _gated_unit(taps_g, taps_v, _tap_rows(cwg_ref, 0), _tap_rows(cwv_ref, 0), cbg_ref[0], cbv_ref[0])
        silu = gate * sig
        dfb = df_ref[...]
        dact = _dot_nt(dfb, wd_ref[0])
        dwd_ref[0] += _dot_tn((silu * val).astype(BF16), dfb)
        dgate = dact * val * (sig * (1.0 + gate * (1.0 - sig)))
        dval = dact * silu
        dg_ref[0] = dgate.astype(BF16)
        dv_ref[0] = dval.astype(BF16)
        dbg_ref[0] += _colsum(dgate)
        dbv_ref[0] += _colsum(dval)
        for k in range(3):
            dcwg_ref[0, k:k + 1, :] += _colsum(dgate * taps_g[k])
            dcwv_ref[0, k:k + 1, :] += _colsum(dval * taps_v[k])

    half = D_FF_SHARDS
    blk = lambda off: pl.BlockSpec((1, tile, cs), lambda s, i: (s + off, i, 0))
    halo = lambda off: pl.BlockSpec((1, HALO, cs), lambda s, i: (s + off, jnp.maximum(i * (tile // HALO) - 1, 0), 0))
    par = lambda off, r: pl.BlockSpec((1, r, cs), lambda s, i: (s + off, 0, 0))
    outs = pl.pallas_call(
        body, name="bwd_down", grid=(half, n_tiles),
        in_specs=[blk(0), blk(half), halo(0), halo(half), par(0, 3), par(half, 3), par(0, 1), par(half, 1),
                  pl.BlockSpec((1, cs, D), lambda s, i: (s, 0, 0)), pl.BlockSpec((tile, D), lambda s, i: (i, 0))],
        out_specs=[blk(0), blk(0), pl.BlockSpec((1, cs, D), lambda s, i: (s, 0, 0)),
                   par(0, 1), par(0, 1), par(0, 3), par(0, 3)],
        out_shape=[jax.ShapeDtypeStruct((half, S, cs), BF16), jax.ShapeDtypeStruct((half, S, cs), BF16),
                   jax.ShapeDtypeStruct((half, cs, D), F32),
                   jax.ShapeDtypeStruct((half, 1, cs), F32), jax.ShapeDtypeStruct((half, 1, cs), F32),
                   jax.ShapeDtypeStruct((half, 3, cs), F32), jax.ShapeDtypeStruct((half, 3, cs), F32)],
        compiler_params=_params("parallel", "arbitrary"),
    )(upre, upre, upre, upre, conv_w_g, conv_w_g, conv_b_g, conv_b_g, w_down4, df)
    dgate, dval, d_wd, dbg, dbv, dcwg, dcwv = outs
    return dgate, dval, d_wd, jnp.concatenate([dbg, dbv], axis=0), jnp.concatenate([dcwg, dcwv], axis=0)


def _bwd_up_x(dgate, dval, conv_w_g, w_up_g, x2, dy, mix, g2, g3, tile):
    half, S, cs = dgate.shape
    nb = 2 * half
    D = x2.shape[1]
    n_tiles = S // tile

    def body(dg_ref, dv_ref, hg_ref, hv_ref, cw_ref, w_ref, x2_ref, dy_ref, mix_ref, g2_ref, g3_ref,
             dupre_ref, dx2_ref, dmix_ref, dg3_ref, dg2_ref):
        i = pl.program_id(0)
        last = i == n_tiles - 1
        _zero_when(i == 0, dg3_ref, dg2_ref)
        dh2 = jnp.zeros((tile, D), F32)
        for d in range(nb):
            src, halo = (dg_ref, hg_ref) if d < half else (dv_ref, hv_ref)
            nxt = jnp.where(last, jnp.zeros_like(halo[d % half]), halo[d % half])
            ext = jnp.concatenate([src[d % half].astype(F32), nxt.astype(F32)], axis=0)
            n = ext.shape[0]
            cw = _tap_rows(cw_ref, d)
            dupre = (cw[2] * ext + cw[1] * pltpu.roll(ext, n - 1, axis=0) + cw[0] * pltpu.roll(ext, n - 2, axis=0))[:tile]
            dupre = dupre.astype(BF16)
            dupre_ref[d] = dupre
            dh2 = dh2 + _dot_nt(dupre, w_ref[d])
        x2 = x2_ref[...]
        r3 = _rms(x2)
        n3 = x2 * r3
        dg3_ref[...] += _colsum(dh2 * n3)
        dx2 = dy_ref[...] + _norm_bwd(dh2 * g3_ref[...], n3, r3)
        dx2_ref[...] = dx2
        mix = mix_ref[...]
        r2 = _rms(mix)
        n2 = mix * r2
        dg2_ref[...] += _colsum(dx2 * n2)
        dmix_ref[...] = _norm_bwd(dx2 * g2_ref[...], n2, r2).astype(BF16)

    row = lambda w: pl.BlockSpec((tile, w), lambda i: (i, 0))
    blk = pl.BlockSpec((half, tile, cs), lambda i: (0, i, 0))
    last_halo = n_tiles * (tile // HALO) - 1
    halo = pl.BlockSpec((half, HALO, cs), lambda i: (0, jnp.minimum((i + 1) * (tile // HALO), last_halo), 0))
    acc = pl.BlockSpec((1, D), lambda i: (0, 0))
    return pl.pallas_call(
        body, name="bwd_up_x", grid=(n_tiles,),
        in_specs=[blk, blk, halo, halo, _const(conv_w_g.shape), _const(w_up_g.shape), row(D), row(D), row(D),
                  _const((1, D)), _const((1, D))],
        out_specs=[pl.BlockSpec((nb, tile, cs), lambda i: (0, i, 0)), row(D), row(D), acc, acc],
        out_shape=[jax.ShapeDtypeStruct((nb, S, cs), BF16), jax.ShapeDtypeStruct((S, D), F32),
                   jax.ShapeDtypeStruct((S, D), BF16), jax.ShapeDtypeStruct((1, D), F32), jax.ShapeDtypeStruct((1, D), F32)],
        compiler_params=_params("arbitrary"),
    )(dgate, dval, dgate, dval, conv_w_g, w_up_g, x2, dy, mix, g2, g3)


def _bwd_weight(act_t, dout, tile):
    D, S = act_t.shape
    nb, _, cs = dout.shape

    def body(a_ref, d_ref, o_ref):
        _zero_when(pl.program_id(1) == 0, o_ref)
        o_ref[0] += _dot(a_ref[...], d_ref[0])

    return pl.pallas_call(
        body, name="bwd_w_up", grid=(nb, S // tile),
        in_specs=[pl.BlockSpec((D, tile), lambda d, i: (0, i)), pl.BlockSpec((1, tile, cs), lambda d, i: (d, i, 0))],
        out_specs=pl.BlockSpec((1, D, cs), lambda d, i: (d, 0, 0)),
        out_shape=jax.ShapeDtypeStruct((nb, D, cs), F32),
        compiler_params=_params("parallel", "arbitrary"),
    )(act_t, dout)


def _bwd_ffn_blocks(gate_val, upre, conv_w_g, w_down4, df, h2_t, tile):
    nb, S, cs = upre.shape
    D = df.shape[1]
    n_tiles = S // tile
    half = D_FF_SHARDS

    def body(g_ref, v_ref, ug_ref, uv_ref, cwg_ref, cwv_ref, wd_ref, df_ref, ht_ref,
             dug_ref, duv_ref, dwd_ref, dwg_ref, dwv_ref, dbg_ref, dbv_ref, dcwg_ref, dcwv_ref, next_ref):
        _zero_when(pl.program_id(1) == 0, dwd_ref, dwg_ref, dwv_ref, dbg_ref, dbv_ref, dcwg_ref, dcwv_ref, next_ref)
        dfb = df_ref[...]
        dact = _dot_nt(dfb, wd_ref[0])
        gate, val = g_ref[0].astype(F32), v_ref[0].astype(F32)
        sig = 1.0 / (1.0 + jnp.exp(-gate))
        silu = gate * sig
        dwd_ref[0] += _dot_tn((silu * val).astype(BF16), dfb)
        ht = ht_ref[...]

        def through_conv(dup, slot, cw_ref, u_ref, du_ref, dw_ref, db_ref, dcw_ref):
            ext = jnp.concatenate([dup, next_ref[slot]], axis=0)
            n = ext.shape[0]
            shifted = (dup, pltpu.roll(ext, n - 1, axis=0)[:tile], pltpu.roll(ext, n - 2, axis=0)[:tile])
            next_ref[slot] = dup[:HALO]
            cw = _tap_rows(cw_ref, 0)
            dupre = (cw[2] * shifted[0] + cw[1] * shifted[1] + cw[0] * shifted[2]).astype(BF16)
            du_ref[0] = dupre
            dw_ref[0] += _dot(ht, dupre)
            u = u_ref[0].astype(F32)
            db_ref[0] += _colsum(dup)
            for k in range(3):
                dcw_ref[0, k:k + 1, :] += _colsum(shifted[2 - k] * u)

        through_conv(dact * val * (sig * (1.0 + gate * (1.0 - sig))), 0, cwg_ref, ug_ref, dug_ref, dwg_ref, dbg_ref, dcwg_ref)
        through_conv(dact * silu, 1, cwv_ref, uv_ref, duv_ref, dwv_ref, dbv_ref, dcwv_ref)

    rev = lambda i: n_tiles - 1 - i
    blk = lambda off: pl.BlockSpec((1, tile, cs), lambda s, i: (s + off, rev(i), 0))
    par = lambda off, r: pl.BlockSpec((1, r, cs), lambda s, i: (s + off, 0, 0))
    acc = lambda r, c: pl.BlockSpec((1, r, c), lambda s, i: (s, 0, 0), pipeline_mode=pl.Buffered(1))
    outs = pl.pallas_call(
        body, name="bwd_ffn_blocks", grid=(half, n_tiles),
        in_specs=[blk(0), blk(half), blk(0), blk(half), par(0, 3), par(half, 3),
                  acc(cs, D), pl.BlockSpec((tile, D), lambda s, i: (rev(i), 0)), pl.BlockSpec((D, tile), lambda s, i: (0, rev(i)))],
        out_specs=[blk(0), blk(0), acc(cs, D), acc(D, cs), acc(D, cs), acc(1, cs), acc(1, cs), acc(3, cs), acc(3, cs)],
        out_shape=[jax.ShapeDtypeStruct((half, S, cs), BF16), jax.ShapeDtypeStruct((half, S, cs), BF16),
                   jax.ShapeDtypeStruct((half, cs, D), F32),
                   jax.ShapeDtypeStruct((half, D, cs), F32), jax.ShapeDtypeStruct((half, D, cs), F32),
                   jax.ShapeDtypeStruct((half, 1, cs), F32), jax.ShapeDtypeStruct((half, 1, cs), F32),
                   jax.ShapeDtypeStruct((half, 3, cs), F32), jax.ShapeDtypeStruct((half, 3, cs), F32)],
        scratch_shapes=[pltpu.VMEM((2, HALO, cs), F32)],
        compiler_params=_params("arbitrary", "arbitrary"),
    )(gate_val, gate_val, upre, upre, conv_w_g, conv_w_g, w_down4, df, h2_t)
    dupre_g, dupre_v, d_wd, d_wg, d_wv, dbg, dbv, dcwg, dcwv = outs
    return (dupre_g, dupre_v, d_wd, jnp.concatenate([d_wg, d_wv], axis=0), jnp.concatenate([dbg, dbv], axis=0),
            jnp.concatenate([dcwg, dcwv], axis=0))


def _bwd_ffn_tokens(dupre_g, dupre_v, w_up_g, x2, dy, mix, g2, g3, tile):
    half, S, cs = dupre_g.shape
    D = x2.shape[1]

    def body(dg_ref, dv_ref, w_ref, x2_ref, dy_ref, mix_ref, g2_ref, g3_ref, dx2_ref, dmix_ref, dg3_ref, dg2_ref):
        _zero_when(pl.program_id(0) == 0, dg3_ref, dg2_ref)
        parts = [_dot_nt(dg_ref[d], w_ref[d]) for d in range(half)] + [_dot_nt(dv_ref[d], w_ref[d + half]) for d in range(half)]
        while len(parts) > 1:
            parts = [a + b for a, b in zip(parts[::2], parts[1::2])]
        dh2 = parts[0]
        x2 = x2_ref[...]
        r3 = _rms(x2)
        n3 = x2 * r3
        dg3_ref[...] += _colsum(dh2 * n3)
        dx2 = dy_ref[...] + _norm_bwd(dh2 * g3_ref[...], n3, r3)
        dx2_ref[...] = dx2
        mix = mix_ref[...]
        r2 = _rms(mix)
        n2 = mix * r2
        dg2_ref[...] += _colsum(dx2 * n2)
        dmix_ref[...] = _norm_bwd(dx2 * g2_ref[...], n2, r2).astype(BF16)

    row = lambda w: pl.BlockSpec((tile, w), lambda i: (i, 0))
    blk = pl.BlockSpec((half, tile, cs), lambda i: (0, i, 0))
    acc = pl.BlockSpec((1, D), lambda i: (0, 0))
    return pl.pallas_call(
        body, name="bwd_ffn_tokens", grid=(S // tile,),
        in_specs=[blk, blk, _const(w_up_g.shape), row(D), row(D), row(D), _const((1, D)), _const((1, D))],
        out_specs=[row(D), row(D), acc, acc],
        out_shape=[jax.ShapeDtypeStruct((S, D), F32), jax.ShapeDtypeStruct((S, D), BF16),
                   jax.ShapeDtypeStruct((1, D), F32), jax.ShapeDtypeStruct((1, D), F32)],
        compiler_params=_params("arbitrary"),
    )(dupre_g, dupre_v, w_up_g, x2, dy, mix, g2, g3)


def _bwd_outproj(dmix, w_out, pool_out, attn_out, pool_scale, attn_scale, tile):
    S, D = dmix.shape
    C = pool_out.shape[1]

    def body(dm_ref, w_ref, p_ref, a_ref, ps_ref, as_ref, dp_ref, da_ref, dw_ref, dps_ref, das_ref):
        _zero_when(pl.program_id(0) == 0, dw_ref, dps_ref, das_ref)
        dmx = dm_ref[...]
        dmerged = _dot_nt(dmx, w_ref[...])
        n_p, r_p, n_a, r_a = _normalized_heads(p_ref[...], a_ref[...])
        merged = jnp.concatenate([(n_p * ps_ref[...]).astype(BF16), (n_a * as_ref[...]).astype(BF16)], axis=1)
        dw_ref[...] += _dot_tn(merged, dmx)
        dm_p, dm_a = dmerged[:, :C], dmerged[:, C:]
        dps_ref[...] += _colsum(dm_p * n_p)
        das_ref[...] += _colsum(dm_a * n_a)
        dp_ref[...] = _norm_bwd(dm_p * ps_ref[...], n_p, r_p)
        da_ref[...] = _norm_bwd(dm_a * as_ref[...], n_a, r_a)

    row = lambda w: pl.BlockSpec((tile, w), lambda i: (i, 0))
    return pl.pallas_call(
        body, name="bwd_outproj", grid=(S // tile,),
        in_specs=[row(D), _const(w_out.shape), row(C), row(C), _const((1, C)), _const((1, C))],
        out_specs=[row(C), row(C), pl.BlockSpec(w_out.shape, lambda i: (0, 0)),
                   pl.BlockSpec((1, C), lambda i: (0, 0)), pl.BlockSpec((1, C), lambda i: (0, 0))],
        out_shape=[jax.ShapeDtypeStruct((S, C), F32), jax.ShapeDtypeStruct((S, C), F32),
                   jax.ShapeDtypeStruct(w_out.shape, F32), jax.ShapeDtypeStruct((1, C), F32), jax.ShapeDtypeStruct((1, C), F32)],
        compiler_params=_params("arbitrary"),
    )(dmix, w_out, pool_out, attn_out, pool_scale, attn_scale)


def _bwd_attn(qkv, d_attn, n_pairs, ex, subs):
    S = qkv.shape[0]
    n_steps = S // (subs * QB)

    def body(q_ref, k_ref, v_ref, do_ref, *rest):
        dq_ref, dk_ref, dv_ref = rest[ex.n:ex.n + 3]
        ex_refs = ex.split(rest[:ex.n] + rest[ex.n + 3:])
        first_step, last_step = _grid_ends((n_pairs, n_steps))

        @pl.when(first_step)
        def _():
            ex.start(*ex_refs)

        @pl.when(pl.program_id(1) == 0)
        def _():
            dk_ref[...] = jnp.zeros_like(dk_ref)
            dv_ref[...] = jnp.zeros_like(dv_ref)

        low_lanes = _low_lanes()
        after_s, from_s = _triangle(False, LOG_PIECES), _triangle(True, GRAD_PIECES)
        zero = jnp.zeros((QB, 1), F32)

        def tiles(qhs, dohs, totals, kws, vws, masks, cs, gs, scores=None):
            fw = _attn_weights(scores or _attn_scores(qhs, kws, masks), masks, cs, after_s)
            gvals = [t[2] * _dot_nt(doh, vw) for t, doh, vw in zip(fw, dohs, vws)]
            sums = [_suffix_sums(g, from_s, g0) for g, g0 in zip(gvals, gs)]
            totals = [tot if m is None else tot + sm[1] for tot, m, sm in zip(totals, masks, sums)]
            dzs = []
            for (z, e, _, _), g, (nearer, _), tot, m in zip(fw, gvals, sums, totals, masks):
                inv = 1.0 / (1.0 + e)
                sig_abs, sig_neg = inv, e * inv
                pos = z >= 0.0
                dz = g * jnp.where(pos, sig_neg, sig_abs) - jnp.where(pos, sig_abs, sig_neg) * (tot - nearer)
                if m is not None:
                    dz = jnp.where(m, dz, 0.0)
                dzs.append((dz * ATTN_SCALE).astype(BF16))
            dqs = [_dot(dz, kw) for dz, kw in zip(dzs, kws)]
            dks = [_dot_tn(dz, qh) for dz, qh in zip(dzs, qhs)]
            dvs = [_dot_tn(t[2].astype(BF16), doh) for t, doh in zip(fw, dohs)]
            return [(dq, dk, dv, t[3], sm[1], tot) for dq, dk, dv, t, sm, tot in zip(dqs, dks, dvs, fw, sums, totals)]

        def cond(c):
            return jnp.logical_and(c[0] >= 0, c[1] == 0)

        qhs, dohs, kws, vws, masks, first_blks, starts = [], [], [], [], [], [], []
        for sub in range(subs):
            i = pl.program_id(1) * subs + sub
            rows = slice(sub * QB, (sub + 1) * QB)
            first_blk, start, offset = _first_window(i)
            first_blks.append(first_blk)
            starts.append(start)
            qhs += _split_heads(q_ref[rows, :].astype(F32), low_lanes)
            dohs += _split_heads(do_ref[rows, :], low_lanes)
            kws += [k_ref[pl.ds(start, 2 * QB), :]] * 2
            vws += [v_ref[pl.ds(start, 2 * QB), :]] * 2
            masks += [_causal_mask(2 * QB, offset)] * 2
        zeros = [zero] * len(qhs)

        scores = _attn_scores(qhs, kws, masks)
        c_first = [_row_sums(sc[3], zero) for sc in scores]
        all_done = _all_done(c_first)

        def far_totals():
            beyond = []
            for sub in range(subs):
                pair = slice(2 * sub, 2 * sub + 2)

                def far_sums(c, qh=qhs[pair], doh=dohs[pair]):
                    j, _, c0, c1, r0, r1 = c
                    at = pl.multiple_of(j * QB, QB)
                    kb = k_ref[pl.ds(at, QB), :]
                    vb = v_ref[pl.ds(at, QB), :]
                    far = _attn_tiles(qh, [kb, kb], [None, None], [c0, c1], after_s)
                    r0 = r0 + jnp.sum(far[0][2] * _dot_nt(doh[0], vb), axis=1, keepdims=True)
                    r1 = r1 + jnp.sum(far[1][2] * _dot_nt(doh[1], vb), axis=1, keepdims=True)
                    return j - 1, _sweep_done(far[0][3], far[1][3]), far[0][3], far[1][3], r0, r1

                c0, c1 = c_first[pair]
                far = lax.while_loop(cond, far_sums, (first_blks[sub] - 1, _sweep_done(c0, c1), c0, c1, zero, zero))
                beyond += [far[4], far[5]]
            return tuple(beyond)

        beyond_first = list(lax.cond(all_done, lambda: tuple(zeros), far_totals))
        done = tiles(qhs, dohs, beyond_first, kws, vws, masks, zeros, zeros, scores)
        for sub in range(subs):
            dk_ref[pl.ds(starts[sub], 2 * QB), :] += done[2 * sub][1] + done[2 * sub + 1][1]
            dv_ref[pl.ds(starts[sub], 2 * QB), :] += done[2 * sub][2] + done[2 * sub + 1][2]
        first_dq = [jnp.where(low_lanes, done[2 * sub][0], done[2 * sub + 1][0]) for sub in range(subs)]

        def sweep_on():
            final = []
            for sub in range(subs):
                pair = slice(2 * sub, 2 * sub + 2)
                t0, t1 = done[pair]

                def step(c, qh=qhs[pair], doh=dohs[pair], total=[t0[5], t1[5]]):
                    j, _, dq, c0, c1, s0, s1 = c
                    at = pl.multiple_of(j * QB, QB)
                    kb = k_ref[pl.ds(at, QB), :]
                    vb = v_ref[pl.ds(at, QB), :]
                    f0, f1 = tiles(qh, doh, total, [kb, kb], [vb, vb], [None, None], [c0, c1], [s0, s1])
                    dk_ref[pl.ds(at, QB), :] += f0[1] + f1[1]
                    dv_ref[pl.ds(at, QB), :] += f0[2] + f1[2]
                    return j - 1, _sweep_done(f0[3], f1[3]), dq + jnp.where(low_lanes, f0[0], f1[0]), f0[3], f1[3], f0[4], f1[4]

                init = (first_blks[sub] - 1, _sweep_done(t0[3], t1[3]), first_dq[sub], t0[3], t1[3], t0[4], t1[4])
                final.append(lax.while_loop(cond, step, init)[2])
            return tuple(final)

        final = lax.cond(all_done, lambda: tuple(first_dq), sweep_on)
        for sub in range(subs):
            dq_ref[sub * QB:(sub + 1) * QB, :] = final[sub]

        @pl.when(last_step)
        def _():
            ex.wait(*ex_refs)

    blk = pl.BlockSpec((subs * QB, QB), lambda p, i: (i, p))
    full = lambda off: pl.BlockSpec((S, QB), lambda p, i: (0, off + p), pipeline_mode=pl.Buffered(1))
    outs = pl.pallas_call(
        body, name="bwd_attn", grid=(n_pairs, n_steps),
        in_specs=[blk, full(n_pairs), full(2 * n_pairs), blk] + ex.specs,
        out_specs=[blk, full(0), full(0)] + ex.specs,
        out_shape=[jax.ShapeDtypeStruct((S, n_pairs * QB), F32)] * 3 + ex.out_shape,
        scratch_shapes=ex.scratch,
        compiler_params=_params("arbitrary", "arbitrary"),
    )(qkv, qkv, qkv, d_attn, *ex.arrays)
    return outs[0], outs[1], outs[2], outs[3:]


def _bwd_pool_w_in(u, d_pool, w_pool, dq, dk, dv, h1_t, n_blocks, tile):
    S, C = u.shape
    D = h1_t.shape[0]
    n_tiles = S // tile
    ng = len(POOL_WINDOWS)
    cs = 4 * C // n_blocks
    per = C // cs

    def body(u_ref, uh_ref, d_ref, dh_ref, wp_ref, dq_ref, dk_ref, dv_ref, ht_ref, dproj_ref, dw_ref, dwp_ref):
        i = pl.program_id(0)
        first = i == 0
        _zero_when(first, dw_ref, dwp_ref)
        ht = ht_ref[...]
        for d in range(per, n_blocks):
            src = (dq_ref, dk_ref, dv_ref)[d // per - 1]
            dproj = src[:, (d % per) * cs:(d % per + 1) * cs].astype(BF16)
            dproj_ref[:, d * cs:(d + 1) * cs] = dproj
            dw_ref[d] += _dot(ht, dproj)
        halo = jnp.where(first, 0.0, uh_ref[...])
        parts = _pool_deviation(u_ref[...], halo, i * tile)
        dout = d_ref[...]
        nxt = jnp.where(i == n_tiles - 1, 0.0, dh_ref[...])
        dext = jnp.concatenate([dout, nxt], axis=0).astype(BF16)
        counts = _pool_counts(i * tile, tile + HALO)
        dps, scaled = [], []
        for g in range(ng):
            lanes = slice(g * POOL_GROUP, (g + 1) * POOL_GROUP)
            dp = _dot_nt(dext[:, lanes], wp_ref[g].astype(BF16))
            dps.append(dp[:tile])
            scaled.append(dp / counts[g])
        sums = _window_sums(jnp.concatenate(scaled, axis=1), forward=True)
        du = []
        for g, w in enumerate(POOL_WINDOWS):
            lanes = slice(g * POOL_GROUP, (g + 1) * POOL_GROUP)
            du.append((sums[w][:tile, lanes] - dps[g]).astype(BF16))
            dwp_ref[g] += _dot_tn(parts[g].astype(BF16), dext[:tile, lanes])
        du = jnp.concatenate(du, axis=1)
        for d in range(per):
            dproj = du[:, d * cs:(d + 1) * cs]
            dproj_ref[:, d * cs:(d + 1) * cs] = dproj
            dw_ref[d] += _dot(ht, dproj)

    row = pl.BlockSpec((tile, C), lambda i: (i, 0))
    return pl.pallas_call(
        body, name="bwd_pool_w_in", grid=(n_tiles,),
        in_specs=[row, _prev_halo_spec(tile, C), row, _next_halo_spec(tile, C, n_tiles), _const(w_pool.shape),
                  row, row, row, pl.BlockSpec((D, tile), lambda i: (0, i))],
        out_specs=[pl.BlockSpec((tile, 4 * C), lambda i: (i, 0)), pl.BlockSpec((n_blocks, D, cs), lambda i: (0, 0, 0)),
                   pl.BlockSpec(w_pool.shape, lambda i: (0, 0, 0))],
        out_shape=[jax.ShapeDtypeStruct((S, 4 * C), BF16), jax.ShapeDtypeStruct((n_blocks, D, cs), F32),
                   jax.ShapeDtypeStruct(w_pool.shape, F32)],
        compiler_params=_params("arbitrary"),
    )(u, u, d_pool, d_pool, w_pool, dq, dk, dv, h1_t)


def _bwd_x(dproj, w_in_t, x, dx2, g1, tile, ex):
    S, D = x.shape
    n_tiles = S // tile

    def body(dp_ref, w_ref, x_ref, dx2_ref, g_ref, *rest):
        dx_ref, dg_ref = rest[ex.n:ex.n + 2]
        ex_refs = ex.split(rest[:ex.n] + rest[ex.n + 2:])
        first, last = _grid_ends((n_tiles,))

        @pl.when(first)
        def _():
            ex.start(*ex_refs)
            dg_ref[...] = jnp.zeros_like(dg_ref)

        dh = _dot(dp_ref[...], w_ref[...])
        xf = x_ref[...]
        r1 = _rms(xf)
        n1 = xf * r1
        dg_ref[...] += _colsum(dh * n1)
        dx_ref[...] = dx2_ref[...] + _norm_bwd(dh * g_ref[...], n1, r1)

        @pl.when(last)
        def _():
            ex.wait(*ex_refs)

    row = lambda w: pl.BlockSpec((tile, w), lambda i: (i, 0))
    outs = pl.pallas_call(
        body, name="bwd_x", grid=(n_tiles,),
        in_specs=[row(w_in_t.shape[0]), _const(w_in_t.shape), row(D), row(D), _const((1, D))] + ex.specs,
        out_specs=[row(D), pl.BlockSpec((1, D), lambda i: (0, 0))] + ex.specs,
        out_shape=[jax.ShapeDtypeStruct((S, D), F32), jax.ShapeDtypeStruct((1, D), F32)] + ex.out_shape,
        scratch_shapes=ex.scratch,
        compiler_params=_params("arbitrary"),
    )(dproj, w_in_t, x, dx2, g1, *ex.arrays)
    return outs[0], outs[1], outs[2:]


def _mesh_position():
    x, y, c = lax.axis_index("x"), lax.axis_index("y"), lax.axis_index("c")
    return x, y, c, 4 * x + 2 * y + c


def _peer(x, y, c, k):
    px = 1 - x if k & 4 else x
    py = 1 - y if k & 2 else y
    pc = 1 - c if k & 1 else c
    return (px, py, pc), 4 * px + 2 * py + pc


class _Exchange:
    def __init__(self, arrays, gather):
        self.arrays, self.gather, self.n = list(arrays), gather, len(arrays)
        self.out_shape = [jax.ShapeDtypeStruct(((N_DEV,) + a.shape) if gather else a.shape, a.dtype) for a in arrays]
        self.specs = [pl.BlockSpec(memory_space=pl.ANY)] * self.n
        copies = self.n * (N_DEV - 1)
        self.scratch = [pltpu.SemaphoreType.DMA((copies,)), pltpu.SemaphoreType.DMA((copies,)),
                        pltpu.SemaphoreType.DMA((self.n,))]

    def _copies(self, ins, outs, sems):
        send_sems, recv_sems, local_sems = sems
        x, y, c, me = _mesh_position()
        local, remote = [], []
        for a in range(self.n):
            mine = ins[a] if self.gather else ins[a].at[me]
            local.append(pltpu.make_async_copy(mine, outs[a].at[me], local_sems.at[a]))
            for k in range(1, N_DEV):
                peer, peer_idx = _peer(x, y, c, k)
                src = ins[a] if self.gather else ins[a].at[peer_idx]
                sem = a * (N_DEV - 1) + k - 1
                remote.append(pltpu.make_async_remote_copy(
                    src_ref=src, dst_ref=outs[a].at[me], send_sem=send_sems.at[sem], recv_sem=recv_sems.at[sem],
                    device_id=peer, device_id_type=MESH))
        return local, remote

    def start(self, ins, outs, sems):
        local, remote = self._copies(ins, outs, sems)
        for cp in local + remote:
            cp.start()

    def wait(self, ins, outs, sems):
        local, remote = self._copies(ins, outs, sems)
        for cp in remote:
            cp.wait_send()
        for cp in remote:
            cp.wait_recv()
        for cp in local:
            cp.wait()

    def split(self, refs):
        return refs[:self.n], refs[self.n:2 * self.n], refs[2 * self.n:]


class _ChipGather(_Exchange):
    def __init__(self, arrays):
        super().__init__(arrays, gather=True)

    def _plan(self, ins, outs, sems, waiting):
        send_sems, recv_sems, local_sems = sems
        x, y, c, me = _mesh_position()
        sibling = (x, y, 1 - c)
        chips = [(1 - x, y), (x, 1 - y), (1 - x, 1 - y)]
        local, first, passed, arrivals = [], [], [], []
        for a in range(self.n):
            def copy(k, block, to, src=None, a=a):
                rows = outs[a].at[block]
                return pltpu.make_async_remote_copy(
                    src_ref=rows if src is None else src, dst_ref=rows, send_sem=send_sems.at[a * (N_DEV - 1) + k],
                    recv_sem=recv_sems.at[a * (N_DEV - 1) + k], device_id=to, device_id_type=MESH)

            local.append(pltpu.make_async_copy(ins[a], outs[a].at[me], local_sems.at[a]))
            first.append(copy(0, me, sibling, src=ins[a]))
            first += [copy(1 + j, me, (px, py, c), src=ins[a]) for j, (px, py) in enumerate(chips)]
            if waiting:
                passed.append([copy(4 + j, 4 * px + 2 * py + c, sibling) for j, (px, py) in enumerate(chips)])
                arrivals.append([copy(k, me, sibling) for k in range(N_DEV - 1)])
        return local, first, passed, arrivals

    def start(self, ins, outs, sems):
        local, first, _, _ = self._plan(ins, outs, sems, waiting=False)
        for cp in local + first:
            cp.start()

    def wait(self, ins, outs, sems):
        local, first, passed, arrivals = self._plan(ins, outs, sems, waiting=True)
        for a in range(self.n):
            for j in range(3):
                arrivals[a][1 + j].wait_recv()
                passed[a][j].start()
        for a in range(self.n):
            arrivals[a][0].wait_recv()
            for j in range(3):
                arrivals[a][4 + j].wait_recv()
        for cp in first + [cp for row in passed for cp in row]:
            cp.wait_send()
        for cp in local:
            cp.wait()


def _all_to_all(arrays, gather, name):
    ex = _ChipGather(arrays) if gather else _Exchange(arrays, gather)

    def body(*refs):
        ins, outs, sems = ex.split(refs)
        ex.start(ins, outs, sems)
        ex.wait(ins, outs, sems)

    return pl.pallas_call(body, name=name, in_specs=ex.specs, out_specs=ex.specs, out_shape=ex.out_shape,
                          scratch_shapes=ex.scratch)(*ex.arrays)


def _reduce_adamw(parts, w, m, v, rows):
    R, C = w.shape

    def body(p_ref, w_ref, m_ref, v_ref, g_ref, d_ref, nm_ref, nv_ref):
        g = p_ref[0].astype(F32)
        for s in range(1, N_DEV):
            g = g + p_ref[s].astype(F32)
        g_ref[...] = g
        m_new = ADAM_B1 * m_ref[...] + (1.0 - ADAM_B1) * g
        v_new = ADAM_B2 * v_ref[...] + (1.0 - ADAM_B2) * (g * g)
        m_hat = m_new / (1.0 - ADAM_B1 ** ADAM_STEP)
        v_hat = v_new / (1.0 - ADAM_B2 ** ADAM_STEP)
        d_ref[...] = -ADAM_LR * (m_hat / (jnp.sqrt(v_hat) + ADAM_EPS) + ADAM_WD * w_ref[...])
        nm_ref[...] = m_new
        nv_ref[...] = v_new

    row = pl.BlockSpec((rows, C), lambda i: (i, 0))
    return pl.pallas_call(
        body, name="reduce_adamw", grid=(R // rows,),
        in_specs=[pl.BlockSpec((N_DEV, rows, C), lambda i: (0, i, 0)), row, row, row],
        out_specs=[row] * 4, out_shape=[jax.ShapeDtypeStruct((R, C), F32)] * 4,
        compiler_params=_params("parallel"),
    )(parts, w, m, v)


def _row_tile(rows, cols):
    fits = [t for t in range(8, rows + 1, 8) if rows % t == 0 and N_DEV * t * cols * 4 <= 4 * 1024 * 1024]
    return max(fits) if fits else rows


SMALL_COLS = 1024


def _pack_small(vals):
    rows = []
    for a in vals:
        flat = a.reshape(-1)
        pad = (-flat.shape[0]) % SMALL_COLS
        rows.append(jnp.pad(flat, (0, pad)).reshape(-1, SMALL_COLS))
    packed = jnp.concatenate(rows, axis=0)
    return jnp.pad(packed, ((0, (-packed.shape[0]) % 8), (0, 0)))


def _unpack_small(packed, like):
    out, r = [], 0
    for a in like:
        n = a.size
        nr = -(-n // SMALL_COLS)
        out.append(packed[r:r + nr].reshape(-1)[:n].reshape(a.shape))
        r += nr
    return out


def kernel(x, norm_mix_pre, w_in, w_pool, pool_scale, attn_scale, w_out, norm_mix_post, norm_ffn_pre, w_up, conv_w, conv_b, w_down, norm_ffn_post, loss_target, m_norm_mix_pre, m_w_in, m_w_pool, m_pool_scale, m_attn_scale, m_w_out, m_norm_mix_post, m_norm_ffn_pre, m_w_up, m_conv_w, m_conv_b, m_w_down, m_norm_ffn_post, v_norm_mix_pre, v_w_in, v_w_pool, v_pool_scale, v_attn_scale, v_w_out, v_norm_mix_post, v_norm_ffn_pre, v_w_up, v_conv_w, v_conv_b, v_w_down, v_norm_ffn_post):
    S, D = x.shape[1], x.shape[2]
    d_ff_block = w_up.shape[2]

    xs, target = x[0], loss_target[0]
    g1, g2, g3, g4 = norm_mix_pre, norm_mix_post, norm_ffn_pre, norm_ffn_post
    big = min(512, S)
    small = min(256, S)
    n_pairs = pool_scale.shape[1] // QB
    conv_b_g = conv_b.reshape(N_DEV, 1, d_ff_block)

    (w_in_g,) = _all_to_all([w_in[0].astype(BF16)], gather=True, name="gather_w_in")
    h1_t, u, qkv, pool_out = _fwd_inproj_pool(xs, g1, w_in_g, w_pool[0], big)
    attn_out, (w_out_g, w_up_g, w_down_g, conv_w_g) = _fwd_attn(
        qkv, n_pairs, _ChipGather([w_out[0].astype(BF16), w_up[0].astype(BF16), w_down[0].astype(BF16), conv_w[0]]),
        min(ATTN_FWD_BLOCKS, S // QB))
    w_out_full = w_out_g.reshape(D, D)
    w_down4 = w_down_g.reshape(D_FF_SHARDS, d_ff_block, D)
    mix, x2, h2, h2_t = _fwd_outproj(pool_out, attn_out, pool_scale, attn_scale, w_out_full, xs, g2, g3, big)
    upre, gate_val, dy, df, loss_cols, dg4 = _fwd_ffn_loss(h2, w_up_g, conv_w_g, conv_b_g, w_down4, x2, target, g4, small)
    loss = lax.psum(0.5 * jnp.sum(loss_cols) / D, ("x", "y", "c"))

    dupre_g, dupre_v, d_wd4, d_wup, d_cb, d_cw = _bwd_ffn_blocks(gate_val, upre, conv_w_g, w_down4, df, h2_t, min(1024, S))
    dx2, dmix, dg3, dg2 = _bwd_ffn_tokens(dupre_g, dupre_v, w_up_g, x2, dy, mix, g2, g3, big)
    d_pool, d_attn, d_wout, d_ps, d_as = _bwd_outproj(dmix, w_out_full, pool_out, attn_out, pool_scale, attn_scale, big)
    d_wdown_g = d_wd4.reshape(N_DEV, w_down.shape[1], D)
    d_wout_g = d_wout.reshape(N_DEV, D // N_DEV, D)
    dq, dk, dv, late_parts = _bwd_attn(qkv, d_attn, n_pairs, _Exchange([d_wout_g, d_wup, d_wdown_g, d_cw], gather=False),
                                       min(ATTN_BWD_BLOCKS, S // QB))
    dproj, d_win, d_wp = _bwd_pool_w_in(u, d_pool, w_pool[0], dq, dk, dv, h1_t, N_DEV, big)
    w_in_t = w_in_g.transpose(0, 2, 1).reshape(-1, D)
    dx, dg1, (win_parts,) = _bwd_x(dproj, w_in_t, xs, dx2, g1, big, _Exchange([d_win], gather=False))
    big_parts = [win_parts] + list(late_parts)
    r = dict(dx=dx, g1=dg1, w_pool=d_wp, pool_scale=d_ps, attn_scale=d_as, g2=dg2, g3=dg3, conv_b=d_cb, g4=dg4)

    small_names = ["norm_mix_pre", "w_pool", "pool_scale", "attn_scale", "norm_mix_post", "norm_ffn_pre", "conv_b", "norm_ffn_post"]
    small_w = dict(norm_mix_pre=norm_mix_pre, w_pool=w_pool, pool_scale=pool_scale, attn_scale=attn_scale,
                   norm_mix_post=norm_mix_post, norm_ffn_pre=norm_ffn_pre, conv_b=conv_b, norm_ffn_post=norm_ffn_post)
    small_m = dict(norm_mix_pre=m_norm_mix_pre, w_pool=m_w_pool, pool_scale=m_pool_scale, attn_scale=m_attn_scale,
                   norm_mix_post=m_norm_mix_post, norm_ffn_pre=m_norm_ffn_pre, conv_b=m_conv_b, norm_ffn_post=m_norm_ffn_post)
    small_v = dict(norm_mix_pre=v_norm_mix_pre, w_pool=v_w_pool, pool_scale=v_pool_scale, attn_scale=v_attn_scale,
                   norm_mix_post=v_norm_mix_post, norm_ffn_pre=v_norm_ffn_pre, conv_b=v_conv_b, norm_ffn_post=v_norm_ffn_post)
    small_g = dict(norm_mix_pre=r["g1"], w_pool=r["w_pool"], pool_scale=r["pool_scale"], attn_scale=r["attn_scale"],
                   norm_mix_post=r["g2"], norm_ffn_pre=r["g3"], conv_b=r["conv_b"], norm_ffn_post=r["g4"])
    like = [small_w[n] for n in small_names]
    packed_g = _pack_small([small_g[n] for n in small_names])

    (small_parts,) = _all_to_all([packed_g], gather=True, name="gather_small_grads")

    def update(parts, w, m, v):
        R, C = w.shape
        return _reduce_adamw(parts, w, m, v, _row_tile(R, C))

    res = {}
    res["w_in"] = update(big_parts[0], w_in[0], m_w_in[0], v_w_in[0])
    res["w_out"] = update(big_parts[1], w_out[0], m_w_out[0], v_w_out[0])
    res["w_up"] = update(big_parts[2], w_up[0], m_w_up[0], v_w_up[0])
    res["w_down"] = update(big_parts[3], w_down[0], m_w_down[0], v_w_down[0])
    res["conv_w"] = update(big_parts[4], conv_w[0], m_conv_w[0], v_conv_w[0])
    small_res = update(small_parts, _pack_small(like), _pack_small([small_m[n] for n in small_names]),
                       _pack_small([small_v[n] for n in small_names]))
    small_res = [_unpack_small(t, like) for t in small_res]
    for idx, n in enumerate(small_names):
        res[n] = tuple(t[idx] for t in small_res)

    order = ["norm_mix_pre", "w_in", "w_pool", "pool_scale", "attn_scale", "w_out", "norm_mix_post", "norm_ffn_pre",
             "w_up", "conv_w", "conv_b", "w_down", "norm_ffn_post"]
    shaped = {n: tuple(t.reshape(s.shape) for t in res[n])
              for n, s in dict(norm_mix_pre=norm_mix_pre, w_in=w_in, w_pool=w_pool, pool_scale=pool_scale, attn_scale=attn_scale,
                               w_out=w_out, norm_mix_post=norm_mix_post, norm_ffn_pre=norm_ffn_pre, w_up=w_up, conv_w=conv_w,
                               conv_b=conv_b, w_down=w_down, norm_ffn_post=norm_ffn_post).items()}
    outs = [loss, r["dx"].reshape(x.shape)]
    for k in range(4):
        outs += [shaped[n][k] for n in order]
    return tuple(outs)
```

```python
import functools

import jax
import jax.numpy as jnp
from jax import lax
from jax.experimental import pallas as pl
from jax.experimental.pallas import tpu as pltpu

F32 = jnp.float32
BF16 = jnp.bfloat16

N_DEV = 8
EPS = 1e-6
POOL_WINDOWS = (2, 4, 8, 16)
POOL_GROUP = 128
HALO = 16
HEAD_DIM = 64
QB = 128
ATTN_SCALE = HEAD_DIM ** -0.5
ATTN_FWD_BLOCKS = 16
ATTN_BWD_BLOCKS = 8
EXP_UNDERFLOW = -88.0
D_FF_SHARDS = 4

ADAM_LR = 0.001
ADAM_B1 = 0.9
ADAM_B2 = 0.999
ADAM_EPS = 1e-08
ADAM_WD = 0.01
ADAM_STEP = 10

VMEM_LIMIT_V7X = 56 * 1024 * 1024
MESH = pl.DeviceIdType.MESH


def _params(*semantics):
    return pltpu.CompilerParams(dimension_semantics=semantics, vmem_limit_bytes=VMEM_LIMIT_V7X)


def _const(shape):
    zeros = (0,) * len(shape)
    return pl.BlockSpec(shape, lambda *_: zeros, pipeline_mode=pl.Buffered(1))


def _dot(a, b):
    return jnp.dot(a, b, preferred_element_type=F32)


def _dot_nt(a, b):
    return lax.dot_general(a, b, (((1,), (1,)), ((), ())), preferred_element_type=F32)


def _dot_tn(a, b):
    return lax.dot_general(a, b, (((0,), (0,)), ((), ())), preferred_element_type=F32)


def _rms(v):
    return lax.rsqrt(jnp.mean(v * v, axis=-1, keepdims=True) + EPS)


def _norm_bwd(dn_times_gain, n, r):
    return r * (dn_times_gain - n * jnp.mean(dn_times_gain * n, axis=-1, keepdims=True))


def _zero_when(first, *refs):
    @pl.when(first)
    def _():
        for ref in refs:
            ref[...] = jnp.zeros_like(ref)


def _colsum(v):
    return jnp.sum(v, axis=0, keepdims=True)


def _grid_ends(grid):
    ids = [pl.program_id(a) for a in range(len(grid))]
    first = functools.reduce(jnp.logical_and, [i == 0 for i in ids])
    last = functools.reduce(jnp.logical_and, [i == n - 1 for i, n in zip(ids, grid)])
    return first, last


def _fwd_inproj_pool(x, g1, w_in_g, w_pool, tile):
    S, D = x.shape
    nb, _, cs = w_in_g.shape
    d_pool = 2 * cs

    def body(x_ref, g_ref, w_ref, wp_ref, ht_ref, u_ref, qkv_ref, pool_ref, halo_ref):
        i = pl.program_id(0)
        _zero_when(i == 0, halo_ref)
        xf = x_ref[...]
        h = (xf * _rms(xf) * g_ref[...]).astype(BF16)
        ht_ref[...] = h.T
        u = jnp.concatenate([_dot(h, w_ref[0]), _dot(h, w_ref[1])], axis=1)
        u_ref[...] = u
        for d in range(2, nb):
            qkv_ref[:, (d - 2) * cs:(d - 1) * cs] = _dot(h, w_ref[d]).astype(BF16)
        parts = _pool_deviation(u, halo_ref[...], i * tile)
        halo_ref[...] = u[tile - HALO:, :]
        for g, p in enumerate(parts):
            pool_ref[:, g * POOL_GROUP:(g + 1) * POOL_GROUP] = _dot(p.astype(BF16), wp_ref[g].astype(BF16))

    row = lambda w: pl.BlockSpec((tile, w), lambda i: (i, 0))
    return pl.pallas_call(
        body, name="fwd_inproj_pool", grid=(S // tile,),
        in_specs=[row(D), _const((1, D)), _const(w_in_g.shape), _const(w_pool.shape)],
        out_specs=[pl.BlockSpec((D, tile), lambda i: (0, i)), row(d_pool), row(3 * d_pool), row(d_pool)],
        out_shape=[jax.ShapeDtypeStruct((D, S), BF16), jax.ShapeDtypeStruct((S, d_pool), F32),
                   jax.ShapeDtypeStruct((S, 3 * d_pool), BF16), jax.ShapeDtypeStruct((S, d_pool), F32)],
        scratch_shapes=[pltpu.VMEM((HALO, d_pool), F32)],
        compiler_params=_params("arbitrary"),
    )(x, g1, w_in_g, w_pool)


def _window_sums(ext, forward):
    n = ext.shape[0]
    sums, s, sh = {}, ext, 1
    while sh < POOL_WINDOWS[-1]:
        s = s + pltpu.roll(s, (n - sh) if forward else sh, axis=0)
        sh *= 2
        sums[sh] = s
    return sums


def _pool_counts(t0, rows):
    t1 = (lax.broadcasted_iota(jnp.int32, (rows, 1), 0) + t0 + 1).astype(F32)
    return [jnp.minimum(t1, float(w)) for w in POOL_WINDOWS]


def _pool_deviation(u, halo, t0):
    T = u.shape[0]
    sums = _window_sums(jnp.concatenate([halo, u], axis=0), forward=False)
    counts = _pool_counts(t0, T)
    parts = []
    for g, w in enumerate(POOL_WINDOWS):
        lanes = slice(g * POOL_GROUP, (g + 1) * POOL_GROUP)
        parts.append(sums[w][HALO:, lanes] / counts[g] - u[:, lanes])
    return parts


def _prev_halo_spec(tile, width):
    return pl.BlockSpec((HALO, width), lambda i: (jnp.maximum(i * (tile // HALO) - 1, 0), 0))


def _next_halo_spec(tile, width, n_tiles):
    last = n_tiles * (tile // HALO) - 1
    return pl.BlockSpec((HALO, width), lambda i: (jnp.minimum((i + 1) * (tile // HALO), last), 0))


def _low_lanes():
    return lax.broadcasted_iota(jnp.int32, (QB, 2 * HEAD_DIM), 1) < HEAD_DIM


LOG_PIECES = 2
GRAD_PIECES = 3


def _triangle(inclusive, pieces):
    row = lax.broadcasted_iota(jnp.int32, (pieces * QB, QB), 0) % QB
    col = lax.broadcasted_iota(jnp.int32, (pieces * QB, QB), 1)
    return ((row >= col) if inclusive else (row > col)).astype(BF16)


def _pieces(v, n):
    out, rest = [], v
    for _ in range(n - 1):
        piece = rest.astype(BF16)
        out.append(piece)
        rest = rest - piece.astype(F32)
    out.append(rest.astype(BF16))
    return jnp.concatenate(out, axis=1)


def _causal_mask(width, offset):
    row = lax.broadcasted_iota(jnp.int32, (QB, width), 0)
    col = lax.broadcasted_iota(jnp.int32, (QB, width), 1)
    return col < row + offset


def _row_sums(vals, carry):
    for b in reversed(range(vals.shape[1] // QB)):
        carry = carry + jnp.sum(vals[:, b * QB:(b + 1) * QB], axis=1, keepdims=True)
    return carry


def _suffix_sums(vals, tri, carry):
    n = vals.shape[1] // QB
    out, run = [None] * n, carry
    for b in reversed(range(n)):
        blk = vals[:, b * QB:(b + 1) * QB]
        out[b] = _dot(_pieces(blk, tri.shape[0] // QB), tri) + run
        run = run + jnp.sum(blk, axis=1, keepdims=True)
    return (out[0] if n == 1 else jnp.concatenate(out, axis=1)), run


def _attn_tiles(qhs, kws, masks, carries, after_s):
    return _attn_weights(_attn_scores(qhs, kws, masks), masks, carries, after_s)


def _attn_scores(qhs, kws, masks):
    zs = [_dot_nt(qh, kw) * ATTN_SCALE for qh, kw in zip(qhs, kws)]
    es = [jnp.exp(-jnp.abs(z)) for z in zs]
    softplus = [jnp.maximum(z, 0.0) + jnp.log(1.0 + e) for z, e in zip(zs, es)]
    log_1m_beta = [-sp if m is None else jnp.where(m, -sp, 0.0) for sp, m in zip(softplus, masks)]
    return list(zip(zs, es, softplus, log_1m_beta))


def _attn_weights(scores, masks, carries, after_s):
    sums = [_suffix_sums(l, after_s, c) for (_, _, _, l), c in zip(scores, carries)]
    weights = [jnp.exp(z - sp + st) for (z, _, sp, _), (st, _) in zip(scores, sums)]
    weights = [a if m is None else jnp.where(m, a, 0.0) for a, m in zip(weights, masks)]
    return [(z, e, a, c) for (z, e, _, _), a, (_, c) in zip(scores, weights, sums)]


def _split_heads(v, low_lanes):
    return jnp.where(low_lanes, v, 0.0).astype(BF16), jnp.where(low_lanes, 0.0, v).astype(BF16)


def _sweep_done(c0, c1):
    return (jnp.maximum(jnp.max(c0), jnp.max(c1)) < EXP_UNDERFLOW).astype(jnp.int32)


def _all_done(carries):
    return jnp.max(functools.reduce(jnp.maximum, carries)) < EXP_UNDERFLOW


def _first_window(i):
    first_blk = jnp.maximum(i - 1, 0)
    return first_blk, pl.multiple_of(first_blk * QB, QB), (i - first_blk) * QB


def _fwd_attn(qkv, n_pairs, ex, subs):
    S = qkv.shape[0]
    n_steps = S // (subs * QB)

    def body(q_ref, k_ref, v_ref, *rest):
        o_ref = rest[ex.n]
        ex_refs = ex.split(rest[:ex.n] + rest[ex.n + 1:])
        first_step, last_step = _grid_ends((n_pairs, n_steps))

        @pl.when(first_step)
        def _():
            ex.start(*ex_refs)

        low_lanes = _low_lanes()
        after_s = _triangle(False, LOG_PIECES)
        zero = jnp.zeros((QB, 1), F32)

        def cond(c):
            return jnp.logical_and(c[0] >= 0, c[1] == 0)

        qhs, kws, vws, masks, first_blks = [], [], [], [], []
        for sub in range(subs):
            i = pl.program_id(1) * subs + sub
            first_blk, start, offset = _first_window(i)
            first_blks.append(first_blk)
            qhs += _split_heads(q_ref[sub * QB:(sub + 1) * QB, :].astype(F32), low_lanes)
            kws += [k_ref[pl.ds(start, 2 * QB), :]] * 2
            vws += [v_ref[pl.ds(start, 2 * QB), :]] * 2
            masks += [_causal_mask(2 * QB, offset)] * 2
        tiles = _attn_tiles(qhs, kws, masks, [zero] * len(qhs), after_s)
        outs = [_dot(t[2].astype(BF16), vw) for t, vw in zip(tiles, vws)]

        first_out = [jnp.where(low_lanes, outs[2 * sub], outs[2 * sub + 1]) for sub in range(subs)]

        def sweep_on():
            final = []
            for sub in range(subs):
                def step(c, qh=qhs[2 * sub:2 * sub + 2]):
                    j, _, acc, c0, c1 = c
                    at = pl.multiple_of(j * QB, QB)
                    kb = k_ref[pl.ds(at, QB), :]
                    vb = v_ref[pl.ds(at, QB), :]
                    far = _attn_tiles(qh, [kb, kb], [None, None], [c0, c1], after_s)
                    acc = acc + jnp.where(low_lanes, _dot(far[0][2].astype(BF16), vb), _dot(far[1][2].astype(BF16), vb))
                    return j - 1, _sweep_done(far[0][3], far[1][3]), acc, far[0][3], far[1][3]

                c0, c1 = tiles[2 * sub][3], tiles[2 * sub + 1][3]
                final.append(lax.while_loop(cond, step, (first_blks[sub] - 1, _sweep_done(c0, c1), first_out[sub], c0, c1))[2])
            return tuple(final)

        final = lax.cond(_all_done([t[3] for t in tiles]), lambda: tuple(first_out), sweep_on)
        for sub in range(subs):
            o_ref[sub * QB:(sub + 1) * QB, :] = final[sub]

        @pl.when(last_step)
        def _():
            ex.wait(*ex_refs)

    outs = pl.pallas_call(
        body, name="fwd_attn", grid=(n_pairs, n_steps),
        in_specs=[pl.BlockSpec((subs * QB, QB), lambda p, i: (i, p)),
                  pl.BlockSpec((S, QB), lambda p, i: (0, n_pairs + p), pipeline_mode=pl.Buffered(1)),
                  pl.BlockSpec((S, QB), lambda p, i: (0, 2 * n_pairs + p), pipeline_mode=pl.Buffered(1))] + ex.specs,
        out_specs=[pl.BlockSpec((subs * QB, QB), lambda p, i: (i, p))] + ex.specs,
        out_shape=[jax.ShapeDtypeStruct((S, n_pairs * QB), F32)] + ex.out_shape,
        scratch_shapes=ex.scratch,
        compiler_params=_params("arbitrary", "arbitrary"),
    )(qkv, qkv, qkv, *ex.arrays)
    return outs[0], outs[1:]


def _normalized_heads(pool_out, attn_out):
    rp, ra = _rms(pool_out), _rms(attn_out)
    return pool_out * rp, rp, attn_out * ra, ra


def _fwd_outproj(pool_out, attn_out, pool_scale, attn_scale, w_out, x, g2, g3, tile):
    S, D = x.shape
    C = pool_out.shape[1]

    def body(p_ref, a_ref, ps_ref, as_ref, w_ref, x_ref, g2_ref, g3_ref, mix_ref, x2_ref, h2t_ref):
        n_p, _, n_a, _ = _normalized_heads(p_ref[...], a_ref[...])
        mix = _dot((n_p * ps_ref[...]).astype(BF16), w_ref[:C, :]) + _dot((n_a * as_ref[...]).astype(BF16), w_ref[C:, :])
        mix_ref[...] = mix
        x2 = x_ref[...] + mix * _rms(mix) * g2_ref[...]
        x2_ref[...] = x2
        h2t_ref[...] = (x2 * _rms(x2) * g3_ref[...]).astype(BF16).T

    row = lambda w: pl.BlockSpec((tile, w), lambda i: (i, 0))
    return pl.pallas_call(
        body, name="fwd_outproj", grid=(S // tile,),
        in_specs=[row(C), row(C), _const((1, C)), _const((1, C)), _const(w_out.shape), row(D), _const((1, D)), _const((1, D))],
        out_specs=[row(D), row(D), pl.BlockSpec((D, tile), lambda i: (0, i))],
        out_shape=[jax.ShapeDtypeStruct((S, D), F32), jax.ShapeDtypeStruct((S, D), F32), jax.ShapeDtypeStruct((D, S), BF16)],
        compiler_params=_params("parallel"),
    )(pool_out, attn_out, pool_scale, attn_scale, w_out, x, g2, g3)


def _conv_taps(tile_rows, halo_rows):
    T = tile_rows.shape[0]
    ext = jnp.concatenate([halo_rows.astype(F32), tile_rows.astype(F32)], axis=0)
    return pltpu.roll(ext, 2, axis=0)[HALO:], pltpu.roll(ext, 1, axis=0)[HALO:], ext[HALO:]


def _tap_rows(cw_ref, d):
    return [cw_ref[d, k:k + 1, :] for k in range(3)]


def _gated_unit(taps_gate, taps_val, cw_gate, cw_val, cb_gate, cb_val):
    gate = cw_gate[0] * taps_gate[0] + cw_gate[1] * taps_gate[1] + cw_gate[2] * taps_gate[2] + cb_gate
    val = cw_val[0] * taps_val[0] + cw_val[1] * taps_val[1] + cw_val[2] * taps_val[2] + cb_val
    sig = 1.0 / (1.0 + jnp.exp(-gate))
    return gate, val, sig


def _fwd_ffn_loss(h2_t, w_up_g, conv_w_g, conv_b_g, w_down4, x2, target, g4, tile):
    S, D = x2.shape
    nb, _, cs = w_up_g.shape
    half = D_FF_SHARDS

    def body(h_ref, w_ref, cw_ref, cb_ref, wd_ref, x2_ref, t_ref, g4_ref, upre_ref, gv_ref, dy_ref, df_ref, loss_ref, dg4_ref, halo_ref):
        _zero_when(pl.program_id(0) == 0, loss_ref, dg4_ref, halo_ref)
        h = h_ref[...].T

        def up(s):
            return _dot(h, w_ref[s]), _dot(h, w_ref[s + half])

        f = jnp.zeros((tile, D), F32)
        ahead = up(0)
        for s in range(half):
            ug, uv = ahead
            if s + 1 < half:
                ahead = up(s + 1)
            upre_ref[s] = ug.astype(BF16)
            upre_ref[s + half] = uv.astype(BF16)
            gate, val, sig = _gated_unit(_conv_taps(ug, halo_ref[s]), _conv_taps(uv, halo_ref[s + half]),
                                         _tap_rows(cw_ref, s), _tap_rows(cw_ref, s + half), cb_ref[s], cb_ref[s + half])
            halo_ref[s] = ug[tile - HALO:, :]
            halo_ref[s + half] = uv[tile - HALO:, :]
            gv_ref[s] = gate.astype(BF16)
            gv_ref[s + half] = val.astype(BF16)
            f = f + _dot((gate * sig * val).astype(BF16), wd_ref[s])
        r4 = _rms(f)
        n4 = f * r4
        err = x2_ref[...] + n4 * g4_ref[...] - t_ref[...]
        dy = err * (1.0 / D)
        dy_ref[...] = dy
        df_ref[...] = _norm_bwd(dy * g4_ref[...], n4, r4).astype(BF16)
        loss_ref[...] += _colsum(err * err)
        dg4_ref[...] += _colsum(dy * n4)

    row = lambda w: pl.BlockSpec((tile, w), lambda i: (i, 0))
    return pl.pallas_call(
        body, name="fwd_ffn_loss", grid=(S // tile,),
        in_specs=[pl.BlockSpec((D, tile), lambda i: (0, i)), _const(w_up_g.shape), _const(conv_w_g.shape), _const(conv_b_g.shape),
                  _const(w_down4.shape), row(D), row(D), _const((1, D))],
        out_specs=[pl.BlockSpec((nb, tile, cs), lambda i: (0, i, 0)), pl.BlockSpec((nb, tile, cs), lambda i: (0, i, 0)), row(D), row(D),
                   pl.BlockSpec((1, D), lambda i: (0, 0)), pl.BlockSpec((1, D), lambda i: (0, 0))],
        out_shape=[jax.ShapeDtypeStruct((nb, S, cs), BF16), jax.ShapeDtypeStruct((nb, S, cs), BF16),
                   jax.ShapeDtypeStruct((S, D), F32), jax.ShapeDtypeStruct((S, D), BF16),
                   jax.ShapeDtypeStruct((1, D), F32), jax.ShapeDtypeStruct((1, D), F32)],
        scratch_shapes=[pltpu.VMEM((nb, HALO, cs), F32)],
        compiler_params=_params("arbitrary"),
    )(h2_t, w_up_g, conv_w_g, conv_b_g, w_down4, x2, target, g4)


def _bwd_ffn_blocks(gate_val, upre, conv_w_g, w_down4, df, h2_t, tile):
    nb, S, cs = upre.shape
    D = df.shape[1]
    n_tiles = S // tile
    half = D_FF_SHARDS

    def body(g_ref, v_ref, ug_ref, uv_ref, cwg_ref, cwv_ref, wd_ref, df_ref, ht_ref,
             dug_ref, duv_ref, dwd_ref, dwg_ref, dwv_ref, dbg_ref, dbv_ref, dcwg_ref, dcwv_ref, next_ref):
        _zero_when(pl.program_id(1) == 0, dwd_ref, dwg_ref, dwv_ref, dbg_ref, dbv_ref, dcwg_ref, dcwv_ref, next_ref)
        dfb = df_ref[...]
        dact = _dot_nt(dfb, wd_ref[0])
        gate, val = g_ref[0].astype(F32), v_ref[0].astype(F32)
        sig = 1.0 / (1.0 + jnp.exp(-gate))
        silu = gate * sig
        dwd_ref[0] += _dot_tn((silu * val).astype(BF16), dfb)
        ht = ht_ref[...]

        def through_conv(dup, slot, cw_ref, u_ref, du_ref, dw_ref, db_ref, dcw_ref):
            ext = jnp.concatenate([dup, next_ref[slot]], axis=0)
            n = ext.shape[0]
            shifted = (dup, pltpu.roll(ext, n - 1, axis=0)[:tile], pltpu.roll(ext, n - 2, axis=0)[:tile])
            next_ref[slot] = dup[:HALO]
            cw = _tap_rows(cw_ref, 0)
            dupre = (cw[2] * shifted[0] + cw[1] * shifted[1] + cw[0] * shifted[2]).astype(BF16)
            du_ref[0] = dupre
            dw_ref[0] += _dot(ht, dupre)
            u = u_ref[0].astype(F32)
            db_ref[0] += _colsum(dup)
            for k in range(3):
                dcw_ref[0, k:k + 1, :] += _colsum(shifted[2 - k] * u)

        through_conv(dact * val * (sig * (1.0 + gate * (1.0 - sig))), 0, cwg_ref, ug_ref, dug_ref, dwg_ref, dbg_ref, dcwg_ref)
        through_conv(dact * silu, 1, cwv_ref, uv_ref, duv_ref, dwv_ref, dbv_ref, dcwv_ref)

    rev = lambda i: n_tiles - 1 - i
    blk = lambda off: pl.BlockSpec((1, tile, cs), lambda s, i: (s + off, rev(i), 0))
    par = lambda off, r: pl.BlockSpec((1, r, cs), lambda s, i: (s + off, 0, 0))
    acc = lambda r, c: pl.BlockSpec((1, r, c), lambda s, i: (s, 0, 0), pipeline_mode=pl.Buffered(1))
    outs = pl.pallas_call(
        body, name="bwd_ffn_blocks", grid=(half, n_tiles),
        in_specs=[blk(0), blk(half), blk(0), blk(half), par(0, 3), par(half, 3),
                  acc(cs, D), pl.BlockSpec((tile, D), lambda s, i: (rev(i), 0)), pl.BlockSpec((D, tile), lambda s, i: (0, rev(i)))],
        out_specs=[blk(0), blk(0), acc(cs, D), acc(D, cs), acc(D, cs), acc(1, cs), acc(1, cs), acc(3, cs), acc(3, cs)],
        out_shape=[jax.ShapeDtypeStruct((half, S, cs), BF16), jax.ShapeDtypeStruct((half, S, cs), BF16),
                   jax.ShapeDtypeStruct((half, cs, D), F32),
                   jax.ShapeDtypeStruct((half, D, cs), F32), jax.ShapeDtypeStruct((half, D, cs), F32),
                   jax.ShapeDtypeStruct((half, 1, cs), F32), jax.ShapeDtypeStruct((half, 1, cs), F32),
                   jax.ShapeDtypeStruct((half, 3, cs), F32), jax.ShapeDtypeStruct((half, 3, cs), F32)],
        scratch_shapes=[pltpu.VMEM((2, HALO, cs), F32)],
        compiler_params=_params("arbitrary", "arbitrary"),
    )(gate_val, gate_val, upre, upre, conv_w_g, conv_w_g, w_down4, df, h2_t)
    dupre_g, dupre_v, d_wd, d_wg, d_wv, dbg, dbv, dcwg, dcwv = outs
    return (dupre_g, dupre_v, d_wd, jnp.concatenate([d_wg, d_wv], axis=0), jnp.concatenate([dbg, dbv], axis=0),
            jnp.concatenate([dcwg, dcwv], axis=0))


def _bwd_ffn_tokens(dupre_g, dupre_v, w_up_g, x2, dy, mix, g2, g3, tile):
    half, S, cs = dupre_g.shape
    D = x2.shape[1]

    def body(dg_ref, dv_ref, w_ref, x2_ref, dy_ref, mix_ref, g2_ref, g3_ref, dx2_ref, dmix_ref, dg3_ref, dg2_ref):
        _zero_when(pl.program_id(0) == 0, dg3_ref, dg2_ref)
        parts = [_dot_nt(dg_ref[d], w_ref[d]) for d in range(half)] + [_dot_nt(dv_ref[d], w_ref[d + half]) for d in range(half)]
        while len(parts) > 1:
            parts = [a + b for a, b in zip(parts[::2], parts[1::2])]
        dh2 = parts[0]
        x2 = x2_ref[...]
        r3 = _rms(x2)
        n3 = x2 * r3
        dg3_ref[...] += _colsum(dh2 * n3)
        dx2 = dy_ref[...] + _norm_bwd(dh2 * g3_ref[...], n3, r3)
        dx2_ref[...] = dx2
        mix = mix_ref[...]
        r2 = _rms(mix)
        n2 = mix * r2
        dg2_ref[...] += _colsum(dx2 * n2)
        dmix_ref[...] = _norm_bwd(dx2 * g2_ref[...], n2, r2).astype(BF16)

    row = lambda w: pl.BlockSpec((tile, w), lambda i: (i, 0))
    blk = pl.BlockSpec((half, tile, cs), lambda i: (0, i, 0))
    acc = pl.BlockSpec((1, D), lambda i: (0, 0))
    return pl.pallas_call(
        body, name="bwd_ffn_tokens", grid=(S // tile,),
        in_specs=[blk, blk, _const(w_up_g.shape), row(D), row(D), row(D), _const((1, D)), _const((1, D))],
        out_specs=[row(D), row(D), acc, acc],
        out_shape=[jax.ShapeDtypeStruct((S, D), F32), jax.ShapeDtypeStruct((S, D), BF16),
                   jax.ShapeDtypeStruct((1, D), F32), jax.ShapeDtypeStruct((1, D), F32)],
        compiler_params=_params("arbitrary"),
    )(dupre_g, dupre_v, w_up_g, x2, dy, mix, g2, g3)


def _bwd_outproj(dmix, w_out, pool_out, attn_out, pool_scale, attn_scale, tile):
    S, D = dmix.shape
    C = pool_out.shape[1]

    def body(dm_ref, w_ref, p_ref, a_ref, ps_ref, as_ref, dp_ref, da_ref, dw_ref, dps_ref, das_ref):
        _zero_when(pl.program_id(0) == 0, dw_ref, dps_ref, das_ref)
        dmx = dm_ref[...]
        dmerged = _dot_nt(dmx, w_ref[...])
        n_p, r_p, n_a, r_a = _normalized_heads(p_ref[...], a_ref[...])
        merged = jnp.concatenate([(n_p * ps_ref[...]).astype(BF16), (n_a * as_ref[...]).astype(BF16)], axis=1)
        dw_ref[...] += _dot_tn(merged, dmx)
        dm_p, dm_a = dmerged[:, :C], dmerged[:, C:]
        dps_ref[...] += _colsum(dm_p * n_p)
        das_ref[...] += _colsum(dm_a * n_a)
        dp_ref[...] = _norm_bwd(dm_p * ps_ref[...], n_p, r_p)
        da_ref[...] = _norm_bwd(dm_a * as_ref[...], n_a, r_a)

    row = lambda w: pl.BlockSpec((tile, w), lambda i: (i, 0))
    return pl.pallas_call(
        body, name="bwd_outproj", grid=(S // tile,),
        in_specs=[row(D), _const(w_out.shape), row(C), row(C), _const((1, C)), _const((1, C))],
        out_specs=[row(C), row(C), pl.BlockSpec(w_out.shape, lambda i: (0, 0)),
                   pl.BlockSpec((1, C), lambda i: (0, 0)), pl.BlockSpec((1, C), lambda i: (0, 0))],
        out_shape=[jax.ShapeDtypeStruct((S, C), F32), jax.ShapeDtypeStruct((S, C), F32),
                   jax.ShapeDtypeStruct(w_out.shape, F32), jax.ShapeDtypeStruct((1, C), F32), jax.ShapeDtypeStruct((1, C), F32)],
        compiler_params=_params("arbitrary"),
    )(dmix, w_out, pool_out, attn_out, pool_scale, attn_scale)


def _bwd_attn(qkv, d_attn, n_pairs, ex, subs):
    S = qkv.shape[0]
    n_steps = S // (subs * QB)

    def body(q_ref, k_ref, v_ref, do_ref, *rest):
        dq_ref, dk_ref, dv_ref = rest[ex.n:ex.n + 3]
        ex_refs = ex.split(rest[:ex.n] + rest[ex.n + 3:])
        first_step, last_step = _grid_ends((n_pairs, n_steps))

        @pl.when(first_step)
        def _():
            ex.start(*ex_refs)

        @pl.when(pl.program_id(1) == 0)
        def _():
            dk_ref[...] = jnp.zeros_like(dk_ref)
            dv_ref[...] = jnp.zeros_like(dv_ref)

        low_lanes = _low_lanes()
        after_s, from_s = _triangle(False, LOG_PIECES), _triangle(True, GRAD_PIECES)
        zero = jnp.zeros((QB, 1), F32)

        def tiles(qhs, dohs, totals, kws, vws, masks, cs, gs, scores=None):
            fw = _attn_weights(scores or _attn_scores(qhs, kws, masks), masks, cs, after_s)
            gvals = [t[2] * _dot_nt(doh, vw) for t, doh, vw in zip(fw, dohs, vws)]
            sums = [_suffix_sums(g, from_s, g0) for g, g0 in zip(gvals, gs)]
            totals = [tot if m is None else tot + sm[1] for tot, m, sm in zip(totals, masks, sums)]
            dzs = []
            for (z, e, _, _), g, (nearer, _), tot, m in zip(fw, gvals, sums, totals, masks):
                inv = 1.0 / (1.0 + e)
                sig_abs, sig_neg = inv, e * inv
                pos = z >= 0.0
                dz = g * jnp.where(pos, sig_neg, sig_abs) - jnp.where(pos, sig_abs, sig_neg) * (tot - nearer)
                if m is not None:
                    dz = jnp.where(m, dz, 0.0)
                dzs.append((dz * ATTN_SCALE).astype(BF16))
            dqs = [_dot(dz, kw) for dz, kw in zip(dzs, kws)]
            dks = [_dot_tn(dz, qh) for dz, qh in zip(dzs, qhs)]
            dvs = [_dot_tn(t[2].astype(BF16), doh) for t, doh in zip(fw, dohs)]
            return [(dq, dk, dv, t[3], sm[1], tot) for dq, dk, dv, t, sm, tot in zip(dqs, dks, dvs, fw, sums, totals)]

        def cond(c):
            return jnp.logical_and(c[0] >= 0, c[1] == 0)

        qhs, dohs, kws, vws, masks, first_blks, starts = [], [], [], [], [], [], []
        for sub in range(subs):
            i = pl.program_id(1) * subs + sub
            rows = slice(sub * QB, (sub + 1) * QB)
            first_blk, start, offset = _first_window(i)
            first_blks.append(first_blk)
            starts.append(start)
            qhs += _split_heads(q_ref[rows, :].astype(F32), low_lanes)
            dohs += _split_heads(do_ref[rows, :], low_lanes)
            kws += [k_ref[pl.ds(start, 2 * QB), :]] * 2
            vws += [v_ref[pl.ds(start, 2 * QB), :]] * 2
            masks += [_causal_mask(2 * QB, offset)] * 2
        zeros = [zero] * len(qhs)

        scores = _attn_scores(qhs, kws, masks)
        c_first = [_row_sums(sc[3], zero) for sc in scores]
        all_done = _all_done(c_first)

        def far_totals():
            beyond = []
            for sub in range(subs):
                pair = slice(2 * sub, 2 * sub + 2)

                def far_sums(c, qh=qhs[pair], doh=dohs[pair]):
                    j, _, c0, c1, r0, r1 = c
                    at = pl.multiple_of(j * QB, QB)
                    kb = k_ref[pl.ds(at, QB), :]
                    vb = v_ref[pl.ds(at, QB), :]
                    far = _attn_tiles(qh, [kb, kb], [None, None], [c0, c1], after_s)
                    r0 = r0 + jnp.sum(far[0][2] * _dot_nt(doh[0], vb), axis=1, keepdims=True)
                    r1 = r1 + jnp.sum(far[1][2] * _dot_nt(doh[1], vb), axis=1, keepdims=True)
                    return j - 1, _sweep_done(far[0][3], far[1][3]), far[0][3], far[1][3], r0, r1

                c0, c1 = c_first[pair]
                far = lax.while_loop(cond, far_sums, (first_blks[sub] - 1, _sweep_done(c0, c1), c0, c1, zero, zero))
                beyond += [far[4], far[5]]
            return tuple(beyond)

        beyond_first = list(lax.cond(all_done, lambda: tuple(zeros), far_totals))
        done = tiles(qhs, dohs, beyond_first, kws, vws, masks, zeros, zeros, scores)
        for sub in range(subs):
            dk_ref[pl.ds(starts[sub], 2 * QB), :] += done[2 * sub][1] + done[2 * sub + 1][1]
            dv_ref[pl.ds(starts[sub], 2 * QB), :] += done[2 * sub][2] + done[2 * sub + 1][2]
        first_dq = [jnp.where(low_lanes, done[2 * sub][0], done[2 * sub + 1][0]) for sub in range(subs)]

        def sweep_on():
            final = []
            for sub in range(subs):
                pair = slice(2 * sub, 2 * sub + 2)
                t0, t1 = done[pair]

                def step(c, qh=qhs[pair], doh=dohs[pair], total=[t0[5], t1[5]]):
                    j, _, dq, c0, c1, s0, s1 = c
                    at = pl.multiple_of(j * QB, QB)
                    kb = k_ref[pl.ds(at, QB), :]
                    vb = v_ref[pl.ds(at, QB), :]
                    f0, f1 = tiles(qh, doh, total, [kb, kb], [vb, vb], [None, None], [c0, c1], [s0, s1])
                    dk_ref[pl.ds(at, QB), :] += f0[1] + f1[1]
                    dv_ref[pl.ds(at, QB), :] += f0[2] + f1[2]
                    return j - 1, _sweep_done(f0[3], f1[3]), dq + jnp.where(low_lanes, f0[0], f1[0]), f0[3], f1[3], f0[4], f1[4]

                init = (first_blks[sub] - 1, _sweep_done(t0[3], t1[3]), first_dq[sub], t0[3], t1[3], t0[4], t1[4])
                final.append(lax.while_loop(cond, step, init)[2])
            return tuple(final)

        final = lax.cond(all_done, lambda: tuple(first_dq), sweep_on)
        for sub in range(subs):
            dq_ref[sub * QB:(sub + 1) * QB, :] = final[sub]

        @pl.when(last_step)
        def _():
            ex.wait(*ex_refs)

    blk = pl.BlockSpec((subs * QB, QB), lambda p, i: (i, p))
    full = lambda off: pl.BlockSpec((S, QB), lambda p, i: (0, off + p), pipeline_mode=pl.Buffered(1))
    outs = pl.pallas_call(
        body, name="bwd_attn", grid=(n_pairs, n_steps),
        in_specs=[blk, full(n_pairs), full(2 * n_pairs), blk] + ex.specs,
        out_specs=[blk, full(0), full(0)] + ex.specs,
        out_shape=[jax.ShapeDtypeStruct((S, n_pairs * QB), F32)] * 3 + ex.out_shape,
        scratch_shapes=ex.scratch,
        compiler_params=_params("arbitrary", "arbitrary"),
    )(qkv, qkv, qkv, d_attn, *ex.arrays)
    return outs[0], outs[1], outs[2], outs[3:]


def _bwd_pool_w_in(u, d_pool, w_pool, dq, dk, dv, h1_t, n_blocks, tile):
    S, C = u.shape
    D = h1_t.shape[0]
    n_tiles = S // tile
    ng = len(POOL_WINDOWS)
    cs = 4 * C // n_blocks
    per = C // cs

    def body(u_ref, uh_ref, d_ref, dh_ref, wp_ref, dq_ref, dk_ref, dv_ref, ht_ref, dproj_ref, dw_ref, dwp_ref):
        i = pl.program_id(0)
        first = i == 0
        _zero_when(first, dw_ref, dwp_ref)
        ht = ht_ref[...]
        for d in range(per, n_blocks):
            src = (dq_ref, dk_ref, dv_ref)[d // per - 1]
            dproj = src[:, (d % per) * cs:(d % per + 1) * cs].astype(BF16)
            dproj_ref[:, d * cs:(d + 1) * cs] = dproj
            dw_ref[d] += _dot(ht, dproj)
        halo = jnp.where(first, 0.0, uh_ref[...])
        parts = _pool_deviation(u_ref[...], halo, i * tile)
        dout = d_ref[...]
        nxt = jnp.where(i == n_tiles - 1, 0.0, dh_ref[...])
        dext = jnp.concatenate([dout, nxt], axis=0).astype(BF16)
        counts = _pool_counts(i * tile, tile + HALO)
        dps, scaled = [], []
        for g in range(ng):
            lanes = slice(g * POOL_GROUP, (g + 1) * POOL_GROUP)
            dp = _dot_nt(dext[:, lanes], wp_ref[g].astype(BF16))
            dps.append(dp[:tile])
            scaled.append(dp / counts[g])
        sums = _window_sums(jnp.concatenate(scaled, axis=1), forward=True)
        du = []
        for g, w in enumerate(POOL_WINDOWS):
            lanes = slice(g * POOL_GROUP, (g + 1) * POOL_GROUP)
            du.append((sums[w][:tile, lanes] - dps[g]).astype(BF16))
            dwp_ref[g] += _dot_tn(parts[g].astype(BF16), dext[:tile, lanes])
        du = jnp.concatenate(du, axis=1)
        for d in range(per):
            dproj = du[:, d * cs:(d + 1) * cs]
            dproj_ref[:, d * cs:(d + 1) * cs] = dproj
            dw_ref[d] += _dot(ht, dproj)

    row = pl.BlockSpec((tile, C), lambda i: (i, 0))
    return pl.pallas_call(
        body, name="bwd_pool_w_in", grid=(n_tiles,),
        in_specs=[row, _prev_halo_spec(tile, C), row, _next_halo_spec(tile, C, n_tiles), _const(w_pool.shape),
                  row, row, row, pl.BlockSpec((D, tile), lambda i: (0, i))],
        out_specs=[pl.BlockSpec((tile, 4 * C), lambda i: (i, 0)), pl.BlockSpec((n_blocks, D, cs), lambda i: (0, 0, 0)),
                   pl.BlockSpec(w_pool.shape, lambda i: (0, 0, 0))],
        out_shape=[jax.ShapeDtypeStruct((S, 4 * C), BF16), jax.ShapeDtypeStruct((n_blocks, D, cs), F32),
                   jax.ShapeDtypeStruct(w_pool.shape, F32)],
        compiler_params=_params("arbitrary"),
    )(u, u, d_pool, d_pool, w_pool, dq, dk, dv, h1_t)


def _bwd_x(dproj, w_in_t, x, dx2, g1, tile, ex):
    S, D = x.shape
    n_tiles = S // tile

    def body(dp_ref, w_ref, x_ref, dx2_ref, g_ref, *rest):
        dx_ref, dg_ref = rest[ex.n:ex.n + 2]
        ex_refs = ex.split(rest[:ex.n] + rest[ex.n + 2:])
        first, last = _grid_ends((n_tiles,))

        @pl.when(first)
        def _():
            ex.start(*ex_refs)
            dg_ref[...] = jnp.zeros_like(dg_ref)

        dh = _dot(dp_ref[...], w_ref[...])
        xf = x_ref[...]
        r1 = _rms(xf)
        n1 = xf * r1
        dg_ref[...] += _colsum(dh * n1)
        dx_ref[...] = dx2_ref[...] + _norm_bwd(dh * g_ref[...], n1, r1)

        @pl.when(last)
        def _():
            ex.wait(*ex_refs)

    row = lambda w: pl.BlockSpec((tile, w), lambda i: (i, 0))
    outs = pl.pallas_call(
        body, name="bwd_x", grid=(n_tiles,),
        in_specs=[row(w_in_t.shape[0]), _const(w_in_t.shape), row(D), row(D), _const((1, D))] + ex.specs,
        out_specs=[row(D), pl.BlockSpec((1, D), lambda i: (0, 0))] + ex.specs,
        out_shape=[jax.ShapeDtypeStruct((S, D), F32), jax.ShapeDtypeStruct((1, D), F32)] + ex.out_shape,
        scratch_shapes=ex.scratch,
        compiler_params=_params("arbitrary"),
    )(dproj, w_in_t, x, dx2, g1, *ex.arrays)
    return outs[0], outs[1], outs[2:]


def _mesh_position():
    x, y, c = lax.axis_index("x"), lax.axis_index("y"), lax.axis_index("c")
    return x, y, c, 4 * x + 2 * y + c


def _peer(x, y, c, k):
    px = 1 - x if k & 4 else x
    py = 1 - y if k & 2 else y
    pc = 1 - c if k & 1 else c
    return (px, py, pc), 4 * px + 2 * py + pc


class _Exchange:
    def __init__(self, arrays, gather):
        self.arrays, self.gather, self.n = list(arrays), gather, len(arrays)
        self.out_shape = [jax.ShapeDtypeStruct(((N_DEV,) + a.shape) if gather else a.shape, a.dtype) for a in arrays]
        self.specs = [pl.BlockSpec(memory_space=pl.ANY)] * self.n
        copies = self.n * (N_DEV - 1)
        self.scratch = [pltpu.SemaphoreType.DMA((copies,)), pltpu.SemaphoreType.DMA((copies,)),
                        pltpu.SemaphoreType.DMA((self.n,))]

    def _copies(self, ins, outs, sems):
        send_sems, recv_sems, local_sems = sems
        x, y, c, me = _mesh_position()
        local, remote = [], []
        for a in range(self.n):
            mine = ins[a] if self.gather else ins[a].at[me]
            local.append(pltpu.make_async_copy(mine, outs[a].at[me], local_sems.at[a]))
            for k in range(1, N_DEV):
                peer, peer_idx = _peer(x, y, c, k)
                src = ins[a] if self.gather else ins[a].at[peer_idx]
                sem = a * (N_DEV - 1) + k - 1
                remote.append(pltpu.make_async_remote_copy(
                    src_ref=src, dst_ref=outs[a].at[me], send_sem=send_sems.at[sem], recv_sem=recv_sems.at[sem],
                    device_id=peer, device_id_type=MESH))
        return local, remote

    def start(self, ins, outs, sems):
        local, remote = self._copies(ins, outs, sems)
        for cp in local + remote:
            cp.start()

    def wait(self, ins, outs, sems):
        local, remote = self._copies(ins, outs, sems)
        for cp in remote:
            cp.wait_send()
        for cp in remote:
            cp.wait_recv()
        for cp in local:
            cp.wait()

    def split(self, refs):
        return refs[:self.n], refs[self.n:2 * self.n], refs[2 * self.n:]


class _ChipGather(_Exchange):
    def __init__(self, arrays):
        super().__init__(arrays, gather=True)

    def _plan(self, ins, outs, sems, waiting):
        send_sems, recv_sems, local_sems = sems
        x, y, c, me = _mesh_position()
        sibling = (x, y, 1 - c)
        chips = [(1 - x, y), (x, 1 - y), (1 - x, 1 - y)]
        local, first, passed, arrivals = [], [], [], []
        for a in range(self.n):
            def copy(k, block, to, src=None, a=a):
                rows = outs[a].at[block]
                return pltpu.make_async_remote_copy(
                    src_ref=rows if src is None else src, dst_ref=rows, send_sem=send_sems.at[a * (N_DEV - 1) + k],
                    recv_sem=recv_sems.at[a * (N_DEV - 1) + k], device_id=to, device_id_type=MESH)

            local.append(pltpu.make_async_copy(ins[a], outs[a].at[me], local_sems.at[a]))
            first.append(copy(0, me, sibling, src=ins[a]))
            first += [copy(1 + j, me, (px, py, c), src=ins[a]) for j, (px, py) in enumerate(chips)]
            if waiting:
                passed.append([copy(4 + j, 4 * px + 2 * py + c, sibling) for j, (px, py) in enumerate(chips)])
                arrivals.append([copy(k, me, sibling) for k in range(N_DEV - 1)])
        return local, first, passed, arrivals

    def start(self, ins, outs, sems):
        local, first, _, _ = self._plan(ins, outs, sems, waiting=False)
        for cp in local + first:
            cp.start()

    def wait(self, ins, outs, sems):
        local, first, passed, arrivals = self._plan(ins, outs, sems, waiting=True)
        for a in range(self.n):
            for j in range(3):
                arrivals[a][1 + j].wait_recv()
                passed[a][j].start()
        for a in range(self.n):
            arrivals[a][0].wait_recv()
            for j in range(3):
                arrivals[a][4 + j].wait_recv()
        for cp in first + [cp for row in passed for cp in row]:
            cp.wait_send()
        for cp in local:
            cp.wait()


def _all_to_all(arrays, gather, name):
    ex = _ChipGather(arrays) if gather else _Exchange(arrays, gather)

    def body(*refs):
        ins, outs, sems = ex.split(refs)
        ex.start(ins, outs, sems)
        ex.wait(ins, outs, sems)

    return pl.pallas_call(body, name=name, in_specs=ex.specs, out_specs=ex.specs, out_shape=ex.out_shape,
                          scratch_shapes=ex.scratch)(*ex.arrays)


def _reduce_adamw(parts, w, m, v, rows):
    R, C = w.shape

    def body(p_ref, w_ref, m_ref, v_ref, g_ref, d_ref, nm_ref, nv_ref):
        g = p_ref[0].astype(F32)
        for s in range(1, N_DEV):
            g = g + p_ref[s].astype(F32)
        g_ref[...] = g
        m_new = ADAM_B1 * m_ref[...] + (1.0 - ADAM_B1) * g
        v_new = ADAM_B2 * v_ref[...] + (1.0 - ADAM_B2) * (g * g)
        m_hat = m_new / (1.0 - ADAM_B1 ** ADAM_STEP)
        v_hat = v_new / (1.0 - ADAM_B2 ** ADAM_STEP)
        d_ref[...] = -ADAM_LR * (m_hat / (jnp.sqrt(v_hat) + ADAM_EPS) + ADAM_WD * w_ref[...])
        nm_ref[...] = m_new
        nv_ref[...] = v_new

    row = pl.BlockSpec((rows, C), lambda i: (i, 0))
    return pl.pallas_call(
        body, name="reduce_adamw", grid=(R // rows,),
        in_specs=[pl.BlockSpec((N_DEV, rows, C), lambda i: (0, i, 0)), row, row, row],
        out_specs=[row] * 4, out_shape=[jax.ShapeDtypeStruct((R, C), F32)] * 4,
        compiler_params=_params("parallel"),
    )(parts, w, m, v)


def _row_tile(rows, cols):
    fits = [t for t in range(8, rows + 1, 8) if rows % t == 0 and N_DEV * t * cols * 4 <= 4 * 1024 * 1024]
    return max(fits) if fits else rows


SMALL_COLS = 1024


def _pack_small(vals):
    rows = []
    for a in vals:
        flat = a.reshape(-1)
        pad = (-flat.shape[0]) % SMALL_COLS
        rows.append(jnp.pad(flat, (0, pad)).reshape(-1, SMALL_COLS))
    packed = jnp.concatenate(rows, axis=0)
    return jnp.pad(packed, ((0, (-packed.shape[0]) % 8), (0, 0)))


def _unpack_small(packed, like):
    out, r = [], 0
    for a in like:
        n = a.size
        nr = -(-n // SMALL_COLS)
        out.append(packed[r:r + nr].reshape(-1)[:n].reshape(a.shape))
        r += nr
    return out


def kernel(x, norm_mix_pre, w_in, w_pool, pool_scale, attn_scale, w_out, norm_mix_post, norm_ffn_pre, w_up, conv_w, conv_b, w_down, norm_ffn_post, loss_target, m_norm_mix_pre, m_w_in, m_w_pool, m_pool_scale, m_attn_scale, m_w_out, m_norm_mix_post, m_norm_ffn_pre, m_w_up, m_conv_w, m_conv_b, m_w_down, m_norm_ffn_post, v_norm_mix_pre, v_w_in, v_w_pool, v_pool_scale, v_attn_scale, v_w_out, v_norm_mix_post, v_norm_ffn_pre, v_w_up, v_conv_w, v_conv_b, v_w_down, v_norm_ffn_post):
    S, D = x.shape[1], x.shape[2]
    d_ff_block = w_up.shape[2]

    xs, target = x[0], loss_target[0]
    g1, g2, g3, g4 = norm_mix_pre, norm_mix_post, norm_ffn_pre, norm_ffn_post
    big = min(512, S)
    small = min(256, S)
    n_pairs = pool_scale.shape[1] // QB
    conv_b_g = conv_b.reshape(N_DEV, 1, d_ff_block)

    (w_in_g,) = _all_to_all([w_in[0].astype(BF16)], gather=True, name="gather_w_in")
    h1_t, u, qkv, pool_out = _fwd_inproj_pool(xs, g1, w_in_g, w_pool[0], big)
    attn_out, (w_out_g, w_up_g, w_down_g, conv_w_g) = _fwd_attn(
        qkv, n_pairs, _ChipGather([w_out[0].astype(BF16), w_up[0].astype(BF16), w_down[0].astype(BF16), conv_w[0]]),
        min(ATTN_FWD_BLOCKS, S // QB))
    w_out_full = w_out_g.reshape(D, D)
    w_down4 = w_down_g.reshape(D_FF_SHARDS, d_ff_block, D)
    mix, x2, h2_t = _fwd_outproj(pool_out, attn_out, pool_scale, attn_scale, w_out_full, xs, g2, g3, big)
    upre, gate_val, dy, df, loss_cols, dg4 = _fwd_ffn_loss(h2_t, w_up_g, conv_w_g, conv_b_g, w_down4, x2, target, g4, small)
    loss = lax.psum(0.5 * jnp.sum(loss_cols) / D, ("x", "y", "c"))

    dupre_g, dupre_v, d_wd4, d_wup, d_cb, d_cw = _bwd_ffn_blocks(gate_val, upre, conv_w_g, w_down4, df, h2_t, min(1024, S))
    dx2, dmix, dg3, dg2 = _bwd_ffn_tokens(dupre_g, dupre_v, w_up_g, x2, dy, mix, g2, g3, big)
    d_pool, d_attn, d_wout, d_ps, d_as = _bwd_outproj(dmix, w_out_full, pool_out, attn_out, pool_scale, attn_scale, big)
    d_wdown_g = d_wd4.reshape(N_DEV, w_down.shape[1], D)
    d_wout_g = d_wout.reshape(N_DEV, D // N_DEV, D)
    dq, dk, dv, late_parts = _bwd_attn(qkv, d_attn, n_pairs, _Exchange([d_wout_g, d_wup, d_wdown_g, d_cw], gather=False),
                                       min(ATTN_BWD_BLOCKS, S // QB))
    dproj, d_win, d_wp = _bwd_pool_w_in(u, d_pool, w_pool[0], dq, dk, dv, h1_t, N_DEV, big)
    w_in_t = w_in_g.transpose(0, 2, 1).reshape(-1, D)
    dx, dg1, (win_parts,) = _bwd_x(dproj, w_in_t, xs, dx2, g1, big, _Exchange([d_win], gather=False))
    big_parts = [win_parts] + list(late_parts)
    r = dict(dx=dx, g1=dg1, w_pool=d_wp, pool_scale=d_ps, attn_scale=d_as, g2=dg2, g3=dg3, conv_b=d_cb, g4=dg4)

    small_names = ["norm_mix_pre", "w_pool", "pool_scale", "attn_scale", "norm_mix_post", "norm_ffn_pre", "conv_b", "norm_ffn_post"]
    small_w = dict(norm_mix_pre=norm_mix_pre, w_pool=w_pool, pool_scale=pool_scale, attn_scale=attn_scale,
                   norm_mix_post=norm_mix_post, norm_ffn_pre=norm_ffn_pre, conv_b=conv_b, norm_ffn_post=norm_ffn_post)
    small_m = dict(norm_mix_pre=m_norm_mix_pre, w_pool=m_w_pool, pool_scale=m_pool_scale, attn_scale=m_attn_scale,
                   norm_mix_post=m_norm_mix_post, norm_ffn_pre=m_norm_ffn_pre, conv_b=m_conv_b, norm_ffn_post=m_norm_ffn_post)
    small_v = dict(norm_mix_pre=v_norm_mix_pre, w_pool=v_w_pool, pool_scale=v_pool_scale, attn_scale=v_attn_scale,
                   norm_mix_post=v_norm_mix_post, norm_ffn_pre=v_norm_ffn_pre, conv_b=v_conv_b, norm_ffn_post=v_norm_ffn_post)
    small_g = dict(norm_mix_pre=r["g1"], w_pool=r["w_pool"], pool_scale=r["pool_scale"], attn_scale=r["attn_scale"],
                   norm_mix_post=r["g2"], norm_ffn_pre=r["g3"], conv_b=r["conv_b"], norm_ffn_post=r["g4"])
    like = [small_w[n] for n in small_names]
    packed_g = _pack_small([small_g[n] for n in small_names])

    (small_parts,) = _all_to_all([packed_g], gather=True, name="gather_small_grads")

    def update(parts, w, m, v):
        R, C = w.shape
        return _reduce_adamw(parts, w, m, v, _row_tile(R, C))

    res = {}
    res["w_in"] = update(big_parts[0], w_in[0], m_w_in[0], v_w_in[0])
    res["w_out"] = update(big_parts[1], w_out[0], m_w_out[0], v_w_out[0])
    res["w_up"] = update(big_parts[2], w_up[0], m_w_up[0], v_w_up[0])
    res["w_down"] = update(big_parts[3], w_down[0], m_w_down[0], v_w_down[0])
    res["conv_w"] = update(big_parts[4], conv_w[0], m_conv_w[0], v_conv_w[0])
    small_res = update(small_parts, _pack_small(like), _pack_small([small_m[n] for n in small_names]),
                       _pack_small([small_v[n] for n in small_names]))
    small_res = [_unpack_small(t, like) for t in small_res]
    for idx, n in enumerate(small_names):
        res[n] = tuple(t[idx] for t in small_res)

    order = ["norm_mix_pre", "w_in", "w_pool", "pool_scale", "attn_scale", "w_out", "norm_mix_post", "norm_ffn_pre",
             "w_up", "conv_w", "conv_b", "w_down", "norm_ffn_post"]
    shaped = {n: tuple(t.reshape(s.shape) for t in res[n])
              for n, s in dict(norm_mix_pre=norm_mix_pre, w_in=w_in, w_pool=w_pool, pool_scale=pool_scale, attn_scale=attn_scale,
                               w_out=w_out, norm_mix_post=norm_mix_post, norm_ffn_pre=norm_ffn_pre, w_up=w_up, conv_w=conv_w,
                               conv_b=conv_b, w_down=w_down, norm_ffn_post=norm_ffn_post).items()}
    outs = [loss, r["dx"].reshape(x.shape)]
    for k in range(4):
        outs += [shaped[n][k] for n in order]
    return tuple(outs)
```

```python
import functools

import jax
import jax.numpy as jnp
from jax import lax
from jax.experimental import pallas as pl
from jax.experimental.pallas import tpu as pltpu

F32 = jnp.float32
BF16 = jnp.bfloat16

N_DEV = 8
EPS = 1e-6
POOL_WINDOWS = (2, 4, 8, 16)
POOL_GROUP = 128
HALO = 16
HEAD_DIM = 64
QB = 128
ATTN_SCALE = HEAD_DIM ** -0.5
ATTN_FWD_BLOCKS = 16
ATTN_BWD_BLOCKS = 8
EXP_UNDERFLOW = -88.0
D_FF_SHARDS = 4

ADAM_LR = 0.001
ADAM_B1 = 0.9
ADAM_B2 = 0.999
ADAM_EPS = 1e-08
ADAM_WD = 0.01
ADAM_STEP = 10

VMEM_LIMIT_V7X = 56 * 1024 * 1024
MESH = pl.DeviceIdType.MESH


def _params(*semantics):
    return pltpu.CompilerParams(dimension_semantics=semantics, vmem_limit_bytes=VMEM_LIMIT_V7X)


def _const(shape):
    zeros = (0,) * len(shape)
    return pl.BlockSpec(shape, lambda *_: zeros, pipeline_mode=pl.Buffered(1))


def _dot(a, b):
    return jnp.dot(a, b, preferred_element_type=F32)


def _dot_nt(a, b):
    return lax.dot_general(a, b, (((1,), (1,)), ((), ())), preferred_element_type=F32)


def _dot_tn(a, b):
    return lax.dot_general(a, b, (((0,), (0,)), ((), ())), preferred_element_type=F32)


def _rms(v):
    return lax.rsqrt(jnp.mean(v * v, axis=-1, keepdims=True) + EPS)


def _norm_bwd(dn_times_gain, n, r):
    return r * (dn_times_gain - n * jnp.mean(dn_times_gain * n, axis=-1, keepdims=True))


def _zero_when(first, *refs):
    @pl.when(first)
    def _():
        for ref in refs:
            ref[...] = jnp.zeros_like(ref)


def _colsum(v):
    return jnp.sum(v, axis=0, keepdims=True)


def _grid_ends(grid):
    ids = [pl.program_id(a) for a in range(len(grid))]
    first = functools.reduce(jnp.logical_and, [i == 0 for i in ids])
    last = functools.reduce(jnp.logical_and, [i == n - 1 for i, n in zip(ids, grid)])
    return first, last


def _fwd_inproj_pool(x, g1, w_in_g, w_pool, tile):
    S, D = x.shape
    nb, _, cs = w_in_g.shape
    d_pool = 2 * cs

    def body(x_ref, g_ref, w_ref, wp_ref, ht_ref, u_ref, qkv_ref, pool_ref, halo_ref):
        i = pl.program_id(0)
        _zero_when(i == 0, halo_ref)
        xf = x_ref[...]
        h = (xf * _rms(xf) * g_ref[...]).astype(BF16)
        ht_ref[...] = h.T
        u = jnp.concatenate([_dot(h, w_ref[0]), _dot(h, w_ref[1])], axis=1)
        u_ref[...] = u
        for d in range(2, nb):
            qkv_ref[:, (d - 2) * cs:(d - 1) * cs] = _dot(h, w_ref[d]).astype(BF16)
        parts = _pool_deviation(u, halo_ref[...], i * tile)
        halo_ref[...] = u[tile - HALO:, :]
        for g, p in enumerate(parts):
            pool_ref[:, g * POOL_GROUP:(g + 1) * POOL_GROUP] = _dot(p.astype(BF16), wp_ref[g].astype(BF16))

    row = lambda w: pl.BlockSpec((tile, w), lambda i: (i, 0))
    return pl.pallas_call(
        body, name="fwd_inproj_pool", grid=(S // tile,),
        in_specs=[row(D), _const((1, D)), _const(w_in_g.shape), _const(w_pool.shape)],
        out_specs=[pl.BlockSpec((D, tile), lambda i: (0, i)), row(d_pool), row(3 * d_pool), row(d_pool)],
        out_shape=[jax.ShapeDtypeStruct((D, S), BF16), jax.ShapeDtypeStruct((S, d_pool), F32),
                   jax.ShapeDtypeStruct((S, 3 * d_pool), BF16), jax.ShapeDtypeStruct((S, d_pool), F32)],
        scratch_shapes=[pltpu.VMEM((HALO, d_pool), F32)],
        compiler_params=_params("arbitrary"),
    )(x, g1, w_in_g, w_pool)


def _window_sums(ext, forward):
    n = ext.shape[0]
    sums, s, sh = {}, ext, 1
    while sh < POOL_WINDOWS[-1]:
        s = s + pltpu.roll(s, (n - sh) if forward else sh, axis=0)
        sh *= 2
        sums[sh] = s
    return sums


def _pool_counts(t0, rows):
    t1 = (lax.broadcasted_iota(jnp.int32, (rows, 1), 0) + t0 + 1).astype(F32)
    return [jnp.minimum(t1, float(w)) for w in POOL_WINDOWS]


def _pool_deviation(u, halo, t0):
    T = u.shape[0]
    sums = _window_sums(jnp.concatenate([halo, u], axis=0), forward=False)
    counts = _pool_counts(t0, T)
    parts = []
    for g, w in enumerate(POOL_WINDOWS):
        lanes = slice(g * POOL_GROUP, (g + 1) * POOL_GROUP)
        parts.append(sums[w][HALO:, lanes] / counts[g] - u[:, lanes])
    return parts


def _prev_halo_spec(tile, width):
    return pl.BlockSpec((HALO, width), lambda i: (jnp.maximum(i * (tile // HALO) - 1, 0), 0))


def _next_halo_spec(tile, width, n_tiles):
    last = n_tiles * (tile // HALO) - 1
    return pl.BlockSpec((HALO, width), lambda i: (jnp.minimum((i + 1) * (tile // HALO), last), 0))


def _low_lanes():
    return lax.broadcasted_iota(jnp.int32, (QB, 2 * HEAD_DIM), 1) < HEAD_DIM


LOG_PIECES = 2
GRAD_PIECES = 2


def _triangle(inclusive, pieces):
    row = lax.broadcasted_iota(jnp.int32, (pieces * QB, QB), 0) % QB
    col = lax.broadcasted_iota(jnp.int32, (pieces * QB, QB), 1)
    return ((row >= col) if inclusive else (row > col)).astype(BF16)


def _pieces(v, n):
    out, rest = [], v
    for _ in range(n - 1):
        piece = rest.astype(BF16)
        out.append(piece)
        rest = rest - piece.astype(F32)
    out.append(rest.astype(BF16))
    return jnp.concatenate(out, axis=1)


def _causal_mask(width, offset):
    row = lax.broadcasted_iota(jnp.int32, (QB, width), 0)
    col = lax.broadcasted_iota(jnp.int32, (QB, width), 1)
    return col < row + offset


def _row_sums(vals, carry):
    for b in reversed(range(vals.shape[1] // QB)):
        carry = carry + jnp.sum(vals[:, b * QB:(b + 1) * QB], axis=1, keepdims=True)
    return carry


def _suffix_sums(vals, tri, carry):
    n = vals.shape[1] // QB
    out, run = [None] * n, carry
    for b in reversed(range(n)):
        blk = vals[:, b * QB:(b + 1) * QB]
        out[b] = _dot(_pieces(blk, tri.shape[0] // QB), tri) + run
        run = run + jnp.sum(blk, axis=1, keepdims=True)
    return (out[0] if n == 1 else jnp.concatenate(out, axis=1)), run


def _attn_tiles(qhs, kws, masks, carries, after_s):
    return _attn_weights(_attn_scores(qhs, kws, masks), masks, carries, after_s)


def _attn_scores(qhs, kws, masks):
    zs = [_dot_nt(qh, kw) * ATTN_SCALE for qh, kw in zip(qhs, kws)]
    es = [jnp.exp(-jnp.abs(z)) for z in zs]
    softplus = [jnp.maximum(z, 0.0) + jnp.log(1.0 + e) for z, e in zip(zs, es)]
    log_1m_beta = [-sp if m is None else jnp.where(m, -sp, 0.0) for sp, m in zip(softplus, masks)]
    return list(zip(zs, es, softplus, log_1m_beta))


def _attn_weights(scores, masks, carries, after_s):
    sums = [_suffix_sums(l, after_s, c) for (_, _, _, l), c in zip(scores, carries)]
    weights = [jnp.exp(z - sp + st) for (z, _, sp, _), (st, _) in zip(scores, sums)]
    weights = [a if m is None else jnp.where(m, a, 0.0) for a, m in zip(weights, masks)]
    return [(z, e, a, c) for (z, e, _, _), a, (_, c) in zip(scores, weights, sums)]


def _split_heads(v, low_lanes):
    return jnp.where(low_lanes, v, 0.0).astype(BF16), jnp.where(low_lanes, 0.0, v).astype(BF16)


def _sweep_done(c0, c1):
    return (jnp.maximum(jnp.max(c0), jnp.max(c1)) < EXP_UNDERFLOW).astype(jnp.int32)


def _all_done(carries):
    return jnp.max(functools.reduce(jnp.maximum, carries)) < EXP_UNDERFLOW


def _first_window(i):
    first_blk = jnp.maximum(i - 1, 0)
    return first_blk, pl.multiple_of(first_blk * QB, QB), (i - first_blk) * QB


def _fwd_attn(qkv, n_pairs, ex, subs):
    S = qkv.shape[0]
    n_steps = S // (subs * QB)

    def body(q_ref, k_ref, v_ref, *rest):
        o_ref = rest[ex.n]
        ex_refs = ex.split(rest[:ex.n] + rest[ex.n + 1:])
        first_step, last_step = _grid_ends((n_pairs, n_steps))

        @pl.when(first_step)
        def _():
            ex.start(*ex_refs)

        low_lanes = _low_lanes()
        after_s = _triangle(False, LOG_PIECES)
        zero = jnp.zeros((QB, 1), F32)

        def cond(c):
            return jnp.logical_and(c[0] >= 0, c[1] == 0)

        qhs, kws, vws, masks, first_blks = [], [], [], [], []
        for sub in range(subs):
            i = pl.program_id(1) * subs + sub
            first_blk, start, offset = _first_window(i)
            first_blks.append(first_blk)
            qhs += _split_heads(q_ref[sub * QB:(sub + 1) * QB, :].astype(F32), low_lanes)
            kws += [k_ref[pl.ds(start, 2 * QB), :]] * 2
            vws += [v_ref[pl.ds(start, 2 * QB), :]] * 2
            masks += [_causal_mask(2 * QB, offset)] * 2
        tiles = _attn_tiles(qhs, kws, masks, [zero] * len(qhs), after_s)
        outs = [_dot(t[2].astype(BF16), vw) for t, vw in zip(tiles, vws)]

        first_out = [jnp.where(low_lanes, outs[2 * sub], outs[2 * sub + 1]) for sub in range(subs)]

        def sweep_on():
            final = []
            for sub in range(subs):
                def step(c, qh=qhs[2 * sub:2 * sub + 2]):
                    j, _, acc, c0, c1 = c
                    at = pl.multiple_of(j * QB, QB)
                    kb = k_ref[pl.ds(at, QB), :]
                    vb = v_ref[pl.ds(at, QB), :]
                    far = _attn_tiles(qh, [kb, kb], [None, None], [c0, c1], after_s)
                    acc = acc + jnp.where(low_lanes, _dot(far[0][2].astype(BF16), vb), _dot(far[1][2].astype(BF16), vb))
                    return j - 1, _sweep_done(far[0][3], far[1][3]), acc, far[0][3], far[1][3]

                c0, c1 = tiles[2 * sub][3], tiles[2 * sub + 1][3]
                final.append(lax.while_loop(cond, step, (first_blks[sub] - 1, _sweep_done(c0, c1), first_out[sub], c0, c1))[2])
            return tuple(final)

        final = lax.cond(_all_done([t[3] for t in tiles]), lambda: tuple(first_out), sweep_on)
        for sub in range(subs):
            o_ref[sub * QB:(sub + 1) * QB, :] = final[sub]

        @pl.when(last_step)
        def _():
            ex.wait(*ex_refs)

    outs = pl.pallas_call(
        body, name="fwd_attn", grid=(n_pairs, n_steps),
        in_specs=[pl.BlockSpec((subs * QB, QB), lambda p, i: (i, p)),
                  pl.BlockSpec((S, QB), lambda p, i: (0, n_pairs + p), pipeline_mode=pl.Buffered(1)),
                  pl.BlockSpec((S, QB), lambda p, i: (0, 2 * n_pairs + p), pipeline_mode=pl.Buffered(1))] + ex.specs,
        out_specs=[pl.BlockSpec((subs * QB, QB), lambda p, i: (i, p))] + ex.specs,
        out_shape=[jax.ShapeDtypeStruct((S, n_pairs * QB), F32)] + ex.out_shape,
        scratch_shapes=ex.scratch,
        compiler_params=_params("arbitrary", "arbitrary"),
    )(qkv, qkv, qkv, *ex.arrays)
    return outs[0], outs[1:]


def _normalized_heads(pool_out, attn_out):
    rp, ra = _rms(pool_out), _rms(attn_out)
    return pool_out * rp, rp, attn_out * ra, ra


def _fwd_outproj(pool_out, attn_out, pool_scale, attn_scale, w_out, x, g2, g3, tile):
    S, D = x.shape
    C = pool_out.shape[1]

    def body(p_ref, a_ref, ps_ref, as_ref, w_ref, x_ref, g2_ref, g3_ref, mix_ref, x2_ref, h2t_ref):
        n_p, _, n_a, _ = _normalized_heads(p_ref[...], a_ref[...])
        mix = _dot((n_p * ps_ref[...]).astype(BF16), w_ref[:C, :]) + _dot((n_a * as_ref[...]).astype(BF16), w_ref[C:, :])
        mix_ref[...] = mix
        x2 = x_ref[...] + mix * _rms(mix) * g2_ref[...]
        x2_ref[...] = x2
        h2t_ref[...] = (x2 * _rms(x2) * g3_ref[...]).astype(BF16).T

    row = lambda w: pl.BlockSpec((tile, w), lambda i: (i, 0))
    return pl.pallas_call(
        body, name="fwd_outproj", grid=(S // tile,),
        in_specs=[row(C), row(C), _const((1, C)), _const((1, C)), _const(w_out.shape), row(D), _const((1, D)), _const((1, D))],
        out_specs=[row(D), row(D), pl.BlockSpec((D, tile), lambda i: (0, i))],
        out_shape=[jax.ShapeDtypeStruct((S, D), F32), jax.ShapeDtypeStruct((S, D), F32), jax.ShapeDtypeStruct((D, S), BF16)],
        compiler_params=_params("parallel"),
    )(pool_out, attn_out, pool_scale, attn_scale, w_out, x, g2, g3)


def _conv_taps(tile_rows, halo_rows):
    T = tile_rows.shape[0]
    ext = jnp.concatenate([halo_rows.astype(F32), tile_rows.astype(F32)], axis=0)
    return pltpu.roll(ext, 2, axis=0)[HALO:], pltpu.roll(ext, 1, axis=0)[HALO:], ext[HALO:]


def _tap_rows(cw_ref, d):
    return [cw_ref[d, k:k + 1, :] for k in range(3)]


def _gated_unit(taps_gate, taps_val, cw_gate, cw_val, cb_gate, cb_val):
    gate = cw_gate[0] * taps_gate[0] + cw_gate[1] * taps_gate[1] + cw_gate[2] * taps_gate[2] + cb_gate
    val = cw_val[0] * taps_val[0] + cw_val[1] * taps_val[1] + cw_val[2] * taps_val[2] + cb_val
    sig = 1.0 / (1.0 + jnp.exp(-gate))
    return gate, val, sig


def _fwd_ffn_loss(h2_t, w_up_g, conv_w_g, conv_b_g, w_down4, x2, target, g4, tile):
    S, D = x2.shape
    nb, _, cs = w_up_g.shape
    half = D_FF_SHARDS

    def body(h_ref, w_ref, cw_ref, cb_ref, wd_ref, x2_ref, t_ref, g4_ref, upre_ref, gv_ref, dy_ref, df_ref, loss_ref, dg4_ref, halo_ref):
        _zero_when(pl.program_id(0) == 0, loss_ref, dg4_ref, halo_ref)
        h = h_ref[...].T

        def up(s):
            return _dot(h, w_ref[s]), _dot(h, w_ref[s + half])

        f = jnp.zeros((tile, D), F32)
        ahead = up(0)
        for s in range(half):
            ug, uv = ahead
            if s + 1 < half:
                ahead = up(s + 1)
            upre_ref[s] = ug.astype(BF16)
            upre_ref[s + half] = uv.astype(BF16)
            gate, val, sig = _gated_unit(_conv_taps(ug, halo_ref[s]), _conv_taps(uv, halo_ref[s + half]),
                                         _tap_rows(cw_ref, s), _tap_rows(cw_ref, s + half), cb_ref[s], cb_ref[s + half])
            halo_ref[s] = ug[tile - HALO:, :]
            halo_ref[s + half] = uv[tile - HALO:, :]
            gv_ref[s] = gate.astype(BF16)
            gv_ref[s + half] = val.astype(BF16)
            f = f + _dot((gate * sig * val).astype(BF16), wd_ref[s])
        r4 = _rms(f)
        n4 = f * r4
        err = x2_ref[...] + n4 * g4_ref[...] - t_ref[...]
        dy = err * (1.0 / D)
        dy_ref[...] = dy
        df_ref[...] = _norm_bwd(dy * g4_ref[...], n4, r4).astype(BF16)
        loss_ref[...] += _colsum(err * err)
        dg4_ref[...] += _colsum(dy * n4)

    row = lambda w: pl.BlockSpec((tile, w), lambda i: (i, 0))
    return pl.pallas_call(
        body, name="fwd_ffn_loss", grid=(S // tile,),
        in_specs=[pl.BlockSpec((D, tile), lambda i: (0, i)), _const(w_up_g.shape), _const(conv_w_g.shape), _const(conv_b_g.shape),
                  _const(w_down4.shape), row(D), row(D), _const((1, D))],
        out_specs=[pl.BlockSpec((nb, tile, cs), lambda i: (0, i, 0)), pl.BlockSpec((nb, tile, cs), lambda i: (0, i, 0)), row(D), row(D),
                   pl.BlockSpec((1, D), lambda i: (0, 0)), pl.BlockSpec((1, D), lambda i: (0, 0))],
        out_shape=[jax.ShapeDtypeStruct((nb, S, cs), BF16), jax.ShapeDtypeStruct((nb, S, cs), BF16),
                   jax.ShapeDtypeStruct((S, D), F32), jax.ShapeDtypeStruct((S, D), BF16),
                   jax.ShapeDtypeStruct((1, D), F32), jax.ShapeDtypeStruct((1, D), F32)],
        scratch_shapes=[pltpu.VMEM((nb, HALO, cs), F32)],
        compiler_params=_params("arbitrary"),
    )(h2_t, w_up_g, conv_w_g, conv_b_g, w_down4, x2, target, g4)


def _bwd_ffn_blocks(gate_val, upre, conv_w_g, w_down4, df, h2_t, tile):
    nb, S, cs = upre.shape
    D = df.shape[1]
    n_tiles = S // tile
    half = D_FF_SHARDS

    def body(g_ref, v_ref, ug_ref, uv_ref, cwg_ref, cwv_ref, wd_ref, df_ref, ht_ref,
             dug_ref, duv_ref, dwd_ref, dwg_ref, dwv_ref, dbg_ref, dbv_ref, dcwg_ref, dcwv_ref, next_ref):
        _zero_when(pl.program_id(1) == 0, dwd_ref, dwg_ref, dwv_ref, dbg_ref, dbv_ref, dcwg_ref, dcwv_ref, next_ref)
        dfb = df_ref[...]
        dact = _dot_nt(dfb, wd_ref[0])
        gate, val = g_ref[0].astype(F32), v_ref[0].astype(F32)
        sig = 1.0 / (1.0 + jnp.exp(-gate))
        silu = gate * sig
        dwd_ref[0] += _dot_tn((silu * val).astype(BF16), dfb)
        ht = ht_ref[...]

        def through_conv(dup, slot, cw_ref, u_ref, du_ref, dw_ref, db_ref, dcw_ref):
            ext = jnp.concatenate([dup, next_ref[slot]], axis=0)
            n = ext.shape[0]
            shifted = (dup, pltpu.roll(ext, n - 1, axis=0)[:tile], pltpu.roll(ext, n - 2, axis=0)[:tile])
            next_ref[slot] = dup[:HALO]
            cw = _tap_rows(cw_ref, 0)
            dupre = (cw[2] * shifted[0] + cw[1] * shifted[1] + cw[0] * shifted[2]).astype(BF16)
            du_ref[0] = dupre
            dw_ref[0] += _dot(ht, dupre)
            u = u_ref[0].astype(F32)
            db_ref[0] += _colsum(dup)
            for k in range(3):
                dcw_ref[0, k:k + 1, :] += _colsum(shifted[2 - k] * u)

        through_conv(dact * (val * (sig + silu * (1.0 - sig))), 0, cwg_ref, ug_ref, dug_ref, dwg_ref, dbg_ref, dcwg_ref)
        through_conv(dact * silu, 1, cwv_ref, uv_ref, duv_ref, dwv_ref, dbv_ref, dcwv_ref)

    rev = lambda i: n_tiles - 1 - i
    blk = lambda off: pl.BlockSpec((1, tile, cs), lambda s, i: (s + off, rev(i), 0))
    par = lambda off, r: pl.BlockSpec((1, r, cs), lambda s, i: (s + off, 0, 0))
    acc = lambda r, c: pl.BlockSpec((1, r, c), lambda s, i: (s, 0, 0), pipeline_mode=pl.Buffered(1))
    outs = pl.pallas_call(
        body, name="bwd_ffn_blocks", grid=(half, n_tiles),
        in_specs=[blk(0), blk(half), blk(0), blk(half), par(0, 3), par(half, 3),
                  acc(cs, D), pl.BlockSpec((tile, D), lambda s, i: (rev(i), 0)), pl.BlockSpec((D, tile), lambda s, i: (0, rev(i)))],
        out_specs=[blk(0), blk(0), acc(cs, D), acc(D, cs), acc(D, cs), acc(1, cs), acc(1, cs), acc(3, cs), acc(3, cs)],
        out_shape=[jax.ShapeDtypeStruct((half, S, cs), BF16), jax.ShapeDtypeStruct((half, S, cs), BF16),
                   jax.ShapeDtypeStruct((half, cs, D), F32),
                   jax.ShapeDtypeStruct((half, D, cs), F32), jax.ShapeDtypeStruct((half, D, cs), F32),
                   jax.ShapeDtypeStruct((half, 1, cs), F32), jax.ShapeDtypeStruct((half, 1, cs), F32),
                   jax.ShapeDtypeStruct((half, 3, cs), F32), jax.ShapeDtypeStruct((half, 3, cs), F32)],
        scratch_shapes=[pltpu.VMEM((2, HALO, cs), F32)],
        compiler_params=_params("arbitrary", "arbitrary"),
    )(gate_val, gate_val, upre, upre, conv_w_g, conv_w_g, w_down4, df, h2_t)
    dupre_g, dupre_v, d_wd, d_wg, d_wv, dbg, dbv, dcwg, dcwv = outs
    return (dupre_g, dupre_v, d_wd, jnp.concatenate([d_wg, d_wv], axis=0), jnp.concatenate([dbg, dbv], axis=0),
            jnp.concatenate([dcwg, dcwv], axis=0))


def _bwd_ffn_tokens(dupre_g, dupre_v, w_up_g, x2, dy, mix, g2, g3, tile):
    half, S, cs = dupre_g.shape
    D = x2.shape[1]

    def body(dg_ref, dv_ref, w_ref, x2_ref, dy_ref, mix_ref, g2_ref, g3_ref, dx2_ref, dmix_ref, dg3_ref, dg2_ref):
        _zero_when(pl.program_id(0) == 0, dg3_ref, dg2_ref)
        parts = [_dot_nt(dg_ref[d], w_ref[d]) for d in range(half)] + [_dot_nt(dv_ref[d], w_ref[d + half]) for d in range(half)]
        while len(parts) > 1:
            parts = [a + b for a, b in zip(parts[::2], parts[1::2])]
        dh2 = parts[0]
        x2 = x2_ref[...]
        r3 = _rms(x2)
        n3 = x2 * r3
        dg3_ref[...] += _colsum(dh2 * n3)
        dx2 = dy_ref[...] + _norm_bwd(dh2 * g3_ref[...], n3, r3)
        dx2_ref[...] = dx2
        mix = mix_ref[...]
        r2 = _rms(mix)
        n2 = mix * r2
        dg2_ref[...] += _colsum(dx2 * n2)
        dmix_ref[...] = _norm_bwd(dx2 * g2_ref[...], n2, r2).astype(BF16)

    row = lambda w: pl.BlockSpec((tile, w), lambda i: (i, 0))
    blk = pl.BlockSpec((half, tile, cs), lambda i: (0, i, 0))
    acc = pl.BlockSpec((1, D), lambda i: (0, 0))
    return pl.pallas_call(
        body, name="bwd_ffn_tokens", grid=(S // tile,),
        in_specs=[blk, blk, _const(w_up_g.shape), row(D), row(D), row(D), _const((1, D)), _const((1, D))],
        out_specs=[row(D), row(D), acc, acc],
        out_shape=[jax.ShapeDtypeStruct((S, D), F32), jax.ShapeDtypeStruct((S, D), BF16),
                   jax.ShapeDtypeStruct((1, D), F32), jax.ShapeDtypeStruct((1, D), F32)],
        compiler_params=_params("arbitrary"),
    )(dupre_g, dupre_v, w_up_g, x2, dy, mix, g2, g3)


def _bwd_outproj(dmix, w_out, pool_out, attn_out, pool_scale, attn_scale, tile):
    S, D = dmix.shape
    C = pool_out.shape[1]

    def body(dm_ref, w_ref, p_ref, a_ref, ps_ref, as_ref, dp_ref, da_ref, dw_ref, dps_ref, das_ref):
        _zero_when(pl.program_id(0) == 0, dw_ref, dps_ref, das_ref)
        dmx = dm_ref[...]
        dmerged = _dot_nt(dmx, w_ref[...])
        n_p, r_p, n_a, r_a = _normalized_heads(p_ref[...], a_ref[...])
        merged = jnp.concatenate([(n_p * ps_ref[...]).astype(BF16), (n_a * as_ref[...]).astype(BF16)], axis=1)
        dw_ref[...] += _dot_tn(merged, dmx)
        dm_p, dm_a = dmerged[:, :C], dmerged[:, C:]
        dps_ref[...] += _colsum(dm_p * n_p)
        das_ref[...] += _colsum(dm_a * n_a)
        dp_ref[...] = _norm_bwd(dm_p * ps_ref[...], n_p, r_p)
        da_ref[...] = _norm_bwd(dm_a * as_ref[...], n_a, r_a)

    row = lambda w: pl.BlockSpec((tile, w), lambda i: (i, 0))
    return pl.pallas_call(
        body, name="bwd_outproj", grid=(S // tile,),
        in_specs=[row(D), _const(w_out.shape), row(C), row(C), _const((1, C)), _const((1, C))],
        out_specs=[row(C), row(C), pl.BlockSpec(w_out.shape, lambda i: (0, 0)),
                   pl.BlockSpec((1, C), lambda i: (0, 0)), pl.BlockSpec((1, C), lambda i: (0, 0))],
        out_shape=[jax.ShapeDtypeStruct((S, C), F32), jax.ShapeDtypeStruct((S, C), F32),
                   jax.ShapeDtypeStruct(w_out.shape, F32), jax.ShapeDtypeStruct((1, C), F32), jax.ShapeDtypeStruct((1, C), F32)],
        compiler_params=_params("arbitrary"),
    )(dmix, w_out, pool_out, attn_out, pool_scale, attn_scale)


def _bwd_attn(qkv, d_attn, n_pairs, ex, subs):
    S = qkv.shape[0]
    n_steps = S // (subs * QB)

    def body(q_ref, k_ref, v_ref, do_ref, *rest):
        dq_ref, dk_ref, dv_ref = rest[ex.n:ex.n + 3]
        ex_refs = ex.split(rest[:ex.n] + rest[ex.n + 3:])
        first_step, last_step = _grid_ends((n_pairs, n_steps))

        @pl.when(first_step)
        def _():
            ex.start(*ex_refs)

        @pl.when(pl.program_id(1) == 0)
        def _():
            dk_ref[...] = jnp.zeros_like(dk_ref)
            dv_ref[...] = jnp.zeros_like(dv_ref)

        low_lanes = _low_lanes()
        after_s, from_s = _triangle(False, LOG_PIECES), _triangle(True, GRAD_PIECES)
        zero = jnp.zeros((QB, 1), F32)

        def tiles(qhs, dohs, totals, kws, vws, masks, cs, gs, scores=None):
            fw = _attn_weights(scores or _attn_scores(qhs, kws, masks), masks, cs, after_s)
            gvals = [t[2] * _dot_nt(doh, vw) for t, doh, vw in zip(fw, dohs, vws)]
            sums = [_suffix_sums(g, from_s, g0) for g, g0 in zip(gvals, gs)]
            totals = [tot if m is None else tot + sm[1] for tot, m, sm in zip(totals, masks, sums)]
            dzs = []
            for (z, e, _, _), g, (nearer, _), tot, m in zip(fw, gvals, sums, totals, masks):
                inv = 1.0 / (1.0 + e)
                sig_abs, sig_neg = inv, e * inv
                pos = z >= 0.0
                dz = g * jnp.where(pos, sig_neg, sig_abs) - jnp.where(pos, sig_abs, sig_neg) * (tot - nearer)
                if m is not None:
                    dz = jnp.where(m, dz, 0.0)
                dzs.append((dz * ATTN_SCALE).astype(BF16))
            dqs = [_dot(dz, kw) for dz, kw in zip(dzs, kws)]
            dks = [_dot_tn(dz, qh) for dz, qh in zip(dzs, qhs)]
            dvs = [_dot_tn(t[2].astype(BF16), doh) for t, doh in zip(fw, dohs)]
            return [(dq, dk, dv, t[3], sm[1], tot) for dq, dk, dv, t, sm, tot in zip(dqs, dks, dvs, fw, sums, totals)]

        def cond(c):
            return jnp.logical_and(c[0] >= 0, c[1] == 0)

        qhs, dohs, kws, vws, masks, first_blks, starts = [], [], [], [], [], [], []
        for sub in range(subs):
            i = pl.program_id(1) * subs + sub
            rows = slice(sub * QB, (sub + 1) * QB)
            first_blk, start, offset = _first_window(i)
            first_blks.append(first_blk)
            starts.append(start)
            qhs += _split_heads(q_ref[rows, :].astype(F32), low_lanes)
            dohs += _split_heads(do_ref[rows, :], low_lanes)
            kws += [k_ref[pl.ds(start, 2 * QB), :]] * 2
            vws += [v_ref[pl.ds(start, 2 * QB), :]] * 2
            masks += [_causal_mask(2 * QB, offset)] * 2
        zeros = [zero] * len(qhs)

        scores = _attn_scores(qhs, kws, masks)
        c_first = [_row_sums(sc[3], zero) for sc in scores]
        all_done = _all_done(c_first)

        def far_totals():
            beyond = []
            for sub in range(subs):
                pair = slice(2 * sub, 2 * sub + 2)

                def far_sums(c, qh=qhs[pair], doh=dohs[pair]):
                    j, _, c0, c1, r0, r1 = c
                    at = pl.multiple_of(j * QB, QB)
                    kb = k_ref[pl.ds(at, QB), :]
                    vb = v_ref[pl.ds(at, QB), :]
                    far = _attn_tiles(qh, [kb, kb], [None, None], [c0, c1], after_s)
                    r0 = r0 + jnp.sum(far[0][2] * _dot_nt(doh[0], vb), axis=1, keepdims=True)
                    r1 = r1 + jnp.sum(far[1][2] * _dot_nt(doh[1], vb), axis=1, keepdims=True)
                    return j - 1, _sweep_done(far[0][3], far[1][3]), far[0][3], far[1][3], r0, r1

                c0, c1 = c_first[pair]
                far = lax.while_loop(cond, far_sums, (first_blks[sub] - 1, _sweep_done(c0, c1), c0, c1, zero, zero))
                beyond += [far[4], far[5]]
            return tuple(beyond)

        beyond_first = list(lax.cond(all_done, lambda: tuple(zeros), far_totals))
        done = tiles(qhs, dohs, beyond_first, kws, vws, masks, zeros, zeros, scores)
        for sub in range(subs):
            dk_ref[pl.ds(starts[sub], 2 * QB), :] += done[2 * sub][1] + done[2 * sub + 1][1]
            dv_ref[pl.ds(starts[sub], 2 * QB), :] += done[2 * sub][2] + done[2 * sub + 1][2]
        first_dq = [jnp.where(low_lanes, done[2 * sub][0], done[2 * sub + 1][0]) for sub in range(subs)]

        def sweep_on():
            final = []
            for sub in range(subs):
                pair = slice(2 * sub, 2 * sub + 2)
                t0, t1 = done[pair]

                def step(c, qh=qhs[pair], doh=dohs[pair], total=[t0[5], t1[5]]):
                    j, _, dq, c0, c1, s0, s1 = c
                    at = pl.multiple_of(j * QB, QB)
                    kb = k_ref[pl.ds(at, QB), :]
                    vb = v_ref[pl.ds(at, QB), :]
                    f0, f1 = tiles(qh, doh, total, [kb, kb], [vb, vb], [None, None], [c0, c1], [s0, s1])
                    dk_ref[pl.ds(at, QB), :] += f0[1] + f1[1]
                    dv_ref[pl.ds(at, QB), :] += f0[2] + f1[2]
                    return j - 1, _sweep_done(f0[3], f1[3]), dq + jnp.where(low_lanes, f0[0], f1[0]), f0[3], f1[3], f0[4], f1[4]

                init = (first_blks[sub] - 1, _sweep_done(t0[3], t1[3]), first_dq[sub], t0[3], t1[3], t0[4], t1[4])
                final.append(lax.while_loop(cond, step, init)[2])
            return tuple(final)

        final = lax.cond(all_done, lambda: tuple(first_dq), sweep_on)
        for sub in range(subs):
            dq_ref[sub * QB:(sub + 1) * QB, :] = final[sub]

        @pl.when(last_step)
        def _():
            ex.wait(*ex_refs)

    blk = pl.BlockSpec((subs * QB, QB), lambda p, i: (i, p))
    full = lambda off: pl.BlockSpec((S, QB), lambda p, i: (0, off + p), pipeline_mode=pl.Buffered(1))
    outs = pl.pallas_call(
        body, name="bwd_attn", grid=(n_pairs, n_steps),
        in_specs=[blk, full(n_pairs), full(2 * n_pairs), blk] + ex.specs,
        out_specs=[blk, full(0), full(0)] + ex.specs,
        out_shape=[jax.ShapeDtypeStruct((S, n_pairs * QB), F32)] * 3 + ex.out_shape,
        scratch_shapes=ex.scratch,
        compiler_params=_params("arbitrary", "arbitrary"),
    )(qkv, qkv, qkv, d_attn, *ex.arrays)
    return outs[0], outs[1], outs[2], outs[3:]


def _bwd_pool_w_in(u, d_pool, w_pool, dq, dk, dv, h1_t, n_blocks, tile):
    S, C = u.shape
    D = h1_t.shape[0]
    n_tiles = S // tile
    ng = len(POOL_WINDOWS)
    cs = 4 * C // n_blocks
    per = C // cs

    def body(u_ref, uh_ref, d_ref, dh_ref, wp_ref, dq_ref, dk_ref, dv_ref, ht_ref, dproj_ref, dw_ref, dwp_ref):
        i = pl.program_id(0)
        first = i == 0
        _zero_when(first, dw_ref, dwp_ref)
        ht = ht_ref[...]
        for d in range(per, n_blocks):
            src = (dq_ref, dk_ref, dv_ref)[d // per - 1]
            dproj = src[:, (d % per) * cs:(d % per + 1) * cs].astype(BF16)
            dproj_ref[:, d * cs:(d + 1) * cs] = dproj
            dw_ref[d] += _dot(ht, dproj)
        halo = jnp.where(first, 0.0, uh_ref[...])
        parts = _pool_deviation(u_ref[...], halo, i * tile)
        dout = d_ref[...]
        nxt = jnp.where(i == n_tiles - 1, 0.0, dh_ref[...])
        dext = jnp.concatenate([dout, nxt], axis=0).astype(BF16)
        counts = _pool_counts(i * tile, tile + HALO)
        dps, scaled = [], []
        for g in range(ng):
            lanes = slice(g * POOL_GROUP, (g + 1) * POOL_GROUP)
            dp = _dot_nt(dext[:, lanes], wp_ref[g].astype(BF16))
            dps.append(dp[:tile])
            scaled.append(dp / counts[g])
        sums = _window_sums(jnp.concatenate(scaled, axis=1), forward=True)
        du = []
        for g, w in enumerate(POOL_WINDOWS):
            lanes = slice(g * POOL_GROUP, (g + 1) * POOL_GROUP)
            du.append((sums[w][:tile, lanes] - dps[g]).astype(BF16))
            dwp_ref[g] += _dot_tn(parts[g].astype(BF16), dext[:tile, lanes])
        du = jnp.concatenate(du, axis=1)
        for d in range(per):
            dproj = du[:, d * cs:(d + 1) * cs]
            dproj_ref[:, d * cs:(d + 1) * cs] = dproj
            dw_ref[d] += _dot(ht, dproj)

    row = pl.BlockSpec((tile, C), lambda i: (i, 0))
    return pl.pallas_call(
        body, name="bwd_pool_w_in", grid=(n_tiles,),
        in_specs=[row, _prev_halo_spec(tile, C), row, _next_halo_spec(tile, C, n_tiles), _const(w_pool.shape),
                  row, row, row, pl.BlockSpec((D, tile), lambda i: (0, i))],
        out_specs=[pl.BlockSpec((tile, 4 * C), lambda i: (i, 0)), pl.BlockSpec((n_blocks, D, cs), lambda i: (0, 0, 0)),
                   pl.BlockSpec(w_pool.shape, lambda i: (0, 0, 0))],
        out_shape=[jax.ShapeDtypeStruct((S, 4 * C), BF16), jax.ShapeDtypeStruct((n_blocks, D, cs), F32),
                   jax.ShapeDtypeStruct(w_pool.shape, F32)],
        compiler_params=_params("arbitrary"),
    )(u, u, d_pool, d_pool, w_pool, dq, dk, dv, h1_t)


def _bwd_x(dproj, w_in_t, x, dx2, g1, tile, ex):
    S, D = x.shape
    n_tiles = S // tile

    def body(dp_ref, w_ref, x_ref, dx2_ref, g_ref, *rest):
        dx_ref, dg_ref = rest[ex.n:ex.n + 2]
        ex_refs = ex.split(rest[:ex.n] + rest[ex.n + 2:])
        first, last = _grid_ends((n_tiles,))

        @pl.when(first)
        def _():
            ex.start(*ex_refs)
            dg_ref[...] = jnp.zeros_like(dg_ref)

        dh = _dot(dp_ref[...], w_ref[...])
        xf = x_ref[...]
        r1 = _rms(xf)
        n1 = xf * r1
        dg_ref[...] += _colsum(dh * n1)
        dx_ref[...] = dx2_ref[...] + _norm_bwd(dh * g_ref[...], n1, r1)

        @pl.when(last)
        def _():
            ex.wait(*ex_refs)

    row = lambda w: pl.BlockSpec((tile, w), lambda i: (i, 0))
    outs = pl.pallas_call(
        body, name="bwd_x", grid=(n_tiles,),
        in_specs=[row(w_in_t.shape[0]), _const(w_in_t.shape), row(D), row(D), _const((1, D))] + ex.specs,
        out_specs=[row(D), pl.BlockSpec((1, D), lambda i: (0, 0))] + ex.specs,
        out_shape=[jax.ShapeDtypeStruct((S, D), F32), jax.ShapeDtypeStruct((1, D), F32)] + ex.out_shape,
        scratch_shapes=ex.scratch,
        compiler_params=_params("arbitrary"),
    )(dproj, w_in_t, x, dx2, g1, *ex.arrays)
    return outs[0], outs[1], outs[2:]


def _mesh_position():
    x, y, c = lax.axis_index("x"), lax.axis_index("y"), lax.axis_index("c")
    return x, y, c, 4 * x + 2 * y + c


def _peer(x, y, c, k):
    px = 1 - x if k & 4 else x
    py = 1 - y if k & 2 else y
    pc = 1 - c if k & 1 else c
    return (px, py, pc), 4 * px + 2 * py + pc


class _Exchange:
    def __init__(self, arrays, gather):
        self.arrays, self.gather, self.n = list(arrays), gather, len(arrays)
        self.out_shape = [jax.ShapeDtypeStruct(((N_DEV,) + a.shape) if gather else a.shape, a.dtype) for a in arrays]
        self.specs = [pl.BlockSpec(memory_space=pl.ANY)] * self.n
        copies = self.n * (N_DEV - 1)
        self.scratch = [pltpu.SemaphoreType.DMA((copies,)), pltpu.SemaphoreType.DMA((copies,)),
                        pltpu.SemaphoreType.DMA((self.n,))]

    def _copies(self, ins, outs, sems):
        send_sems, recv_sems, local_sems = sems
        x, y, c, me = _mesh_position()
        local, remote = [], []
        for a in range(self.n):
            mine = ins[a] if self.gather else ins[a].at[me]
            local.append(pltpu.make_async_copy(mine, outs[a].at[me], local_sems.at[a]))
            for k in range(1, N_DEV):
                peer, peer_idx = _peer(x, y, c, k)
                src = ins[a] if self.gather else ins[a].at[peer_idx]
                sem = a * (N_DEV - 1) + k - 1
                remote.append(pltpu.make_async_remote_copy(
                    src_ref=src, dst_ref=outs[a].at[me], send_sem=send_sems.at[sem], recv_sem=recv_sems.at[sem],
                    device_id=peer, device_id_type=MESH))
        return local, remote

    def start(self, ins, outs, sems):
        local, remote = self._copies(ins, outs, sems)
        for cp in local + remote:
            cp.start()

    def wait(self, ins, outs, sems):
        local, remote = self._copies(ins, outs, sems)
        for cp in remote:
            cp.wait_send()
        for cp in remote:
            cp.wait_recv()
        for cp in local:
            cp.wait()

    def split(self, refs):
        return refs[:self.n], refs[self.n:2 * self.n], refs[2 * self.n:]


class _ChipGather(_Exchange):
    def __init__(self, arrays):
        super().__init__(arrays, gather=True)

    def _plan(self, ins, outs, sems, waiting):
        send_sems, recv_sems, local_sems = sems
        x, y, c, me = _mesh_position()
        sibling = (x, y, 1 - c)
        chips = [(1 - x, y), (x, 1 - y), (1 - x, 1 - y)]
        local, first, passed, arrivals = [], [], [], []
        for a in range(self.n):
            def copy(k, block, to, src=None, a=a):
                rows = outs[a].at[block]
                return pltpu.make_async_remote_copy(
                    src_ref=rows if src is None else src, dst_ref=rows, send_sem=send_sems.at[a * (N_DEV - 1) + k],
                    recv_sem=recv_sems.at[a * (N_DEV - 1) + k], device_id=to, device_id_type=MESH)

            local.append(pltpu.make_async_copy(ins[a], outs[a].at[me], local_sems.at[a]))
            first.append(copy(0, me, sibling, src=ins[a]))
            first += [copy(1 + j, me, (px, py, c), src=ins[a]) for j, (px, py) in enumerate(chips)]
            if waiting:
                passed.append([copy(4 + j, 4 * px + 2 * py + c, sibling) for j, (px, py) in enumerate(chips)])
                arrivals.append([copy(k, me, sibling) for k in range(N_DEV - 1)])
        return local, first, passed, arrivals

    def start(self, ins, outs, sems):
        local, first, _, _ = self._plan(ins, outs, sems, waiting=False)
        for cp in local + first:
            cp.start()

    def wait(self, ins, outs, sems):
        local, first, passed, arrivals = self._plan(ins, outs, sems, waiting=True)
        for a in range(self.n):
            for j in range(3):
                arrivals[a][1 + j].wait_recv()
                passed[a][j].start()
        for a in range(self.n):
            arrivals[a][0].wait_recv()
            for j in range(3):
                arrivals[a][4 + j].wait_recv()
        for cp in first + [cp for row in passed for cp in row]:
            cp.wait_send()
        for cp in local:
            cp.wait()


def _all_to_all(arrays, gather, name):
    ex = _ChipGather(arrays) if gather else _Exchange(arrays, gather)

    def body(*refs):
        ins, outs, sems = ex.split(refs)
        ex.start(ins, outs, sems)
        ex.wait(ins, outs, sems)

    return pl.pallas_call(body, name=name, in_specs=ex.specs, out_specs=ex.specs, out_shape=ex.out_shape,
                          scratch_shapes=ex.scratch)(*ex.arrays)


def _reduce_adamw(parts, w, m, v, rows):
    R, C = w.shape

    def body(p_ref, w_ref, m_ref, v_ref, g_ref, d_ref, nm_ref, nv_ref):
        g = p_ref[0].astype(F32)
        for s in range(1, N_DEV):
            g = g + p_ref[s].astype(F32)
        g_ref[...] = g
        m_new = ADAM_B1 * m_ref[...] + (1.0 - ADAM_B1) * g
        v_new = ADAM_B2 * v_ref[...] + (1.0 - ADAM_B2) * (g * g)
        m_hat = m_new / (1.0 - ADAM_B1 ** ADAM_STEP)
        v_hat = v_new / (1.0 - ADAM_B2 ** ADAM_STEP)
        d_ref[...] = -ADAM_LR * (m_hat / (jnp.sqrt(v_hat) + ADAM_EPS) + ADAM_WD * w_ref[...])
        nm_ref[...] = m_new
        nv_ref[...] = v_new

    row = pl.BlockSpec((rows, C), lambda i: (i, 0))
    return pl.pallas_call(
        body, name="reduce_adamw", grid=(R // rows,),
        in_specs=[pl.BlockSpec((N_DEV, rows, C), lambda i: (0, i, 0)), row, row, row],
        out_specs=[row] * 4, out_shape=[jax.ShapeDtypeStruct((R, C), F32)] * 4,
        compiler_params=_params("parallel"),
    )(parts, w, m, v)


def _row_tile(rows, cols):
    fits = [t for t in range(8, rows + 1, 8) if rows % t == 0 and N_DEV * t * cols * 4 <= 4 * 1024 * 1024]
    return max(fits) if fits else rows


SMALL_COLS = 1024


def _pack_small(vals):
    rows = []
    for a in vals:
        flat = a.reshape(-1)
        pad = (-flat.shape[0]) % SMALL_COLS
        rows.append(jnp.pad(flat, (0, pad)).reshape(-1, SMALL_COLS))
    packed = jnp.concatenate(rows, axis=0)
    return jnp.pad(packed, ((0, (-packed.shape[0]) % 8), (0, 0)))


def _unpack_small(packed, like):
    out, r = [], 0
    for a in like:
        n = a.size
        nr = -(-n // SMALL_COLS)
        out.append(packed[r:r + nr].reshape(-1)[:n].reshape(a.shape))
        r += nr
    return out


def kernel(x, norm_mix_pre, w_in, w_pool, pool_scale, attn_scale, w_out, norm_mix_post, norm_ffn_pre, w_up, conv_w, conv_b, w_down, norm_ffn_post, loss_target, m_norm_mix_pre, m_w_in, m_w_pool, m_pool_scale, m_attn_scale, m_w_out, m_norm_mix_post, m_norm_ffn_pre, m_w_up, m_conv_w, m_conv_b, m_w_down, m_norm_ffn_post, v_norm_mix_pre, v_w_in, v_w_pool, v_pool_scale, v_attn_scale, v_w_out, v_norm_mix_post, v_norm_ffn_pre, v_w_up, v_conv_w, v_conv_b, v_w_down, v_norm_ffn_post):
    S, D = x.shape[1], x.shape[2]
    d_ff_block = w_up.shape[2]

    xs, target = x[0], loss_target[0]
    g1, g2, g3, g4 = norm_mix_pre, norm_mix_post, norm_ffn_pre, norm_ffn_post
    big = min(512, S)
    small = min(256, S)
    n_pairs = pool_scale.shape[1] // QB
    conv_b_g = conv_b.reshape(N_DEV, 1, d_ff_block)

    (w_in_g,) = _all_to_all([w_in[0].astype(BF16)], gather=True, name="gather_w_in")
    h1_t, u, qkv, pool_out = _fwd_inproj_pool(xs, g1, w_in_g, w_pool[0], big)
    attn_out, (w_out_g, w_up_g, w_down_g, conv_w_g) = _fwd_attn(
        qkv, n_pairs, _ChipGather([w_out[0].astype(BF16), w_up[0].astype(BF16), w_down[0].astype(BF16), conv_w[0]]),
        min(ATTN_FWD_BLOCKS, S // QB))
    w_out_full = w_out_g.reshape(D, D)
    w_down4 = w_down_g.reshape(D_FF_SHARDS, d_ff_block, D)
    mix, x2, h2_t = _fwd_outproj(pool_out, attn_out, pool_scale, attn_scale, w_out_full, xs, g2, g3, big)
    upre, gate_val, dy, df, loss_cols, dg4 = _fwd_ffn_loss(h2_t, w_up_g, conv_w_g, conv_b_g, w_down4, x2, target, g4, small)
    loss = lax.psum(0.5 * jnp.sum(loss_cols) / D, ("x", "y", "c"))

    dupre_g, dupre_v, d_wd4, d_wup, d_cb, d_cw = _bwd_ffn_blocks(gate_val, upre, conv_w_g, w_down4, df, h2_t, min(1024, S))
    dx2, dmix, dg3, dg2 = _bwd_ffn_tokens(dupre_g, dupre_v, w_up_g, x2, dy, mix, g2, g3, big)
    d_pool, d_attn, d_wout, d_ps, d_as = _bwd_outproj(dmix, w_out_full, pool_out, attn_out, pool_scale, attn_scale, big)
    d_wdown_g = d_wd4.reshape(N_DEV, w_down.shape[1], D)
    d_wout_g = d_wout.reshape(N_DEV, D // N_DEV, D)
    dq, dk, dv, late_parts = _bwd_attn(qkv, d_attn, n_pairs, _Exchange([d_wout_g, d_wup, d_wdown_g, d_cw], gather=False),
                                       min(ATTN_BWD_BLOCKS, S // QB))
    dproj, d_win, d_wp = _bwd_pool_w_in(u, d_pool, w_pool[0], dq, dk, dv, h1_t, N_DEV, big)
    w_in_t = w_in_g.transpose(0, 2, 1).reshape(-1, D)
    dx, dg1, (win_parts,) = _bwd_x(dproj, w_in_t, xs, dx2, g1, big, _Exchange([d_win], gather=False))
    big_parts = [win_parts] + list(late_parts)
    r = dict(dx=dx, g1=dg1, w_pool=d_wp, pool_scale=d_ps, attn_scale=d_as, g2=dg2, g3=dg3, conv_b=d_cb, g4=dg4)

    small_names = ["norm_mix_pre", "w_pool", "pool_scale", "attn_scale", "norm_mix_post", "norm_ffn_pre", "conv_b", "norm_ffn_post"]
    small_w = dict(norm_mix_pre=norm_mix_pre, w_pool=w_pool, pool_scale=pool_scale, attn_scale=attn_scale,
                   norm_mix_post=norm_mix_post, norm_ffn_pre=norm_ffn_pre, conv_b=conv_b, norm_ffn_post=norm_ffn_post)
    small_m = dict(norm_mix_pre=m_norm_mix_pre, w_pool=m_w_pool, pool_scale=m_pool_scale, attn_scale=m_attn_scale,
                   norm_mix_post=m_norm_mix_post, norm_ffn_pre=m_norm_ffn_pre, conv_b=m_conv_b, norm_ffn_post=m_norm_ffn_post)
    small_v = dict(norm_mix_pre=v_norm_mix_pre, w_pool=v_w_pool, pool_scale=v_pool_scale, attn_scale=v_attn_scale,
                   norm_mix_post=v_norm_mix_post, norm_ffn_pre=v_norm_ffn_pre, conv_b=v_conv_b, norm_ffn_post=v_norm_ffn_post)
    small_g = dict(norm_mix_pre=r["g1"], w_pool=r["w_pool"], pool_scale=r["pool_scale"], attn_scale=r["attn_scale"],
                   norm_mix_post=r["g2"], norm_ffn_pre=r["g3"], conv_b=r["conv_b"], norm_ffn_post=r["g4"])
    like = [small_w[n] for n in small_names]
    packed_g = _pack_small([small_g[n] for n in small_names])

    (small_parts,) = _all_to_all([packed_g], gather=True, name="gather_small_grads")

    def update(parts, w, m, v):
        R, C = w.shape
        return _reduce_adamw(parts, w, m, v, _row_tile(R, C))

    res = {}
    res["w_in"] = update(big_parts[0], w_in[0], m_w_in[0], v_w_in[0])
    res["w_out"] = update(big_parts[1], w_out[0], m_w_out[0], v_w_out[0])
    res["w_up"] = update(big_parts[2], w_up[0], m_w_up[0], v_w_up[0])
    res["w_down"] = update(big_parts[3], w_down[0], m_w_down[0], v_w_down[0])
    res["conv_w"] = update(big_parts[4], conv_w[0], m_conv_w[0], v_conv_w[0])
    small_res = update(small_parts, _pack_small(like), _pack_small([small_m[n] for n in small_names]),
                       _pack_small([small_v[n] for n in small_names]))
    small_res = [_unpack_small(t, like) for t in small_res]
    for idx, n in enumerate(small_names):
        res[n] = tuple(t[idx] for t in small_res)

    order = ["norm_mix_pre", "w_in", "w_pool", "pool_scale", "attn_scale", "w_out", "norm_mix_post", "norm_ffn_pre",
             "w_up", "conv_w", "conv_b", "w_down", "norm_ffn_post"]
    shaped = {n: tuple(t.reshape(s.shape) for t in res[n])
              for n, s in dict(norm_mix_pre=norm_mix_pre, w_in=w_in, w_pool=w_pool, pool_scale=pool_scale, attn_scale=attn_scale,
                               w_out=w_out, norm_mix_post=norm_mix_post, norm_ffn_pre=norm_ffn_pre, w_up=w_up, conv_w=conv_w,
                               conv_b=conv_b, w_down=w_down, norm_ffn_post=norm_ffn_post).items()}
    outs = [loss, r["dx"].reshape(x.shape)]
    for k in range(4):
        outs += [shaped[n][k] for n in order]
    return tuple(outs)
```

```python
import functools

import jax
import jax.numpy as jnp
from jax import lax
from jax.experimental import pallas as pl
from jax.experimental.pallas import tpu as pltpu

F32 = jnp.float32
BF16 = jnp.bfloat16

N_DEV = 8
EPS = 1e-6
POOL_WINDOWS = (2, 4, 8, 16)
POOL_GROUP = 128
HALO = 16
HEAD_DIM = 64
QB = 128
ATTN_SCALE = HEAD_DIM ** -0.5
ATTN_FWD_BLOCKS = 16
ATTN_BWD_BLOCKS = 8
EXP_UNDERFLOW = -88.0
D_FF_SHARDS = 4

ADAM_LR = 0.001
ADAM_B1 = 0.9
ADAM_B2 = 0.999
ADAM_EPS = 1e-08
ADAM_WD = 0.01
ADAM_STEP = 10

VMEM_LIMIT_V7X = 56 * 1024 * 1024
MESH = pl.DeviceIdType.MESH


def _params(*semantics):
    return pltpu.CompilerParams(dimension_semantics=semantics, vmem_limit_bytes=VMEM_LIMIT_V7X)


def _const(shape):
    zeros = (0,) * len(shape)
    return pl.BlockSpec(shape, lambda *_: zeros, pipeline_mode=pl.Buffered(1))


def _dot(a, b):
    return jnp.dot(a, b, preferred_element_type=F32)


def _dot_nt(a, b):
    return lax.dot_general(a, b, (((1,), (1,)), ((), ())), preferred_element_type=F32)


def _dot_tn(a, b):
    return lax.dot_general(a, b, (((0,), (0,)), ((), ())), preferred_element_type=F32)


def _rms(v):
    return lax.rsqrt(jnp.mean(v * v, axis=-1, keepdims=True) + EPS)


def _norm_bwd(dn_times_gain, n, r):
    return r * (dn_times_gain - n * jnp.mean(dn_times_gain * n, axis=-1, keepdims=True))


def _zero_when(first, *refs):
    @pl.when(first)
    def _():
        for ref in refs:
            ref[...] = jnp.zeros_like(ref)


def _colsum(v):
    return jnp.sum(v, axis=0, keepdims=True)


def _grid_ends(grid):
    ids = [pl.program_id(a) for a in range(len(grid))]
    first = functools.reduce(jnp.logical_and, [i == 0 for i in ids])
    last = functools.reduce(jnp.logical_and, [i == n - 1 for i, n in zip(ids, grid)])
    return first, last


def _fwd_inproj_pool(x, g1, w_in_g, w_pool, tile, ex):
    S, D = x.shape
    nb, _, cs = w_in_g.shape
    d_pool = 2 * cs
    n_tiles = S // tile

    def body(x_ref, g_ref, w_ref, wp_ref, *rest):
        ht_ref, u_ref, qkv_ref, pool_ref = rest[ex.n:ex.n + 4]
        halo_ref = rest[2 * ex.n + 4]
        ex_refs = ex.split(rest[:ex.n] + rest[ex.n + 4:2 * ex.n + 4] + rest[2 * ex.n + 5:])
        i = pl.program_id(0)

        @pl.when(i == 0)
        def _():
            ex.start(*ex_refs)
            halo_ref[...] = jnp.zeros_like(halo_ref)

        xf = x_ref[...]
        h = (xf * _rms(xf) * g_ref[...]).astype(BF16)
        ht_ref[...] = h.T
        u = jnp.concatenate([_dot(h, w_ref[0]), _dot(h, w_ref[1])], axis=1)
        u_ref[...] = u
        for d in range(2, nb):
            qkv_ref[:, (d - 2) * cs:(d - 1) * cs] = _dot(h, w_ref[d]).astype(BF16)
        parts = _pool_deviation(u, halo_ref[...], i * tile)
        halo_ref[...] = u[tile - HALO:, :]
        for g, p in enumerate(parts):
            pool_ref[:, g * POOL_GROUP:(g + 1) * POOL_GROUP] = _dot(p.astype(BF16), wp_ref[g].astype(BF16))

        @pl.when(i == n_tiles - 1)
        def _():
            ex.wait(*ex_refs)

    row = lambda w: pl.BlockSpec((tile, w), lambda i: (i, 0))
    outs = pl.pallas_call(
        body, name="fwd_inproj_pool", grid=(n_tiles,),
        in_specs=[row(D), _const((1, D)), _const(w_in_g.shape), _const(w_pool.shape)] + ex.specs,
        out_specs=[pl.BlockSpec((D, tile), lambda i: (0, i)), row(d_pool), row(3 * d_pool), row(d_pool)] + ex.specs,
        out_shape=[jax.ShapeDtypeStruct((D, S), BF16), jax.ShapeDtypeStruct((S, d_pool), F32),
                   jax.ShapeDtypeStruct((S, 3 * d_pool), BF16), jax.ShapeDtypeStruct((S, d_pool), F32)] + ex.out_shape,
        scratch_shapes=[pltpu.VMEM((HALO, d_pool), F32)] + ex.scratch,
        compiler_params=_params("arbitrary"),
    )(x, g1, w_in_g, w_pool, *ex.arrays)
    return outs[0], outs[1], outs[2], outs[3], outs[4:]


def _window_sums(ext, forward):
    n = ext.shape[0]
    sums, s, sh = {}, ext, 1
    while sh < POOL_WINDOWS[-1]:
        s = s + pltpu.roll(s, (n - sh) if forward else sh, axis=0)
        sh *= 2
        sums[sh] = s
    return sums


def _pool_counts(t0, rows):
    t1 = (lax.broadcasted_iota(jnp.int32, (rows, 1), 0) + t0 + 1).astype(F32)
    return [jnp.minimum(t1, float(w)) for w in POOL_WINDOWS]


def _pool_deviation(u, halo, t0):
    T = u.shape[0]
    sums = _window_sums(jnp.concatenate([halo, u], axis=0), forward=False)
    counts = _pool_counts(t0, T)
    parts = []
    for g, w in enumerate(POOL_WINDOWS):
        lanes = slice(g * POOL_GROUP, (g + 1) * POOL_GROUP)
        parts.append(sums[w][HALO:, lanes] / counts[g] - u[:, lanes])
    return parts


def _prev_halo_spec(tile, width):
    return pl.BlockSpec((HALO, width), lambda i: (jnp.maximum(i * (tile // HALO) - 1, 0), 0))


def _next_halo_spec(tile, width, n_tiles):
    last = n_tiles * (tile // HALO) - 1
    return pl.BlockSpec((HALO, width), lambda i: (jnp.minimum((i + 1) * (tile // HALO), last), 0))


def _low_lanes():
    return lax.broadcasted_iota(jnp.int32, (QB, 2 * HEAD_DIM), 1) < HEAD_DIM


LOG_PIECES = 2
GRAD_PIECES = 2


def _triangle(inclusive, pieces):
    row = lax.broadcasted_iota(jnp.int32, (pieces * QB, QB), 0) % QB
    col = lax.broadcasted_iota(jnp.int32, (pieces * QB, QB), 1)
    return ((row >= col) if inclusive else (row > col)).astype(BF16)


def _pieces(v, n):
    out, rest = [], v
    for _ in range(n - 1):
        piece = rest.astype(BF16)
        out.append(piece)
        rest = rest - piece.astype(F32)
    out.append(rest.astype(BF16))
    return jnp.concatenate(out, axis=1)


def _causal_mask(width, offset):
    row = lax.broadcasted_iota(jnp.int32, (QB, width), 0)
    col = lax.broadcasted_iota(jnp.int32, (QB, width), 1)
    return col < row + offset


def _row_sums(vals, carry):
    for b in reversed(range(vals.shape[1] // QB)):
        carry = carry + jnp.sum(vals[:, b * QB:(b + 1) * QB], axis=1, keepdims=True)
    return carry


def _suffix_sums(vals, tri, carry):
    n = vals.shape[1] // QB
    out, run = [None] * n, carry
    for b in reversed(range(n)):
        blk = vals[:, b * QB:(b + 1) * QB]
        out[b] = _dot(_pieces(blk, tri.shape[0] // QB), tri) + run
        run = run + jnp.sum(blk, axis=1, keepdims=True)
    return (out[0] if n == 1 else jnp.concatenate(out, axis=1)), run


def _attn_tiles(qhs, kws, masks, carries, after_s):
    return _attn_weights(_attn_scores(qhs, kws, masks), masks, carries, after_s)


def _attn_scores(qhs, kws, masks):
    zs = [_dot_nt(qh, kw) * ATTN_SCALE for qh, kw in zip(qhs, kws)]
    es = [jnp.exp(-jnp.abs(z)) for z in zs]
    softplus = [jnp.maximum(z, 0.0) + jnp.log(1.0 + e) for z, e in zip(zs, es)]
    log_1m_beta = [-sp if m is None else jnp.where(m, -sp, 0.0) for sp, m in zip(softplus, masks)]
    return list(zip(zs, es, softplus, log_1m_beta))


def _attn_weights(scores, masks, carries, after_s):
    sums = [_suffix_sums(l, after_s, c) for (_, _, _, l), c in zip(scores, carries)]
    weights = [jnp.exp(z - sp + st) for (z, _, sp, _), (st, _) in zip(scores, sums)]
    weights = [a if m is None else jnp.where(m, a, 0.0) for a, m in zip(weights, masks)]
    return [(z, e, a, c) for (z, e, _, _), a, (_, c) in zip(scores, weights, sums)]


def _split_heads(v, low_lanes):
    return jnp.where(low_lanes, v, 0.0).astype(BF16), jnp.where(low_lanes, 0.0, v).astype(BF16)


def _sweep_done(c0, c1):
    return (jnp.maximum(jnp.max(c0), jnp.max(c1)) < EXP_UNDERFLOW).astype(jnp.int32)


def _all_done(carries):
    return jnp.max(functools.reduce(jnp.maximum, carries)) < EXP_UNDERFLOW


def _first_window(i):
    first_blk = jnp.maximum(i - 1, 0)
    return first_blk, pl.multiple_of(first_blk * QB, QB), (i - first_blk) * QB


def _fwd_attn(qkv, n_pairs, ex, subs):
    S = qkv.shape[0]
    n_steps = S // (subs * QB)

    def body(q_ref, k_ref, v_ref, *rest):
        o_ref = rest[ex.n]
        ex_refs = ex.split(rest[:ex.n] + rest[ex.n + 1:])
        first_step, last_step = _grid_ends((n_pairs, n_steps))

        @pl.when(first_step)
        def _():
            ex.start(*ex_refs)

        low_lanes = _low_lanes()
        after_s = _triangle(False, LOG_PIECES)
        zero = jnp.zeros((QB, 1), F32)

        def cond(c):
            return jnp.logical_and(c[0] >= 0, c[1] == 0)

        qhs, kws, vws, masks, first_blks = [], [], [], [], []
        for sub in range(subs):
            i = pl.program_id(1) * subs + sub
            first_blk, start, offset = _first_window(i)
            first_blks.append(first_blk)
            qhs += _split_heads(q_ref[sub * QB:(sub + 1) * QB, :].astype(F32), low_lanes)
            kws += [k_ref[pl.ds(start, 2 * QB), :]] * 2
            vws += [v_ref[pl.ds(start, 2 * QB), :]] * 2
            masks += [_causal_mask(2 * QB, offset)] * 2
        tiles = _attn_tiles(qhs, kws, masks, [zero] * len(qhs), after_s)
        outs = [_dot(t[2].astype(BF16), vw) for t, vw in zip(tiles, vws)]

        first_out = [jnp.where(low_lanes, outs[2 * sub], outs[2 * sub + 1]) for sub in range(subs)]

        def sweep_on():
            final = []
            for sub in range(subs):
                def step(c, qh=qhs[2 * sub:2 * sub + 2]):
                    j, _, acc, c0, c1 = c
                    at = pl.multiple_of(j * QB, QB)
                    kb = k_ref[pl.ds(at, QB), :]
                    vb = v_ref[pl.ds(at, QB), :]
                    far = _attn_tiles(qh, [kb, kb], [None, None], [c0, c1], after_s)
                    acc = acc + jnp.where(low_lanes, _dot(far[0][2].astype(BF16), vb), _dot(far[1][2].astype(BF16), vb))
                    return j - 1, _sweep_done(far[0][3], far[1][3]), acc, far[0][3], far[1][3]

                c0, c1 = tiles[2 * sub][3], tiles[2 * sub + 1][3]
                final.append(lax.while_loop(cond, step, (first_blks[sub] - 1, _sweep_done(c0, c1), first_out[sub], c0, c1))[2])
            return tuple(final)

        final = lax.cond(_all_done([t[3] for t in tiles]), lambda: tuple(first_out), sweep_on)
        for sub in range(subs):
            o_ref[sub * QB:(sub + 1) * QB, :] = final[sub]

        @pl.when(last_step)
        def _():
            ex.wait(*ex_refs)

    outs = pl.pallas_call(
        body, name="fwd_attn", grid=(n_pairs, n_steps),
        in_specs=[pl.BlockSpec((subs * QB, QB), lambda p, i: (i, p)),
                  pl.BlockSpec((S, QB), lambda p, i: (0, n_pairs + p), pipeline_mode=pl.Buffered(1)),
                  pl.BlockSpec((S, QB), lambda p, i: (0, 2 * n_pairs + p), pipeline_mode=pl.Buffered(1))] + ex.specs,
        out_specs=[pl.BlockSpec((subs * QB, QB), lambda p, i: (i, p))] + ex.specs,
        out_shape=[jax.ShapeDtypeStruct((S, n_pairs * QB), F32)] + ex.out_shape,
        scratch_shapes=ex.scratch,
        compiler_params=_params("arbitrary", "arbitrary"),
    )(qkv, qkv, qkv, *ex.arrays)
    return outs[0], outs[1:]


def _normalized_heads(pool_out, attn_out):
    rp, ra = _rms(pool_out), _rms(attn_out)
    return pool_out * rp, rp, attn_out * ra, ra


def _fwd_outproj(pool_out, attn_out, pool_scale, attn_scale, w_out, x, g2, g3, tile):
    S, D = x.shape
    C = pool_out.shape[1]

    def body(p_ref, a_ref, ps_ref, as_ref, w_ref, x_ref, g2_ref, g3_ref, mix_ref, x2_ref, h2t_ref):
        n_p, _, n_a, _ = _normalized_heads(p_ref[...], a_ref[...])
        mix = _dot((n_p * ps_ref[...]).astype(BF16), w_ref[:C, :]) + _dot((n_a * as_ref[...]).astype(BF16), w_ref[C:, :])
        mix_ref[...] = mix
        x2 = x_ref[...] + mix * _rms(mix) * g2_ref[...]
        x2_ref[...] = x2
        h2t_ref[...] = (x2 * _rms(x2) * g3_ref[...]).astype(BF16).T

    row = lambda w: pl.BlockSpec((tile, w), lambda i: (i, 0))
    return pl.pallas_call(
        body, name="fwd_outproj", grid=(S // tile,),
        in_specs=[row(C), row(C), _const((1, C)), _const((1, C)), _const(w_out.shape), row(D), _const((1, D)), _const((1, D))],
        out_specs=[row(D), row(D), pl.BlockSpec((D, tile), lambda i: (0, i))],
        out_shape=[jax.ShapeDtypeStruct((S, D), F32), jax.ShapeDtypeStruct((S, D), F32), jax.ShapeDtypeStruct((D, S), BF16)],
        compiler_params=_params("parallel"),
    )(pool_out, attn_out, pool_scale, attn_scale, w_out, x, g2, g3)


def _conv_taps(tile_rows, halo_rows):
    T = tile_rows.shape[0]
    ext = jnp.concatenate([halo_rows.astype(F32), tile_rows.astype(F32)], axis=0)
    return pltpu.roll(ext, 2, axis=0)[HALO:], pltpu.roll(ext, 1, axis=0)[HALO:], ext[HALO:]


def _tap_rows(cw_ref, d):
    return [cw_ref[d, k:k + 1, :] for k in range(3)]


def _gated_unit(taps_gate, taps_val, cw_gate, cw_val, cb_gate, cb_val):
    gate = cw_gate[0] * taps_gate[0] + cw_gate[1] * taps_gate[1] + cw_gate[2] * taps_gate[2] + cb_gate
    val = cw_val[0] * taps_val[0] + cw_val[1] * taps_val[1] + cw_val[2] * taps_val[2] + cb_val
    sig = 1.0 / (1.0 + jnp.exp(-gate))
    return gate, val, sig


def _fwd_ffn_loss(h2_t, w_up_g, conv_w_g, conv_b_g, w_down4, x2, target, g4, tile):
    S, D = x2.shape
    nb, _, cs = w_up_g.shape
    half = D_FF_SHARDS

    def body(h_ref, w_ref, cw_ref, cb_ref, wd_ref, x2_ref, t_ref, g4_ref, upre_ref, gv_ref, dy_ref, df_ref, loss_ref, dg4_ref, halo_ref):
        _zero_when(pl.program_id(0) == 0, loss_ref, dg4_ref, halo_ref)
        h = h_ref[...].T

        def up(s):
            return _dot(h, w_ref[s]), _dot(h, w_ref[s + half])

        f = jnp.zeros((tile, D), F32)
        ahead = up(0)
        for s in range(half):
            ug, uv = ahead
            if s + 1 < half:
                ahead = up(s + 1)
            upre_ref[s] = ug.astype(BF16)
            upre_ref[s + half] = uv.astype(BF16)
            gate, val, sig = _gated_unit(_conv_taps(ug, halo_ref[s]), _conv_taps(uv, halo_ref[s + half]),
                                         _tap_rows(cw_ref, s), _tap_rows(cw_ref, s + half), cb_ref[s], cb_ref[s + half])
            halo_ref[s] = ug[tile - HALO:, :]
            halo_ref[s + half] = uv[tile - HALO:, :]
            gv_ref[s] = gate.astype(BF16)
            gv_ref[s + half] = val.astype(BF16)
            f = f + _dot((gate * sig * val).astype(BF16), wd_ref[s])
        r4 = _rms(f)
        n4 = f * r4
        err = x2_ref[...] + n4 * g4_ref[...] - t_ref[...]
        dy = err * (1.0 / D)
        dy_ref[...] = dy
        df_ref[...] = _norm_bwd(dy * g4_ref[...], n4, r4).astype(BF16)
        loss_ref[...] += _colsum(err * err)
        dg4_ref[...] += _colsum(dy * n4)

    row = lambda w: pl.BlockSpec((tile, w), lambda i: (i, 0))
    return pl.pallas_call(
        body, name="fwd_ffn_loss", grid=(S // tile,),
        in_specs=[pl.BlockSpec((D, tile), lambda i: (0, i)), _const(w_up_g.shape), _const(conv_w_g.shape), _const(conv_b_g.shape),
                  _const(w_down4.shape), row(D), row(D), _const((1, D))],
        out_specs=[pl.BlockSpec((nb, tile, cs), lambda i: (0, i, 0)), pl.BlockSpec((nb, tile, cs), lambda i: (0, i, 0)), row(D), row(D),
                   pl.BlockSpec((1, D), lambda i: (0, 0)), pl.BlockSpec((1, D), lambda i: (0, 0))],
        out_shape=[jax.ShapeDtypeStruct((nb, S, cs), BF16), jax.ShapeDtypeStruct((nb, S, cs), BF16),
                   jax.ShapeDtypeStruct((S, D), F32), jax.ShapeDtypeStruct((S, D), BF16),
                   jax.ShapeDtypeStruct((1, D), F32), jax.ShapeDtypeStruct((1, D), F32)],
        scratch_shapes=[pltpu.VMEM((nb, HALO, cs), F32)],
        compiler_params=_params("arbitrary"),
    )(h2_t, w_up_g, conv_w_g, conv_b_g, w_down4, x2, target, g4)


def _bwd_ffn_blocks(gate_val, upre, conv_w_g, w_down4, df, h2_t, tile):
    nb, S, cs = upre.shape
    D = df.shape[1]
    n_tiles = S // tile
    half = D_FF_SHARDS

    def body(g_ref, v_ref, ug_ref, uv_ref, cwg_ref, cwv_ref, wd_ref, df_ref, ht_ref,
             dug_ref, duv_ref, dwd_ref, dwg_ref, dwv_ref, dbg_ref, dbv_ref, dcwg_ref, dcwv_ref, next_ref):
        _zero_when(pl.program_id(1) == 0, dwd_ref, dwg_ref, dwv_ref, dbg_ref, dbv_ref, dcwg_ref, dcwv_ref, next_ref)
        dfb = df_ref[...]
        dact = _dot_nt(dfb, wd_ref[0])
        gate, val = g_ref[0].astype(F32), v_ref[0].astype(F32)
        sig = 1.0 / (1.0 + jnp.exp(-gate))
        silu = gate * sig
        dwd_ref[0] += _dot_tn((silu * val).astype(BF16), dfb)
        ht = ht_ref[...]

        def through_conv(dup, slot, cw_ref, u_ref, du_ref, dw_ref, db_ref, dcw_ref):
            ext = jnp.concatenate([dup, next_ref[slot]], axis=0)
            n = ext.shape[0]
            shifted = (dup, pltpu.roll(ext, n - 1, axis=0)[:tile], pltpu.roll(ext, n - 2, axis=0)[:tile])
            next_ref[slot] = dup[:HALO]
            cw = _tap_rows(cw_ref, 0)
            dupre = (cw[2] * shifted[0] + cw[1] * shifted[1] + cw[0] * shifted[2]).astype(BF16)
            du_ref[0] = dupre
            dw_ref[0] += _dot(ht, dupre)
            u = u_ref[0].astype(F32)
            db_ref[0] += _colsum(dup)
            for k in range(3):
                dcw_ref[0, k:k + 1, :] += _colsum(shifted[2 - k] * u)

        through_conv(dact * (val * (sig + silu * (1.0 - sig))), 0, cwg_ref, ug_ref, dug_ref, dwg_ref, dbg_ref, dcwg_ref)
        through_conv(dact * silu, 1, cwv_ref, uv_ref, duv_ref, dwv_ref, dbv_ref, dcwv_ref)

    rev = lambda i: n_tiles - 1 - i
    blk = lambda off: pl.BlockSpec((1, tile, cs), lambda s, i: (s + off, rev(i), 0))
    par = lambda off, r: pl.BlockSpec((1, r, cs), lambda s, i: (s + off, 0, 0))
    acc = lambda r, c: pl.BlockSpec((1, r, c), lambda s, i: (s, 0, 0), pipeline_mode=pl.Buffered(1))
    outs = pl.pallas_call(
        body, name="bwd_ffn_blocks", grid=(half, n_tiles),
        in_specs=[blk(0), blk(half), blk(0), blk(half), par(0, 3), par(half, 3),
                  acc(cs, D), pl.BlockSpec((tile, D), lambda s, i: (rev(i), 0)), pl.BlockSpec((D, tile), lambda s, i: (0, rev(i)))],
        out_specs=[blk(0), blk(0), acc(cs, D), acc(D, cs), acc(D, cs), acc(1, cs), acc(1, cs), acc(3, cs), acc(3, cs)],
        out_shape=[jax.ShapeDtypeStruct((half, S, cs), BF16), jax.ShapeDtypeStruct((half, S, cs), BF16),
                   jax.ShapeDtypeStruct((half, cs, D), F32),
                   jax.ShapeDtypeStruct((half, D, cs), F32), jax.ShapeDtypeStruct((half, D, cs), F32),
                   jax.ShapeDtypeStruct((half, 1, cs), F32), jax.ShapeDtypeStruct((half, 1, cs), F32),
                   jax.ShapeDtypeStruct((half, 3, cs), F32), jax.ShapeDtypeStruct((half, 3, cs), F32)],
        scratch_shapes=[pltpu.VMEM((2, HALO, cs), F32)],
        compiler_params=_params("arbitrary", "arbitrary"),
    )(gate_val, gate_val, upre, upre, conv_w_g, conv_w_g, w_down4, df, h2_t)
    dupre_g, dupre_v, d_wd, d_wg, d_wv, dbg, dbv, dcwg, dcwv = outs
    return (dupre_g, dupre_v, d_wd, jnp.concatenate([d_wg, d_wv], axis=0), jnp.concatenate([dbg, dbv], axis=0),
            jnp.concatenate([dcwg, dcwv], axis=0))


def _bwd_ffn_tokens(dupre_g, dupre_v, w_up_g, x2, dy, mix, g2, g3, tile):
    half, S, cs = dupre_g.shape
    D = x2.shape[1]

    def body(dg_ref, dv_ref, w_ref, x2_ref, dy_ref, mix_ref, g2_ref, g3_ref, dx2_ref, dmix_ref, dg3_ref, dg2_ref):
        _zero_when(pl.program_id(0) == 0, dg3_ref, dg2_ref)
        parts = [_dot_nt(dg_ref[d], w_ref[d]) for d in range(half)] + [_dot_nt(dv_ref[d], w_ref[d + half]) for d in range(half)]
        while len(parts) > 1:
            parts = [a + b for a, b in zip(parts[::2], parts[1::2])]
        dh2 = parts[0]
        x2 = x2_ref[...]
        r3 = _rms(x2)
        n3 = x2 * r3
        dg3_ref[...] += _colsum(dh2 * n3)
        dx2 = dy_ref[...] + _norm_bwd(dh2 * g3_ref[...], n3, r3)
        dx2_ref[...] = dx2
        mix = mix_ref[...]
        r2 = _rms(mix)
        n2 = mix * r2
        dg2_ref[...] += _colsum(dx2 * n2)
        dmix_ref[...] = _norm_bwd(dx2 * g2_ref[...], n2, r2).astype(BF16)

    row = lambda w: pl.BlockSpec((tile, w), lambda i: (i, 0))
    blk = pl.BlockSpec((half, tile, cs), lambda i: (0, i, 0))
    acc = pl.BlockSpec((1, D), lambda i: (0, 0))
    return pl.pallas_call(
        body, name="bwd_ffn_tokens", grid=(S // tile,),
        in_specs=[blk, blk, _const(w_up_g.shape), row(D), row(D), row(D), _const((1, D)), _const((1, D))],
        out_specs=[row(D), row(D), acc, acc],
        out_shape=[jax.ShapeDtypeStruct((S, D), F32), jax.ShapeDtypeStruct((S, D), BF16),
                   jax.ShapeDtypeStruct((1, D), F32), jax.ShapeDtypeStruct((1, D), F32)],
        compiler_params=_params("arbitrary"),
    )(dupre_g, dupre_v, w_up_g, x2, dy, mix, g2, g3)


def _bwd_outproj(dmix, w_out, pool_out, attn_out, pool_scale, attn_scale, tile):
    S, D = dmix.shape
    C = pool_out.shape[1]

    def body(dm_ref, w_ref, p_ref, a_ref, ps_ref, as_ref, dp_ref, da_ref, dw_ref, dps_ref, das_ref):
        _zero_when(pl.program_id(0) == 0, dw_ref, dps_ref, das_ref)
        dmx = dm_ref[...]
        dmerged = _dot_nt(dmx, w_ref[...])
        n_p, r_p, n_a, r_a = _normalized_heads(p_ref[...], a_ref[...])
        merged = jnp.concatenate([(n_p * ps_ref[...]).astype(BF16), (n_a * as_ref[...]).astype(BF16)], axis=1)
        dw_ref[...] += _dot_tn(merged, dmx)
        dm_p, dm_a = dmerged[:, :C], dmerged[:, C:]
        dps_ref[...] += _colsum(dm_p * n_p)
        das_ref[...] += _colsum(dm_a * n_a)
        dp_ref[...] = _norm_bwd(dm_p * ps_ref[...], n_p, r_p)
        da_ref[...] = _norm_bwd(dm_a * as_ref[...], n_a, r_a)

    row = lambda w: pl.BlockSpec((tile, w), lambda i: (i, 0))
    return pl.pallas_call(
        body, name="bwd_outproj", grid=(S // tile,),
        in_specs=[row(D), _const(w_out.shape), row(C), row(C), _const((1, C)), _const((1, C))],
        out_specs=[row(C), row(C), pl.BlockSpec(w_out.shape, lambda i: (0, 0)),
                   pl.BlockSpec((1, C), lambda i: (0, 0)), pl.BlockSpec((1, C), lambda i: (0, 0))],
        out_shape=[jax.ShapeDtypeStruct((S, C), F32), jax.ShapeDtypeStruct((S, C), F32),
                   jax.ShapeDtypeStruct(w_out.shape, F32), jax.ShapeDtypeStruct((1, C), F32), jax.ShapeDtypeStruct((1, C), F32)],
        compiler_params=_params("arbitrary"),
    )(dmix, w_out, pool_out, attn_out, pool_scale, attn_scale)


def _bwd_attn(qkv, d_attn, n_pairs, ex, subs):
    S = qkv.shape[0]
    n_steps = S // (subs * QB)

    def body(q_ref, k_ref, v_ref, do_ref, *rest):
        dq_ref, dk_ref, dv_ref = rest[ex.n:ex.n + 3]
        ex_refs = ex.split(rest[:ex.n] + rest[ex.n + 3:])
        first_step, last_step = _grid_ends((n_pairs, n_steps))

        @pl.when(first_step)
        def _():
            ex.start(*ex_refs)

        @pl.when(pl.program_id(1) == 0)
        def _():
            dk_ref[...] = jnp.zeros_like(dk_ref)
            dv_ref[...] = jnp.zeros_like(dv_ref)

        low_lanes = _low_lanes()
        after_s, from_s = _triangle(False, LOG_PIECES), _triangle(True, GRAD_PIECES)
        zero = jnp.zeros((QB, 1), F32)

        def tiles(qhs, dohs, totals, kws, vws, masks, cs, gs, scores=None):
            fw = _attn_weights(scores or _attn_scores(qhs, kws, masks), masks, cs, after_s)
            gvals = [t[2] * _dot_nt(doh, vw) for t, doh, vw in zip(fw, dohs, vws)]
            sums = [_suffix_sums(g, from_s, g0) for g, g0 in zip(gvals, gs)]
            totals = [tot if m is None else tot + sm[1] for tot, m, sm in zip(totals, masks, sums)]
            dzs = []
            for (z, e, _, _), g, (nearer, _), tot, m in zip(fw, gvals, sums, totals, masks):
                inv = 1.0 / (1.0 + e)
                sig_abs, sig_neg = inv, e * inv
                pos = z >= 0.0
                dz = g * jnp.where(pos, sig_neg, sig_abs) - jnp.where(pos, sig_abs, sig_neg) * (tot - nearer)
                if m is not None:
                    dz = jnp.where(m, dz, 0.0)
                dzs.append((dz * ATTN_SCALE).astype(BF16))
            dqs = [_dot(dz, kw) for dz, kw in zip(dzs, kws)]
            dks = [_dot_tn(dz, qh) for dz, qh in zip(dzs, qhs)]
            dvs = [_dot_tn(t[2].astype(BF16), doh) for t, doh in zip(fw, dohs)]
            return [(dq, dk, dv, t[3], sm[1], tot) for dq, dk, dv, t, sm, tot in zip(dqs, dks, dvs, fw, sums, totals)]

        def cond(c):
            return jnp.logical_and(c[0] >= 0, c[1] == 0)

        qhs, dohs, kws, vws, masks, first_blks, starts = [], [], [], [], [], [], []
        for sub in range(subs):
            i = pl.program_id(1) * subs + sub
            rows = slice(sub * QB, (sub + 1) * QB)
            first_blk, start, offset = _first_window(i)
            first_blks.append(first_blk)
            starts.append(start)
            qhs += _split_heads(q_ref[rows, :].astype(F32), low_lanes)
            dohs += _split_heads(do_ref[rows, :], low_lanes)
            kws += [k_ref[pl.ds(start, 2 * QB), :]] * 2
            vws += [v_ref[pl.ds(start, 2 * QB), :]] * 2
            masks += [_causal_mask(2 * QB, offset)] * 2
        zeros = [zero] * len(qhs)

        scores = _attn_scores(qhs, kws, masks)
        c_first = [_row_sums(sc[3], zero) for sc in scores]
        all_done = _all_done(c_first)

        def far_totals():
            beyond = []
            for sub in range(subs):
                pair = slice(2 * sub, 2 * sub + 2)

                def far_sums(c, qh=qhs[pair], doh=dohs[pair]):
                    j, _, c0, c1, r0, r1 = c
                    at = pl.multiple_of(j * QB, QB)
                    kb = k_ref[pl.ds(at, QB), :]
                    vb = v_ref[pl.ds(at, QB), :]
                    far = _attn_tiles(qh, [kb, kb], [None, None], [c0, c1], after_s)
                    r0 = r0 + jnp.sum(far[0][2] * _dot_nt(doh[0], vb), axis=1, keepdims=True)
                    r1 = r1 + jnp.sum(far[1][2] * _dot_nt(doh[1], vb), axis=1, keepdims=True)
                    return j - 1, _sweep_done(far[0][3], far[1][3]), far[0][3], far[1][3], r0, r1

                c0, c1 = c_first[pair]
                far = lax.while_loop(cond, far_sums, (first_blks[sub] - 1, _sweep_done(c0, c1), c0, c1, zero, zero))
                beyond += [far[4], far[5]]
            return tuple(beyond)

        beyond_first = list(lax.cond(all_done, lambda: tuple(zeros), far_totals))
        done = tiles(qhs, dohs, beyond_first, kws, vws, masks, zeros, zeros, scores)
        for sub in range(subs):
            dk_ref[pl.ds(starts[sub], 2 * QB), :] += done[2 * sub][1] + done[2 * sub + 1][1]
            dv_ref[pl.ds(starts[sub], 2 * QB), :] += done[2 * sub][2] + done[2 * sub + 1][2]
        first_dq = [jnp.where(low_lanes, done[2 * sub][0], done[2 * sub + 1][0]) for sub in range(subs)]

        def sweep_on():
            final = []
            for sub in range(subs):
                pair = slice(2 * sub, 2 * sub + 2)
                t0, t1 = done[pair]

                def step(c, qh=qhs[pair], doh=dohs[pair], total=[t0[5], t1[5]]):
                    j, _, dq, c0, c1, s0, s1 = c
                    at = pl.multiple_of(j * QB, QB)
                    kb = k_ref[pl.ds(at, QB), :]
                    vb = v_ref[pl.ds(at, QB), :]
                    f0, f1 = tiles(qh, doh, total, [kb, kb], [vb, vb], [None, None], [c0, c1], [s0, s1])
                    dk_ref[pl.ds(at, QB), :] += f0[1] + f1[1]
                    dv_ref[pl.ds(at, QB), :] += f0[2] + f1[2]
                    return j - 1, _sweep_done(f0[3], f1[3]), dq + jnp.where(low_lanes, f0[0], f1[0]), f0[3], f1[3], f0[4], f1[4]

                init = (first_blks[sub] - 1, _sweep_done(t0[3], t1[3]), first_dq[sub], t0[3], t1[3], t0[4], t1[4])
                final.append(lax.while_loop(cond, step, init)[2])
            return tuple(final)

        final = lax.cond(all_done, lambda: tuple(first_dq), sweep_on)
        for sub in range(subs):
            dq_ref[sub * QB:(sub + 1) * QB, :] = final[sub]

        @pl.when(last_step)
        def _():
            ex.wait(*ex_refs)

    blk = pl.BlockSpec((subs * QB, QB), lambda p, i: (i, p))
    full = lambda off: pl.BlockSpec((S, QB), lambda p, i: (0, off + p), pipeline_mode=pl.Buffered(1))
    outs = pl.pallas_call(
        body, name="bwd_attn", grid=(n_pairs, n_steps),
        in_specs=[blk, full(n_pairs), full(2 * n_pairs), blk] + ex.specs,
        out_specs=[blk, full(0), full(0)] + ex.specs,
        out_shape=[jax.ShapeDtypeStruct((S, n_pairs * QB), F32)] * 3 + ex.out_shape,
        scratch_shapes=ex.scratch,
        compiler_params=_params("arbitrary", "arbitrary"),
    )(qkv, qkv, qkv, d_attn, *ex.arrays)
    return outs[0], outs[1], outs[2], outs[3:]


def _bwd_pool_w_in(u, d_pool, w_pool, dq, dk, dv, h1_t, n_blocks, tile):
    S, C = u.shape
    D = h1_t.shape[0]
    n_tiles = S // tile
    ng = len(POOL_WINDOWS)
    cs = 4 * C // n_blocks
    per = C // cs

    def body(u_ref, uh_ref, d_ref, dh_ref, wp_ref, dq_ref, dk_ref, dv_ref, ht_ref, dproj_ref, dw_ref, dwp_ref):
        i = pl.program_id(0)
        first = i == 0
        _zero_when(first, dw_ref, dwp_ref)
        ht = ht_ref[...]
        for d in range(per, n_blocks):
            src = (dq_ref, dk_ref, dv_ref)[d // per - 1]
            dproj = src[:, (d % per) * cs:(d % per + 1) * cs].astype(BF16)
            dproj_ref[:, d * cs:(d + 1) * cs] = dproj
            dw_ref[d] += _dot(ht, dproj)
        halo = jnp.where(first, 0.0, uh_ref[...])
        parts = _pool_deviation(u_ref[...], halo, i * tile)
        dout = d_ref[...]
        nxt = jnp.where(i == n_tiles - 1, 0.0, dh_ref[...])
        dext = jnp.concatenate([dout, nxt], axis=0).astype(BF16)
        counts = _pool_counts(i * tile, tile + HALO)
        dps, scaled = [], []
        for g in range(ng):
            lanes = slice(g * POOL_GROUP, (g + 1) * POOL_GROUP)
            dp = _dot_nt(dext[:, lanes], wp_ref[g].astype(BF16))
            dps.append(dp[:tile])
            scaled.append(dp / counts[g])
        sums = _window_sums(jnp.concatenate(scaled, axis=1), forward=True)
        du = []
        for g, w in enumerate(POOL_WINDOWS):
            lanes = slice(g * POOL_GROUP, (g + 1) * POOL_GROUP)
            du.append((sums[w][:tile, lanes] - dps[g]).astype(BF16))
            dwp_ref[g] += _dot_tn(parts[g].astype(BF16), dext[:tile, lanes])
        du = jnp.concatenate(du, axis=1)
        for d in range(per):
            dproj = du[:, d * cs:(d + 1) * cs]
            dproj_ref[:, d * cs:(d + 1) * cs] = dproj
            dw_ref[d] += _dot(ht, dproj)

    row = pl.BlockSpec((tile, C), lambda i: (i, 0))
    return pl.pallas_call(
        body, name="bwd_pool_w_in", grid=(n_tiles,),
        in_specs=[row, _prev_halo_spec(tile, C), row, _next_halo_spec(tile, C, n_tiles), _const(w_pool.shape),
                  row, row, row, pl.BlockSpec((D, tile), lambda i: (0, i))],
        out_specs=[pl.BlockSpec((tile, 4 * C), lambda i: (i, 0)), pl.BlockSpec((n_blocks, D, cs), lambda i: (0, 0, 0)),
                   pl.BlockSpec(w_pool.shape, lambda i: (0, 0, 0))],
        out_shape=[jax.ShapeDtypeStruct((S, 4 * C), BF16), jax.ShapeDtypeStruct((n_blocks, D, cs), F32),
                   jax.ShapeDtypeStruct(w_pool.shape, F32)],
        compiler_params=_params("arbitrary"),
    )(u, u, d_pool, d_pool, w_pool, dq, dk, dv, h1_t)


def _bwd_x(dproj, w_in_t, x, dx2, g1, tile, ex):
    S, D = x.shape
    n_tiles = S // tile

    def body(dp_ref, w_ref, x_ref, dx2_ref, g_ref, *rest):
        dx_ref, dg_ref = rest[ex.n:ex.n + 2]
        ex_refs = ex.split(rest[:ex.n] + rest[ex.n + 2:])
        first, last = _grid_ends((n_tiles,))

        @pl.when(first)
        def _():
            ex.start(*ex_refs)
            dg_ref[...] = jnp.zeros_like(dg_ref)

        dh = _dot(dp_ref[...], w_ref[...])
        xf = x_ref[...]
        r1 = _rms(xf)
        n1 = xf * r1
        dg_ref[...] += _colsum(dh * n1)
        dx_ref[...] = dx2_ref[...] + _norm_bwd(dh * g_ref[...], n1, r1)

        @pl.when(last)
        def _():
            ex.wait(*ex_refs)

    row = lambda w: pl.BlockSpec((tile, w), lambda i: (i, 0))
    outs = pl.pallas_call(
        body, name="bwd_x", grid=(n_tiles,),
        in_specs=[row(w_in_t.shape[0]), _const(w_in_t.shape), row(D), row(D), _const((1, D))] + ex.specs,
        out_specs=[row(D), pl.BlockSpec((1, D), lambda i: (0, 0))] + ex.specs,
        out_shape=[jax.ShapeDtypeStruct((S, D), F32), jax.ShapeDtypeStruct((1, D), F32)] + ex.out_shape,
        scratch_shapes=ex.scratch,
        compiler_params=_params("arbitrary"),
    )(dproj, w_in_t, x, dx2, g1, *ex.arrays)
    return outs[0], outs[1], outs[2:]


def _mesh_position():
    x, y, c = lax.axis_index("x"), lax.axis_index("y"), lax.axis_index("c")
    return x, y, c, 4 * x + 2 * y + c


def _peer(x, y, c, k):
    px = 1 - x if k & 4 else x
    py = 1 - y if k & 2 else y
    pc = 1 - c if k & 1 else c
    return (px, py, pc), 4 * px + 2 * py + pc


class _Exchange:
    def __init__(self, arrays, gather):
        self.arrays, self.gather, self.n = list(arrays), gather, len(arrays)
        self.out_shape = [jax.ShapeDtypeStruct(((N_DEV,) + a.shape) if gather else a.shape, a.dtype) for a in arrays]
        self.specs = [pl.BlockSpec(memory_space=pl.ANY)] * self.n
        copies = self.n * (N_DEV - 1)
        self.scratch = [pltpu.SemaphoreType.DMA((copies,)), pltpu.SemaphoreType.DMA((copies,)),
                        pltpu.SemaphoreType.DMA((self.n,))]

    def _copies(self, ins, outs, sems):
        send_sems, recv_sems, local_sems = sems
        x, y, c, me = _mesh_position()
        local, remote = [], []
        for a in range(self.n):
            mine = ins[a] if self.gather else ins[a].at[me]
            local.append(pltpu.make_async_copy(mine, outs[a].at[me], local_sems.at[a]))
            for k in range(1, N_DEV):
                peer, peer_idx = _peer(x, y, c, k)
                src = ins[a] if self.gather else ins[a].at[peer_idx]
                sem = a * (N_DEV - 1) + k - 1
                remote.append(pltpu.make_async_remote_copy(
                    src_ref=src, dst_ref=outs[a].at[me], send_sem=send_sems.at[sem], recv_sem=recv_sems.at[sem],
                    device_id=peer, device_id_type=MESH))
        return local, remote

    def start(self, ins, outs, sems):
        local, remote = self._copies(ins, outs, sems)
        for cp in local + remote:
            cp.start()

    def wait(self, ins, outs, sems):
        local, remote = self._copies(ins, outs, sems)
        for cp in remote:
            cp.wait_send()
        for cp in remote:
            cp.wait_recv()
        for cp in local:
            cp.wait()

    def split(self, refs):
        return refs[:self.n], refs[self.n:2 * self.n], refs[2 * self.n:]


class _ChipGather(_Exchange):
    def __init__(self, arrays):
        super().__init__(arrays, gather=True)

    def _plan(self, ins, outs, sems, waiting):
        send_sems, recv_sems, local_sems = sems
        x, y, c, me = _mesh_position()
        sibling = (x, y, 1 - c)
        chips = [(1 - x, y), (x, 1 - y), (1 - x, 1 - y)]
        local, first, passed, arrivals = [], [], [], []
        for a in range(self.n):
            def copy(k, block, to, src=None, a=a):
                rows = outs[a].at[block]
                return pltpu.make_async_remote_copy(
                    src_ref=rows if src is None else src, dst_ref=rows, send_sem=send_sems.at[a * (N_DEV - 1) + k],
                    recv_sem=recv_sems.at[a * (N_DEV - 1) + k], device_id=to, device_id_type=MESH)

            local.append(pltpu.make_async_copy(ins[a], outs[a].at[me], local_sems.at[a]))
            first.append(copy(0, me, sibling, src=ins[a]))
            first += [copy(1 + j, me, (px, py, c), src=ins[a]) for j, (px, py) in enumerate(chips)]
            if waiting:
                passed.append([copy(4 + j, 4 * px + 2 * py + c, sibling) for j, (px, py) in enumerate(chips)])
                arrivals.append([copy(k, me, sibling) for k in range(N_DEV - 1)])
        return local, first, passed, arrivals

    def start(self, ins, outs, sems):
        local, first, _, _ = self._plan(ins, outs, sems, waiting=False)
        for cp in local + first:
            cp.start()

    def wait(self, ins, outs, sems):
        local, first, passed, arrivals = self._plan(ins, outs, sems, waiting=True)
        for a in range(self.n):
            for j in range(3):
                arrivals[a][1 + j].wait_recv()
                passed[a][j].start()
        for a in range(self.n):
            arrivals[a][0].wait_recv()
            for j in range(3):
                arrivals[a][4 + j].wait_recv()
        for cp in first + [cp for row in passed for cp in row]:
            cp.wait_send()
        for cp in local:
            cp.wait()


def _all_to_all(arrays, gather, name):
    ex = _ChipGather(arrays) if gather else _Exchange(arrays, gather)

    def body(*refs):
        ins, outs, sems = ex.split(refs)
        ex.start(ins, outs, sems)
        ex.wait(ins, outs, sems)

    return pl.pallas_call(body, name=name, in_specs=ex.specs, out_specs=ex.specs, out_shape=ex.out_shape,
                          scratch_shapes=ex.scratch)(*ex.arrays)


def _reduce_adamw(parts, w, m, v, rows):
    R, C = w.shape

    def body(p_ref, w_ref, m_ref, v_ref, g_ref, d_ref, nm_ref, nv_ref):
        g = p_ref[0].astype(F32)
        for s in range(1, N_DEV):
            g = g + p_ref[s].astype(F32)
        g_ref[...] = g
        m_new = ADAM_B1 * m_ref[...] + (1.0 - ADAM_B1) * g
        v_new = ADAM_B2 * v_ref[...] + (1.0 - ADAM_B2) * (g * g)
        m_hat = m_new / (1.0 - ADAM_B1 ** ADAM_STEP)
        v_hat = v_new / (1.0 - ADAM_B2 ** ADAM_STEP)
        d_ref[...] = -ADAM_LR * (m_hat / (jnp.sqrt(v_hat) + ADAM_EPS) + ADAM_WD * w_ref[...])
        nm_ref[...] = m_new
        nv_ref[...] = v_new

    row = pl.BlockSpec((rows, C), lambda i: (i, 0))
    return pl.pallas_call(
        body, name="reduce_adamw", grid=(R // rows,),
        in_specs=[pl.BlockSpec((N_DEV, rows, C), lambda i: (0, i, 0)), row, row, row],
        out_specs=[row] * 4, out_shape=[jax.ShapeDtypeStruct((R, C), F32)] * 4,
        compiler_params=_params("parallel"),
    )(parts, w, m, v)


def _row_tile(rows, cols):
    fits = [t for t in range(8, rows + 1, 8) if rows % t == 0 and N_DEV * t * cols * 4 <= 4 * 1024 * 1024]
    return max(fits) if fits else rows


SMALL_COLS = 1024


def _pack_small(vals):
    rows = []
    for a in vals:
        flat = a.reshape(-1)
        pad = (-flat.shape[0]) % SMALL_COLS
        rows.append(jnp.pad(flat, (0, pad)).reshape(-1, SMALL_COLS))
    packed = jnp.concatenate(rows, axis=0)
    return jnp.pad(packed, ((0, (-packed.shape[0]) % 8), (0, 0)))


def _unpack_small(packed, like):
    out, r = [], 0
    for a in like:
        n = a.size
        nr = -(-n // SMALL_COLS)
        out.append(packed[r:r + nr].reshape(-1)[:n].reshape(a.shape))
        r += nr
    return out


def kernel(x, norm_mix_pre, w_in, w_pool, pool_scale, attn_scale, w_out, norm_mix_post, norm_ffn_pre, w_up, conv_w, conv_b, w_down, norm_ffn_post, loss_target, m_norm_mix_pre, m_w_in, m_w_pool, m_pool_scale, m_attn_scale, m_w_out, m_norm_mix_post, m_norm_ffn_pre, m_w_up, m_conv_w, m_conv_b, m_w_down, m_norm_ffn_post, v_norm_mix_pre, v_w_in, v_w_pool, v_pool_scale, v_attn_scale, v_w_out, v_norm_mix_post, v_norm_ffn_pre, v_w_up, v_conv_w, v_conv_b, v_w_down, v_norm_ffn_post):
    S, D = x.shape[1], x.shape[2]
    d_ff_block = w_up.shape[2]

    xs, target = x[0], loss_target[0]
    g1, g2, g3, g4 = norm_mix_pre, norm_mix_post, norm_ffn_pre, norm_ffn_post
    big = min(512, S)
    small = min(256, S)
    n_pairs = pool_scale.shape[1] // QB
    conv_b_g = conv_b.reshape(N_DEV, 1, d_ff_block)

    (w_in_g,) = _all_to_all([w_in[0].astype(BF16)], gather=True, name="gather_w_in")
    h1_t, u, qkv, pool_out, (w_out_g, w_down_g) = _fwd_inproj_pool(
        xs, g1, w_in_g, w_pool[0], big, _ChipGather([w_out[0].astype(BF16), w_down[0].astype(BF16)]))
    attn_out, (w_up_g, conv_w_g) = _fwd_attn(
        qkv, n_pairs, _ChipGather([w_up[0].astype(BF16), conv_w[0]]), min(ATTN_FWD_BLOCKS, S // QB))
    w_out_full = w_out_g.reshape(D, D)
    w_down4 = w_down_g.reshape(D_FF_SHARDS, d_ff_block, D)
    mix, x2, h2_t = _fwd_outproj(pool_out, attn_out, pool_scale, attn_scale, w_out_full, xs, g2, g3, big)
    upre, gate_val, dy, df, loss_cols, dg4 = _fwd_ffn_loss(h2_t, w_up_g, conv_w_g, conv_b_g, w_down4, x2, target, g4, small)
    loss = lax.psum(0.5 * jnp.sum(loss_cols) / D, ("x", "y", "c"))

    dupre_g, dupre_v, d_wd4, d_wup, d_cb, d_cw = _bwd_ffn_blocks(gate_val, upre, conv_w_g, w_down4, df, h2_t, min(1024, S))
    dx2, dmix, dg3, dg2 = _bwd_ffn_tokens(dupre_g, dupre_v, w_up_g, x2, dy, mix, g2, g3, big)
    d_pool, d_attn, d_wout, d_ps, d_as = _bwd_outproj(dmix, w_out_full, pool_out, attn_out, pool_scale, attn_scale, big)
    d_wdown_g = d_wd4.reshape(N_DEV, w_down.shape[1], D)
    d_wout_g = d_wout.reshape(N_DEV, D // N_DEV, D)
    dq, dk, dv, late_parts = _bwd_attn(qkv, d_attn, n_pairs, _Exchange([d_wout_g, d_wup, d_wdown_g, d_cw], gather=False),
                                       min(ATTN_BWD_BLOCKS, S // QB))
    dproj, d_win, d_wp = _bwd_pool_w_in(u, d_pool, w_pool[0], dq, dk, dv, h1_t, N_DEV, big)
    w_in_t = w_in_g.transpose(0, 2, 1).reshape(-1, D)
    dx, dg1, (win_parts,) = _bwd_x(dproj, w_in_t, xs, dx2, g1, big, _Exchange([d_win], gather=False))
    big_parts = [win_parts] + list(late_parts)
    r = dict(dx=dx, g1=dg1, w_pool=d_wp, pool_scale=d_ps, attn_scale=d_as, g2=dg2, g3=dg3, conv_b=d_cb, g4=dg4)

    small_names = ["norm_mix_pre", "w_pool", "pool_scale", "attn_scale", "norm_mix_post", "norm_ffn_pre", "conv_b", "norm_ffn_post"]
    small_w = dict(norm_mix_pre=norm_mix_pre, w_pool=w_pool, pool_scale=pool_scale, attn_scale=attn_scale,
                   norm_mix_post=norm_mix_post, norm_ffn_pre=norm_ffn_pre, conv_b=conv_b, norm_ffn_post=norm_ffn_post)
    small_m = dict(norm_mix_pre=m_norm_mix_pre, w_pool=m_w_pool, pool_scale=m_pool_scale, attn_scale=m_attn_scale,
                   norm_mix_post=m_norm_mix_post, norm_ffn_pre=m_norm_ffn_pre, conv_b=m_conv_b, norm_ffn_post=m_norm_ffn_post)
    small_v = dict(norm_mix_pre=v_norm_mix_pre, w_pool=v_w_pool, pool_scale=v_pool_scale, attn_scale=v_attn_scale,
                   norm_mix_post=v_norm_mix_post, norm_ffn_pre=v_norm_ffn_pre, conv_b=v_conv_b, norm_ffn_post=v_norm_ffn_post)
    small_g = dict(norm_mix_pre=r["g1"], w_pool=r["w_pool"], pool_scale=r["pool_scale"], attn_scale=r["attn_scale"],
                   norm_mix_post=r["g2"], norm_ffn_pre=r["g3"], conv_b=r["conv_b"], norm_ffn_post=r["g4"])
    like = [small_w[n] for n in small_names]
    packed_g = _pack_small([small_g[n] for n in small_names])

    (small_parts,) = _all_to_all([packed_g], gather=True, name="gather_small_grads")

    def update(parts, w, m, v):
        R, C = w.shape
        return _reduce_adamw(parts, w, m, v, _row_tile(R, C))

    res = {}
    res["w_in"] = update(big_parts[0], w_in[0], m_w_in[0], v_w_in[0])
    res["w_out"] = update(big_parts[1], w_out[0], m_w_out[0], v_w_out[0])
    res["w_up"] = update(big_parts[2], w_up[0], m_w_up[0], v_w_up[0])
    res["w_down"] = update(big_parts[3], w_down[0], m_w_down[0], v_w_down[0])
    res["conv_w"] = update(big_parts[4], conv_w[0], m_conv_w[0], v_conv_w[0])
    small_res = update(small_parts, _pack_small(like), _pack_small([small_m[n] for n in small_names]),
                       _pack_small([small_v[n] for n in small_names]))
    small_res = [_unpack_small(t, like) for t in small_res]
    for idx, n in enumerate(small_names):
        res[n] = tuple(t[idx] for t in small_res)

    order = ["norm_mix_pre", "w_in", "w_pool", "pool_scale", "attn_scale", "w_out", "norm_mix_post", "norm_ffn_pre",
             "w_up", "conv_w", "conv_b", "w_down", "norm_ffn_post"]
    shaped = {n: tuple(t.reshape(s.shape) for t in res[n])
              for n, s in dict(norm_mix_pre=norm_mix_pre, w_in=w_in, w_pool=w_pool, pool_scale=pool_scale, attn_scale=attn_scale,
                               w_out=w_out, norm_mix_post=norm_mix_post, norm_ffn_pre=norm_ffn_pre, w_up=w_up, conv_w=conv_w,
                               conv_b=conv_b, w_down=w_down, norm_ffn_post=norm_ffn_post).items()}
    outs = [loss, r["dx"].reshape(x.shape)]
    for k in range(4):
        outs += [shaped[n][k] for n in order]
    return tuple(outs)
```

```python
import functools

import jax
import jax.numpy as jnp
from jax import lax
from jax.experimental import pallas as pl
from jax.experimental.pallas import tpu as pltpu

F32 = jnp.float32
BF16 = jnp.bfloat16

N_DEV = 8
EPS = 1e-6
POOL_WINDOWS = (2, 4, 8, 16)
POOL_GROUP = 128
HALO = 16
HEAD_DIM = 64
QB = 128
ATTN_SCALE = HEAD_DIM ** -0.5
ATTN_FWD_BLOCKS = 16
ATTN_BWD_BLOCKS = 8
EXP_UNDERFLOW = -88.0
D_FF_SHARDS = 4

ADAM_LR = 0.001
ADAM_B1 = 0.9
ADAM_B2 = 0.999
ADAM_EPS = 1e-08
ADAM_WD = 0.01
ADAM_STEP = 10

VMEM_LIMIT_V7X = 56 * 1024 * 1024
MESH = pl.DeviceIdType.MESH


def _params(*semantics):
    return pltpu.CompilerParams(dimension_semantics=semantics, vmem_limit_bytes=VMEM_LIMIT_V7X)


def _const(shape):
    zeros = (0,) * len(shape)
    return pl.BlockSpec(shape, lambda *_: zeros, pipeline_mode=pl.Buffered(1))


def _dot(a, b):
    return jnp.dot(a, b, preferred_element_type=F32)


def _dot_nt(a, b):
    return lax.dot_general(a, b, (((1,), (1,)), ((), ())), preferred_element_type=F32)


def _dot_tn(a, b):
    return lax.dot_general(a, b, (((0,), (0,)), ((), ())), preferred_element_type=F32)


def _rms(v):
    return lax.rsqrt(jnp.mean(v * v, axis=-1, keepdims=True) + EPS)


def _norm_bwd(dn_times_gain, n, r):
    return r * (dn_times_gain - n * jnp.mean(dn_times_gain * n, axis=-1, keepdims=True))


def _zero_when(first, *refs):
    @pl.when(first)
    def _():
        for ref in refs:
            ref[...] = jnp.zeros_like(ref)


def _colsum(v):
    return jnp.sum(v, axis=0, keepdims=True)


def _grid_ends(grid):
    ids = [pl.program_id(a) for a in range(len(grid))]
    first = functools.reduce(jnp.logical_and, [i == 0 for i in ids])
    last = functools.reduce(jnp.logical_and, [i == n - 1 for i, n in zip(ids, grid)])
    return first, last


def _fwd_inproj_pool(x, g1, w_in_g, w_pool, tile):
    S, D = x.shape
    nb, _, cs = w_in_g.shape
    d_pool = 2 * cs

    def body(x_ref, g_ref, w_ref, wp_ref, ht_ref, u_ref, qkv_ref, pool_ref, halo_ref):
        i = pl.program_id(0)
        _zero_when(i == 0, halo_ref)
        xf = x_ref[...]
        h = (xf * _rms(xf) * g_ref[...]).astype(BF16)
        ht_ref[...] = h.T
        u = jnp.concatenate([_dot(h, w_ref[0]), _dot(h, w_ref[1])], axis=1)
        u_ref[...] = u
        for d in range(2, nb):
            qkv_ref[:, (d - 2) * cs:(d - 1) * cs] = _dot(h, w_ref[d]).astype(BF16)
        parts = _pool_deviation(u, halo_ref[...], i * tile)
        halo_ref[...] = u[tile - HALO:, :]
        for g, p in enumerate(parts):
            pool_ref[:, g * POOL_GROUP:(g + 1) * POOL_GROUP] = _dot(p.astype(BF16), wp_ref[g].astype(BF16))

    row = lambda w: pl.BlockSpec((tile, w), lambda i: (i, 0))
    return pl.pallas_call(
        body, name="fwd_inproj_pool", grid=(S // tile,),
        in_specs=[row(D), _const((1, D)), _const(w_in_g.shape), _const(w_pool.shape)],
        out_specs=[pl.BlockSpec((D, tile), lambda i: (0, i)), row(d_pool), row(3 * d_pool), row(d_pool)],
        out_shape=[jax.ShapeDtypeStruct((D, S), BF16), jax.ShapeDtypeStruct((S, d_pool), F32),
                   jax.ShapeDtypeStruct((S, 3 * d_pool), BF16), jax.ShapeDtypeStruct((S, d_pool), F32)],
        scratch_shapes=[pltpu.VMEM((HALO, d_pool), F32)],
        compiler_params=_params("arbitrary"),
    )(x, g1, w_in_g, w_pool)


def _window_sums(ext, forward):
    n = ext.shape[0]
    sums, s, sh = {}, ext, 1
    while sh < POOL_WINDOWS[-1]:
        s = s + pltpu.roll(s, (n - sh) if forward else sh, axis=0)
        sh *= 2
        sums[sh] = s
    return sums


def _pool_counts(t0, rows):
    t1 = (lax.broadcasted_iota(jnp.int32, (rows, 1), 0) + t0 + 1).astype(F32)
    return [jnp.minimum(t1, float(w)) for w in POOL_WINDOWS]


def _pool_deviation(u, halo, t0):
    T = u.shape[0]
    sums = _window_sums(jnp.concatenate([halo, u], axis=0), forward=False)
    counts = _pool_counts(t0, T)
    parts = []
    for g, w in enumerate(POOL_WINDOWS):
        lanes = slice(g * POOL_GROUP, (g + 1) * POOL_GROUP)
        parts.append(sums[w][HALO:, lanes] / counts[g] - u[:, lanes])
    return parts


def _prev_halo_spec(tile, width):
    return pl.BlockSpec((HALO, width), lambda i: (jnp.maximum(i * (tile // HALO) - 1, 0), 0))


def _next_halo_spec(tile, width, n_tiles):
    last = n_tiles * (tile // HALO) - 1
    return pl.BlockSpec((HALO, width), lambda i: (jnp.minimum((i + 1) * (tile // HALO), last), 0))


def _low_lanes():
    return lax.broadcasted_iota(jnp.int32, (QB, 2 * HEAD_DIM), 1) < HEAD_DIM


LOG_PIECES = 2
GRAD_PIECES = 2


def _triangle(inclusive, pieces):
    row = lax.broadcasted_iota(jnp.int32, (pieces * QB, QB), 0) % QB
    col = lax.broadcasted_iota(jnp.int32, (pieces * QB, QB), 1)
    return ((row >= col) if inclusive else (row > col)).astype(BF16)


def _pieces(v, n):
    out, rest = [], v
    for _ in range(n - 1):
        piece = rest.astype(BF16)
        out.append(piece)
        rest = rest - piece.astype(F32)
    out.append(rest.astype(BF16))
    return jnp.concatenate(out, axis=1)


def _causal_mask(width, offset):
    row = lax.broadcasted_iota(jnp.int32, (QB, width), 0)
    col = lax.broadcasted_iota(jnp.int32, (QB, width), 1)
    return col < row + offset


def _row_sums(vals, carry):
    for b in reversed(range(vals.shape[1] // QB)):
        carry = carry + jnp.sum(vals[:, b * QB:(b + 1) * QB], axis=1, keepdims=True)
    return carry


def _suffix_sums(vals, tri, carry):
    n = vals.shape[1] // QB
    out, run = [None] * n, carry
    for b in reversed(range(n)):
        blk = vals[:, b * QB:(b + 1) * QB]
        out[b] = _dot(_pieces(blk, tri.shape[0] // QB), tri) + run
        run = run + jnp.sum(blk, axis=1, keepdims=True)
    return (out[0] if n == 1 else jnp.concatenate(out, axis=1)), run


def _attn_tiles(qhs, kws, masks, carries, after_s):
    return _attn_weights(_attn_scores(qhs, kws, masks), masks, carries, after_s)


def _attn_scores(qhs, kws, masks):
    zs = [_dot_nt(qh, kw) * ATTN_SCALE for qh, kw in zip(qhs, kws)]
    es = [jnp.exp(-jnp.abs(z)) for z in zs]
    softplus = [jnp.maximum(z, 0.0) + jnp.log(1.0 + e) for z, e in zip(zs, es)]
    log_1m_beta = [-sp if m is None else jnp.where(m, -sp, 0.0) for sp, m in zip(softplus, masks)]
    return list(zip(zs, es, softplus, log_1m_beta))


def _attn_weights(scores, masks, carries, after_s):
    sums = [_suffix_sums(l, after_s, c) for (_, _, _, l), c in zip(scores, carries)]
    weights = [jnp.exp(z - sp + st) for (z, _, sp, _), (st, _) in zip(scores, sums)]
    weights = [a if m is None else jnp.where(m, a, 0.0) for a, m in zip(weights, masks)]
    return [(z, e, a, c) for (z, e, _, _), a, (_, c) in zip(scores, weights, sums)]


def _split_heads(v, low_lanes):
    return jnp.where(low_lanes, v, 0.0).astype(BF16), jnp.where(low_lanes, 0.0, v).astype(BF16)


def _sweep_done(c0, c1):
    return (jnp.maximum(jnp.max(c0), jnp.max(c1)) < EXP_UNDERFLOW).astype(jnp.int32)


def _all_done(carries):
    return jnp.max(functools.reduce(jnp.maximum, carries)) < EXP_UNDERFLOW


def _first_window(i):
    first_blk = jnp.maximum(i - 1, 0)
    return first_blk, pl.multiple_of(first_blk * QB, QB), (i - first_blk) * QB


def _fwd_attn(qkv, n_pairs, ex, subs):
    S = qkv.shape[0]
    n_steps = S // (subs * QB)

    def body(q_ref, k_ref, v_ref, *rest):
        o_ref = rest[ex.n]
        ex_refs = ex.split(rest[:ex.n] + rest[ex.n + 1:])
        first_step, last_step = _grid_ends((n_pairs, n_steps))

        @pl.when(first_step)
        def _():
            ex.start(*ex_refs)

        low_lanes = _low_lanes()
        after_s = _triangle(False, LOG_PIECES)
        zero = jnp.zeros((QB, 1), F32)

        def cond(c):
            return jnp.logical_and(c[0] >= 0, c[1] == 0)

        qhs, kws, vws, masks, first_blks = [], [], [], [], []
        for sub in range(subs):
            i = pl.program_id(1) * subs + sub
            first_blk, start, offset = _first_window(i)
            first_blks.append(first_blk)
            qhs += _split_heads(q_ref[sub * QB:(sub + 1) * QB, :].astype(F32), low_lanes)
            kws += [k_ref[pl.ds(start, 2 * QB), :]] * 2
            vws += [v_ref[pl.ds(start, 2 * QB), :]] * 2
            masks += [_causal_mask(2 * QB, offset)] * 2
        tiles = _attn_tiles(qhs, kws, masks, [zero] * len(qhs), after_s)
        outs = [_dot(t[2].astype(BF16), vw) for t, vw in zip(tiles, vws)]

        first_out = [jnp.where(low_lanes, outs[2 * sub], outs[2 * sub + 1]) for sub in range(subs)]

        def sweep_on():
            final = []
            for sub in range(subs):
                def step(c, qh=qhs[2 * sub:2 * sub + 2]):
                    j, _, acc, c0, c1 = c
                    at = pl.multiple_of(j * QB, QB)
                    kb = k_ref[pl.ds(at, QB), :]
                    vb = v_ref[pl.ds(at, QB), :]
                    far = _attn_tiles(qh, [kb, kb], [None, None], [c0, c1], after_s)
                    acc = acc + jnp.where(low_lanes, _dot(far[0][2].astype(BF16), vb), _dot(far[1][2].astype(BF16), vb))
                    return j - 1, _sweep_done(far[0][3], far[1][3]), acc, far[0][3], far[1][3]

                c0, c1 = tiles[2 * sub][3], tiles[2 * sub + 1][3]
                final.append(lax.while_loop(cond, step, (first_blks[sub] - 1, _sweep_done(c0, c1), first_out[sub], c0, c1))[2])
            return tuple(final)

        final = lax.cond(_all_done([t[3] for t in tiles]), lambda: tuple(first_out), sweep_on)
        for sub in range(subs):
            o_ref[sub * QB:(sub + 1) * QB, :] = final[sub]

        @pl.when(last_step)
        def _():
            ex.wait(*ex_refs)

    outs = pl.pallas_call(
        body, name="fwd_attn", grid=(n_pairs, n_steps),
        in_specs=[pl.BlockSpec((subs * QB, QB), lambda p, i: (i, p)),
                  pl.BlockSpec((S, QB), lambda p, i: (0, n_pairs + p), pipeline_mode=pl.Buffered(1)),
                  pl.BlockSpec((S, QB), lambda p, i: (0, 2 * n_pairs + p), pipeline_mode=pl.Buffered(1))] + ex.specs,
        out_specs=[pl.BlockSpec((subs * QB, QB), lambda p, i: (i, p))] + ex.specs,
        out_shape=[jax.ShapeDtypeStruct((S, n_pairs * QB), F32)] + ex.out_shape,
        scratch_shapes=ex.scratch,
        compiler_params=_params("arbitrary", "arbitrary"),
    )(qkv, qkv, qkv, *ex.arrays)
    return outs[0], outs[1:]


def _normalized_heads(pool_out, attn_out):
    rp, ra = _rms(pool_out), _rms(attn_out)
    return pool_out * rp, rp, attn_out * ra, ra


def _fwd_outproj(pool_out, attn_out, pool_scale, attn_scale, w_out, x, g2, g3, tile):
    S, D = x.shape
    C = pool_out.shape[1]

    def body(p_ref, a_ref, ps_ref, as_ref, w_ref, x_ref, g2_ref, g3_ref, mix_ref, x2_ref, h2t_ref):
        n_p, _, n_a, _ = _normalized_heads(p_ref[...], a_ref[...])
        mix = _dot((n_p * ps_ref[...]).astype(BF16), w_ref[:C, :]) + _dot((n_a * as_ref[...]).astype(BF16), w_ref[C:, :])
        mix_ref[...] = mix
        x2 = x_ref[...] + mix * _rms(mix) * g2_ref[...]
        x2_ref[...] = x2
        h2t_ref[...] = (x2 * _rms(x2) * g3_ref[...]).astype(BF16).T

    row = lambda w: pl.BlockSpec((tile, w), lambda i: (i, 0))
    return pl.pallas_call(
        body, name="fwd_outproj", grid=(S // tile,),
        in_specs=[row(C), row(C), _const((1, C)), _const((1, C)), _const(w_out.shape), row(D), _const((1, D)), _const((1, D))],
        out_specs=[row(D), row(D), pl.BlockSpec((D, tile), lambda i: (0, i))],
        out_shape=[jax.ShapeDtypeStruct((S, D), F32), jax.ShapeDtypeStruct((S, D), F32), jax.ShapeDtypeStruct((D, S), BF16)],
        compiler_params=_params("parallel"),
    )(pool_out, attn_out, pool_scale, attn_scale, w_out, x, g2, g3)


def _conv_taps(tile_rows, halo_rows):
    T = tile_rows.shape[0]
    ext = jnp.concatenate([halo_rows.astype(F32), tile_rows.astype(F32)], axis=0)
    return pltpu.roll(ext, 2, axis=0)[HALO:], pltpu.roll(ext, 1, axis=0)[HALO:], ext[HALO:]


def _tap_rows(cw_ref, d):
    return [cw_ref[d, k:k + 1, :] for k in range(3)]


def _gated_unit(taps_gate, taps_val, cw_gate, cw_val, cb_gate, cb_val):
    gate = cw_gate[0] * taps_gate[0] + cw_gate[1] * taps_gate[1] + cw_gate[2] * taps_gate[2] + cb_gate
    val = cw_val[0] * taps_val[0] + cw_val[1] * taps_val[1] + cw_val[2] * taps_val[2] + cb_val
    sig = 1.0 / (1.0 + jnp.exp(-gate))
    return gate, val, sig


def _fwd_ffn_loss(h2_t, w_up_g, conv_w_g, conv_b_g, w_down4, x2, target, g4, tile):
    S, D = x2.shape
    nb, _, cs = w_up_g.shape
    half = D_FF_SHARDS

    def body(h_ref, w_ref, cw_ref, cb_ref, wd_ref, x2_ref, t_ref, g4_ref, upre_ref, gv_ref, dy_ref, df_ref, loss_ref, dg4_ref, halo_ref):
        _zero_when(pl.program_id(0) == 0, loss_ref, dg4_ref, halo_ref)
        h = h_ref[...].T

        def up(s):
            return _dot(h, w_ref[s]), _dot(h, w_ref[s + half])

        f = jnp.zeros((tile, D), F32)
        ahead = up(0)
        for s in range(half):
            ug, uv = ahead
            if s + 1 < half:
                ahead = up(s + 1)
            upre_ref[s] = ug.astype(BF16)
            upre_ref[s + half] = uv.astype(BF16)
            gate, val, sig = _gated_unit(_conv_taps(ug, halo_ref[s]), _conv_taps(uv, halo_ref[s + half]),
                                         _tap_rows(cw_ref, s), _tap_rows(cw_ref, s + half), cb_ref[s], cb_ref[s + half])
            halo_ref[s] = ug[tile - HALO:, :]
            halo_ref[s + half] = uv[tile - HALO:, :]
            gv_ref[s] = gate.astype(BF16)
            gv_ref[s + half] = val.astype(BF16)
            f = f + _dot((gate * sig * val).astype(BF16), wd_ref[s])
        r4 = _rms(f)
        n4 = f * r4
        err = x2_ref[...] + n4 * g4_ref[...] - t_ref[...]
        dy = err * (1.0 / D)
        dy_ref[...] = dy
        df_ref[...] = _norm_bwd(dy * g4_ref[...], n4, r4).astype(BF16)
        loss_ref[...] += _colsum(err * err)
        dg4_ref[...] += _colsum(dy * n4)

    row = lambda w: pl.BlockSpec((tile, w), lambda i: (i, 0))
    return pl.pallas_call(
        body, name="fwd_ffn_loss", grid=(S // tile,),
        in_specs=[pl.BlockSpec((D, tile), lambda i: (0, i)), _const(w_up_g.shape), _const(conv_w_g.shape), _const(conv_b_g.shape),
                  _const(w_down4.shape), row(D), row(D), _const((1, D))],
        out_specs=[pl.BlockSpec((nb, tile, cs), lambda i: (0, i, 0)), pl.BlockSpec((nb, tile, cs), lambda i: (0, i, 0)), row(D), row(D),
                   pl.BlockSpec((1, D), lambda i: (0, 0)), pl.BlockSpec((1, D), lambda i: (0, 0))],
        out_shape=[jax.ShapeDtypeStruct((nb, S, cs), BF16), jax.ShapeDtypeStruct((nb, S, cs), BF16),
                   jax.ShapeDtypeStruct((S, D), F32), jax.ShapeDtypeStruct((S, D), BF16),
                   jax.ShapeDtypeStruct((1, D), F32), jax.ShapeDtypeStruct((1, D), F32)],
        scratch_shapes=[pltpu.VMEM((nb, HALO, cs), F32)],
        compiler_params=_params("arbitrary"),
    )(h2_t, w_up_g, conv_w_g, conv_b_g, w_down4, x2, target, g4)


def _bwd_ffn_blocks(gate_val, upre, conv_w_g, w_down4, df, h2_t, tile):
    nb, S, cs = upre.shape
    D = df.shape[1]
    n_tiles = S // tile
    half = D_FF_SHARDS

    def body(g_ref, v_ref, ug_ref, uv_ref, cwg_ref, cwv_ref, wd_ref, df_ref, ht_ref,
             dug_ref, duv_ref, dwd_ref, dwg_ref, dwv_ref, dbg_ref, dbv_ref, dcwg_ref, dcwv_ref, next_ref):
        _zero_when(pl.program_id(1) == 0, dwd_ref, dwg_ref, dwv_ref, dbg_ref, dbv_ref, dcwg_ref, dcwv_ref, next_ref)
        dfb = df_ref[...]
        dact = _dot_nt(dfb, wd_ref[0])
        gate, val = g_ref[0].astype(F32), v_ref[0].astype(F32)
        sig = 1.0 / (1.0 + jnp.exp(-gate))
        silu = gate * sig
        dwd_ref[0] += _dot_tn((silu * val).astype(BF16), dfb)
        ht = ht_ref[...]

        def through_conv(dup, slot, cw_ref, u_ref, du_ref, dw_ref, db_ref, dcw_ref):
            ext = jnp.concatenate([dup, next_ref[slot]], axis=0)
            n = ext.shape[0]
            shifted = (dup, pltpu.roll(ext, n - 1, axis=0)[:tile], pltpu.roll(ext, n - 2, axis=0)[:tile])
            next_ref[slot] = dup[:HALO]
            cw = _tap_rows(cw_ref, 0)
            dupre = (cw[2] * shifted[0] + cw[1] * shifted[1] + cw[0] * shifted[2]).astype(BF16)
            du_ref[0] = dupre
            dw_ref[0] += _dot(ht, dupre)
            u = u_ref[0].astype(F32)
            db_ref[0] += _colsum(dup)
            for k in range(3):
                dcw_ref[0, k:k + 1, :] += _colsum(shifted[2 - k] * u)

        through_conv(dact * (val * (sig + silu * (1.0 - sig))), 0, cwg_ref, ug_ref, dug_ref, dwg_ref, dbg_ref, dcwg_ref)
        through_conv(dact * silu, 1, cwv_ref, uv_ref, duv_ref, dwv_ref, dbv_ref, dcwv_ref)

    rev = lambda i: n_tiles - 1 - i
    blk = lambda off: pl.BlockSpec((1, tile, cs), lambda s, i: (s + off, rev(i), 0))
    par = lambda off, r: pl.BlockSpec((1, r, cs), lambda s, i: (s + off, 0, 0))
    acc = lambda r, c: pl.BlockSpec((1, r, c), lambda s, i: (s, 0, 0), pipeline_mode=pl.Buffered(1))
    outs = pl.pallas_call(
        body, name="bwd_ffn_blocks", grid=(half, n_tiles),
        in_specs=[blk(0), blk(half), blk(0), blk(half), par(0, 3), par(half, 3),
                  acc(cs, D), pl.BlockSpec((tile, D), lambda s, i: (rev(i), 0)), pl.BlockSpec((D, tile), lambda s, i: (0, rev(i)))],
        out_specs=[blk(0), blk(0), acc(cs, D), acc(D, cs), acc(D, cs), acc(1, cs), acc(1, cs), acc(3, cs), acc(3, cs)],
        out_shape=[jax.ShapeDtypeStruct((half, S, cs), BF16), jax.ShapeDtypeStruct((half, S, cs), BF16),
                   jax.ShapeDtypeStruct((half, cs, D), F32),
                   jax.ShapeDtypeStruct((half, D, cs), F32), jax.ShapeDtypeStruct((half, D, cs), F32),
                   jax.ShapeDtypeStruct((half, 1, cs), F32), jax.ShapeDtypeStruct((half, 1, cs), F32),
                   jax.ShapeDtypeStruct((half, 3, cs), F32), jax.ShapeDtypeStruct((half, 3, cs), F32)],
        scratch_shapes=[pltpu.VMEM((2, HALO, cs), F32)],
        compiler_params=_params("arbitrary", "arbitrary"),
    )(gate_val, gate_val, upre, upre, conv_w_g, conv_w_g, w_down4, df, h2_t)
    dupre_g, dupre_v, d_wd, d_wg, d_wv, dbg, dbv, dcwg, dcwv = outs
    return (dupre_g, dupre_v, d_wd, jnp.concatenate([d_wg, d_wv], axis=0), jnp.concatenate([dbg, dbv], axis=0),
            jnp.concatenate([dcwg, dcwv], axis=0))


def _bwd_ffn_tokens(dupre_g, dupre_v, w_up_g, x2, dy, mix, g2, g3, tile):
    half, S, cs = dupre_g.shape
    D = x2.shape[1]

    def body(dg_ref, dv_ref, w_ref, x2_ref, dy_ref, mix_ref, g2_ref, g3_ref, dx2_ref, dmix_ref, dg3_ref, dg2_ref):
        _zero_when(pl.program_id(0) == 0, dg3_ref, dg2_ref)
        parts = [_dot_nt(dg_ref[d], w_ref[d]) for d in range(half)] + [_dot_nt(dv_ref[d], w_ref[d + half]) for d in range(half)]
        while len(parts) > 1:
            parts = [a + b for a, b in zip(parts[::2], parts[1::2])]
        dh2 = parts[0]
        x2 = x2_ref[...]
        r3 = _rms(x2)
        n3 = x2 * r3
        dg3_ref[...] += _colsum(dh2 * n3)
        dx2 = dy_ref[...] + _norm_bwd(dh2 * g3_ref[...], n3, r3)
        dx2_ref[...] = dx2
        mix = mix_ref[...]
        r2 = _rms(mix)
        n2 = mix * r2
        dg2_ref[...] += _colsum(dx2 * n2)
        dmix_ref[...] = _norm_bwd(dx2 * g2_ref[...], n2, r2).astype(BF16)

    row = lambda w: pl.BlockSpec((tile, w), lambda i: (i, 0))
    blk = pl.BlockSpec((half, tile, cs), lambda i: (0, i, 0))
    acc = pl.BlockSpec((1, D), lambda i: (0, 0))
    return pl.pallas_call(
        body, name="bwd_ffn_tokens", grid=(S // tile,),
        in_specs=[blk, blk, _const(w_up_g.shape), row(D), row(D), row(D), _const((1, D)), _const((1, D))],
        out_specs=[row(D), row(D), acc, acc],
        out_shape=[jax.ShapeDtypeStruct((S, D), F32), jax.ShapeDtypeStruct((S, D), BF16),
                   jax.ShapeDtypeStruct((1, D), F32), jax.ShapeDtypeStruct((1, D), F32)],
        compiler_params=_params("arbitrary"),
    )(dupre_g, dupre_v, w_up_g, x2, dy, mix, g2, g3)


def _bwd_outproj(dmix, w_out, pool_out, attn_out, pool_scale, attn_scale, tile):
    S, D = dmix.shape
    C = pool_out.shape[1]

    def body(dm_ref, w_ref, p_ref, a_ref, ps_ref, as_ref, dp_ref, da_ref, dw_ref, dps_ref, das_ref):
        _zero_when(pl.program_id(0) == 0, dw_ref, dps_ref, das_ref)
        dmx = dm_ref[...]
        dmerged = _dot_nt(dmx, w_ref[...])
        n_p, r_p, n_a, r_a = _normalized_heads(p_ref[...], a_ref[...])
        merged = jnp.concatenate([(n_p * ps_ref[...]).astype(BF16), (n_a * as_ref[...]).astype(BF16)], axis=1)
        dw_ref[...] += _dot_tn(merged, dmx)
        dm_p, dm_a = dmerged[:, :C], dmerged[:, C:]
        dps_ref[...] += _colsum(dm_p * n_p)
        das_ref[...] += _colsum(dm_a * n_a)
        dp_ref[...] = _norm_bwd(dm_p * ps_ref[...], n_p, r_p)
        da_ref[...] = _norm_bwd(dm_a * as_ref[...], n_a, r_a)

    row = lambda w: pl.BlockSpec((tile, w), lambda i: (i, 0))
    return pl.pallas_call(
        body, name="bwd_outproj", grid=(S // tile,),
        in_specs=[row(D), _const(w_out.shape), row(C), row(C), _const((1, C)), _const((1, C))],
        out_specs=[row(C), row(C), pl.BlockSpec(w_out.shape, lambda i: (0, 0)),
                   pl.BlockSpec((1, C), lambda i: (0, 0)), pl.BlockSpec((1, C), lambda i: (0, 0))],
        out_shape=[jax.ShapeDtypeStruct((S, C), F32), jax.ShapeDtypeStruct((S, C), F32),
                   jax.ShapeDtypeStruct(w_out.shape, F32), jax.ShapeDtypeStruct((1, C), F32), jax.ShapeDtypeStruct((1, C), F32)],
        compiler_params=_params("arbitrary"),
    )(dmix, w_out, pool_out, attn_out, pool_scale, attn_scale)


def _bwd_attn(qkv, d_attn, n_pairs, ex, subs):
    S = qkv.shape[0]
    n_steps = S // (subs * QB)

    def body(q_ref, k_ref, v_ref, do_ref, *rest):
        dq_ref, dk_ref, dv_ref = rest[ex.n:ex.n + 3]
        ex_refs = ex.split(rest[:ex.n] + rest[ex.n + 3:])
        first_step, last_step = _grid_ends((n_pairs, n_steps))

        @pl.when(first_step)
        def _():
            ex.start(*ex_refs)

        @pl.when(pl.program_id(1) == 0)
        def _():
            dk_ref[...] = jnp.zeros_like(dk_ref)
            dv_ref[...] = jnp.zeros_like(dv_ref)

        low_lanes = _low_lanes()
        after_s, from_s = _triangle(False, LOG_PIECES), _triangle(True, GRAD_PIECES)
        zero = jnp.zeros((QB, 1), F32)

        def tiles(qhs, dohs, totals, kws, vws, masks, cs, gs, scores=None):
            fw = _attn_weights(scores or _attn_scores(qhs, kws, masks), masks, cs, after_s)
            gvals = [t[2] * _dot_nt(doh, vw) for t, doh, vw in zip(fw, dohs, vws)]
            sums = [_suffix_sums(g, from_s, g0) for g, g0 in zip(gvals, gs)]
            totals = [tot if m is None else tot + sm[1] for tot, m, sm in zip(totals, masks, sums)]
            dzs = []
            for (z, e, _, _), g, (nearer, _), tot, m in zip(fw, gvals, sums, totals, masks):
                inv = 1.0 / (1.0 + e)
                sig_abs, sig_neg = inv, e * inv
                pos = z >= 0.0
                dz = g * jnp.where(pos, sig_neg, sig_abs) - jnp.where(pos, sig_abs, sig_neg) * (tot - nearer)
                if m is not None:
                    dz = jnp.where(m, dz, 0.0)
                dzs.append((dz * ATTN_SCALE).astype(BF16))
            dqs = [_dot(dz, kw) for dz, kw in zip(dzs, kws)]
            dks = [_dot_tn(dz, qh) for dz, qh in zip(dzs, qhs)]
            dvs = [_dot_tn(t[2].astype(BF16), doh) for t, doh in zip(fw, dohs)]
            return [(dq, dk, dv, t[3], sm[1], tot) for dq, dk, dv, t, sm, tot in zip(dqs, dks, dvs, fw, sums, totals)]

        def cond(c):
            return jnp.logical_and(c[0] >= 0, c[1] == 0)

        qhs, dohs, kws, vws, masks, first_blks, starts = [], [], [], [], [], [], []
        for sub in range(subs):
            i = pl.program_id(1) * subs + sub
            rows = slice(sub * QB, (sub + 1) * QB)
            first_blk, start, offset = _first_window(i)
            first_blks.append(first_blk)
            starts.append(start)
            qhs += _split_heads(q_ref[rows, :].astype(F32), low_lanes)
            dohs += _split_heads(do_ref[rows, :], low_lanes)
            kws += [k_ref[pl.ds(start, 2 * QB), :]] * 2
            vws += [v_ref[pl.ds(start, 2 * QB), :]] * 2
            masks += [_causal_mask(2 * QB, offset)] * 2
        zeros = [zero] * len(qhs)

        scores = _attn_scores(qhs, kws, masks)
        c_first = [_row_sums(sc[3], zero) for sc in scores]
        all_done = _all_done(c_first)

        def far_totals():
            beyond = []
            for sub in range(subs):
                pair = slice(2 * sub, 2 * sub + 2)

                def far_sums(c, qh=qhs[pair], doh=dohs[pair]):
                    j, _, c0, c1, r0, r1 = c
                    at = pl.multiple_of(j * QB, QB)
                    kb = k_ref[pl.ds(at, QB), :]
                    vb = v_ref[pl.ds(at, QB), :]
                    far = _attn_tiles(qh, [kb, kb], [None, None], [c0, c1], after_s)
                    r0 = r0 + jnp.sum(far[0][2] * _dot_nt(doh[0], vb), axis=1, keepdims=True)
                    r1 = r1 + jnp.sum(far[1][2] * _dot_nt(doh[1], vb), axis=1, keepdims=True)
                    return j - 1, _sweep_done(far[0][3], far[1][3]), far[0][3], far[1][3], r0, r1

                c0, c1 = c_first[pair]
                far = lax.while_loop(cond, far_sums, (first_blks[sub] - 1, _sweep_done(c0, c1), c0, c1, zero, zero))
                beyond += [far[4], far[5]]
            return tuple(beyond)

        beyond_first = list(lax.cond(all_done, lambda: tuple(zeros), far_totals))
        done = tiles(qhs, dohs, beyond_first, kws, vws, masks, zeros, zeros, scores)
        for sub in range(subs):
            dk_ref[pl.ds(starts[sub], 2 * QB), :] += done[2 * sub][1] + done[2 * sub + 1][1]
            dv_ref[pl.ds(starts[sub], 2 * QB), :] += done[2 * sub][2] + done[2 * sub + 1][2]
        first_dq = [jnp.where(low_lanes, done[2 * sub][0], done[2 * sub + 1][0]) for sub in range(subs)]

        def sweep_on():
            final = []
            for sub in range(subs):
                pair = slice(2 * sub, 2 * sub + 2)
                t0, t1 = done[pair]

                def step(c, qh=qhs[pair], doh=dohs[pair], total=[t0[5], t1[5]]):
                    j, _, dq, c0, c1, s0, s1 = c
                    at = pl.multiple_of(j * QB, QB)
                    kb = k_ref[pl.ds(at, QB), :]
                    vb = v_ref[pl.ds(at, QB), :]
                    f0, f1 = tiles(qh, doh, total, [kb, kb], [vb, vb], [None, None], [c0, c1], [s0, s1])
                    dk_ref[pl.ds(at, QB), :] += f0[1] + f1[1]
                    dv_ref[pl.ds(at, QB), :] += f0[2] + f1[2]
                    return j - 1, _sweep_done(f0[3], f1[3]), dq + jnp.where(low_lanes, f0[0], f1[0]), f0[3], f1[3], f0[4], f1[4]

                init = (first_blks[sub] - 1, _sweep_done(t0[3], t1[3]), first_dq[sub], t0[3], t1[3], t0[4], t1[4])
                final.append(lax.while_loop(cond, step, init)[2])
            return tuple(final)

        final = lax.cond(all_done, lambda: tuple(first_dq), sweep_on)
        for sub in range(subs):
            dq_ref[sub * QB:(sub + 1) * QB, :] = final[sub]

        @pl.when(last_step)
        def _():
            ex.wait(*ex_refs)

    blk = pl.BlockSpec((subs * QB, QB), lambda p, i: (i, p))
    full = lambda off: pl.BlockSpec((S, QB), lambda p, i: (0, off + p), pipeline_mode=pl.Buffered(1))
    outs = pl.pallas_call(
        body, name="bwd_attn", grid=(n_pairs, n_steps),
        in_specs=[blk, full(n_pairs), full(2 * n_pairs), blk] + ex.specs,
        out_specs=[blk, full(0), full(0)] + ex.specs,
        out_shape=[jax.ShapeDtypeStruct((S, n_pairs * QB), F32)] * 3 + ex.out_shape,
        scratch_shapes=ex.scratch,
        compiler_params=_params("arbitrary", "arbitrary"),
    )(qkv, qkv, qkv, d_attn, *ex.arrays)
    return outs[0], outs[1], outs[2], outs[3:]


def _bwd_pool_w_in(u, d_pool, w_pool, dq, dk, dv, h1_t, n_blocks, tile):
    S, C = u.shape
    D = h1_t.shape[0]
    n_tiles = S // tile
    ng = len(POOL_WINDOWS)
    cs = 4 * C // n_blocks
    per = C // cs

    def body(u_ref, uh_ref, d_ref, dh_ref, wp_ref, dq_ref, dk_ref, dv_ref, ht_ref, dproj_ref, dw_ref, dwp_ref):
        i = pl.program_id(0)
        first = i == 0
        _zero_when(first, dw_ref, dwp_ref)
        ht = ht_ref[...]
        for d in range(per, n_blocks):
            src = (dq_ref, dk_ref, dv_ref)[d // per - 1]
            dproj = src[:, (d % per) * cs:(d % per + 1) * cs].astype(BF16)
            dproj_ref[:, d * cs:(d + 1) * cs] = dproj
            dw_ref[d] += _dot(ht, dproj)
        halo = jnp.where(first, 0.0, uh_ref[...])
        parts = _pool_deviation(u_ref[...], halo, i * tile)
        dout = d_ref[...]
        nxt = jnp.where(i == n_tiles - 1, 0.0, dh_ref[...])
        dext = jnp.concatenate([dout, nxt], axis=0).astype(BF16)
        counts = _pool_counts(i * tile, tile + HALO)
        dps, scaled = [], []
        for g in range(ng):
            lanes = slice(g * POOL_GROUP, (g + 1) * POOL_GROUP)
            dp = _dot_nt(dext[:, lanes], wp_ref[g].astype(BF16))
            dps.append(dp[:tile])
            scaled.append(dp / counts[g])
        sums = _window_sums(jnp.concatenate(scaled, axis=1), forward=True)
        du = []
        for g, w in enumerate(POOL_WINDOWS):
            lanes = slice(g * POOL_GROUP, (g + 1) * POOL_GROUP)
            du.append((sums[w][:tile, lanes] - dps[g]).astype(BF16))
            dwp_ref[g] += _dot_tn(parts[g].astype(BF16), dext[:tile, lanes])
        du = jnp.concatenate(du, axis=1)
        for d in range(per):
            dproj = du[:, d * cs:(d + 1) * cs]
            dproj_ref[:, d * cs:(d + 1) * cs] = dproj
            dw_ref[d] += _dot(ht, dproj)

    row = pl.BlockSpec((tile, C), lambda i: (i, 0))
    return pl.pallas_call(
        body, name="bwd_pool_w_in", grid=(n_tiles,),
        in_specs=[row, _prev_halo_spec(tile, C), row, _next_halo_spec(tile, C, n_tiles), _const(w_pool.shape),
                  row, row, row, pl.BlockSpec((D, tile), lambda i: (0, i))],
        out_specs=[pl.BlockSpec((tile, 4 * C), lambda i: (i, 0)),
                   pl.BlockSpec((n_blocks, D, cs), lambda i: (0, 0, 0), pipeline_mode=pl.Buffered(1)),
                   pl.BlockSpec(w_pool.shape, lambda i: (0, 0, 0))],
        out_shape=[jax.ShapeDtypeStruct((S, 4 * C), BF16), jax.ShapeDtypeStruct((n_blocks, D, cs), F32),
                   jax.ShapeDtypeStruct(w_pool.shape, F32)],
        compiler_params=_params("arbitrary"),
    )(u, u, d_pool, d_pool, w_pool, dq, dk, dv, h1_t)


def _bwd_x(dproj, w_in_t, x, dx2, g1, tile, ex):
    S, D = x.shape
    n_tiles = S // tile

    def body(dp_ref, w_ref, x_ref, dx2_ref, g_ref, *rest):
        dx_ref, dg_ref = rest[ex.n:ex.n + 2]
        ex_refs = ex.split(rest[:ex.n] + rest[ex.n + 2:])
        first, last = _grid_ends((n_tiles,))

        @pl.when(first)
        def _():
            ex.start(*ex_refs)
            dg_ref[...] = jnp.zeros_like(dg_ref)

        dh = _dot(dp_ref[...], w_ref[...])
        xf = x_ref[...]
        r1 = _rms(xf)
        n1 = xf * r1
        dg_ref[...] += _colsum(dh * n1)
        dx_ref[...] = dx2_ref[...] + _norm_bwd(dh * g_ref[...], n1, r1)

        @pl.when(last)
        def _():
            ex.wait(*ex_refs)

    row = lambda w: pl.BlockSpec((tile, w), lambda i: (i, 0))
    outs = pl.pallas_call(
        body, name="bwd_x", grid=(n_tiles,),
        in_specs=[row(w_in_t.shape[0]), _const(w_in_t.shape), row(D), row(D), _const((1, D))] + ex.specs,
        out_specs=[row(D), pl.BlockSpec((1, D), lambda i: (0, 0))] + ex.specs,
        out_shape=[jax.ShapeDtypeStruct((S, D), F32), jax.ShapeDtypeStruct((1, D), F32)] + ex.out_shape,
        scratch_shapes=ex.scratch,
        compiler_params=_params("arbitrary"),
    )(dproj, w_in_t, x, dx2, g1, *ex.arrays)
    return outs[0], outs[1], outs[2:]


def _mesh_position():
    x, y, c = lax.axis_index("x"), lax.axis_index("y"), lax.axis_index("c")
    return x, y, c, 4 * x + 2 * y + c


def _peer(x, y, c, k):
    px = 1 - x if k & 4 else x
    py = 1 - y if k & 2 else y
    pc = 1 - c if k & 1 else c
    return (px, py, pc), 4 * px + 2 * py + pc


class _Exchange:
    def __init__(self, arrays, gather):
        self.arrays, self.gather, self.n = list(arrays), gather, len(arrays)
        self.out_shape = [jax.ShapeDtypeStruct(((N_DEV,) + a.shape) if gather else a.shape, a.dtype) for a in arrays]
        self.specs = [pl.BlockSpec(memory_space=pl.ANY)] * self.n
        copies = self.n * (N_DEV - 1)
        self.scratch = [pltpu.SemaphoreType.DMA((copies,)), pltpu.SemaphoreType.DMA((copies,)),
                        pltpu.SemaphoreType.DMA((self.n,))]

    def _copies(self, ins, outs, sems):
        send_sems, recv_sems, local_sems = sems
        x, y, c, me = _mesh_position()
        local, remote = [], []
        for a in range(self.n):
            mine = ins[a] if self.gather else ins[a].at[me]
            local.append(pltpu.make_async_copy(mine, outs[a].at[me], local_sems.at[a]))
            for k in range(1, N_DEV):
                peer, peer_idx = _peer(x, y, c, k)
                src = ins[a] if self.gather else ins[a].at[peer_idx]
                sem = a * (N_DEV - 1) + k - 1
                remote.append(pltpu.make_async_remote_copy(
                    src_ref=src, dst_ref=outs[a].at[me], send_sem=send_sems.at[sem], recv_sem=recv_sems.at[sem],
                    device_id=peer, device_id_type=MESH))
        return local, remote

    def start(self, ins, outs, sems):
        local, remote = self._copies(ins, outs, sems)
        for cp in local + remote:
            cp.start()

    def wait(self, ins, outs, sems):
        local, remote = self._copies(ins, outs, sems)
        for cp in remote:
            cp.wait_send()
        for cp in remote:
            cp.wait_recv()
        for cp in local:
            cp.wait()

    def split(self, refs):
        return refs[:self.n], refs[self.n:2 * self.n], refs[2 * self.n:]


class _ChipGather(_Exchange):
    def __init__(self, arrays):
        super().__init__(arrays, gather=True)

    def _plan(self, ins, outs, sems, waiting):
        send_sems, recv_sems, local_sems = sems
        x, y, c, me = _mesh_position()
        sibling = (x, y, 1 - c)
        chips = [(1 - x, y), (x, 1 - y), (1 - x, 1 - y)]
        local, first, passed, arrivals = [], [], [], []
        for a in range(self.n):
            def copy(k, block, to, src=None, a=a):
                rows = outs[a].at[block]
                return pltpu.make_async_remote_copy(
                    src_ref=rows if src is None else src, dst_ref=rows, send_sem=send_sems.at[a * (N_DEV - 1) + k],
                    recv_sem=recv_sems.at[a * (N_DEV - 1) + k], device_id=to, device_id_type=MESH)

            local.append(pltpu.make_async_copy(ins[a], outs[a].at[me], local_sems.at[a]))
            first.append(copy(0, me, sibling, src=ins[a]))
            first += [copy(1 + j, me, (px, py, c), src=ins[a]) for j, (px, py) in enumerate(chips)]
            if waiting:
                passed.append([copy(4 + j, 4 * px + 2 * py + c, sibling) for j, (px, py) in enumerate(chips)])
                arrivals.append([copy(k, me, sibling) for k in range(N_DEV - 1)])
        return local, first, passed, arrivals

    def start(self, ins, outs, sems):
        local, first, _, _ = self._plan(ins, outs, sems, waiting=False)
        for cp in local + first:
            cp.start()

    def wait(self, ins, outs, sems):
        local, first, passed, arrivals = self._plan(ins, outs, sems, waiting=True)
        for a in range(self.n):
            for j in range(3):
                arrivals[a][1 + j].wait_recv()
                passed[a][j].start()
        for a in range(self.n):
            arrivals[a][0].wait_recv()
            for j in range(3):
                arrivals[a][4 + j].wait_recv()
        for cp in first + [cp for row in passed for cp in row]:
            cp.wait_send()
        for cp in local:
            cp.wait()


def _all_to_all(arrays, gather, name):
    ex = _ChipGather(arrays) if gather else _Exchange(arrays, gather)

    def body(*refs):
        ins, outs, sems = ex.split(refs)
        ex.start(ins, outs, sems)
        ex.wait(ins, outs, sems)

    return pl.pallas_call(body, name=name, in_specs=ex.specs, out_specs=ex.specs, out_shape=ex.out_shape,
                          scratch_shapes=ex.scratch)(*ex.arrays)


def _reduce_adamw(parts, w, m, v, rows):
    R, C = w.shape

    def body(p_ref, w_ref, m_ref, v_ref, g_ref, d_ref, nm_ref, nv_ref):
        g = p_ref[0].astype(F32)
        for s in range(1, N_DEV):
            g = g + p_ref[s].astype(F32)
        g_ref[...] = g
        m_new = ADAM_B1 * m_ref[...] + (1.0 - ADAM_B1) * g
        v_new = ADAM_B2 * v_ref[...] + (1.0 - ADAM_B2) * (g * g)
        m_hat = m_new / (1.0 - ADAM_B1 ** ADAM_STEP)
        v_hat = v_new / (1.0 - ADAM_B2 ** ADAM_STEP)
        d_ref[...] = -ADAM_LR * (m_hat / (jnp.sqrt(v_hat) + ADAM_EPS) + ADAM_WD * w_ref[...])
        nm_ref[...] = m_new
        nv_ref[...] = v_new

    row = pl.BlockSpec((rows, C), lambda i: (i, 0))
    return pl.pallas_call(
        body, name="reduce_adamw", grid=(R // rows,),
        in_specs=[pl.BlockSpec((N_DEV, rows, C), lambda i: (0, i, 0)), row, row, row],
        out_specs=[row] * 4, out_shape=[jax.ShapeDtypeStruct((R, C), F32)] * 4,
        compiler_params=_params("parallel"),
    )(parts, w, m, v)


def _row_tile(rows, cols):
    fits = [t for t in range(8, rows + 1, 8) if rows % t == 0 and N_DEV * t * cols * 4 <= 4 * 1024 * 1024]
    return max(fits) if fits else rows


SMALL_COLS = 1024


def _pack_small(vals):
    rows = []
    for a in vals:
        flat = a.reshape(-1)
        pad = (-flat.shape[0]) % SMALL_COLS
        rows.append(jnp.pad(flat, (0, pad)).reshape(-1, SMALL_COLS))
    packed = jnp.concatenate(rows, axis=0)
    return jnp.pad(packed, ((0, (-packed.shape[0]) % 8), (0, 0)))


def _unpack_small(packed, like):
    out, r = [], 0
    for a in like:
        n = a.size
        nr = -(-n // SMALL_COLS)
        out.append(packed[r:r + nr].reshape(-1)[:n].reshape(a.shape))
        r += nr
    return out


def kernel(x, norm_mix_pre, w_in, w_pool, pool_scale, attn_scale, w_out, norm_mix_post, norm_ffn_pre, w_up, conv_w, conv_b, w_down, norm_ffn_post, loss_target, m_norm_mix_pre, m_w_in, m_w_pool, m_pool_scale, m_attn_scale, m_w_out, m_norm_mix_post, m_norm_ffn_pre, m_w_up, m_conv_w, m_conv_b, m_w_down, m_norm_ffn_post, v_norm_mix_pre, v_w_in, v_w_pool, v_pool_scale, v_attn_scale, v_w_out, v_norm_mix_post, v_norm_ffn_pre, v_w_up, v_conv_w, v_conv_b, v_w_down, v_norm_ffn_post):
    S, D = x.shape[1], x.shape[2]
    d_ff_block = w_up.shape[2]

    xs, target = x[0], loss_target[0]
    g1, g2, g3, g4 = norm_mix_pre, norm_mix_post, norm_ffn_pre, norm_ffn_post
    big = min(512, S)
    small = min(256, S)
    n_pairs = pool_scale.shape[1] // QB
    conv_b_g = conv_b.reshape(N_DEV, 1, d_ff_block)

    (w_in_g,) = _all_to_all([w_in[0].astype(BF16)], gather=True, name="gather_w_in")
    h1_t, u, qkv, pool_out = _fwd_inproj_pool(xs, g1, w_in_g, w_pool[0], big)
    attn_out, (w_out_g, w_up_g, w_down_g, conv_w_g) = _fwd_attn(
        qkv, n_pairs, _ChipGather([w_out[0].astype(BF16), w_up[0].astype(BF16), w_down[0].astype(BF16), conv_w[0]]),
        min(ATTN_FWD_BLOCKS, S // QB))
    w_out_full = w_out_g.reshape(D, D)
    w_down4 = w_down_g.reshape(D_FF_SHARDS, d_ff_block, D)
    mix, x2, h2_t = _fwd_outproj(pool_out, attn_out, pool_scale, attn_scale, w_out_full, xs, g2, g3, big)
    upre, gate_val, dy, df, loss_cols, dg4 = _fwd_ffn_loss(h2_t, w_up_g, conv_w_g, conv_b_g, w_down4, x2, target, g4, small)
    loss = lax.psum(0.5 * jnp.sum(loss_cols) / D, ("x", "y", "c"))

    dupre_g, dupre_v, d_wd4, d_wup, d_cb, d_cw = _bwd_ffn_blocks(gate_val, upre, conv_w_g, w_down4, df, h2_t, min(1024, S))
    dx2, dmix, dg3, dg2 = _bwd_ffn_tokens(dupre_g, dupre_v, w_up_g, x2, dy, mix, g2, g3, big)
    d_pool, d_attn, d_wout, d_ps, d_as = _bwd_outproj(dmix, w_out_full, pool_out, attn_out, pool_scale, attn_scale, min(1024, S))
    d_wdown_g = d_wd4.reshape(N_DEV, w_down.shape[1], D)
    d_wout_g = d_wout.reshape(N_DEV, D // N_DEV, D)
    dq, dk, dv, late_parts = _bwd_attn(qkv, d_attn, n_pairs, _Exchange([d_wout_g, d_wup, d_wdown_g, d_cw], gather=False),
                                       min(ATTN_BWD_BLOCKS, S // QB))
    dproj, d_win, d_wp = _bwd_pool_w_in(u, d_pool, w_pool[0], dq, dk, dv, h1_t, N_DEV, min(1024, S))
    w_in_t = w_in_g.transpose(0, 2, 1).reshape(-1, D)
    dx, dg1, (win_parts,) = _bwd_x(dproj, w_in_t, xs, dx2, g1, big, _Exchange([d_win], gather=False))
    big_parts = [win_parts] + list(late_parts)
    r = dict(dx=dx, g1=dg1, w_pool=d_wp, pool_scale=d_ps, attn_scale=d_as, g2=dg2, g3=dg3, conv_b=d_cb, g4=dg4)

    small_names = ["norm_mix_pre", "w_pool", "pool_scale", "attn_scale", "norm_mix_post", "norm_ffn_pre", "conv_b", "norm_ffn_post"]
    small_w = dict(norm_mix_pre=norm_mix_pre, w_pool=w_pool, pool_scale=pool_scale, attn_scale=attn_scale,
                   norm_mix_post=norm_mix_post, norm_ffn_pre=norm_ffn_pre, conv_b=conv_b, norm_ffn_post=norm_ffn_post)
    small_m = dict(norm_mix_pre=m_norm_mix_pre, w_pool=m_w_pool, pool_scale=m_pool_scale, attn_scale=m_attn_scale,
                   norm_mix_post=m_norm_mix_post, norm_ffn_pre=m_norm_ffn_pre, conv_b=m_conv_b, norm_ffn_post=m_norm_ffn_post)
    small_v = dict(norm_mix_pre=v_norm_mix_pre, w_pool=v_w_pool, pool_scale=v_pool_scale, attn_scale=v_attn_scale,
                   norm_mix_post=v_norm_mix_post, norm_ffn_pre=v_norm_ffn_pre, conv_b=v_conv_b, norm_ffn_post=v_norm_ffn_post)
    small_g = dict(norm_mix_pre=r["g1"], w_pool=r["w_pool"], pool_scale=r["pool_scale"], attn_scale=r["attn_scale"],
                   norm_mix_post=r["g2"], norm_ffn_pre=r["g3"], conv_b=r["conv_b"], norm_ffn_post=r["g4"])
    like = [small_w[n] for n in small_names]
    packed_g = _pack_small([small_g[n] for n in small_names])

    (small_parts,) = _all_to_all([packed_g], gather=True, name="gather_small_grads")

    def update(parts, w, m, v):
        R, C = w.shape
        return _reduce_adamw(parts, w, m, v, _row_tile(R, C))

    res = {}
    res["w_in"] = update(big_parts[0], w_in[0], m_w_in[0], v_w_in[0])
    res["w_out"] = update(big_parts[1], w_out[0], m_w_out[0], v_w_out[0])
    res["w_up"] = update(big_parts[2], w_up[0], m_w_up[0], v_w_up[0])
    res["w_down"] = update(big_parts[3], w_down[0], m_w_down[0], v_w_down[0])
    res["conv_w"] = update(big_parts[4], conv_w[0], m_conv_w[0], v_conv_w[0])
    small_res = update(small_parts, _pack_small(like), _pack_small([small_m[n] for n in small_names]),
                       _pack_small([small_v[n] for n in small_names]))
    small_res = [_unpack_small(t, like) for t in small_res]
    for idx, n in enumerate(small_names):
        res[n] = tuple(t[idx] for t in small_res)

    order = ["norm_mix_pre", "w_in", "w_pool", "pool_scale", "attn_scale", "w_out", "norm_mix_post", "norm_ffn_pre",
             "w_up", "conv_w", "conv_b", "w_down", "norm_ffn_post"]
    shaped = {n: tuple(t.reshape(s.shape) for t in res[n])
              for n, s in dict(norm_mix_pre=norm_mix_pre, w_in=w_in, w_pool=w_pool, pool_scale=pool_scale, attn_scale=attn_scale,
                               w_out=w_out, norm_mix_post=norm_mix_post, norm_ffn_pre=norm_ffn_pre, w_up=w_up, conv_w=conv_w,
                               conv_b=conv_b, w_down=w_down, norm_ffn_post=norm_ffn_post).items()}
    outs = [loss, r["dx"].reshape(x.shape)]
    for k in range(4):
        outs += [shaped[n][k] for n in order]
    return tuple(outs)
```

```python
import functools

import jax
import jax.numpy as jnp
from jax import lax
from jax.experimental import pallas as pl
from jax.experimental.pallas import tpu as pltpu

F32 = jnp.float32
BF16 = jnp.bfloat16

N_DEV = 8
EPS = 1e-6
POOL_WINDOWS = (2, 4, 8, 16)
POOL_GROUP = 128
HALO = 16
HEAD_DIM = 64
QB = 128
ATTN_SCALE = HEAD_DIM ** -0.5
ATTN_FWD_BLOCKS = 16
ATTN_BWD_BLOCKS = 8
EXP_UNDERFLOW = -88.0
D_FF_SHARDS = 4

ADAM_LR = 0.001
ADAM_B1 = 0.9
ADAM_B2 = 0.999
ADAM_EPS = 1e-08
ADAM_WD = 0.01
ADAM_STEP = 10

VMEM_LIMIT_V7X = 56 * 1024 * 1024
MESH = pl.DeviceIdType.MESH


def _params(*semantics):
    return pltpu.CompilerParams(dimension_semantics=semantics, vmem_limit_bytes=VMEM_LIMIT_V7X)


def _const(shape):
    zeros = (0,) * len(shape)
    return pl.BlockSpec(shape, lambda *_: zeros, pipeline_mode=pl.Buffered(1))


def _dot(a, b):
    return jnp.dot(a, b, preferred_element_type=F32)


def _dot_nt(a, b):
    return lax.dot_general(a, b, (((1,), (1,)), ((), ())), preferred_element_type=F32)


def _dot_tn(a, b):
    return lax.dot_general(a, b, (((0,), (0,)), ((), ())), preferred_element_type=F32)


def _rms(v):
    return lax.rsqrt(jnp.mean(v * v, axis=-1, keepdims=True) + EPS)


def _norm_bwd(dn_times_gain, n, r):
    return r * (dn_times_gain - n * jnp.mean(dn_times_gain * n, axis=-1, keepdims=True))


def _zero_when(first, *refs):
    @pl.when(first)
    def _():
        for ref in refs:
            ref[...] = jnp.zeros_like(ref)


def _colsum(v):
    return jnp.sum(v, axis=0, keepdims=True)


def _grid_ends(grid):
    ids = [pl.program_id(a) for a in range(len(grid))]
    first = functools.reduce(jnp.logical_and, [i == 0 for i in ids])
    last = functools.reduce(jnp.logical_and, [i == n - 1 for i, n in zip(ids, grid)])
    return first, last


def _fwd_inproj_pool(x, g1, w_in_g, w_pool, tile):
    S, D = x.shape
    nb, _, cs = w_in_g.shape
    d_pool = 2 * cs

    def body(x_ref, g_ref, w_ref, wp_ref, ht_ref, u_ref, qkv_ref, pool_ref, halo_ref):
        i = pl.program_id(0)
        _zero_when(i == 0, halo_ref)
        xf = x_ref[...]
        h = (xf * _rms(xf) * g_ref[...]).astype(BF16)
        ht_ref[...] = h.T
        u = jnp.concatenate([_dot(h, w_ref[0]), _dot(h, w_ref[1])], axis=1)
        u_ref[...] = u
        for d in range(2, nb):
            qkv_ref[:, (d - 2) * cs:(d - 1) * cs] = _dot(h, w_ref[d]).astype(BF16)
        parts = _pool_deviation(u, halo_ref[...], i * tile)
        halo_ref[...] = u[tile - HALO:, :]
        for g, p in enumerate(parts):
            pool_ref[:, g * POOL_GROUP:(g + 1) * POOL_GROUP] = _dot(p.astype(BF16), wp_ref[g].astype(BF16))

    row = lambda w: pl.BlockSpec((tile, w), lambda i: (i, 0))
    return pl.pallas_call(
        body, name="fwd_inproj_pool", grid=(S // tile,),
        in_specs=[row(D), _const((1, D)), _const(w_in_g.shape), _const(w_pool.shape)],
        out_specs=[pl.BlockSpec((D, tile), lambda i: (0, i)), row(d_pool), row(3 * d_pool), row(d_pool)],
        out_shape=[jax.ShapeDtypeStruct((D, S), BF16), jax.ShapeDtypeStruct((S, d_pool), F32),
                   jax.ShapeDtypeStruct((S, 3 * d_pool), BF16), jax.ShapeDtypeStruct((S, d_pool), F32)],
        scratch_shapes=[pltpu.VMEM((HALO, d_pool), F32)],
        compiler_params=_params("arbitrary"),
    )(x, g1, w_in_g, w_pool)


def _window_sums(ext, forward):
    n = ext.shape[0]
    sums, s, sh = {}, ext, 1
    while sh < POOL_WINDOWS[-1]:
        s = s + pltpu.roll(s, (n - sh) if forward else sh, axis=0)
        sh *= 2
        sums[sh] = s
    return sums


def _pool_counts(t0, rows):
    t1 = (lax.broadcasted_iota(jnp.int32, (rows, 1), 0) + t0 + 1).astype(F32)
    return [jnp.minimum(t1, float(w)) for w in POOL_WINDOWS]


def _pool_deviation(u, halo, t0):
    T = u.shape[0]
    sums = _window_sums(jnp.concatenate([halo, u], axis=0), forward=False)
    counts = _pool_counts(t0, T)
    parts = []
    for g, w in enumerate(POOL_WINDOWS):
        lanes = slice(g * POOL_GROUP, (g + 1) * POOL_GROUP)
        parts.append(sums[w][HALO:, lanes] / counts[g] - u[:, lanes])
    return parts


def _prev_halo_spec(tile, width):
    return pl.BlockSpec((HALO, width), lambda i: (jnp.maximum(i * (tile // HALO) - 1, 0), 0))


def _next_halo_spec(tile, width, n_tiles):
    last = n_tiles * (tile // HALO) - 1
    return pl.BlockSpec((HALO, width), lambda i: (jnp.minimum((i + 1) * (tile // HALO), last), 0))


def _low_lanes():
    return lax.broadcasted_iota(jnp.int32, (QB, 2 * HEAD_DIM), 1) < HEAD_DIM


LOG_PIECES = 2
GRAD_PIECES = 2


def _triangle(inclusive, pieces):
    row = lax.broadcasted_iota(jnp.int32, (pieces * QB, QB), 0) % QB
    col = lax.broadcasted_iota(jnp.int32, (pieces * QB, QB), 1)
    return ((row >= col) if inclusive else (row > col)).astype(BF16)


def _pieces(v, n):
    out, rest = [], v
    for _ in range(n - 1):
        piece = rest.astype(BF16)
        out.append(piece)
        rest = rest - piece.astype(F32)
    out.append(rest.astype(BF16))
    return jnp.concatenate(out, axis=1)


def _causal_mask(width, offset):
    row = lax.broadcasted_iota(jnp.int32, (QB, width), 0)
    col = lax.broadcasted_iota(jnp.int32, (QB, width), 1)
    return col < row + offset


def _row_sums(vals, carry):
    for b in reversed(range(vals.shape[1] // QB)):
        carry = carry + jnp.sum(vals[:, b * QB:(b + 1) * QB], axis=1, keepdims=True)
    return carry


def _suffix_sums(vals, tri, carry):
    n = vals.shape[1] // QB
    out, run = [None] * n, carry
    for b in reversed(range(n)):
        blk = vals[:, b * QB:(b + 1) * QB]
        out[b] = _dot(_pieces(blk, tri.shape[0] // QB), tri) + run
        run = run + jnp.sum(blk, axis=1, keepdims=True)
    return (out[0] if n == 1 else jnp.concatenate(out, axis=1)), run


def _attn_tiles(qhs, kws, masks, carries, after_s):
    return _attn_weights(_attn_scores(qhs, kws, masks), masks, carries, after_s)


def _attn_scores(qhs, kws, masks):
    zs = [_dot_nt(qh, kw) * ATTN_SCALE for qh, kw in zip(qhs, kws)]
    es = [jnp.exp(-jnp.abs(z)) for z in zs]
    softplus = [jnp.maximum(z, 0.0) + jnp.log(1.0 + e) for z, e in zip(zs, es)]
    log_1m_beta = [-sp if m is None else jnp.where(m, -sp, 0.0) for sp, m in zip(softplus, masks)]
    return list(zip(zs, es, softplus, log_1m_beta))


def _attn_weights(scores, masks, carries, after_s):
    sums = [_suffix_sums(l, after_s, c) for (_, _, _, l), c in zip(scores, carries)]
    weights = [jnp.exp(z - sp + st) for (z, _, sp, _), (st, _) in zip(scores, sums)]
    weights = [a if m is None else jnp.where(m, a, 0.0) for a, m in zip(weights, masks)]
    return [(z, e, a, c) for (z, e, _, _), a, (_, c) in zip(scores, weights, sums)]


def _split_heads(v, low_lanes):
    return jnp.where(low_lanes, v, 0.0).astype(BF16), jnp.where(low_lanes, 0.0, v).astype(BF16)


def _sweep_done(c0, c1):
    return (jnp.maximum(jnp.max(c0), jnp.max(c1)) < EXP_UNDERFLOW).astype(jnp.int32)


def _all_done(carries):
    return jnp.max(functools.reduce(jnp.maximum, carries)) < EXP_UNDERFLOW


def _first_window(i):
    first_blk = jnp.maximum(i - 1, 0)
    return first_blk, pl.multiple_of(first_blk * QB, QB), (i - first_blk) * QB


def _fwd_attn(qkv, n_pairs, ex, subs):
    S = qkv.shape[0]
    n_steps = S // (subs * QB)

    def body(q_ref, k_ref, v_ref, *rest):
        o_ref = rest[ex.n]
        ex_refs = ex.split(rest[:ex.n] + rest[ex.n + 1:])
        first_step, last_step = _grid_ends((n_pairs, n_steps))

        @pl.when(first_step)
        def _():
            ex.start(*ex_refs)

        low_lanes = _low_lanes()
        after_s = _triangle(False, LOG_PIECES)
        zero = jnp.zeros((QB, 1), F32)

        def cond(c):
            return jnp.logical_and(c[0] >= 0, c[1] == 0)

        qhs, kws, vws, masks, first_blks = [], [], [], [], []
        for sub in range(subs):
            i = pl.program_id(1) * subs + sub
            first_blk, start, offset = _first_window(i)
            first_blks.append(first_blk)
            qhs += _split_heads(q_ref[sub * QB:(sub + 1) * QB, :].astype(F32), low_lanes)
            kws += [k_ref[pl.ds(start, 2 * QB), :]] * 2
            vws += [v_ref[pl.ds(start, 2 * QB), :]] * 2
            masks += [_causal_mask(2 * QB, offset)] * 2
        tiles = _attn_tiles(qhs, kws, masks, [zero] * len(qhs), after_s)
        outs = [_dot(t[2].astype(BF16), vw) for t, vw in zip(tiles, vws)]

        first_out = [jnp.where(low_lanes, outs[2 * sub], outs[2 * sub + 1]) for sub in range(subs)]

        def sweep_on():
            final = []
            for sub in range(subs):
                def step(c, qh=qhs[2 * sub:2 * sub + 2]):
                    j, _, acc, c0, c1 = c
                    at = pl.multiple_of(j * QB, QB)
                    kb = k_ref[pl.ds(at, QB), :]
                    vb = v_ref[pl.ds(at, QB), :]
                    far = _attn_tiles(qh, [kb, kb], [None, None], [c0, c1], after_s)
                    acc = acc + jnp.where(low_lanes, _dot(far[0][2].astype(BF16), vb), _dot(far[1][2].astype(BF16), vb))
                    return j - 1, _sweep_done(far[0][3], far[1][3]), acc, far[0][3], far[1][3]

                c0, c1 = tiles[2 * sub][3], tiles[2 * sub + 1][3]
                final.append(lax.while_loop(cond, step, (first_blks[sub] - 1, _sweep_done(c0, c1), first_out[sub], c0, c1))[2])
            return tuple(final)

        final = lax.cond(_all_done([t[3] for t in tiles]), lambda: tuple(first_out), sweep_on)
        for sub in range(subs):
            o_ref[sub * QB:(sub + 1) * QB, :] = final[sub]

        @pl.when(last_step)
        def _():
            ex.wait(*ex_refs)

    outs = pl.pallas_call(
        body, name="fwd_attn", grid=(n_pairs, n_steps),
        in_specs=[pl.BlockSpec((subs * QB, QB), lambda p, i: (i, p)),
                  pl.BlockSpec((S, QB), lambda p, i: (0, n_pairs + p), pipeline_mode=pl.Buffered(1)),
                  pl.BlockSpec((S, QB), lambda p, i: (0, 2 * n_pairs + p), pipeline_mode=pl.Buffered(1))] + ex.specs,
        out_specs=[pl.BlockSpec((subs * QB, QB), lambda p, i: (i, p))] + ex.specs,
        out_shape=[jax.ShapeDtypeStruct((S, n_pairs * QB), F32)] + ex.out_shape,
        scratch_shapes=ex.scratch,
        compiler_params=_params("arbitrary", "arbitrary"),
    )(qkv, qkv, qkv, *ex.arrays)
    return outs[0], outs[1:]


def _normalized_heads(pool_out, attn_out):
    rp, ra = _rms(pool_out), _rms(attn_out)
    return pool_out * rp, rp, attn_out * ra, ra


def _fwd_outproj(pool_out, attn_out, pool_scale, attn_scale, w_out, x, g2, g3, tile):
    S, D = x.shape
    C = pool_out.shape[1]

    def body(p_ref, a_ref, ps_ref, as_ref, w_ref, x_ref, g2_ref, g3_ref, mix_ref, x2_ref, h2t_ref):
        n_p, _, n_a, _ = _normalized_heads(p_ref[...], a_ref[...])
        mix = _dot((n_p * ps_ref[...]).astype(BF16), w_ref[:C, :]) + _dot((n_a * as_ref[...]).astype(BF16), w_ref[C:, :])
        mix_ref[...] = mix
        x2 = x_ref[...] + mix * _rms(mix) * g2_ref[...]
        x2_ref[...] = x2
        h2t_ref[...] = (x2 * _rms(x2) * g3_ref[...]).astype(BF16).T

    row = lambda w: pl.BlockSpec((tile, w), lambda i: (i, 0))
    return pl.pallas_call(
        body, name="fwd_outproj", grid=(S // tile,),
        in_specs=[row(C), row(C), _const((1, C)), _const((1, C)), _const(w_out.shape), row(D), _const((1, D)), _const((1, D))],
        out_specs=[row(D), row(D), pl.BlockSpec((D, tile), lambda i: (0, i))],
        out_shape=[jax.ShapeDtypeStruct((S, D), F32), jax.ShapeDtypeStruct((S, D), F32), jax.ShapeDtypeStruct((D, S), BF16)],
        compiler_params=_params("parallel"),
    )(pool_out, attn_out, pool_scale, attn_scale, w_out, x, g2, g3)


def _conv_taps(tile_rows, halo_rows):
    T = tile_rows.shape[0]
    ext = jnp.concatenate([halo_rows.astype(F32), tile_rows.astype(F32)], axis=0)
    return pltpu.roll(ext, 2, axis=0)[HALO:], pltpu.roll(ext, 1, axis=0)[HALO:], ext[HALO:]


def _tap_rows(cw_ref, d):
    return [cw_ref[d, k:k + 1, :] for k in range(3)]


def _gated_unit(taps_gate, taps_val, cw_gate, cw_val, cb_gate, cb_val):
    gate = cw_gate[0] * taps_gate[0] + cw_gate[1] * taps_gate[1] + cw_gate[2] * taps_gate[2] + cb_gate
    val = cw_val[0] * taps_val[0] + cw_val[1] * taps_val[1] + cw_val[2] * taps_val[2] + cb_val
    sig = 1.0 / (1.0 + jnp.exp(-gate))
    return gate, val, sig


def _fwd_ffn_loss(h2_t, w_up_g, conv_w_g, conv_b_g, w_down4, x2, target, g4, tile):
    S, D = x2.shape
    nb, _, cs = w_up_g.shape
    half = D_FF_SHARDS

    def body(h_ref, w_ref, cw_ref, cb_ref, wd_ref, x2_ref, t_ref, g4_ref, upre_ref, gv_ref, dy_ref, df_ref, loss_ref, dg4_ref, halo_ref):
        _zero_when(pl.program_id(0) == 0, loss_ref, dg4_ref, halo_ref)
        h = h_ref[...].T

        def up(s):
            return _dot(h, w_ref[s]), _dot(h, w_ref[s + half])

        f = jnp.zeros((tile, D), F32)
        ahead = up(0)
        for s in range(half):
            ug, uv = ahead
            if s + 1 < half:
                ahead = up(s + 1)
            upre_ref[s] = ug.astype(BF16)
            upre_ref[s + half] = uv.astype(BF16)
            gate, val, sig = _gated_unit(_conv_taps(ug, halo_ref[s]), _conv_taps(uv, halo_ref[s + half]),
                                         _tap_rows(cw_ref, s), _tap_rows(cw_ref, s + half), cb_ref[s], cb_ref[s + half])
            halo_ref[s] = ug[tile - HALO:, :]
            halo_ref[s + half] = uv[tile - HALO:, :]
            gv_ref[s] = gate.astype(BF16)
            gv_ref[s + half] = val.astype(BF16)
            f = f + _dot((gate * sig * val).astype(BF16), wd_ref[s])
        r4 = _rms(f)
        n4 = f * r4
        err = x2_ref[...] + n4 * g4_ref[...] - t_ref[...]
        dy = err * (1.0 / D)
        dy_ref[...] = dy
        df_ref[...] = _norm_bwd(dy * g4_ref[...], n4, r4).astype(BF16)
        loss_ref[...] += _colsum(err * err)
        dg4_ref[...] += _colsum(dy * n4)

    row = lambda w: pl.BlockSpec((tile, w), lambda i: (i, 0))
    return pl.pallas_call(
        body, name="fwd_ffn_loss", grid=(S // tile,),
        in_specs=[pl.BlockSpec((D, tile), lambda i: (0, i)), _const(w_up_g.shape), _const(conv_w_g.shape), _const(conv_b_g.shape),
                  _const(w_down4.shape), row(D), row(D), _const((1, D))],
        out_specs=[pl.BlockSpec((nb, tile, cs), lambda i: (0, i, 0)), pl.BlockSpec((nb, tile, cs), lambda i: (0, i, 0)), row(D), row(D),
                   pl.BlockSpec((1, D), lambda i: (0, 0)), pl.BlockSpec((1, D), lambda i: (0, 0))],
        out_shape=[jax.ShapeDtypeStruct((nb, S, cs), BF16), jax.ShapeDtypeStruct((nb, S, cs), BF16),
                   jax.ShapeDtypeStruct((S, D), F32), jax.ShapeDtypeStruct((S, D), BF16),
                   jax.ShapeDtypeStruct((1, D), F32), jax.ShapeDtypeStruct((1, D), F32)],
        scratch_shapes=[pltpu.VMEM((nb, HALO, cs), F32)],
        compiler_params=_params("arbitrary"),
    )(h2_t, w_up_g, conv_w_g, conv_b_g, w_down4, x2, target, g4)


def _bwd_ffn_blocks(gate_val, upre, conv_w_g, w_down4, df, h2_t, tile):
    nb, S, cs = upre.shape
    D = df.shape[1]
    n_tiles = S // tile
    half = D_FF_SHARDS

    def body(g_ref, v_ref, ug_ref, uv_ref, cwg_ref, cwv_ref, wd_ref, df_ref, ht_ref,
             dug_ref, duv_ref, dwd_ref, dwg_ref, dwv_ref, dbg_ref, dbv_ref, dcwg_ref, dcwv_ref, next_ref):
        _zero_when(pl.program_id(1) == 0, dwd_ref, dwg_ref, dwv_ref, dbg_ref, dbv_ref, dcwg_ref, dcwv_ref, next_ref)
        dfb = df_ref[...]
        dact = _dot_nt(dfb, wd_ref[0])
        gate, val = g_ref[0].astype(F32), v_ref[0].astype(F32)
        sig = 1.0 / (1.0 + jnp.exp(-gate))
        silu = gate * sig
        dwd_ref[0] += _dot_tn((silu * val).astype(BF16), dfb)
        ht = ht_ref[...]

        def through_conv(dup, slot, cw_ref, u_ref, du_ref, dw_ref, db_ref, dcw_ref):
            ext = jnp.concatenate([dup, next_ref[slot]], axis=0)
            n = ext.shape[0]
            shifted = (dup, pltpu.roll(ext, n - 1, axis=0)[:tile], pltpu.roll(ext, n - 2, axis=0)[:tile])
            next_ref[slot] = dup[:HALO]
            cw = _tap_rows(cw_ref, 0)
            dupre = (cw[2] * shifted[0] + cw[1] * shifted[1] + cw[0] * shifted[2]).astype(BF16)
            du_ref[0] = dupre
            dw_ref[0] += _dot(ht, dupre)
            u = u_ref[0].astype(F32)
            db_ref[0] += _colsum(dup)
            for k in range(3):
                dcw_ref[0, k:k + 1, :] += _colsum(shifted[2 - k] * u)

        through_conv(dact * (val * (sig + silu * (1.0 - sig))), 0, cwg_ref, ug_ref, dug_ref, dwg_ref, dbg_ref, dcwg_ref)
        through_conv(dact * silu, 1, cwv_ref, uv_ref, duv_ref, dwv_ref, dbv_ref, dcwv_ref)

    rev = lambda i: n_tiles - 1 - i
    blk = lambda off: pl.BlockSpec((1, tile, cs), lambda s, i: (s + off, rev(i), 0))
    par = lambda off, r: pl.BlockSpec((1, r, cs), lambda s, i: (s + off, 0, 0))
    acc = lambda r, c: pl.BlockSpec((1, r, c), lambda s, i: (s, 0, 0), pipeline_mode=pl.Buffered(1))
    outs = pl.pallas_call(
        body, name="bwd_ffn_blocks", grid=(half, n_tiles),
        in_specs=[blk(0), blk(half), blk(0), blk(half), par(0, 3), par(half, 3),
                  acc(cs, D), pl.BlockSpec((tile, D), lambda s, i: (rev(i), 0)), pl.BlockSpec((D, tile), lambda s, i: (0, rev(i)))],
        out_specs=[blk(0), blk(0), acc(cs, D), acc(D, cs), acc(D, cs), acc(1, cs), acc(1, cs), acc(3, cs), acc(3, cs)],
        out_shape=[jax.ShapeDtypeStruct((half, S, cs), BF16), jax.ShapeDtypeStruct((half, S, cs), BF16),
                   jax.ShapeDtypeStruct((half, cs, D), F32),
                   jax.ShapeDtypeStruct((half, D, cs), F32), jax.ShapeDtypeStruct((half, D, cs), F32),
                   jax.ShapeDtypeStruct((half, 1, cs), F32), jax.ShapeDtypeStruct((half, 1, cs), F32),
                   jax.ShapeDtypeStruct((half, 3, cs), F32), jax.ShapeDtypeStruct((half, 3, cs), F32)],
        scratch_shapes=[pltpu.VMEM((2, HALO, cs), F32)],
        compiler_params=_params("arbitrary", "arbitrary"),
    )(gate_val, gate_val, upre, upre, conv_w_g, conv_w_g, w_down4, df, h2_t)
    dupre_g, dupre_v, d_wd, d_wg, d_wv, dbg, dbv, dcwg, dcwv = outs
    return (dupre_g, dupre_v, d_wd, jnp.concatenate([d_wg, d_wv], axis=0), jnp.concatenate([dbg, dbv], axis=0),
            jnp.concatenate([dcwg, dcwv], axis=0))


def _bwd_ffn_tokens(dupre_g, dupre_v, w_up_g, x2, dy, mix, g2, g3, tile):
    half, S, cs = dupre_g.shape
    D = x2.shape[1]

    def body(dg_ref, dv_ref, w_ref, x2_ref, dy_ref, mix_ref, g2_ref, g3_ref, dx2_ref, dmix_ref, dg3_ref, dg2_ref):
        _zero_when(pl.program_id(0) == 0, dg3_ref, dg2_ref)
        parts = [_dot_nt(dg_ref[d], w_ref[d]) for d in range(half)] + [_dot_nt(dv_ref[d], w_ref[d + half]) for d in range(half)]
        while len(parts) > 1:
            parts = [a + b for a, b in zip(parts[::2], parts[1::2])]
        dh2 = parts[0]
        x2 = x2_ref[...]
        r3 = _rms(x2)
        n3 = x2 * r3
        dg3_ref[...] += _colsum(dh2 * n3)
        dx2 = dy_ref[...] + _norm_bwd(dh2 * g3_ref[...], n3, r3)
        dx2_ref[...] = dx2
        mix = mix_ref[...]
        r2 = _rms(mix)
        n2 = mix * r2
        dg2_ref[...] += _colsum(dx2 * n2)
        dmix_ref[...] = _norm_bwd(dx2 * g2_ref[...], n2, r2).astype(BF16)

    row = lambda w: pl.BlockSpec((tile, w), lambda i: (i, 0))
    blk = pl.BlockSpec((half, tile, cs), lambda i: (0, i, 0))
    acc = pl.BlockSpec((1, D), lambda i: (0, 0))
    return pl.pallas_call(
        body, name="bwd_ffn_tokens", grid=(S // tile,),
        in_specs=[blk, blk, _const(w_up_g.shape), row(D), row(D), row(D), _const((1, D)), _const((1, D))],
        out_specs=[row(D), row(D), acc, acc],
        out_shape=[jax.ShapeDtypeStruct((S, D), F32), jax.ShapeDtypeStruct((S, D), BF16),
                   jax.ShapeDtypeStruct((1, D), F32), jax.ShapeDtypeStruct((1, D), F32)],
        compiler_params=_params("arbitrary"),
    )(dupre_g, dupre_v, w_up_g, x2, dy, mix, g2, g3)


def _bwd_outproj(dmix, w_out, pool_out, attn_out, pool_scale, attn_scale, tile):
    S, D = dmix.shape
    C = pool_out.shape[1]

    def body(dm_ref, w_ref, p_ref, a_ref, ps_ref, as_ref, dp_ref, da_ref, dw_ref, dps_ref, das_ref):
        _zero_when(pl.program_id(0) == 0, dw_ref, dps_ref, das_ref)
        dmx = dm_ref[...]
        dmerged = _dot_nt(dmx, w_ref[...])
        n_p, r_p, n_a, r_a = _normalized_heads(p_ref[...], a_ref[...])
        merged = jnp.concatenate([(n_p * ps_ref[...]).astype(BF16), (n_a * as_ref[...]).astype(BF16)], axis=1)
        dw_ref[...] += _dot_tn(merged, dmx)
        dm_p, dm_a = dmerged[:, :C], dmerged[:, C:]
        dps_ref[...] += _colsum(dm_p * n_p)
        das_ref[...] += _colsum(dm_a * n_a)
        dp_ref[...] = _norm_bwd(dm_p * ps_ref[...], n_p, r_p)
        da_ref[...] = _norm_bwd(dm_a * as_ref[...], n_a, r_a)

    row = lambda w: pl.BlockSpec((tile, w), lambda i: (i, 0))
    return pl.pallas_call(
        body, name="bwd_outproj", grid=(S // tile,),
        in_specs=[row(D), _const(w_out.shape), row(C), row(C), _const((1, C)), _const((1, C))],
        out_specs=[row(C), row(C), pl.BlockSpec(w_out.shape, lambda i: (0, 0)),
                   pl.BlockSpec((1, C), lambda i: (0, 0)), pl.BlockSpec((1, C), lambda i: (0, 0))],
        out_shape=[jax.ShapeDtypeStruct((S, C), F32), jax.ShapeDtypeStruct((S, C), F32),
                   jax.ShapeDtypeStruct(w_out.shape, F32), jax.ShapeDtypeStruct((1, C), F32), jax.ShapeDtypeStruct((1, C), F32)],
        compiler_params=_params("arbitrary"),
    )(dmix, w_out, pool_out, attn_out, pool_scale, attn_scale)


def _bwd_attn(qkv, d_attn, n_pairs, ex, subs):
    S = qkv.shape[0]
    n_steps = S // (subs * QB)

    def body(q_ref, k_ref, v_ref, do_ref, *rest):
        dq_ref, dk_ref, dv_ref = rest[ex.n:ex.n + 3]
        ex_refs = ex.split(rest[:ex.n] + rest[ex.n + 3:])
        first_step, last_step = _grid_ends((n_pairs, n_steps))

        @pl.when(first_step)
        def _():
            ex.start(*ex_refs)

        @pl.when(pl.program_id(1) == 0)
        def _():
            dk_ref[...] = jnp.zeros_like(dk_ref)
            dv_ref[...] = jnp.zeros_like(dv_ref)

        low_lanes = _low_lanes()
        after_s, from_s = _triangle(False, LOG_PIECES), _triangle(True, GRAD_PIECES)
        zero = jnp.zeros((QB, 1), F32)

        def tiles(qhs, dohs, totals, kws, vws, masks, cs, gs, scores=None):
            fw = _attn_weights(scores or _attn_scores(qhs, kws, masks), masks, cs, after_s)
            gvals = [t[2] * _dot_nt(doh, vw) for t, doh, vw in zip(fw, dohs, vws)]
            sums = [_suffix_sums(g, from_s, g0) for g, g0 in zip(gvals, gs)]
            totals = [tot if m is None else tot + sm[1] for tot, m, sm in zip(totals, masks, sums)]
            dzs = []
            for (z, e, _, _), g, (nearer, _), tot, m in zip(fw, gvals, sums, totals, masks):
                inv = 1.0 / (1.0 + e)
                sig_abs, sig_neg = inv, e * inv
                pos = z >= 0.0
                dz = g * jnp.where(pos, sig_neg, sig_abs) - jnp.where(pos, sig_abs, sig_neg) * (tot - nearer)
                if m is not None:
                    dz = jnp.where(m, dz, 0.0)
                dzs.append((dz * ATTN_SCALE).astype(BF16))
            dqs = [_dot(dz, kw) for dz, kw in zip(dzs, kws)]
            dks = [_dot_tn(dz, qh) for dz, qh in zip(dzs, qhs)]
            dvs = [_dot_tn(t[2].astype(BF16), doh) for t, doh in zip(fw, dohs)]
            return [(dq, dk, dv, t[3], sm[1], tot) for dq, dk, dv, t, sm, tot in zip(dqs, dks, dvs, fw, sums, totals)]

        def cond(c):
            return jnp.logical_and(c[0] >= 0, c[1] == 0)

        qhs, dohs, kws, vws, masks, first_blks, starts = [], [], [], [], [], [], []
        for sub in range(subs):
            i = pl.program_id(1) * subs + sub
            rows = slice(sub * QB, (sub + 1) * QB)
            first_blk, start, offset = _first_window(i)
            first_blks.append(first_blk)
            starts.append(start)
            qhs += _split_heads(q_ref[rows, :].astype(F32), low_lanes)
            dohs += _split_heads(do_ref[rows, :], low_lanes)
            kws += [k_ref[pl.ds(start, 2 * QB), :]] * 2
            vws += [v_ref[pl.ds(start, 2 * QB), :]] * 2
            masks += [_causal_mask(2 * QB, offset)] * 2
        zeros = [zero] * len(qhs)

        scores = _attn_scores(qhs, kws, masks)
        c_first = [_row_sums(sc[3], zero) for sc in scores]
        all_done = _all_done(c_first)

        def far_totals():
            beyond = []
            for sub in range(subs):
                pair = slice(2 * sub, 2 * sub + 2)

                def far_sums(c, qh=qhs[pair], doh=dohs[pair]):
                    j, _, c0, c1, r0, r1 = c
                    at = pl.multiple_of(j * QB, QB)
                    kb = k_ref[pl.ds(at, QB), :]
                    vb = v_ref[pl.ds(at, QB), :]
                    far = _attn_tiles(qh, [kb, kb], [None, None], [c0, c1], after_s)
                    r0 = r0 + jnp.sum(far[0][2] * _dot_nt(doh[0], vb), axis=1, keepdims=True)
                    r1 = r1 + jnp.sum(far[1][2] * _dot_nt(doh[1], vb), axis=1, keepdims=True)
                    return j - 1, _sweep_done(far[0][3], far[1][3]), far[0][3], far[1][3], r0, r1

                c0, c1 = c_first[pair]
                far = lax.while_loop(cond, far_sums, (first_blks[sub] - 1, _sweep_done(c0, c1), c0, c1, zero, zero))
                beyond += [far[4], far[5]]
            return tuple(beyond)

        beyond_first = list(lax.cond(all_done, lambda: tuple(zeros), far_totals))
        done = tiles(qhs, dohs, beyond_first, kws, vws, masks, zeros, zeros, scores)
        for sub in range(subs):
            dk_ref[pl.ds(starts[sub], 2 * QB), :] += done[2 * sub][1] + done[2 * sub + 1][1]
            dv_ref[pl.ds(starts[sub], 2 * QB), :] += done[2 * sub][2] + done[2 * sub + 1][2]
        first_dq = [jnp.where(low_lanes, done[2 * sub][0], done[2 * sub + 1][0]) for sub in range(subs)]

        def sweep_on():
            final = []
            for sub in range(subs):
                pair = slice(2 * sub, 2 * sub + 2)
                t0, t1 = done[pair]

                def step(c, qh=qhs[pair], doh=dohs[pair], total=[t0[5], t1[5]]):
                    j, _, dq, c0, c1, s0, s1 = c
                    at = pl.multiple_of(j * QB, QB)
                    kb = k_ref[pl.ds(at, QB), :]
                    vb = v_ref[pl.ds(at, QB), :]
                    f0, f1 = tiles(qh, doh, total, [kb, kb], [vb, vb], [None, None], [c0, c1], [s0, s1])
                    dk_ref[pl.ds(at, QB), :] += f0[1] + f1[1]
                    dv_ref[pl.ds(at, QB), :] += f0[2] + f1[2]
                    return j - 1, _sweep_done(f0[3], f1[3]), dq + jnp.where(low_lanes, f0[0], f1[0]), f0[3], f1[3], f0[4], f1[4]

                init = (first_blks[sub] - 1, _sweep_done(t0[3], t1[3]), first_dq[sub], t0[3], t1[3], t0[4], t1[4])
                final.append(lax.while_loop(cond, step, init)[2])
            return tuple(final)

        final = lax.cond(all_done, lambda: tuple(first_dq), sweep_on)
        for sub in range(subs):
            dq_ref[sub * QB:(sub + 1) * QB, :] = final[sub]

        @pl.when(last_step)
        def _():
            ex.wait(*ex_refs)

    blk = pl.BlockSpec((subs * QB, QB), lambda p, i: (i, p))
    full = lambda off: pl.BlockSpec((S, QB), lambda p, i: (0, off + p), pipeline_mode=pl.Buffered(1))
    outs = pl.pallas_call(
        body, name="bwd_attn", grid=(n_pairs, n_steps),
        in_specs=[blk, full(n_pairs), full(2 * n_pairs), blk] + ex.specs,
        out_specs=[blk, full(0), full(0)] + ex.specs,
        out_shape=[jax.ShapeDtypeStruct((S, n_pairs * QB), F32)] * 3 + ex.out_shape,
        scratch_shapes=ex.scratch,
        compiler_params=_params("arbitrary", "arbitrary"),
    )(qkv, qkv, qkv, d_attn, *ex.arrays)
    return outs[0], outs[1], outs[2], outs[3:]


def _bwd_pool_w_in(u, d_pool, w_pool, dq, dk, dv, h1_t, n_blocks, tile):
    S, C = u.shape
    D = h1_t.shape[0]
    n_tiles = S // tile
    ng = len(POOL_WINDOWS)
    cs = 4 * C // n_blocks
    per = C // cs

    def body(u_ref, uh_ref, d_ref, dh_ref, wp_ref, dq_ref, dk_ref, dv_ref, ht_ref, dproj_ref, dw_ref, dwp_ref):
        i = pl.program_id(0)
        first = i == 0
        _zero_when(first, dw_ref, dwp_ref)
        ht = ht_ref[...]
        for d in range(per, n_blocks):
            src = (dq_ref, dk_ref, dv_ref)[d // per - 1]
            dproj = src[:, (d % per) * cs:(d % per + 1) * cs].astype(BF16)
            dproj_ref[:, d * cs:(d + 1) * cs] = dproj
            dw_ref[d] += _dot(ht, dproj)
        halo = jnp.where(first, 0.0, uh_ref[...])
        parts = _pool_deviation(u_ref[...], halo, i * tile)
        dout = d_ref[...]
        nxt = jnp.where(i == n_tiles - 1, 0.0, dh_ref[...])
        dext = jnp.concatenate([dout, nxt], axis=0).astype(BF16)
        counts = _pool_counts(i * tile, tile + HALO)
        dps, scaled = [], []
        for g in range(ng):
            lanes = slice(g * POOL_GROUP, (g + 1) * POOL_GROUP)
            dp = _dot_nt(dext[:, lanes], wp_ref[g].astype(BF16))
            dps.append(dp[:tile])
            scaled.append(dp / counts[g])
        sums = _window_sums(jnp.concatenate(scaled, axis=1), forward=True)
        du = []
        for g, w in enumerate(POOL_WINDOWS):
            lanes = slice(g * POOL_GROUP, (g + 1) * POOL_GROUP)
            du.append((sums[w][:tile, lanes] - dps[g]).astype(BF16))
            dwp_ref[g] += _dot_tn(parts[g].astype(BF16), dext[:tile, lanes])
        du = jnp.concatenate(du, axis=1)
        for d in range(per):
            dproj = du[:, d * cs:(d + 1) * cs]
            dproj_ref[:, d * cs:(d + 1) * cs] = dproj
            dw_ref[d] += _dot(ht, dproj)

    row = pl.BlockSpec((tile, C), lambda i: (i, 0))
    return pl.pallas_call(
        body, name="bwd_pool_w_in", grid=(n_tiles,),
        in_specs=[row, _prev_halo_spec(tile, C), row, _next_halo_spec(tile, C, n_tiles), _const(w_pool.shape),
                  row, row, row, pl.BlockSpec((D, tile), lambda i: (0, i))],
        out_specs=[pl.BlockSpec((tile, 4 * C), lambda i: (i, 0)),
                   pl.BlockSpec((n_blocks, D, cs), lambda i: (0, 0, 0), pipeline_mode=pl.Buffered(1)),
                   pl.BlockSpec(w_pool.shape, lambda i: (0, 0, 0))],
        out_shape=[jax.ShapeDtypeStruct((S, 4 * C), BF16), jax.ShapeDtypeStruct((n_blocks, D, cs), F32),
                   jax.ShapeDtypeStruct(w_pool.shape, F32)],
        compiler_params=_params("arbitrary"),
    )(u, u, d_pool, d_pool, w_pool, dq, dk, dv, h1_t)


def _bwd_x(dproj, w_in_t, x, dx2, g1, tile, ex):
    S, D = x.shape
    n_tiles = S // tile

    def body(dp_ref, w_ref, x_ref, dx2_ref, g_ref, *rest):
        dx_ref, dg_ref = rest[ex.n:ex.n + 2]
        ex_refs = ex.split(rest[:ex.n] + rest[ex.n + 2:])
        first, last = _grid_ends((n_tiles,))

        @pl.when(first)
        def _():
            ex.start(*ex_refs)
            dg_ref[...] = jnp.zeros_like(dg_ref)

        dh = _dot(dp_ref[...], w_ref[...])
        xf = x_ref[...]
        r1 = _rms(xf)
        n1 = xf * r1
        dg_ref[...] += _colsum(dh * n1)
        dx_ref[...] = dx2_ref[...] + _norm_bwd(dh * g_ref[...], n1, r1)

        @pl.when(last)
        def _():
            ex.wait(*ex_refs)

    row = lambda w: pl.BlockSpec((tile, w), lambda i: (i, 0))
    outs = pl.pallas_call(
        body, name="bwd_x", grid=(n_tiles,),
        in_specs=[row(w_in_t.shape[0]), _const(w_in_t.shape), row(D), row(D), _const((1, D))] + ex.specs,
        out_specs=[row(D), pl.BlockSpec((1, D), lambda i: (0, 0))] + ex.specs,
        out_shape=[jax.ShapeDtypeStruct((S, D), F32), jax.ShapeDtypeStruct((1, D), F32)] + ex.out_shape,
        scratch_shapes=ex.scratch,
        compiler_params=_params("arbitrary"),
    )(dproj, w_in_t, x, dx2, g1, *ex.arrays)
    return outs[0], outs[1], outs[2:]


def _mesh_position():
    x, y, c = lax.axis_index("x"), lax.axis_index("y"), lax.axis_index("c")
    return x, y, c, 4 * x + 2 * y + c


def _peer(x, y, c, k):
    px = 1 - x if k & 4 else x
    py = 1 - y if k & 2 else y
    pc = 1 - c if k & 1 else c
    return (px, py, pc), 4 * px + 2 * py + pc


class _Exchange:
    def __init__(self, arrays, gather):
        self.arrays, self.gather, self.n = list(arrays), gather, len(arrays)
        self.out_shape = [jax.ShapeDtypeStruct(((N_DEV,) + a.shape) if gather else a.shape, a.dtype) for a in arrays]
        self.specs = [pl.BlockSpec(memory_space=pl.ANY)] * self.n
        copies = self.n * (N_DEV - 1)
        self.scratch = [pltpu.SemaphoreType.DMA((copies,)), pltpu.SemaphoreType.DMA((copies,)),
                        pltpu.SemaphoreType.DMA((self.n,))]

    def _copies(self, ins, outs, sems):
        send_sems, recv_sems, local_sems = sems
        x, y, c, me = _mesh_position()
        local, remote = [], []
        for a in range(self.n):
            mine = ins[a] if self.gather else ins[a].at[me]
            local.append(pltpu.make_async_copy(mine, outs[a].at[me], local_sems.at[a]))
            for k in range(1, N_DEV):
                peer, peer_idx = _peer(x, y, c, k)
                src = ins[a] if self.gather else ins[a].at[peer_idx]
                sem = a * (N_DEV - 1) + k - 1
                remote.append(pltpu.make_async_remote_copy(
                    src_ref=src, dst_ref=outs[a].at[me], send_sem=send_sems.at[sem], recv_sem=recv_sems.at[sem],
                    device_id=peer, device_id_type=MESH))
        return local, remote

    def start(self, ins, outs, sems):
        local, remote = self._copies(ins, outs, sems)
        for cp in local + remote:
            cp.start()

    def wait(self, ins, outs, sems):
        local, remote = self._copies(ins, outs, sems)
        for cp in remote:
            cp.wait_send()
        for cp in remote:
            cp.wait_recv()
        for cp in local:
            cp.wait()

    def split(self, refs):
        return refs[:self.n], refs[self.n:2 * self.n], refs[2 * self.n:]


class _ChipGather(_Exchange):
    def __init__(self, arrays):
        super().__init__(arrays, gather=True)

    def _plan(self, ins, outs, sems, waiting):
        send_sems, recv_sems, local_sems = sems
        x, y, c, me = _mesh_position()
        sibling = (x, y, 1 - c)
        chips = [(1 - x, y), (x, 1 - y), (1 - x, 1 - y)]
        local, first, passed, arrivals = [], [], [], []
        for a in range(self.n):
            def copy(k, block, to, src=None, a=a):
                rows = outs[a].at[block]
                return pltpu.make_async_remote_copy(
                    src_ref=rows if src is None else src, dst_ref=rows, send_sem=send_sems.at[a * (N_DEV - 1) + k],
                    recv_sem=recv_sems.at[a * (N_DEV - 1) + k], device_id=to, device_id_type=MESH)

            local.append(pltpu.make_async_copy(ins[a], outs[a].at[me], local_sems.at[a]))
            first.append(copy(0, me, sibling, src=ins[a]))
            first += [copy(1 + j, me, (px, py, c), src=ins[a]) for j, (px, py) in enumerate(chips)]
            if waiting:
                passed.append([copy(4 + j, 4 * px + 2 * py + c, sibling) for j, (px, py) in enumerate(chips)])
                arrivals.append([copy(k, me, sibling) for k in range(N_DEV - 1)])
        return local, first, passed, arrivals

    def start(self, ins, outs, sems):
        local, first, _, _ = self._plan(ins, outs, sems, waiting=False)
        for cp in local + first:
            cp.start()

    def wait(self, ins, outs, sems):
        local, first, passed, arrivals = self._plan(ins, outs, sems, waiting=True)
        for a in range(self.n):
            for j in range(3):
                arrivals[a][1 + j].wait_recv()
                passed[a][j].start()
        for a in range(self.n):
            arrivals[a][0].wait_recv()
            for j in range(3):
                arrivals[a][4 + j].wait_recv()
        for cp in first + [cp for row in passed for cp in row]:
            cp.wait_send()
        for cp in local:
            cp.wait()


def _all_to_all(arrays, gather, name):
    ex = _ChipGather(arrays) if gather else _Exchange(arrays, gather)

    def body(*refs):
        ins, outs, sems = ex.split(refs)
        ex.start(ins, outs, sems)
        ex.wait(ins, outs, sems)

    return pl.pallas_call(body, name=name, in_specs=ex.specs, out_specs=ex.specs, out_shape=ex.out_shape,
                          scratch_shapes=ex.scratch)(*ex.arrays)


def _reduce_adamw(parts, w, m, v, rows):
    R, C = w.shape

    def body(p_ref, w_ref, m_ref, v_ref, g_ref, d_ref, nm_ref, nv_ref):
        g = p_ref[0].astype(F32)
        for s in range(1, N_DEV):
            g = g + p_ref[s].astype(F32)
        g_ref[...] = g
        m_new = ADAM_B1 * m_ref[...] + (1.0 - ADAM_B1) * g
        v_new = ADAM_B2 * v_ref[...] + (1.0 - ADAM_B2) * (g * g)
        m_hat = m_new / (1.0 - ADAM_B1 ** ADAM_STEP)
        v_hat = v_new / (1.0 - ADAM_B2 ** ADAM_STEP)
        d_ref[...] = -ADAM_LR * (m_hat / (jnp.sqrt(v_hat) + ADAM_EPS) + ADAM_WD * w_ref[...])
        nm_ref[...] = m_new
        nv_ref[...] = v_new

    row = pl.BlockSpec((rows, C), lambda i: (i, 0))
    return pl.pallas_call(
        body, name="reduce_adamw", grid=(R // rows,),
        in_specs=[pl.BlockSpec((N_DEV, rows, C), lambda i: (0, i, 0)), row, row, row],
        out_specs=[row] * 4, out_shape=[jax.ShapeDtypeStruct((R, C), F32)] * 4,
        compiler_params=_params("parallel"),
    )(parts, w, m, v)


def _row_tile(rows, cols):
    fits = [t for t in range(8, rows + 1, 8) if rows % t == 0 and N_DEV * t * cols * 4 <= 4 * 1024 * 1024]
    return max(fits) if fits else rows


SMALL_COLS = 1024


def _pack_small(vals):
    rows = []
    for a in vals:
        flat = a.reshape(-1)
        pad = (-flat.shape[0]) % SMALL_COLS
        rows.append(jnp.pad(flat, (0, pad)).reshape(-1, SMALL_COLS))
    packed = jnp.concatenate(rows, axis=0)
    return jnp.pad(packed, ((0, (-packed.shape[0]) % 8), (0, 0)))


def _unpack_small(packed, like):
    out, r = [], 0
    for a in like:
        n = a.size
        nr = -(-n // SMALL_COLS)
        out.append(packed[r:r + nr].reshape(-1)[:n].reshape(a.shape))
        r += nr
    return out


def kernel(x, norm_mix_pre, w_in, w_pool, pool_scale, attn_scale, w_out, norm_mix_post, norm_ffn_pre, w_up, conv_w, conv_b, w_down, norm_ffn_post, loss_target, m_norm_mix_pre, m_w_in, m_w_pool, m_pool_scale, m_attn_scale, m_w_out, m_norm_mix_post, m_norm_ffn_pre, m_w_up, m_conv_w, m_conv_b, m_w_down, m_norm_ffn_post, v_norm_mix_pre, v_w_in, v_w_pool, v_pool_scale, v_attn_scale, v_w_out, v_norm_mix_post, v_norm_ffn_pre, v_w_up, v_conv_w, v_conv_b, v_w_down, v_norm_ffn_post):
    S, D = x.shape[1], x.shape[2]
    d_ff_block = w_up.shape[2]

    xs, target = x[0], loss_target[0]
    g1, g2, g3, g4 = norm_mix_pre, norm_mix_post, norm_ffn_pre, norm_ffn_post
    big = min(512, S)
    small = min(256, S)
    n_pairs = pool_scale.shape[1] // QB
    conv_b_g = conv_b.reshape(N_DEV, 1, d_ff_block)

    (w_in_g,) = _all_to_all([w_in[0].astype(BF16)], gather=True, name="gather_w_in")
    h1_t, u, qkv, pool_out = _fwd_inproj_pool(xs, g1, w_in_g, w_pool[0], big)
    attn_out, (w_out_g, w_up_g, w_down_g, conv_w_g) = _fwd_attn(
        qkv, n_pairs, _ChipGather([w_out[0].astype(BF16), w_up[0].astype(BF16), w_down[0].astype(BF16), conv_w[0]]),
        min(ATTN_FWD_BLOCKS, S // QB))
    w_out_full = w_out_g.reshape(D, D)
    w_down4 = w_down_g.reshape(D_FF_SHARDS, d_ff_block, D)
    mix, x2, h2_t = _fwd_outproj(pool_out, attn_out, pool_scale, attn_scale, w_out_full, xs, g2, g3, big)
    upre, gate_val, dy, df, loss_cols, dg4 = _fwd_ffn_loss(h2_t, w_up_g, conv_w_g, conv_b_g, w_down4, x2, target, g4, small)
    loss_part = (0.5 * jnp.sum(loss_cols) / D).reshape(1)

    dupre_g, dupre_v, d_wd4, d_wup, d_cb, d_cw = _bwd_ffn_blocks(gate_val, upre, conv_w_g, w_down4, df, h2_t, min(1024, S))
    dx2, dmix, dg3, dg2 = _bwd_ffn_tokens(dupre_g, dupre_v, w_up_g, x2, dy, mix, g2, g3, big)
    d_pool, d_attn, d_wout, d_ps, d_as = _bwd_outproj(dmix, w_out_full, pool_out, attn_out, pool_scale, attn_scale, min(1024, S))
    d_wdown_g = d_wd4.reshape(N_DEV, w_down.shape[1], D)
    d_wout_g = d_wout.reshape(N_DEV, D // N_DEV, D)
    dq, dk, dv, late_parts = _bwd_attn(qkv, d_attn, n_pairs, _Exchange([d_wout_g, d_wup, d_wdown_g, d_cw], gather=False),
                                       min(ATTN_BWD_BLOCKS, S // QB))
    dproj, d_win, d_wp = _bwd_pool_w_in(u, d_pool, w_pool[0], dq, dk, dv, h1_t, N_DEV, min(1024, S))
    w_in_t = w_in_g.transpose(0, 2, 1).reshape(-1, D)
    dx, dg1, (win_parts,) = _bwd_x(dproj, w_in_t, xs, dx2, g1, big, _Exchange([d_win], gather=False))
    big_parts = [win_parts] + list(late_parts)
    r = dict(dx=dx, g1=dg1, w_pool=d_wp, pool_scale=d_ps, attn_scale=d_as, g2=dg2, g3=dg3, conv_b=d_cb, g4=dg4)

    small_names = ["norm_mix_pre", "w_pool", "pool_scale", "attn_scale", "norm_mix_post", "norm_ffn_pre", "conv_b", "norm_ffn_post"]
    small_w = dict(norm_mix_pre=norm_mix_pre, w_pool=w_pool, pool_scale=pool_scale, attn_scale=attn_scale,
                   norm_mix_post=norm_mix_post, norm_ffn_pre=norm_ffn_pre, conv_b=conv_b, norm_ffn_post=norm_ffn_post)
    small_m = dict(norm_mix_pre=m_norm_mix_pre, w_pool=m_w_pool, pool_scale=m_pool_scale, attn_scale=m_attn_scale,
                   norm_mix_post=m_norm_mix_post, norm_ffn_pre=m_norm_ffn_pre, conv_b=m_conv_b, norm_ffn_post=m_norm_ffn_post)
    small_v = dict(norm_mix_pre=v_norm_mix_pre, w_pool=v_w_pool, pool_scale=v_pool_scale, attn_scale=v_attn_scale,
                   norm_mix_post=v_norm_mix_post, norm_ffn_pre=v_norm_ffn_pre, conv_b=v_conv_b, norm_ffn_post=v_norm_ffn_post)
    small_g = dict(norm_mix_pre=r["g1"], w_pool=r["w_pool"], pool_scale=r["pool_scale"], attn_scale=r["attn_scale"],
                   norm_mix_post=r["g2"], norm_ffn_pre=r["g3"], conv_b=r["conv_b"], norm_ffn_post=r["g4"])
    nothing = jnp.zeros((1,), F32)
    like = [small_w[n] for n in small_names] + [nothing]
    packed_g = _pack_small([small_g[n] for n in small_names] + [loss_part])

    (small_parts,) = _all_to_all([packed_g], gather=True, name="gather_small_grads")

    def update(parts, w, m, v):
        R, C = w.shape
        return _reduce_adamw(parts, w, m, v, _row_tile(R, C))

    res = {}
    res["w_in"] = update(big_parts[0], w_in[0], m_w_in[0], v_w_in[0])
    res["w_out"] = update(big_parts[1], w_out[0], m_w_out[0], v_w_out[0])
    res["w_up"] = update(big_parts[2], w_up[0], m_w_up[0], v_w_up[0])
    res["w_down"] = update(big_parts[3], w_down[0], m_w_down[0], v_w_down[0])
    res["conv_w"] = update(big_parts[4], conv_w[0], m_conv_w[0], v_conv_w[0])
    small_res = update(small_parts, _pack_small(like), _pack_small([small_m[n] for n in small_names] + [nothing]),
                       _pack_small([small_v[n] for n in small_names] + [nothing]))
    small_res = [_unpack_small(t, like) for t in small_res]
    loss = small_res[0][-1].reshape(())
    for idx, n in enumerate(small_names):
        res[n] = tuple(t[idx] for t in small_res)

    order = ["norm_mix_pre", "w_in", "w_pool", "pool_scale", "attn_scale", "w_out", "norm_mix_post", "norm_ffn_pre",
             "w_up", "conv_w", "conv_b", "w_down", "norm_ffn_post"]
    shaped = {n: tuple(t.reshape(s.shape) for t in res[n])
              for n, s in dict(norm_mix_pre=norm_mix_pre, w_in=w_in, w_pool=w_pool, pool_scale=pool_scale, attn_scale=attn_scale,
                               w_out=w_out, norm_mix_post=norm_mix_post, norm_ffn_pre=norm_ffn_pre, w_up=w_up, conv_w=conv_w,
                               conv_b=conv_b, w_down=w_down, norm_ffn_post=norm_ffn_post).items()}
    outs = [loss, r["dx"].reshape(x.shape)]
    for k in range(4):
        outs += [shaped[n][k] for n in order]
    return tuple(outs)
```

```python
import functools

import jax
import jax.numpy as jnp
from jax import lax
from jax.experimental import pallas as pl
from jax.experimental.pallas import tpu as pltpu

F32 = jnp.float32
BF16 = jnp.bfloat16

N_DEV = 8
EPS = 1e-6
POOL_WINDOWS = (2, 4, 8, 16)
POOL_GROUP = 128
HALO = 16
HEAD_DIM = 64
QB = 128
ATTN_SCALE = HEAD_DIM ** -0.5
ATTN_FWD_BLOCKS = 16
ATTN_BWD_BLOCKS = 8
EXP_UNDERFLOW = -88.0
D_FF_SHARDS = 4

ADAM_LR = 0.001
ADAM_B1 = 0.9
ADAM_B2 = 0.999
ADAM_EPS = 1e-08
ADAM_WD = 0.01
ADAM_STEP = 10

VMEM_LIMIT_V7X = 56 * 1024 * 1024
MESH = pl.DeviceIdType.MESH


def _params(*semantics):
    return pltpu.CompilerParams(dimension_semantics=semantics, vmem_limit_bytes=VMEM_LIMIT_V7X)


def _const(shape):
    zeros = (0,) * len(shape)
    return pl.BlockSpec(shape, lambda *_: zeros, pipeline_mode=pl.Buffered(1))


def _dot(a, b):
    return jnp.dot(a, b, preferred_element_type=F32)


def _dot_nt(a, b):
    return lax.dot_general(a, b, (((1,), (1,)), ((), ())), preferred_element_type=F32)


def _dot_tn(a, b):
    return lax.dot_general(a, b, (((0,), (0,)), ((), ())), preferred_element_type=F32)


def _rms(v):
    return lax.rsqrt(jnp.mean(v * v, axis=-1, keepdims=True) + EPS)


def _norm_bwd(dn_times_gain, n, r):
    return r * (dn_times_gain - n * jnp.mean(dn_times_gain * n, axis=-1, keepdims=True))


def _zero_when(first, *refs):
    @pl.when(first)
    def _():
        for ref in refs:
            ref[...] = jnp.zeros_like(ref)


def _colsum(v):
    return jnp.sum(v, axis=0, keepdims=True)


def _grid_ends(grid):
    ids = [pl.program_id(a) for a in range(len(grid))]
    first = functools.reduce(jnp.logical_and, [i == 0 for i in ids])
    last = functools.reduce(jnp.logical_and, [i == n - 1 for i, n in zip(ids, grid)])
    return first, last


def _fwd_inproj_pool(x, g1, w_in_g, w_pool, tile):
    S, D = x.shape
    nb, _, cs = w_in_g.shape
    d_pool = 2 * cs

    def body(x_ref, g_ref, w_ref, wp_ref, ht_ref, u_ref, qkv_ref, pool_ref, halo_ref):
        i = pl.program_id(0)
        _zero_when(i == 0, halo_ref)
        xf = x_ref[...]
        h = (xf * _rms(xf) * g_ref[...]).astype(BF16)
        ht_ref[...] = h.T
        u = jnp.concatenate([_dot(h, w_ref[0]), _dot(h, w_ref[1])], axis=1)
        u_ref[...] = u
        for d in range(2, nb):
            qkv_ref[:, (d - 2) * cs:(d - 1) * cs] = _dot(h, w_ref[d]).astype(BF16)
        parts = _pool_deviation(u, halo_ref[...], i * tile)
        halo_ref[...] = u[tile - HALO:, :]
        for g, p in enumerate(parts):
            pool_ref[:, g * POOL_GROUP:(g + 1) * POOL_GROUP] = _dot(p.astype(BF16), wp_ref[g].astype(BF16))

    row = lambda w: pl.BlockSpec((tile, w), lambda i: (i, 0))
    return pl.pallas_call(
        body, name="fwd_inproj_pool", grid=(S // tile,),
        in_specs=[row(D), _const((1, D)), _const(w_in_g.shape), _const(w_pool.shape)],
        out_specs=[pl.BlockSpec((D, tile), lambda i: (0, i)), row(d_pool), row(3 * d_pool), row(d_pool)],
        out_shape=[jax.ShapeDtypeStruct((D, S), BF16), jax.ShapeDtypeStruct((S, d_pool), F32),
                   jax.ShapeDtypeStruct((S, 3 * d_pool), BF16), jax.ShapeDtypeStruct((S, d_pool), F32)],
        scratch_shapes=[pltpu.VMEM((HALO, d_pool), F32)],
        compiler_params=_params("arbitrary"),
    )(x, g1, w_in_g, w_pool)


def _window_sums(ext, forward):
    n = ext.shape[0]
    sums, s, sh = {}, ext, 1
    while sh < POOL_WINDOWS[-1]:
        s = s + pltpu.roll(s, (n - sh) if forward else sh, axis=0)
        sh *= 2
        sums[sh] = s
    return sums


def _pool_counts(t0, rows):
    t1 = (lax.broadcasted_iota(jnp.int32, (rows, 1), 0) + t0 + 1).astype(F32)
    return [jnp.minimum(t1, float(w)) for w in POOL_WINDOWS]


def _pool_deviation(u, halo, t0):
    T = u.shape[0]
    sums = _window_sums(jnp.concatenate([halo, u], axis=0), forward=False)
    counts = _pool_counts(t0, T)
    parts = []
    for g, w in enumerate(POOL_WINDOWS):
        lanes = slice(g * POOL_GROUP, (g + 1) * POOL_GROUP)
        parts.append(sums[w][HALO:, lanes] / counts[g] - u[:, lanes])
    return parts


def _prev_halo_spec(tile, width):
    return pl.BlockSpec((HALO, width), lambda i: (jnp.maximum(i * (tile // HALO) - 1, 0), 0))


def _next_halo_spec(tile, width, n_tiles):
    last = n_tiles * (tile // HALO) - 1
    return pl.BlockSpec((HALO, width), lambda i: (jnp.minimum((i + 1) * (tile // HALO), last), 0))


def _low_lanes():
    return lax.broadcasted_iota(jnp.int32, (QB, 2 * HEAD_DIM), 1) < HEAD_DIM


LOG_PIECES = 2
GRAD_PIECES = 2


def _triangle(inclusive, pieces):
    row = lax.broadcasted_iota(jnp.int32, (pieces * QB, QB), 0) % QB
    col = lax.broadcasted_iota(jnp.int32, (pieces * QB, QB), 1)
    return ((row >= col) if inclusive else (row > col)).astype(BF16)


def _pieces(v, n):
    out, rest = [], v
    for _ in range(n - 1):
        piece = rest.astype(BF16)
        out.append(piece)
        rest = rest - piece.astype(F32)
    out.append(rest.astype(BF16))
    return jnp.concatenate(out, axis=1)


def _causal_mask(width, offset):
    row = lax.broadcasted_iota(jnp.int32, (QB, width), 0)
    col = lax.broadcasted_iota(jnp.int32, (QB, width), 1)
    return col < row + offset


def _row_sums(vals, carry):
    for b in reversed(range(vals.shape[1] // QB)):
        carry = carry + jnp.sum(vals[:, b * QB:(b + 1) * QB], axis=1, keepdims=True)
    return carry


def _suffix_sums(vals, tri, carry):
    n = vals.shape[1] // QB
    out, run = [None] * n, carry
    for b in reversed(range(n)):
        blk = vals[:, b * QB:(b + 1) * QB]
        out[b] = _dot(_pieces(blk, tri.shape[0] // QB), tri) + run
        run = run + jnp.sum(blk, axis=1, keepdims=True)
    return (out[0] if n == 1 else jnp.concatenate(out, axis=1)), run


def _attn_tiles(qhs, kws, masks, carries, after_s):
    return _attn_weights(_attn_scores(qhs, kws, masks), masks, carries, after_s)


def _attn_scores(qhs, kws, masks):
    zs = [_dot_nt(qh, kw) * ATTN_SCALE for qh, kw in zip(qhs, kws)]
    es = [jnp.exp(-jnp.abs(z)) for z in zs]
    softplus = [jnp.maximum(z, 0.0) + jnp.log(1.0 + e) for z, e in zip(zs, es)]
    log_1m_beta = [-sp if m is None else jnp.where(m, -sp, 0.0) for sp, m in zip(softplus, masks)]
    return list(zip(zs, es, softplus, log_1m_beta))


def _attn_weights(scores, masks, carries, after_s):
    sums = [_suffix_sums(l, after_s, c) for (_, _, _, l), c in zip(scores, carries)]
    weights = [jnp.exp(z - sp + st) for (z, _, sp, _), (st, _) in zip(scores, sums)]
    weights = [a if m is None else jnp.where(m, a, 0.0) for a, m in zip(weights, masks)]
    return [(z, e, a, c) for (z, e, _, _), a, (_, c) in zip(scores, weights, sums)]


def _split_heads(v, low_lanes):
    return jnp.where(low_lanes, v, 0.0).astype(BF16), jnp.where(low_lanes, 0.0, v).astype(BF16)


def _sweep_done(c0, c1):
    return (jnp.maximum(jnp.max(c0), jnp.max(c1)) < EXP_UNDERFLOW).astype(jnp.int32)


def _all_done(carries):
    return jnp.max(functools.reduce(jnp.maximum, carries)) < EXP_UNDERFLOW


def _first_window(i):
    first_blk = jnp.maximum(i - 1, 0)
    return first_blk, pl.multiple_of(first_blk * QB, QB), (i - first_blk) * QB


def _fwd_attn(qkv, n_pairs, ex, subs):
    S = qkv.shape[0]
    n_steps = S // (subs * QB)

    def body(q_ref, k_ref, v_ref, *rest):
        o_ref = rest[ex.n]
        ex_refs = ex.split(rest[:ex.n] + rest[ex.n + 1:])
        first_step, last_step = _grid_ends((n_pairs, n_steps))

        @pl.when(first_step)
        def _():
            ex.start(*ex_refs)

        low_lanes = _low_lanes()
        after_s = _triangle(False, LOG_PIECES)
        zero = jnp.zeros((QB, 1), F32)

        def cond(c):
            return jnp.logical_and(c[0] >= 0, c[1] == 0)

        qhs, kws, vws, masks, first_blks = [], [], [], [], []
        for sub in range(subs):
            i = pl.program_id(1) * subs + sub
            first_blk, start, offset = _first_window(i)
            first_blks.append(first_blk)
            qhs += _split_heads(q_ref[sub * QB:(sub + 1) * QB, :].astype(F32), low_lanes)
            kws += [k_ref[pl.ds(start, 2 * QB), :]] * 2
            vws += [v_ref[pl.ds(start, 2 * QB), :]] * 2
            masks += [_causal_mask(2 * QB, offset)] * 2
        tiles = _attn_tiles(qhs, kws, masks, [zero] * len(qhs), after_s)
        outs = [_dot(t[2].astype(BF16), vw) for t, vw in zip(tiles, vws)]

        first_out = [jnp.where(low_lanes, outs[2 * sub], outs[2 * sub + 1]) for sub in range(subs)]

        def sweep_on():
            final = []
            for sub in range(subs):
                def step(c, qh=qhs[2 * sub:2 * sub + 2]):
                    j, _, acc, c0, c1 = c
                    at = pl.multiple_of(j * QB, QB)
                    kb = k_ref[pl.ds(at, QB), :]
                    vb = v_ref[pl.ds(at, QB), :]
                    far = _attn_tiles(qh, [kb, kb], [None, None], [c0, c1], after_s)
                    acc = acc + jnp.where(low_lanes, _dot(far[0][2].astype(BF16), vb), _dot(far[1][2].astype(BF16), vb))
                    return j - 1, _sweep_done(far[0][3], far[1][3]), acc, far[0][3], far[1][3]

                c0, c1 = tiles[2 * sub][3], tiles[2 * sub + 1][3]
                final.append(lax.while_loop(cond, step, (first_blks[sub] - 1, _sweep_done(c0, c1), first_out[sub], c0, c1))[2])
            return tuple(final)

        final = lax.cond(_all_done([t[3] for t in tiles]), lambda: tuple(first_out), sweep_on)
        for sub in range(subs):
            o_ref[sub * QB:(sub + 1) * QB, :] = final[sub]

        @pl.when(last_step)
        def _():
            ex.wait(*ex_refs)

    outs = pl.pallas_call(
        body, name="fwd_attn", grid=(n_pairs, n_steps),
        in_specs=[pl.BlockSpec((subs * QB, QB), lambda p, i: (i, p)),
                  pl.BlockSpec((S, QB), lambda p, i: (0, n_pairs + p), pipeline_mode=pl.Buffered(1)),
                  pl.BlockSpec((S, QB), lambda p, i: (0, 2 * n_pairs + p), pipeline_mode=pl.Buffered(1))] + ex.specs,
        out_specs=[pl.BlockSpec((subs * QB, QB), lambda p, i: (i, p))] + ex.specs,
        out_shape=[jax.ShapeDtypeStruct((S, n_pairs * QB), F32)] + ex.out_shape,
        scratch_shapes=ex.scratch,
        compiler_params=_params("arbitrary", "arbitrary"),
    )(qkv, qkv, qkv, *ex.arrays)
    return outs[0], outs[1:]


def _normalized_heads(pool_out, attn_out):
    rp, ra = _rms(pool_out), _rms(attn_out)
    return pool_out * rp, rp, attn_out * ra, ra


def _fwd_outproj(pool_out, attn_out, pool_scale, attn_scale, w_out, x, g2, g3, tile):
    S, D = x.shape
    C = pool_out.shape[1]

    def body(p_ref, a_ref, ps_ref, as_ref, w_ref, x_ref, g2_ref, g3_ref, mix_ref, x2_ref, h2t_ref):
        n_p, _, n_a, _ = _normalized_heads(p_ref[...], a_ref[...])
        mix = _dot((n_p * ps_ref[...]).astype(BF16), w_ref[:C, :]) + _dot((n_a * as_ref[...]).astype(BF16), w_ref[C:, :])
        mix_ref[...] = mix
        x2 = x_ref[...] + mix * _rms(mix) * g2_ref[...]
        x2_ref[...] = x2
        h2t_ref[...] = (x2 * _rms(x2) * g3_ref[...]).astype(BF16).T

    row = lambda w: pl.BlockSpec((tile, w), lambda i: (i, 0))
    return pl.pallas_call(
        body, name="fwd_outproj", grid=(S // tile,),
        in_specs=[row(C), row(C), _const((1, C)), _const((1, C)), _const(w_out.shape), row(D), _const((1, D)), _const((1, D))],
        out_specs=[row(D), row(D), pl.BlockSpec((D, tile), lambda i: (0, i))],
        out_shape=[jax.ShapeDtypeStruct((S, D), F32), jax.ShapeDtypeStruct((S, D), F32), jax.ShapeDtypeStruct((D, S), BF16)],
        compiler_params=_params("parallel"),
    )(pool_out, attn_out, pool_scale, attn_scale, w_out, x, g2, g3)


def _conv_taps(tile_rows, halo_rows):
    T = tile_rows.shape[0]
    ext = jnp.concatenate([halo_rows.astype(F32), tile_rows.astype(F32)], axis=0)
    return pltpu.roll(ext, 2, axis=0)[HALO:], pltpu.roll(ext, 1, axis=0)[HALO:], ext[HALO:]


def _tap_rows(cw_ref, d):
    return [cw_ref[d, k:k + 1, :] for k in range(3)]


def _gated_unit(taps_gate, taps_val, cw_gate, cw_val, cb_gate, cb_val):
    gate = cw_gate[0] * taps_gate[0] + cw_gate[1] * taps_gate[1] + cw_gate[2] * taps_gate[2] + cb_gate
    val = cw_val[0] * taps_val[0] + cw_val[1] * taps_val[1] + cw_val[2] * taps_val[2] + cb_val
    sig = 1.0 / (1.0 + jnp.exp(-gate))
    return gate, val, sig


def _fwd_ffn_loss(h2_t, w_up_g, conv_w_g, conv_b_g, w_down4, x2, target, g4, tile):
    S, D = x2.shape
    nb, _, cs = w_up_g.shape
    half = D_FF_SHARDS

    def body(h_ref, w_ref, cw_ref, cb_ref, wd_ref, x2_ref, t_ref, g4_ref, upre_ref, gv_ref, dy_ref, df_ref, loss_ref, dg4_ref, halo_ref):
        _zero_when(pl.program_id(0) == 0, loss_ref, dg4_ref, halo_ref)
        h = h_ref[...].T

        def up(s):
            return _dot(h, w_ref[s]), _dot(h, w_ref[s + half])

        f = jnp.zeros((tile, D), F32)
        ahead = up(0)
        for s in range(half):
            ug, uv = ahead
            if s + 1 < half:
                ahead = up(s + 1)
            upre_ref[s] = ug.astype(BF16)
            upre_ref[s + half] = uv.astype(BF16)
            gate, val, sig = _gated_unit(_conv_taps(ug, halo_ref[s]), _conv_taps(uv, halo_ref[s + half]),
                                         _tap_rows(cw_ref, s), _tap_rows(cw_ref, s + half), cb_ref[s], cb_ref[s + half])
            halo_ref[s] = ug[tile - HALO:, :]
            halo_ref[s + half] = uv[tile - HALO:, :]
            gv_ref[s] = gate.astype(BF16)
            gv_ref[s + half] = val.astype(BF16)
            f = f + _dot((gate * sig * val).astype(BF16), wd_ref[s])
        r4 = _rms(f)
        n4 = f * r4
        err = x2_ref[...] + n4 * g4_ref[...] - t_ref[...]
        dy = err * (1.0 / D)
        dy_ref[...] = dy
        df_ref[...] = _norm_bwd(dy * g4_ref[...], n4, r4).astype(BF16)
        loss_ref[...] += _colsum(err * err)
        dg4_ref[...] += _colsum(dy * n4)

    row = lambda w: pl.BlockSpec((tile, w), lambda i: (i, 0))
    return pl.pallas_call(
        body, name="fwd_ffn_loss", grid=(S // tile,),
        in_specs=[pl.BlockSpec((D, tile), lambda i: (0, i)), _const(w_up_g.shape), _const(conv_w_g.shape), _const(conv_b_g.shape),
                  _const(w_down4.shape), row(D), row(D), _const((1, D))],
        out_specs=[pl.BlockSpec((nb, tile, cs), lambda i: (0, i, 0)), pl.BlockSpec((nb, tile, cs), lambda i: (0, i, 0)), row(D), row(D),
                   pl.BlockSpec((1, D), lambda i: (0, 0)), pl.BlockSpec((1, D), lambda i: (0, 0))],
        out_shape=[jax.ShapeDtypeStruct((nb, S, cs), BF16), jax.ShapeDtypeStruct((nb, S, cs), BF16),
                   jax.ShapeDtypeStruct((S, D), F32), jax.ShapeDtypeStruct((S, D), BF16),
                   jax.ShapeDtypeStruct((1, D), F32), jax.ShapeDtypeStruct((1, D), F32)],
        scratch_shapes=[pltpu.VMEM((nb, HALO, cs), F32)],
        compiler_params=_params("arbitrary"),
    )(h2_t, w_up_g, conv_w_g, conv_b_g, w_down4, x2, target, g4)


def _bwd_ffn_blocks(gate_val, upre, conv_w_g, w_down4, df, h2_t, tile):
    nb, S, cs = upre.shape
    D = df.shape[1]
    n_tiles = S // tile
    half = D_FF_SHARDS

    def body(g_ref, v_ref, ug_ref, uv_ref, cwg_ref, cwv_ref, wd_ref, df_ref, ht_ref,
             dug_ref, duv_ref, dwd_ref, dwg_ref, dwv_ref, dbg_ref, dbv_ref, dcwg_ref, dcwv_ref, next_ref):
        _zero_when(pl.program_id(1) == 0, dwd_ref, dwg_ref, dwv_ref, dbg_ref, dbv_ref, dcwg_ref, dcwv_ref, next_ref)
        dfb = df_ref[...]
        dact = _dot_nt(dfb, wd_ref[0])
        gate, val = g_ref[0].astype(F32), v_ref[0].astype(F32)
        sig = 1.0 / (1.0 + jnp.exp(-gate))
        silu = gate * sig
        dwd_ref[0] += _dot_tn((silu * val).astype(BF16), dfb)
        ht = ht_ref[...]

        def through_conv(dup, slot, cw_ref, u_ref, du_ref, dw_ref, db_ref, dcw_ref):
            ext = jnp.concatenate([dup, next_ref[slot]], axis=0)
            n = ext.shape[0]
            shifted = (dup, pltpu.roll(ext, n - 1, axis=0)[:tile], pltpu.roll(ext, n - 2, axis=0)[:tile])
            next_ref[slot] = dup[:HALO]
            cw = _tap_rows(cw_ref, 0)
            dupre = (cw[2] * shifted[0] + cw[1] * shifted[1] + cw[0] * shifted[2]).astype(BF16)
            du_ref[0] = dupre
            dw_ref[0] += _dot(ht, dupre)
            u = u_ref[0].astype(F32)
            db_ref[0] += _colsum(dup)
            for k in range(3):
                dcw_ref[0, k:k + 1, :] += _colsum(shifted[2 - k] * u)

        through_conv(dact * (val * (sig + silu * (1.0 - sig))), 0, cwg_ref, ug_ref, dug_ref, dwg_ref, dbg_ref, dcwg_ref)
        through_conv(dact * silu, 1, cwv_ref, uv_ref, duv_ref, dwv_ref, dbv_ref, dcwv_ref)

    rev = lambda i: n_tiles - 1 - i
    blk = lambda off: pl.BlockSpec((1, tile, cs), lambda s, i: (s + off, rev(i), 0))
    par = lambda off, r: pl.BlockSpec((1, r, cs), lambda s, i: (s + off, 0, 0))
    acc = lambda r, c: pl.BlockSpec((1, r, c), lambda s, i: (s, 0, 0), pipeline_mode=pl.Buffered(1))
    outs = pl.pallas_call(
        body, name="bwd_ffn_blocks", grid=(half, n_tiles),
        in_specs=[blk(0), blk(half), blk(0), blk(half), par(0, 3), par(half, 3),
                  acc(cs, D), pl.BlockSpec((tile, D), lambda s, i: (rev(i), 0)), pl.BlockSpec((D, tile), lambda s, i: (0, rev(i)))],
        out_specs=[blk(0), blk(0), acc(cs, D), acc(D, cs), acc(D, cs), acc(1, cs), acc(1, cs), acc(3, cs), acc(3, cs)],
        out_shape=[jax.ShapeDtypeStruct((half, S, cs), BF16), jax.ShapeDtypeStruct((half, S, cs), BF16),
                   jax.ShapeDtypeStruct((half, cs, D), F32),
                   jax.ShapeDtypeStruct((half, D, cs), F32), jax.ShapeDtypeStruct((half, D, cs), F32),
                   jax.ShapeDtypeStruct((half, 1, cs), F32), jax.ShapeDtypeStruct((half, 1, cs), F32),
                   jax.ShapeDtypeStruct((half, 3, cs), F32), jax.ShapeDtypeStruct((half, 3, cs), F32)],
        scratch_shapes=[pltpu.VMEM((2, HALO, cs), F32)],
        compiler_params=_params("arbitrary", "arbitrary"),
    )(gate_val, gate_val, upre, upre, conv_w_g, conv_w_g, w_down4, df, h2_t)
    dupre_g, dupre_v, d_wd, d_wg, d_wv, dbg, dbv, dcwg, dcwv = outs
    return (dupre_g, dupre_v, d_wd, jnp.concatenate([d_wg, d_wv], axis=0), jnp.concatenate([dbg, dbv], axis=0),
            jnp.concatenate([dcwg, dcwv], axis=0))


def _bwd_ffn_tokens(dupre_g, dupre_v, w_up_g, x2, dy, mix, g2, g3, tile):
    half, S, cs = dupre_g.shape
    D = x2.shape[1]

    def body(dg_ref, dv_ref, w_ref, x2_ref, dy_ref, mix_ref, g2_ref, g3_ref, dx2_ref, dmix_ref, dg3_ref, dg2_ref):
        _zero_when(pl.program_id(0) == 0, dg3_ref, dg2_ref)
        parts = [_dot_nt(dg_ref[d], w_ref[d]) for d in range(half)] + [_dot_nt(dv_ref[d], w_ref[d + half]) for d in range(half)]
        while len(parts) > 1:
            parts = [a + b for a, b in zip(parts[::2], parts[1::2])]
        dh2 = parts[0]
        x2 = x2_ref[...]
        r3 = _rms(x2)
        n3 = x2 * r3
        dg3_ref[...] += _colsum(dh2 * n3)
        dx2 = dy_ref[...] + _norm_bwd(dh2 * g3_ref[...], n3, r3)
        dx2_ref[...] = dx2
        mix = mix_ref[...]
        r2 = _rms(mix)
        n2 = mix * r2
        dg2_ref[...] += _colsum(dx2 * n2)
        dmix_ref[...] = _norm_bwd(dx2 * g2_ref[...], n2, r2).astype(BF16)

    row = lambda w: pl.BlockSpec((tile, w), lambda i: (i, 0))
    blk = pl.BlockSpec((half, tile, cs), lambda i: (0, i, 0))
    acc = pl.BlockSpec((1, D), lambda i: (0, 0))
    return pl.pallas_call(
        body, name="bwd_ffn_tokens", grid=(S // tile,),
        in_specs=[blk, blk, _const(w_up_g.shape), row(D), row(D), row(D), _const((1, D)), _const((1, D))],
        out_specs=[row(D), row(D), acc, acc],
        out_shape=[jax.ShapeDtypeStruct((S, D), F32), jax.ShapeDtypeStruct((S, D), BF16),
                   jax.ShapeDtypeStruct((1, D), F32), jax.ShapeDtypeStruct((1, D), F32)],
        compiler_params=_params("arbitrary"),
    )(dupre_g, dupre_v, w_up_g, x2, dy, mix, g2, g3)


def _bwd_outproj(dmix, w_out, pool_out, attn_out, pool_scale, attn_scale, tile):
    S, D = dmix.shape
    C = pool_out.shape[1]

    def body(dm_ref, w_ref, p_ref, a_ref, ps_ref, as_ref, dp_ref, da_ref, dw_ref, dps_ref, das_ref):
        _zero_when(pl.program_id(0) == 0, dw_ref, dps_ref, das_ref)
        dmx = dm_ref[...]
        dmerged = _dot_nt(dmx, w_ref[...])
        n_p, r_p, n_a, r_a = _normalized_heads(p_ref[...], a_ref[...])
        merged = jnp.concatenate([(n_p * ps_ref[...]).astype(BF16), (n_a * as_ref[...]).astype(BF16)], axis=1)
        dw_ref[...] += _dot_tn(merged, dmx)
        dm_p, dm_a = dmerged[:, :C], dmerged[:, C:]
        dps_ref[...] += _colsum(dm_p * n_p)
        das_ref[...] += _colsum(dm_a * n_a)
        dp_ref[...] = _norm_bwd(dm_p * ps_ref[...], n_p, r_p)
        da_ref[...] = _norm_bwd(dm_a * as_ref[...], n_a, r_a)

    row = lambda w: pl.BlockSpec((tile, w), lambda i: (i, 0))
    return pl.pallas_call(
        body, name="bwd_outproj", grid=(S // tile,),
        in_specs=[row(D), _const(w_out.shape), row(C), row(C), _const((1, C)), _const((1, C))],
        out_specs=[row(C), row(C), pl.BlockSpec(w_out.shape, lambda i: (0, 0)),
                   pl.BlockSpec((1, C), lambda i: (0, 0)), pl.BlockSpec((1, C), lambda i: (0, 0))],
        out_shape=[jax.ShapeDtypeStruct((S, C), F32), jax.ShapeDtypeStruct((S, C), F32),
                   jax.ShapeDtypeStruct(w_out.shape, F32), jax.ShapeDtypeStruct((1, C), F32), jax.ShapeDtypeStruct((1, C), F32)],
        compiler_params=_params("arbitrary"),
    )(dmix, w_out, pool_out, attn_out, pool_scale, attn_scale)


def _bwd_attn(qkv, d_attn, n_pairs, ex, subs):
    S = qkv.shape[0]
    n_steps = S // (subs * QB)

    def body(q_ref, k_ref, v_ref, do_ref, *rest):
        dq_ref, dk_ref, dv_ref = rest[ex.n:ex.n + 3]
        ex_refs = ex.split(rest[:ex.n] + rest[ex.n + 3:])
        first_step, last_step = _grid_ends((n_pairs, n_steps))

        @pl.when(first_step)
        def _():
            ex.start(*ex_refs)

        @pl.when(pl.program_id(1) == 0)
        def _():
            dk_ref[...] = jnp.zeros_like(dk_ref)
            dv_ref[...] = jnp.zeros_like(dv_ref)

        low_lanes = _low_lanes()
        after_s, from_s = _triangle(False, LOG_PIECES), _triangle(True, GRAD_PIECES)
        zero = jnp.zeros((QB, 1), F32)

        def tiles(qhs, dohs, totals, kws, vws, masks, cs, gs, scores=None):
            fw = _attn_weights(scores or _attn_scores(qhs, kws, masks), masks, cs, after_s)
            gvals = [t[2] * _dot_nt(doh, vw) for t, doh, vw in zip(fw, dohs, vws)]
            sums = [_suffix_sums(g, from_s, g0) for g, g0 in zip(gvals, gs)]
            totals = [tot if m is None else tot + sm[1] for tot, m, sm in zip(totals, masks, sums)]
            dzs = []
            for (z, e, _, _), g, (nearer, _), tot, m in zip(fw, gvals, sums, totals, masks):
                inv = 1.0 / (1.0 + e)
                sig_abs, sig_neg = inv, e * inv
                pos = z >= 0.0
                dz = g * jnp.where(pos, sig_neg, sig_abs) - jnp.where(pos, sig_abs, sig_neg) * (tot - nearer)
                if m is not None:
                    dz = jnp.where(m, dz, 0.0)
                dzs.append((dz * ATTN_SCALE).astype(BF16))
            dqs = [_dot(dz, kw) for dz, kw in zip(dzs, kws)]
            dks = [_dot_tn(dz, qh) for dz, qh in zip(dzs, qhs)]
            dvs = [_dot_tn(t[2].astype(BF16), doh) for t, doh in zip(fw, dohs)]
            return [(dq, dk, dv, t[3], sm[1], tot) for dq, dk, dv, t, sm, tot in zip(dqs, dks, dvs, fw, sums, totals)]

        def cond(c):
            return jnp.logical_and(c[0] >= 0, c[1] == 0)

        qhs, dohs, kws, vws, masks, first_blks, starts = [], [], [], [], [], [], []
        for sub in range(subs):
            i = pl.program_id(1) * subs + sub
            rows = slice(sub * QB, (sub + 1) * QB)
            first_blk, start, offset = _first_window(i)
            first_blks.append(first_blk)
            starts.append(start)
            qhs += _split_heads(q_ref[rows, :].astype(F32), low_lanes)
            dohs += _split_heads(do_ref[rows, :], low_lanes)
            kws += [k_ref[pl.ds(start, 2 * QB), :]] * 2
            vws += [v_ref[pl.ds(start, 2 * QB), :]] * 2
            masks += [_causal_mask(2 * QB, offset)] * 2
        zeros = [zero] * len(qhs)

        scores = _attn_scores(qhs, kws, masks)
        c_first = [_row_sums(sc[3], zero) for sc in scores]
        all_done = _all_done(c_first)

        def far_totals():
            beyond = []
            for sub in range(subs):
                pair = slice(2 * sub, 2 * sub + 2)

                def far_sums(c, qh=qhs[pair], doh=dohs[pair]):
                    j, _, c0, c1, r0, r1 = c
                    at = pl.multiple_of(j * QB, QB)
                    kb = k_ref[pl.ds(at, QB), :]
                    vb = v_ref[pl.ds(at, QB), :]
                    far = _attn_tiles(qh, [kb, kb], [None, None], [c0, c1], after_s)
                    r0 = r0 + jnp.sum(far[0][2] * _dot_nt(doh[0], vb), axis=1, keepdims=True)
                    r1 = r1 + jnp.sum(far[1][2] * _dot_nt(doh[1], vb), axis=1, keepdims=True)
                    return j - 1, _sweep_done(far[0][3], far[1][3]), far[0][3], far[1][3], r0, r1

                c0, c1 = c_first[pair]
                far = lax.while_loop(cond, far_sums, (first_blks[sub] - 1, _sweep_done(c0, c1), c0, c1, zero, zero))
                beyond += [far[4], far[5]]
            return tuple(beyond)

        beyond_first = list(lax.cond(all_done, lambda: tuple(zeros), far_totals))
        done = tiles(qhs, dohs, beyond_first, kws, vws, masks, zeros, zeros, scores)
        for sub in range(subs):
            dk_ref[pl.ds(starts[sub], 2 * QB), :] += done[2 * sub][1] + done[2 * sub + 1][1]
            dv_ref[pl.ds(starts[sub], 2 * QB), :] += done[2 * sub][2] + done[2 * sub + 1][2]
        first_dq = [jnp.where(low_lanes, done[2 * sub][0], done[2 * sub + 1][0]) for sub in range(subs)]

        def sweep_on():
            final = []
            for sub in range(subs):
                pair = slice(2 * sub, 2 * sub + 2)
                t0, t1 = done[pair]

                def step(c, qh=qhs[pair], doh=dohs[pair], total=[t0[5], t1[5]]):
                    j, _, dq, c0, c1, s0, s1 = c
                    at = pl.multiple_of(j * QB, QB)
                    kb = k_ref[pl.ds(at, QB), :]
                    vb = v_ref[pl.ds(at, QB), :]
                    f0, f1 = tiles(qh, doh, total, [kb, kb], [vb, vb], [None, None], [c0, c1], [s0, s1])
                    dk_ref[pl.ds(at, QB), :] += f0[1] + f1[1]
                    dv_ref[pl.ds(at, QB), :] += f0[2] + f1[2]
                    return j - 1, _sweep_done(f0[3], f1[3]), dq + jnp.where(low_lanes, f0[0], f1[0]), f0[3], f1[3], f0[4], f1[4]

                init = (first_blks[sub] - 1, _sweep_done(t0[3], t1[3]), first_dq[sub], t0[3], t1[3], t0[4], t1[4])
                final.append(lax.while_loop(cond, step, init)[2])
            return tuple(final)

        final = lax.cond(all_done, lambda: tuple(first_dq), sweep_on)
        for sub in range(subs):
            dq_ref[sub * QB:(sub + 1) * QB, :] = final[sub]

        @pl.when(last_step)
        def _():
            ex.wait(*ex_refs)

    blk = pl.BlockSpec((subs * QB, QB), lambda p, i: (i, p))
    full = lambda off: pl.BlockSpec((S, QB), lambda p, i: (0, off + p), pipeline_mode=pl.Buffered(1))
    outs = pl.pallas_call(
        body, name="bwd_attn", grid=(n_pairs, n_steps),
        in_specs=[blk, full(n_pairs), full(2 * n_pairs), blk] + ex.specs,
        out_specs=[blk, full(0), full(0)] + ex.specs,
        out_shape=[jax.ShapeDtypeStruct((S, n_pairs * QB), F32)] * 3 + ex.out_shape,
        scratch_shapes=ex.scratch,
        compiler_params=_params("arbitrary", "arbitrary"),
    )(qkv, qkv, qkv, d_attn, *ex.arrays)
    return outs[0], outs[1], outs[2], outs[3:]


def _bwd_pool_w_in(u, d_pool, w_pool, dq, dk, dv, h1_t, n_blocks, tile):
    S, C = u.shape
    D = h1_t.shape[0]
    n_tiles = S // tile
    ng = len(POOL_WINDOWS)
    cs = 4 * C // n_blocks
    per = C // cs

    def body(u_ref, uh_ref, d_ref, dh_ref, wp_ref, dq_ref, dk_ref, dv_ref, ht_ref, dproj_ref, dw_ref, dwp_ref):
        i = pl.program_id(0)
        first = i == 0
        _zero_when(first, dw_ref, dwp_ref)
        ht = ht_ref[...]
        for d in range(per, n_blocks):
            src = (dq_ref, dk_ref, dv_ref)[d // per - 1]
            dproj = src[:, (d % per) * cs:(d % per + 1) * cs].astype(BF16)
            dproj_ref[:, d * cs:(d + 1) * cs] = dproj
            dw_ref[d] += _dot(ht, dproj)
        halo = jnp.where(first, 0.0, uh_ref[...])
        parts = _pool_deviation(u_ref[...], halo, i * tile)
        dout = d_ref[...]
        nxt = jnp.where(i == n_tiles - 1, 0.0, dh_ref[...])
        dext = jnp.concatenate([dout, nxt], axis=0).astype(BF16)
        counts = _pool_counts(i * tile, tile + HALO)
        dps, scaled = [], []
        for g in range(ng):
            lanes = slice(g * POOL_GROUP, (g + 1) * POOL_GROUP)
            dp = _dot_nt(dext[:, lanes], wp_ref[g].astype(BF16))
            dps.append(dp[:tile])
            scaled.append(dp / counts[g])
        sums = _window_sums(jnp.concatenate(scaled, axis=1), forward=True)
        du = []
        for g, w in enumerate(POOL_WINDOWS):
            lanes = slice(g * POOL_GROUP, (g + 1) * POOL_GROUP)
            du.append((sums[w][:tile, lanes] - dps[g]).astype(BF16))
            dwp_ref[g] += _dot_tn(parts[g].astype(BF16), dext[:tile, lanes])
        du = jnp.concatenate(du, axis=1)
        for d in range(per):
            dproj = du[:, d * cs:(d + 1) * cs]
            dproj_ref[:, d * cs:(d + 1) * cs] = dproj
            dw_ref[d] += _dot(ht, dproj)

    row = pl.BlockSpec((tile, C), lambda i: (i, 0))
    return pl.pallas_call(
        body, name="bwd_pool_w_in", grid=(n_tiles,),
        in_specs=[row, _prev_halo_spec(tile, C), row, _next_halo_spec(tile, C, n_tiles), _const(w_pool.shape),
                  row, row, row, pl.BlockSpec((D, tile), lambda i: (0, i))],
        out_specs=[pl.BlockSpec((tile, 4 * C), lambda i: (i, 0)),
                   pl.BlockSpec((n_blocks, D, cs), lambda i: (0, 0, 0), pipeline_mode=pl.Buffered(1)),
                   pl.BlockSpec(w_pool.shape, lambda i: (0, 0, 0))],
        out_shape=[jax.ShapeDtypeStruct((S, 4 * C), BF16), jax.ShapeDtypeStruct((n_blocks, D, cs), F32),
                   jax.ShapeDtypeStruct(w_pool.shape, F32)],
        compiler_params=_params("arbitrary"),
    )(u, u, d_pool, d_pool, w_pool, dq, dk, dv, h1_t)


def _bwd_x(dproj, w_in_t, x, dx2, g1, tile, ex):
    S, D = x.shape
    n_tiles = S // tile

    def body(dp_ref, w_ref, x_ref, dx2_ref, g_ref, *rest):
        dx_ref, dg_ref = rest[ex.n:ex.n + 2]
        ex_refs = ex.split(rest[:ex.n] + rest[ex.n + 2:])
        first, last = _grid_ends((n_tiles,))

        @pl.when(first)
        def _():
            ex.start(*ex_refs)
            dg_ref[...] = jnp.zeros_like(dg_ref)

        dh = _dot(dp_ref[...], w_ref[...])
        xf = x_ref[...]
        r1 = _rms(xf)
        n1 = xf * r1
        dg_ref[...] += _colsum(dh * n1)
        dx_ref[...] = dx2_ref[...] + _norm_bwd(dh * g_ref[...], n1, r1)

        @pl.when(last)
        def _():
            ex.wait(*ex_refs)

    row = lambda w: pl.BlockSpec((tile, w), lambda i: (i, 0))
    outs = pl.pallas_call(
        body, name="bwd_x", grid=(n_tiles,),
        in_specs=[row(w_in_t.shape[0]), _const(w_in_t.shape), row(D), row(D), _const((1, D))] + ex.specs,
        out_specs=[row(D), pl.BlockSpec((1, D), lambda i: (0, 0))] + ex.specs,
        out_shape=[jax.ShapeDtypeStruct((S, D), F32), jax.ShapeDtypeStruct((1, D), F32)] + ex.out_shape,
        scratch_shapes=ex.scratch,
        compiler_params=_params("arbitrary"),
    )(dproj, w_in_t, x, dx2, g1, *ex.arrays)
    return outs[0], outs[1], outs[2:]


def _mesh_position():
    x, y, c = lax.axis_index("x"), lax.axis_index("y"), lax.axis_index("c")
    return x, y, c, 4 * x + 2 * y + c


def _peer(x, y, c, k):
    px = 1 - x if k & 4 else x
    py = 1 - y if k & 2 else y
    pc = 1 - c if k & 1 else c
    return (px, py, pc), 4 * px + 2 * py + pc


class _Exchange:
    def __init__(self, arrays, gather):
        self.arrays, self.gather, self.n = list(arrays), gather, len(arrays)
        self.out_shape = [jax.ShapeDtypeStruct(((N_DEV,) + a.shape) if gather else a.shape, a.dtype) for a in arrays]
        self.specs = [pl.BlockSpec(memory_space=pl.ANY)] * self.n
        copies = self.n * (N_DEV - 1)
        self.scratch = [pltpu.SemaphoreType.DMA((copies,)), pltpu.SemaphoreType.DMA((copies,)),
                        pltpu.SemaphoreType.DMA((self.n,))]

    def _copies(self, ins, outs, sems):
        send_sems, recv_sems, local_sems = sems
        x, y, c, me = _mesh_position()
        local, remote = [], []
        for a in range(self.n):
            mine = ins[a] if self.gather else ins[a].at[me]
            local.append(pltpu.make_async_copy(mine, outs[a].at[me], local_sems.at[a]))
            for k in range(1, N_DEV):
                peer, peer_idx = _peer(x, y, c, k)
                src = ins[a] if self.gather else ins[a].at[peer_idx]
                sem = a * (N_DEV - 1) + k - 1
                remote.append(pltpu.make_async_remote_copy(
                    src_ref=src, dst_ref=outs[a].at[me], send_sem=send_sems.at[sem], recv_sem=recv_sems.at[sem],
                    device_id=peer, device_id_type=MESH))
        return local, remote

    def start(self, ins, outs, sems):
        local, remote = self._copies(ins, outs, sems)
        for cp in local + remote:
            cp.start()

    def wait(self, ins, outs, sems):
        local, remote = self._copies(ins, outs, sems)
        for cp in remote:
            cp.wait_send()
        for cp in remote:
            cp.wait_recv()
        for cp in local:
            cp.wait()

    def split(self, refs):
        return refs[:self.n], refs[self.n:2 * self.n], refs[2 * self.n:]


class _ChipGather(_Exchange):
    def __init__(self, arrays):
        super().__init__(arrays, gather=True)

    def _plan(self, ins, outs, sems, waiting):
        send_sems, recv_sems, local_sems = sems
        x, y, c, me = _mesh_position()
        sibling = (x, y, 1 - c)
        chips = [(1 - x, y), (x, 1 - y), (1 - x, 1 - y)]
        local, first, passed, arrivals = [], [], [], []
        for a in range(self.n):
            def copy(k, block, to, src=None, a=a):
                rows = outs[a].at[block]
                return pltpu.make_async_remote_copy(
                    src_ref=rows if src is None else src, dst_ref=rows, send_sem=send_sems.at[a * (N_DEV - 1) + k],
                    recv_sem=recv_sems.at[a * (N_DEV - 1) + k], device_id=to, device_id_type=MESH)

            local.append(pltpu.make_async_copy(ins[a], outs[a].at[me], local_sems.at[a]))
            first.append(copy(0, me, sibling, src=ins[a]))
            first += [copy(1 + j, me, (px, py, c), src=ins[a]) for j, (px, py) in enumerate(chips)]
            if waiting:
                passed.append([copy(4 + j, 4 * px + 2 * py + c, sibling) for j, (px, py) in enumerate(chips)])
                arrivals.append([copy(k, me, sibling) for k in range(N_DEV - 1)])
        return local, first, passed, arrivals

    def start(self, ins, outs, sems):
        local, first, _, _ = self._plan(ins, outs, sems, waiting=False)
        for cp in local + first:
            cp.start()

    def wait(self, ins, outs, sems):
        local, first, passed, arrivals = self._plan(ins, outs, sems, waiting=True)
        for a in range(self.n):
            for j in range(3):
                arrivals[a][1 + j].wait_recv()
                passed[a][j].start()
        for a in range(self.n):
            arrivals[a][0].wait_recv()
            for j in range(3):
                arrivals[a][4 + j].wait_recv()
        for cp in first + [cp for row in passed for cp in row]:
            cp.wait_send()
        for cp in local:
            cp.wait()


def _all_to_all(arrays, gather, name):
    ex = _ChipGather(arrays) if gather else _Exchange(arrays, gather)

    def body(*refs):
        ins, outs, sems = ex.split(refs)
        ex.start(ins, outs, sems)
        ex.wait(ins, outs, sems)

    return pl.pallas_call(body, name=name, in_specs=ex.specs, out_specs=ex.specs, out_shape=ex.out_shape,
                          scratch_shapes=ex.scratch)(*ex.arrays)


def _reduce_adamw(parts, w, m, v, rows):
    R, C = w.shape

    def body(p_ref, w_ref, m_ref, v_ref, g_ref, d_ref, nm_ref, nv_ref):
        g = p_ref[0].astype(F32)
        for s in range(1, N_DEV):
            g = g + p_ref[s].astype(F32)
        g_ref[...] = g
        m_new = ADAM_B1 * m_ref[...] + (1.0 - ADAM_B1) * g
        v_new = ADAM_B2 * v_ref[...] + (1.0 - ADAM_B2) * (g * g)
        m_hat = m_new / (1.0 - ADAM_B1 ** ADAM_STEP)
        v_hat = v_new / (1.0 - ADAM_B2 ** ADAM_STEP)
        d_ref[...] = -ADAM_LR * (m_hat / (jnp.sqrt(v_hat) + ADAM_EPS) + ADAM_WD * w_ref[...])
        nm_ref[...] = m_new
        nv_ref[...] = v_new

    row = pl.BlockSpec((rows, C), lambda i: (i, 0))
    return pl.pallas_call(
        body, name="reduce_adamw", grid=(R // rows,),
        in_specs=[pl.BlockSpec((N_DEV, rows, C), lambda i: (0, i, 0)), row, row, row],
        out_specs=[row] * 4, out_shape=[jax.ShapeDtypeStruct((R, C), F32)] * 4,
        compiler_params=_params("parallel"),
    )(parts, w, m, v)


def _row_tile(rows, cols):
    fits = [t for t in range(8, rows + 1, 8) if rows % t == 0 and N_DEV * t * cols * 4 <= 4 * 1024 * 1024]
    return max(fits) if fits else rows


SMALL_COLS = 1024


def _pack_small(vals):
    rows = []
    for a in vals:
        flat = a.reshape(-1)
        pad = (-flat.shape[0]) % SMALL_COLS
        rows.append(jnp.pad(flat, (0, pad)).reshape(-1, SMALL_COLS))
    packed = jnp.concatenate(rows, axis=0)
    return jnp.pad(packed, ((0, (-packed.shape[0]) % 8), (0, 0)))


def _unpack_small(packed, like):
    out, r = [], 0
    for a in like:
        n = a.size
        nr = -(-n // SMALL_COLS)
        out.append(packed[r:r + nr].reshape(-1)[:n].reshape(a.shape))
        r += nr
    return out


def kernel(x, norm_mix_pre, w_in, w_pool, pool_scale, attn_scale, w_out, norm_mix_post, norm_ffn_pre, w_up, conv_w, conv_b, w_down, norm_ffn_post, loss_target, m_norm_mix_pre, m_w_in, m_w_pool, m_pool_scale, m_attn_scale, m_w_out, m_norm_mix_post, m_norm_ffn_pre, m_w_up, m_conv_w, m_conv_b, m_w_down, m_norm_ffn_post, v_norm_mix_pre, v_w_in, v_w_pool, v_pool_scale, v_attn_scale, v_w_out, v_norm_mix_post, v_norm_ffn_pre, v_w_up, v_conv_w, v_conv_b, v_w_down, v_norm_ffn_post):
    S, D = x.shape[1], x.shape[2]
    d_ff_block = w_up.shape[2]

    xs, target = x[0], loss_target[0]
    g1, g2, g3, g4 = norm_mix_pre, norm_mix_post, norm_ffn_pre, norm_ffn_post
    big = min(512, S)
    small = min(256, S)
    n_pairs = pool_scale.shape[1] // QB
    conv_b_g = conv_b.reshape(N_DEV, 1, d_ff_block)

    (w_in_g,) = _all_to_all([w_in[0].astype(BF16)], gather=True, name="gather_w_in")
    h1_t, u, qkv, pool_out = _fwd_inproj_pool(xs, g1, w_in_g, w_pool[0], big)
    attn_out, (w_out_g, w_up_g, w_down_g, conv_w_g) = _fwd_attn(
        qkv, n_pairs, _ChipGather([w_out[0].astype(BF16), w_up[0].astype(BF16), w_down[0].astype(BF16), conv_w[0]]),
        min(ATTN_FWD_BLOCKS, S // QB))
    w_out_full = w_out_g.reshape(D, D)
    w_down4 = w_down_g.reshape(D_FF_SHARDS, d_ff_block, D)
    mix, x2, h2_t = _fwd_outproj(pool_out, attn_out, pool_scale, attn_scale, w_out_full, xs, g2, g3, big)
    upre, gate_val, dy, df, loss_cols, dg4 = _fwd_ffn_loss(h2_t, w_up_g, conv_w_g, conv_b_g, w_down4, x2, target, g4, small)
    loss = lax.psum(0.5 * jnp.sum(loss_cols) / D, ("x", "y", "c"))

    dupre_g, dupre_v, d_wd4, d_wup, d_cb, d_cw = _bwd_ffn_blocks(gate_val, upre, conv_w_g, w_down4, df, h2_t, min(1024, S))
    dx2, dmix, dg3, dg2 = _bwd_ffn_tokens(dupre_g, dupre_v, w_up_g, x2, dy, mix, g2, g3, big)
    d_pool, d_attn, d_wout, d_ps, d_as = _bwd_outproj(dmix, w_out_full, pool_out, attn_out, pool_scale, attn_scale, min(1024, S))
    d_wdown_g = d_wd4.reshape(N_DEV, w_down.shape[1], D)
    d_wout_g = d_wout.reshape(N_DEV, D // N_DEV, D)
    dq, dk, dv, late_parts = _bwd_attn(qkv, d_attn, n_pairs, _Exchange([d_wout_g, d_wup, d_wdown_g, d_cw], gather=False),
                                       min(ATTN_BWD_BLOCKS, S // QB))
    dproj, d_win, d_wp = _bwd_pool_w_in(u, d_pool, w_pool[0], dq, dk, dv, h1_t, N_DEV, min(1024, S))
    w_in_t = w_in_g.transpose(0, 2, 1).reshape(-1, D)
    dx, dg1, (win_parts,) = _bwd_x(dproj, w_in_t, xs, dx2, g1, big, _Exchange([d_win.astype(BF16)], gather=False))
    big_parts = [win_parts] + list(late_parts)
    r = dict(dx=dx, g1=dg1, w_pool=d_wp, pool_scale=d_ps, attn_scale=d_as, g2=dg2, g3=dg3, conv_b=d_cb, g4=dg4)

    small_names = ["norm_mix_pre", "w_pool", "pool_scale", "attn_scale", "norm_mix_post", "norm_ffn_pre", "conv_b", "norm_ffn_post"]
    small_w = dict(norm_mix_pre=norm_mix_pre, w_pool=w_pool, pool_scale=pool_scale, attn_scale=attn_scale,
                   norm_mix_post=norm_mix_post, norm_ffn_pre=norm_ffn_pre, conv_b=conv_b, norm_ffn_post=norm_ffn_post)
    small_m = dict(norm_mix_pre=m_norm_mix_pre, w_pool=m_w_pool, pool_scale=m_pool_scale, attn_scale=m_attn_scale,
                   norm_mix_post=m_norm_mix_post, norm_ffn_pre=m_norm_ffn_pre, conv_b=m_conv_b, norm_ffn_post=m_norm_ffn_post)
    small_v = dict(norm_mix_pre=v_norm_mix_pre, w_pool=v_w_pool, pool_scale=v_pool_scale, attn_scale=v_attn_scale,
                   norm_mix_post=v_norm_mix_post, norm_ffn_pre=v_norm_ffn_pre, conv_b=v_conv_b, norm_ffn_post=v_norm_ffn_post)
    small_g = dict(norm_mix_pre=r["g1"], w_pool=r["w_pool"], pool_scale=r["pool_scale"], attn_scale=r["attn_scale"],
                   norm_mix_post=r["g2"], norm_ffn_pre=r["g3"], conv_b=r["conv_b"], norm_ffn_post=r["g4"])
    like = [small_w[n] for n in small_names]
    packed_g = _pack_small([small_g[n] for n in small_names])

    (small_parts,) = _all_to_all([packed_g], gather=True, name="gather_small_grads")

    def update(parts, w, m, v):
        R, C = w.shape
        return _reduce_adamw(parts, w, m, v, _row_tile(R, C))

    res = {}
    res["w_in"] = update(big_parts[0], w_in[0], m_w_in[0], v_w_in[0])
    res["w_out"] = update(big_parts[1], w_out[0], m_w_out[0], v_w_out[0])
    res["w_up"] = update(big_parts[2], w_up[0], m_w_up[0], v_w_up[0])
    res["w_down"] = update(big_parts[3], w_down[0], m_w_down[0], v_w_down[0])
    res["conv_w"] = update(big_parts[4], conv_w[0], m_conv_w[0], v_conv_w[0])
    small_res = update(small_parts, _pack_small(like), _pack_small([small_m[n] for n in small_names]),
                       _pack_small([small_v[n] for n in small_names]))
    small_res = [_unpack_small(t, like) for t in small_res]
    for idx, n in enumerate(small_names):
        res[n] = tuple(t[idx] for t in small_res)

    order = ["norm_mix_pre", "w_in", "w_pool", "pool_scale", "attn_scale", "w_out", "norm_mix_post", "norm_ffn_pre",
             "w_up", "conv_w", "conv_b", "w_down", "norm_ffn_post"]
    shaped = {n: tuple(t.reshape(s.shape) for t in res[n])
              for n, s in dict(norm_mix_pre=norm_mix_pre, w_in=w_in, w_pool=w_pool, pool_scale=pool_scale, attn_scale=attn_scale,
                               w_out=w_out, norm_mix_post=norm_mix_post, norm_ffn_pre=norm_ffn_pre, w_up=w_up, conv_w=conv_w,
                               conv_b=conv_b, w_down=w_down, norm_ffn_post=norm_ffn_post).items()}
    outs = [loss, r["dx"].reshape(x.shape)]
    for k in range(4):
        outs += [shaped[n][k] for n in order]
    return tuple(outs)
```
